```python
import jax, jax.numpy as jnp
from jax import lax
import numpy as np

D_MODEL = 1024
BATCH = 8
SEQ = 8192
DEPTH = 2

N_META = 16
CHUNK = 64
META_PAD = CHUNK - N_META
D_MIX = D_MODEL
LRU_WIDTH = D_MIX // 2
LRU_HEADS = 8
LRU_HEAD_DIM = LRU_WIDTH // LRU_HEADS
LRU_C = 8.0
CONV_WIDTH = 4
CONV_LEFT = 2
CONV_RIGHT = CONV_WIDTH - 1 - CONV_LEFT
GLA_WIDTH = D_MIX - LRU_WIDTH
GLA_HEADS = 4
GLA_DV = GLA_WIDTH // GLA_HEADS
GLA_DK = GLA_DV // 2
GLA_RANK = 16
GLA_GATE_NORM = 16.0
D_FF = 4 * D_MODEL
EPS = 1e-6

IN_WIDTHS = [LRU_WIDTH, LRU_WIDTH,
             GLA_HEADS * GLA_DK, GLA_HEADS * GLA_DK,
             GLA_WIDTH, GLA_WIDTH,
             GLA_RANK, GLA_RANK]
D_IN = int(sum(IN_WIDTHS))
IN_SPLITS = [int(s) for s in np.cumsum(IN_WIDTHS)[:-1]]

kernel_name = "bidir_hymba_rglru_gla_block"


def rmsnorm(x, g):
    xf = x.astype(jnp.float32)
    y = xf * lax.rsqrt(jnp.mean(xf * xf, axis=-1, keepdims=True) + EPS)
    return y.astype(x.dtype) * g


def centred_dwconv(x, w, b):
    y = lax.conv_general_dilated(x, w[:, None, :], window_strides=(1,),
                                 padding=[(CONV_LEFT, CONV_RIGHT)],
                                 dimension_numbers=('NWC', 'WIO', 'NWC'),
                                 feature_group_count=x.shape[-1])
    return y + b


def _lin_combine(c1, c2):
    a1, b1 = c1
    a2, b2 = c2
    return a1 * a2, a2 * b1 + b2


def rglru_direction(xc, wa, ba, wx, bx, lam, reverse):
    B, T, _ = xc.shape
    xh = xc.reshape(B, T, LRU_HEADS, LRU_HEAD_DIM)
    r = jax.nn.sigmoid(jnp.einsum('bthi,hij->bthj', xh, wa).reshape(B, T, LRU_WIDTH) + ba)
    i = jax.nn.sigmoid(jnp.einsum('bthi,hij->bthj', xh, wx).reshape(B, T, LRU_WIDTH) + bx)
    log_a = -LRU_C * r * jax.nn.softplus(-lam)
    a = jnp.exp(log_a)
    u = jnp.sqrt(-jnp.expm1(2.0 * log_a)) * (i * xc)
    _, h = lax.associative_scan(_lin_combine, (a, u), axis=1, reverse=reverse)
    return h


def rglru_group(x_br, gate_br, conv_w, conv_b,
                wa_f, ba_f, wx_f, bx_f, lam_f, wa_b, ba_b, wx_b, bx_b, lam_b):
    xc = centred_dwconv(x_br, conv_w, conv_b).astype(jnp.float32)
    h = (rglru_direction(xc, wa_f, ba_f, wx_f, bx_f, lam_f, reverse=False)
         + rglru_direction(xc, wa_b, ba_b, wx_b, bx_b, lam_b, reverse=True))
    return h.astype(x_br.dtype) * jax.nn.gelu(gate_br)


def insert_pad(t):
    z = jnp.zeros((t.shape[0], META_PAD, t.shape[2]), t.dtype)
    return jnp.concatenate([t[:, :N_META], z, t[:, N_META:]], axis=1)


def remove_pad(t):
    return jnp.concatenate([t[:, :N_META], t[:, CHUNK:]], axis=1)


def to_chunks(t, d):
    B, Tp = t.shape[0], t.shape[1]
    return t.reshape(B, Tp // CHUNK, CHUNK, GLA_HEADS, d).transpose(1, 0, 3, 2, 4)


def from_chunks(o):
    N, B = o.shape[0], o.shape[1]
    return o.transpose(1, 0, 3, 2, 4).reshape(B, N * CHUNK, GLA_HEADS * GLA_DV)


def gla_chunk_scan(q, k, v, g, strict):
    B, H = q.shape[1], q.shape[2]
    b = jnp.cumsum(g, axis=3)
    mask = jnp.tril(jnp.ones((CHUNK, CHUNK), dtype=bool), k=-1 if strict else 0)

    def step(S, inp):
        qc, kc, vc, bc = inp
        diff = bc[:, :, :, None, :] - bc[:, :, None, :, :]
        decay = jnp.where(mask[:, :, None], jnp.exp(jnp.minimum(diff, 0.0)), 0.0)
        scores = jnp.einsum('bhtc,bhtsc,bhsc->bhts', qc, decay, kc)
        intra = jnp.einsum('bhts,bhsv->bhtv', scores, vc)
        inter = jnp.einsum('bhtc,bhcv->bhtv', qc * jnp.exp(bc), S)
        b_last = bc[:, :, -1:, :]
        S_new = (jnp.exp(b_last)[:, :, 0, :, None] * S
                 + jnp.einsum('bhsc,bhsv->bhcv', kc * jnp.exp(b_last - bc), vc))
        return S_new, intra + inter

    S0 = jnp.zeros((B, H, GLA_DK, GLA_DV), jnp.float32)
    _, o = lax.scan(step, S0, (q, k, v, b))
    return o


def gla_bidirectional(q, k, v, g_f, g_b):
    qp, kp, vp, gfp, gbp = (insert_pad(t) for t in (q, k, v, g_f, g_b))
    o_f = gla_chunk_scan(to_chunks(qp, GLA_DK), to_chunks(kp, GLA_DK),
                         to_chunks(vp, GLA_DV), to_chunks(gfp, GLA_DK), strict=False)
    flip = lambda t: t[:, ::-1]
    o_b = gla_chunk_scan(to_chunks(flip(qp), GLA_DK), to_chunks(flip(kp), GLA_DK),
                         to_chunks(flip(vp), GLA_DV), to_chunks(flip(gbp), GLA_DK), strict=True)
    o = from_chunks(o_f) + flip(from_chunks(o_b))
    return remove_pad(o)


def gla_group(q, k, v, g_out, zg_f, zg_b, wg_f, bg_f, wg_b, bg_b, head_norm):
    B, T = q.shape[0], q.shape[1]
    in_dtype = q.dtype
    q = q.astype(jnp.float32) * (GLA_DK ** -0.5)
    g_f = jax.nn.log_sigmoid((zg_f @ wg_f + bg_f).astype(jnp.float32)) / GLA_GATE_NORM
    g_b = jax.nn.log_sigmoid((zg_b @ wg_b + bg_b).astype(jnp.float32)) / GLA_GATE_NORM
    o = gla_bidirectional(q, k.astype(jnp.float32), v.astype(jnp.float32), g_f, g_b)
    o = o.reshape(B, T, GLA_HEADS, GLA_DV)
    o = o * lax.rsqrt(jnp.mean(o * o, axis=-1, keepdims=True) + EPS)
    o = o.reshape(B, T, GLA_WIDTH).astype(in_dtype) * head_norm
    return o * jax.nn.silu(g_out)


def hybrid_mixer(h, w_in, conv_w, conv_b,
                 lru_wa_f, lru_ba_f, lru_wx_f, lru_bx_f, lru_lambda_f,
                 lru_wa_b, lru_ba_b, lru_wx_b, lru_bx_b, lru_lambda_b,
                 gla_wg_f, gla_bg_f, gla_wg_b, gla_bg_b, gla_head_norm, w_out):
    z = h @ w_in
    x_br, gate_br, q, k, v, g_out, zg_f, zg_b = jnp.split(z, IN_SPLITS, axis=-1)
    y_lru = rglru_group(x_br, gate_br, conv_w, conv_b,
                        lru_wa_f, lru_ba_f, lru_wx_f, lru_bx_f, lru_lambda_f,
                        lru_wa_b, lru_ba_b, lru_wx_b, lru_bx_b, lru_lambda_b)
    y_gla = gla_group(q, k, v, g_out, zg_f, zg_b,
                      gla_wg_f, gla_bg_f, gla_wg_b, gla_bg_b, gla_head_norm)
    return jnp.concatenate([y_lru, y_gla], axis=-1) @ w_out


def squared_relu_mlp(h, w_up, w_down):
    return jnp.square(jax.nn.relu(h @ w_up)) @ w_down


def _fwd_setup_inputs(seed: int = 0) -> dict:
    key = jax.random.key(seed)
    ks = iter(jax.random.split(key, 40))
    nrm = lambda shape, scale: jax.random.normal(next(ks), shape, jnp.float32) * scale
    gain = lambda shape: 1.0 + nrm(shape, 0.05)

    def lam(shape):
        u = jax.random.uniform(next(ks), shape, jnp.float32, 0.9, 0.999)
        s = u ** (1.0 / LRU_C)
        return jnp.log(s) - jnp.log1p(-s)

    L = DEPTH
    return {
        "x": nrm((BATCH, SEQ, D_MODEL), 1.0),
        "meta_tokens": nrm((N_META, D_MODEL), 1.0),
        "norm_mix_pre": gain((L, D_MODEL)),
        "norm_mix_post": gain((L, D_MODEL)),
        "norm_mlp_pre": gain((L, D_MODEL)),
        "norm_mlp_post": gain((L, D_MODEL)),
        "w_in": nrm((L, D_MODEL, D_IN), D_MODEL ** -0.5),
        "conv_w": nrm((L, CONV_WIDTH, LRU_WIDTH), CONV_WIDTH ** -0.5),
        "conv_b": nrm((L, LRU_WIDTH), 0.01),
        "lru_wa_f": nrm((L, LRU_HEADS, LRU_HEAD_DIM, LRU_HEAD_DIM), LRU_HEAD_DIM ** -0.5),
        "lru_ba_f": nrm((L, LRU_WIDTH), 0.01),
        "lru_wx_f": nrm((L, LRU_HEADS, LRU_HEAD_DIM, LRU_HEAD_DIM), LRU_HEAD_DIM ** -0.5),
        "lru_bx_f": nrm((L, LRU_WIDTH), 0.01),
        "lru_lambda_f": lam((L, LRU_WIDTH)),
        "lru_wa_b": nrm((L, LRU_HEADS, LRU_HEAD_DIM, LRU_HEAD_DIM), LRU_HEAD_DIM ** -0.5),
        "lru_ba_b": nrm((L, LRU_WIDTH), 0.01),
        "lru_wx_b": nrm((L, LRU_HEADS, LRU_HEAD_DIM, LRU_HEAD_DIM), LRU_HEAD_DIM ** -0.5),
        "lru_bx_b": nrm((L, LRU_WIDTH), 0.01),
        "lru_lambda_b": lam((L, LRU_WIDTH)),
        "gla_wg_f": nrm((L, GLA_RANK, GLA_HEADS * GLA_DK), GLA_RANK ** -0.5),
        "gla_bg_f": nrm((L, GLA_HEADS * GLA_DK), 0.1),
        "gla_wg_b": nrm((L, GLA_RANK, GLA_HEADS * GLA_DK), GLA_RANK ** -0.5),
        "gla_bg_b": nrm((L, GLA_HEADS * GLA_DK), 0.1),
        "gla_head_norm": gain((L, GLA_WIDTH)),
        "w_out": nrm((L, D_MIX, D_MODEL), D_MIX ** -0.5),
        "w_mlp_up": nrm((L, D_MODEL, D_FF), D_MODEL ** -0.5),
        "w_mlp_down": nrm((L, D_FF, D_MODEL), D_FF ** -0.5),
    }


def _fwd_reference(x, meta_tokens, norm_mix_pre, norm_mix_post, norm_mlp_pre, norm_mlp_post,
              w_in, conv_w, conv_b,
              lru_wa_f, lru_ba_f, lru_wx_f, lru_bx_f, lru_lambda_f,
              lru_wa_b, lru_ba_b, lru_wx_b, lru_bx_b, lru_lambda_b,
              gla_wg_f, gla_bg_f, gla_wg_b, gla_bg_b, gla_head_norm,
              w_out, w_mlp_up, w_mlp_down):
    B = x.shape[0]
    meta = jnp.broadcast_to(meta_tokens.astype(x.dtype)[None], (B, N_META, D_MODEL))
    h = jnp.concatenate([meta, x], axis=1)
    for l in range(DEPTH):
        mix = hybrid_mixer(rmsnorm(h, norm_mix_pre[l]), w_in[l], conv_w[l], conv_b[l],
                           lru_wa_f[l], lru_ba_f[l], lru_wx_f[l], lru_bx_f[l], lru_lambda_f[l],
                           lru_wa_b[l], lru_ba_b[l], lru_wx_b[l], lru_bx_b[l], lru_lambda_b[l],
                           gla_wg_f[l], gla_bg_f[l], gla_wg_b[l], gla_bg_b[l], gla_head_norm[l],
                           w_out[l])
        h = h + rmsnorm(mix, norm_mix_post[l])
        ff = squared_relu_mlp(rmsnorm(h, norm_mlp_pre[l]), w_mlp_up[l], w_mlp_down[l])
        h = h + rmsnorm(ff, norm_mlp_post[l])
    return h[:, N_META:]


import jax as _jax
import jax.numpy as _jnp

TWIN_FORMAT = 'train_step'
FWD_PARAMS = ['x', 'meta_tokens', 'norm_mix_pre', 'norm_mix_post', 'norm_mlp_pre', 'norm_mlp_post', 'w_in', 'conv_w', 'conv_b', 'lru_wa_f', 'lru_ba_f', 'lru_wx_f', 'lru_bx_f', 'lru_lambda_f', 'lru_wa_b', 'lru_ba_b', 'lru_wx_b', 'lru_bx_b', 'lru_lambda_b', 'gla_wg_f', 'gla_bg_f', 'gla_wg_b', 'gla_bg_b', 'gla_head_norm', 'w_out', 'w_mlp_up', 'w_mlp_down']
TWIN_WEIGHTS = ['meta_tokens', 'norm_mix_pre', 'norm_mix_post', 'norm_mlp_pre', 'norm_mlp_post', 'w_in', 'conv_w', 'conv_b', 'lru_wa_f', 'lru_ba_f', 'lru_wx_f', 'lru_bx_f', 'lru_lambda_f', 'lru_wa_b', 'lru_ba_b', 'lru_wx_b', 'lru_bx_b', 'lru_lambda_b', 'gla_wg_f', 'gla_bg_f', 'gla_wg_b', 'gla_bg_b', 'gla_head_norm', 'w_out', 'w_mlp_up', 'w_mlp_down']
TWIN_DIFF_INPUT = 'x'
TWIN_INPUTS = ['x', 'meta_tokens', 'norm_mix_pre', 'norm_mix_post', 'norm_mlp_pre', 'norm_mlp_post', 'w_in', 'conv_w', 'conv_b', 'lru_wa_f', 'lru_ba_f', 'lru_wx_f', 'lru_bx_f', 'lru_lambda_f', 'lru_wa_b', 'lru_ba_b', 'lru_wx_b', 'lru_bx_b', 'lru_lambda_b', 'gla_wg_f', 'gla_bg_f', 'gla_wg_b', 'gla_bg_b', 'gla_head_norm', 'w_out', 'w_mlp_up', 'w_mlp_down', 'loss_target', 'm_meta_tokens', 'm_norm_mix_pre', 'm_norm_mix_post', 'm_norm_mlp_pre', 'm_norm_mlp_post', 'm_w_in', 'm_conv_w', 'm_conv_b', 'm_lru_wa_f', 'm_lru_ba_f', 'm_lru_wx_f', 'm_lru_bx_f', 'm_lru_lambda_f', 'm_lru_wa_b', 'm_lru_ba_b', 'm_lru_wx_b', 'm_lru_bx_b', 'm_lru_lambda_b', 'm_gla_wg_f', 'm_gla_bg_f', 'm_gla_wg_b', 'm_gla_bg_b', 'm_gla_head_norm', 'm_w_out', 'm_w_mlp_up', 'm_w_mlp_down', 'v_meta_tokens', 'v_norm_mix_pre', 'v_norm_mix_post', 'v_norm_mlp_pre', 'v_norm_mlp_post', 'v_w_in', 'v_conv_w', 'v_conv_b', 'v_lru_wa_f', 'v_lru_ba_f', 'v_lru_wx_f', 'v_lru_bx_f', 'v_lru_lambda_f', 'v_lru_wa_b', 'v_lru_ba_b', 'v_lru_wx_b', 'v_lru_bx_b', 'v_lru_lambda_b', 'v_gla_wg_f', 'v_gla_bg_f', 'v_gla_wg_b', 'v_gla_bg_b', 'v_gla_head_norm', 'v_w_out', 'v_w_mlp_up', 'v_w_mlp_down']
TWIN_OUTPUTS = ['loss', 'grad_x', 'grad_meta_tokens', 'grad_norm_mix_pre', 'grad_norm_mix_post', 'grad_norm_mlp_pre', 'grad_norm_mlp_post', 'grad_w_in', 'grad_conv_w', 'grad_conv_b', 'grad_lru_wa_f', 'grad_lru_ba_f', 'grad_lru_wx_f', 'grad_lru_bx_f', 'grad_lru_lambda_f', 'grad_lru_wa_b', 'grad_lru_ba_b', 'grad_lru_wx_b', 'grad_lru_bx_b', 'grad_lru_lambda_b', 'grad_gla_wg_f', 'grad_gla_bg_f', 'grad_gla_wg_b', 'grad_gla_bg_b', 'grad_gla_head_norm', 'grad_w_out', 'grad_w_mlp_up', 'grad_w_mlp_down', 'delta_meta_tokens', 'delta_norm_mix_pre', 'delta_norm_mix_post', 'delta_norm_mlp_pre', 'delta_norm_mlp_post', 'delta_w_in', 'delta_conv_w', 'delta_conv_b', 'delta_lru_wa_f', 'delta_lru_ba_f', 'delta_lru_wx_f', 'delta_lru_bx_f', 'delta_lru_lambda_f', 'delta_lru_wa_b', 'delta_lru_ba_b', 'delta_lru_wx_b', 'delta_lru_bx_b', 'delta_lru_lambda_b', 'delta_gla_wg_f', 'delta_gla_bg_f', 'delta_gla_wg_b', 'delta_gla_bg_b', 'delta_gla_head_norm', 'delta_w_out', 'delta_w_mlp_up', 'delta_w_mlp_down', 'new_m_meta_tokens', 'new_m_norm_mix_pre', 'new_m_norm_mix_post', 'new_m_norm_mlp_pre', 'new_m_norm_mlp_post', 'new_m_w_in', 'new_m_conv_w', 'new_m_conv_b', 'new_m_lru_wa_f', 'new_m_lru_ba_f', 'new_m_lru_wx_f', 'new_m_lru_bx_f', 'new_m_lru_lambda_f', 'new_m_lru_wa_b', 'new_m_lru_ba_b', 'new_m_lru_wx_b', 'new_m_lru_bx_b', 'new_m_lru_lambda_b', 'new_m_gla_wg_f', 'new_m_gla_bg_f', 'new_m_gla_wg_b', 'new_m_gla_bg_b', 'new_m_gla_head_norm', 'new_m_w_out', 'new_m_w_mlp_up', 'new_m_w_mlp_down', 'new_v_meta_tokens', 'new_v_norm_mix_pre', 'new_v_norm_mix_post', 'new_v_norm_mlp_pre', 'new_v_norm_mlp_post', 'new_v_w_in', 'new_v_conv_w', 'new_v_conv_b', 'new_v_lru_wa_f', 'new_v_lru_ba_f', 'new_v_lru_wx_f', 'new_v_lru_bx_f', 'new_v_lru_lambda_f', 'new_v_lru_wa_b', 'new_v_lru_ba_b', 'new_v_lru_wx_b', 'new_v_lru_bx_b', 'new_v_lru_lambda_b', 'new_v_gla_wg_f', 'new_v_gla_bg_f', 'new_v_gla_wg_b', 'new_v_gla_bg_b', 'new_v_gla_head_norm', 'new_v_w_out', 'new_v_w_mlp_up', 'new_v_w_mlp_down']
TWIN_LEAF_KINDS = {'loss': 'loss', 'grad_x': 'grad_x', 'grad_meta_tokens': 'grad_w', 'grad_norm_mix_pre': 'grad_w', 'grad_norm_mix_post': 'grad_w', 'grad_norm_mlp_pre': 'grad_w', 'grad_norm_mlp_post': 'grad_w', 'grad_w_in': 'grad_w', 'grad_conv_w': 'grad_w', 'grad_conv_b': 'grad_w', 'grad_lru_wa_f': 'grad_w', 'grad_lru_ba_f': 'grad_w', 'grad_lru_wx_f': 'grad_w', 'grad_lru_bx_f': 'grad_w', 'grad_lru_lambda_f': 'grad_w', 'grad_lru_wa_b': 'grad_w', 'grad_lru_ba_b': 'grad_w', 'grad_lru_wx_b': 'grad_w', 'grad_lru_bx_b': 'grad_w', 'grad_lru_lambda_b': 'grad_w', 'grad_gla_wg_f': 'grad_w', 'grad_gla_bg_f': 'grad_w', 'grad_gla_wg_b': 'grad_w', 'grad_gla_bg_b': 'grad_w', 'grad_gla_head_norm': 'grad_w', 'grad_w_out': 'grad_w', 'grad_w_mlp_up': 'grad_w', 'grad_w_mlp_down': 'grad_w', 'delta_meta_tokens': 'delta_w', 'delta_norm_mix_pre': 'delta_w', 'delta_norm_mix_post': 'delta_w', 'delta_norm_mlp_pre': 'delta_w', 'delta_norm_mlp_post': 'delta_w', 'delta_w_in': 'delta_w', 'delta_conv_w': 'delta_w', 'delta_conv_b': 'delta_w', 'delta_lru_wa_f': 'delta_w', 'delta_lru_ba_f': 'delta_w', 'delta_lru_wx_f': 'delta_w', 'delta_lru_bx_f': 'delta_w', 'delta_lru_lambda_f': 'delta_w', 'delta_lru_wa_b': 'delta_w', 'delta_lru_ba_b': 'delta_w', 'delta_lru_wx_b': 'delta_w', 'delta_lru_bx_b': 'delta_w', 'delta_lru_lambda_b': 'delta_w', 'delta_gla_wg_f': 'delta_w', 'delta_gla_bg_f': 'delta_w', 'delta_gla_wg_b': 'delta_w', 'delta_gla_bg_b': 'delta_w', 'delta_gla_head_norm': 'delta_w', 'delta_w_out': 'delta_w', 'delta_w_mlp_up': 'delta_w', 'delta_w_mlp_down': 'delta_w', 'new_m_meta_tokens': 'new_m', 'new_m_norm_mix_pre': 'new_m', 'new_m_norm_mix_post': 'new_m', 'new_m_norm_mlp_pre': 'new_m', 'new_m_norm_mlp_post': 'new_m', 'new_m_w_in': 'new_m', 'new_m_conv_w': 'new_m', 'new_m_conv_b': 'new_m', 'new_m_lru_wa_f': 'new_m', 'new_m_lru_ba_f': 'new_m', 'new_m_lru_wx_f': 'new_m', 'new_m_lru_bx_f': 'new_m', 'new_m_lru_lambda_f': 'new_m', 'new_m_lru_wa_b': 'new_m', 'new_m_lru_ba_b': 'new_m', 'new_m_lru_wx_b': 'new_m', 'new_m_lru_bx_b': 'new_m', 'new_m_lru_lambda_b': 'new_m', 'new_m_gla_wg_f': 'new_m', 'new_m_gla_bg_f': 'new_m', 'new_m_gla_wg_b': 'new_m', 'new_m_gla_bg_b': 'new_m', 'new_m_gla_head_norm': 'new_m', 'new_m_w_out': 'new_m', 'new_m_w_mlp_up': 'new_m', 'new_m_w_mlp_down': 'new_m', 'new_v_meta_tokens': 'new_v', 'new_v_norm_mix_pre': 'new_v', 'new_v_norm_mix_post': 'new_v', 'new_v_norm_mlp_pre': 'new_v', 'new_v_norm_mlp_post': 'new_v', 'new_v_w_in': 'new_v', 'new_v_conv_w': 'new_v', 'new_v_conv_b': 'new_v', 'new_v_lru_wa_f': 'new_v', 'new_v_lru_ba_f': 'new_v', 'new_v_lru_wx_f': 'new_v', 'new_v_lru_bx_f': 'new_v', 'new_v_lru_lambda_f': 'new_v', 'new_v_lru_wa_b': 'new_v', 'new_v_lru_ba_b': 'new_v', 'new_v_lru_wx_b': 'new_v', 'new_v_lru_bx_b': 'new_v', 'new_v_lru_lambda_b': 'new_v', 'new_v_gla_wg_f': 'new_v', 'new_v_gla_bg_f': 'new_v', 'new_v_gla_wg_b': 'new_v', 'new_v_gla_bg_b': 'new_v', 'new_v_gla_head_norm': 'new_v', 'new_v_w_out': 'new_v', 'new_v_w_mlp_up': 'new_v', 'new_v_w_mlp_down': 'new_v'}


def _forward(args):
    return _fwd_reference(*[args[k] for k in FWD_PARAMS])


def _output_shape():
    def fwd():
        inp = _fwd_setup_inputs(0)
        return _fwd_reference(*[inp[k] for k in FWD_PARAMS])
    out = _jax.eval_shape(fwd)
    return out.shape, out.dtype

N_MICROBATCH = 1
ADAM_LR = 0.001
ADAM_B1 = 0.9
ADAM_B2 = 0.999
ADAM_EPS = 1e-08
ADAM_WD = 0.01
ADAM_STEP = 10
PER_EXAMPLE_BATCH_AXIS = {'x': 0, 'loss_target': 0}
SHARED_INPUTS = []
_WEIGHT_DTYPES = {'meta_tokens': _jnp.float32, 'norm_mix_pre': _jnp.float32, 'norm_mix_post': _jnp.float32, 'norm_mlp_pre': _jnp.float32, 'norm_mlp_post': _jnp.float32, 'w_in': _jnp.float32, 'conv_w': _jnp.float32, 'conv_b': _jnp.float32, 'lru_wa_f': _jnp.float32, 'lru_ba_f': _jnp.float32, 'lru_wx_f': _jnp.float32, 'lru_bx_f': _jnp.float32, 'lru_lambda_f': _jnp.float32, 'lru_wa_b': _jnp.float32, 'lru_ba_b': _jnp.float32, 'lru_wx_b': _jnp.float32, 'lru_bx_b': _jnp.float32, 'lru_lambda_b': _jnp.float32, 'gla_wg_f': _jnp.float32, 'gla_bg_f': _jnp.float32, 'gla_wg_b': _jnp.float32, 'gla_bg_b': _jnp.float32, 'gla_head_norm': _jnp.float32, 'w_out': _jnp.float32, 'w_mlp_up': _jnp.float32, 'w_mlp_down': _jnp.float32}
MOMENT_SCALE = {'meta_tokens': 9.976004e-02, 'norm_mix_pre': 9.885960e+00, 'norm_mix_post': 6.624959e+01, 'norm_mlp_pre': 7.069938e+00, 'norm_mlp_post': 6.975350e+01, 'w_in': 6.480567e+00, 'conv_w': 1.955532e+01, 'conv_b': 1.943604e+02, 'lru_wa_f': 3.336240e+00, 'lru_ba_f': 2.022174e+00, 'lru_wx_f': 6.147936e+00, 'lru_bx_f': 3.477553e+00, 'lru_lambda_f': 3.829595e+00, 'lru_wa_b': 3.285917e+00, 'lru_ba_b': 2.041851e+00, 'lru_wx_b': 6.130486e+00, 'lru_bx_b': 3.724765e+00, 'lru_lambda_b': 3.213872e+00, 'gla_wg_f': 1.056863e-01, 'gla_bg_f': 3.896167e-01, 'gla_wg_b': 9.525829e-02, 'gla_bg_b': 3.936334e-01, 'gla_head_norm': 1.709312e+00, 'w_out': 1.475303e+01, 'w_mlp_up': 3.725959e+00, 'w_mlp_down': 2.062267e+01}


def _to_microbatches(a, axis):
    t = _jnp.moveaxis(a, axis, 0)
    t = t.reshape((N_MICROBATCH, t.shape[0] // N_MICROBATCH) + t.shape[1:])
    return _jnp.moveaxis(t, 1, axis + 1)


def setup_inputs(seed: int = 0) -> dict:
    inp = _fwd_setup_inputs(seed)
    key = _jax.random.fold_in(_jax.random.key(seed), 7919)
    shape, _ = _output_shape()
    out = dict(inp)
    out["loss_target"] = _jax.random.normal(_jax.random.fold_in(key, 0), shape, _jnp.float32)
    for i, name in enumerate(TWIN_WEIGHTS):
        w = inp[name].astype(_jnp.float32)
        if MOMENT_SCALE is None:
            s = _jnp.sqrt(_jnp.mean(_jnp.square(w)) + 1e-30)
        else:
            s = MOMENT_SCALE[name]
        km, kv = _jax.random.split(_jax.random.fold_in(key, i + 1))
        out[name] = w
        out["m_" + name] = s * _jax.random.normal(km, w.shape, _jnp.float32)
        out["v_" + name] = (s * s) * _jax.random.uniform(kv, w.shape, _jnp.float32, 0.5, 1.5)
    if N_MICROBATCH > 1:
        for name, axis in PER_EXAMPLE_BATCH_AXIS.items():
            out[name] = _to_microbatches(out[name], axis)
    return {'x': out['x'], 'meta_tokens': out['meta_tokens'], 'norm_mix_pre': out['norm_mix_pre'], 'norm_mix_post': out['norm_mix_post'], 'norm_mlp_pre': out['norm_mlp_pre'], 'norm_mlp_post': out['norm_mlp_post'], 'w_in': out['w_in'], 'conv_w': out['conv_w'], 'conv_b': out['conv_b'], 'lru_wa_f': out['lru_wa_f'], 'lru_ba_f': out['lru_ba_f'], 'lru_wx_f': out['lru_wx_f'], 'lru_bx_f': out['lru_bx_f'], 'lru_lambda_f': out['lru_lambda_f'], 'lru_wa_b': out['lru_wa_b'], 'lru_ba_b': out['lru_ba_b'], 'lru_wx_b': out['lru_wx_b'], 'lru_bx_b': out['lru_bx_b'], 'lru_lambda_b': out['lru_lambda_b'], 'gla_wg_f': out['gla_wg_f'], 'gla_bg_f': out['gla_bg_f'], 'gla_wg_b': out['gla_wg_b'], 'gla_bg_b': out['gla_bg_b'], 'gla_head_norm': out['gla_head_norm'], 'w_out': out['w_out'], 'w_mlp_up': out['w_mlp_up'], 'w_mlp_down': out['w_mlp_down'], 'loss_target': out['loss_target'], 'm_meta_tokens': out['m_meta_tokens'], 'm_norm_mix_pre': out['m_norm_mix_pre'], 'm_norm_mix_post': out['m_norm_mix_post'], 'm_norm_mlp_pre': out['m_norm_mlp_pre'], 'm_norm_mlp_post': out['m_norm_mlp_post'], 'm_w_in': out['m_w_in'], 'm_conv_w': out['m_conv_w'], 'm_conv_b': out['m_conv_b'], 'm_lru_wa_f': out['m_lru_wa_f'], 'm_lru_ba_f': out['m_lru_ba_f'], 'm_lru_wx_f': out['m_lru_wx_f'], 'm_lru_bx_f': out['m_lru_bx_f'], 'm_lru_lambda_f': out['m_lru_lambda_f'], 'm_lru_wa_b': out['m_lru_wa_b'], 'm_lru_ba_b': out['m_lru_ba_b'], 'm_lru_wx_b': out['m_lru_wx_b'], 'm_lru_bx_b': out['m_lru_bx_b'], 'm_lru_lambda_b': out['m_lru_lambda_b'], 'm_gla_wg_f': out['m_gla_wg_f'], 'm_gla_bg_f': out['m_gla_bg_f'], 'm_gla_wg_b': out['m_gla_wg_b'], 'm_gla_bg_b': out['m_gla_bg_b'], 'm_gla_head_norm': out['m_gla_head_norm'], 'm_w_out': out['m_w_out'], 'm_w_mlp_up': out['m_w_mlp_up'], 'm_w_mlp_down': out['m_w_mlp_down'], 'v_meta_tokens': out['v_meta_tokens'], 'v_norm_mix_pre': out['v_norm_mix_pre'], 'v_norm_mix_post': out['v_norm_mix_post'], 'v_norm_mlp_pre': out['v_norm_mlp_pre'], 'v_norm_mlp_post': out['v_norm_mlp_post'], 'v_w_in': out['v_w_in'], 'v_conv_w': out['v_conv_w'], 'v_conv_b': out['v_conv_b'], 'v_lru_wa_f': out['v_lru_wa_f'], 'v_lru_ba_f': out['v_lru_ba_f'], 'v_lru_wx_f': out['v_lru_wx_f'], 'v_lru_bx_f': out['v_lru_bx_f'], 'v_lru_lambda_f': out['v_lru_lambda_f'], 'v_lru_wa_b': out['v_lru_wa_b'], 'v_lru_ba_b': out['v_lru_ba_b'], 'v_lru_wx_b': out['v_lru_wx_b'], 'v_lru_bx_b': out['v_lru_bx_b'], 'v_lru_lambda_b': out['v_lru_lambda_b'], 'v_gla_wg_f': out['v_gla_wg_f'], 'v_gla_bg_f': out['v_gla_bg_f'], 'v_gla_wg_b': out['v_gla_wg_b'], 'v_gla_bg_b': out['v_gla_bg_b'], 'v_gla_head_norm': out['v_gla_head_norm'], 'v_w_out': out['v_w_out'], 'v_w_mlp_up': out['v_w_mlp_up'], 'v_w_mlp_down': out['v_w_mlp_down']}


def _loss(weights, diff, rest, loss_target):
    with _jax.named_scope("forward"):
        args = {**rest, TWIN_DIFF_INPUT: diff, **{k: w.astype(_WEIGHT_DTYPES[k]) for k, w in weights.items()}}
        y = _forward(args)
    with _jax.named_scope("loss_head"):
        err = _jnp.square(y.astype(_jnp.float32) - loss_target)
        return 0.5 * _jnp.sum(_jnp.mean(err, axis=-1)) if err.ndim else 0.5 * err


def _adamw(w, g, m, v):
    m = ADAM_B1 * m + (1.0 - ADAM_B1) * g
    v = ADAM_B2 * v + (1.0 - ADAM_B2) * _jnp.square(g)
    m_hat = m / (1.0 - ADAM_B1 ** ADAM_STEP)
    v_hat = v / (1.0 - ADAM_B2 ** ADAM_STEP)
    delta = -ADAM_LR * (m_hat / (_jnp.sqrt(v_hat) + ADAM_EPS) + ADAM_WD * w)
    return delta, m, v


def reference(x, meta_tokens, norm_mix_pre, norm_mix_post, norm_mlp_pre, norm_mlp_post, w_in, conv_w, conv_b, lru_wa_f, lru_ba_f, lru_wx_f, lru_bx_f, lru_lambda_f, lru_wa_b, lru_ba_b, lru_wx_b, lru_bx_b, lru_lambda_b, gla_wg_f, gla_bg_f, gla_wg_b, gla_bg_b, gla_head_norm, w_out, w_mlp_up, w_mlp_down, loss_target, m_meta_tokens, m_norm_mix_pre, m_norm_mix_post, m_norm_mlp_pre, m_norm_mlp_post, m_w_in, m_conv_w, m_conv_b, m_lru_wa_f, m_lru_ba_f, m_lru_wx_f, m_lru_bx_f, m_lru_lambda_f, m_lru_wa_b, m_lru_ba_b, m_lru_wx_b, m_lru_bx_b, m_lru_lambda_b, m_gla_wg_f, m_gla_bg_f, m_gla_wg_b, m_gla_bg_b, m_gla_head_norm, m_w_out, m_w_mlp_up, m_w_mlp_down, v_meta_tokens, v_norm_mix_pre, v_norm_mix_post, v_norm_mlp_pre, v_norm_mlp_post, v_w_in, v_conv_w, v_conv_b, v_lru_wa_f, v_lru_ba_f, v_lru_wx_f, v_lru_bx_f, v_lru_lambda_f, v_lru_wa_b, v_lru_ba_b, v_lru_wx_b, v_lru_bx_b, v_lru_lambda_b, v_gla_wg_f, v_gla_bg_f, v_gla_wg_b, v_gla_bg_b, v_gla_head_norm, v_w_out, v_w_mlp_up, v_w_mlp_down):
    given = dict(x=x, meta_tokens=meta_tokens, norm_mix_pre=norm_mix_pre, norm_mix_post=norm_mix_post, norm_mlp_pre=norm_mlp_pre, norm_mlp_post=norm_mlp_post, w_in=w_in, conv_w=conv_w, conv_b=conv_b, lru_wa_f=lru_wa_f, lru_ba_f=lru_ba_f, lru_wx_f=lru_wx_f, lru_bx_f=lru_bx_f, lru_lambda_f=lru_lambda_f, lru_wa_b=lru_wa_b, lru_ba_b=lru_ba_b, lru_wx_b=lru_wx_b, lru_bx_b=lru_bx_b, lru_lambda_b=lru_lambda_b, gla_wg_f=gla_wg_f, gla_bg_f=gla_bg_f, gla_wg_b=gla_wg_b, gla_bg_b=gla_bg_b, gla_head_norm=gla_head_norm, w_out=w_out, w_mlp_up=w_mlp_up, w_mlp_down=w_mlp_down, loss_target=loss_target, m_meta_tokens=m_meta_tokens, m_norm_mix_pre=m_norm_mix_pre, m_norm_mix_post=m_norm_mix_post, m_norm_mlp_pre=m_norm_mlp_pre, m_norm_mlp_post=m_norm_mlp_post, m_w_in=m_w_in, m_conv_w=m_conv_w, m_conv_b=m_conv_b, m_lru_wa_f=m_lru_wa_f, m_lru_ba_f=m_lru_ba_f, m_lru_wx_f=m_lru_wx_f, m_lru_bx_f=m_lru_bx_f, m_lru_lambda_f=m_lru_lambda_f, m_lru_wa_b=m_lru_wa_b, m_lru_ba_b=m_lru_ba_b, m_lru_wx_b=m_lru_wx_b, m_lru_bx_b=m_lru_bx_b, m_lru_lambda_b=m_lru_lambda_b, m_gla_wg_f=m_gla_wg_f, m_gla_bg_f=m_gla_bg_f, m_gla_wg_b=m_gla_wg_b, m_gla_bg_b=m_gla_bg_b, m_gla_head_norm=m_gla_head_norm, m_w_out=m_w_out, m_w_mlp_up=m_w_mlp_up, m_w_mlp_down=m_w_mlp_down, v_meta_tokens=v_meta_tokens, v_norm_mix_pre=v_norm_mix_pre, v_norm_mix_post=v_norm_mix_post, v_norm_mlp_pre=v_norm_mlp_pre, v_norm_mlp_post=v_norm_mlp_post, v_w_in=v_w_in, v_conv_w=v_conv_w, v_conv_b=v_conv_b, v_lru_wa_f=v_lru_wa_f, v_lru_ba_f=v_lru_ba_f, v_lru_wx_f=v_lru_wx_f, v_lru_bx_f=v_lru_bx_f, v_lru_lambda_f=v_lru_lambda_f, v_lru_wa_b=v_lru_wa_b, v_lru_ba_b=v_lru_ba_b, v_lru_wx_b=v_lru_wx_b, v_lru_bx_b=v_lru_bx_b, v_lru_lambda_b=v_lru_lambda_b, v_gla_wg_f=v_gla_wg_f, v_gla_bg_f=v_gla_bg_f, v_gla_wg_b=v_gla_wg_b, v_gla_bg_b=v_gla_bg_b, v_gla_head_norm=v_gla_head_norm, v_w_out=v_w_out, v_w_mlp_up=v_w_mlp_up, v_w_mlp_down=v_w_mlp_down)
    weights = {n: given[n] for n in TWIN_WEIGHTS}
    shared = {n: given[n] for n in SHARED_INPUTS}
    per_example = {n: given[n] for n in ['x']}
    grad_fn = _jax.value_and_grad(_loss, argnums=(0, 1))

    def one_microbatch(ex, loss_target):
        ex = dict(ex)
        diff = ex.pop(TWIN_DIFF_INPUT)
        return grad_fn(weights, diff, {**shared, **ex}, loss_target)

    if N_MICROBATCH == 1:
        loss, (grad_w, grad_x) = one_microbatch(per_example, given["loss_target"])
    else:
        def body(carry, xs):
            loss_sum, grad_sum = carry
            l_k, (gw_k, gx_k) = one_microbatch(xs[0], xs[1])
            with _jax.named_scope("update"):
                return (loss_sum + l_k, _jax.tree.map(_jnp.add, grad_sum, gw_k)), gx_k

        init = (_jnp.zeros((), _jnp.float32), _jax.tree.map(_jnp.zeros_like, weights))
        (loss, grad_w), grad_x = _jax.lax.scan(body, init, (per_example, given["loss_target"]))
    with _jax.named_scope("update"):
        delta_w, new_m, new_v = {}, {}, {}
        for n in TWIN_WEIGHTS:
            delta_w[n], new_m[n], new_v[n] = _adamw(weights[n], grad_w[n], given["m_" + n], given["v_" + n])
    return (loss, grad_x, *[grad_w[n] for n in TWIN_WEIGHTS], *[delta_w[n] for n in TWIN_WEIGHTS],
            *[new_m[n] for n in TWIN_WEIGHTS], *[new_v[n] for n in TWIN_WEIGHTS])
```

```python
import math

import jax
import jax.numpy as jnp
from jax import lax
from jax.experimental import pallas as pl
from jax.experimental.pallas import tpu as pltpu

F32 = jnp.float32
BF16 = jnp.bfloat16

D_MODEL = 1024
DEPTH = 2
N_META = 16
LRU_W = 512
LRU_HEADS = 8
LRU_HD = 64
LRU_C = 8.0
GLA_W = 512
GLA_H = 4
GLA_DK = 64
GLA_DV = 128
GLA_RANK = 16
GLA_GATE_NORM = 16.0
D_FF = 4096
D_IN = 2592
EPS = 1e-6
ADAM_LR, ADAM_B1, ADAM_B2, ADAM_EPS, ADAM_WD, ADAM_STEP = 0.001, 0.9, 0.999, 1e-08, 0.01, 10

LANES = 128
SUBLANES = 8
FRONT = 128
BACK = 128
D_IN_PAD = 2688
ZG_OFF = 2560
CHUNK = 64
EXP_CLAMP = 80.0
VMEM_LIMIT = 56 * 1024 * 1024
MESH_ID = pl.DeviceIdType.MESH

NN = (((1,), (0,)), ((), ()))
NT = (((1,), (1,)), ((), ()))
TN = (((0,), (0,)), ((), ()))


def _dot(a, b, dims=NN):
    return lax.dot_general(a, b, dims, preferred_element_type=F32)


def _cparams(sem):
    return pltpu.CompilerParams(dimension_semantics=sem, vmem_limit_bytes=VMEM_LIMIT)


def _sigmoid(x):
    return 1.0 / (1.0 + jnp.exp(-x))


def _gelu(x):
    c = math.sqrt(2.0 / math.pi)
    return 0.5 * x * (1.0 + jnp.tanh(c * (x + 0.044715 * x * x * x)))


def _gelu_grad(x):
    c = math.sqrt(2.0 / math.pi)
    t = jnp.tanh(c * (x + 0.044715 * x * x * x))
    return 0.5 * (1.0 + t) + 0.5 * x * (1.0 - t * t) * c * (1.0 + 3.0 * 0.044715 * x * x)


def _rms_fwd(x, g):
    rstd = lax.rsqrt(jnp.mean(x * x, axis=1, keepdims=True) + EPS)
    return (x * rstd) * g


def _rms_bwd(x, g, dy):
    rstd = lax.rsqrt(jnp.mean(x * x, axis=1, keepdims=True) + EPS)
    xh = x * rstd
    dxh = dy * g
    dx = rstd * (dxh - xh * jnp.mean(dxh * xh, axis=1, keepdims=True))
    dg = jnp.sum(dy * xh, axis=0, keepdims=True)
    return dx, dg


def _mm(a, b, mode, tm, tn, tk, outs, name, epilogue=None, extras=()):
    if mode == "nn":
        (m, k), n = a.shape, b.shape[1]
        a_spec = pl.BlockSpec((tm, tk), lambda i, j, kk: (i, kk))
        b_spec = pl.BlockSpec((tk, tn), lambda i, j, kk: (kk, j))
        dims = NN
    elif mode == "nt":
        (m, k), n = a.shape, b.shape[0]
        a_spec = pl.BlockSpec((tm, tk), lambda i, j, kk: (i, kk))
        b_spec = pl.BlockSpec((tn, tk), lambda i, j, kk: (j, kk))
        dims = NT
    else:
        (k, m), n = a.shape, b.shape[1]
        a_spec = pl.BlockSpec((tk, tm), lambda i, j, kk: (kk, i))
        b_spec = pl.BlockSpec((tk, tn), lambda i, j, kk: (kk, j))
        dims = TN
    assert m % tm == 0 and n % tn == 0 and k % tk == 0, (name, m, n, k)
    nk = k // tk
    ne = len(extras)
    e_specs = [pl.BlockSpec((tm, tn), lambda i, j, kk: (i, j)) for _ in extras]
    o_specs, o_shapes = [], []
    for width, dtype in outs:
        if width == n:
            o_specs.append(pl.BlockSpec((tm, tn), lambda i, j, kk: (i, j)))
        else:
            assert n == tn, name
            o_specs.append(pl.BlockSpec((tm, width), lambda i, j, kk: (i, 0)))
        o_shapes.append(jax.ShapeDtypeStruct((m, width), dtype))

    def finish(acc, extra_refs, out_refs):
        res = epilogue(acc, *[e[...] for e in extra_refs]) if epilogue else (acc,)
        for o, r in zip(out_refs, res):
            o[...] = r.astype(o.dtype)

    if nk == 1:
        def body(a_ref, b_ref, *rest):
            finish(_dot(a_ref[...], b_ref[...], dims), rest[:ne], rest[ne:])
        scratch = []
    else:
        def body(a_ref, b_ref, *rest):
            acc_ref = rest[-1]
            kk = pl.program_id(2)

            @pl.when(kk == 0)
            def _():
                acc_ref[...] = jnp.zeros_like(acc_ref)

            acc_ref[...] += _dot(a_ref[...], b_ref[...], dims)

            @pl.when(kk == nk - 1)
            def _():
                finish(acc_ref[...], rest[:ne], rest[ne:-1])
        scratch = [pltpu.VMEM((tm, tn), F32)]

    return pl.pallas_call(
        body, name=name, grid=(m // tm, n // tn, nk),
        in_specs=[a_spec, b_spec] + e_specs, out_specs=o_specs, out_shape=o_shapes,
        scratch_shapes=scratch,
        compiler_params=_cparams(("parallel", "parallel", "arbitrary")),
    )(a, b, *extras)


def _rows(body, n_rows, tm, ins, consts, outs, accs, name):
    assert n_rows % tm == 0, (name, n_rows, tm)
    in_specs = [pl.BlockSpec((tm, w), (lambda i, cb=cb: (i, cb))) for _, w, cb in ins]
    in_specs += [pl.BlockSpec(c.shape, lambda i: (0, 0)) for c in consts]
    out_specs = [pl.BlockSpec((tm, w), lambda i: (i, 0)) for w, _ in outs]
    out_specs += [pl.BlockSpec(s, lambda i: (0, 0)) for s, _ in accs]
    out_shape = [jax.ShapeDtypeStruct((n_rows, w), d) for w, d in outs]
    out_shape += [jax.ShapeDtypeStruct(s, d) for s, d in accs]
    ni, nc, no = len(ins), len(consts), len(outs)

    def kern(*refs):
        body(pl.program_id(0), refs[:ni], refs[ni:ni + nc], refs[ni + nc:ni + nc + no], refs[ni + nc + no:])

    res = pl.pallas_call(
        kern, name=name, grid=(n_rows // tm,), in_specs=in_specs, out_specs=out_specs, out_shape=out_shape,
        compiler_params=_cparams(("arbitrary",)),
    )(*[a for a, _, _ in ins], *consts)
    return res


def _acc_add(i, ref, val):
    @pl.when(i == 0)
    def _():
        ref[...] = jnp.zeros_like(ref)

    ref[...] += val


def _norm_cast(h, g, t_rows, name):
    def body(i, ins, consts, outs, accs):
        outs[0][...] = _rms_fwd(ins[0][...], consts[0][...]).astype(BF16)
    return _rows(body, t_rows, 384, [(h, D_MODEL, 0)], [g], [(D_MODEL, BF16)], [], name)[0]


def _norm_residual(y, g, h, t_rows, name):
    def body(i, ins, consts, outs, accs):
        outs[0][...] = ins[1][...] + _rms_fwd(ins[0][...], consts[0][...])
    return _rows(body, t_rows, 384, [(y, D_MODEL, 0), (h, D_MODEL, 0)], [g], [(D_MODEL, F32)], [], name)[0]


def _norm_residual_bwd(dh, y, g, t_rows, name):
    def body(i, ins, consts, outs, accs):
        dy, dg = _rms_bwd(ins[1][...], consts[0][...], ins[0][...])
        outs[0][...] = dy.astype(BF16)
        _acc_add(i, accs[0], dg)
    return _rows(body, t_rows, 384, [(dh, D_MODEL, 0), (y, D_MODEL, 0)], [g],
                 [(D_MODEL, BF16)], [((1, D_MODEL), F32)], name)


def _norm_bwd_add(dn, x, g, dh, t_rows, name):
    def body(i, ins, consts, outs, accs):
        dx, dg = _rms_bwd(ins[1][...], consts[0][...], ins[0][...])
        outs[0][...] = ins[2][...] + dx
        _acc_add(i, accs[0], dg)
    return _rows(body, t_rows, 384, [(dn, D_MODEL, 0), (x, D_MODEL, 0), (dh, D_MODEL, 0)], [g],
                 [(D_MODEL, F32)], [((1, D_MODEL), F32)], name)


def _head_norm_parts(o):
    ns, rs = [], []
    for hd in range(GLA_H):
        oh = o[:, hd * GLA_DV:(hd + 1) * GLA_DV]
        r = lax.rsqrt(jnp.mean(oh * oh, axis=1, keepdims=True) + EPS)
        ns.append(oh * r)
        rs.append(r)
    return ns, rs


def _mix_fwd(h_f, h_b, z_lru, o_f, o_b, z_go, head_norm, t_rows, name):
    def body(i, ins, consts, outs, accs):
        hf, hb, gate, of, ob, go = [r[...] for r in ins]
        hn = consts[0][...]
        outs[0][:, 0:LRU_W] = ((hf + hb) * _gelu(gate)).astype(BF16)
        ns, _ = _head_norm_parts(of + ob)
        silu = go * _sigmoid(go)
        for hd in range(GLA_H):
            sl = slice(hd * GLA_DV, (hd + 1) * GLA_DV)
            outs[0][:, LRU_W + hd * GLA_DV:LRU_W + (hd + 1) * GLA_DV] = (ns[hd] * hn[:, sl] * silu[:, sl]).astype(BF16)
    ins = [(h_f, LRU_W, 0), (h_b, LRU_W, 0), (z_lru, LRU_W, 1), (o_f, GLA_W, 0), (o_b, GLA_W, 0), (z_go, GLA_W, 0)]
    return _rows(body, t_rows, 384, ins, [head_norm], [(D_MODEL, BF16)], [], name)[0]


def _mix_bwd(dymix, h_f, h_b, z_lru, o_f, o_b, z_go, head_norm, t_rows, name):
    def body(i, ins, consts, outs, accs):
        dyl, dyg, hf, hb, gate, of, ob, go = [r[...] for r in ins]
        hn = consts[0][...]
        outs[0][...] = dyl * _gelu(gate)
        outs[1][...] = dyl * (hf + hb) * _gelu_grad(gate)
        ns, rs = _head_norm_parts(of + ob)
        sg = _sigmoid(go)
        silu = go * sg
        dsilu = sg * (1.0 + go * (1.0 - sg))
        dhn = []
        for hd in range(GLA_H):
            sl = slice(hd * GLA_DV, (hd + 1) * GLA_DV)
            dy = dyg[:, sl]
            e = dy * hn[:, sl] * silu[:, sl]
            outs[2][:, sl] = rs[hd] * (e - ns[hd] * jnp.mean(e * ns[hd], axis=1, keepdims=True))
            outs[3][:, sl] = dy * ns[hd] * hn[:, sl] * dsilu[:, sl]
            dhn.append(jnp.sum(dy * ns[hd] * silu[:, sl], axis=0, keepdims=True))
        _acc_add(i, accs[0], jnp.concatenate(dhn, axis=1))
    ins = [(dymix, LRU_W, 0), (dymix, GLA_W, 1), (h_f, LRU_W, 0), (h_b, LRU_W, 0), (z_lru, LRU_W, 1),
           (o_f, GLA_W, 0), (o_b, GLA_W, 0), (z_go, GLA_W, 0)]
    return _rows(body, t_rows, 384, ins, [head_norm],
                 [(LRU_W, F32), (LRU_W, F32), (GLA_W, F32), (GLA_W, F32)], [((1, GLA_W), F32)], name)


def _dz_assemble(dx_br, dgate, gla_grads, dgo, z_zg, wg, t_rows, name):
    dq_f, dk_f, dv_f, dpre_f, dq_b, dk_b, dv_b, dpre_b = gla_grads
    qk = GLA_H * GLA_DK

    def body(i, ins, consts, outs, accs):
        (dxb, dgt, dqf, dqb, dkf, dkb, dvf, dvb, dg_o, dpf, dpb, zg) = [r[...] for r in ins]
        w = consts[0][...]
        o = outs[0]
        o[:, 0:LRU_W] = dxb.astype(BF16)
        o[:, LRU_W:2 * LRU_W] = dgt.astype(BF16)
        o[:, 1024:1024 + qk] = ((dqf + dqb) * (GLA_DK ** -0.5)).astype(BF16)
        o[:, 1280:1280 + qk] = (dkf + dkb).astype(BF16)
        o[:, 1536:1536 + GLA_W] = (dvf + dvb).astype(BF16)
        o[:, 2048:2048 + GLA_W] = dg_o.astype(BF16)
        dpre = jnp.concatenate([dpf, dpb], axis=1)
        dpre16 = dpre.astype(BF16)
        o[:, ZG_OFF:ZG_OFF + LANES] = _dot(dpre16, w, NT).astype(BF16)
        _acc_add(i, accs[0], _dot(zg.astype(BF16), dpre16, TN))
        _acc_add(i, accs[1], jnp.sum(dpre, axis=0, keepdims=True))

    ins = [(dx_br, LRU_W, 0), (dgate, LRU_W, 0), (dq_f, qk, 0), (dq_b, qk, 0), (dk_f, qk, 0), (dk_b, qk, 0),
           (dv_f, GLA_W, 0), (dv_b, GLA_W, 0), (dgo, GLA_W, 0), (dpre_f, qk, 0), (dpre_b, qk, 0), (z_zg, LANES, 0)]
    return _rows(body, t_rows, 384, ins, [wg], [(D_IN_PAD, BF16)],
                 [((LANES, 2 * qk), F32), ((1, 2 * qk), F32)], name)


def _loss_and_grad(h, target, t_rows, seq):
    tm = FRONT
    first, n_t = FRONT // tm, seq // tm

    def kern(h_ref, t_ref, dh_ref, loss_ref):
        i = pl.program_id(0)
        valid = jnp.logical_and(i >= first, i < first + n_t)

        @pl.when(i == 0)
        def _():
            loss_ref[...] = jnp.zeros_like(loss_ref)

        @pl.when(valid)
        def _():
            err = h_ref[...] - t_ref[...]
            dh_ref[...] = err * (1.0 / D_MODEL)
            loss_ref[...] += jnp.sum(err * err) * (0.5 / D_MODEL)

        @pl.when(jnp.logical_not(valid))
        def _():
            dh_ref[...] = jnp.zeros_like(dh_ref)

    return pl.pallas_call(
        kern, name="loss", grid=(t_rows // tm,),
        in_specs=[pl.BlockSpec((tm, D_MODEL), lambda i: (i, 0)),
                  pl.BlockSpec((tm, D_MODEL), lambda i: (jnp.clip(i - first, 0, n_t - 1), 0))],
        out_specs=[pl.BlockSpec((tm, D_MODEL), lambda i: (i, 0)), pl.BlockSpec((SUBLANES, LANES), lambda i: (0, 0))],
        out_shape=[jax.ShapeDtypeStruct((t_rows, D_MODEL), F32), jax.ShapeDtypeStruct((SUBLANES, LANES), F32)],
        compiler_params=_cparams(("arbitrary",)),
    )(h, target)


LRU_RB = 64
HALO = SUBLANES


def _scan8(a, u, rev):
    rows = lax.broadcasted_iota(jnp.int32, a.shape, 0)
    for d in (1, 2, 4):
        if rev:
            a_s, u_s, ok = pltpu.roll(a, SUBLANES - d, 0), pltpu.roll(u, SUBLANES - d, 0), rows < SUBLANES - d
        else:
            a_s, u_s, ok = pltpu.roll(a, d, 0), pltpu.roll(u, d, 0), rows >= d
        u = jnp.where(ok, a * u_s + u, u)
        a = jnp.where(ok, a * a_s, a)
    return a, u


def _edge_row(x, rev):
    r = x[0:1, :] if rev else x[SUBLANES - 1:SUBLANES, :]
    return jnp.broadcast_to(r, x.shape)


def _lru_gates(xc, pre_a, pre_x, ba, bx, sp8):
    r = _sigmoid(pre_a + ba)
    i = _sigmoid(pre_x + bx)
    log_a = -(r * sp8)
    a = jnp.exp(log_a)
    y = 2.0 * log_a
    series = -y * (1.0 + y * (0.5 + y * (1.0 / 6.0 + y * (1.0 / 24.0 + y * (1.0 / 120.0)))))
    om = jnp.where(y > -0.25, series, 1.0 - a * a)
    s = jnp.sqrt(om)
    return r, i, a, s


def _softplus(x):
    return jnp.maximum(x, 0.0) + jnp.log(1.0 + jnp.exp(-jnp.abs(x)))


def _conv_taps(xpad_ref, r0, nrows, shifts=(-2, -1, 0, 1)):
    ext = xpad_ref[pl.ds(r0, nrows + 2 * HALO), :]
    taps = []
    for s in shifts:
        sh = ext if s == 0 else pltpu.roll(ext, (-s) % (nrows + 2 * HALO), 0)
        taps.append(sh[HALO:HALO + nrows])
    return taps


def _row_mask(r0, nrows, seq):
    rows = r0 + lax.broadcasted_iota(jnp.int32, (nrows, LANES), 0)
    return jnp.logical_and(rows >= FRONT - N_META, rows < FRONT + seq).astype(F32)


def _lru_load_conv(z_hbm, xpad, xc, sem, cw_ref, cb_ref, t_rows):
    j = pl.program_id(0)
    zeros = jnp.zeros((HALO, LANES), F32)
    xpad[0:HALO, :] = zeros
    xpad[t_rows + HALO:t_rows + 2 * HALO, :] = zeros
    cp = pltpu.make_async_copy(z_hbm.at[:, pl.ds(pl.multiple_of(j * LANES, LANES), LANES)],
                               xpad.at[pl.ds(HALO, t_rows)], sem)
    cp.start()
    cp.wait()
    cw = cw_ref[...]
    cb = cb_ref[...]

    def conv_blk(r, carry):
        r0 = pl.multiple_of(r * LRU_RB, LRU_RB)
        taps = _conv_taps(xpad, r0, LRU_RB)
        acc = cb + cw[0:1, :] * taps[0]
        for k in range(1, 4):
            acc = acc + cw[k:k + 1, :] * taps[k]
        xc[pl.ds(r0, LRU_RB), :] = acc
        return carry

    lax.fori_loop(0, t_rows // LRU_RB, conv_blk, 0)


def _lru_specs(t_rows):
    return [pl.BlockSpec(memory_space=pl.ANY),
            pl.BlockSpec((4, LANES), lambda j: (0, j)),
            pl.BlockSpec((1, LANES), lambda j: (0, j)),
            pl.BlockSpec((1, LANES, 4 * LANES), lambda j: (j, 0, 0)),
            pl.BlockSpec((SUBLANES, LANES), lambda j: (0, j))]


def _lru_fwd(z_lru, conv_w, conv_b, wg, pb, t_rows, seq, name):
    nblk = t_rows // LRU_RB
    nsub = LRU_RB // SUBLANES

    def kern(z_hbm, cw_ref, cb_ref, wg_ref, pb_ref, hf_ref, hb_ref, xpad, xc, sem):
        _lru_load_conv(z_hbm, xpad, xc, sem, cw_ref, cb_ref, t_rows)
        w = wg_ref[0]
        pb_v = pb_ref[...]
        for rev in (False, True):
            o = 2 if rev else 0
            ba, bx = pb_v[o:o + 1, :], pb_v[o + 1:o + 2, :]
            sp8 = LRU_C * _softplus(-pb_v[5:6, :] if rev else -pb_v[4:5, :])
            out_ref = hb_ref if rev else hf_ref

            def blk(it, carry, rev=rev, o=o, ba=ba, bx=bx, sp8=sp8, out_ref=out_ref):
                r = (nblk - 1 - it) if rev else it
                r0 = pl.multiple_of(r * LRU_RB, LRU_RB)
                xcb = xc[pl.ds(r0, LRU_RB), :]
                pre = _dot(xcb.astype(BF16), w)
                _, gi, a, s = _lru_gates(xcb, pre[:, o * LANES:(o + 1) * LANES], pre[:, (o + 1) * LANES:(o + 2) * LANES],
                                         ba, bx, sp8)
                u = s * (gi * xcb) * _row_mask(r0, LRU_RB, seq)
                hs = [None] * nsub
                for sb in (range(nsub - 1, -1, -1) if rev else range(nsub)):
                    rs = slice(sb * SUBLANES, (sb + 1) * SUBLANES)
                    a_c, h_l = _scan8(a[rs], u[rs], rev)
                    h = h_l + a_c * carry
                    carry = _edge_row(h, rev)
                    hs[sb] = h
                out_ref[pl.ds(r0, LRU_RB), :] = jnp.concatenate(hs, axis=0)
                return carry

            lax.fori_loop(0, nblk, blk, jnp.zeros((SUBLANES, LANES), F32))

    return pl.pallas_call(
        kern, name=name, grid=(LRU_W // LANES,), in_specs=_lru_specs(t_rows),
        out_specs=[pl.BlockSpec((t_rows, LANES), lambda j: (0, j))] * 2,
        out_shape=[jax.ShapeDtypeStruct((t_rows, LRU_W), F32)] * 2,
        scratch_shapes=[pltpu.VMEM((t_rows + 2 * HALO, LANES), F32), pltpu.VMEM((t_rows, LANES), F32),
                        pltpu.SemaphoreType.DMA],
        compiler_params=_cparams(("arbitrary",)),
    )(z_lru, conv_w, conv_b, wg, pb)


def _lru_bwd(z_lru, conv_w, conv_b, wg, pb, h_f, h_b, dh, t_rows, seq, name):
    nblk = t_rows // LRU_RB
    nsub = LRU_RB // SUBLANES

    def kern(z_hbm, cw_ref, cb_ref, wg_ref, pb_ref, hf_ref, hb_ref, dh_ref,
             dx_ref, dcw_ref, dcb_ref, dwg_ref, dpb_ref, xpad, xc, dxc, sem):
        _lru_load_conv(z_hbm, xpad, xc, sem, cw_ref, cb_ref, t_rows)
        w = wg_ref[0]
        pb_v = pb_ref[...]
        zeros = jnp.zeros((HALO, LANES), F32)
        dxc[0:HALO, :] = zeros
        dxc[t_rows + HALO:t_rows + 2 * HALO, :] = zeros
        dwg_ref[...] = jnp.zeros_like(dwg_ref)
        dpb_rows = [None] * SUBLANES

        for rev in (False, True):
            o = 2 if rev else 0
            ba, bx = pb_v[o:o + 1, :], pb_v[o + 1:o + 2, :]
            lam = pb_v[5:6, :] if rev else pb_v[4:5, :]
            sp8 = LRU_C * _softplus(-lam)
            dlam_scale = LRU_C * _sigmoid(-lam)
            h_ref = hb_ref if rev else hf_ref
            w_a, w_x = w[:, o * LANES:(o + 1) * LANES], w[:, (o + 1) * LANES:(o + 2) * LANES]

            def blk(it, carry, rev=rev, o=o, ba=ba, bx=bx, sp8=sp8, dlam_scale=dlam_scale, h_ref=h_ref,
                    w_a=w_a, w_x=w_x):
                nu_c, acc_ba, acc_bx, acc_lam = carry
                adj_rev = not rev
                r = (nblk - 1 - it) if adj_rev else it
                r0 = pl.multiple_of(r * LRU_RB, LRU_RB)
                xcb = xc[pl.ds(r0, LRU_RB), :]
                pre = _dot(xcb.astype(BF16), w)
                gr, gi, a, s = _lru_gates(xcb, pre[:, o * LANES:(o + 1) * LANES],
                                          pre[:, (o + 1) * LANES:(o + 2) * LANES], ba, bx, sp8)
                mask = _row_mask(r0, LRU_RB, seq)
                dhb = dh_ref[pl.ds(r0, LRU_RB), :]
                hb = h_ref[pl.ds(r0, LRU_RB), :]
                if rev:
                    nxt0 = pl.multiple_of(jnp.minimum(r0 + LRU_RB, t_rows - SUBLANES), SUBLANES)
                    edge = h_ref[pl.ds(nxt0, SUBLANES), :][0:1, :] * (r < nblk - 1).astype(F32)
                    h_prev = jnp.concatenate([hb[1:], edge], axis=0)
                else:
                    prv0 = pl.multiple_of(jnp.maximum(r0 - SUBLANES, 0), SUBLANES)
                    edge = h_ref[pl.ds(prv0, SUBLANES), :][SUBLANES - 1:SUBLANES, :] * (r > 0).astype(F32)
                    h_prev = jnp.concatenate([edge, hb[:-1]], axis=0)
                rows = lax.broadcasted_iota(jnp.int32, (SUBLANES, LANES), 0)
                lams = [None] * nsub
                for sb in (range(nsub - 1, -1, -1) if adj_rev else range(nsub)):
                    rs = slice(sb * SUBLANES, (sb + 1) * SUBLANES)
                    a_c, nu_l = _scan8(a[rs], a[rs] * dhb[rs], adj_rev)
                    nu = nu_l + a_c * nu_c
                    if adj_rev:
                        nu_next = jnp.where(rows == SUBLANES - 1, nu_c, pltpu.roll(nu, SUBLANES - 1, 0))
                    else:
                        nu_next = jnp.where(rows == 0, nu_c, pltpu.roll(nu, 1, 0))
                    lams[sb] = dhb[rs] + nu_next
                    nu_c = _edge_row(nu, adj_rev)
                lam_t = jnp.concatenate(lams, axis=0)
                du = lam_t * mask
                da = lam_t * h_prev
                ix = gi * xcb
                dlog = da * a - du * ix * (a * a) / s
                d_i = du * s * xcb
                dxcb = du * s * gi
                dr = -dlog * sp8
                dpa = dr * gr * (1.0 - gr)
                dpx = d_i * gi * (1.0 - gi)
                dpa16, dpx16, xc16 = dpa.astype(BF16), dpx.astype(BF16), xcb.astype(BF16)
                dxcb = dxcb + _dot(dpa16, w_a, NT) + _dot(dpx16, w_x, NT)
                dwg_ref[0, :, o * LANES:(o + 1) * LANES] += _dot(xc16, dpa16, TN)
                dwg_ref[0, :, (o + 1) * LANES:(o + 2) * LANES] += _dot(xc16, dpx16, TN)
                if rev:
                    dxc[pl.ds(r0 + HALO, LRU_RB), :] += dxcb
                else:
                    dxc[pl.ds(r0 + HALO, LRU_RB), :] = dxcb
                acc_ba = acc_ba + jnp.sum(dpa, axis=0, keepdims=True)
                acc_bx = acc_bx + jnp.sum(dpx, axis=0, keepdims=True)
                acc_lam = acc_lam + jnp.sum(dlog * gr, axis=0, keepdims=True)
                return nu_c, acc_ba, acc_bx, acc_lam

            z1 = jnp.zeros((1, LANES), F32)
            _, s_ba, s_bx, s_lam = lax.fori_loop(0, nblk, blk, (jnp.zeros((SUBLANES, LANES), F32), z1, z1, z1))
            dpb_rows[o], dpb_rows[o + 1] = s_ba, s_bx
            dpb_rows[5 if rev else 4] = s_lam * dlam_scale
        dpb_rows[6] = dpb_rows[7] = jnp.zeros((1, LANES), F32)
        dpb_ref[...] = jnp.concatenate(dpb_rows, axis=0)

        cw = cw_ref[...]

        def conv_bwd(r, carry):
            r0 = pl.multiple_of(r * LRU_RB, LRU_RB)
            gt = _conv_taps(dxc, r0, LRU_RB, (2, 1, 0, -1))
            xt = _conv_taps(xpad, r0, LRU_RB)
            dx_ref[pl.ds(r0, LRU_RB), :] = (cw[0:1, :] * gt[0] + cw[1:2, :] * gt[1] + cw[2:3, :] * gt[2]
                                           + cw[3:4, :] * gt[3])
            g = gt[2]
            new = [carry[k] + jnp.sum(g * xt[k], axis=0, keepdims=True) for k in range(4)]
            new.append(carry[4] + jnp.sum(g, axis=0, keepdims=True))
            return tuple(new)

        z1 = jnp.zeros((1, LANES), F32)
        sums = lax.fori_loop(0, nblk, conv_bwd, (z1, z1, z1, z1, z1))
        dcw_ref[...] = jnp.concatenate(sums[:4], axis=0)
        dcb_ref[...] = sums[4]

    col = lambda j: (0, j)
    return pl.pallas_call(
        kern, name=name, grid=(LRU_W // LANES,),
        in_specs=_lru_specs(t_rows) + [pl.BlockSpec((t_rows, LANES), col)] * 3,
        out_specs=[pl.BlockSpec((t_rows, LANES), col), pl.BlockSpec((4, LANES), col), pl.BlockSpec((1, LANES), col),
                   pl.BlockSpec((1, LANES, 4 * LANES), lambda j: (j, 0, 0)), pl.BlockSpec((SUBLANES, LANES), col)],
        out_shape=[jax.ShapeDtypeStruct((t_rows, LRU_W), F32), jax.ShapeDtypeStruct((4, LRU_W), F32),
                   jax.ShapeDtypeStruct((1, LRU_W), F32), jax.ShapeDtypeStruct((4, LANES, 4 * LANES), F32),
                   jax.ShapeDtypeStruct((SUBLANES, LRU_W), F32)],
        scratch_shapes=[pltpu.VMEM((t_rows + 2 * HALO, LANES), F32), pltpu.VMEM((t_rows, LANES), F32),
                        pltpu.VMEM((t_rows + 2 * HALO, LANES), F32), pltpu.SemaphoreType.DMA],
        compiler_params=_cparams(("arbitrary",)),
    )(z_lru, conv_w, conv_b, wg, pb, h_f, h_b, dh)


QK = GLA_H * GLA_DK


def _cumsum_rows(x, rev):
    n = x.shape[0]
    rows = lax.broadcasted_iota(jnp.int32, x.shape, 0)
    d = 1
    while d < n:
        if rev:
            x = x + jnp.where(rows < n - d, pltpu.roll(x, n - d, 0), 0.0)
        else:
            x = x + jnp.where(rows >= d, pltpu.roll(x, d, 0), 0.0)
        d *= 2
    return x


def _gla_chunk_terms(q, k, zg, wg, bg, rev):
    pre = (_dot(zg.astype(BF16), wg) + bg)[:, (QK if rev else 0):(2 * QK if rev else QK)]
    g = (jnp.minimum(pre, 0.0) - jnp.log(1.0 + jnp.exp(-jnp.abs(pre)))) * (1.0 / GLA_GATE_NORM)
    b = _cumsum_rows(g, rev)
    b_last = b[0:1, :] if rev else b[CHUNK - 1:CHUNK, :]
    b_mid = b[CHUNK // 2:CHUNK // 2 + 1, :]
    qs = q * (GLA_DK ** -0.5)
    terms = dict(
        pre=pre, qs=qs, k=k,
        eb=jnp.exp(b), ek=jnp.exp(b_last - b), e_last=jnp.exp(b_last),
        eqm=jnp.exp(jnp.minimum(b - b_mid, EXP_CLAMP)), ekm=jnp.exp(jnp.minimum(b_mid - b, EXP_CLAMP)))
    return terms


def _gla_mask(rev):
    t = lax.broadcasted_iota(jnp.int32, (CHUNK, CHUNK), 0)
    s = lax.broadcasted_iota(jnp.int32, (CHUNK, CHUNK), 1)
    return (s > t) if rev else (s <= t)


def _gla_head_ops(tm, hd):
    sl = slice(hd * GLA_DK, (hd + 1) * GLA_DK)
    q_b = (tm["qs"][:, sl] * tm["eb"][:, sl]).astype(BF16)
    k_e = (tm["k"][:, sl] * tm["ek"][:, sl]).astype(BF16)
    q_m = (tm["qs"][:, sl] * tm["eqm"][:, sl]).astype(BF16)
    k_m = (tm["k"][:, sl] * tm["ekm"][:, sl]).astype(BF16)
    return sl, q_b, k_e, q_m, k_m


def _gla_fwd(z_qk, z_v, z_zg, wg, bg, t_rows, name):
    nc = t_rows // CHUNK

    def kern(qf, kf, vf, zf, qb, kb, vb, zb, wg_ref, bg_ref, of_ref, ob_ref, sf_ref, sb_ref, st_f, st_b):
        i = pl.program_id(0)

        @pl.when(i == 0)
        def _():
            st_f[...] = jnp.zeros_like(st_f)
            st_b[...] = jnp.zeros_like(st_b)

        wg_v, bg_v = wg_ref[...], bg_ref[...]
        for rev, (q_r, k_r, v_r, z_r, o_r, s_r, st) in ((False, (qf, kf, vf, zf, of_ref, sf_ref, st_f)),
                                                        (True, (qb, kb, vb, zb, ob_ref, sb_ref, st_b))):
            tm = _gla_chunk_terms(q_r[...], k_r[...], z_r[...], wg_v, bg_v, rev)
            mask = _gla_mask(rev)
            v = v_r[...]
            for hd in range(GLA_H):
                sl, q_b, k_e, q_m, k_m = _gla_head_ops(tm, hd)
                vs = slice(hd * GLA_DV, (hd + 1) * GLA_DV)
                v16 = v[:, vs].astype(BF16)
                a = jnp.where(mask, _dot(q_m, k_m, NT), 0.0).astype(BF16)
                s_in = st[hd]
                s_r[0, hd] = s_in
                o_r[:, vs] = _dot(a, v16) + _dot(q_b, s_in.astype(BF16), NT)
                st[hd] = s_in * tm["e_last"][:, sl] + _dot(v16, k_e, TN)

    fw = lambda c: (lambda i: (i, c))
    rv = lambda c: (lambda i: (nc - 1 - i, c))
    def side(ix):
        return [pl.BlockSpec((CHUNK, QK), ix(0)), pl.BlockSpec((CHUNK, QK), ix(1)),
                pl.BlockSpec((CHUNK, GLA_W), ix(0)), pl.BlockSpec((CHUNK, LANES), ix(0))]
    st_shape = (nc, GLA_H, GLA_DV, GLA_DK)
    return pl.pallas_call(
        kern, name=name, grid=(nc,),
        in_specs=side(fw) + side(rv) + [pl.BlockSpec(wg.shape, lambda i: (0, 0)), pl.BlockSpec(bg.shape, lambda i: (0, 0))],
        out_specs=[pl.BlockSpec((CHUNK, GLA_W), fw(0)), pl.BlockSpec((CHUNK, GLA_W), rv(0)),
                   pl.BlockSpec((1,) + st_shape[1:], lambda i: (i, 0, 0, 0)),
                   pl.BlockSpec((1,) + st_shape[1:], lambda i: (nc - 1 - i, 0, 0, 0))],
        out_shape=[jax.ShapeDtypeStruct((t_rows, GLA_W), F32)] * 2 + [jax.ShapeDtypeStruct(st_shape, F32)] * 2,
        scratch_shapes=[pltpu.VMEM(st_shape[1:], F32)] * 2,
        compiler_params=_cparams(("arbitrary",)),
    )(z_qk, z_qk, z_v, z_zg, z_qk, z_qk, z_v, z_zg, wg, bg)


def _gla_bwd(z_qk, z_v, z_zg, wg, bg, do, s_f, s_b, t_rows, name):
    nc = t_rows // CHUNK

    def kern(qf, kf, vf, zf, dof, ssf, qb, kb, vb, zb, dob, ssb, wg_ref, bg_ref,
             dqf, dkf, dvf, dpf, dqb, dkb, dvb, dpb, ds_f, ds_b):
        i = pl.program_id(0)

        @pl.when(i == 0)
        def _():
            ds_f[...] = jnp.zeros_like(ds_f)
            ds_b[...] = jnp.zeros_like(ds_b)

        wg_v, bg_v = wg_ref[...], bg_ref[...]
        sides = ((False, (qf, kf, vf, zf, dof, ssf, dqf, dkf, dvf, dpf, ds_f)),
                 (True, (qb, kb, vb, zb, dob, ssb, dqb, dkb, dvb, dpb, ds_b)))
        for rev, (q_r, k_r, v_r, z_r, do_r, ss_r, dq_r, dk_r, dv_r, dp_r, ds) in sides:
            tm = _gla_chunk_terms(q_r[...], k_r[...], z_r[...], wg_v, bg_v, rev)
            mask = _gla_mask(rev)
            v = v_r[...]
            d_o = do_r[...]
            db_parts, cross_parts = [], []
            for hd in range(GLA_H):
                sl, q_b, k_e, q_m, k_m = _gla_head_ops(tm, hd)
                vs = slice(hd * GLA_DV, (hd + 1) * GLA_DV)
                v16, do16 = v[:, vs].astype(BF16), d_o[:, vs].astype(BF16)
                a = jnp.where(mask, _dot(q_m, k_m, NT), 0.0).astype(BF16)
                da = jnp.where(mask, _dot(do16, v16, NT), 0.0).astype(BF16)
                s_in = ss_r[0, hd]
                s_in16 = s_in.astype(BF16)
                ds_out = ds[hd]
                ds16 = ds_out.astype(BF16)
                dv_r[:, vs] = _dot(a, do16, TN) + _dot(k_e, ds16, NT)
                dqs = _dot(da, k_m) * tm["eqm"][:, sl] + _dot(do16, s_in16) * tm["eb"][:, sl]
                dk = _dot(da, q_m, TN) * tm["ekm"][:, sl] + _dot(v16, ds16) * tm["ek"][:, sl]
                ds[hd] = ds_out * tm["e_last"][:, sl] + _dot(do16, q_b, TN)
                dq_r[:, sl] = dqs
                dk_r[:, sl] = dk
                db_parts.append(tm["qs"][:, sl] * dqs - tm["k"][:, sl] * dk)
                s_out = s_in * tm["e_last"][:, sl] + _dot(v16, k_e, TN)
                cross_parts.append(jnp.sum(s_out * ds_out, axis=0, keepdims=True))
            d_b = jnp.concatenate(db_parts, axis=1)
            dg = _cumsum_rows(d_b, not rev) + jnp.concatenate(cross_parts, axis=1)
            dp_r[...] = dg * (1.0 / GLA_GATE_NORM) * _sigmoid(-tm["pre"])

    fw = lambda c: (lambda i: (nc - 1 - i, c))
    rv = lambda c: (lambda i: (i, c))
    st_blk = (1, GLA_H, GLA_DV, GLA_DK)
    def side(ix, st_ix):
        return [pl.BlockSpec((CHUNK, QK), ix(0)), pl.BlockSpec((CHUNK, QK), ix(1)),
                pl.BlockSpec((CHUNK, GLA_W), ix(0)), pl.BlockSpec((CHUNK, LANES), ix(0)),
                pl.BlockSpec((CHUNK, GLA_W), ix(0)), pl.BlockSpec(st_blk, st_ix)]
    def oside(ix):
        return [pl.BlockSpec((CHUNK, QK), ix(0)), pl.BlockSpec((CHUNK, QK), ix(0)),
                pl.BlockSpec((CHUNK, GLA_W), ix(0)), pl.BlockSpec((CHUNK, QK), ix(0))]
    o_shapes = [jax.ShapeDtypeStruct((t_rows, QK), F32), jax.ShapeDtypeStruct((t_rows, QK), F32),
                jax.ShapeDtypeStruct((t_rows, GLA_W), F32), jax.ShapeDtypeStruct((t_rows, QK), F32)]
    return pl.pallas_call(
        kern, name=name, grid=(nc,),
        in_specs=(side(fw, lambda i: (nc - 1 - i, 0, 0, 0)) + side(rv, lambda i: (i, 0, 0, 0))
                  + [pl.BlockSpec(wg.shape, lambda i: (0, 0)), pl.BlockSpec(bg.shape, lambda i: (0, 0))]),
        out_specs=oside(fw) + oside(rv), out_shape=o_shapes * 2,
        scratch_shapes=[pltpu.VMEM((GLA_H, GLA_DV, GLA_DK), F32)] * 2,
        compiler_params=_cparams(("arbitrary",)),
    )(z_qk, z_qk, z_v, z_zg, do, s_f, z_qk, z_qk, z_v, z_zg, do, s_b, wg, bg)


FLAT_W = 1024


def _adamw(w, g, m, v):
    rows = w.shape[0]
    tm = 256 if rows % 256 == 0 else SUBLANES
    bc1 = 1.0 - ADAM_B1 ** ADAM_STEP
    bc2 = 1.0 - ADAM_B2 ** ADAM_STEP

    def body(i, ins, consts, outs, accs):
        wv, gv, mv, vv = [r[...] for r in ins]
        m_new = ADAM_B1 * mv + (1.0 - ADAM_B1) * gv
        v_new = ADAM_B2 * vv + (1.0 - ADAM_B2) * (gv * gv)
        outs[0][...] = -ADAM_LR * ((m_new / bc1) / (jnp.sqrt(v_new / bc2) + ADAM_EPS) + ADAM_WD * wv)
        outs[1][...] = m_new
        outs[2][...] = v_new

    return _rows(body, rows, tm, [(x, FLAT_W, 0) for x in (w, g, m, v)], [], [(FLAT_W, F32)] * 3, [], "adamw")


def _add2(a, b, name):
    rows = a.shape[0]
    def body(i, ins, consts, outs, accs):
        outs[0][...] = ins[0][...] + ins[1][...]
    return _rows(body, rows, 256 if rows % 256 == 0 else SUBLANES, [(a, FLAT_W, 0), (b, FLAT_W, 0)], [],
                 [(FLAT_W, F32)], [], name)[0]


def _sum4(x, name):
    rows = x.shape[1]
    tm = 256 if rows % 256 == 0 else SUBLANES

    def kern(x_ref, o_ref):
        o_ref[...] = ((x_ref[0] + x_ref[1]) + x_ref[2]) + x_ref[3]

    return pl.pallas_call(
        kern, name=name, grid=(rows // tm,),
        in_specs=[pl.BlockSpec((4, tm, FLAT_W), lambda i: (0, i, 0))],
        out_specs=pl.BlockSpec((tm, FLAT_W), lambda i: (i, 0)),
        out_shape=jax.ShapeDtypeStruct((rows, FLAT_W), F32),
        compiler_params=_cparams(("arbitrary",)),
    )(x)


def _place():
    x, y, c = lax.axis_index("x"), lax.axis_index("y"), lax.axis_index("c")
    return x, y, c


def _other_chips(x, y):
    return [(1 - x, y), (x, 1 - y), (1 - x, 1 - y)]


def _hbm_call(kern, name, ins, out_shapes, n_sems):
    return pl.pallas_call(
        kern, name=name,
        in_specs=[pl.BlockSpec(memory_space=pl.ANY)] * len(ins),
        out_specs=[pl.BlockSpec(memory_space=pl.ANY)] * len(out_shapes),
        out_shape=out_shapes,
        scratch_shapes=[pltpu.SemaphoreType.DMA((n,)) for n in n_sems],
        compiler_params=pltpu.CompilerParams(has_side_effects=True),
    )(*ins)


def _allgather_chips(x, name):
    rows, cols = x.shape
    half = rows // 2

    def kern(x_ref, o_ref, send1, recv1, send2, recv2, loc):
        px, py, c = _place()
        me = 2 * px + py
        mine = pl.ds(pl.multiple_of(c * half, SUBLANES), half)
        theirs = pl.ds(pl.multiple_of((1 - c) * half, SUBLANES), half)
        own = pltpu.make_async_copy(x_ref, o_ref.at[me], loc.at[0])
        own.start()
        chips = _other_chips(px, py)
        first = [pltpu.make_async_remote_copy(x_ref.at[mine], o_ref.at[me, mine], send1.at[k], recv1.at[k],
                                              device_id=(qx, qy, c), device_id_type=MESH_ID)
                 for k, (qx, qy) in enumerate(chips)]
        for cp in first:
            cp.start()
        passed = []
        for k, (qx, qy) in enumerate(chips):
            slot = o_ref.at[2 * qx + qy, mine]
            pltpu.make_async_remote_copy(slot, slot, send1.at[k], recv1.at[k],
                                         device_id=(qx, qy, c), device_id_type=MESH_ID).wait_recv()
            fwd = pltpu.make_async_remote_copy(slot, slot, send2.at[k], recv2.at[k],
                                               device_id=(px, py, 1 - c), device_id_type=MESH_ID)
            fwd.start()
            passed.append(fwd)
        for k, (qx, qy) in enumerate(chips):
            slot = o_ref.at[2 * qx + qy, theirs]
            pltpu.make_async_remote_copy(slot, slot, send2.at[k], recv2.at[k],
                                         device_id=(px, py, 1 - c), device_id_type=MESH_ID).wait_recv()
        for cp in first + passed:
            cp.wait_send()
        own.wait()

    return _hbm_call(kern, name, [x], [jax.ShapeDtypeStruct((4, rows, cols), x.dtype)], [3, 3, 3, 3, 1])[0]


def _sibling_halves(g, name):
    _, _, rows, cols = g.shape

    def kern(g_ref, mine_ref, got_ref, send, recv, loc):
        px, py, c = _place()
        keep = pltpu.make_async_copy(g_ref.at[:, c], mine_ref, loc.at[0])
        keep.start()
        cp = pltpu.make_async_remote_copy(g_ref.at[:, 1 - c], got_ref, send.at[0], recv.at[0],
                                          device_id=(px, py, 1 - c), device_id_type=MESH_ID)
        cp.start()
        cp.wait()
        keep.wait()

    shp = jax.ShapeDtypeStruct((4, rows, cols), g.dtype)
    return _hbm_call(kern, name, [g], [shp, shp], [1, 1, 1])


def _chip_exchange(p, name):
    _, rows, cols = p.shape

    def kern(p_ref, o_ref, send, recv, loc):
        px, py, c = _place()
        me = 2 * px + py
        own = pltpu.make_async_copy(p_ref.at[me], o_ref.at[me], loc.at[0])
        own.start()
        cps = [pltpu.make_async_remote_copy(p_ref.at[2 * qx + qy], o_ref.at[me], send.at[k], recv.at[k],
                                            device_id=(qx, qy, c), device_id_type=MESH_ID)
               for k, (qx, qy) in enumerate(_other_chips(px, py))]
        for cp in cps:
            cp.start()
        for cp in cps:
            cp.wait()
        own.wait()

    return _hbm_call(kern, name, [p], [jax.ShapeDtypeStruct(p.shape, p.dtype)], [3, 3, 1])[0]


def _sibling_join(r, name):
    rows, cols = r.shape

    def kern(r_ref, o_ref, send, recv, loc):
        px, py, c = _place()
        own = pltpu.make_async_copy(r_ref, o_ref.at[c], loc.at[0])
        own.start()
        cp = pltpu.make_async_remote_copy(r_ref, o_ref.at[c], send.at[0], recv.at[0],
                                          device_id=(px, py, 1 - c), device_id_type=MESH_ID)
        cp.start()
        cp.wait()
        own.wait()

    return _hbm_call(kern, name, [r], [jax.ShapeDtypeStruct((2, rows, cols), r.dtype)], [1, 1, 1])[0]


SHARDED = {"meta_tokens": 1, "w_in": 2, "conv_w": 2, "gla_wg_f": 2, "gla_wg_b": 2, "w_out": 1, "w_mlp_up": 2,
           "w_mlp_down": 1}
BIG = ("w_in", "w_out", "w_mlp_up", "w_mlp_down")
WEIGHTS = ["meta_tokens", "norm_mix_pre", "norm_mix_post", "norm_mlp_pre", "norm_mlp_post", "w_in", "conv_w",
           "conv_b", "lru_wa_f", "lru_ba_f", "lru_wx_f", "lru_bx_f", "lru_lambda_f", "lru_wa_b", "lru_ba_b",
           "lru_wx_b", "lru_bx_b", "lru_lambda_b", "gla_wg_f", "gla_bg_f", "gla_wg_b", "gla_bg_b", "gla_head_norm",
           "w_out", "w_mlp_up", "w_mlp_down"]
N_CHIPS = 4


def _pad_to(v, n):
    return jnp.pad(v, (0, n - v.shape[0]))


def _round_up(n, m):
    return -(-n // m) * m


def _flat_cat(arrs, total):
    return _pad_to(jnp.concatenate([a.reshape(-1) for a in arrs]), total)


def _split_flat(flat, shapes):
    out, off = [], 0
    for s in shapes:
        n = math.prod(s)
        out.append(flat[off:off + n].reshape(s))
        off += n
    return out


def _shard_of(full, axis, q):
    n = full.shape[axis] // N_CHIPS
    return lax.slice_in_dim(full, q * n, (q + 1) * n, axis=axis)


def _block_diag_gates(wa_f, wx_f, wa_b, wx_b):
    def bd(w):
        z = jnp.zeros((LRU_HD, LRU_HD), w.dtype)
        return jnp.stack([jnp.block([[w[2 * j], z], [z, w[2 * j + 1]]]) for j in range(4)])
    return jnp.concatenate([bd(wa_f), bd(wx_f), bd(wa_b), bd(wx_b)], axis=2).astype(BF16)


def _gate_diag_blocks(dwg, o):
    blk = dwg[:, :, o * LANES:(o + 1) * LANES]
    return jnp.stack([blk[j, hh * LRU_HD:(hh + 1) * LRU_HD, hh * LRU_HD:(hh + 1) * LRU_HD]
                      for j in range(4) for hh in range(2)])


def kernel(x, meta_tokens, norm_mix_pre, norm_mix_post, norm_mlp_pre, norm_mlp_post, w_in, conv_w, conv_b, lru_wa_f, lru_ba_f, lru_wx_f, lru_bx_f, lru_lambda_f, lru_wa_b, lru_ba_b, lru_wx_b, lru_bx_b, lru_lambda_b, gla_wg_f, gla_bg_f, gla_wg_b, gla_bg_b, gla_head_norm, w_out, w_mlp_up, w_mlp_down, loss_target, m_meta_tokens, m_norm_mix_pre, m_norm_mix_post, m_norm_mlp_pre, m_norm_mlp_post, m_w_in, m_conv_w, m_conv_b, m_lru_wa_f, m_lru_ba_f, m_lru_wx_f, m_lru_bx_f, m_lru_lambda_f, m_lru_wa_b, m_lru_ba_b, m_lru_wx_b, m_lru_bx_b, m_lru_lambda_b, m_gla_wg_f, m_gla_bg_f, m_gla_wg_b, m_gla_bg_b, m_gla_head_norm, m_w_out, m_w_mlp_up, m_w_mlp_down, v_meta_tokens, v_norm_mix_pre, v_norm_mix_post, v_norm_mlp_pre, v_norm_mlp_post, v_w_in, v_conv_w, v_conv_b, v_lru_wa_f, v_lru_ba_f, v_lru_wx_f, v_lru_bx_f, v_lru_lambda_f, v_lru_wa_b, v_lru_ba_b, v_lru_wx_b, v_lru_bx_b, v_lru_lambda_b, v_gla_wg_f, v_gla_bg_f, v_gla_wg_b, v_gla_bg_b, v_gla_head_norm, v_w_out, v_w_mlp_up, v_w_mlp_down):
    args = dict(locals())
    w_loc = {n: args[n] for n in WEIGHTS}
    m_loc = {n: args["m_" + n] for n in WEIGHTS}
    v_loc = {n: args["v_" + n] for n in WEIGHTS}
    seq = x.shape[1]
    t_rows = FRONT + seq + BACK
    assert t_rows % 384 == 0 and seq % FRONT == 0, seq

    big_sh = [w_loc[n] for n in BIG]
    small_names = [n for n in SHARDED if n not in BIG]
    small_sh = [w_loc[n] for n in small_names]
    small_bits = [lax.bitcast_convert_type(a, BF16) for a in small_sh]
    pieces = [a.astype(BF16) for a in big_sh] + small_bits
    n_el = sum(p.size for p in pieces)
    g_rows = _round_up(n_el, FLAT_W * 32) // FLAT_W
    packed = _flat_cat(pieces, g_rows * FLAT_W).reshape(g_rows, FLAT_W)
    gathered = _allgather_chips(packed, "gather_weights").reshape(N_CHIPS, -1)
    per_chip = [_split_flat(gathered[q], [p.shape for p in pieces]) for q in range(N_CHIPS)]
    full = {}
    for idx, n in enumerate(list(BIG) + small_names):
        parts = [per_chip[q][idx] for q in range(N_CHIPS)]
        if n not in BIG:
            parts = [lax.bitcast_convert_type(p, F32) for p in parts]
        full[n] = jnp.concatenate(parts, axis=SHARDED[n])
    for n in WEIGHTS:
        if n not in SHARDED:
            full[n] = w_loc[n]

    meta_blk = jnp.concatenate([jnp.zeros((FRONT - N_META, D_MODEL), F32), full["meta_tokens"]], axis=0)
    h = jnp.concatenate([meta_blk, x[0], jnp.zeros((BACK, D_MODEL), F32)], axis=0)
    target = loss_target[0]

    layer_w = []
    for l in range(DEPTH):
        w_in_p = jnp.pad(full["w_in"][l], ((0, 0), (0, D_IN_PAD - D_IN)))
        wg = jnp.zeros((LANES, 2 * QK), F32)
        wg = wg.at[0:GLA_RANK, 0:QK].set(full["gla_wg_f"][l]).at[GLA_RANK:2 * GLA_RANK, QK:].set(full["gla_wg_b"][l])
        pb = jnp.stack([full["lru_ba_f"][l], full["lru_bx_f"][l], full["lru_ba_b"][l], full["lru_bx_b"][l],
                        full["lru_lambda_f"][l], full["lru_lambda_b"][l], jnp.zeros((LRU_W,), F32),
                        jnp.zeros((LRU_W,), F32)])
        layer_w.append(dict(
            w_in=w_in_p, w_out=full["w_out"][l], w_up=full["w_mlp_up"][l], w_down=full["w_mlp_down"][l],
            g1=full["norm_mix_pre"][l][None], g2=full["norm_mix_post"][l][None],
            g3=full["norm_mlp_pre"][l][None], g4=full["norm_mlp_post"][l][None],
            conv_w=full["conv_w"][l], conv_b=full["conv_b"][l][None],
            lru_wg=_block_diag_gates(full["lru_wa_f"][l], full["lru_wx_f"][l], full["lru_wa_b"][l], full["lru_wx_b"][l]),
            lru_pb=pb, gla_wg=wg.astype(BF16),
            gla_bg=jnp.concatenate([full["gla_bg_f"][l], full["gla_bg_b"][l]])[None],
            head_norm=full["gla_head_norm"][l][None]))

    def split_z(acc):
        return (acc[:, 0:1024], acc[:, 1024:1536], acc[:, 1536:2048], acc[:, 2048:2560], acc[:, ZG_OFF:ZG_OFF + LANES])

    def relu2(acc):
        r = jnp.maximum(acc, 0.0)
        return acc, r * r

    saved = []
    for l in range(DEPTH):
        lw = layer_w[l]
        tag = f"_l{l}"
        n1 = _norm_cast(h, lw["g1"], t_rows, "norm_mix_pre" + tag)
        z_lru, z_qk, z_v, z_go, z_zg = _mm(
            n1, lw["w_in"], "nn", 384, D_IN_PAD, D_MODEL,
            [(1024, F32), (512, F32), (512, F32), (512, F32), (LANES, F32)], "in_proj" + tag, epilogue=split_z)
        h_f, h_b = _lru_fwd(z_lru, lw["conv_w"], lw["conv_b"], lw["lru_wg"], lw["lru_pb"], t_rows, seq, "lru_fwd" + tag)
        o_f, o_b, s_f, s_b = _gla_fwd(z_qk, z_v, z_zg, lw["gla_wg"], lw["gla_bg"], t_rows, "gla_fwd" + tag)
        ymix = _mix_fwd(h_f, h_b, z_lru, o_f, o_b, z_go, lw["head_norm"], t_rows, "mix_fwd" + tag)
        mix = _mm(ymix, lw["w_out"], "nn", 768, D_MODEL, D_MODEL, [(D_MODEL, F32)], "out_proj" + tag)[0]
        h1 = _norm_residual(mix, lw["g2"], h, t_rows, "norm_mix_post" + tag)
        n3 = _norm_cast(h1, lw["g3"], t_rows, "norm_mlp_pre" + tag)
        u, act = _mm(n3, lw["w_up"], "nn", 768, 1024, D_MODEL, [(D_FF, F32), (D_FF, BF16)], "mlp_up" + tag,
                     epilogue=relu2)
        ff = _mm(act, lw["w_down"], "nn", 768, D_MODEL, 1024, [(D_MODEL, F32)], "mlp_down" + tag)[0]
        h2 = _norm_residual(ff, lw["g4"], h1, t_rows, "norm_mlp_post" + tag)
        saved.append(dict(h=h, n1=n1, z_lru=z_lru, z_qk=z_qk, z_v=z_v, z_go=z_go, z_zg=z_zg, h_f=h_f, h_b=h_b,
                          o_f=o_f, o_b=o_b, s_f=s_f, s_b=s_b, ymix=ymix, mix=mix, h1=h1, n3=n3, u=u, act=act, ff=ff))
        h = h2

    dh, loss_part = _loss_and_grad(h, target, t_rows, seq)
    loss = lax.psum(loss_part[0, 0], ("x", "y", "c"))

    grads = {n: [None] * DEPTH for n in WEIGHTS if n != "meta_tokens"}
    for l in reversed(range(DEPTH)):
        lw, sv = layer_w[l], saved[l]
        tag = f"_l{l}"
        dff, dg4 = _norm_residual_bwd(dh, sv["ff"], lw["g4"], t_rows, "norm_mlp_post_bwd" + tag)
        grads["norm_mlp_post"][l] = dg4[0]
        grads["w_mlp_down"][l] = _mm(sv["act"], dff, "tn", 1024, D_MODEL, 768, [(D_MODEL, F32)], "dw_down" + tag)[0]
        du = _mm(dff, lw["w_down"], "nt", 768, 1024, D_MODEL, [(D_FF, BF16)], "d_act" + tag,
                 epilogue=lambda acc, uu: (acc * 2.0 * jnp.maximum(uu, 0.0),), extras=(sv["u"],))[0]
        grads["w_mlp_up"][l] = _mm(sv["n3"], du, "tn", D_MODEL, 1024, 768, [(D_FF, F32)], "dw_up" + tag)[0]
        dn3 = _mm(du, lw["w_up"], "nt", 768, D_MODEL, 1024, [(D_MODEL, F32)], "d_n3" + tag)[0]
        dh1, dg3 = _norm_bwd_add(dn3, sv["h1"], lw["g3"], dh, t_rows, "norm_mlp_pre_bwd" + tag)
        grads["norm_mlp_pre"][l] = dg3[0]
        dmix, dg2 = _norm_residual_bwd(dh1, sv["mix"], lw["g2"], t_rows, "norm_mix_post_bwd" + tag)
        grads["norm_mix_post"][l] = dg2[0]
        grads["w_out"][l] = _mm(sv["ymix"], dmix, "tn", D_MODEL, D_MODEL, 768, [(D_MODEL, F32)], "dw_out" + tag)[0]
        dymix = _mm(dmix, lw["w_out"], "nt", 768, D_MODEL, D_MODEL, [(D_MODEL, F32)], "d_ymix" + tag)[0]
        dh_lru, dgate, do, dgo, dhn = _mix_bwd(dymix, sv["h_f"], sv["h_b"], sv["z_lru"], sv["o_f"], sv["o_b"],
                                               sv["z_go"], lw["head_norm"], t_rows, "mix_bwd" + tag)
        grads["gla_head_norm"][l] = dhn[0]
        gla_grads = _gla_bwd(sv["z_qk"], sv["z_v"], sv["z_zg"], lw["gla_wg"], lw["gla_bg"], do, sv["s_f"], sv["s_b"],
                             t_rows, "gla_bwd" + tag)
        dx_br, dcw, dcb, dwg_lru, dpb = _lru_bwd(sv["z_lru"], lw["conv_w"], lw["conv_b"], lw["lru_wg"], lw["lru_pb"],
                                                 sv["h_f"], sv["h_b"], dh_lru, t_rows, seq, "lru_bwd" + tag)
        grads["conv_w"][l], grads["conv_b"][l] = dcw, dcb[0]
        for o, n in enumerate(("lru_wa_f", "lru_wx_f", "lru_wa_b", "lru_wx_b")):
            grads[n][l] = _gate_diag_blocks(dwg_lru, o)
        for row, n in enumerate(("lru_ba_f", "lru_bx_f", "lru_ba_b", "lru_bx_b", "lru_lambda_f", "lru_lambda_b")):
            grads[n][l] = dpb[row]
        dz, dwg_gla, dbg = _dz_assemble(dx_br, dgate, gla_grads, dgo, sv["z_zg"], lw["gla_wg"], t_rows,
                                        "dz_assemble" + tag)
        grads["gla_wg_f"][l] = dwg_gla[0:GLA_RANK, 0:QK]
        grads["gla_wg_b"][l] = dwg_gla[GLA_RANK:2 * GLA_RANK, QK:]
        grads["gla_bg_f"][l], grads["gla_bg_b"][l] = dbg[0, 0:QK], dbg[0, QK:]
        grads["w_in"][l] = _mm(sv["n1"], dz, "tn", D_MODEL, 896, 768, [(D_IN_PAD, F32)], "dw_in" + tag)[0][:, :D_IN]
        dn1 = _mm(dz, lw["w_in"], "nt", 768, D_MODEL, 896, [(D_MODEL, F32)], "d_n1" + tag)[0]
        dh, dg1 = _norm_bwd_add(dn1, sv["h"], lw["g1"], dh1, t_rows, "norm_mix_pre_bwd" + tag)
        grads["norm_mix_pre"][l] = dg1[0]

    grad_x = dh[FRONT:FRONT + seq][None]
    g_full = {n: jnp.stack(v) for n, v in grads.items()}
    g_full["meta_tokens"] = dh[FRONT - N_META:FRONT]

    sh_names = [n for n in WEIGHTS if n in SHARDED]
    rep_names = [n for n in WEIGHTS if n not in SHARDED]
    n_rep = sum(w_loc[n].size for n in rep_names)
    rep_q = _round_up(n_rep, N_CHIPS * FLAT_W) // N_CHIPS
    rep_flat = _flat_cat([g_full[n] for n in rep_names], N_CHIPS * rep_q)
    n_sh = sum(w_loc[n].size for n in sh_names)
    half_rows = _round_up(n_sh + rep_q, 2 * FLAT_W * 256) // (2 * FLAT_W)
    shard_len = 2 * half_rows * FLAT_W
    g_pack = jnp.stack([
        _flat_cat([_shard_of(g_full[n], SHARDED[n], q) for n in sh_names] + [rep_flat[q * rep_q:(q + 1) * rep_q]],
                  shard_len) for q in range(N_CHIPS)]).reshape(N_CHIPS, 2, half_rows, FLAT_W)
    mine, got = _sibling_halves(g_pack, "reduce_sibling_halves")
    chip_part = _add2(mine.reshape(-1, FLAT_W), got.reshape(-1, FLAT_W), "reduce_add_sibling")
    landed = _chip_exchange(chip_part.reshape(N_CHIPS, half_rows, FLAT_W), "reduce_chip_exchange")
    my_half = _sum4(landed, "reduce_sum_chips")
    total = _sibling_join(my_half, "reduce_sibling_join").reshape(-1)
    g_sh = _split_flat(total, [w_loc[n].shape for n in sh_names])
    rep_rows = _round_up(rep_q, FLAT_W * 16) // FLAT_W
    rep_mine = _pad_to(total[n_sh:n_sh + rep_q], rep_rows * FLAT_W).reshape(rep_rows, FLAT_W)
    rep_all = _allgather_chips(rep_mine, "gather_small_grads").reshape(N_CHIPS, -1)[:, :rep_q].reshape(-1)
    g_rep = _split_flat(rep_all, [w_loc[n].shape for n in rep_names])
    g_tot = dict(zip(sh_names, g_sh))
    g_tot.update(dict(zip(rep_names, g_rep)))

    a_rows = _round_up(n_sh + n_rep, FLAT_W * 256) // FLAT_W
    order = sh_names + rep_names
    def pack(d):
        return _flat_cat([d[n] for n in order], a_rows * FLAT_W).reshape(a_rows, FLAT_W)
    delta_f, m_f, v_f = _adamw(pack(w_loc), pack(g_tot), pack(m_loc), pack(v_loc))
    shapes = [w_loc[n].shape for n in order]
    delta = dict(zip(order, _split_flat(delta_f.reshape(-1), shapes)))
    new_m = dict(zip(order, _split_flat(m_f.reshape(-1), shapes)))
    new_v = dict(zip(order, _split_flat(v_f.reshape(-1), shapes)))
    return (loss, grad_x, *[g_tot[n] for n in WEIGHTS], *[delta[n] for n in WEIGHTS],
            *[new_m[n] for n in WEIGHTS], *[new_v[n] for n in WEIGHTS])
```

```python
import math

import jax
import jax.numpy as jnp
from jax import lax
from jax.experimental import pallas as pl
from jax.experimental.pallas import tpu as pltpu

F32 = jnp.float32
BF16 = jnp.bfloat16

D_MODEL = 1024
DEPTH = 2
N_META = 16
LRU_W = 512
LRU_HEADS = 8
LRU_HD = 64
LRU_C = 8.0
GLA_W = 512
GLA_H = 4
GLA_DK = 64
GLA_DV = 128
GLA_RANK = 16
GLA_GATE_NORM = 16.0
D_FF = 4096
D_IN = 2592
EPS = 1e-6
ADAM_LR, ADAM_B1, ADAM_B2, ADAM_EPS, ADAM_WD, ADAM_STEP = 0.001, 0.9, 0.999, 1e-08, 0.01, 10

LANES = 128
SUBLANES = 8
FRONT = 128
BACK = 128
D_IN_PAD = 2688
ZG_OFF = 2560
CHUNK = 64
EXP_CLAMP = 80.0
VMEM_LIMIT = 56 * 1024 * 1024
MESH_ID = pl.DeviceIdType.MESH

NN = (((1,), (0,)), ((), ()))
NT = (((1,), (1,)), ((), ()))
TN = (((0,), (0,)), ((), ()))


def _dot(a, b, dims=NN):
    return lax.dot_general(a, b, dims, preferred_element_type=F32)


def _cparams(sem):
    return pltpu.CompilerParams(dimension_semantics=sem, vmem_limit_bytes=VMEM_LIMIT)


def _sigmoid(x):
    return 1.0 / (1.0 + jnp.exp(-x))


def _gelu(x):
    c = math.sqrt(2.0 / math.pi)
    return 0.5 * x * (1.0 + jnp.tanh(c * (x + 0.044715 * x * x * x)))


def _gelu_grad(x):
    c = math.sqrt(2.0 / math.pi)
    t = jnp.tanh(c * (x + 0.044715 * x * x * x))
    return 0.5 * (1.0 + t) + 0.5 * x * (1.0 - t * t) * c * (1.0 + 3.0 * 0.044715 * x * x)


def _rms_fwd(x, g):
    rstd = lax.rsqrt(jnp.mean(x * x, axis=1, keepdims=True) + EPS)
    return (x * rstd) * g


def _rms_bwd(x, g, dy):
    rstd = lax.rsqrt(jnp.mean(x * x, axis=1, keepdims=True) + EPS)
    xh = x * rstd
    dxh = dy * g
    dx = rstd * (dxh - xh * jnp.mean(dxh * xh, axis=1, keepdims=True))
    dg = jnp.sum(dy * xh, axis=0, keepdims=True)
    return dx, dg


def _mm(a, b, mode, tm, tn, tk, outs, name, epilogue=None, extras=(), col_blocks_first=False):
    if mode == "nn":
        (m, k), n = a.shape, b.shape[1]
        a_spec = pl.BlockSpec((tm, tk), lambda i, j, kk: (i, kk))
        b_spec = pl.BlockSpec((tk, tn), lambda i, j, kk: (kk, j))
        dims = NN
    elif mode == "nt":
        (m, k), n = a.shape, b.shape[0]
        a_spec = pl.BlockSpec((tm, tk), lambda i, j, kk: (i, kk))
        b_spec = pl.BlockSpec((tn, tk), lambda i, j, kk: (j, kk))
        dims = NT
    else:
        (k, m), n = a.shape, b.shape[1]
        a_spec = pl.BlockSpec((tk, tm), lambda i, j, kk: (kk, i))
        b_spec = pl.BlockSpec((tk, tn), lambda i, j, kk: (kk, j))
        dims = TN
    assert m % tm == 0 and n % tn == 0 and k % tk == 0, (name, m, n, k)
    nk = k // tk
    ne = len(extras)
    e_specs = [pl.BlockSpec((tm, tn), lambda i, j, kk: (i, j)) for _ in extras]
    o_specs, o_shapes = [], []
    for width, dtype in outs:
        if col_blocks_first:
            assert width == n, name
            o_specs.append(pl.BlockSpec((None, tm, tn), lambda i, j, kk: (j, i, 0)))
            o_shapes.append(jax.ShapeDtypeStruct((n // tn, m, tn), dtype))
            continue
        if width == n:
            o_specs.append(pl.BlockSpec((tm, tn), lambda i, j, kk: (i, j)))
        else:
            assert n == tn, name
            o_specs.append(pl.BlockSpec((tm, width), lambda i, j, kk: (i, 0)))
        o_shapes.append(jax.ShapeDtypeStruct((m, width), dtype))

    def finish(acc, extra_refs, out_refs):
        res = epilogue(acc, *[e[...] for e in extra_refs]) if epilogue else (acc,)
        for o, r in zip(out_refs, res):
            o[...] = r.astype(o.dtype)

    if nk == 1:
        def body(a_ref, b_ref, *rest):
            finish(_dot(a_ref[...], b_ref[...], dims), rest[:ne], rest[ne:])
        scratch = []
    else:
        def body(a_ref, b_ref, *rest):
            acc_ref = rest[-1]
            kk = pl.program_id(2)

            @pl.when(kk == 0)
            def _():
                acc_ref[...] = jnp.zeros_like(acc_ref)

            acc_ref[...] += _dot(a_ref[...], b_ref[...], dims)

            @pl.when(kk == nk - 1)
            def _():
                finish(acc_ref[...], rest[:ne], rest[ne:-1])
        scratch = [pltpu.VMEM((tm, tn), F32)]

    return pl.pallas_call(
        body, name=name, grid=(m // tm, n // tn, nk),
        in_specs=[a_spec, b_spec] + e_specs, out_specs=o_specs, out_shape=o_shapes,
        scratch_shapes=scratch,
        compiler_params=_cparams(("parallel", "parallel", "arbitrary")),
    )(a, b, *extras)


def _rows(body, n_rows, tm, ins, consts, outs, accs, name):
    assert n_rows % tm == 0, (name, n_rows, tm)
    in_specs = [pl.BlockSpec((tm, w), (lambda i, cb=cb: (i, cb))) for _, w, cb in ins]
    in_specs += [pl.BlockSpec(c.shape, lambda i: (0, 0)) for c in consts]
    out_specs = [pl.BlockSpec((tm, w), lambda i: (i, 0)) for w, _ in outs]
    out_specs += [pl.BlockSpec(s, lambda i: (0, 0)) for s, _ in accs]
    out_shape = [jax.ShapeDtypeStruct((n_rows, w), d) for w, d in outs]
    out_shape += [jax.ShapeDtypeStruct(s, d) for s, d in accs]
    ni, nc, no = len(ins), len(consts), len(outs)

    def kern(*refs):
        body(pl.program_id(0), refs[:ni], refs[ni:ni + nc], refs[ni + nc:ni + nc + no], refs[ni + nc + no:])

    res = pl.pallas_call(
        kern, name=name, grid=(n_rows // tm,), in_specs=in_specs, out_specs=out_specs, out_shape=out_shape,
        compiler_params=_cparams(("arbitrary",)),
    )(*[a for a, _, _ in ins], *consts)
    return res


def _acc_add(i, ref, val):
    @pl.when(i == 0)
    def _():
        ref[...] = jnp.zeros_like(ref)

    ref[...] += val


def _norm_cast(h, g, t_rows, name):
    def body(i, ins, consts, outs, accs):
        outs[0][...] = _rms_fwd(ins[0][...], consts[0][...]).astype(BF16)
    return _rows(body, t_rows, 384, [(h, D_MODEL, 0)], [g], [(D_MODEL, BF16)], [], name)[0]


def _norm_residual(y, g, h, t_rows, name):
    def body(i, ins, consts, outs, accs):
        outs[0][...] = ins[1][...] + _rms_fwd(ins[0][...], consts[0][...])
    return _rows(body, t_rows, 384, [(y, D_MODEL, 0), (h, D_MODEL, 0)], [g], [(D_MODEL, F32)], [], name)[0]


def _norm_residual_bwd(dh, y, g, t_rows, name):
    def body(i, ins, consts, outs, accs):
        dy, dg = _rms_bwd(ins[1][...], consts[0][...], ins[0][...])
        outs[0][...] = dy.astype(BF16)
        _acc_add(i, accs[0], dg)
    return _rows(body, t_rows, 384, [(dh, D_MODEL, 0), (y, D_MODEL, 0)], [g],
                 [(D_MODEL, BF16)], [((1, D_MODEL), F32)], name)


def _norm_bwd_add(dn, x, g, dh, t_rows, name):
    def body(i, ins, consts, outs, accs):
        dx, dg = _rms_bwd(ins[1][...], consts[0][...], ins[0][...])
        outs[0][...] = ins[2][...] + dx
        _acc_add(i, accs[0], dg)
    return _rows(body, t_rows, 384, [(dn, D_MODEL, 0), (x, D_MODEL, 0), (dh, D_MODEL, 0)], [g],
                 [(D_MODEL, F32)], [((1, D_MODEL), F32)], name)


def _head_norm_parts(o):
    ns, rs = [], []
    for hd in range(GLA_H):
        oh = o[:, hd * GLA_DV:(hd + 1) * GLA_DV]
        r = lax.rsqrt(jnp.mean(oh * oh, axis=1, keepdims=True) + EPS)
        ns.append(oh * r)
        rs.append(r)
    return ns, rs


def _mix_fwd(h_f, h_b, z_lru, o_f, o_b, z_go, head_norm, t_rows, name):
    def body(i, ins, consts, outs, accs):
        hf, hb, gate, of, ob, go = [r[...] for r in ins]
        hn = consts[0][...]
        outs[0][:, 0:LRU_W] = ((hf + hb) * _gelu(gate)).astype(BF16)
        ns, _ = _head_norm_parts(of + ob)
        silu = go * _sigmoid(go)
        for hd in range(GLA_H):
            sl = slice(hd * GLA_DV, (hd + 1) * GLA_DV)
            outs[0][:, LRU_W + hd * GLA_DV:LRU_W + (hd + 1) * GLA_DV] = (ns[hd] * hn[:, sl] * silu[:, sl]).astype(BF16)
    ins = [(h_f, LRU_W, 0), (h_b, LRU_W, 0), (z_lru, LRU_W, 1), (o_f, GLA_W, 0), (o_b, GLA_W, 0), (z_go, GLA_W, 0)]
    return _rows(body, t_rows, 384, ins, [head_norm], [(D_MODEL, BF16)], [], name)[0]


def _mix_bwd(dymix, h_f, h_b, z_lru, o_f, o_b, z_go, head_norm, t_rows, name):
    def body(i, ins, consts, outs, accs):
        dyl, dyg, hf, hb, gate, of, ob, go = [r[...] for r in ins]
        hn = consts[0][...]
        outs[0][...] = dyl * _gelu(gate)
        outs[1][...] = dyl * (hf + hb) * _gelu_grad(gate)
        ns, rs = _head_norm_parts(of + ob)
        sg = _sigmoid(go)
        silu = go * sg
        dsilu = sg * (1.0 + go * (1.0 - sg))
        dhn = []
        for hd in range(GLA_H):
            sl = slice(hd * GLA_DV, (hd + 1) * GLA_DV)
            dy = dyg[:, sl]
            e = dy * hn[:, sl] * silu[:, sl]
            outs[2][:, sl] = rs[hd] * (e - ns[hd] * jnp.mean(e * ns[hd], axis=1, keepdims=True))
            outs[3][:, sl] = dy * ns[hd] * hn[:, sl] * dsilu[:, sl]
            dhn.append(jnp.sum(dy * ns[hd] * silu[:, sl], axis=0, keepdims=True))
        _acc_add(i, accs[0], jnp.concatenate(dhn, axis=1))
    ins = [(dymix, LRU_W, 0), (dymix, GLA_W, 1), (h_f, LRU_W, 0), (h_b, LRU_W, 0), (z_lru, LRU_W, 1),
           (o_f, GLA_W, 0), (o_b, GLA_W, 0), (z_go, GLA_W, 0)]
    return _rows(body, t_rows, 384, ins, [head_norm],
                 [(LRU_W, F32), (LRU_W, F32), (GLA_W, F32), (GLA_W, F32)], [((1, GLA_W), F32)], name)


def _dz_assemble(dx_br, dgate, gla_grads, dgo, z_zg, wg, t_rows, name):
    dq_f, dk_f, dv_f, dpre_f, dq_b, dk_b, dv_b, dpre_b = gla_grads
    qk = GLA_H * GLA_DK

    def body(i, ins, consts, outs, accs):
        (dxb, dgt, dqf, dqb, dkf, dkb, dvf, dvb, dg_o, dpf, dpb, zg) = [r[...] for r in ins]
        w = consts[0][...]
        o = outs[0]
        o[:, 0:LRU_W] = dxb.astype(BF16)
        o[:, LRU_W:2 * LRU_W] = dgt.astype(BF16)
        o[:, 1024:1024 + qk] = ((dqf + dqb) * (GLA_DK ** -0.5)).astype(BF16)
        o[:, 1280:1280 + qk] = (dkf + dkb).astype(BF16)
        o[:, 1536:1536 + GLA_W] = (dvf + dvb).astype(BF16)
        o[:, 2048:2048 + GLA_W] = dg_o.astype(BF16)
        dpre = jnp.concatenate([dpf, dpb], axis=1)
        dpre16 = dpre.astype(BF16)
        o[:, ZG_OFF:ZG_OFF + LANES] = _dot(dpre16, w, NT).astype(BF16)
        _acc_add(i, accs[0], _dot(zg.astype(BF16), dpre16, TN))
        _acc_add(i, accs[1], jnp.sum(dpre, axis=0, keepdims=True))

    ins = [(dx_br, LRU_W, 0), (dgate, LRU_W, 0), (dq_f, qk, 0), (dq_b, qk, 0), (dk_f, qk, 0), (dk_b, qk, 0),
           (dv_f, GLA_W, 0), (dv_b, GLA_W, 0), (dgo, GLA_W, 0), (dpre_f, qk, 0), (dpre_b, qk, 0), (z_zg, LANES, 0)]
    return _rows(body, t_rows, 384, ins, [wg], [(D_IN_PAD, BF16)],
                 [((LANES, 2 * qk), F32), ((1, 2 * qk), F32)], name)


def _loss_and_grad(h, target, t_rows, seq):
    tm = FRONT
    first, n_t = FRONT // tm, seq // tm

    def kern(h_ref, t_ref, dh_ref, loss_ref):
        i = pl.program_id(0)
        valid = jnp.logical_and(i >= first, i < first + n_t)

        @pl.when(i == 0)
        def _():
            loss_ref[...] = jnp.zeros_like(loss_ref)

        @pl.when(valid)
        def _():
            err = h_ref[...] - t_ref[...]
            dh_ref[...] = err * (1.0 / D_MODEL)
            loss_ref[...] += jnp.sum(err * err) * (0.5 / D_MODEL)

        @pl.when(jnp.logical_not(valid))
        def _():
            dh_ref[...] = jnp.zeros_like(dh_ref)

    return pl.pallas_call(
        kern, name="loss", grid=(t_rows // tm,),
        in_specs=[pl.BlockSpec((tm, D_MODEL), lambda i: (i, 0)),
                  pl.BlockSpec((tm, D_MODEL), lambda i: (jnp.clip(i - first, 0, n_t - 1), 0))],
        out_specs=[pl.BlockSpec((tm, D_MODEL), lambda i: (i, 0)), pl.BlockSpec((SUBLANES, LANES), lambda i: (0, 0))],
        out_shape=[jax.ShapeDtypeStruct((t_rows, D_MODEL), F32), jax.ShapeDtypeStruct((SUBLANES, LANES), F32)],
        compiler_params=_cparams(("arbitrary",)),
    )(h, target)


LRU_RB = 64
HALO = SUBLANES


def _scan8(a, u, rev):
    rows = lax.broadcasted_iota(jnp.int32, a.shape, 0)
    for d in (1, 2, 4):
        if rev:
            a_s, u_s, ok = pltpu.roll(a, SUBLANES - d, 0), pltpu.roll(u, SUBLANES - d, 0), rows < SUBLANES - d
        else:
            a_s, u_s, ok = pltpu.roll(a, d, 0), pltpu.roll(u, d, 0), rows >= d
        u = jnp.where(ok, a * u_s + u, u)
        a = jnp.where(ok, a * a_s, a)
    return a, u


def _edge_row(x, rev):
    r = x[0:1, :] if rev else x[SUBLANES - 1:SUBLANES, :]
    return jnp.broadcast_to(r, x.shape)


def _lru_gates(xc, pre_a, pre_x, ba, bx, sp8):
    r = _sigmoid(pre_a + ba)
    i = _sigmoid(pre_x + bx)
    log_a = -(r * sp8)
    a = jnp.exp(log_a)
    y = 2.0 * log_a
    series = -y * (1.0 + y * (0.5 + y * (1.0 / 6.0 + y * (1.0 / 24.0 + y * (1.0 / 120.0)))))
    om = jnp.where(y > -0.25, series, 1.0 - a * a)
    s = jnp.sqrt(om)
    return r, i, a, s


def _softplus(x):
    return jnp.maximum(x, 0.0) + jnp.log(1.0 + jnp.exp(-jnp.abs(x)))


def _conv_taps(xpad_ref, r0, nrows, shifts=(-2, -1, 0, 1)):
    ext = xpad_ref[pl.ds(r0, nrows + 2 * HALO), :]
    taps = []
    for s in shifts:
        sh = ext if s == 0 else pltpu.roll(ext, (-s) % (nrows + 2 * HALO), 0)
        taps.append(sh[HALO:HALO + nrows])
    return taps


def _row_mask(r0, nrows, seq):
    rows = r0 + lax.broadcasted_iota(jnp.int32, (nrows, LANES), 0)
    return jnp.logical_and(rows >= FRONT - N_META, rows < FRONT + seq).astype(F32)


def _lru_load_conv(z_hbm, xpad, xc, sem, cw_ref, cb_ref, t_rows):
    j = pl.program_id(0)
    zeros = jnp.zeros((HALO, LANES), F32)
    xpad[0:HALO, :] = zeros
    xpad[t_rows + HALO:t_rows + 2 * HALO, :] = zeros
    cp = pltpu.make_async_copy(z_hbm.at[:, pl.ds(pl.multiple_of(j * LANES, LANES), LANES)],
                               xpad.at[pl.ds(HALO, t_rows)], sem)
    cp.start()
    cp.wait()
    cw = cw_ref[...]
    cb = cb_ref[...]

    def conv_blk(r, carry):
        r0 = pl.multiple_of(r * LRU_RB, LRU_RB)
        taps = _conv_taps(xpad, r0, LRU_RB)
        acc = cb + cw[0:1, :] * taps[0]
        for k in range(1, 4):
            acc = acc + cw[k:k + 1, :] * taps[k]
        xc[pl.ds(r0, LRU_RB), :] = acc
        return carry

    lax.fori_loop(0, t_rows // LRU_RB, conv_blk, 0)


def _lru_specs(t_rows):
    return [pl.BlockSpec(memory_space=pl.ANY),
            pl.BlockSpec((4, LANES), lambda j: (0, j)),
            pl.BlockSpec((1, LANES), lambda j: (0, j)),
            pl.BlockSpec((1, LANES, 4 * LANES), lambda j: (j, 0, 0)),
            pl.BlockSpec((SUBLANES, LANES), lambda j: (0, j))]


def _lru_fwd(z_lru, conv_w, conv_b, wg, pb, t_rows, seq, name):
    nblk = t_rows // LRU_RB
    nsub = LRU_RB // SUBLANES

    def kern(z_hbm, cw_ref, cb_ref, wg_ref, pb_ref, hf_ref, hb_ref, xpad, xc, sem):
        _lru_load_conv(z_hbm, xpad, xc, sem, cw_ref, cb_ref, t_rows)
        w = wg_ref[0]
        pb_v = pb_ref[...]
        for rev in (False, True):
            o = 2 if rev else 0
            ba, bx = pb_v[o:o + 1, :], pb_v[o + 1:o + 2, :]
            sp8 = LRU_C * _softplus(-pb_v[5:6, :] if rev else -pb_v[4:5, :])
            out_ref = hb_ref if rev else hf_ref

            def blk(it, carry, rev=rev, o=o, ba=ba, bx=bx, sp8=sp8, out_ref=out_ref):
                r = (nblk - 1 - it) if rev else it
                r0 = pl.multiple_of(r * LRU_RB, LRU_RB)
                xcb = xc[pl.ds(r0, LRU_RB), :]
                pre = _dot(xcb.astype(BF16), w)
                _, gi, a, s = _lru_gates(xcb, pre[:, o * LANES:(o + 1) * LANES], pre[:, (o + 1) * LANES:(o + 2) * LANES],
                                         ba, bx, sp8)
                u = s * (gi * xcb) * _row_mask(r0, LRU_RB, seq)
                hs = [None] * nsub
                for sb in (range(nsub - 1, -1, -1) if rev else range(nsub)):
                    rs = slice(sb * SUBLANES, (sb + 1) * SUBLANES)
                    a_c, h_l = _scan8(a[rs], u[rs], rev)
                    h = h_l + a_c * carry
                    carry = _edge_row(h, rev)
                    hs[sb] = h
                out_ref[pl.ds(r0, LRU_RB), :] = jnp.concatenate(hs, axis=0)
                return carry

            lax.fori_loop(0, nblk, blk, jnp.zeros((SUBLANES, LANES), F32))

    return pl.pallas_call(
        kern, name=name, grid=(LRU_W // LANES,), in_specs=_lru_specs(t_rows),
        out_specs=[pl.BlockSpec((t_rows, LANES), lambda j: (0, j))] * 2,
        out_shape=[jax.ShapeDtypeStruct((t_rows, LRU_W), F32)] * 2,
        scratch_shapes=[pltpu.VMEM((t_rows + 2 * HALO, LANES), F32), pltpu.VMEM((t_rows, LANES), F32),
                        pltpu.SemaphoreType.DMA],
        compiler_params=_cparams(("arbitrary",)),
    )(z_lru, conv_w, conv_b, wg, pb)


def _lru_bwd(z_lru, conv_w, conv_b, wg, pb, h_f, h_b, dh, t_rows, seq, name):
    nblk = t_rows // LRU_RB
    nsub = LRU_RB // SUBLANES

    def kern(z_hbm, cw_ref, cb_ref, wg_ref, pb_ref, hf_ref, hb_ref, dh_ref,
             dx_ref, dcw_ref, dcb_ref, dwg_ref, dpb_ref, xpad, xc, dxc, sem):
        _lru_load_conv(z_hbm, xpad, xc, sem, cw_ref, cb_ref, t_rows)
        w = wg_ref[0]
        pb_v = pb_ref[...]
        zeros = jnp.zeros((HALO, LANES), F32)
        dxc[0:HALO, :] = zeros
        dxc[t_rows + HALO:t_rows + 2 * HALO, :] = zeros
        dwg_ref[...] = jnp.zeros_like(dwg_ref)
        dpb_rows = [None] * SUBLANES

        for rev in (False, True):
            o = 2 if rev else 0
            ba, bx = pb_v[o:o + 1, :], pb_v[o + 1:o + 2, :]
            lam = pb_v[5:6, :] if rev else pb_v[4:5, :]
            sp8 = LRU_C * _softplus(-lam)
            dlam_scale = LRU_C * _sigmoid(-lam)
            h_ref = hb_ref if rev else hf_ref
            w_a, w_x = w[:, o * LANES:(o + 1) * LANES], w[:, (o + 1) * LANES:(o + 2) * LANES]

            def blk(it, carry, rev=rev, o=o, ba=ba, bx=bx, sp8=sp8, dlam_scale=dlam_scale, h_ref=h_ref,
                    w_a=w_a, w_x=w_x):
                nu_c, acc_ba, acc_bx, acc_lam = carry
                adj_rev = not rev
                r = (nblk - 1 - it) if adj_rev else it
                r0 = pl.multiple_of(r * LRU_RB, LRU_RB)
                xcb = xc[pl.ds(r0, LRU_RB), :]
                pre = _dot(xcb.astype(BF16), w)
                gr, gi, a, s = _lru_gates(xcb, pre[:, o * LANES:(o + 1) * LANES],
                                          pre[:, (o + 1) * LANES:(o + 2) * LANES], ba, bx, sp8)
                mask = _row_mask(r0, LRU_RB, seq)
                dhb = dh_ref[pl.ds(r0, LRU_RB), :]
                hb = h_ref[pl.ds(r0, LRU_RB), :]
                if rev:
                    nxt0 = pl.multiple_of(jnp.minimum(r0 + LRU_RB, t_rows - SUBLANES), SUBLANES)
                    edge = h_ref[pl.ds(nxt0, SUBLANES), :][0:1, :] * (r < nblk - 1).astype(F32)
                    h_prev = jnp.concatenate([hb[1:], edge], axis=0)
                else:
                    prv0 = pl.multiple_of(jnp.maximum(r0 - SUBLANES, 0), SUBLANES)
                    edge = h_ref[pl.ds(prv0, SUBLANES), :][SUBLANES - 1:SUBLANES, :] * (r > 0).astype(F32)
                    h_prev = jnp.concatenate([edge, hb[:-1]], axis=0)
                rows = lax.broadcasted_iota(jnp.int32, (SUBLANES, LANES), 0)
                lams = [None] * nsub
                for sb in (range(nsub - 1, -1, -1) if adj_rev else range(nsub)):
                    rs = slice(sb * SUBLANES, (sb + 1) * SUBLANES)
                    a_c, nu_l = _scan8(a[rs], a[rs] * dhb[rs], adj_rev)
                    nu = nu_l + a_c * nu_c
                    if adj_rev:
                        nu_next = jnp.where(rows == SUBLANES - 1, nu_c, pltpu.roll(nu, SUBLANES - 1, 0))
                    else:
                        nu_next = jnp.where(rows == 0, nu_c, pltpu.roll(nu, 1, 0))
                    lams[sb] = dhb[rs] + nu_next
                    nu_c = _edge_row(nu, adj_rev)
                lam_t = jnp.concatenate(lams, axis=0)
                du = lam_t * mask
                da = lam_t * h_prev
                ix = gi * xcb
                dlog = da * a - du * ix * (a * a) / s
                d_i = du * s * xcb
                dxcb = du * s * gi
                dr = -dlog * sp8
                dpa = dr * gr * (1.0 - gr)
                dpx = d_i * gi * (1.0 - gi)
                dpa16, dpx16, xc16 = dpa.astype(BF16), dpx.astype(BF16), xcb.astype(BF16)
                dxcb = dxcb + _dot(dpa16, w_a, NT) + _dot(dpx16, w_x, NT)
                dwg_ref[0, :, o * LANES:(o + 1) * LANES] += _dot(xc16, dpa16, TN)
                dwg_ref[0, :, (o + 1) * LANES:(o + 2) * LANES] += _dot(xc16, dpx16, TN)
                if rev:
                    dxc[pl.ds(r0 + HALO, LRU_RB), :] += dxcb
                else:
                    dxc[pl.ds(r0 + HALO, LRU_RB), :] = dxcb
                acc_ba = acc_ba + jnp.sum(dpa, axis=0, keepdims=True)
                acc_bx = acc_bx + jnp.sum(dpx, axis=0, keepdims=True)
                acc_lam = acc_lam + jnp.sum(dlog * gr, axis=0, keepdims=True)
                return nu_c, acc_ba, acc_bx, acc_lam

            z1 = jnp.zeros((1, LANES), F32)
            _, s_ba, s_bx, s_lam = lax.fori_loop(0, nblk, blk, (jnp.zeros((SUBLANES, LANES), F32), z1, z1, z1))
            dpb_rows[o], dpb_rows[o + 1] = s_ba, s_bx
            dpb_rows[5 if rev else 4] = s_lam * dlam_scale
        dpb_rows[6] = dpb_rows[7] = jnp.zeros((1, LANES), F32)
        dpb_ref[...] = jnp.concatenate(dpb_rows, axis=0)

        cw = cw_ref[...]

        def conv_bwd(r, carry):
            r0 = pl.multiple_of(r * LRU_RB, LRU_RB)
            gt = _conv_taps(dxc, r0, LRU_RB, (2, 1, 0, -1))
            xt = _conv_taps(xpad, r0, LRU_RB)
            dx_ref[pl.ds(r0, LRU_RB), :] = (cw[0:1, :] * gt[0] + cw[1:2, :] * gt[1] + cw[2:3, :] * gt[2]
                                           + cw[3:4, :] * gt[3])
            g = gt[2]
            new = [carry[k] + jnp.sum(g * xt[k], axis=0, keepdims=True) for k in range(4)]
            new.append(carry[4] + jnp.sum(g, axis=0, keepdims=True))
            return tuple(new)

        z1 = jnp.zeros((1, LANES), F32)
        sums = lax.fori_loop(0, nblk, conv_bwd, (z1, z1, z1, z1, z1))
        dcw_ref[...] = jnp.concatenate(sums[:4], axis=0)
        dcb_ref[...] = sums[4]

    col = lambda j: (0, j)
    return pl.pallas_call(
        kern, name=name, grid=(LRU_W // LANES,),
        in_specs=_lru_specs(t_rows) + [pl.BlockSpec((t_rows, LANES), col)] * 3,
        out_specs=[pl.BlockSpec((t_rows, LANES), col), pl.BlockSpec((4, LANES), col), pl.BlockSpec((1, LANES), col),
                   pl.BlockSpec((1, LANES, 4 * LANES), lambda j: (j, 0, 0)), pl.BlockSpec((SUBLANES, LANES), col)],
        out_shape=[jax.ShapeDtypeStruct((t_rows, LRU_W), F32), jax.ShapeDtypeStruct((4, LRU_W), F32),
                   jax.ShapeDtypeStruct((1, LRU_W), F32), jax.ShapeDtypeStruct((4, LANES, 4 * LANES), F32),
                   jax.ShapeDtypeStruct((SUBLANES, LRU_W), F32)],
        scratch_shapes=[pltpu.VMEM((t_rows + 2 * HALO, LANES), F32), pltpu.VMEM((t_rows, LANES), F32),
                        pltpu.VMEM((t_rows + 2 * HALO, LANES), F32), pltpu.SemaphoreType.DMA],
        compiler_params=_cparams(("arbitrary",)),
    )(z_lru, conv_w, conv_b, wg, pb, h_f, h_b, dh)


QK = GLA_H * GLA_DK


def _cumsum_rows(x, rev):
    n = x.shape[0]
    rows = lax.broadcasted_iota(jnp.int32, x.shape, 0)
    d = 1
    while d < n:
        if rev:
            x = x + jnp.where(rows < n - d, pltpu.roll(x, n - d, 0), 0.0)
        else:
            x = x + jnp.where(rows >= d, pltpu.roll(x, d, 0), 0.0)
        d *= 2
    return x


def _gla_chunk_terms(q, k, zg, wg, bg, rev):
    pre = (_dot(zg.astype(BF16), wg) + bg)[:, (QK if rev else 0):(2 * QK if rev else QK)]
    g = (jnp.minimum(pre, 0.0) - jnp.log(1.0 + jnp.exp(-jnp.abs(pre)))) * (1.0 / GLA_GATE_NORM)
    b = _cumsum_rows(g, rev)
    b_last = b[0:1, :] if rev else b[CHUNK - 1:CHUNK, :]
    b_mid = b[CHUNK // 2:CHUNK // 2 + 1, :]
    qs = q * (GLA_DK ** -0.5)
    terms = dict(
        pre=pre, qs=qs, k=k,
        eb=jnp.exp(b), ek=jnp.exp(b_last - b), e_last=jnp.exp(b_last),
        eqm=jnp.exp(jnp.minimum(b - b_mid, EXP_CLAMP)), ekm=jnp.exp(jnp.minimum(b_mid - b, EXP_CLAMP)))
    return terms


def _gla_mask(rev):
    t = lax.broadcasted_iota(jnp.int32, (CHUNK, CHUNK), 0)
    s = lax.broadcasted_iota(jnp.int32, (CHUNK, CHUNK), 1)
    return (s > t) if rev else (s <= t)


def _gla_head_ops(tm, hd):
    sl = slice(hd * GLA_DK, (hd + 1) * GLA_DK)
    q_b = (tm["qs"][:, sl] * tm["eb"][:, sl]).astype(BF16)
    k_e = (tm["k"][:, sl] * tm["ek"][:, sl]).astype(BF16)
    q_m = (tm["qs"][:, sl] * tm["eqm"][:, sl]).astype(BF16)
    k_m = (tm["k"][:, sl] * tm["ekm"][:, sl]).astype(BF16)
    return sl, q_b, k_e, q_m, k_m


def _gla_fwd(z_qk, z_v, z_zg, wg, bg, t_rows, name):
    nc = t_rows // CHUNK

    def kern(qf, kf, vf, zf, qb, kb, vb, zb, wg_ref, bg_ref, of_ref, ob_ref, sf_ref, sb_ref, st_f, st_b):
        i = pl.program_id(0)

        @pl.when(i == 0)
        def _():
            st_f[...] = jnp.zeros_like(st_f)
            st_b[...] = jnp.zeros_like(st_b)

        wg_v, bg_v = wg_ref[...], bg_ref[...]
        for rev, (q_r, k_r, v_r, z_r, o_r, s_r, st) in ((False, (qf, kf, vf, zf, of_ref, sf_ref, st_f)),
                                                        (True, (qb, kb, vb, zb, ob_ref, sb_ref, st_b))):
            tm = _gla_chunk_terms(q_r[...], k_r[...], z_r[...], wg_v, bg_v, rev)
            mask = _gla_mask(rev)
            v = v_r[...]
            for hd in range(GLA_H):
                sl, q_b, k_e, q_m, k_m = _gla_head_ops(tm, hd)
                vs = slice(hd * GLA_DV, (hd + 1) * GLA_DV)
                v16 = v[:, vs].astype(BF16)
                a = jnp.where(mask, _dot(q_m, k_m, NT), 0.0).astype(BF16)
                s_in = st[hd]
                s_r[0, hd] = s_in
                o_r[:, vs] = _dot(a, v16) + _dot(q_b, s_in.astype(BF16), NT)
                st[hd] = s_in * tm["e_last"][:, sl] + _dot(v16, k_e, TN)

    fw = lambda c: (lambda i: (i, c))
    rv = lambda c: (lambda i: (nc - 1 - i, c))
    def side(ix):
        return [pl.BlockSpec((CHUNK, QK), ix(0)), pl.BlockSpec((CHUNK, QK), ix(1)),
                pl.BlockSpec((CHUNK, GLA_W), ix(0)), pl.BlockSpec((CHUNK, LANES), ix(0))]
    st_shape = (nc, GLA_H, GLA_DV, GLA_DK)
    return pl.pallas_call(
        kern, name=name, grid=(nc,),
        in_specs=side(fw) + side(rv) + [pl.BlockSpec(wg.shape, lambda i: (0, 0)), pl.BlockSpec(bg.shape, lambda i: (0, 0))],
        out_specs=[pl.BlockSpec((CHUNK, GLA_W), fw(0)), pl.BlockSpec((CHUNK, GLA_W), rv(0)),
                   pl.BlockSpec((1,) + st_shape[1:], lambda i: (i, 0, 0, 0)),
                   pl.BlockSpec((1,) + st_shape[1:], lambda i: (nc - 1 - i, 0, 0, 0))],
        out_shape=[jax.ShapeDtypeStruct((t_rows, GLA_W), F32)] * 2 + [jax.ShapeDtypeStruct(st_shape, F32)] * 2,
        scratch_shapes=[pltpu.VMEM(st_shape[1:], F32)] * 2,
        compiler_params=_cparams(("arbitrary",)),
    )(z_qk, z_qk, z_v, z_zg, z_qk, z_qk, z_v, z_zg, wg, bg)


def _gla_bwd(z_qk, z_v, z_zg, wg, bg, do, s_f, s_b, t_rows, name):
    nc = t_rows // CHUNK

    def kern(qf, kf, vf, zf, dof, ssf, qb, kb, vb, zb, dob, ssb, wg_ref, bg_ref,
             dqf, dkf, dvf, dpf, dqb, dkb, dvb, dpb, ds_f, ds_b):
        i = pl.program_id(0)

        @pl.when(i == 0)
        def _():
            ds_f[...] = jnp.zeros_like(ds_f)
            ds_b[...] = jnp.zeros_like(ds_b)

        wg_v, bg_v = wg_ref[...], bg_ref[...]
        sides = ((False, (qf, kf, vf, zf, dof, ssf, dqf, dkf, dvf, dpf, ds_f)),
                 (True, (qb, kb, vb, zb, dob, ssb, dqb, dkb, dvb, dpb, ds_b)))
        for rev, (q_r, k_r, v_r, z_r, do_r, ss_r, dq_r, dk_r, dv_r, dp_r, ds) in sides:
            tm = _gla_chunk_terms(q_r[...], k_r[...], z_r[...], wg_v, bg_v, rev)
            mask = _gla_mask(rev)
            v = v_r[...]
            d_o = do_r[...]
            db_parts, cross_parts = [], []
            for hd in range(GLA_H):
                sl, q_b, k_e, q_m, k_m = _gla_head_ops(tm, hd)
                vs = slice(hd * GLA_DV, (hd + 1) * GLA_DV)
                v16, do16 = v[:, vs].astype(BF16), d_o[:, vs].astype(BF16)
                a = jnp.where(mask, _dot(q_m, k_m, NT), 0.0).astype(BF16)
                da = jnp.where(mask, _dot(do16, v16, NT), 0.0).astype(BF16)
                s_in = ss_r[0, hd]
                s_in16 = s_in.astype(BF16)
                ds_out = ds[hd]
                ds16 = ds_out.astype(BF16)
                dv_r[:, vs] = _dot(a, do16, TN) + _dot(k_e, ds16, NT)
                dqs = _dot(da, k_m) * tm["eqm"][:, sl] + _dot(do16, s_in16) * tm["eb"][:, sl]
                dk = _dot(da, q_m, TN) * tm["ekm"][:, sl] + _dot(v16, ds16) * tm["ek"][:, sl]
                ds[hd] = ds_out * tm["e_last"][:, sl] + _dot(do16, q_b, TN)
                dq_r[:, sl] = dqs
                dk_r[:, sl] = dk
                db_parts.append(tm["qs"][:, sl] * dqs - tm["k"][:, sl] * dk)
                s_out = s_in * tm["e_last"][:, sl] + _dot(v16, k_e, TN)
                cross_parts.append(jnp.sum(s_out * ds_out, axis=0, keepdims=True))
            d_b = jnp.concatenate(db_parts, axis=1)
            dg = _cumsum_rows(d_b, not rev) + jnp.concatenate(cross_parts, axis=1)
            dp_r[...] = dg * (1.0 / GLA_GATE_NORM) * _sigmoid(-tm["pre"])

    fw = lambda c: (lambda i: (nc - 1 - i, c))
    rv = lambda c: (lambda i: (i, c))
    st_blk = (1, GLA_H, GLA_DV, GLA_DK)
    def side(ix, st_ix):
        return [pl.BlockSpec((CHUNK, QK), ix(0)), pl.BlockSpec((CHUNK, QK), ix(1)),
                pl.BlockSpec((CHUNK, GLA_W), ix(0)), pl.BlockSpec((CHUNK, LANES), ix(0)),
                pl.BlockSpec((CHUNK, GLA_W), ix(0)), pl.BlockSpec(st_blk, st_ix)]
    def oside(ix):
        return [pl.BlockSpec((CHUNK, QK), ix(0)), pl.BlockSpec((CHUNK, QK), ix(0)),
                pl.BlockSpec((CHUNK, GLA_W), ix(0)), pl.BlockSpec((CHUNK, QK), ix(0))]
    o_shapes = [jax.ShapeDtypeStruct((t_rows, QK), F32), jax.ShapeDtypeStruct((t_rows, QK), F32),
                jax.ShapeDtypeStruct((t_rows, GLA_W), F32), jax.ShapeDtypeStruct((t_rows, QK), F32)]
    return pl.pallas_call(
        kern, name=name, grid=(nc,),
        in_specs=(side(fw, lambda i: (nc - 1 - i, 0, 0, 0)) + side(rv, lambda i: (i, 0, 0, 0))
                  + [pl.BlockSpec(wg.shape, lambda i: (0, 0)), pl.BlockSpec(bg.shape, lambda i: (0, 0))]),
        out_specs=oside(fw) + oside(rv), out_shape=o_shapes * 2,
        scratch_shapes=[pltpu.VMEM((GLA_H, GLA_DV, GLA_DK), F32)] * 2,
        compiler_params=_cparams(("arbitrary",)),
    )(z_qk, z_qk, z_v, z_zg, do, s_f, z_qk, z_qk, z_v, z_zg, do, s_b, wg, bg)


FLAT_W = 1024


def _adam_math(wv, gv, mv, vv):
    bc1 = 1.0 - ADAM_B1 ** ADAM_STEP
    bc2 = 1.0 - ADAM_B2 ** ADAM_STEP
    m_new = ADAM_B1 * mv + (1.0 - ADAM_B1) * gv
    v_new = ADAM_B2 * vv + (1.0 - ADAM_B2) * (gv * gv)
    delta = -ADAM_LR * ((m_new / bc1) / (jnp.sqrt(v_new / bc2) + ADAM_EPS) + ADAM_WD * wv)
    return delta, m_new, v_new


def _row_tile(rows, cap=256):
    t = cap
    while rows % t:
        t //= 2
    assert t >= SUBLANES, rows
    return t


def _adamw_layers(w, m, v, g_layers, name):
    depth, rows, cols = w.shape
    tr = _row_tile(rows)

    def kern(w_ref, m_ref, v_ref, *rest):
        g_refs, (go_ref, d_ref, mo_ref, vo_ref) = rest[:depth], rest[depth:]
        l = pl.program_id(0)
        gv = g_refs[0][...]
        for k in range(1, depth):
            gv = jnp.where(l == k, g_refs[k][...], gv)
        delta, m_new, v_new = _adam_math(w_ref[...], gv, m_ref[...], v_ref[...])
        go_ref[...] = gv
        d_ref[...] = delta
        mo_ref[...] = m_new
        vo_ref[...] = v_new

    stacked = pl.BlockSpec((None, tr, cols), lambda l, i: (l, i, 0))
    return pl.pallas_call(
        kern, name=name, grid=(depth, rows // tr),
        in_specs=[stacked] * 3 + [pl.BlockSpec((tr, cols), lambda l, i: (i, 0))] * depth,
        out_specs=[stacked] * 4, out_shape=[jax.ShapeDtypeStruct(w.shape, F32)] * 4,
        compiler_params=_cparams(("arbitrary", "arbitrary")),
    )(w, m, v, *g_layers)


def _adamw_flat(w, g, m, v):
    rows = w.shape[0]

    def body(i, ins, consts, outs, accs):
        delta, m_new, v_new = _adam_math(*[r[...] for r in ins])
        outs[0][...] = delta
        outs[1][...] = m_new
        outs[2][...] = v_new

    w_, g_, m_, v_ = w, g, m, v
    return _rows(body, rows, _row_tile(rows), [(w_, FLAT_W, 0), (g_, FLAT_W, 0), (m_, FLAT_W, 0), (v_, FLAT_W, 0)],
                 [], [(FLAT_W, F32)] * 3, [], "adamw_small")


def _add_sibling(g, got, c_idx, name):
    _, _, rows, cols = g.shape
    tr = _row_tile(rows)

    def kern(c_ref, g_ref, got_ref, o_ref):
        o_ref[...] = (g_ref[...] + got_ref[...]).astype(BF16)

    grid_spec = pltpu.PrefetchScalarGridSpec(
        num_scalar_prefetch=1, grid=(N_CHIPS, rows // tr),
        in_specs=[pl.BlockSpec((None, None, tr, cols), lambda q, i, c_ref: (q, c_ref[0], i, 0)),
                  pl.BlockSpec((None, tr, cols), lambda q, i, c_ref: (q, i, 0))],
        out_specs=pl.BlockSpec((None, tr, cols), lambda q, i, c_ref: (q, i, 0)))
    return pl.pallas_call(
        kern, name=name, grid_spec=grid_spec, out_shape=jax.ShapeDtypeStruct((N_CHIPS, rows, cols), BF16),
        compiler_params=_cparams(("arbitrary", "arbitrary")),
    )(c_idx, g, got)


def _sum_chips(x, name):
    _, rows, cols = x.shape
    tr = _row_tile(rows)

    def kern(x_ref, o_ref):
        xv = x_ref[...].astype(F32)
        o_ref[...] = ((xv[0] + xv[1]) + xv[2]) + xv[3]

    return pl.pallas_call(
        kern, name=name, grid=(rows // tr,),
        in_specs=[pl.BlockSpec((N_CHIPS, tr, cols), lambda i: (0, i, 0))],
        out_specs=pl.BlockSpec((tr, cols), lambda i: (i, 0)),
        out_shape=jax.ShapeDtypeStruct((rows, cols), F32),
        compiler_params=_cparams(("arbitrary",)),
    )(x)


def _place():
    x, y, c = lax.axis_index("x"), lax.axis_index("y"), lax.axis_index("c")
    return x, y, c


def _other_chips(x, y):
    return [(1 - x, y), (x, 1 - y), (1 - x, 1 - y)]


def _hbm_call(kern, name, ins, out_shapes, n_sems):
    return pl.pallas_call(
        kern, name=name,
        in_specs=[pl.BlockSpec(memory_space=pl.ANY)] * len(ins),
        out_specs=[pl.BlockSpec(memory_space=pl.ANY)] * len(out_shapes),
        out_shape=out_shapes,
        scratch_shapes=[pltpu.SemaphoreType.DMA((n,)) for n in n_sems],
        compiler_params=pltpu.CompilerParams(has_side_effects=True),
    )(*ins)


DMA_CHUNK_BYTES = 1 << 20
DMA_MAX_CHUNKS = 4


def _row_chunks(rows, row_bytes, align=16):
    units = rows // align
    k = max(1, min(DMA_MAX_CHUNKS, -(-rows * row_bytes // DMA_CHUNK_BYTES), units))
    out, start = [], 0
    for i in range(k):
        size = (units // k + (1 if i < units % k else 0)) * align
        out.append((start, size))
        start += size
    if start < rows:
        out[-1] = (out[-1][0], out[-1][1] + rows - start)
    return out


def _row_bytes(shape, dtype):
    return math.prod(shape[1:]) * jnp.dtype(dtype).itemsize


def _remote(src, dst, send, recv, idx, dev):
    return pltpu.make_async_remote_copy(src, dst, send.at[idx], recv.at[idx], device_id=dev, device_id_type=MESH_ID)


def _allgather_chips(xs, name):
    na = len(xs)
    chunks = [_row_chunks(x.shape[1], _row_bytes(x.shape[1:], x.dtype)) for x in xs]
    n_cp = 3 * sum(len(ch) for ch in chunks)

    def kern(*refs):
        x_refs, o_refs = refs[:na], refs[na:2 * na]
        send1, recv1, send2, recv2, loc = refs[2 * na:]
        px, py, c = _place()
        me = 2 * px + py
        sib = (px, py, 1 - c)
        owns = [pltpu.make_async_copy(x, o.at[me], loc.at[a]) for a, (x, o) in enumerate(zip(x_refs, o_refs))]
        for cp in owns:
            cp.start()
        plan = [(a, qx, qy, pl.ds(s, n)) for a in range(na) for (qx, qy) in _other_chips(px, py) for (s, n) in chunks[a]]
        first = [_remote(x_refs[a].at[c, rows], o_refs[a].at[me, c, rows], send1, recv1, i, (qx, qy, c))
                 for i, (a, qx, qy, rows) in enumerate(plan)]
        for cp in first:
            cp.start()
        passed = []
        for i, (a, qx, qy, rows) in enumerate(plan):
            slot = o_refs[a].at[2 * qx + qy, c, rows]
            _remote(slot, slot, send1, recv1, i, (qx, qy, c)).wait_recv()
            fwd = _remote(slot, slot, send2, recv2, i, sib)
            fwd.start()
            passed.append(fwd)
        for i, (a, qx, qy, rows) in enumerate(plan):
            slot = o_refs[a].at[2 * qx + qy, 1 - c, rows]
            _remote(slot, slot, send2, recv2, i, sib).wait_recv()
        for cp in first + passed:
            cp.wait_send()
        for cp in owns:
            cp.wait()

    outs = [jax.ShapeDtypeStruct((N_CHIPS,) + x.shape, x.dtype) for x in xs]
    return _hbm_call(kern, name, list(xs), outs, [n_cp, n_cp, n_cp, n_cp, na])


def _sibling_halves(gs, name):
    na = len(gs)
    chunks = [_row_chunks(g.shape[2], _row_bytes(g.shape[2:], g.dtype)) for g in gs]
    n_cp = N_CHIPS * sum(len(ch) for ch in chunks)

    def kern(*refs):
        g_refs, o_refs = refs[:na], refs[na:2 * na]
        send, recv = refs[2 * na:]
        px, py, c = _place()
        plan = [(a, q, pl.ds(s, n)) for a in range(na) for q in range(N_CHIPS) for (s, n) in chunks[a]]
        cps = [_remote(g_refs[a].at[q, 1 - c, rows], o_refs[a].at[q, rows], send, recv, i, (px, py, 1 - c))
               for i, (a, q, rows) in enumerate(plan)]
        for cp in cps:
            cp.start()
        for cp in cps:
            cp.wait()

    outs = [jax.ShapeDtypeStruct((N_CHIPS,) + g.shape[2:], g.dtype) for g in gs]
    return _hbm_call(kern, name, list(gs), outs, [n_cp, n_cp])


def _chip_exchange(ps, name):
    na = len(ps)
    chunks = [_row_chunks(p.shape[1], _row_bytes(p.shape[1:], p.dtype)) for p in ps]
    n_cp = 3 * sum(len(ch) for ch in chunks)

    def kern(*refs):
        p_refs, o_refs = refs[:na], refs[na:2 * na]
        send, recv, loc = refs[2 * na:]
        px, py, c = _place()
        me = 2 * px + py
        owns = [pltpu.make_async_copy(p.at[me], o.at[me], loc.at[a]) for a, (p, o) in enumerate(zip(p_refs, o_refs))]
        for cp in owns:
            cp.start()
        plan = [(a, qx, qy, pl.ds(s, n)) for a in range(na) for (qx, qy) in _other_chips(px, py) for (s, n) in chunks[a]]
        cps = [_remote(p_refs[a].at[2 * qx + qy, rows], o_refs[a].at[me, rows], send, recv, i, (qx, qy, c))
               for i, (a, qx, qy, rows) in enumerate(plan)]
        for cp in cps:
            cp.start()
        for cp in cps:
            cp.wait()
        for cp in owns:
            cp.wait()

    outs = [jax.ShapeDtypeStruct(p.shape, p.dtype) for p in ps]
    return _hbm_call(kern, name, list(ps), outs, [n_cp, n_cp, na])


def _sibling_join(rs, name):
    na = len(rs)
    chunks = [_row_chunks(r.shape[0], _row_bytes(r.shape, r.dtype)) for r in rs]
    n_cp = sum(len(ch) for ch in chunks)

    def kern(*refs):
        r_refs, o_refs = refs[:na], refs[na:2 * na]
        send, recv, loc = refs[2 * na:]
        px, py, c = _place()
        owns = [pltpu.make_async_copy(r, o.at[c], loc.at[a]) for a, (r, o) in enumerate(zip(r_refs, o_refs))]
        for cp in owns:
            cp.start()
        plan = [(a, pl.ds(s, n)) for a in range(na) for (s, n) in chunks[a]]
        cps = [_remote(r_refs[a].at[rows], o_refs[a].at[c, rows], send, recv, i, (px, py, 1 - c))
               for i, (a, rows) in enumerate(plan)]
        for cp in cps:
            cp.start()
        for cp in cps:
            cp.wait()
        for cp in owns:
            cp.wait()

    outs = [jax.ShapeDtypeStruct((2,) + r.shape, r.dtype) for r in rs]
    return _hbm_call(kern, name, list(rs), outs, [n_cp, n_cp, na])


def _reduce_pack(gs, c_idx, tag):
    got = _sibling_halves(gs, "reduce_sibling_halves" + tag)
    part = [_add_sibling(g, o, c_idx, f"reduce_add_sibling{tag}_{a}") for a, (g, o) in enumerate(zip(gs, got))]
    landed = _chip_exchange(part, "reduce_chip_exchange" + tag)
    mine = [_sum_chips(x, f"reduce_sum_chips{tag}_{a}") for a, x in enumerate(landed)]
    return _sibling_join(mine, "reduce_sibling_join" + tag)


WEIGHTS = ["meta_tokens", "norm_mix_pre", "norm_mix_post", "norm_mlp_pre", "norm_mlp_post", "w_in", "conv_w",
           "conv_b", "lru_wa_f", "lru_ba_f", "lru_wx_f", "lru_bx_f", "lru_lambda_f", "lru_wa_b", "lru_ba_b",
           "lru_wx_b", "lru_bx_b", "lru_lambda_b", "gla_wg_f", "gla_bg_f", "gla_wg_b", "gla_bg_b", "gla_head_norm",
           "w_out", "w_mlp_up", "w_mlp_down"]
BIG = ("w_in", "w_out", "w_mlp_up", "w_mlp_down")
SMALL = [n for n in WEIGHTS if n not in BIG]
SMALL_SHARDED = {"meta_tokens": 1, "conv_w": 2, "gla_wg_f": 2, "gla_wg_b": 2}
N_CHIPS = 4
SMALL_RAW = [("g1", (1, D_MODEL)), ("g2", (1, D_MODEL)), ("g3", (1, D_MODEL)), ("g4", (1, D_MODEL)),
             ("conv_w", (4, LRU_W)), ("conv_b", (1, LRU_W)), ("lru_wg", (4, LANES, 4 * LANES)),
             ("lru_pb", (SUBLANES, LRU_W)), ("gla_wg", (2 * GLA_RANK, 2 * QK)), ("gla_bg", (1, 2 * QK)),
             ("head_norm", (1, GLA_W))]
META_SHAPE = (N_META, D_MODEL)


def _pad_to(v, n):
    return jnp.pad(v, (0, n - v.shape[0]))


def _round_up(n, m):
    return -(-n // m) * m


def _flat_cat(arrs, total):
    return _pad_to(jnp.concatenate([a.reshape(-1) for a in arrs]), total)


def _split_flat(flat, shapes):
    out, off = [], 0
    for s in shapes:
        n = math.prod(s)
        out.append(flat[off:off + n].reshape(s))
        off += n
    return out


def _my_shard(full, axis, chip):
    n = full.shape[axis] // N_CHIPS
    return lax.dynamic_slice_in_dim(full, chip * n, n, axis=axis)


def _block_diag_gates(wa_f, wx_f, wa_b, wx_b):
    def bd(w):
        z = jnp.zeros((LRU_HD, LRU_HD), w.dtype)
        return jnp.stack([jnp.block([[w[2 * j], z], [z, w[2 * j + 1]]]) for j in range(4)])
    return jnp.concatenate([bd(wa_f), bd(wx_f), bd(wa_b), bd(wx_b)], axis=2).astype(BF16)


def _gate_diag_blocks(dwg, o):
    blk = dwg[:, :, o * LANES:(o + 1) * LANES]
    return jnp.stack([blk[j, hh * LRU_HD:(hh + 1) * LRU_HD, hh * LRU_HD:(hh + 1) * LRU_HD]
                      for j in range(4) for hh in range(2)])


def kernel(x, meta_tokens, norm_mix_pre, norm_mix_post, norm_mlp_pre, norm_mlp_post, w_in, conv_w, conv_b, lru_wa_f, lru_ba_f, lru_wx_f, lru_bx_f, lru_lambda_f, lru_wa_b, lru_ba_b, lru_wx_b, lru_bx_b, lru_lambda_b, gla_wg_f, gla_bg_f, gla_wg_b, gla_bg_b, gla_head_norm, w_out, w_mlp_up, w_mlp_down, loss_target, m_meta_tokens, m_norm_mix_pre, m_norm_mix_post, m_norm_mlp_pre, m_norm_mlp_post, m_w_in, m_conv_w, m_conv_b, m_lru_wa_f, m_lru_ba_f, m_lru_wx_f, m_lru_bx_f, m_lru_lambda_f, m_lru_wa_b, m_lru_ba_b, m_lru_wx_b, m_lru_bx_b, m_lru_lambda_b, m_gla_wg_f, m_gla_bg_f, m_gla_wg_b, m_gla_bg_b, m_gla_head_norm, m_w_out, m_w_mlp_up, m_w_mlp_down, v_meta_tokens, v_norm_mix_pre, v_norm_mix_post, v_norm_mlp_pre, v_norm_mlp_post, v_w_in, v_conv_w, v_conv_b, v_lru_wa_f, v_lru_ba_f, v_lru_wx_f, v_lru_bx_f, v_lru_lambda_f, v_lru_wa_b, v_lru_ba_b, v_lru_wx_b, v_lru_bx_b, v_lru_lambda_b, v_gla_wg_f, v_gla_bg_f, v_gla_wg_b, v_gla_bg_b, v_gla_head_norm, v_w_out, v_w_mlp_up, v_w_mlp_down):
    args = dict(locals())
    w_loc = {n: args[n] for n in WEIGHTS}
    m_loc = {n: args["m_" + n] for n in WEIGHTS}
    v_loc = {n: args["v_" + n] for n in WEIGHTS}
    seq = x.shape[1]
    t_rows = FRONT + seq + BACK
    assert t_rows % 768 == 0 and seq % FRONT == 0, seq
    chip = 2 * lax.axis_index("x") + lax.axis_index("y")
    c_idx = lax.axis_index("c").astype(jnp.int32).reshape(1)

    def halves(a):
        return a.reshape((2, a.shape[0] // 2) + a.shape[1:])

    big16 = {n: w_loc[n].astype(BF16) for n in BIG}
    sm_names = list(SMALL_SHARDED)
    sm_len = _round_up(sum(w_loc[n].size for n in sm_names), 2 * SUBLANES * FLAT_W)
    sm_pack = _flat_cat([w_loc[n] for n in sm_names], sm_len).reshape(2, -1, FLAT_W)
    gathered = _allgather_chips([halves(big16[n][l]) for l in range(DEPTH) for n in BIG] + [sm_pack], "gather_weights")
    sm_parts = [_split_flat(gathered[-1][q].reshape(-1), [w_loc[n].shape for n in sm_names]) for q in range(N_CHIPS)]
    full = {n: jnp.concatenate([sm_parts[q][i] for q in range(N_CHIPS)], axis=SMALL_SHARDED[n])
            for i, n in enumerate(sm_names)}
    for n in SMALL:
        if n not in SMALL_SHARDED:
            full[n] = w_loc[n]
    big_full = []
    for l in range(DEPTH):
        g_in, g_out, g_up, g_down = gathered[l * len(BIG):(l + 1) * len(BIG)]
        w_in_l = jnp.transpose(g_in.reshape(N_CHIPS, D_MODEL, -1), (1, 0, 2)).reshape(D_MODEL, D_IN)
        big_full.append(dict(
            w_in=jnp.pad(w_in_l, ((0, 0), (0, D_IN_PAD - D_IN))),
            w_out=g_out.reshape(-1, D_MODEL),
            w_up=jnp.transpose(g_up.reshape(N_CHIPS, D_MODEL, -1), (1, 0, 2)).reshape(D_MODEL, D_FF),
            w_down=g_down.reshape(D_FF, D_MODEL)))

    meta_blk = jnp.concatenate([jnp.zeros((FRONT - N_META, D_MODEL), F32), full["meta_tokens"]], axis=0)
    h = jnp.concatenate([meta_blk, x[0], jnp.zeros((BACK, D_MODEL), F32)], axis=0)
    target = loss_target[0]

    layer_w = []
    for l in range(DEPTH):
        wg = jnp.zeros((LANES, 2 * QK), F32)
        wg = wg.at[0:GLA_RANK, 0:QK].set(full["gla_wg_f"][l]).at[GLA_RANK:2 * GLA_RANK, QK:].set(full["gla_wg_b"][l])
        pb = jnp.stack([full["lru_ba_f"][l], full["lru_bx_f"][l], full["lru_ba_b"][l], full["lru_bx_b"][l],
                        full["lru_lambda_f"][l], full["lru_lambda_b"][l], jnp.zeros((LRU_W,), F32),
                        jnp.zeros((LRU_W,), F32)])
        layer_w.append(dict(
            big_full[l],
            g1=full["norm_mix_pre"][l][None], g2=full["norm_mix_post"][l][None],
            g3=full["norm_mlp_pre"][l][None], g4=full["norm_mlp_post"][l][None],
            conv_w=full["conv_w"][l], conv_b=full["conv_b"][l][None],
            lru_wg=_block_diag_gates(full["lru_wa_f"][l], full["lru_wx_f"][l], full["lru_wa_b"][l], full["lru_wx_b"][l]),
            lru_pb=pb, gla_wg=wg.astype(BF16),
            gla_bg=jnp.concatenate([full["gla_bg_f"][l], full["gla_bg_b"][l]])[None],
            head_norm=full["gla_head_norm"][l][None]))

    def split_z(acc):
        return (acc[:, 0:1024], acc[:, 1024:1536], acc[:, 1536:2048], acc[:, 2048:2560], acc[:, ZG_OFF:ZG_OFF + LANES])

    def relu2(acc):
        r = jnp.maximum(acc, 0.0)
        return acc, r * r

    saved = []
    for l in range(DEPTH):
        lw = layer_w[l]
        tag = f"_l{l}"
        n1 = _norm_cast(h, lw["g1"], t_rows, "norm_mix_pre" + tag)
        z_lru, z_qk, z_v, z_go, z_zg = _mm(
            n1, lw["w_in"], "nn", 384, D_IN_PAD, D_MODEL,
            [(1024, F32), (512, F32), (512, F32), (512, F32), (LANES, F32)], "in_proj" + tag, epilogue=split_z)
        h_f, h_b = _lru_fwd(z_lru, lw["conv_w"], lw["conv_b"], lw["lru_wg"], lw["lru_pb"], t_rows, seq, "lru_fwd" + tag)
        o_f, o_b, s_f, s_b = _gla_fwd(z_qk, z_v, z_zg, lw["gla_wg"], lw["gla_bg"], t_rows, "gla_fwd" + tag)
        ymix = _mix_fwd(h_f, h_b, z_lru, o_f, o_b, z_go, lw["head_norm"], t_rows, "mix_fwd" + tag)
        mix = _mm(ymix, lw["w_out"], "nn", 768, D_MODEL, D_MODEL, [(D_MODEL, F32)], "out_proj" + tag)[0]
        h1 = _norm_residual(mix, lw["g2"], h, t_rows, "norm_mix_post" + tag)
        n3 = _norm_cast(h1, lw["g3"], t_rows, "norm_mlp_pre" + tag)
        u, act = _mm(n3, lw["w_up"], "nn", 768, 1024, D_MODEL, [(D_FF, F32), (D_FF, BF16)], "mlp_up" + tag,
                     epilogue=relu2)
        ff = _mm(act, lw["w_down"], "nn", 768, D_MODEL, 1024, [(D_MODEL, F32)], "mlp_down" + tag)[0]
        h2 = _norm_residual(ff, lw["g4"], h1, t_rows, "norm_mlp_post" + tag)
        saved.append(dict(h=h, n1=n1, z_lru=z_lru, z_qk=z_qk, z_v=z_v, z_go=z_go, z_zg=z_zg, h_f=h_f, h_b=h_b,
                          o_f=o_f, o_b=o_b, s_f=s_f, s_b=s_b, ymix=ymix, mix=mix, h1=h1, n3=n3, u=u, act=act, ff=ff))
        h = h2

    dh, loss_part = _loss_and_grad(h, target, t_rows, seq)
    loss = lax.psum(loss_part[0, 0], ("x", "y", "c"))

    small_len = _round_up(sum(math.prod(s) for _, s in SMALL_RAW) + math.prod(META_SHAPE),
                          N_CHIPS * 2 * 2 * SUBLANES * FLAT_W)
    totals = [None] * DEPTH
    for l in reversed(range(DEPTH)):
        lw, sv = layer_w[l], saved[l]
        tag = f"_l{l}"
        raw = {}
        dff, raw["g4"] = _norm_residual_bwd(dh, sv["ff"], lw["g4"], t_rows, "norm_mlp_post_bwd" + tag)
        gw_down = _mm(sv["act"], dff, "tn", 1024, D_MODEL, 768, [(D_MODEL, F32)], "dw_down" + tag)[0]
        du = _mm(dff, lw["w_down"], "nt", 768, 1024, D_MODEL, [(D_FF, BF16)], "d_act" + tag,
                 epilogue=lambda acc, uu: (acc * 2.0 * jnp.maximum(uu, 0.0),), extras=(sv["u"],))[0]
        gw_up = _mm(sv["n3"], du, "tn", D_MODEL, D_FF // N_CHIPS, 768, [(D_FF, F32)], "dw_up" + tag,
                    col_blocks_first=True)[0]
        dn3 = _mm(du, lw["w_up"], "nt", 768, D_MODEL, 1024, [(D_MODEL, F32)], "d_n3" + tag)[0]
        dh1, raw["g3"] = _norm_bwd_add(dn3, sv["h1"], lw["g3"], dh, t_rows, "norm_mlp_pre_bwd" + tag)
        dmix, raw["g2"] = _norm_residual_bwd(dh1, sv["mix"], lw["g2"], t_rows, "norm_mix_post_bwd" + tag)
        gw_out = _mm(sv["ymix"], dmix, "tn", D_MODEL, D_MODEL, 768, [(D_MODEL, F32)], "dw_out" + tag)[0]
        dymix = _mm(dmix, lw["w_out"], "nt", 768, D_MODEL, D_MODEL, [(D_MODEL, F32)], "d_ymix" + tag)[0]
        dh_lru, dgate, do, dgo, raw["head_norm"] = _mix_bwd(
            dymix, sv["h_f"], sv["h_b"], sv["z_lru"], sv["o_f"], sv["o_b"], sv["z_go"], lw["head_norm"], t_rows,
            "mix_bwd" + tag)
        gla_grads = _gla_bwd(sv["z_qk"], sv["z_v"], sv["z_zg"], lw["gla_wg"], lw["gla_bg"], do, sv["s_f"], sv["s_b"],
                             t_rows, "gla_bwd" + tag)
        dx_br, raw["conv_w"], raw["conv_b"], raw["lru_wg"], raw["lru_pb"] = _lru_bwd(
            sv["z_lru"], lw["conv_w"], lw["conv_b"], lw["lru_wg"], lw["lru_pb"], sv["h_f"], sv["h_b"], dh_lru,
            t_rows, seq, "lru_bwd" + tag)
        dz, dwg_gla, raw["gla_bg"] = _dz_assemble(dx_br, dgate, gla_grads, dgo, sv["z_zg"], lw["gla_wg"], t_rows,
                                                  "dz_assemble" + tag)
        raw["gla_wg"] = dwg_gla[0:2 * GLA_RANK]
        gw_in = _mm(sv["n1"], dz, "tn", D_MODEL, 896, 768, [(D_IN_PAD, F32)], "dw_in" + tag)[0]
        dn1 = _mm(dz, lw["w_in"], "nt", 768, D_MODEL, 896, [(D_MODEL, F32)], "d_n1" + tag)[0]
        dh, raw["g1"] = _norm_bwd_add(dn1, sv["h"], lw["g1"], dh1, t_rows, "norm_mix_pre_bwd" + tag)

        cols = D_IN // N_CHIPS
        in_sh = jnp.stack([gw_in[:, q * cols:(q + 1) * cols] for q in range(N_CHIPS)])
        meta_g = dh[FRONT - N_META:FRONT] if l == 0 else jnp.zeros(META_SHAPE, F32)
        small = _flat_cat([raw[n] for n, _ in SMALL_RAW] + [meta_g], small_len)
        pack = [in_sh.reshape(N_CHIPS, 2, D_MODEL // 2, cols),
                gw_out.reshape(N_CHIPS, 2, -1, D_MODEL),
                gw_up.reshape(N_CHIPS, 2, D_MODEL // 2, D_FF // N_CHIPS),
                gw_down.reshape(N_CHIPS, 2, -1, D_MODEL),
                small.reshape(N_CHIPS, 2, -1, FLAT_W)]
        totals[l] = _reduce_pack(pack, c_idx, tag)

    grad_x = dh[FRONT:FRONT + seq][None]

    sm_all = _allgather_chips([totals[l][4] for l in range(DEPTH)], "gather_small_grads")
    g_tot = {}
    per_layer = []
    for l in range(DEPTH):
        pieces = _split_flat(sm_all[l].reshape(-1), [s for _, s in SMALL_RAW] + [META_SHAPE])
        per_layer.append(dict(zip([n for n, _ in SMALL_RAW] + ["meta"], pieces)))
    stack = lambda f: jnp.stack([f(per_layer[l]) for l in range(DEPTH)])
    g_tot["meta_tokens"] = _my_shard(per_layer[0]["meta"], 1, chip)
    for n, key in (("norm_mix_pre", "g1"), ("norm_mix_post", "g2"), ("norm_mlp_pre", "g3"), ("norm_mlp_post", "g4"),
                   ("conv_b", "conv_b"), ("gla_head_norm", "head_norm")):
        g_tot[n] = stack(lambda d, key=key: d[key][0])
    g_tot["conv_w"] = _my_shard(stack(lambda d: d["conv_w"]), 2, chip)
    for o, n in enumerate(("lru_wa_f", "lru_wx_f", "lru_wa_b", "lru_wx_b")):
        g_tot[n] = stack(lambda d, o=o: _gate_diag_blocks(d["lru_wg"], o))
    for row, n in enumerate(("lru_ba_f", "lru_bx_f", "lru_ba_b", "lru_bx_b", "lru_lambda_f", "lru_lambda_b")):
        g_tot[n] = stack(lambda d, row=row: d["lru_pb"][row])
    g_tot["gla_wg_f"] = _my_shard(stack(lambda d: d["gla_wg"][0:GLA_RANK, 0:QK]), 2, chip)
    g_tot["gla_wg_b"] = _my_shard(stack(lambda d: d["gla_wg"][GLA_RANK:, QK:]), 2, chip)
    g_tot["gla_bg_f"] = stack(lambda d: d["gla_bg"][0, 0:QK])
    g_tot["gla_bg_b"] = stack(lambda d: d["gla_bg"][0, QK:])

    delta, new_m, new_v = {}, {}, {}
    for a, n in enumerate(BIG):
        shp = w_loc[n].shape
        flat3 = lambda t, shp=shp: t.reshape(DEPTH, -1, shp[-1])
        g_l = [totals[l][a].reshape(-1, shp[-1]) for l in range(DEPTH)]
        outs = _adamw_layers(flat3(w_loc[n]), flat3(m_loc[n]), flat3(v_loc[n]), g_l, "adamw_" + n)
        g_tot[n], delta[n], new_m[n], new_v[n] = [o.reshape(shp) for o in outs]
    a_rows = _round_up(sum(w_loc[n].size for n in SMALL), SUBLANES * FLAT_W) // FLAT_W
    pack_small = lambda d: _flat_cat([d[n] for n in SMALL], a_rows * FLAT_W).reshape(a_rows, FLAT_W)
    outs = _adamw_flat(pack_small(w_loc), pack_small(g_tot), pack_small(m_loc), pack_small(v_loc))
    shapes = [w_loc[n].shape for n in SMALL]
    for d, o in zip((delta, new_m, new_v), outs):
        d.update(zip(SMALL, _split_flat(o.reshape(-1), shapes)))
    return (loss, grad_x, *[g_tot[n] for n in WEIGHTS], *[delta[n] for n in WEIGHTS],
            *[new_m[n] for n in WEIGHTS], *[new_v[n] for n in WEIGHTS])
```

```python
import math

import jax
import jax.numpy as jnp
from jax import lax
from jax.experimental import pallas as pl
from jax.experimental.pallas import tpu as pltpu

F32 = jnp.float32
BF16 = jnp.bfloat16

D_MODEL = 1024
DEPTH = 2
N_META = 16
LRU_W = 512
LRU_HEADS = 8
LRU_HD = 64
LRU_C = 8.0
GLA_W = 512
GLA_H = 4
GLA_DK = 64
GLA_DV = 128
GLA_RANK = 16
GLA_GATE_NORM = 16.0
D_FF = 4096
D_IN = 2592
EPS = 1e-6
ADAM_LR, ADAM_B1, ADAM_B2, ADAM_EPS, ADAM_WD, ADAM_STEP = 0.001, 0.9, 0.999, 1e-08, 0.01, 10

LANES = 128
SUBLANES = 8
FRONT = 128
BACK = 128
D_IN_PAD = 2688
ZG_OFF = 2560
CHUNK = 64
EXP_CLAMP = 80.0
VMEM_LIMIT = 56 * 1024 * 1024
MESH_ID = pl.DeviceIdType.MESH

NN = (((1,), (0,)), ((), ()))
NT = (((1,), (1,)), ((), ()))
TN = (((0,), (0,)), ((), ()))


def _dot(a, b, dims=NN):
    return lax.dot_general(a, b, dims, preferred_element_type=F32)


def _cparams(sem):
    return pltpu.CompilerParams(dimension_semantics=sem, vmem_limit_bytes=VMEM_LIMIT)


def _sigmoid(x):
    return 1.0 / (1.0 + jnp.exp(-x))


def _gelu(x):
    c = math.sqrt(2.0 / math.pi)
    return 0.5 * x * (1.0 + jnp.tanh(c * (x + 0.044715 * x * x * x)))


def _gelu_grad(x):
    c = math.sqrt(2.0 / math.pi)
    t = jnp.tanh(c * (x + 0.044715 * x * x * x))
    return 0.5 * (1.0 + t) + 0.5 * x * (1.0 - t * t) * c * (1.0 + 3.0 * 0.044715 * x * x)


def _rms_fwd(x, g):
    rstd = lax.rsqrt(jnp.mean(x * x, axis=1, keepdims=True) + EPS)
    return (x * rstd) * g


def _rms_bwd(x, g, dy):
    rstd = lax.rsqrt(jnp.mean(x * x, axis=1, keepdims=True) + EPS)
    xh = x * rstd
    dxh = dy * g
    dx = rstd * (dxh - xh * jnp.mean(dxh * xh, axis=1, keepdims=True))
    dg = jnp.sum(dy * xh, axis=0, keepdims=True)
    return dx, dg


def _mm(a, b, mode, tm, tn, tk, outs, name, epilogue=None, extras=(), col_blocks_first=False):
    if mode == "nn":
        (m, k), n = a.shape, b.shape[1]
        a_spec = pl.BlockSpec((tm, tk), lambda i, j, kk: (i, kk))
        b_spec = pl.BlockSpec((tk, tn), lambda i, j, kk: (kk, j))
        dims = NN
    elif mode == "nt":
        (m, k), n = a.shape, b.shape[0]
        a_spec = pl.BlockSpec((tm, tk), lambda i, j, kk: (i, kk))
        b_spec = pl.BlockSpec((tn, tk), lambda i, j, kk: (j, kk))
        dims = NT
    else:
        (k, m), n = a.shape, b.shape[1]
        a_spec = pl.BlockSpec((tk, tm), lambda i, j, kk: (kk, i))
        b_spec = pl.BlockSpec((tk, tn), lambda i, j, kk: (kk, j))
        dims = TN
    assert m % tm == 0 and n % tn == 0 and k % tk == 0, (name, m, n, k)
    nk = k // tk
    ne = len(extras)
    e_specs = [pl.BlockSpec((tm, tn), lambda i, j, kk: (i, j)) for _ in extras]
    o_specs, o_shapes = [], []
    for width, dtype in outs:
        if col_blocks_first:
            assert width == n, name
            o_specs.append(pl.BlockSpec((None, tm, tn), lambda i, j, kk: (j, i, 0)))
            o_shapes.append(jax.ShapeDtypeStruct((n // tn, m, tn), dtype))
            continue
        if width == n:
            o_specs.append(pl.BlockSpec((tm, tn), lambda i, j, kk: (i, j)))
        else:
            assert n == tn, name
            o_specs.append(pl.BlockSpec((tm, width), lambda i, j, kk: (i, 0)))
        o_shapes.append(jax.ShapeDtypeStruct((m, width), dtype))

    def finish(acc, extra_refs, out_refs):
        res = epilogue(acc, *[e[...] for e in extra_refs]) if epilogue else (acc,)
        for o, r in zip(out_refs, res):
            o[...] = r.astype(o.dtype)

    if nk == 1:
        def body(a_ref, b_ref, *rest):
            finish(_dot(a_ref[...], b_ref[...], dims), rest[:ne], rest[ne:])
        scratch = []
    else:
        def body(a_ref, b_ref, *rest):
            acc_ref = rest[-1]
            kk = pl.program_id(2)

            @pl.when(kk == 0)
            def _():
                acc_ref[...] = jnp.zeros_like(acc_ref)

            acc_ref[...] += _dot(a_ref[...], b_ref[...], dims)

            @pl.when(kk == nk - 1)
            def _():
                finish(acc_ref[...], rest[:ne], rest[ne:-1])
        scratch = [pltpu.VMEM((tm, tn), F32)]

    return pl.pallas_call(
        body, name=name, grid=(m // tm, n // tn, nk),
        in_specs=[a_spec, b_spec] + e_specs, out_specs=o_specs, out_shape=o_shapes,
        scratch_shapes=scratch,
        compiler_params=_cparams(("parallel", "parallel", "arbitrary")),
    )(a, b, *extras)


def _rows(body, n_rows, tm, ins, consts, outs, accs, name):
    assert n_rows % tm == 0, (name, n_rows, tm)
    in_specs = [pl.BlockSpec((tm, w), (lambda i, cb=cb: (i, cb))) for _, w, cb in ins]
    in_specs += [pl.BlockSpec(c.shape, lambda i: (0, 0)) for c in consts]
    out_specs = [pl.BlockSpec((tm, w), lambda i: (i, 0)) for w, _ in outs]
    out_specs += [pl.BlockSpec(s, lambda i: (0, 0)) for s, _ in accs]
    out_shape = [jax.ShapeDtypeStruct((n_rows, w), d) for w, d in outs]
    out_shape += [jax.ShapeDtypeStruct(s, d) for s, d in accs]
    ni, nc, no = len(ins), len(consts), len(outs)

    def kern(*refs):
        body(pl.program_id(0), refs[:ni], refs[ni:ni + nc], refs[ni + nc:ni + nc + no], refs[ni + nc + no:])

    res = pl.pallas_call(
        kern, name=name, grid=(n_rows // tm,), in_specs=in_specs, out_specs=out_specs, out_shape=out_shape,
        compiler_params=_cparams(("arbitrary",)),
    )(*[a for a, _, _ in ins], *consts)
    return res


def _acc_add(i, ref, val):
    @pl.when(i == 0)
    def _():
        ref[...] = jnp.zeros_like(ref)

    ref[...] += val


def _norm_cast(h, g, t_rows, name):
    def body(i, ins, consts, outs, accs):
        outs[0][...] = _rms_fwd(ins[0][...], consts[0][...]).astype(BF16)
    return _rows(body, t_rows, 384, [(h, D_MODEL, 0)], [g], [(D_MODEL, BF16)], [], name)[0]


def _norm_residual(y, g, h, t_rows, name):
    def body(i, ins, consts, outs, accs):
        outs[0][...] = ins[1][...] + _rms_fwd(ins[0][...], consts[0][...])
    return _rows(body, t_rows, 384, [(y, D_MODEL, 0), (h, D_MODEL, 0)], [g], [(D_MODEL, F32)], [], name)[0]


def _norm_residual_bwd(dh, y, g, t_rows, name):
    def body(i, ins, consts, outs, accs):
        dy, dg = _rms_bwd(ins[1][...], consts[0][...], ins[0][...])
        outs[0][...] = dy.astype(BF16)
        _acc_add(i, accs[0], dg)
    return _rows(body, t_rows, 384, [(dh, D_MODEL, 0), (y, D_MODEL, 0)], [g],
                 [(D_MODEL, BF16)], [((1, D_MODEL), F32)], name)


def _norm_bwd_add(dn, x, g, dh, t_rows, name):
    def body(i, ins, consts, outs, accs):
        dx, dg = _rms_bwd(ins[1][...], consts[0][...], ins[0][...])
        outs[0][...] = ins[2][...] + dx
        _acc_add(i, accs[0], dg)
    return _rows(body, t_rows, 384, [(dn, D_MODEL, 0), (x, D_MODEL, 0), (dh, D_MODEL, 0)], [g],
                 [(D_MODEL, F32)], [((1, D_MODEL), F32)], name)


def _head_norm_parts(o):
    ns, rs = [], []
    for hd in range(GLA_H):
        oh = o[:, hd * GLA_DV:(hd + 1) * GLA_DV]
        r = lax.rsqrt(jnp.mean(oh * oh, axis=1, keepdims=True) + EPS)
        ns.append(oh * r)
        rs.append(r)
    return ns, rs


def _mix_fwd(h_f, h_b, z_lru, o_f, o_b, z_go, head_norm, t_rows, name):
    def body(i, ins, consts, outs, accs):
        hf, hb, gate, of, ob, go = [r[...] for r in ins]
        hn = consts[0][...]
        outs[0][:, 0:LRU_W] = ((hf + hb) * _gelu(gate)).astype(BF16)
        ns, _ = _head_norm_parts(of + ob)
        silu = go * _sigmoid(go)
        for hd in range(GLA_H):
            sl = slice(hd * GLA_DV, (hd + 1) * GLA_DV)
            outs[0][:, LRU_W + hd * GLA_DV:LRU_W + (hd + 1) * GLA_DV] = (ns[hd] * hn[:, sl] * silu[:, sl]).astype(BF16)
    ins = [(h_f, LRU_W, 0), (h_b, LRU_W, 0), (z_lru, LRU_W, 1), (o_f, GLA_W, 0), (o_b, GLA_W, 0), (z_go, GLA_W, 0)]
    return _rows(body, t_rows, 384, ins, [head_norm], [(D_MODEL, BF16)], [], name)[0]


def _mix_bwd(dymix, h_f, h_b, z_lru, o_f, o_b, z_go, head_norm, t_rows, name):
    def body(i, ins, consts, outs, accs):
        dyl, dyg, hf, hb, gate, of, ob, go = [r[...] for r in ins]
        hn = consts[0][...]
        outs[0][...] = dyl * _gelu(gate)
        outs[1][...] = dyl * (hf + hb) * _gelu_grad(gate)
        ns, rs = _head_norm_parts(of + ob)
        sg = _sigmoid(go)
        silu = go * sg
        dsilu = sg * (1.0 + go * (1.0 - sg))
        dhn = []
        for hd in range(GLA_H):
            sl = slice(hd * GLA_DV, (hd + 1) * GLA_DV)
            dy = dyg[:, sl]
            e = dy * hn[:, sl] * silu[:, sl]
            outs[2][:, sl] = rs[hd] * (e - ns[hd] * jnp.mean(e * ns[hd], axis=1, keepdims=True))
            outs[3][:, sl] = dy * ns[hd] * hn[:, sl] * dsilu[:, sl]
            dhn.append(jnp.sum(dy * ns[hd] * silu[:, sl], axis=0, keepdims=True))
        _acc_add(i, accs[0], jnp.concatenate(dhn, axis=1))
    ins = [(dymix, LRU_W, 0), (dymix, GLA_W, 1), (h_f, LRU_W, 0), (h_b, LRU_W, 0), (z_lru, LRU_W, 1),
           (o_f, GLA_W, 0), (o_b, GLA_W, 0), (z_go, GLA_W, 0)]
    return _rows(body, t_rows, 384, ins, [head_norm],
                 [(LRU_W, F32), (LRU_W, F32), (GLA_W, F32), (GLA_W, F32)], [((1, GLA_W), F32)], name)


def _dz_assemble(dx_br, dgate, gla_grads, dgo, z_zg, wg, t_rows, name):
    dq_f, dk_f, dv_f, dpre_f, dq_b, dk_b, dv_b, dpre_b = gla_grads
    qk = GLA_H * GLA_DK

    def body(i, ins, consts, outs, accs):
        (dxb, dgt, dqf, dqb, dkf, dkb, dvf, dvb, dg_o, dpf, dpb, zg) = [r[...] for r in ins]
        w = consts[0][...]
        o = outs[0]
        o[:, 0:LRU_W] = dxb.astype(BF16)
        o[:, LRU_W:2 * LRU_W] = dgt.astype(BF16)
        o[:, 1024:1024 + qk] = ((dqf + dqb) * (GLA_DK ** -0.5)).astype(BF16)
        o[:, 1280:1280 + qk] = (dkf + dkb).astype(BF16)
        o[:, 1536:1536 + GLA_W] = (dvf + dvb).astype(BF16)
        o[:, 2048:2048 + GLA_W] = dg_o.astype(BF16)
        dpre = jnp.concatenate([dpf, dpb], axis=1)
        dpre16 = dpre.astype(BF16)
        o[:, ZG_OFF:ZG_OFF + LANES] = _dot(dpre16, w, NT).astype(BF16)
        _acc_add(i, accs[0], _dot(zg.astype(BF16), dpre16, TN))
        _acc_add(i, accs[1], jnp.sum(dpre, axis=0, keepdims=True))

    ins = [(dx_br, LRU_W, 0), (dgate, LRU_W, 0), (dq_f, qk, 0), (dq_b, qk, 0), (dk_f, qk, 0), (dk_b, qk, 0),
           (dv_f, GLA_W, 0), (dv_b, GLA_W, 0), (dgo, GLA_W, 0), (dpre_f, qk, 0), (dpre_b, qk, 0), (z_zg, LANES, 0)]
    return _rows(body, t_rows, 384, ins, [wg], [(D_IN_PAD, BF16)],
                 [((LANES, 2 * qk), F32), ((1, 2 * qk), F32)], name)


def _loss_and_grad(h, target, t_rows, seq):
    tm = FRONT
    first, n_t = FRONT // tm, seq // tm

    def kern(h_ref, t_ref, dh_ref, loss_ref):
        i = pl.program_id(0)
        valid = jnp.logical_and(i >= first, i < first + n_t)

        @pl.when(i == 0)
        def _():
            loss_ref[...] = jnp.zeros_like(loss_ref)

        @pl.when(valid)
        def _():
            err = h_ref[...] - t_ref[...]
            dh_ref[...] = err * (1.0 / D_MODEL)
            loss_ref[...] += jnp.sum(err * err) * (0.5 / D_MODEL)

        @pl.when(jnp.logical_not(valid))
        def _():
            dh_ref[...] = jnp.zeros_like(dh_ref)

    return pl.pallas_call(
        kern, name="loss", grid=(t_rows // tm,),
        in_specs=[pl.BlockSpec((tm, D_MODEL), lambda i: (i, 0)),
                  pl.BlockSpec((tm, D_MODEL), lambda i: (jnp.clip(i - first, 0, n_t - 1), 0))],
        out_specs=[pl.BlockSpec((tm, D_MODEL), lambda i: (i, 0)), pl.BlockSpec((SUBLANES, LANES), lambda i: (0, 0))],
        out_shape=[jax.ShapeDtypeStruct((t_rows, D_MODEL), F32), jax.ShapeDtypeStruct((SUBLANES, LANES), F32)],
        compiler_params=_cparams(("arbitrary",)),
    )(h, target)


LRU_RB = 64
HALO = SUBLANES


def _scan8(a, u, rev):
    rows = lax.broadcasted_iota(jnp.int32, a.shape, 0)
    for d in (1, 2, 4):
        if rev:
            a_s, u_s, ok = pltpu.roll(a, SUBLANES - d, 0), pltpu.roll(u, SUBLANES - d, 0), rows < SUBLANES - d
        else:
            a_s, u_s, ok = pltpu.roll(a, d, 0), pltpu.roll(u, d, 0), rows >= d
        u = jnp.where(ok, a * u_s + u, u)
        a = jnp.where(ok, a * a_s, a)
    return a, u


def _edge_row(x, rev):
    r = x[0:1, :] if rev else x[SUBLANES - 1:SUBLANES, :]
    return jnp.broadcast_to(r, x.shape)


def _scan_block(a, u, carry, rev):
    nsub = a.shape[0] // SUBLANES
    loc = [_scan8(a[sb * SUBLANES:(sb + 1) * SUBLANES], u[sb * SUBLANES:(sb + 1) * SUBLANES], rev) for sb in range(nsub)]
    edges = [(_edge_row(a_c, rev), _edge_row(h_l, rev)) for a_c, h_l in loc]
    carries = [None] * nsub
    for sb in (range(nsub - 1, -1, -1) if rev else range(nsub)):
        carries[sb] = carry
        carry = edges[sb][0] * carry + edges[sb][1]
    h = jnp.concatenate([h_l + a_c * cin for (a_c, h_l), cin in zip(loc, carries)], axis=0)
    return h, carry


def _lru_gates(xc, pre_a, pre_x, ba, bx, sp8):
    r = _sigmoid(pre_a + ba)
    i = _sigmoid(pre_x + bx)
    log_a = -(r * sp8)
    a = jnp.exp(log_a)
    y = 2.0 * log_a
    series = -y * (1.0 + y * (0.5 + y * (1.0 / 6.0 + y * (1.0 / 24.0 + y * (1.0 / 120.0)))))
    om = jnp.where(y > -0.25, series, 1.0 - a * a)
    s = jnp.sqrt(om)
    return r, i, a, s


def _softplus(x):
    return jnp.maximum(x, 0.0) + jnp.log(1.0 + jnp.exp(-jnp.abs(x)))


def _conv_taps(xpad_ref, r0, nrows, shifts=(-2, -1, 0, 1)):
    ext = xpad_ref[pl.ds(r0, nrows + 2 * HALO), :]
    taps = []
    for s in shifts:
        sh = ext if s == 0 else pltpu.roll(ext, (-s) % (nrows + 2 * HALO), 0)
        taps.append(sh[HALO:HALO + nrows])
    return taps


def _row_mask(r0, nrows, seq):
    rows = r0 + lax.broadcasted_iota(jnp.int32, (nrows, LANES), 0)
    return jnp.logical_and(rows >= FRONT - N_META, rows < FRONT + seq).astype(F32)


def _lru_load_conv(z_hbm, xpad, xc, sem, cw_ref, cb_ref, t_rows, seq):
    j = pl.program_id(0)
    zeros = jnp.zeros((HALO, LANES), F32)
    xpad[0:HALO, :] = zeros
    xpad[t_rows + HALO:t_rows + 2 * HALO, :] = zeros
    cp = pltpu.make_async_copy(z_hbm.at[:, pl.ds(pl.multiple_of(j * LANES, LANES), LANES)],
                               xpad.at[pl.ds(HALO, t_rows)], sem)
    cp.start()
    cp.wait()
    cw = cw_ref[...]
    cb = cb_ref[...]

    def conv_blk(r, carry):
        r0 = pl.multiple_of(r * LRU_RB, LRU_RB)
        taps = _conv_taps(xpad, r0, LRU_RB)
        acc = cb + cw[0:1, :] * taps[0]
        for k in range(1, 4):
            acc = acc + cw[k:k + 1, :] * taps[k]
        xc[pl.ds(r0, LRU_RB), :] = acc * _row_mask(r0, LRU_RB, seq)
        return carry

    lax.fori_loop(0, t_rows // LRU_RB, conv_blk, 0)


def _lru_specs(t_rows):
    return [pl.BlockSpec(memory_space=pl.ANY),
            pl.BlockSpec((4, LANES), lambda j: (0, j)),
            pl.BlockSpec((1, LANES), lambda j: (0, j)),
            pl.BlockSpec((1, LANES, 4 * LANES), lambda j: (j, 0, 0)),
            pl.BlockSpec((SUBLANES, LANES), lambda j: (0, j))]


def _lru_fwd(z_lru, conv_w, conv_b, wg, pb, t_rows, seq, name):
    nblk = t_rows // LRU_RB
    nsub = LRU_RB // SUBLANES

    def kern(z_hbm, cw_ref, cb_ref, wg_ref, pb_ref, hf_ref, hb_ref, xpad, xc, sem):
        _lru_load_conv(z_hbm, xpad, xc, sem, cw_ref, cb_ref, t_rows, seq)
        w = wg_ref[0]
        pb_v = pb_ref[...]
        dirs = []
        for rev in (False, True):
            o = 2 if rev else 0
            dirs.append(dict(rev=rev, ba=pb_v[o:o + 1, :], bx=pb_v[o + 1:o + 2, :],
                             sp8=LRU_C * _softplus(-pb_v[5:6, :] if rev else -pb_v[4:5, :]),
                             w=w[:, o * LANES:(o + 2) * LANES], out=hb_ref if rev else hf_ref))

        def blk(it, carries):
            new = []
            for d, carry in zip(dirs, carries):
                r = (nblk - 1 - it) if d["rev"] else it
                r0 = pl.multiple_of(r * LRU_RB, LRU_RB)
                xcb = xc[pl.ds(r0, LRU_RB), :]
                pre = _dot(xcb.astype(BF16), d["w"])
                _, gi, a, s = _lru_gates(xcb, pre[:, 0:LANES], pre[:, LANES:2 * LANES], d["ba"], d["bx"], d["sp8"])
                h, carry = _scan_block(a, s * (gi * xcb), carry, d["rev"])
                d["out"][pl.ds(r0, LRU_RB), :] = h
                new.append(carry)
            return tuple(new)

        z8 = jnp.zeros((SUBLANES, LANES), F32)
        lax.fori_loop(0, nblk, blk, (z8, z8), unroll=2)

    return pl.pallas_call(
        kern, name=name, grid=(LRU_W // LANES,), in_specs=_lru_specs(t_rows),
        out_specs=[pl.BlockSpec((t_rows, LANES), lambda j: (0, j))] * 2,
        out_shape=[jax.ShapeDtypeStruct((t_rows, LRU_W), F32)] * 2,
        scratch_shapes=[pltpu.VMEM((t_rows + 2 * HALO, LANES), F32), pltpu.VMEM((t_rows, LANES), F32),
                        pltpu.SemaphoreType.DMA],
        compiler_params=_cparams(("arbitrary",)),
    )(z_lru, conv_w, conv_b, wg, pb)


def _lru_bwd(z_lru, conv_w, conv_b, wg, pb, h_f, h_b, dh, t_rows, seq, name):
    nblk = t_rows // LRU_RB
    nsub = LRU_RB // SUBLANES

    def kern(z_hbm, cw_ref, cb_ref, wg_ref, pb_ref, hf_ref, hb_ref, dh_ref,
             dx_ref, dcw_ref, dcb_ref, dwg_ref, dpb_ref, xpad, xc, dxc, sem):
        _lru_load_conv(z_hbm, xpad, xc, sem, cw_ref, cb_ref, t_rows, seq)
        w = wg_ref[0]
        pb_v = pb_ref[...]
        zeros = jnp.zeros((HALO, LANES), F32)
        dxc[0:HALO, :] = zeros
        dxc[t_rows + HALO:t_rows + 2 * HALO, :] = zeros
        dwg_ref[...] = jnp.zeros_like(dwg_ref)

        def zero_blk(r, carry):
            dxc[pl.ds(pl.multiple_of(r * LRU_RB, LRU_RB) + HALO, LRU_RB), :] = jnp.zeros((LRU_RB, LANES), F32)
            return carry

        lax.fori_loop(0, nblk, zero_blk, 0)
        dirs = []
        for rev in (False, True):
            o = 2 if rev else 0
            lam = pb_v[5:6, :] if rev else pb_v[4:5, :]
            dirs.append(dict(rev=rev, o=o, ba=pb_v[o:o + 1, :], bx=pb_v[o + 1:o + 2, :], sp8=LRU_C * _softplus(-lam),
                             dlam_scale=LRU_C * _sigmoid(-lam), h_ref=hb_ref if rev else hf_ref,
                             w=w[:, o * LANES:(o + 2) * LANES]))
        rows_b = lax.broadcasted_iota(jnp.int32, (LRU_RB, LANES), 0)

        def blk(it, carries):
            new = []
            for d, (nu_c, acc_ba, acc_bx, acc_lam) in zip(dirs, carries):
                rev, o, h_ref = d["rev"], d["o"], d["h_ref"]
                adj_rev = not rev
                r = (nblk - 1 - it) if adj_rev else it
                r0 = pl.multiple_of(r * LRU_RB, LRU_RB)
                xcb = xc[pl.ds(r0, LRU_RB), :]
                xc16 = xcb.astype(BF16)
                pre = _dot(xc16, d["w"])
                gr, gi, a, s = _lru_gates(xcb, pre[:, 0:LANES], pre[:, LANES:2 * LANES], d["ba"], d["bx"], d["sp8"])
                dhb = dh_ref[pl.ds(r0, LRU_RB), :]
                hb = h_ref[pl.ds(r0, LRU_RB), :]
                if rev:
                    nxt0 = pl.multiple_of(jnp.minimum(r0 + LRU_RB, t_rows - SUBLANES), SUBLANES)
                    edge = h_ref[pl.ds(nxt0, SUBLANES), :][0:1, :] * (r < nblk - 1).astype(F32)
                    h_prev = jnp.concatenate([hb[1:], edge], axis=0)
                else:
                    prv0 = pl.multiple_of(jnp.maximum(r0 - SUBLANES, 0), SUBLANES)
                    edge = h_ref[pl.ds(prv0, SUBLANES), :][SUBLANES - 1:SUBLANES, :] * (r > 0).astype(F32)
                    h_prev = jnp.concatenate([edge, hb[:-1]], axis=0)
                nu, nu_out = _scan_block(a, a * dhb, nu_c, adj_rev)
                cin = jnp.broadcast_to(nu_c[0:1, :], (LRU_RB, LANES))
                if adj_rev:
                    nu_next = jnp.where(rows_b == LRU_RB - 1, cin, pltpu.roll(nu, LRU_RB - 1, 0))
                else:
                    nu_next = jnp.where(rows_b == 0, cin, pltpu.roll(nu, 1, 0))
                du = dhb + nu_next
                da = du * h_prev
                ix = gi * xcb
                dlog = da * a - du * ix * (a * a) / s
                d_i = du * s * xcb
                dr = -dlog * d["sp8"]
                dpa = dr * gr * (1.0 - gr)
                dpx = d_i * gi * (1.0 - gi)
                dp16 = jnp.concatenate([dpa, dpx], axis=1).astype(BF16)
                dxc[pl.ds(r0 + HALO, LRU_RB), :] += du * s * gi + _dot(dp16, d["w"], NT)
                dwg_ref[0, :, o * LANES:(o + 2) * LANES] += _dot(xc16, dp16, TN)
                new.append((nu_out, acc_ba + jnp.sum(dpa, axis=0, keepdims=True),
                            acc_bx + jnp.sum(dpx, axis=0, keepdims=True),
                            acc_lam + jnp.sum(dlog * gr, axis=0, keepdims=True)))
            return tuple(new)

        z1 = jnp.zeros((1, LANES), F32)
        init = (jnp.zeros((SUBLANES, LANES), F32), z1, z1, z1)
        (_, ba_f, bx_f, lam_f), (_, ba_b, bx_b, lam_b) = lax.fori_loop(0, nblk, blk, (init, init), unroll=2)
        dpb_ref[...] = jnp.concatenate([ba_f, bx_f, ba_b, bx_b, lam_f * dirs[0]["dlam_scale"],
                                        lam_b * dirs[1]["dlam_scale"], z1, z1], axis=0)

        def mask_blk(r, carry):
            r0 = pl.multiple_of(r * LRU_RB, LRU_RB)
            dxc[pl.ds(r0 + HALO, LRU_RB), :] = dxc[pl.ds(r0 + HALO, LRU_RB), :] * _row_mask(r0, LRU_RB, seq)
            return carry

        lax.fori_loop(0, nblk, mask_blk, 0)

        cw = cw_ref[...]

        def conv_bwd(r, carry):
            r0 = pl.multiple_of(r * LRU_RB, LRU_RB)
            gt = _conv_taps(dxc, r0, LRU_RB, (2, 1, 0, -1))
            xt = _conv_taps(xpad, r0, LRU_RB)
            dx_ref[pl.ds(r0, LRU_RB), :] = (cw[0:1, :] * gt[0] + cw[1:2, :] * gt[1] + cw[2:3, :] * gt[2]
                                           + cw[3:4, :] * gt[3])
            g = gt[2]
            new = [carry[k] + jnp.sum(g * xt[k], axis=0, keepdims=True) for k in range(4)]
            new.append(carry[4] + jnp.sum(g, axis=0, keepdims=True))
            return tuple(new)

        z1 = jnp.zeros((1, LANES), F32)
        sums = lax.fori_loop(0, nblk, conv_bwd, (z1, z1, z1, z1, z1))
        dcw_ref[...] = jnp.concatenate(sums[:4], axis=0)
        dcb_ref[...] = sums[4]

    col = lambda j: (0, j)
    return pl.pallas_call(
        kern, name=name, grid=(LRU_W // LANES,),
        in_specs=_lru_specs(t_rows) + [pl.BlockSpec((t_rows, LANES), col)] * 3,
        out_specs=[pl.BlockSpec((t_rows, LANES), col), pl.BlockSpec((4, LANES), col), pl.BlockSpec((1, LANES), col),
                   pl.BlockSpec((1, LANES, 4 * LANES), lambda j: (j, 0, 0)), pl.BlockSpec((SUBLANES, LANES), col)],
        out_shape=[jax.ShapeDtypeStruct((t_rows, LRU_W), F32), jax.ShapeDtypeStruct((4, LRU_W), F32),
                   jax.ShapeDtypeStruct((1, LRU_W), F32), jax.ShapeDtypeStruct((4, LANES, 4 * LANES), F32),
                   jax.ShapeDtypeStruct((SUBLANES, LRU_W), F32)],
        scratch_shapes=[pltpu.VMEM((t_rows + 2 * HALO, LANES), F32), pltpu.VMEM((t_rows, LANES), F32),
                        pltpu.VMEM((t_rows + 2 * HALO, LANES), F32), pltpu.SemaphoreType.DMA],
        compiler_params=_cparams(("arbitrary",)),
    )(z_lru, conv_w, conv_b, wg, pb, h_f, h_b, dh)


QK = GLA_H * GLA_DK


def _cumsum_rows(x, rev):
    n = x.shape[0]
    rows = lax.broadcasted_iota(jnp.int32, x.shape, 0)
    d = 1
    while d < n:
        if rev:
            x = x + jnp.where(rows < n - d, pltpu.roll(x, n - d, 0), 0.0)
        else:
            x = x + jnp.where(rows >= d, pltpu.roll(x, d, 0), 0.0)
        d *= 2
    return x


def _gla_chunk_terms(q, k, zg, wg, bg, rev):
    pre = (_dot(zg.astype(BF16), wg) + bg)[:, (QK if rev else 0):(2 * QK if rev else QK)]
    g = (jnp.minimum(pre, 0.0) - jnp.log(1.0 + jnp.exp(-jnp.abs(pre)))) * (1.0 / GLA_GATE_NORM)
    b = _cumsum_rows(g, rev)
    b_last = b[0:1, :] if rev else b[CHUNK - 1:CHUNK, :]
    b_mid = b[CHUNK // 2:CHUNK // 2 + 1, :]
    qs = q * (GLA_DK ** -0.5)
    terms = dict(
        pre=pre, qs=qs, k=k,
        eb=jnp.exp(b), ek=jnp.exp(b_last - b), e_last=jnp.exp(b_last),
        eqm=jnp.exp(jnp.minimum(b - b_mid, EXP_CLAMP)), ekm=jnp.exp(jnp.minimum(b_mid - b, EXP_CLAMP)))
    return terms


def _gla_mask(rev):
    t = lax.broadcasted_iota(jnp.int32, (CHUNK, CHUNK), 0)
    s = lax.broadcasted_iota(jnp.int32, (CHUNK, CHUNK), 1)
    return (s > t) if rev else (s <= t)


def _gla_head_ops(tm, hd):
    sl = slice(hd * GLA_DK, (hd + 1) * GLA_DK)
    q_b = (tm["qs"][:, sl] * tm["eb"][:, sl]).astype(BF16)
    k_e = (tm["k"][:, sl] * tm["ek"][:, sl]).astype(BF16)
    q_m = (tm["qs"][:, sl] * tm["eqm"][:, sl]).astype(BF16)
    k_m = (tm["k"][:, sl] * tm["ekm"][:, sl]).astype(BF16)
    return sl, q_b, k_e, q_m, k_m


def _gla_fwd(z_qk, z_v, z_zg, wg, bg, t_rows, name):
    nc = t_rows // CHUNK

    def kern(qf, kf, vf, zf, qb, kb, vb, zb, wg_ref, bg_ref, of_ref, ob_ref, sf_ref, sb_ref, st_f, st_b):
        i = pl.program_id(0)

        @pl.when(i == 0)
        def _():
            st_f[...] = jnp.zeros_like(st_f)
            st_b[...] = jnp.zeros_like(st_b)

        wg_v, bg_v = wg_ref[...], bg_ref[...]
        for rev, (q_r, k_r, v_r, z_r, o_r, s_r, st) in ((False, (qf, kf, vf, zf, of_ref, sf_ref, st_f)),
                                                        (True, (qb, kb, vb, zb, ob_ref, sb_ref, st_b))):
            tm = _gla_chunk_terms(q_r[...], k_r[...], z_r[...], wg_v, bg_v, rev)
            mask = _gla_mask(rev)
            v = v_r[...]
            for hd in range(GLA_H):
                sl, q_b, k_e, q_m, k_m = _gla_head_ops(tm, hd)
                vs = slice(hd * GLA_DV, (hd + 1) * GLA_DV)
                v16 = v[:, vs].astype(BF16)
                a = jnp.where(mask, _dot(q_m, k_m, NT), 0.0).astype(BF16)
                s_in = st[hd]
                s_r[0, hd] = s_in
                o_r[:, vs] = _dot(a, v16) + _dot(q_b, s_in.astype(BF16), NT)
                st[hd] = s_in * tm["e_last"][:, sl] + _dot(v16, k_e, TN)

    fw = lambda c: (lambda i: (i, c))
    rv = lambda c: (lambda i: (nc - 1 - i, c))
    def side(ix):
        return [pl.BlockSpec((CHUNK, QK), ix(0)), pl.BlockSpec((CHUNK, QK), ix(1)),
                pl.BlockSpec((CHUNK, GLA_W), ix(0)), pl.BlockSpec((CHUNK, LANES), ix(0))]
    st_shape = (nc, GLA_H, GLA_DV, GLA_DK)
    return pl.pallas_call(
        kern, name=name, grid=(nc,),
        in_specs=side(fw) + side(rv) + [pl.BlockSpec(wg.shape, lambda i: (0, 0)), pl.BlockSpec(bg.shape, lambda i: (0, 0))],
        out_specs=[pl.BlockSpec((CHUNK, GLA_W), fw(0)), pl.BlockSpec((CHUNK, GLA_W), rv(0)),
                   pl.BlockSpec((1,) + st_shape[1:], lambda i: (i, 0, 0, 0)),
                   pl.BlockSpec((1,) + st_shape[1:], lambda i: (nc - 1 - i, 0, 0, 0))],
        out_shape=[jax.ShapeDtypeStruct((t_rows, GLA_W), F32)] * 2 + [jax.ShapeDtypeStruct(st_shape, F32)] * 2,
        scratch_shapes=[pltpu.VMEM(st_shape[1:], F32)] * 2,
        compiler_params=_cparams(("arbitrary",)),
    )(z_qk, z_qk, z_v, z_zg, z_qk, z_qk, z_v, z_zg, wg, bg)


def _gla_bwd(z_qk, z_v, z_zg, wg, bg, do, s_f, s_b, t_rows, name):
    nc = t_rows // CHUNK

    def kern(qf, kf, vf, zf, dof, ssf, qb, kb, vb, zb, dob, ssb, wg_ref, bg_ref,
             dqf, dkf, dvf, dpf, dqb, dkb, dvb, dpb, ds_f, ds_b):
        i = pl.program_id(0)

        @pl.when(i == 0)
        def _():
            ds_f[...] = jnp.zeros_like(ds_f)
            ds_b[...] = jnp.zeros_like(ds_b)

        wg_v, bg_v = wg_ref[...], bg_ref[...]
        sides = ((False, (qf, kf, vf, zf, dof, ssf, dqf, dkf, dvf, dpf, ds_f)),
                 (True, (qb, kb, vb, zb, dob, ssb, dqb, dkb, dvb, dpb, ds_b)))
        for rev, (q_r, k_r, v_r, z_r, do_r, ss_r, dq_r, dk_r, dv_r, dp_r, ds) in sides:
            tm = _gla_chunk_terms(q_r[...], k_r[...], z_r[...], wg_v, bg_v, rev)
            mask = _gla_mask(rev)
            v = v_r[...]
            d_o = do_r[...]
            db_parts, cross_parts = [], []
            for hd in range(GLA_H):
                sl, q_b, k_e, q_m, k_m = _gla_head_ops(tm, hd)
                vs = slice(hd * GLA_DV, (hd + 1) * GLA_DV)
                v16, do16 = v[:, vs].astype(BF16), d_o[:, vs].astype(BF16)
                a = jnp.where(mask, _dot(q_m, k_m, NT), 0.0).astype(BF16)
                da = jnp.where(mask, _dot(do16, v16, NT), 0.0).astype(BF16)
                s_in = ss_r[0, hd]
                s_in16 = s_in.astype(BF16)
                ds_out = ds[hd]
                ds16 = ds_out.astype(BF16)
                dv_r[:, vs] = _dot(a, do16, TN) + _dot(k_e, ds16, NT)
                dqs = _dot(da, k_m) * tm["eqm"][:, sl] + _dot(do16, s_in16) * tm["eb"][:, sl]
                dk = _dot(da, q_m, TN) * tm["ekm"][:, sl] + _dot(v16, ds16) * tm["ek"][:, sl]
                ds[hd] = ds_out * tm["e_last"][:, sl] + _dot(do16, q_b, TN)
                dq_r[:, sl] = dqs
                dk_r[:, sl] = dk
                db_parts.append(tm["qs"][:, sl] * dqs - tm["k"][:, sl] * dk)
                s_out = s_in * tm["e_last"][:, sl] + _dot(v16, k_e, TN)
                cross_parts.append(jnp.sum(s_out * ds_out, axis=0, keepdims=True))
            d_b = jnp.concatenate(db_parts, axis=1)
            dg = _cumsum_rows(d_b, not rev) + jnp.concatenate(cross_parts, axis=1)
            dp_r[...] = dg * (1.0 / GLA_GATE_NORM) * _sigmoid(-tm["pre"])

    fw = lambda c: (lambda i: (nc - 1 - i, c))
    rv = lambda c: (lambda i: (i, c))
    st_blk = (1, GLA_H, GLA_DV, GLA_DK)
    def side(ix, st_ix):
        return [pl.BlockSpec((CHUNK, QK), ix(0)), pl.BlockSpec((CHUNK, QK), ix(1)),
                pl.BlockSpec((CHUNK, GLA_W), ix(0)), pl.BlockSpec((CHUNK, LANES), ix(0)),
                pl.BlockSpec((CHUNK, GLA_W), ix(0)), pl.BlockSpec(st_blk, st_ix)]
    def oside(ix):
        return [pl.BlockSpec((CHUNK, QK), ix(0)), pl.BlockSpec((CHUNK, QK), ix(0)),
                pl.BlockSpec((CHUNK, GLA_W), ix(0)), pl.BlockSpec((CHUNK, QK), ix(0))]
    o_shapes = [jax.ShapeDtypeStruct((t_rows, QK), F32), jax.ShapeDtypeStruct((t_rows, QK), F32),
                jax.ShapeDtypeStruct((t_rows, GLA_W), F32), jax.ShapeDtypeStruct((t_rows, QK), F32)]
    return pl.pallas_call(
        kern, name=name, grid=(nc,),
        in_specs=(side(fw, lambda i: (nc - 1 - i, 0, 0, 0)) + side(rv, lambda i: (i, 0, 0, 0))
                  + [pl.BlockSpec(wg.shape, lambda i: (0, 0)), pl.BlockSpec(bg.shape, lambda i: (0, 0))]),
        out_specs=oside(fw) + oside(rv), out_shape=o_shapes * 2,
        scratch_shapes=[pltpu.VMEM((GLA_H, GLA_DV, GLA_DK), F32)] * 2,
        compiler_params=_cparams(("arbitrary",)),
    )(z_qk, z_qk, z_v, z_zg, do, s_f, z_qk, z_qk, z_v, z_zg, do, s_b, wg, bg)


FLAT_W = 1024


def _adam_math(wv, gv, mv, vv):
    bc1 = 1.0 - ADAM_B1 ** ADAM_STEP
    bc2 = 1.0 - ADAM_B2 ** ADAM_STEP
    m_new = ADAM_B1 * mv + (1.0 - ADAM_B1) * gv
    v_new = ADAM_B2 * vv + (1.0 - ADAM_B2) * (gv * gv)
    delta = -ADAM_LR * ((m_new / bc1) / (jnp.sqrt(v_new / bc2) + ADAM_EPS) + ADAM_WD * wv)
    return delta, m_new, v_new


def _row_tile(rows, cap=256):
    t = cap
    while rows % t:
        t //= 2
    assert t >= SUBLANES, rows
    return t


def _adamw_layers(w, m, v, g_layers, name):
    depth, rows, cols = w.shape
    tr = _row_tile(rows)

    def kern(w_ref, m_ref, v_ref, *rest):
        g_refs, (go_ref, d_ref, mo_ref, vo_ref) = rest[:depth], rest[depth:]
        l = pl.program_id(0)
        gv = g_refs[0][...]
        for k in range(1, depth):
            gv = jnp.where(l == k, g_refs[k][...], gv)
        delta, m_new, v_new = _adam_math(w_ref[...], gv, m_ref[...], v_ref[...])
        go_ref[...] = gv
        d_ref[...] = delta
        mo_ref[...] = m_new
        vo_ref[...] = v_new

    stacked = pl.BlockSpec((None, tr, cols), lambda l, i: (l, i, 0))
    return pl.pallas_call(
        kern, name=name, grid=(depth, rows // tr),
        in_specs=[stacked] * 3 + [pl.BlockSpec((tr, cols), lambda l, i: (i, 0))] * depth,
        out_specs=[stacked] * 4, out_shape=[jax.ShapeDtypeStruct(w.shape, F32)] * 4,
        compiler_params=_cparams(("arbitrary", "arbitrary")),
    )(w, m, v, *g_layers)


def _adamw_flat(w, g, m, v):
    rows = w.shape[0]

    def body(i, ins, consts, outs, accs):
        delta, m_new, v_new = _adam_math(*[r[...] for r in ins])
        outs[0][...] = delta
        outs[1][...] = m_new
        outs[2][...] = v_new

    w_, g_, m_, v_ = w, g, m, v
    return _rows(body, rows, _row_tile(rows), [(w_, FLAT_W, 0), (g_, FLAT_W, 0), (m_, FLAT_W, 0), (v_, FLAT_W, 0)],
                 [], [(FLAT_W, F32)] * 3, [], "adamw_small")


def _add_sibling(g, got, c_idx, name):
    _, _, rows, cols = g.shape
    tr = _row_tile(rows)

    def kern(c_ref, g_ref, got_ref, o_ref):
        o_ref[...] = (g_ref[...] + got_ref[...]).astype(BF16)

    grid_spec = pltpu.PrefetchScalarGridSpec(
        num_scalar_prefetch=1, grid=(N_CHIPS, rows // tr),
        in_specs=[pl.BlockSpec((None, None, tr, cols), lambda q, i, c_ref: (q, c_ref[0], i, 0)),
                  pl.BlockSpec((None, tr, cols), lambda q, i, c_ref: (q, i, 0))],
        out_specs=pl.BlockSpec((None, tr, cols), lambda q, i, c_ref: (q, i, 0)))
    return pl.pallas_call(
        kern, name=name, grid_spec=grid_spec, out_shape=jax.ShapeDtypeStruct((N_CHIPS, rows, cols), BF16),
        compiler_params=_cparams(("arbitrary", "arbitrary")),
    )(c_idx, g, got)


def _sum_chips(x, name):
    _, rows, cols = x.shape
    tr = _row_tile(rows)

    def kern(x_ref, o_ref):
        xv = x_ref[...].astype(F32)
        o_ref[...] = ((xv[0] + xv[1]) + xv[2]) + xv[3]

    return pl.pallas_call(
        kern, name=name, grid=(rows // tr,),
        in_specs=[pl.BlockSpec((N_CHIPS, tr, cols), lambda i: (0, i, 0))],
        out_specs=pl.BlockSpec((tr, cols), lambda i: (i, 0)),
        out_shape=jax.ShapeDtypeStruct((rows, cols), F32),
        compiler_params=_cparams(("arbitrary",)),
    )(x)


def _place():
    x, y, c = lax.axis_index("x"), lax.axis_index("y"), lax.axis_index("c")
    return x, y, c


def _other_chips(x, y):
    return [(1 - x, y), (x, 1 - y), (1 - x, 1 - y)]


def _hbm_call(kern, name, ins, out_shapes, n_sems):
    return pl.pallas_call(
        kern, name=name,
        in_specs=[pl.BlockSpec(memory_space=pl.ANY)] * len(ins),
        out_specs=[pl.BlockSpec(memory_space=pl.ANY)] * len(out_shapes),
        out_shape=out_shapes,
        scratch_shapes=[pltpu.SemaphoreType.DMA((n,)) for n in n_sems],
        compiler_params=pltpu.CompilerParams(has_side_effects=True),
    )(*ins)


DMA_CHUNK_BYTES = 1 << 20
DMA_MAX_CHUNKS = 4


def _row_chunks(rows, row_bytes, align=16):
    units = rows // align
    k = max(1, min(DMA_MAX_CHUNKS, -(-rows * row_bytes // DMA_CHUNK_BYTES), units))
    out, start = [], 0
    for i in range(k):
        size = (units // k + (1 if i < units % k else 0)) * align
        out.append((start, size))
        start += size
    if start < rows:
        out[-1] = (out[-1][0], out[-1][1] + rows - start)
    return out


def _row_bytes(shape, dtype):
    return math.prod(shape[1:]) * jnp.dtype(dtype).itemsize


def _remote(src, dst, send, recv, idx, dev):
    return pltpu.make_async_remote_copy(src, dst, send.at[idx], recv.at[idx], device_id=dev, device_id_type=MESH_ID)


def _allgather_chips(xs, name):
    na = len(xs)
    chunks = [_row_chunks(x.shape[1], _row_bytes(x.shape[1:], x.dtype)) for x in xs]
    n_cp = 3 * sum(len(ch) for ch in chunks)
    local = [(a, hh, s, n) for a in range(na) for hh in range(2) for (s, n) in chunks[a]]

    def kern(*refs):
        x_refs, o_refs = refs[:na], refs[na:2 * na]
        send1, recv1, send2, recv2, loc = refs[2 * na:]
        px, py, c = _place()
        me = 2 * px + py
        sib = (px, py, 1 - c)
        owns = [pltpu.make_async_copy(x_refs[a].at[hh, pl.ds(s, n)], o_refs[a].at[me, hh, pl.ds(s, n)], loc.at[i])
                for i, (a, hh, s, n) in enumerate(local)]
        for cp in owns:
            cp.start()
        plan = [(a, qx, qy, pl.ds(s, n)) for a in range(na) for (qx, qy) in _other_chips(px, py) for (s, n) in chunks[a]]
        first = [_remote(x_refs[a].at[c, rows], o_refs[a].at[me, c, rows], send1, recv1, i, (qx, qy, c))
                 for i, (a, qx, qy, rows) in enumerate(plan)]
        for cp in first:
            cp.start()
        passed = []
        for i, (a, qx, qy, rows) in enumerate(plan):
            slot = o_refs[a].at[2 * qx + qy, c, rows]
            _remote(slot, slot, send1, recv1, i, (qx, qy, c)).wait_recv()
            fwd = _remote(slot, slot, send2, recv2, i, sib)
            fwd.start()
            passed.append(fwd)
        for i, (a, qx, qy, rows) in enumerate(plan):
            slot = o_refs[a].at[2 * qx + qy, 1 - c, rows]
            _remote(slot, slot, send2, recv2, i, sib).wait_recv()
        for cp in first + passed:
            cp.wait_send()
        for cp in owns:
            cp.wait()

    outs = [jax.ShapeDtypeStruct((N_CHIPS,) + x.shape, x.dtype) for x in xs]
    return _hbm_call(kern, name, list(xs), outs, [n_cp, n_cp, n_cp, n_cp, len(local)])


def _sibling_halves(gs, name):
    na = len(gs)
    chunks = [_row_chunks(g.shape[2], _row_bytes(g.shape[2:], g.dtype)) for g in gs]
    n_cp = N_CHIPS * sum(len(ch) for ch in chunks)

    def kern(*refs):
        g_refs, o_refs = refs[:na], refs[na:2 * na]
        send, recv = refs[2 * na:]
        px, py, c = _place()
        plan = [(a, q, pl.ds(s, n)) for a in range(na) for q in range(N_CHIPS) for (s, n) in chunks[a]]
        cps = [_remote(g_refs[a].at[q, 1 - c, rows], o_refs[a].at[q, rows], send, recv, i, (px, py, 1 - c))
               for i, (a, q, rows) in enumerate(plan)]
        for cp in cps:
            cp.start()
        for cp in cps:
            cp.wait()

    outs = [jax.ShapeDtypeStruct((N_CHIPS,) + g.shape[2:], g.dtype) for g in gs]
    return _hbm_call(kern, name, list(gs), outs, [n_cp, n_cp])


def _chip_exchange(ps, name):
    na = len(ps)
    chunks = [_row_chunks(p.shape[1], _row_bytes(p.shape[1:], p.dtype)) for p in ps]
    n_cp = 3 * sum(len(ch) for ch in chunks)
    local = [(a, s, n) for a in range(na) for (s, n) in chunks[a]]

    def kern(*refs):
        p_refs, o_refs = refs[:na], refs[na:2 * na]
        send, recv, loc = refs[2 * na:]
        px, py, c = _place()
        me = 2 * px + py
        owns = [pltpu.make_async_copy(p_refs[a].at[me, pl.ds(s, n)], o_refs[a].at[me, pl.ds(s, n)], loc.at[i])
                for i, (a, s, n) in enumerate(local)]
        for cp in owns:
            cp.start()
        plan = [(a, qx, qy, pl.ds(s, n)) for a in range(na) for (qx, qy) in _other_chips(px, py) for (s, n) in chunks[a]]
        cps = [_remote(p_refs[a].at[2 * qx + qy, rows], o_refs[a].at[me, rows], send, recv, i, (qx, qy, c))
               for i, (a, qx, qy, rows) in enumerate(plan)]
        for cp in cps:
            cp.start()
        for cp in cps:
            cp.wait()
        for cp in owns:
            cp.wait()

    outs = [jax.ShapeDtypeStruct(p.shape, p.dtype) for p in ps]
    return _hbm_call(kern, name, list(ps), outs, [n_cp, n_cp, len(local)])


def _sibling_join(rs, name):
    na = len(rs)
    chunks = [_row_chunks(r.shape[0], _row_bytes(r.shape, r.dtype)) for r in rs]
    n_cp = sum(len(ch) for ch in chunks)

    def kern(*refs):
        r_refs, o_refs = refs[:na], refs[na:2 * na]
        send, recv, loc = refs[2 * na:]
        px, py, c = _place()
        plan = [(a, pl.ds(s, n)) for a in range(na) for (s, n) in chunks[a]]
        owns = [pltpu.make_async_copy(r_refs[a].at[rows], o_refs[a].at[c, rows], loc.at[i])
                for i, (a, rows) in enumerate(plan)]
        for cp in owns:
            cp.start()
        cps = [_remote(r_refs[a].at[rows], o_refs[a].at[c, rows], send, recv, i, (px, py, 1 - c))
               for i, (a, rows) in enumerate(plan)]
        for cp in cps:
            cp.start()
        for cp in cps:
            cp.wait()
        for cp in owns:
            cp.wait()

    outs = [jax.ShapeDtypeStruct((2,) + r.shape, r.dtype) for r in rs]
    return _hbm_call(kern, name, list(rs), outs, [n_cp, n_cp, n_cp])


def _reduce_pack(gs, c_idx, tag):
    got = _sibling_halves(gs, "reduce_sibling_halves" + tag)
    part = [_add_sibling(g, o, c_idx, f"reduce_add_sibling{tag}_{a}") for a, (g, o) in enumerate(zip(gs, got))]
    landed = _chip_exchange(part, "reduce_chip_exchange" + tag)
    mine = [_sum_chips(x, f"reduce_sum_chips{tag}_{a}") for a, x in enumerate(landed)]
    return _sibling_join(mine, "reduce_sibling_join" + tag)


WEIGHTS = ["meta_tokens", "norm_mix_pre", "norm_mix_post", "norm_mlp_pre", "norm_mlp_post", "w_in", "conv_w",
           "conv_b", "lru_wa_f", "lru_ba_f", "lru_wx_f", "lru_bx_f", "lru_lambda_f", "lru_wa_b", "lru_ba_b",
           "lru_wx_b", "lru_bx_b", "lru_lambda_b", "gla_wg_f", "gla_bg_f", "gla_wg_b", "gla_bg_b", "gla_head_norm",
           "w_out", "w_mlp_up", "w_mlp_down"]
BIG = ("w_in", "w_out", "w_mlp_up", "w_mlp_down")
SMALL = [n for n in WEIGHTS if n not in BIG]
SMALL_SHARDED = {"meta_tokens": 1, "conv_w": 2, "gla_wg_f": 2, "gla_wg_b": 2}
N_CHIPS = 4
SMALL_RAW = [("g1", (1, D_MODEL)), ("g2", (1, D_MODEL)), ("g3", (1, D_MODEL)), ("g4", (1, D_MODEL)),
             ("conv_w", (4, LRU_W)), ("conv_b", (1, LRU_W)), ("lru_wg", (4, LANES, 4 * LANES)),
             ("lru_pb", (SUBLANES, LRU_W)), ("gla_wg", (2 * GLA_RANK, 2 * QK)), ("gla_bg", (1, 2 * QK)),
             ("head_norm", (1, GLA_W))]
META_SHAPE = (N_META, D_MODEL)


def _pad_to(v, n):
    return jnp.pad(v, (0, n - v.shape[0]))


def _round_up(n, m):
    return -(-n // m) * m


def _flat_cat(arrs, total):
    return _pad_to(jnp.concatenate([a.reshape(-1) for a in arrs]), total)


def _split_flat(flat, shapes):
    out, off = [], 0
    for s in shapes:
        n = math.prod(s)
        out.append(flat[off:off + n].reshape(s))
        off += n
    return out


def _my_shard(full, axis, chip):
    n = full.shape[axis] // N_CHIPS
    return lax.dynamic_slice_in_dim(full, chip * n, n, axis=axis)


def _block_diag_gates(wa_f, wx_f, wa_b, wx_b):
    def bd(w):
        z = jnp.zeros((LRU_HD, LRU_HD), w.dtype)
        return jnp.stack([jnp.block([[w[2 * j], z], [z, w[2 * j + 1]]]) for j in range(4)])
    return jnp.concatenate([bd(wa_f), bd(wx_f), bd(wa_b), bd(wx_b)], axis=2).astype(BF16)


def _gate_diag_blocks(dwg, o):
    blk = dwg[:, :, o * LANES:(o + 1) * LANES]
    return jnp.stack([blk[j, hh * LRU_HD:(hh + 1) * LRU_HD, hh * LRU_HD:(hh + 1) * LRU_HD]
                      for j in range(4) for hh in range(2)])


def kernel(x, meta_tokens, norm_mix_pre, norm_mix_post, norm_mlp_pre, norm_mlp_post, w_in, conv_w, conv_b, lru_wa_f, lru_ba_f, lru_wx_f, lru_bx_f, lru_lambda_f, lru_wa_b, lru_ba_b, lru_wx_b, lru_bx_b, lru_lambda_b, gla_wg_f, gla_bg_f, gla_wg_b, gla_bg_b, gla_head_norm, w_out, w_mlp_up, w_mlp_down, loss_target, m_meta_tokens, m_norm_mix_pre, m_norm_mix_post, m_norm_mlp_pre, m_norm_mlp_post, m_w_in, m_conv_w, m_conv_b, m_lru_wa_f, m_lru_ba_f, m_lru_wx_f, m_lru_bx_f, m_lru_lambda_f, m_lru_wa_b, m_lru_ba_b, m_lru_wx_b, m_lru_bx_b, m_lru_lambda_b, m_gla_wg_f, m_gla_bg_f, m_gla_wg_b, m_gla_bg_b, m_gla_head_norm, m_w_out, m_w_mlp_up, m_w_mlp_down, v_meta_tokens, v_norm_mix_pre, v_norm_mix_post, v_norm_mlp_pre, v_norm_mlp_post, v_w_in, v_conv_w, v_conv_b, v_lru_wa_f, v_lru_ba_f, v_lru_wx_f, v_lru_bx_f, v_lru_lambda_f, v_lru_wa_b, v_lru_ba_b, v_lru_wx_b, v_lru_bx_b, v_lru_lambda_b, v_gla_wg_f, v_gla_bg_f, v_gla_wg_b, v_gla_bg_b, v_gla_head_norm, v_w_out, v_w_mlp_up, v_w_mlp_down):
    args = dict(locals())
    w_loc = {n: args[n] for n in WEIGHTS}
    m_loc = {n: args["m_" + n] for n in WEIGHTS}
    v_loc = {n: args["v_" + n] for n in WEIGHTS}
    seq = x.shape[1]
    t_rows = FRONT + seq + BACK
    assert t_rows % 768 == 0 and seq % FRONT == 0, seq
    chip = 2 * lax.axis_index("x") + lax.axis_index("y")
    c_idx = lax.axis_index("c").astype(jnp.int32).reshape(1)

    def halves(a):
        return a.reshape((2, a.shape[0] // 2) + a.shape[1:])

    big16 = {n: w_loc[n].astype(BF16) for n in BIG}
    sm_names = list(SMALL_SHARDED)
    sm_len = _round_up(sum(w_loc[n].size for n in sm_names), 2 * SUBLANES * FLAT_W)
    sm_pack = _flat_cat([w_loc[n] for n in sm_names], sm_len).reshape(2, -1, FLAT_W)
    gathered = _allgather_chips([halves(big16[n][l]) for l in range(DEPTH) for n in BIG] + [sm_pack], "gather_weights")
    sm_parts = [_split_flat(gathered[-1][q].reshape(-1), [w_loc[n].shape for n in sm_names]) for q in range(N_CHIPS)]
    full = {n: jnp.concatenate([sm_parts[q][i] for q in range(N_CHIPS)], axis=SMALL_SHARDED[n])
            for i, n in enumerate(sm_names)}
    for n in SMALL:
        if n not in SMALL_SHARDED:
            full[n] = w_loc[n]
    big_full = []
    for l in range(DEPTH):
        g_in, g_out, g_up, g_down = gathered[l * len(BIG):(l + 1) * len(BIG)]
        w_in_l = jnp.transpose(g_in.reshape(N_CHIPS, D_MODEL, -1), (1, 0, 2)).reshape(D_MODEL, D_IN)
        big_full.append(dict(
            w_in=jnp.pad(w_in_l, ((0, 0), (0, D_IN_PAD - D_IN))),
            w_out=g_out.reshape(-1, D_MODEL),
            w_up=jnp.transpose(g_up.reshape(N_CHIPS, D_MODEL, -1), (1, 0, 2)).reshape(D_MODEL, D_FF),
            w_down=g_down.reshape(D_FF, D_MODEL)))

    meta_blk = jnp.concatenate([jnp.zeros((FRONT - N_META, D_MODEL), F32), full["meta_tokens"]], axis=0)
    h = jnp.concatenate([meta_blk, x[0], jnp.zeros((BACK, D_MODEL), F32)], axis=0)
    target = loss_target[0]

    layer_w = []
    for l in range(DEPTH):
        wg = jnp.zeros((LANES, 2 * QK), F32)
        wg = wg.at[0:GLA_RANK, 0:QK].set(full["gla_wg_f"][l]).at[GLA_RANK:2 * GLA_RANK, QK:].set(full["gla_wg_b"][l])
        pb = jnp.stack([full["lru_ba_f"][l], full["lru_bx_f"][l], full["lru_ba_b"][l], full["lru_bx_b"][l],
                        full["lru_lambda_f"][l], full["lru_lambda_b"][l], jnp.zeros((LRU_W,), F32),
                        jnp.zeros((LRU_W,), F32)])
        layer_w.append(dict(
            big_full[l],
            g1=full["norm_mix_pre"][l][None], g2=full["norm_mix_post"][l][None],
            g3=full["norm_mlp_pre"][l][None], g4=full["norm_mlp_post"][l][None],
            conv_w=full["conv_w"][l], conv_b=full["conv_b"][l][None],
            lru_wg=_block_diag_gates(full["lru_wa_f"][l], full["lru_wx_f"][l], full["lru_wa_b"][l], full["lru_wx_b"][l]),
            lru_pb=pb, gla_wg=wg.astype(BF16),
            gla_bg=jnp.concatenate([full["gla_bg_f"][l], full["gla_bg_b"][l]])[None],
            head_norm=full["gla_head_norm"][l][None]))

    def split_z(acc):
        return (acc[:, 0:1024], acc[:, 1024:1536], acc[:, 1536:2048], acc[:, 2048:2560], acc[:, ZG_OFF:ZG_OFF + LANES])

    def relu2(acc):
        r = jnp.maximum(acc, 0.0)
        return acc, r * r

    saved = []
    for l in range(DEPTH):
        lw = layer_w[l]
        tag = f"_l{l}"
        n1 = _norm_cast(h, lw["g1"], t_rows, "norm_mix_pre" + tag)
        z_lru, z_qk, z_v, z_go, z_zg = _mm(
            n1, lw["w_in"], "nn", 384, D_IN_PAD, D_MODEL,
            [(1024, F32), (512, F32), (512, F32), (512, F32), (LANES, F32)], "in_proj" + tag, epilogue=split_z)
        h_f, h_b = _lru_fwd(z_lru, lw["conv_w"], lw["conv_b"], lw["lru_wg"], lw["lru_pb"], t_rows, seq, "lru_fwd" + tag)
        o_f, o_b, s_f, s_b = _gla_fwd(z_qk, z_v, z_zg, lw["gla_wg"], lw["gla_bg"], t_rows, "gla_fwd" + tag)
        ymix = _mix_fwd(h_f, h_b, z_lru, o_f, o_b, z_go, lw["head_norm"], t_rows, "mix_fwd" + tag)
        mix = _mm(ymix, lw["w_out"], "nn", 768, D_MODEL, D_MODEL, [(D_MODEL, F32)], "out_proj" + tag)[0]
        h1 = _norm_residual(mix, lw["g2"], h, t_rows, "norm_mix_post" + tag)
        n3 = _norm_cast(h1, lw["g3"], t_rows, "norm_mlp_pre" + tag)
        u, act = _mm(n3, lw["w_up"], "nn", 768, 1024, D_MODEL, [(D_FF, F32), (D_FF, BF16)], "mlp_up" + tag,
                     epilogue=relu2)
        ff = _mm(act, lw["w_down"], "nn", 768, D_MODEL, 1024, [(D_MODEL, F32)], "mlp_down" + tag)[0]
        h2 = _norm_residual(ff, lw["g4"], h1, t_rows, "norm_mlp_post" + tag)
        saved.append(dict(h=h, n1=n1, z_lru=z_lru, z_qk=z_qk, z_v=z_v, z_go=z_go, z_zg=z_zg, h_f=h_f, h_b=h_b,
                          o_f=o_f, o_b=o_b, s_f=s_f, s_b=s_b, ymix=ymix, mix=mix, h1=h1, n3=n3, u=u, act=act, ff=ff))
        h = h2

    dh, loss_part = _loss_and_grad(h, target, t_rows, seq)
    loss = lax.psum(loss_part[0, 0], ("x", "y", "c"))

    small_len = _round_up(sum(math.prod(s) for _, s in SMALL_RAW) + math.prod(META_SHAPE),
                          N_CHIPS * 2 * 2 * SUBLANES * FLAT_W)
    totals = [None] * DEPTH
    for l in reversed(range(DEPTH)):
        lw, sv = layer_w[l], saved[l]
        tag = f"_l{l}"
        raw = {}
        dff, raw["g4"] = _norm_residual_bwd(dh, sv["ff"], lw["g4"], t_rows, "norm_mlp_post_bwd" + tag)
        gw_down = _mm(sv["act"], dff, "tn", 1024, D_MODEL, 768, [(D_MODEL, F32)], "dw_down" + tag)[0]
        du = _mm(dff, lw["w_down"], "nt", 768, 1024, D_MODEL, [(D_FF, BF16)], "d_act" + tag,
                 epilogue=lambda acc, uu: (acc * 2.0 * jnp.maximum(uu, 0.0),), extras=(sv["u"],))[0]
        gw_up = _mm(sv["n3"], du, "tn", D_MODEL, D_FF // N_CHIPS, 768, [(D_FF, F32)], "dw_up" + tag,
                    col_blocks_first=True)[0]
        dn3 = _mm(du, lw["w_up"], "nt", 768, D_MODEL, 1024, [(D_MODEL, F32)], "d_n3" + tag)[0]
        dh1, raw["g3"] = _norm_bwd_add(dn3, sv["h1"], lw["g3"], dh, t_rows, "norm_mlp_pre_bwd" + tag)
        dmix, raw["g2"] = _norm_residual_bwd(dh1, sv["mix"], lw["g2"], t_rows, "norm_mix_post_bwd" + tag)
        gw_out = _mm(sv["ymix"], dmix, "tn", D_MODEL, D_MODEL, 768, [(D_MODEL, F32)], "dw_out" + tag)[0]
        dymix = _mm(dmix, lw["w_out"], "nt", 768, D_MODEL, D_MODEL, [(D_MODEL, F32)], "d_ymix" + tag)[0]
        dh_lru, dgate, do, dgo, raw["head_norm"] = _mix_bwd(
            dymix, sv["h_f"], sv["h_b"], sv["z_lru"], sv["o_f"], sv["o_b"], sv["z_go"], lw["head_norm"], t_rows,
            "mix_bwd" + tag)
        gla_grads = _gla_bwd(sv["z_qk"], sv["z_v"], sv["z_zg"], lw["gla_wg"], lw["gla_bg"], do, sv["s_f"], sv["s_b"],
                             t_rows, "gla_bwd" + tag)
        dx_br, raw["conv_w"], raw["conv_b"], raw["lru_wg"], raw["lru_pb"] = _lru_bwd(
            sv["z_lru"], lw["conv_w"], lw["conv_b"], lw["lru_wg"], lw["lru_pb"], sv["h_f"], sv["h_b"], dh_lru,
            t_rows, seq, "lru_bwd" + tag)
        dz, dwg_gla, raw["gla_bg"] = _dz_assemble(dx_br, dgate, gla_grads, dgo, sv["z_zg"], lw["gla_wg"], t_rows,
                                                  "dz_assemble" + tag)
        raw["gla_wg"] = dwg_gla[0:2 * GLA_RANK]
        gw_in = _mm(sv["n1"], dz, "tn", D_MODEL, 896, 768, [(D_IN_PAD, F32)], "dw_in" + tag)[0]
        dn1 = _mm(dz, lw["w_in"], "nt", 768, D_MODEL, 896, [(D_MODEL, F32)], "d_n1" + tag)[0]
        dh, raw["g1"] = _norm_bwd_add(dn1, sv["h"], lw["g1"], dh1, t_rows, "norm_mix_pre_bwd" + tag)

        cols = D_IN // N_CHIPS
        in_sh = jnp.stack([gw_in[:, q * cols:(q + 1) * cols] for q in range(N_CHIPS)])
        meta_g = dh[FRONT - N_META:FRONT] if l == 0 else jnp.zeros(META_SHAPE, F32)
        small = _flat_cat([raw[n] for n, _ in SMALL_RAW] + [meta_g], small_len)
        pack = [in_sh.reshape(N_CHIPS, 2, D_MODEL // 2, cols),
                gw_out.reshape(N_CHIPS, 2, -1, D_MODEL),
                gw_up.reshape(N_CHIPS, 2, D_MODEL // 2, D_FF // N_CHIPS),
                gw_down.reshape(N_CHIPS, 2, -1, D_MODEL),
                small.reshape(N_CHIPS, 2, -1, FLAT_W)]
        totals[l] = _reduce_pack(pack, c_idx, tag)

    grad_x = dh[FRONT:FRONT + seq][None]

    sm_all = _allgather_chips([totals[l][4] for l in range(DEPTH)], "gather_small_grads")
    g_tot = {}
    per_layer = []
    for l in range(DEPTH):
        pieces = _split_flat(sm_all[l].reshape(-1), [s for _, s in SMALL_RAW] + [META_SHAPE])
        per_layer.append(dict(zip([n for n, _ in SMALL_RAW] + ["meta"], pieces)))
    stack = lambda f: jnp.stack([f(per_layer[l]) for l in range(DEPTH)])
    g_tot["meta_tokens"] = _my_shard(per_layer[0]["meta"], 1, chip)
    for n, key in (("norm_mix_pre", "g1"), ("norm_mix_post", "g2"), ("norm_mlp_pre", "g3"), ("norm_mlp_post", "g4"),
                   ("conv_b", "conv_b"), ("gla_head_norm", "head_norm")):
        g_tot[n] = stack(lambda d, key=key: d[key][0])
    g_tot["conv_w"] = _my_shard(stack(lambda d: d["conv_w"]), 2, chip)
    for o, n in enumerate(("lru_wa_f", "lru_wx_f", "lru_wa_b", "lru_wx_b")):
        g_tot[n] = stack(lambda d, o=o: _gate_diag_blocks(d["lru_wg"], o))
    for row, n in enumerate(("lru_ba_f", "lru_bx_f", "lru_ba_b", "lru_bx_b", "lru_lambda_f", "lru_lambda_b")):
        g_tot[n] = stack(lambda d, row=row: d["lru_pb"][row])
    g_tot["gla_wg_f"] = _my_shard(stack(lambda d: d["gla_wg"][0:GLA_RANK, 0:QK]), 2, chip)
    g_tot["gla_wg_b"] = _my_shard(stack(lambda d: d["gla_wg"][GLA_RANK:, QK:]), 2, chip)
    g_tot["gla_bg_f"] = stack(lambda d: d["gla_bg"][0, 0:QK])
    g_tot["gla_bg_b"] = stack(lambda d: d["gla_bg"][0, QK:])

    delta, new_m, new_v = {}, {}, {}
    for a, n in enumerate(BIG):
        shp = w_loc[n].shape
        flat3 = lambda t, shp=shp: t.reshape(DEPTH, -1, shp[-1])
        g_l = [totals[l][a].reshape(-1, shp[-1]) for l in range(DEPTH)]
        outs = _adamw_layers(flat3(w_loc[n]), flat3(m_loc[n]), flat3(v_loc[n]), g_l, "adamw_" + n)
        g_tot[n], delta[n], new_m[n], new_v[n] = [o.reshape(shp) for o in outs]
    a_rows = _round_up(sum(w_loc[n].size for n in SMALL), SUBLANES * FLAT_W) // FLAT_W
    pack_small = lambda d: _flat_cat([d[n] for n in SMALL], a_rows * FLAT_W).reshape(a_rows, FLAT_W)
    outs = _adamw_flat(pack_small(w_loc), pack_small(g_tot), pack_small(m_loc), pack_small(v_loc))
    shapes = [w_loc[n].shape for n in SMALL]
    for d, o in zip((delta, new_m, new_v), outs):
        d.update(zip(SMALL, _split_flat(o.reshape(-1), shapes)))
    return (loss, grad_x, *[g_tot[n] for n in WEIGHTS], *[delta[n] for n in WEIGHTS],
            *[new_m[n] for n in WEIGHTS], *[new_v[n] for n in WEIGHTS])
```

```python
import math

import jax
import jax.numpy as jnp
from jax import lax
from jax.experimental import pallas as pl
from jax.experimental.pallas import tpu as pltpu

F32 = jnp.float32
BF16 = jnp.bfloat16

D_MODEL = 1024
DEPTH = 2
N_META = 16
LRU_W = 512
LRU_HEADS = 8
LRU_HD = 64
LRU_C = 8.0
GLA_W = 512
GLA_H = 4
GLA_DK = 64
GLA_DV = 128
GLA_RANK = 16
GLA_GATE_NORM = 16.0
D_FF = 4096
D_IN = 2592
EPS = 1e-6
ADAM_LR, ADAM_B1, ADAM_B2, ADAM_EPS, ADAM_WD, ADAM_STEP = 0.001, 0.9, 0.999, 1e-08, 0.01, 10

LANES = 128
SUBLANES = 8
FRONT = 128
BACK = 128
D_IN_PAD = 2688
ZG_OFF = 2560
CHUNK = 64
EXP_CLAMP = 80.0
VMEM_LIMIT = 56 * 1024 * 1024
MESH_ID = pl.DeviceIdType.MESH

NN = (((1,), (0,)), ((), ()))
NT = (((1,), (1,)), ((), ()))
TN = (((0,), (0,)), ((), ()))


def _dot(a, b, dims=NN):
    return lax.dot_general(a, b, dims, preferred_element_type=F32)


def _cparams(sem):
    return pltpu.CompilerParams(dimension_semantics=sem, vmem_limit_bytes=VMEM_LIMIT)


def _sigmoid(x):
    return 1.0 / (1.0 + jnp.exp(-x))


def _gelu(x):
    c = math.sqrt(2.0 / math.pi)
    return 0.5 * x * (1.0 + jnp.tanh(c * (x + 0.044715 * x * x * x)))


def _gelu_grad(x):
    c = math.sqrt(2.0 / math.pi)
    t = jnp.tanh(c * (x + 0.044715 * x * x * x))
    return 0.5 * (1.0 + t) + 0.5 * x * (1.0 - t * t) * c * (1.0 + 3.0 * 0.044715 * x * x)


def _rms_fwd(x, g):
    rstd = lax.rsqrt(jnp.mean(x * x, axis=1, keepdims=True) + EPS)
    return (x * rstd) * g


def _rms_bwd(x, g, dy):
    rstd = lax.rsqrt(jnp.mean(x * x, axis=1, keepdims=True) + EPS)
    xh = x * rstd
    dxh = dy * g
    dx = rstd * (dxh - xh * jnp.mean(dxh * xh, axis=1, keepdims=True))
    dg = jnp.sum(dy * xh, axis=0, keepdims=True)
    return dx, dg


def _mm(a, b, mode, tm, tn, tk, outs, name, epilogue=None, extras=(), col_blocks_first=False,
        rows_in=(), consts=(), accs=()):
    if mode == "nn":
        (m, k), n = a.shape, b.shape[1]
        a_spec = pl.BlockSpec((tm, tk), lambda i, j, kk: (i, kk))
        b_spec = pl.BlockSpec((tk, tn), lambda i, j, kk: (kk, j))
        dims = NN
    elif mode == "nt":
        (m, k), n = a.shape, b.shape[0]
        a_spec = pl.BlockSpec((tm, tk), lambda i, j, kk: (i, kk))
        b_spec = pl.BlockSpec((tn, tk), lambda i, j, kk: (j, kk))
        dims = NT
    else:
        (k, m), n = a.shape, b.shape[1]
        a_spec = pl.BlockSpec((tk, tm), lambda i, j, kk: (kk, i))
        b_spec = pl.BlockSpec((tk, tn), lambda i, j, kk: (kk, j))
        dims = TN
    assert m % tm == 0 and n % tn == 0 and k % tk == 0, (name, m, n, k)
    nk = k // tk
    ne = len(extras) + len(rows_in) + len(consts)
    e_specs = [pl.BlockSpec((tm, tn), lambda i, j, kk: (i, j)) for _ in extras]
    e_specs += [pl.BlockSpec((tm, r.shape[1]), lambda i, j, kk: (i, 0)) for r in rows_in]
    e_specs += [pl.BlockSpec(c.shape, lambda i, j, kk: (0, 0)) for c in consts]
    n_out = len(outs)
    o_specs, o_shapes = [], []
    for width, dtype in outs:
        if col_blocks_first:
            assert width == n, name
            o_specs.append(pl.BlockSpec((None, tm, tn), lambda i, j, kk: (j, i, 0)))
            o_shapes.append(jax.ShapeDtypeStruct((n // tn, m, tn), dtype))
            continue
        if width == n:
            o_specs.append(pl.BlockSpec((tm, tn), lambda i, j, kk: (i, j)))
        else:
            assert n == tn, name
            o_specs.append(pl.BlockSpec((tm, width), lambda i, j, kk: (i, 0)))
        o_shapes.append(jax.ShapeDtypeStruct((m, width), dtype))
    o_specs += [pl.BlockSpec(s, lambda i, j, kk: (0, 0)) for s, _ in accs]
    o_shapes += [jax.ShapeDtypeStruct(s, d) for s, d in accs]

    def finish(acc, extra_refs, out_refs):
        res = epilogue(acc, *[e[...] for e in extra_refs]) if epilogue else (acc,)
        for o, r in zip(out_refs[:n_out], res[:n_out]):
            o[...] = r.astype(o.dtype)
        first = jnp.logical_and(pl.program_id(0) == 0, pl.program_id(1) == 0)
        for o, r in zip(out_refs[n_out:], res[n_out:]):
            @pl.when(first)
            def _(o=o):
                o[...] = jnp.zeros_like(o)

            o[...] += r

    if nk == 1:
        def body(a_ref, b_ref, *rest):
            finish(_dot(a_ref[...], b_ref[...], dims), rest[:ne], rest[ne:])
        scratch = []
    else:
        def body(a_ref, b_ref, *rest):
            acc_ref = rest[-1]
            kk = pl.program_id(2)

            @pl.when(kk == 0)
            def _():
                acc_ref[...] = jnp.zeros_like(acc_ref)

            acc_ref[...] += _dot(a_ref[...], b_ref[...], dims)

            @pl.when(kk == nk - 1)
            def _():
                finish(acc_ref[...], rest[:ne], rest[ne:-1])
        scratch = [pltpu.VMEM((tm, tn), F32)]

    return pl.pallas_call(
        body, name=name, grid=(m // tm, n // tn, nk),
        in_specs=[a_spec, b_spec] + e_specs, out_specs=o_specs, out_shape=o_shapes,
        scratch_shapes=scratch,
        compiler_params=_cparams(("arbitrary", "arbitrary", "arbitrary") if accs else ("parallel", "parallel", "arbitrary")),
    )(a, b, *extras, *rows_in, *consts)


def _rows(body, n_rows, tm, ins, consts, outs, accs, name):
    assert n_rows % tm == 0, (name, n_rows, tm)
    in_specs = [pl.BlockSpec((tm, w), (lambda i, cb=cb: (i, cb))) for _, w, cb in ins]
    in_specs += [pl.BlockSpec(c.shape, lambda i: (0, 0)) for c in consts]
    out_specs = [pl.BlockSpec((tm, w), lambda i: (i, 0)) for w, _ in outs]
    out_specs += [pl.BlockSpec(s, lambda i: (0, 0)) for s, _ in accs]
    out_shape = [jax.ShapeDtypeStruct((n_rows, w), d) for w, d in outs]
    out_shape += [jax.ShapeDtypeStruct(s, d) for s, d in accs]
    ni, nc, no = len(ins), len(consts), len(outs)

    def kern(*refs):
        body(pl.program_id(0), refs[:ni], refs[ni:ni + nc], refs[ni + nc:ni + nc + no], refs[ni + nc + no:])

    res = pl.pallas_call(
        kern, name=name, grid=(n_rows // tm,), in_specs=in_specs, out_specs=out_specs, out_shape=out_shape,
        compiler_params=_cparams(("arbitrary",)),
    )(*[a for a, _, _ in ins], *consts)
    return res


def _acc_add(i, ref, val):
    @pl.when(i == 0)
    def _():
        ref[...] = jnp.zeros_like(ref)

    ref[...] += val


def _norm_cast(h, g, t_rows, name):
    def body(i, ins, consts, outs, accs):
        outs[0][...] = _rms_fwd(ins[0][...], consts[0][...]).astype(BF16)
    return _rows(body, t_rows, 384, [(h, D_MODEL, 0)], [g], [(D_MODEL, BF16)], [], name)[0]


def _head_norm_parts(o):
    ns, rs = [], []
    for hd in range(GLA_H):
        oh = o[:, hd * GLA_DV:(hd + 1) * GLA_DV]
        r = lax.rsqrt(jnp.mean(oh * oh, axis=1, keepdims=True) + EPS)
        ns.append(oh * r)
        rs.append(r)
    return ns, rs


def _mix_fwd(h_f, h_b, z_lru, o_f, o_b, z_go, head_norm, t_rows, name):
    def body(i, ins, consts, outs, accs):
        hf, hb, gate, of, ob, go = [r[...] for r in ins]
        hn = consts[0][...]
        outs[0][:, 0:LRU_W] = ((hf + hb) * _gelu(gate)).astype(BF16)
        ns, _ = _head_norm_parts(of + ob)
        silu = go * _sigmoid(go)
        for hd in range(GLA_H):
            sl = slice(hd * GLA_DV, (hd + 1) * GLA_DV)
            outs[0][:, LRU_W + hd * GLA_DV:LRU_W + (hd + 1) * GLA_DV] = (ns[hd] * hn[:, sl] * silu[:, sl]).astype(BF16)
    ins = [(h_f, LRU_W, 0), (h_b, LRU_W, 0), (z_lru, LRU_W, 1), (o_f, GLA_W, 0), (o_b, GLA_W, 0), (z_go, GLA_W, 0)]
    return _rows(body, t_rows, 384, ins, [head_norm], [(D_MODEL, BF16)], [], name)[0]


def _mix_bwd(dymix, h_f, h_b, z_lru, o_f, o_b, z_go, head_norm, t_rows, name):
    def body(i, ins, consts, outs, accs):
        dyl, dyg, hf, hb, gate, of, ob, go = [r[...] for r in ins]
        hn = consts[0][...]
        outs[0][...] = dyl * _gelu(gate)
        outs[1][...] = dyl * (hf + hb) * _gelu_grad(gate)
        ns, rs = _head_norm_parts(of + ob)
        sg = _sigmoid(go)
        silu = go * sg
        dsilu = sg * (1.0 + go * (1.0 - sg))
        dhn = []
        for hd in range(GLA_H):
            sl = slice(hd * GLA_DV, (hd + 1) * GLA_DV)
            dy = dyg[:, sl]
            e = dy * hn[:, sl] * silu[:, sl]
            outs[2][:, sl] = rs[hd] * (e - ns[hd] * jnp.mean(e * ns[hd], axis=1, keepdims=True))
            outs[3][:, sl] = dy * ns[hd] * hn[:, sl] * dsilu[:, sl]
            dhn.append(jnp.sum(dy * ns[hd] * silu[:, sl], axis=0, keepdims=True))
        _acc_add(i, accs[0], jnp.concatenate(dhn, axis=1))
    ins = [(dymix, LRU_W, 0), (dymix, GLA_W, 1), (h_f, LRU_W, 0), (h_b, LRU_W, 0), (z_lru, LRU_W, 1),
           (o_f, GLA_W, 0), (o_b, GLA_W, 0), (z_go, GLA_W, 0)]
    return _rows(body, t_rows, 384, ins, [head_norm],
                 [(LRU_W, F32), (LRU_W, F32), (GLA_W, F32), (GLA_W, F32)], [((1, GLA_W), F32)], name)


def _dz_assemble(dx_br, dgate, gla_grads, dgo, z_zg, wg, t_rows, name):
    dq_f, dk_f, dv_f, dpre_f, dq_b, dk_b, dv_b, dpre_b = gla_grads
    qk = GLA_H * GLA_DK

    def body(i, ins, consts, outs, accs):
        (dxb, dgt, dqf, dqb, dkf, dkb, dvf, dvb, dg_o, dpf, dpb, zg) = [r[...] for r in ins]
        w = consts[0][...]
        o = outs[0]
        o[:, 0:LRU_W] = dxb.astype(BF16)
        o[:, LRU_W:2 * LRU_W] = dgt.astype(BF16)
        o[:, 1024:1024 + qk] = ((dqf + dqb) * (GLA_DK ** -0.5)).astype(BF16)
        o[:, 1280:1280 + qk] = (dkf + dkb).astype(BF16)
        o[:, 1536:1536 + GLA_W] = (dvf + dvb).astype(BF16)
        o[:, 2048:2048 + GLA_W] = dg_o.astype(BF16)
        dpre = jnp.concatenate([dpf, dpb], axis=1)
        dpre16 = dpre.astype(BF16)
        o[:, ZG_OFF:ZG_OFF + LANES] = _dot(dpre16, w, NT).astype(BF16)
        _acc_add(i, accs[0], _dot(zg.astype(BF16), dpre16, TN))
        _acc_add(i, accs[1], jnp.sum(dpre, axis=0, keepdims=True))

    ins = [(dx_br, LRU_W, 0), (dgate, LRU_W, 0), (dq_f, qk, 0), (dq_b, qk, 0), (dk_f, qk, 0), (dk_b, qk, 0),
           (dv_f, GLA_W, 0), (dv_b, GLA_W, 0), (dgo, GLA_W, 0), (dpre_f, qk, 0), (dpre_b, qk, 0), (z_zg, LANES, 0)]
    return _rows(body, t_rows, 384, ins, [wg], [(D_IN_PAD, BF16)],
                 [((LANES, 2 * qk), F32), ((1, 2 * qk), F32)], name)


LRU_RB = 64
HALO = SUBLANES


def _scan8(a, u, rev):
    rows = lax.broadcasted_iota(jnp.int32, a.shape, 0)
    for d in (1, 2, 4):
        if rev:
            a_s, u_s, ok = pltpu.roll(a, SUBLANES - d, 0), pltpu.roll(u, SUBLANES - d, 0), rows < SUBLANES - d
        else:
            a_s, u_s, ok = pltpu.roll(a, d, 0), pltpu.roll(u, d, 0), rows >= d
        u = jnp.where(ok, a * u_s + u, u)
        a = jnp.where(ok, a * a_s, a)
    return a, u


def _edge_row(x, rev):
    r = x[0:1, :] if rev else x[SUBLANES - 1:SUBLANES, :]
    return jnp.broadcast_to(r, x.shape)


def _scan_block(a, u, carry, rev):
    nsub = a.shape[0] // SUBLANES
    loc = [_scan8(a[sb * SUBLANES:(sb + 1) * SUBLANES], u[sb * SUBLANES:(sb + 1) * SUBLANES], rev) for sb in range(nsub)]
    edges = [(_edge_row(a_c, rev), _edge_row(h_l, rev)) for a_c, h_l in loc]
    carries = [None] * nsub
    for sb in (range(nsub - 1, -1, -1) if rev else range(nsub)):
        carries[sb] = carry
        carry = edges[sb][0] * carry + edges[sb][1]
    h = jnp.concatenate([h_l + a_c * cin for (a_c, h_l), cin in zip(loc, carries)], axis=0)
    return h, carry


def _lru_gates(xc, pre_a, pre_x, ba, bx, sp8):
    r = _sigmoid(pre_a + ba)
    i = _sigmoid(pre_x + bx)
    log_a = -(r * sp8)
    a = jnp.exp(log_a)
    y = 2.0 * log_a
    series = -y * (1.0 + y * (0.5 + y * (1.0 / 6.0 + y * (1.0 / 24.0 + y * (1.0 / 120.0)))))
    om = jnp.where(y > -0.25, series, 1.0 - a * a)
    s = jnp.sqrt(om)
    return r, i, a, s


def _softplus(x):
    return jnp.maximum(x, 0.0) + jnp.log(1.0 + jnp.exp(-jnp.abs(x)))


def _conv_taps(xpad_ref, r0, nrows, shifts=(-2, -1, 0, 1)):
    ext = xpad_ref[pl.ds(r0, nrows + 2 * HALO), :]
    taps = []
    for s in shifts:
        sh = ext if s == 0 else pltpu.roll(ext, (-s) % (nrows + 2 * HALO), 0)
        taps.append(sh[HALO:HALO + nrows])
    return taps


def _row_mask(r0, nrows, seq):
    rows = r0 + lax.broadcasted_iota(jnp.int32, (nrows, LANES), 0)
    return jnp.logical_and(rows >= FRONT - N_META, rows < FRONT + seq).astype(F32)


def _lru_load_conv(z_hbm, xpad, xc, sem, cw_ref, cb_ref, t_rows, seq):
    j = pl.program_id(0)
    zeros = jnp.zeros((HALO, LANES), F32)
    xpad[0:HALO, :] = zeros
    xpad[t_rows + HALO:t_rows + 2 * HALO, :] = zeros
    cp = pltpu.make_async_copy(z_hbm.at[:, pl.ds(pl.multiple_of(j * LANES, LANES), LANES)],
                               xpad.at[pl.ds(HALO, t_rows)], sem)
    cp.start()
    cp.wait()
    cw = cw_ref[...]
    cb = cb_ref[...]

    def conv_blk(r, carry):
        r0 = pl.multiple_of(r * LRU_RB, LRU_RB)
        taps = _conv_taps(xpad, r0, LRU_RB)
        acc = cb + cw[0:1, :] * taps[0]
        for k in range(1, 4):
            acc = acc + cw[k:k + 1, :] * taps[k]
        xc[pl.ds(r0, LRU_RB), :] = acc * _row_mask(r0, LRU_RB, seq)
        return carry

    lax.fori_loop(0, t_rows // LRU_RB, conv_blk, 0)


def _lru_specs(t_rows):
    return [pl.BlockSpec(memory_space=pl.ANY),
            pl.BlockSpec((4, LANES), lambda j: (0, j)),
            pl.BlockSpec((1, LANES), lambda j: (0, j)),
            pl.BlockSpec((1, LANES, 4 * LANES), lambda j: (j, 0, 0)),
            pl.BlockSpec((SUBLANES, LANES), lambda j: (0, j))]


def _lru_fwd(z_lru, conv_w, conv_b, wg, pb, t_rows, seq, name):
    nblk = t_rows // LRU_RB
    nsub = LRU_RB // SUBLANES

    def kern(z_hbm, cw_ref, cb_ref, wg_ref, pb_ref, hf_ref, hb_ref, xpad, xc, sem):
        _lru_load_conv(z_hbm, xpad, xc, sem, cw_ref, cb_ref, t_rows, seq)
        w = wg_ref[0]
        pb_v = pb_ref[...]
        dirs = []
        for rev in (False, True):
            o = 2 if rev else 0
            dirs.append(dict(rev=rev, ba=pb_v[o:o + 1, :], bx=pb_v[o + 1:o + 2, :],
                             sp8=LRU_C * _softplus(-pb_v[5:6, :] if rev else -pb_v[4:5, :]),
                             w=w[:, o * LANES:(o + 2) * LANES], out=hb_ref if rev else hf_ref))

        def blk(it, carries):
            new = []
            for d, carry in zip(dirs, carries):
                r = (nblk - 1 - it) if d["rev"] else it
                r0 = pl.multiple_of(r * LRU_RB, LRU_RB)
                xcb = xc[pl.ds(r0, LRU_RB), :]
                pre = _dot(xcb.astype(BF16), d["w"])
                _, gi, a, s = _lru_gates(xcb, pre[:, 0:LANES], pre[:, LANES:2 * LANES], d["ba"], d["bx"], d["sp8"])
                h, carry = _scan_block(a, s * (gi * xcb), carry, d["rev"])
                d["out"][pl.ds(r0, LRU_RB), :] = h
                new.append(carry)
            return tuple(new)

        z8 = jnp.zeros((SUBLANES, LANES), F32)
        lax.fori_loop(0, nblk, blk, (z8, z8), unroll=2)

    return pl.pallas_call(
        kern, name=name, grid=(LRU_W // LANES,), in_specs=_lru_specs(t_rows),
        out_specs=[pl.BlockSpec((t_rows, LANES), lambda j: (0, j))] * 2,
        out_shape=[jax.ShapeDtypeStruct((t_rows, LRU_W), F32)] * 2,
        scratch_shapes=[pltpu.VMEM((t_rows + 2 * HALO, LANES), F32), pltpu.VMEM((t_rows, LANES), F32),
                        pltpu.SemaphoreType.DMA],
        compiler_params=_cparams(("arbitrary",)),
    )(z_lru, conv_w, conv_b, wg, pb)


def _lru_bwd(z_lru, conv_w, conv_b, wg, pb, h_f, h_b, dh, t_rows, seq, name):
    nblk = t_rows // LRU_RB
    nsub = LRU_RB // SUBLANES

    def kern(z_hbm, cw_ref, cb_ref, wg_ref, pb_ref, hf_ref, hb_ref, dh_ref,
             dx_ref, dcw_ref, dcb_ref, dwg_ref, dpb_ref, xpad, xc, dxc, sem):
        _lru_load_conv(z_hbm, xpad, xc, sem, cw_ref, cb_ref, t_rows, seq)
        w = wg_ref[0]
        pb_v = pb_ref[...]
        zeros = jnp.zeros((HALO, LANES), F32)
        dxc[0:HALO, :] = zeros
        dxc[t_rows + HALO:t_rows + 2 * HALO, :] = zeros
        dwg_ref[...] = jnp.zeros_like(dwg_ref)

        def zero_blk(r, carry):
            dxc[pl.ds(pl.multiple_of(r * LRU_RB, LRU_RB) + HALO, LRU_RB), :] = jnp.zeros((LRU_RB, LANES), F32)
            return carry

        lax.fori_loop(0, nblk, zero_blk, 0)
        dirs = []
        for rev in (False, True):
            o = 2 if rev else 0
            lam = pb_v[5:6, :] if rev else pb_v[4:5, :]
            dirs.append(dict(rev=rev, o=o, ba=pb_v[o:o + 1, :], bx=pb_v[o + 1:o + 2, :], sp8=LRU_C * _softplus(-lam),
                             dlam_scale=LRU_C * _sigmoid(-lam), h_ref=hb_ref if rev else hf_ref,
                             w=w[:, o * LANES:(o + 2) * LANES]))
        rows_b = lax.broadcasted_iota(jnp.int32, (LRU_RB, LANES), 0)

        def blk(it, carries):
            new = []
            for d, (nu_c, acc_ba, acc_bx, acc_lam) in zip(dirs, carries):
                rev, o, h_ref = d["rev"], d["o"], d["h_ref"]
                adj_rev = not rev
                r = (nblk - 1 - it) if adj_rev else it
                r0 = pl.multiple_of(r * LRU_RB, LRU_RB)
                xcb = xc[pl.ds(r0, LRU_RB), :]
                xc16 = xcb.astype(BF16)
                pre = _dot(xc16, d["w"])
                gr, gi, a, s = _lru_gates(xcb, pre[:, 0:LANES], pre[:, LANES:2 * LANES], d["ba"], d["bx"], d["sp8"])
                dhb = dh_ref[pl.ds(r0, LRU_RB), :]
                hb = h_ref[pl.ds(r0, LRU_RB), :]
                if rev:
                    nxt0 = pl.multiple_of(jnp.minimum(r0 + LRU_RB, t_rows - SUBLANES), SUBLANES)
                    edge = h_ref[pl.ds(nxt0, SUBLANES), :][0:1, :] * (r < nblk - 1).astype(F32)
                    h_prev = jnp.concatenate([hb[1:], edge], axis=0)
                else:
                    prv0 = pl.multiple_of(jnp.maximum(r0 - SUBLANES, 0), SUBLANES)
                    edge = h_ref[pl.ds(prv0, SUBLANES), :][SUBLANES - 1:SUBLANES, :] * (r > 0).astype(F32)
                    h_prev = jnp.concatenate([edge, hb[:-1]], axis=0)
                nu, nu_out = _scan_block(a, a * dhb, nu_c, adj_rev)
                cin = jnp.broadcast_to(nu_c[0:1, :], (LRU_RB, LANES))
                if adj_rev:
                    nu_next = jnp.where(rows_b == LRU_RB - 1, cin, pltpu.roll(nu, LRU_RB - 1, 0))
                else:
                    nu_next = jnp.where(rows_b == 0, cin, pltpu.roll(nu, 1, 0))
                du = dhb + nu_next
                da = du * h_prev
                ix = gi * xcb
                dlog = da * a - du * ix * (a * a) / s
                d_i = du * s * xcb
                dr = -dlog * d["sp8"]
                dpa = dr * gr * (1.0 - gr)
                dpx = d_i * gi * (1.0 - gi)
                dp16 = jnp.concatenate([dpa, dpx], axis=1).astype(BF16)
                dxc[pl.ds(r0 + HALO, LRU_RB), :] += du * s * gi + _dot(dp16, d["w"], NT)
                dwg_ref[0, :, o * LANES:(o + 2) * LANES] += _dot(xc16, dp16, TN)
                new.append((nu_out, acc_ba + jnp.sum(dpa, axis=0, keepdims=True),
                            acc_bx + jnp.sum(dpx, axis=0, keepdims=True),
                            acc_lam + jnp.sum(dlog * gr, axis=0, keepdims=True)))
            return tuple(new)

        z1 = jnp.zeros((1, LANES), F32)
        init = (jnp.zeros((SUBLANES, LANES), F32), z1, z1, z1)
        (_, ba_f, bx_f, lam_f), (_, ba_b, bx_b, lam_b) = lax.fori_loop(0, nblk, blk, (init, init), unroll=2)
        dpb_ref[...] = jnp.concatenate([ba_f, bx_f, ba_b, bx_b, lam_f * dirs[0]["dlam_scale"],
                                        lam_b * dirs[1]["dlam_scale"], z1, z1], axis=0)

        def mask_blk(r, carry):
            r0 = pl.multiple_of(r * LRU_RB, LRU_RB)
            dxc[pl.ds(r0 + HALO, LRU_RB), :] = dxc[pl.ds(r0 + HALO, LRU_RB), :] * _row_mask(r0, LRU_RB, seq)
            return carry

        lax.fori_loop(0, nblk, mask_blk, 0)

        cw = cw_ref[...]

        def conv_bwd(r, carry):
            r0 = pl.multiple_of(r * LRU_RB, LRU_RB)
            gt = _conv_taps(dxc, r0, LRU_RB, (2, 1, 0, -1))
            xt = _conv_taps(xpad, r0, LRU_RB)
            dx_ref[pl.ds(r0, LRU_RB), :] = (cw[0:1, :] * gt[0] + cw[1:2, :] * gt[1] + cw[2:3, :] * gt[2]
                                           + cw[3:4, :] * gt[3])
            g = gt[2]
            new = [carry[k] + jnp.sum(g * xt[k], axis=0, keepdims=True) for k in range(4)]
            new.append(carry[4] + jnp.sum(g, axis=0, keepdims=True))
            return tuple(new)

        z1 = jnp.zeros((1, LANES), F32)
        sums = lax.fori_loop(0, nblk, conv_bwd, (z1, z1, z1, z1, z1))
        dcw_ref[...] = jnp.concatenate(sums[:4], axis=0)
        dcb_ref[...] = sums[4]

    col = lambda j: (0, j)
    return pl.pallas_call(
        kern, name=name, grid=(LRU_W // LANES,),
        in_specs=_lru_specs(t_rows) + [pl.BlockSpec((t_rows, LANES), col)] * 3,
        out_specs=[pl.BlockSpec((t_rows, LANES), col), pl.BlockSpec((4, LANES), col), pl.BlockSpec((1, LANES), col),
                   pl.BlockSpec((1, LANES, 4 * LANES), lambda j: (j, 0, 0)), pl.BlockSpec((SUBLANES, LANES), col)],
        out_shape=[jax.ShapeDtypeStruct((t_rows, LRU_W), F32), jax.ShapeDtypeStruct((4, LRU_W), F32),
                   jax.ShapeDtypeStruct((1, LRU_W), F32), jax.ShapeDtypeStruct((4, LANES, 4 * LANES), F32),
                   jax.ShapeDtypeStruct((SUBLANES, LRU_W), F32)],
        scratch_shapes=[pltpu.VMEM((t_rows + 2 * HALO, LANES), F32), pltpu.VMEM((t_rows, LANES), F32),
                        pltpu.VMEM((t_rows + 2 * HALO, LANES), F32), pltpu.SemaphoreType.DMA],
        compiler_params=_cparams(("arbitrary",)),
    )(z_lru, conv_w, conv_b, wg, pb, h_f, h_b, dh)


QK = GLA_H * GLA_DK


def _cumsum_rows(x, rev):
    n = x.shape[0]
    rows = lax.broadcasted_iota(jnp.int32, x.shape, 0)
    d = 1
    while d < n:
        if rev:
            x = x + jnp.where(rows < n - d, pltpu.roll(x, n - d, 0), 0.0)
        else:
            x = x + jnp.where(rows >= d, pltpu.roll(x, d, 0), 0.0)
        d *= 2
    return x


def _gla_chunk_terms(q, k, zg, wg, bg, rev):
    pre = (_dot(zg.astype(BF16), wg) + bg)[:, (QK if rev else 0):(2 * QK if rev else QK)]
    g = (jnp.minimum(pre, 0.0) - jnp.log(1.0 + jnp.exp(-jnp.abs(pre)))) * (1.0 / GLA_GATE_NORM)
    b = _cumsum_rows(g, rev)
    b_last = b[0:1, :] if rev else b[CHUNK - 1:CHUNK, :]
    b_mid = b[CHUNK // 2:CHUNK // 2 + 1, :]
    qs = q * (GLA_DK ** -0.5)
    terms = dict(
        pre=pre, qs=qs, k=k,
        eb=jnp.exp(b), ek=jnp.exp(b_last - b), e_last=jnp.exp(b_last),
        eqm=jnp.exp(jnp.minimum(b - b_mid, EXP_CLAMP)), ekm=jnp.exp(jnp.minimum(b_mid - b, EXP_CLAMP)))
    return terms


def _gla_mask(rev):
    t = lax.broadcasted_iota(jnp.int32, (CHUNK, CHUNK), 0)
    s = lax.broadcasted_iota(jnp.int32, (CHUNK, CHUNK), 1)
    return (s > t) if rev else (s <= t)


def _gla_head_ops(tm, hd):
    sl = slice(hd * GLA_DK, (hd + 1) * GLA_DK)
    q_b = (tm["qs"][:, sl] * tm["eb"][:, sl]).astype(BF16)
    k_e = (tm["k"][:, sl] * tm["ek"][:, sl]).astype(BF16)
    q_m = (tm["qs"][:, sl] * tm["eqm"][:, sl]).astype(BF16)
    k_m = (tm["k"][:, sl] * tm["ekm"][:, sl]).astype(BF16)
    return sl, q_b, k_e, q_m, k_m


def _gla_fwd(z_qk, z_v, z_zg, wg, bg, t_rows, name):
    nc = t_rows // CHUNK

    def kern(qf, kf, vf, zf, qb, kb, vb, zb, wg_ref, bg_ref, of_ref, ob_ref, sf_ref, sb_ref, st_f, st_b):
        i = pl.program_id(0)

        @pl.when(i == 0)
        def _():
            st_f[...] = jnp.zeros_like(st_f)
            st_b[...] = jnp.zeros_like(st_b)

        wg_v, bg_v = wg_ref[...], bg_ref[...]
        for rev, (q_r, k_r, v_r, z_r, o_r, s_r, st) in ((False, (qf, kf, vf, zf, of_ref, sf_ref, st_f)),
                                                        (True, (qb, kb, vb, zb, ob_ref, sb_ref, st_b))):
            tm = _gla_chunk_terms(q_r[...], k_r[...], z_r[...], wg_v, bg_v, rev)
            mask = _gla_mask(rev)
            v = v_r[...]
            for hd in range(GLA_H):
                sl, q_b, k_e, q_m, k_m = _gla_head_ops(tm, hd)
                vs = slice(hd * GLA_DV, (hd + 1) * GLA_DV)
                v16 = v[:, vs].astype(BF16)
                a = jnp.where(mask, _dot(q_m, k_m, NT), 0.0).astype(BF16)
                s_in = st[hd]
                s_r[0, hd] = s_in
                o_r[:, vs] = _dot(a, v16) + _dot(q_b, s_in.astype(BF16), NT)
                st[hd] = s_in * tm["e_last"][:, sl] + _dot(v16, k_e, TN)

    fw = lambda c: (lambda i: (i, c))
    rv = lambda c: (lambda i: (nc - 1 - i, c))
    def side(ix):
        return [pl.BlockSpec((CHUNK, QK), ix(0)), pl.BlockSpec((CHUNK, QK), ix(1)),
                pl.BlockSpec((CHUNK, GLA_W), ix(0)), pl.BlockSpec((CHUNK, LANES), ix(0))]
    st_shape = (nc, GLA_H, GLA_DV, GLA_DK)
    return pl.pallas_call(
        kern, name=name, grid=(nc,),
        in_specs=side(fw) + side(rv) + [pl.BlockSpec(wg.shape, lambda i: (0, 0)), pl.BlockSpec(bg.shape, lambda i: (0, 0))],
        out_specs=[pl.BlockSpec((CHUNK, GLA_W), fw(0)), pl.BlockSpec((CHUNK, GLA_W), rv(0)),
                   pl.BlockSpec((1,) + st_shape[1:], lambda i: (i, 0, 0, 0)),
                   pl.BlockSpec((1,) + st_shape[1:], lambda i: (nc - 1 - i, 0, 0, 0))],
        out_shape=[jax.ShapeDtypeStruct((t_rows, GLA_W), F32)] * 2 + [jax.ShapeDtypeStruct(st_shape, F32)] * 2,
        scratch_shapes=[pltpu.VMEM(st_shape[1:], F32)] * 2,
        compiler_params=_cparams(("arbitrary",)),
    )(z_qk, z_qk, z_v, z_zg, z_qk, z_qk, z_v, z_zg, wg, bg)


def _gla_bwd(z_qk, z_v, z_zg, wg, bg, do, s_f, s_b, t_rows, name):
    nc = t_rows // CHUNK

    def kern(qf, kf, vf, zf, dof, ssf, qb, kb, vb, zb, dob, ssb, wg_ref, bg_ref,
             dqf, dkf, dvf, dpf, dqb, dkb, dvb, dpb, ds_f, ds_b):
        i = pl.program_id(0)

        @pl.when(i == 0)
        def _():
            ds_f[...] = jnp.zeros_like(ds_f)
            ds_b[...] = jnp.zeros_like(ds_b)

        wg_v, bg_v = wg_ref[...], bg_ref[...]
        sides = ((False, (qf, kf, vf, zf, dof, ssf, dqf, dkf, dvf, dpf, ds_f)),
                 (True, (qb, kb, vb, zb, dob, ssb, dqb, dkb, dvb, dpb, ds_b)))
        for rev, (q_r, k_r, v_r, z_r, do_r, ss_r, dq_r, dk_r, dv_r, dp_r, ds) in sides:
            tm = _gla_chunk_terms(q_r[...], k_r[...], z_r[...], wg_v, bg_v, rev)
            mask = _gla_mask(rev)
            v = v_r[...]
            d_o = do_r[...]
            db_parts, cross_parts = [], []
            for hd in range(GLA_H):
                sl, q_b, k_e, q_m, k_m = _gla_head_ops(tm, hd)
                vs = slice(hd * GLA_DV, (hd + 1) * GLA_DV)
                v16, do16 = v[:, vs].astype(BF16), d_o[:, vs].astype(BF16)
                a = jnp.where(mask, _dot(q_m, k_m, NT), 0.0).astype(BF16)
                da = jnp.where(mask, _dot(do16, v16, NT), 0.0).astype(BF16)
                s_in = ss_r[0, hd]
                s_in16 = s_in.astype(BF16)
                ds_out = ds[hd]
                ds16 = ds_out.astype(BF16)
                dv_r[:, vs] = _dot(a, do16, TN) + _dot(k_e, ds16, NT)
                dqs = _dot(da, k_m) * tm["eqm"][:, sl] + _dot(do16, s_in16) * tm["eb"][:, sl]
                dk = _dot(da, q_m, TN) * tm["ekm"][:, sl] + _dot(v16, ds16) * tm["ek"][:, sl]
                ds[hd] = ds_out * tm["e_last"][:, sl] + _dot(do16, q_b, TN)
                dq_r[:, sl] = dqs
                dk_r[:, sl] = dk
                db_parts.append(tm["qs"][:, sl] * dqs - tm["k"][:, sl] * dk)
                s_out = s_in * tm["e_last"][:, sl] + _dot(v16, k_e, TN)
                cross_parts.append(jnp.sum(s_out * ds_out, axis=0, keepdims=True))
            d_b = jnp.concatenate(db_parts, axis=1)
            dg = _cumsum_rows(d_b, not rev) + jnp.concatenate(cross_parts, axis=1)
            dp_r[...] = dg * (1.0 / GLA_GATE_NORM) * _sigmoid(-tm["pre"])

    fw = lambda c: (lambda i: (nc - 1 - i, c))
    rv = lambda c: (lambda i: (i, c))
    st_blk = (1, GLA_H, GLA_DV, GLA_DK)
    def side(ix, st_ix):
        return [pl.BlockSpec((CHUNK, QK), ix(0)), pl.BlockSpec((CHUNK, QK), ix(1)),
                pl.BlockSpec((CHUNK, GLA_W), ix(0)), pl.BlockSpec((CHUNK, LANES), ix(0)),
                pl.BlockSpec((CHUNK, GLA_W), ix(0)), pl.BlockSpec(st_blk, st_ix)]
    def oside(ix):
        return [pl.BlockSpec((CHUNK, QK), ix(0)), pl.BlockSpec((CHUNK, QK), ix(0)),
                pl.BlockSpec((CHUNK, GLA_W), ix(0)), pl.BlockSpec((CHUNK, QK), ix(0))]
    o_shapes = [jax.ShapeDtypeStruct((t_rows, QK), F32), jax.ShapeDtypeStruct((t_rows, QK), F32),
                jax.ShapeDtypeStruct((t_rows, GLA_W), F32), jax.ShapeDtypeStruct((t_rows, QK), F32)]
    return pl.pallas_call(
        kern, name=name, grid=(nc,),
        in_specs=(side(fw, lambda i: (nc - 1 - i, 0, 0, 0)) + side(rv, lambda i: (i, 0, 0, 0))
                  + [pl.BlockSpec(wg.shape, lambda i: (0, 0)), pl.BlockSpec(bg.shape, lambda i: (0, 0))]),
        out_specs=oside(fw) + oside(rv), out_shape=o_shapes * 2,
        scratch_shapes=[pltpu.VMEM((GLA_H, GLA_DV, GLA_DK), F32)] * 2,
        compiler_params=_cparams(("arbitrary",)),
    )(z_qk, z_qk, z_v, z_zg, do, s_f, z_qk, z_qk, z_v, z_zg, do, s_b, wg, bg)


FLAT_W = 1024


def _adam_math(wv, gv, mv, vv):
    bc1 = 1.0 - ADAM_B1 ** ADAM_STEP
    bc2 = 1.0 - ADAM_B2 ** ADAM_STEP
    m_new = ADAM_B1 * mv + (1.0 - ADAM_B1) * gv
    v_new = ADAM_B2 * vv + (1.0 - ADAM_B2) * (gv * gv)
    delta = -ADAM_LR * ((m_new / bc1) / (jnp.sqrt(v_new / bc2) + ADAM_EPS) + ADAM_WD * wv)
    return delta, m_new, v_new


def _row_tile(rows, cap=256):
    t = cap
    while rows % t:
        t //= 2
    assert t >= SUBLANES, rows
    return t


def _adamw_layers(w, m, v, g_layers, name):
    depth, rows, cols = w.shape
    tr = _row_tile(rows)

    def kern(w_ref, m_ref, v_ref, *rest):
        g_refs, (go_ref, d_ref, mo_ref, vo_ref) = rest[:depth], rest[depth:]
        l = pl.program_id(0)
        gv = g_refs[0][...]
        for k in range(1, depth):
            gv = jnp.where(l == k, g_refs[k][...], gv)
        delta, m_new, v_new = _adam_math(w_ref[...], gv, m_ref[...], v_ref[...])
        go_ref[...] = gv
        d_ref[...] = delta
        mo_ref[...] = m_new
        vo_ref[...] = v_new

    stacked = pl.BlockSpec((None, tr, cols), lambda l, i: (l, i, 0))
    return pl.pallas_call(
        kern, name=name, grid=(depth, rows // tr),
        in_specs=[stacked] * 3 + [pl.BlockSpec((tr, cols), lambda l, i: (i, 0))] * depth,
        out_specs=[stacked] * 4, out_shape=[jax.ShapeDtypeStruct(w.shape, F32)] * 4,
        compiler_params=_cparams(("arbitrary", "arbitrary")),
    )(w, m, v, *g_layers)


def _adamw_flat(w, g, m, v):
    rows = w.shape[0]

    def body(i, ins, consts, outs, accs):
        delta, m_new, v_new = _adam_math(*[r[...] for r in ins])
        outs[0][...] = delta
        outs[1][...] = m_new
        outs[2][...] = v_new

    w_, g_, m_, v_ = w, g, m, v
    return _rows(body, rows, _row_tile(rows), [(w_, FLAT_W, 0), (g_, FLAT_W, 0), (m_, FLAT_W, 0), (v_, FLAT_W, 0)],
                 [], [(FLAT_W, F32)] * 3, [], "adamw_small")


def _add_sibling(g, got, c_idx, name):
    _, _, rows, cols = g.shape
    tr = _row_tile(rows)

    def kern(c_ref, g_ref, got_ref, o_ref):
        o_ref[...] = (g_ref[...] + got_ref[...]).astype(BF16)

    grid_spec = pltpu.PrefetchScalarGridSpec(
        num_scalar_prefetch=1, grid=(N_CHIPS, rows // tr),
        in_specs=[pl.BlockSpec((None, None, tr, cols), lambda q, i, c_ref: (q, c_ref[0], i, 0)),
                  pl.BlockSpec((None, tr, cols), lambda q, i, c_ref: (q, i, 0))],
        out_specs=pl.BlockSpec((None, tr, cols), lambda q, i, c_ref: (q, i, 0)))
    return pl.pallas_call(
        kern, name=name, grid_spec=grid_spec, out_shape=jax.ShapeDtypeStruct((N_CHIPS, rows, cols), BF16),
        compiler_params=_cparams(("arbitrary", "arbitrary")),
    )(c_idx, g, got)


def _sum_chips(x, name):
    _, rows, cols = x.shape
    tr = _row_tile(rows)

    def kern(x_ref, o_ref):
        xv = x_ref[...].astype(F32)
        o_ref[...] = ((xv[0] + xv[1]) + xv[2]) + xv[3]

    return pl.pallas_call(
        kern, name=name, grid=(rows // tr,),
        in_specs=[pl.BlockSpec((N_CHIPS, tr, cols), lambda i: (0, i, 0))],
        out_specs=pl.BlockSpec((tr, cols), lambda i: (i, 0)),
        out_shape=jax.ShapeDtypeStruct((rows, cols), F32),
        compiler_params=_cparams(("arbitrary",)),
    )(x)


def _place():
    x, y, c = lax.axis_index("x"), lax.axis_index("y"), lax.axis_index("c")
    return x, y, c


def _other_chips(x, y):
    return [(1 - x, y), (x, 1 - y), (1 - x, 1 - y)]


def _hbm_call(kern, name, ins, out_shapes, n_sems, stage=()):
    return pl.pallas_call(
        kern, name=name,
        in_specs=[pl.BlockSpec(memory_space=pl.ANY)] * len(ins),
        out_specs=[pl.BlockSpec(memory_space=pl.ANY)] * len(out_shapes),
        out_shape=out_shapes,
        scratch_shapes=[pltpu.SemaphoreType.DMA((n,)) for n in n_sems] + [pltpu.VMEM(s, d) for s, d in stage],
        compiler_params=pltpu.CompilerParams(has_side_effects=True, vmem_limit_bytes=VMEM_LIMIT),
    )(*ins)


class _StagedCopies:
    def __init__(self, pairs, bufs, sem_in, sem_out):
        self.ins = [pltpu.make_async_copy(src, buf, sem_in.at[i]) for i, ((src, _), buf) in enumerate(zip(pairs, bufs))]
        self.outs = [pltpu.make_async_copy(buf, dst, sem_out.at[i]) for i, ((_, dst), buf) in enumerate(zip(pairs, bufs))]

    def load(self):
        for cp in self.ins:
            cp.start()

    def store(self):
        for cp in self.ins:
            cp.wait()
        for cp in self.outs:
            cp.start()

    def finish(self):
        for cp in self.outs:
            cp.wait()


DMA_CHUNK_BYTES = 1 << 20
DMA_MAX_CHUNKS = 4


def _row_chunks(rows, row_bytes, align=16):
    units = rows // align
    k = max(1, min(DMA_MAX_CHUNKS, -(-rows * row_bytes // DMA_CHUNK_BYTES), units))
    out, start = [], 0
    for i in range(k):
        size = (units // k + (1 if i < units % k else 0)) * align
        out.append((start, size))
        start += size
    if start < rows:
        out[-1] = (out[-1][0], out[-1][1] + rows - start)
    return out


def _row_bytes(shape, dtype):
    return math.prod(shape[1:]) * jnp.dtype(dtype).itemsize


def _remote(src, dst, send, recv, idx, dev):
    return pltpu.make_async_remote_copy(src, dst, send.at[idx], recv.at[idx], device_id=dev, device_id_type=MESH_ID)


def _allgather_chips(xs, name):
    na = len(xs)
    chunks = [_row_chunks(x.shape[1], _row_bytes(x.shape[1:], x.dtype)) for x in xs]
    n_cp = 3 * sum(len(ch) for ch in chunks)

    def kern(*refs):
        x_refs, o_refs = refs[:na], refs[na:2 * na]
        send1, recv1, send2, recv2, loc_in, loc_out = refs[2 * na:2 * na + 6]
        px, py, c = _place()
        me = 2 * px + py
        sib = (px, py, 1 - c)
        own = _StagedCopies([(x, o.at[me]) for x, o in zip(x_refs, o_refs)], refs[2 * na + 6:], loc_in, loc_out)
        own.load()
        plan = [(a, qx, qy, pl.ds(s, n)) for a in range(na) for (qx, qy) in _other_chips(px, py) for (s, n) in chunks[a]]
        first = [_remote(x_refs[a].at[c, rows], o_refs[a].at[me, c, rows], send1, recv1, i, (qx, qy, c))
                 for i, (a, qx, qy, rows) in enumerate(plan)]
        for cp in first:
            cp.start()
        own.store()
        passed = []
        for i, (a, qx, qy, rows) in enumerate(plan):
            slot = o_refs[a].at[2 * qx + qy, c, rows]
            _remote(slot, slot, send1, recv1, i, (qx, qy, c)).wait_recv()
            fwd = _remote(slot, slot, send2, recv2, i, sib)
            fwd.start()
            passed.append(fwd)
        for i, (a, qx, qy, rows) in enumerate(plan):
            slot = o_refs[a].at[2 * qx + qy, 1 - c, rows]
            _remote(slot, slot, send2, recv2, i, sib).wait_recv()
        for cp in first + passed:
            cp.wait_send()
        own.finish()

    outs = [jax.ShapeDtypeStruct((N_CHIPS,) + x.shape, x.dtype) for x in xs]
    return _hbm_call(kern, name, list(xs), outs, [n_cp, n_cp, n_cp, n_cp, na, na], stage=[(x.shape, x.dtype) for x in xs])


def _sibling_halves(gs, name):
    na = len(gs)
    chunks = [_row_chunks(g.shape[2], _row_bytes(g.shape[2:], g.dtype)) for g in gs]
    n_cp = N_CHIPS * sum(len(ch) for ch in chunks)

    def kern(*refs):
        g_refs, o_refs = refs[:na], refs[na:2 * na]
        send, recv = refs[2 * na:]
        px, py, c = _place()
        plan = [(a, q, pl.ds(s, n)) for a in range(na) for q in range(N_CHIPS) for (s, n) in chunks[a]]
        cps = [_remote(g_refs[a].at[q, 1 - c, rows], o_refs[a].at[q, rows], send, recv, i, (px, py, 1 - c))
               for i, (a, q, rows) in enumerate(plan)]
        for cp in cps:
            cp.start()
        for cp in cps:
            cp.wait()

    outs = [jax.ShapeDtypeStruct((N_CHIPS,) + g.shape[2:], g.dtype) for g in gs]
    return _hbm_call(kern, name, list(gs), outs, [n_cp, n_cp])


def _chip_exchange(ps, name):
    na = len(ps)
    chunks = [_row_chunks(p.shape[1], _row_bytes(p.shape[1:], p.dtype)) for p in ps]
    n_cp = 3 * sum(len(ch) for ch in chunks)

    def kern(*refs):
        p_refs, o_refs = refs[:na], refs[na:2 * na]
        send, recv, loc_in, loc_out = refs[2 * na:2 * na + 4]
        px, py, c = _place()
        me = 2 * px + py
        own = _StagedCopies([(p.at[me], o.at[me]) for p, o in zip(p_refs, o_refs)], refs[2 * na + 4:], loc_in, loc_out)
        own.load()
        plan = [(a, qx, qy, pl.ds(s, n)) for a in range(na) for (qx, qy) in _other_chips(px, py) for (s, n) in chunks[a]]
        cps = [_remote(p_refs[a].at[2 * qx + qy, rows], o_refs[a].at[me, rows], send, recv, i, (qx, qy, c))
               for i, (a, qx, qy, rows) in enumerate(plan)]
        for cp in cps:
            cp.start()
        own.store()
        for cp in cps:
            cp.wait()
        own.finish()

    outs = [jax.ShapeDtypeStruct(p.shape, p.dtype) for p in ps]
    return _hbm_call(kern, name, list(ps), outs, [n_cp, n_cp, na, na], stage=[(p.shape[1:], p.dtype) for p in ps])


def _sibling_join(rs, name):
    na = len(rs)
    chunks = [_row_chunks(r.shape[0], _row_bytes(r.shape, r.dtype)) for r in rs]
    n_cp = sum(len(ch) for ch in chunks)

    def kern(*refs):
        r_refs, o_refs = refs[:na], refs[na:2 * na]
        send, recv, loc_in, loc_out = refs[2 * na:2 * na + 4]
        px, py, c = _place()
        own = _StagedCopies([(r, o.at[c]) for r, o in zip(r_refs, o_refs)], refs[2 * na + 4:], loc_in, loc_out)
        own.load()
        plan = [(a, pl.ds(s, n)) for a in range(na) for (s, n) in chunks[a]]
        cps = [_remote(r_refs[a].at[rows], o_refs[a].at[c, rows], send, recv, i, (px, py, 1 - c))
               for i, (a, rows) in enumerate(plan)]
        for cp in cps:
            cp.start()
        own.store()
        for cp in cps:
            cp.wait()
        own.finish()

    outs = [jax.ShapeDtypeStruct((2,) + r.shape, r.dtype) for r in rs]
    return _hbm_call(kern, name, list(rs), outs, [n_cp, n_cp, na, na], stage=[(r.shape, r.dtype) for r in rs])


def _reduce_pack(gs, c_idx, tag):
    got = _sibling_halves(gs, "reduce_sibling_halves" + tag)
    part = [_add_sibling(g, o, c_idx, f"reduce_add_sibling{tag}_{a}") for a, (g, o) in enumerate(zip(gs, got))]
    landed = _chip_exchange(part, "reduce_chip_exchange" + tag)
    mine = [_sum_chips(x, f"reduce_sum_chips{tag}_{a}") for a, x in enumerate(landed)]
    return _sibling_join(mine, "reduce_sibling_join" + tag)


WEIGHTS = ["meta_tokens", "norm_mix_pre", "norm_mix_post", "norm_mlp_pre", "norm_mlp_post", "w_in", "conv_w",
           "conv_b", "lru_wa_f", "lru_ba_f", "lru_wx_f", "lru_bx_f", "lru_lambda_f", "lru_wa_b", "lru_ba_b",
           "lru_wx_b", "lru_bx_b", "lru_lambda_b", "gla_wg_f", "gla_bg_f", "gla_wg_b", "gla_bg_b", "gla_head_norm",
           "w_out", "w_mlp_up", "w_mlp_down"]
BIG = ("w_in", "w_out", "w_mlp_up", "w_mlp_down")
SMALL = [n for n in WEIGHTS if n not in BIG]
SMALL_SHARDED = {"meta_tokens": 1, "conv_w": 2, "gla_wg_f": 2, "gla_wg_b": 2}
N_CHIPS = 4
SMALL_RAW = [("g1", (1, D_MODEL)), ("g2", (1, D_MODEL)), ("g3", (1, D_MODEL)), ("g4", (1, D_MODEL)),
             ("conv_w", (4, LRU_W)), ("conv_b", (1, LRU_W)), ("lru_wg", (4, LANES, 4 * LANES)),
             ("lru_pb", (SUBLANES, LRU_W)), ("gla_wg", (2 * GLA_RANK, 2 * QK)), ("gla_bg", (1, 2 * QK)),
             ("head_norm", (1, GLA_W))]
META_SHAPE = (N_META, D_MODEL)


def _pad_to(v, n):
    return jnp.pad(v, (0, n - v.shape[0]))


def _round_up(n, m):
    return -(-n // m) * m


def _flat_cat(arrs, total):
    return _pad_to(jnp.concatenate([a.reshape(-1) for a in arrs]), total)


def _split_flat(flat, shapes):
    out, off = [], 0
    for s in shapes:
        n = math.prod(s)
        out.append(flat[off:off + n].reshape(s))
        off += n
    return out


def _my_shard(full, axis, chip):
    n = full.shape[axis] // N_CHIPS
    return lax.dynamic_slice_in_dim(full, chip * n, n, axis=axis)


def _block_diag_gates(wa_f, wx_f, wa_b, wx_b):
    def bd(w):
        z = jnp.zeros((LRU_HD, LRU_HD), w.dtype)
        return jnp.stack([jnp.block([[w[2 * j], z], [z, w[2 * j + 1]]]) for j in range(4)])
    return jnp.concatenate([bd(wa_f), bd(wx_f), bd(wa_b), bd(wx_b)], axis=2).astype(BF16)


def _gate_diag_blocks(dwg, o):
    blk = dwg[:, :, o * LANES:(o + 1) * LANES]
    return jnp.stack([blk[j, hh * LRU_HD:(hh + 1) * LRU_HD, hh * LRU_HD:(hh + 1) * LRU_HD]
                      for j in range(4) for hh in range(2)])


def kernel(x, meta_tokens, norm_mix_pre, norm_mix_post, norm_mlp_pre, norm_mlp_post, w_in, conv_w, conv_b, lru_wa_f, lru_ba_f, lru_wx_f, lru_bx_f, lru_lambda_f, lru_wa_b, lru_ba_b, lru_wx_b, lru_bx_b, lru_lambda_b, gla_wg_f, gla_bg_f, gla_wg_b, gla_bg_b, gla_head_norm, w_out, w_mlp_up, w_mlp_down, loss_target, m_meta_tokens, m_norm_mix_pre, m_norm_mix_post, m_norm_mlp_pre, m_norm_mlp_post, m_w_in, m_conv_w, m_conv_b, m_lru_wa_f, m_lru_ba_f, m_lru_wx_f, m_lru_bx_f, m_lru_lambda_f, m_lru_wa_b, m_lru_ba_b, m_lru_wx_b, m_lru_bx_b, m_lru_lambda_b, m_gla_wg_f, m_gla_bg_f, m_gla_wg_b, m_gla_bg_b, m_gla_head_norm, m_w_out, m_w_mlp_up, m_w_mlp_down, v_meta_tokens, v_norm_mix_pre, v_norm_mix_post, v_norm_mlp_pre, v_norm_mlp_post, v_w_in, v_conv_w, v_conv_b, v_lru_wa_f, v_lru_ba_f, v_lru_wx_f, v_lru_bx_f, v_lru_lambda_f, v_lru_wa_b, v_lru_ba_b, v_lru_wx_b, v_lru_bx_b, v_lru_lambda_b, v_gla_wg_f, v_gla_bg_f, v_gla_wg_b, v_gla_bg_b, v_gla_head_norm, v_w_out, v_w_mlp_up, v_w_mlp_down):
    args = dict(locals())
    w_loc = {n: args[n] for n in WEIGHTS}
    m_loc = {n: args["m_" + n] for n in WEIGHTS}
    v_loc = {n: args["v_" + n] for n in WEIGHTS}
    seq = x.shape[1]
    t_rows = FRONT + seq + BACK
    assert t_rows % 768 == 0 and seq % FRONT == 0, seq
    chip = 2 * lax.axis_index("x") + lax.axis_index("y")
    c_idx = lax.axis_index("c").astype(jnp.int32).reshape(1)

    def halves(a):
        return a.reshape((2, a.shape[0] // 2) + a.shape[1:])

    big16 = {n: w_loc[n].astype(BF16) for n in BIG}
    sm_names = list(SMALL_SHARDED)
    sm_len = _round_up(sum(w_loc[n].size for n in sm_names), 2 * SUBLANES * FLAT_W)
    sm_pack = _flat_cat([w_loc[n] for n in sm_names], sm_len).reshape(2, -1, FLAT_W)
    gathered = _allgather_chips([halves(big16[n][l]) for l in range(DEPTH) for n in BIG] + [sm_pack], "gather_weights")
    sm_parts = [_split_flat(gathered[-1][q].reshape(-1), [w_loc[n].shape for n in sm_names]) for q in range(N_CHIPS)]
    full = {n: jnp.concatenate([sm_parts[q][i] for q in range(N_CHIPS)], axis=SMALL_SHARDED[n])
            for i, n in enumerate(sm_names)}
    for n in SMALL:
        if n not in SMALL_SHARDED:
            full[n] = w_loc[n]
    big_full = []
    for l in range(DEPTH):
        g_in, g_out, g_up, g_down = gathered[l * len(BIG):(l + 1) * len(BIG)]
        w_in_l = jnp.transpose(g_in.reshape(N_CHIPS, D_MODEL, -1), (1, 0, 2)).reshape(D_MODEL, D_IN)
        big_full.append(dict(
            w_in=jnp.pad(w_in_l, ((0, 0), (0, D_IN_PAD - D_IN))),
            w_out=g_out.reshape(-1, D_MODEL),
            w_up=jnp.transpose(g_up.reshape(N_CHIPS, D_MODEL, -1), (1, 0, 2)).reshape(D_MODEL, D_FF),
            w_down=g_down.reshape(D_FF, D_MODEL)))

    meta_blk = jnp.concatenate([jnp.zeros((FRONT - N_META, D_MODEL), F32), full["meta_tokens"]], axis=0)
    h = jnp.concatenate([meta_blk, x[0], jnp.zeros((BACK, D_MODEL), F32)], axis=0)
    target = loss_target[0]

    layer_w = []
    for l in range(DEPTH):
        wg = jnp.zeros((LANES, 2 * QK), F32)
        wg = wg.at[0:GLA_RANK, 0:QK].set(full["gla_wg_f"][l]).at[GLA_RANK:2 * GLA_RANK, QK:].set(full["gla_wg_b"][l])
        pb = jnp.stack([full["lru_ba_f"][l], full["lru_bx_f"][l], full["lru_ba_b"][l], full["lru_bx_b"][l],
                        full["lru_lambda_f"][l], full["lru_lambda_b"][l], jnp.zeros((LRU_W,), F32),
                        jnp.zeros((LRU_W,), F32)])
        layer_w.append(dict(
            big_full[l],
            g1=full["norm_mix_pre"][l][None], g2=full["norm_mix_post"][l][None],
            g3=full["norm_mlp_pre"][l][None], g4=full["norm_mlp_post"][l][None],
            conv_w=full["conv_w"][l], conv_b=full["conv_b"][l][None],
            lru_wg=_block_diag_gates(full["lru_wa_f"][l], full["lru_wx_f"][l], full["lru_wa_b"][l], full["lru_wx_b"][l]),
            lru_pb=pb, gla_wg=wg.astype(BF16),
            gla_bg=jnp.concatenate([full["gla_bg_f"][l], full["gla_bg_b"][l]])[None],
            head_norm=full["gla_head_norm"][l][None]))

    def split_z(acc):
        return (acc[:, 0:1024], acc[:, 1024:1536], acc[:, 1536:2048], acc[:, 2048:2560], acc[:, ZG_OFF:ZG_OFF + LANES])

    def relu2(acc):
        r = jnp.maximum(acc, 0.0)
        return acc, r * r

    tm_e = 384

    def post_mix(acc, hh, g2, g3):
        h1 = hh + _rms_fwd(acc, g2)
        return acc, h1, _rms_fwd(h1, g3)

    def post_mlp(acc, h1, g4, g1_next):
        h2 = h1 + _rms_fwd(acc, g4)
        return acc, h2, _rms_fwd(h2, g1_next)

    def post_mlp_loss(acc, h1, tgt, g4):
        h2 = h1 + _rms_fwd(acc, g4)
        rows = pl.program_id(0) * tm_e + lax.broadcasted_iota(jnp.int32, (tm_e, 1), 0)
        err = jnp.where(jnp.logical_and(rows >= FRONT, rows < FRONT + seq), h2 - tgt, 0.0)
        d_h = err * (1.0 / D_MODEL)
        dff, dg4 = _rms_bwd(acc, g4, d_h)
        loss_part = jnp.zeros((SUBLANES, LANES), F32) + jnp.sum(err * err) * (0.5 / D_MODEL)
        return d_h, dff, loss_part, dg4

    row3 = [(D_MODEL, F32), (D_MODEL, F32), (D_MODEL, BF16)]
    vec = ((1, D_MODEL), F32)
    target_p = jnp.concatenate([jnp.zeros((FRONT, D_MODEL), F32), target, jnp.zeros((BACK, D_MODEL), F32)], axis=0)
    saved = []
    n1 = _norm_cast(h, layer_w[0]["g1"], t_rows, "norm_mix_pre_l0")
    for l in range(DEPTH):
        lw = layer_w[l]
        tag = f"_l{l}"
        z_lru, z_qk, z_v, z_go, z_zg = _mm(
            n1, lw["w_in"], "nn", 384, D_IN_PAD, D_MODEL,
            [(1024, F32), (512, F32), (512, F32), (512, F32), (LANES, F32)], "in_proj" + tag, epilogue=split_z)
        h_f, h_b = _lru_fwd(z_lru, lw["conv_w"], lw["conv_b"], lw["lru_wg"], lw["lru_pb"], t_rows, seq, "lru_fwd" + tag)
        o_f, o_b, s_f, s_b = _gla_fwd(z_qk, z_v, z_zg, lw["gla_wg"], lw["gla_bg"], t_rows, "gla_fwd" + tag)
        ymix = _mix_fwd(h_f, h_b, z_lru, o_f, o_b, z_go, lw["head_norm"], t_rows, "mix_fwd" + tag)
        mix, h1, n3 = _mm(ymix, lw["w_out"], "nn", tm_e, D_MODEL, D_MODEL, row3, "out_proj" + tag, epilogue=post_mix,
                          rows_in=(h,), consts=(lw["g2"], lw["g3"]))
        u, act = _mm(n3, lw["w_up"], "nn", 768, 1024, D_MODEL, [(D_FF, BF16), (D_FF, BF16)], "mlp_up" + tag,
                     epilogue=relu2)
        sv = dict(h=h, n1=n1, z_lru=z_lru, z_qk=z_qk, z_v=z_v, z_go=z_go, z_zg=z_zg, h_f=h_f, h_b=h_b,
                  o_f=o_f, o_b=o_b, s_f=s_f, s_b=s_b, ymix=ymix, mix=mix, h1=h1, n3=n3, u=u, act=act)
        if l + 1 < DEPTH:
            sv["ff"], h, n1 = _mm(act, lw["w_down"], "nn", tm_e, D_MODEL, D_FF, row3, "mlp_down" + tag,
                                  epilogue=post_mlp, rows_in=(h1,), consts=(lw["g4"], layer_w[l + 1]["g1"]))
        else:
            dh, dff, loss_part, dg4 = _mm(act, lw["w_down"], "nn", tm_e, D_MODEL, D_FF,
                                          [(D_MODEL, F32), (D_MODEL, BF16)], "mlp_down_loss" + tag,
                                          epilogue=post_mlp_loss, rows_in=(h1, target_p), consts=(lw["g4"],),
                                          accs=[((SUBLANES, LANES), F32), vec])
        saved.append(sv)

    loss = lax.psum(loss_part[0, 0], ("x", "y", "c"))

    small_len = _round_up(sum(math.prod(s) for _, s in SMALL_RAW) + math.prod(META_SHAPE),
                          N_CHIPS * 2 * 2 * SUBLANES * FLAT_W)
    totals = [None] * DEPTH
    for l in reversed(range(DEPTH)):
        lw, sv = layer_w[l], saved[l]
        tag = f"_l{l}"
        raw = {"g4": dg4}
        gw_down = _mm(sv["act"], dff, "tn", 1024, D_MODEL, 768, [(D_MODEL, F32)], "dw_down" + tag)[0]
        du = _mm(dff, lw["w_down"], "nt", 768, 1024, D_MODEL, [(D_FF, BF16)], "d_act" + tag,
                 epilogue=lambda acc, uu: (acc * 2.0 * jnp.maximum(uu.astype(F32), 0.0),), extras=(sv["u"],))[0]
        gw_up = _mm(sv["n3"], du, "tn", D_MODEL, D_FF // N_CHIPS, 768, [(D_FF, F32)], "dw_up" + tag,
                    col_blocks_first=True)[0]

        def pre_mlp_bwd(acc, h1, d_h, mix, g3, g2):
            dx, dg3 = _rms_bwd(h1, g3, acc)
            d_h1 = d_h + dx
            d_mix, dg2 = _rms_bwd(mix, g2, d_h1)
            return d_h1, d_mix, dg3, dg2

        dh1, dmix, raw["g3"], raw["g2"] = _mm(
            du, lw["w_up"], "nt", tm_e, D_MODEL, D_FF, [(D_MODEL, F32), (D_MODEL, BF16)], "d_n3" + tag,
            epilogue=pre_mlp_bwd, rows_in=(sv["h1"], dh, sv["mix"]), consts=(lw["g3"], lw["g2"]), accs=[vec, vec])
        gw_out = _mm(sv["ymix"], dmix, "tn", D_MODEL, D_MODEL, 768, [(D_MODEL, F32)], "dw_out" + tag)[0]
        dymix = _mm(dmix, lw["w_out"], "nt", 768, D_MODEL, D_MODEL, [(D_MODEL, F32)], "d_ymix" + tag)[0]
        dh_lru, dgate, do, dgo, raw["head_norm"] = _mix_bwd(
            dymix, sv["h_f"], sv["h_b"], sv["z_lru"], sv["o_f"], sv["o_b"], sv["z_go"], lw["head_norm"], t_rows,
            "mix_bwd" + tag)
        gla_grads = _gla_bwd(sv["z_qk"], sv["z_v"], sv["z_zg"], lw["gla_wg"], lw["gla_bg"], do, sv["s_f"], sv["s_b"],
                             t_rows, "gla_bwd" + tag)
        dx_br, raw["conv_w"], raw["conv_b"], raw["lru_wg"], raw["lru_pb"] = _lru_bwd(
            sv["z_lru"], lw["conv_w"], lw["conv_b"], lw["lru_wg"], lw["lru_pb"], sv["h_f"], sv["h_b"], dh_lru,
            t_rows, seq, "lru_bwd" + tag)
        dz, dwg_gla, raw["gla_bg"] = _dz_assemble(dx_br, dgate, gla_grads, dgo, sv["z_zg"], lw["gla_wg"], t_rows,
                                                  "dz_assemble" + tag)
        raw["gla_wg"] = dwg_gla[0:2 * GLA_RANK]
        gw_in = _mm(sv["n1"], dz, "tn", D_MODEL, 896, 768, [(D_IN_PAD, F32)], "dw_in" + tag)[0]
        if l > 0:
            def pre_mix_bwd(acc, hh, d_h1, ff_prev, g1, g4_prev):
                dx, dg1 = _rms_bwd(hh, g1, acc)
                d_h = d_h1 + dx
                d_ff, dg4_prev = _rms_bwd(ff_prev, g4_prev, d_h)
                return d_h, d_ff, dg1, dg4_prev

            dh, dff, raw["g1"], dg4 = _mm(
                dz, lw["w_in"], "nt", tm_e, D_MODEL, D_IN_PAD, [(D_MODEL, F32), (D_MODEL, BF16)], "d_n1" + tag,
                epilogue=pre_mix_bwd, rows_in=(sv["h"], dh1, saved[l - 1]["ff"]),
                consts=(lw["g1"], layer_w[l - 1]["g4"]), accs=[vec, vec])
        else:
            def pre_mix_bwd0(acc, hh, d_h1, g1):
                dx, dg1 = _rms_bwd(hh, g1, acc)
                return d_h1 + dx, dg1

            dh, raw["g1"] = _mm(dz, lw["w_in"], "nt", tm_e, D_MODEL, D_IN_PAD, [(D_MODEL, F32)], "d_n1" + tag,
                                epilogue=pre_mix_bwd0, rows_in=(sv["h"], dh1), consts=(lw["g1"],), accs=[vec])

        cols = D_IN // N_CHIPS
        in_sh = jnp.stack([gw_in[:, q * cols:(q + 1) * cols] for q in range(N_CHIPS)])
        meta_g = dh[FRONT - N_META:FRONT] if l == 0 else jnp.zeros(META_SHAPE, F32)
        small = _flat_cat([raw[n] for n, _ in SMALL_RAW] + [meta_g], small_len)
        pack = [in_sh.reshape(N_CHIPS, 2, D_MODEL // 2, cols),
                gw_out.reshape(N_CHIPS, 2, -1, D_MODEL),
                gw_up.reshape(N_CHIPS, 2, D_MODEL // 2, D_FF // N_CHIPS),
                gw_down.reshape(N_CHIPS, 2, -1, D_MODEL),
                small.reshape(N_CHIPS, 2, -1, FLAT_W)]
        totals[l] = _reduce_pack(pack, c_idx, tag)

    grad_x = dh[FRONT:FRONT + seq][None]

    sm_all = _allgather_chips([totals[l][4] for l in range(DEPTH)], "gather_small_grads")
    g_tot = {}
    per_layer = []
    for l in range(DEPTH):
        pieces = _split_flat(sm_all[l].reshape(-1), [s for _, s in SMALL_RAW] + [META_SHAPE])
        per_layer.append(dict(zip([n for n, _ in SMALL_RAW] + ["meta"], pieces)))
    stack = lambda f: jnp.stack([f(per_layer[l]) for l in range(DEPTH)])
    g_tot["meta_tokens"] = _my_shard(per_layer[0]["meta"], 1, chip)
    for n, key in (("norm_mix_pre", "g1"), ("norm_mix_post", "g2"), ("norm_mlp_pre", "g3"), ("norm_mlp_post", "g4"),
                   ("conv_b", "conv_b"), ("gla_head_norm", "head_norm")):
        g_tot[n] = stack(lambda d, key=key: d[key][0])
    g_tot["conv_w"] = _my_shard(stack(lambda d: d["conv_w"]), 2, chip)
    for o, n in enumerate(("lru_wa_f", "lru_wx_f", "lru_wa_b", "lru_wx_b")):
        g_tot[n] = stack(lambda d, o=o: _gate_diag_blocks(d["lru_wg"], o))
    for row, n in enumerate(("lru_ba_f", "lru_bx_f", "lru_ba_b", "lru_bx_b", "lru_lambda_f", "lru_lambda_b")):
        g_tot[n] = stack(lambda d, row=row: d["lru_pb"][row])
    g_tot["gla_wg_f"] = _my_shard(stack(lambda d: d["gla_wg"][0:GLA_RANK, 0:QK]), 2, chip)
    g_tot["gla_wg_b"] = _my_shard(stack(lambda d: d["gla_wg"][GLA_RANK:, QK:]), 2, chip)
    g_tot["gla_bg_f"] = stack(lambda d: d["gla_bg"][0, 0:QK])
    g_tot["gla_bg_b"] = stack(lambda d: d["gla_bg"][0, QK:])

    delta, new_m, new_v = {}, {}, {}
    for a, n in enumerate(BIG):
        shp = w_loc[n].shape
        flat3 = lambda t, shp=shp: t.reshape(DEPTH, -1, shp[-1])
        g_l = [totals[l][a].reshape(-1, shp[-1]) for l in range(DEPTH)]
        outs = _adamw_layers(flat3(w_loc[n]), flat3(m_loc[n]), flat3(v_loc[n]), g_l, "adamw_" + n)
        g_tot[n], delta[n], new_m[n], new_v[n] = [o.reshape(shp) for o in outs]
    a_rows = _round_up(sum(w_loc[n].size for n in SMALL), SUBLANES * FLAT_W) // FLAT_W
    pack_small = lambda d: _flat_cat([d[n] for n in SMALL], a_rows * FLAT_W).reshape(a_rows, FLAT_W)
    outs = _adamw_flat(pack_small(w_loc), pack_small(g_tot), pack_small(m_loc), pack_small(v_loc))
    shapes = [w_loc[n].shape for n in SMALL]
    for d, o in zip((delta, new_m, new_v), outs):
        d.update(zip(SMALL, _split_flat(o.reshape(-1), shapes)))
    return (loss, grad_x, *[g_tot[n] for n in WEIGHTS], *[delta[n] for n in WEIGHTS],
            *[new_m[n] for n in WEIGHTS], *[new_v[n] for n in WEIGHTS])
```

```python
import math

import jax
import jax.numpy as jnp
from jax import lax
from jax.experimental import pallas as pl
from jax.experimental.pallas import tpu as pltpu

F32 = jnp.float32
BF16 = jnp.bfloat16

D_MODEL = 1024
DEPTH = 2
N_META = 16
LRU_W = 512
LRU_HEADS = 8
LRU_HD = 64
LRU_C = 8.0
GLA_W = 512
GLA_H = 4
GLA_DK = 64
GLA_DV = 128
GLA_RANK = 16
GLA_GATE_NORM = 16.0
D_FF = 4096
D_IN = 2592
EPS = 1e-6
ADAM_LR, ADAM_B1, ADAM_B2, ADAM_EPS, ADAM_WD, ADAM_STEP = 0.001, 0.9, 0.999, 1e-08, 0.01, 10

LANES = 128
SUBLANES = 8
FRONT = 128
BACK = 128
D_IN_PAD = 2688
ZG_OFF = 2560
CHUNK = 64
EXP_CLAMP = 80.0
VMEM_LIMIT = 56 * 1024 * 1024
MESH_ID = pl.DeviceIdType.MESH

NN = (((1,), (0,)), ((), ()))
NT = (((1,), (1,)), ((), ()))
TN = (((0,), (0,)), ((), ()))


def _dot(a, b, dims=NN):
    return lax.dot_general(a, b, dims, preferred_element_type=F32)


def _cparams(sem):
    return pltpu.CompilerParams(dimension_semantics=sem, vmem_limit_bytes=VMEM_LIMIT)


def _sigmoid(x):
    return 1.0 / (1.0 + jnp.exp(-x))


def _gelu(x):
    c = math.sqrt(2.0 / math.pi)
    return 0.5 * x * (1.0 + jnp.tanh(c * (x + 0.044715 * x * x * x)))


def _gelu_grad(x):
    c = math.sqrt(2.0 / math.pi)
    t = jnp.tanh(c * (x + 0.044715 * x * x * x))
    return 0.5 * (1.0 + t) + 0.5 * x * (1.0 - t * t) * c * (1.0 + 3.0 * 0.044715 * x * x)


def _rms_fwd(x, g):
    rstd = lax.rsqrt(jnp.mean(x * x, axis=1, keepdims=True) + EPS)
    return (x * rstd) * g


def _rms_bwd(x, g, dy):
    rstd = lax.rsqrt(jnp.mean(x * x, axis=1, keepdims=True) + EPS)
    xh = x * rstd
    dxh = dy * g
    dx = rstd * (dxh - xh * jnp.mean(dxh * xh, axis=1, keepdims=True))
    dg = jnp.sum(dy * xh, axis=0, keepdims=True)
    return dx, dg


def _mm(a, b, mode, tm, tn, tk, outs, name, epilogue=None, extras=(), col_blocks_first=False,
        rows_in=(), consts=(), accs=(), comm=None):
    if mode == "nn":
        (m, k), n = a.shape, b.shape[1]
        a_spec = pl.BlockSpec((tm, tk), lambda i, j, kk: (i, kk))
        b_spec = pl.BlockSpec((tk, tn), lambda i, j, kk: (kk, j))
        dims = NN
    elif mode == "nt":
        (m, k), n = a.shape, b.shape[0]
        a_spec = pl.BlockSpec((tm, tk), lambda i, j, kk: (i, kk))
        b_spec = pl.BlockSpec((tn, tk), lambda i, j, kk: (j, kk))
        dims = NT
    else:
        (k, m), n = a.shape, b.shape[1]
        a_spec = pl.BlockSpec((tk, tm), lambda i, j, kk: (kk, i))
        b_spec = pl.BlockSpec((tk, tn), lambda i, j, kk: (kk, j))
        dims = TN
    assert m % tm == 0 and n % tn == 0 and k % tk == 0, (name, m, n, k)
    nk = k // tk
    ne = len(extras) + len(rows_in) + len(consts)
    e_specs = [pl.BlockSpec((tm, tn), lambda i, j, kk: (i, j)) for _ in extras]
    e_specs += [pl.BlockSpec((tm, r.shape[1]), lambda i, j, kk: (i, 0)) for r in rows_in]
    e_specs += [pl.BlockSpec(c.shape, lambda i, j, kk: (0, 0)) for c in consts]
    n_out = len(outs)
    o_specs, o_shapes = [], []
    for width, dtype in outs:
        if col_blocks_first:
            assert width == n, name
            o_specs.append(pl.BlockSpec((None, tm, tn), lambda i, j, kk: (j, i, 0)))
            o_shapes.append(jax.ShapeDtypeStruct((n // tn, m, tn), dtype))
            continue
        if width == n:
            o_specs.append(pl.BlockSpec((tm, tn), lambda i, j, kk: (i, j)))
        else:
            assert n == tn, name
            o_specs.append(pl.BlockSpec((tm, width), lambda i, j, kk: (i, 0)))
        o_shapes.append(jax.ShapeDtypeStruct((m, width), dtype))
    o_specs += [pl.BlockSpec(s, lambda i, j, kk: (0, 0)) for s, _ in accs]
    o_shapes += [jax.ShapeDtypeStruct(s, d) for s, d in accs]

    def finish(acc, extra_refs, out_refs):
        res = epilogue(acc, *[e[...] for e in extra_refs]) if epilogue else (acc,)
        for o, r in zip(out_refs[:n_out], res[:n_out]):
            o[...] = r.astype(o.dtype)
        first = jnp.logical_and(pl.program_id(0) == 0, pl.program_id(1) == 0)
        for o, r in zip(out_refs[n_out:], res[n_out:]):
            @pl.when(first)
            def _(o=o):
                o[...] = jnp.zeros_like(o)

            o[...] += r

    if nk == 1:
        def body(a_ref, b_ref, *rest):
            finish(_dot(a_ref[...], b_ref[...], dims), rest[:ne], rest[ne:])
        scratch = []
    else:
        def body(a_ref, b_ref, *rest):
            acc_ref = rest[-1]
            kk = pl.program_id(2)

            @pl.when(kk == 0)
            def _():
                acc_ref[...] = jnp.zeros_like(acc_ref)

            acc_ref[...] += _dot(a_ref[...], b_ref[...], dims)

            @pl.when(kk == nk - 1)
            def _():
                finish(acc_ref[...], rest[:ne], rest[ne:-1])
        scratch = [pltpu.VMEM((tm, tn), F32)]

    sem = ("arbitrary", "arbitrary", "arbitrary") if accs else ("parallel", "parallel", "arbitrary")
    res, comm_res = _pcall(body, name, (m // tm, n // tn, nk), [a_spec, b_spec] + e_specs, o_specs, o_shapes, scratch,
                           sem, (a, b, *extras, *rows_in, *consts), comm)
    return res if comm is None else (res, comm_res)


def _rows(body, n_rows, tm, ins, consts, outs, accs, name):
    assert n_rows % tm == 0, (name, n_rows, tm)
    in_specs = [pl.BlockSpec((tm, w), (lambda i, cb=cb: (i, cb))) for _, w, cb in ins]
    in_specs += [pl.BlockSpec(c.shape, lambda i: (0, 0)) for c in consts]
    out_specs = [pl.BlockSpec((tm, w), lambda i: (i, 0)) for w, _ in outs]
    out_specs += [pl.BlockSpec(s, lambda i: (0, 0)) for s, _ in accs]
    out_shape = [jax.ShapeDtypeStruct((n_rows, w), d) for w, d in outs]
    out_shape += [jax.ShapeDtypeStruct(s, d) for s, d in accs]
    ni, nc, no = len(ins), len(consts), len(outs)

    def kern(*refs):
        body(pl.program_id(0), refs[:ni], refs[ni:ni + nc], refs[ni + nc:ni + nc + no], refs[ni + nc + no:])

    res = pl.pallas_call(
        kern, name=name, grid=(n_rows // tm,), in_specs=in_specs, out_specs=out_specs, out_shape=out_shape,
        compiler_params=_cparams(("arbitrary",)),
    )(*[a for a, _, _ in ins], *consts)
    return res


def _acc_add(i, ref, val):
    @pl.when(i == 0)
    def _():
        ref[...] = jnp.zeros_like(ref)

    ref[...] += val


def _norm_cast(h, g, t_rows, name):
    def body(i, ins, consts, outs, accs):
        outs[0][...] = _rms_fwd(ins[0][...], consts[0][...]).astype(BF16)
    return _rows(body, t_rows, 384, [(h, D_MODEL, 0)], [g], [(D_MODEL, BF16)], [], name)[0]


def _head_norm_parts(o):
    ns, rs = [], []
    for hd in range(GLA_H):
        oh = o[:, hd * GLA_DV:(hd + 1) * GLA_DV]
        r = lax.rsqrt(jnp.mean(oh * oh, axis=1, keepdims=True) + EPS)
        ns.append(oh * r)
        rs.append(r)
    return ns, rs


def _mix_fwd(h_f, h_b, z_lru, o_f, o_b, z_go, head_norm, t_rows, name):
    def body(i, ins, consts, outs, accs):
        hf, hb, gate, of, ob, go = [r[...] for r in ins]
        hn = consts[0][...]
        outs[0][:, 0:LRU_W] = ((hf + hb) * _gelu(gate)).astype(BF16)
        ns, _ = _head_norm_parts(of + ob)
        silu = go * _sigmoid(go)
        for hd in range(GLA_H):
            sl = slice(hd * GLA_DV, (hd + 1) * GLA_DV)
            outs[0][:, LRU_W + hd * GLA_DV:LRU_W + (hd + 1) * GLA_DV] = (ns[hd] * hn[:, sl] * silu[:, sl]).astype(BF16)
    ins = [(h_f, LRU_W, 0), (h_b, LRU_W, 0), (z_lru, LRU_W, 1), (o_f, GLA_W, 0), (o_b, GLA_W, 0), (z_go, GLA_W, 0)]
    return _rows(body, t_rows, 384, ins, [head_norm], [(D_MODEL, BF16)], [], name)[0]


def _mix_bwd(dymix, h_f, h_b, z_lru, o_f, o_b, z_go, head_norm, t_rows, name):
    def body(i, ins, consts, outs, accs):
        dyl, dyg, hf, hb, gate, of, ob, go = [r[...] for r in ins]
        hn = consts[0][...]
        outs[0][...] = dyl * _gelu(gate)
        outs[1][...] = dyl * (hf + hb) * _gelu_grad(gate)
        ns, rs = _head_norm_parts(of + ob)
        sg = _sigmoid(go)
        silu = go * sg
        dsilu = sg * (1.0 + go * (1.0 - sg))
        dhn = []
        for hd in range(GLA_H):
            sl = slice(hd * GLA_DV, (hd + 1) * GLA_DV)
            dy = dyg[:, sl]
            e = dy * hn[:, sl] * silu[:, sl]
            outs[2][:, sl] = rs[hd] * (e - ns[hd] * jnp.mean(e * ns[hd], axis=1, keepdims=True))
            outs[3][:, sl] = dy * ns[hd] * hn[:, sl] * dsilu[:, sl]
            dhn.append(jnp.sum(dy * ns[hd] * silu[:, sl], axis=0, keepdims=True))
        _acc_add(i, accs[0], jnp.concatenate(dhn, axis=1))
    ins = [(dymix, LRU_W, 0), (dymix, GLA_W, 1), (h_f, LRU_W, 0), (h_b, LRU_W, 0), (z_lru, LRU_W, 1),
           (o_f, GLA_W, 0), (o_b, GLA_W, 0), (z_go, GLA_W, 0)]
    return _rows(body, t_rows, 384, ins, [head_norm],
                 [(LRU_W, F32), (LRU_W, F32), (GLA_W, F32), (GLA_W, F32)], [((1, GLA_W), F32)], name)


def _dz_assemble(dx_br, dgate, gla_grads, dgo, z_zg, wg, t_rows, name):
    dq_f, dk_f, dv_f, dpre_f, dq_b, dk_b, dv_b, dpre_b = gla_grads
    qk = GLA_H * GLA_DK

    def body(i, ins, consts, outs, accs):
        (dxb, dgt, dqf, dqb, dkf, dkb, dvf, dvb, dg_o, dpf, dpb, zg) = [r[...] for r in ins]
        w = consts[0][...]
        o = outs[0]
        o[:, 0:LRU_W] = dxb.astype(BF16)
        o[:, LRU_W:2 * LRU_W] = dgt.astype(BF16)
        o[:, 1024:1024 + qk] = ((dqf + dqb) * (GLA_DK ** -0.5)).astype(BF16)
        o[:, 1280:1280 + qk] = (dkf + dkb).astype(BF16)
        o[:, 1536:1536 + GLA_W] = (dvf + dvb).astype(BF16)
        o[:, 2048:2048 + GLA_W] = dg_o.astype(BF16)
        dpre = jnp.concatenate([dpf, dpb], axis=1)
        dpre16 = dpre.astype(BF16)
        o[:, ZG_OFF:ZG_OFF + LANES] = _dot(dpre16, w, NT).astype(BF16)
        _acc_add(i, accs[0], _dot(zg.astype(BF16), dpre16, TN))
        _acc_add(i, accs[1], jnp.sum(dpre, axis=0, keepdims=True))

    ins = [(dx_br, LRU_W, 0), (dgate, LRU_W, 0), (dq_f, qk, 0), (dq_b, qk, 0), (dk_f, qk, 0), (dk_b, qk, 0),
           (dv_f, GLA_W, 0), (dv_b, GLA_W, 0), (dgo, GLA_W, 0), (dpre_f, qk, 0), (dpre_b, qk, 0), (z_zg, LANES, 0)]
    return _rows(body, t_rows, 384, ins, [wg], [(D_IN_PAD, BF16)],
                 [((LANES, 2 * qk), F32), ((1, 2 * qk), F32)], name)


LRU_RB = 64
HALO = SUBLANES


def _scan8(a, u, rev):
    rows = lax.broadcasted_iota(jnp.int32, a.shape, 0)
    for d in (1, 2, 4):
        if rev:
            a_s, u_s, ok = pltpu.roll(a, SUBLANES - d, 0), pltpu.roll(u, SUBLANES - d, 0), rows < SUBLANES - d
        else:
            a_s, u_s, ok = pltpu.roll(a, d, 0), pltpu.roll(u, d, 0), rows >= d
        u = jnp.where(ok, a * u_s + u, u)
        a = jnp.where(ok, a * a_s, a)
    return a, u


def _edge_row(x, rev):
    r = x[0:1, :] if rev else x[SUBLANES - 1:SUBLANES, :]
    return jnp.broadcast_to(r, x.shape)


def _scan_block(a, u, carry, rev):
    nsub = a.shape[0] // SUBLANES
    loc = [_scan8(a[sb * SUBLANES:(sb + 1) * SUBLANES], u[sb * SUBLANES:(sb + 1) * SUBLANES], rev) for sb in range(nsub)]
    edges = [(_edge_row(a_c, rev), _edge_row(h_l, rev)) for a_c, h_l in loc]
    carries = [None] * nsub
    for sb in (range(nsub - 1, -1, -1) if rev else range(nsub)):
        carries[sb] = carry
        carry = edges[sb][0] * carry + edges[sb][1]
    h = jnp.concatenate([h_l + a_c * cin for (a_c, h_l), cin in zip(loc, carries)], axis=0)
    return h, carry


def _lru_gates(xc, pre_a, pre_x, ba, bx, sp8):
    r = _sigmoid(pre_a + ba)
    i = _sigmoid(pre_x + bx)
    log_a = -(r * sp8)
    a = jnp.exp(log_a)
    y = 2.0 * log_a
    series = -y * (1.0 + y * (0.5 + y * (1.0 / 6.0 + y * (1.0 / 24.0 + y * (1.0 / 120.0)))))
    om = jnp.where(y > -0.25, series, 1.0 - a * a)
    s = jnp.sqrt(om)
    return r, i, a, s


def _softplus(x):
    return jnp.maximum(x, 0.0) + jnp.log(1.0 + jnp.exp(-jnp.abs(x)))


def _conv_taps(xpad_ref, r0, nrows, shifts=(-2, -1, 0, 1)):
    ext = xpad_ref[pl.ds(r0, nrows + 2 * HALO), :]
    taps = []
    for s in shifts:
        sh = ext if s == 0 else pltpu.roll(ext, (-s) % (nrows + 2 * HALO), 0)
        taps.append(sh[HALO:HALO + nrows])
    return taps


def _row_mask(r0, nrows, seq):
    rows = r0 + lax.broadcasted_iota(jnp.int32, (nrows, LANES), 0)
    return jnp.logical_and(rows >= FRONT - N_META, rows < FRONT + seq).astype(F32)


def _lru_load_conv(z_hbm, xpad, xc, sem, cw_ref, cb_ref, t_rows, seq):
    j = pl.program_id(0)
    zeros = jnp.zeros((HALO, LANES), F32)
    xpad[0:HALO, :] = zeros
    xpad[t_rows + HALO:t_rows + 2 * HALO, :] = zeros
    cp = pltpu.make_async_copy(z_hbm.at[:, pl.ds(pl.multiple_of(j * LANES, LANES), LANES)],
                               xpad.at[pl.ds(HALO, t_rows)], sem)
    cp.start()
    cp.wait()
    cw = cw_ref[...]
    cb = cb_ref[...]

    def conv_blk(r, carry):
        r0 = pl.multiple_of(r * LRU_RB, LRU_RB)
        taps = _conv_taps(xpad, r0, LRU_RB)
        acc = cb + cw[0:1, :] * taps[0]
        for k in range(1, 4):
            acc = acc + cw[k:k + 1, :] * taps[k]
        xc[pl.ds(r0, LRU_RB), :] = acc * _row_mask(r0, LRU_RB, seq)
        return carry

    lax.fori_loop(0, t_rows // LRU_RB, conv_blk, 0)


def _lru_specs(t_rows):
    return [pl.BlockSpec(memory_space=pl.ANY),
            pl.BlockSpec((4, LANES), lambda j: (0, j)),
            pl.BlockSpec((1, LANES), lambda j: (0, j)),
            pl.BlockSpec((1, LANES, 4 * LANES), lambda j: (j, 0, 0)),
            pl.BlockSpec((SUBLANES, LANES), lambda j: (0, j))]


def _lru_fwd(z_lru, conv_w, conv_b, wg, pb, t_rows, seq, name, comm=None):
    nblk = t_rows // LRU_RB
    nsub = LRU_RB // SUBLANES

    def kern(z_hbm, cw_ref, cb_ref, wg_ref, pb_ref, hf_ref, hb_ref, xpad, xc, sem):
        _lru_load_conv(z_hbm, xpad, xc, sem, cw_ref, cb_ref, t_rows, seq)
        w = wg_ref[0]
        pb_v = pb_ref[...]
        dirs = []
        for rev in (False, True):
            o = 2 if rev else 0
            dirs.append(dict(rev=rev, ba=pb_v[o:o + 1, :], bx=pb_v[o + 1:o + 2, :],
                             sp8=LRU_C * _softplus(-pb_v[5:6, :] if rev else -pb_v[4:5, :]),
                             w=w[:, o * LANES:(o + 2) * LANES], out=hb_ref if rev else hf_ref))

        def blk(it, carries):
            new = []
            for d, carry in zip(dirs, carries):
                r = (nblk - 1 - it) if d["rev"] else it
                r0 = pl.multiple_of(r * LRU_RB, LRU_RB)
                xcb = xc[pl.ds(r0, LRU_RB), :]
                pre = _dot(xcb.astype(BF16), d["w"])
                _, gi, a, s = _lru_gates(xcb, pre[:, 0:LANES], pre[:, LANES:2 * LANES], d["ba"], d["bx"], d["sp8"])
                h, carry = _scan_block(a, s * (gi * xcb), carry, d["rev"])
                d["out"][pl.ds(r0, LRU_RB), :] = h
                new.append(carry)
            return tuple(new)

        z8 = jnp.zeros((SUBLANES, LANES), F32)
        lax.fori_loop(0, nblk, blk, (z8, z8), unroll=2)

    res, comm_res = _pcall(
        kern, name, (LRU_W // LANES,), _lru_specs(t_rows), [pl.BlockSpec((t_rows, LANES), lambda j: (0, j))] * 2,
        [jax.ShapeDtypeStruct((t_rows, LRU_W), F32)] * 2,
        [pltpu.VMEM((t_rows + 2 * HALO, LANES), F32), pltpu.VMEM((t_rows, LANES), F32), pltpu.SemaphoreType.DMA],
        ("arbitrary",), (z_lru, conv_w, conv_b, wg, pb), comm)
    return res if comm is None else (res, comm_res)


def _lru_bwd(z_lru, conv_w, conv_b, wg, pb, h_f, h_b, dh, t_rows, seq, name):
    nblk = t_rows // LRU_RB
    nsub = LRU_RB // SUBLANES

    def kern(z_hbm, cw_ref, cb_ref, wg_ref, pb_ref, hf_ref, hb_ref, dh_ref,
             dx_ref, dcw_ref, dcb_ref, dwg_ref, dpb_ref, xpad, xc, dxc, sem):
        _lru_load_conv(z_hbm, xpad, xc, sem, cw_ref, cb_ref, t_rows, seq)
        w = wg_ref[0]
        pb_v = pb_ref[...]
        zeros = jnp.zeros((HALO, LANES), F32)
        dxc[0:HALO, :] = zeros
        dxc[t_rows + HALO:t_rows + 2 * HALO, :] = zeros
        dwg_ref[...] = jnp.zeros_like(dwg_ref)

        def zero_blk(r, carry):
            dxc[pl.ds(pl.multiple_of(r * LRU_RB, LRU_RB) + HALO, LRU_RB), :] = jnp.zeros((LRU_RB, LANES), F32)
            return carry

        lax.fori_loop(0, nblk, zero_blk, 0)
        dirs = []
        for rev in (False, True):
            o = 2 if rev else 0
            lam = pb_v[5:6, :] if rev else pb_v[4:5, :]
            dirs.append(dict(rev=rev, o=o, ba=pb_v[o:o + 1, :], bx=pb_v[o + 1:o + 2, :], sp8=LRU_C * _softplus(-lam),
                             dlam_scale=LRU_C * _sigmoid(-lam), h_ref=hb_ref if rev else hf_ref,
                             w=w[:, o * LANES:(o + 2) * LANES]))
        rows_b = lax.broadcasted_iota(jnp.int32, (LRU_RB, LANES), 0)

        def blk(it, carries):
            new = []
            for d, (nu_c, acc_ba, acc_bx, acc_lam) in zip(dirs, carries):
                rev, o, h_ref = d["rev"], d["o"], d["h_ref"]
                adj_rev = not rev
                r = (nblk - 1 - it) if adj_rev else it
                r0 = pl.multiple_of(r * LRU_RB, LRU_RB)
                xcb = xc[pl.ds(r0, LRU_RB), :]
                xc16 = xcb.astype(BF16)
                pre = _dot(xc16, d["w"])
                gr, gi, a, s = _lru_gates(xcb, pre[:, 0:LANES], pre[:, LANES:2 * LANES], d["ba"], d["bx"], d["sp8"])
                dhb = dh_ref[pl.ds(r0, LRU_RB), :]
                hb = h_ref[pl.ds(r0, LRU_RB), :]
                if rev:
                    nxt0 = pl.multiple_of(jnp.minimum(r0 + LRU_RB, t_rows - SUBLANES), SUBLANES)
                    edge = h_ref[pl.ds(nxt0, SUBLANES), :][0:1, :] * (r < nblk - 1).astype(F32)
                    h_prev = jnp.concatenate([hb[1:], edge], axis=0)
                else:
                    prv0 = pl.multiple_of(jnp.maximum(r0 - SUBLANES, 0), SUBLANES)
                    edge = h_ref[pl.ds(prv0, SUBLANES), :][SUBLANES - 1:SUBLANES, :] * (r > 0).astype(F32)
                    h_prev = jnp.concatenate([edge, hb[:-1]], axis=0)
                nu, nu_out = _scan_block(a, a * dhb, nu_c, adj_rev)
                cin = jnp.broadcast_to(nu_c[0:1, :], (LRU_RB, LANES))
                if adj_rev:
                    nu_next = jnp.where(rows_b == LRU_RB - 1, cin, pltpu.roll(nu, LRU_RB - 1, 0))
                else:
                    nu_next = jnp.where(rows_b == 0, cin, pltpu.roll(nu, 1, 0))
                du = dhb + nu_next
                da = du * h_prev
                ix = gi * xcb
                dlog = da * a - du * ix * (a * a) / s
                d_i = du * s * xcb
                dr = -dlog * d["sp8"]
                dpa = dr * gr * (1.0 - gr)
                dpx = d_i * gi * (1.0 - gi)
                dp16 = jnp.concatenate([dpa, dpx], axis=1).astype(BF16)
                dxc[pl.ds(r0 + HALO, LRU_RB), :] += du * s * gi + _dot(dp16, d["w"], NT)
                dwg_ref[0, :, o * LANES:(o + 2) * LANES] += _dot(xc16, dp16, TN)
                new.append((nu_out, acc_ba + jnp.sum(dpa, axis=0, keepdims=True),
                            acc_bx + jnp.sum(dpx, axis=0, keepdims=True),
                            acc_lam + jnp.sum(dlog * gr, axis=0, keepdims=True)))
            return tuple(new)

        z1 = jnp.zeros((1, LANES), F32)
        init = (jnp.zeros((SUBLANES, LANES), F32), z1, z1, z1)
        (_, ba_f, bx_f, lam_f), (_, ba_b, bx_b, lam_b) = lax.fori_loop(0, nblk, blk, (init, init), unroll=2)
        dpb_ref[...] = jnp.concatenate([ba_f, bx_f, ba_b, bx_b, lam_f * dirs[0]["dlam_scale"],
                                        lam_b * dirs[1]["dlam_scale"], z1, z1], axis=0)

        def mask_blk(r, carry):
            r0 = pl.multiple_of(r * LRU_RB, LRU_RB)
            dxc[pl.ds(r0 + HALO, LRU_RB), :] = dxc[pl.ds(r0 + HALO, LRU_RB), :] * _row_mask(r0, LRU_RB, seq)
            return carry

        lax.fori_loop(0, nblk, mask_blk, 0)

        cw = cw_ref[...]

        def conv_bwd(r, carry):
            r0 = pl.multiple_of(r * LRU_RB, LRU_RB)
            gt = _conv_taps(dxc, r0, LRU_RB, (2, 1, 0, -1))
            xt = _conv_taps(xpad, r0, LRU_RB)
            dx_ref[pl.ds(r0, LRU_RB), :] = (cw[0:1, :] * gt[0] + cw[1:2, :] * gt[1] + cw[2:3, :] * gt[2]
                                           + cw[3:4, :] * gt[3])
            g = gt[2]
            new = [carry[k] + jnp.sum(g * xt[k], axis=0, keepdims=True) for k in range(4)]
            new.append(carry[4] + jnp.sum(g, axis=0, keepdims=True))
            return tuple(new)

        z1 = jnp.zeros((1, LANES), F32)
        sums = lax.fori_loop(0, nblk, conv_bwd, (z1, z1, z1, z1, z1))
        dcw_ref[...] = jnp.concatenate(sums[:4], axis=0)
        dcb_ref[...] = sums[4]

    col = lambda j: (0, j)
    return pl.pallas_call(
        kern, name=name, grid=(LRU_W // LANES,),
        in_specs=_lru_specs(t_rows) + [pl.BlockSpec((t_rows, LANES), col)] * 3,
        out_specs=[pl.BlockSpec((t_rows, LANES), col), pl.BlockSpec((4, LANES), col), pl.BlockSpec((1, LANES), col),
                   pl.BlockSpec((1, LANES, 4 * LANES), lambda j: (j, 0, 0)), pl.BlockSpec((SUBLANES, LANES), col)],
        out_shape=[jax.ShapeDtypeStruct((t_rows, LRU_W), F32), jax.ShapeDtypeStruct((4, LRU_W), F32),
                   jax.ShapeDtypeStruct((1, LRU_W), F32), jax.ShapeDtypeStruct((4, LANES, 4 * LANES), F32),
                   jax.ShapeDtypeStruct((SUBLANES, LRU_W), F32)],
        scratch_shapes=[pltpu.VMEM((t_rows + 2 * HALO, LANES), F32), pltpu.VMEM((t_rows, LANES), F32),
                        pltpu.VMEM((t_rows + 2 * HALO, LANES), F32), pltpu.SemaphoreType.DMA],
        compiler_params=_cparams(("arbitrary",)),
    )(z_lru, conv_w, conv_b, wg, pb, h_f, h_b, dh)


QK = GLA_H * GLA_DK


def _cumsum_rows(x, rev):
    n = x.shape[0]
    rows = lax.broadcasted_iota(jnp.int32, x.shape, 0)
    d = 1
    while d < n:
        if rev:
            x = x + jnp.where(rows < n - d, pltpu.roll(x, n - d, 0), 0.0)
        else:
            x = x + jnp.where(rows >= d, pltpu.roll(x, d, 0), 0.0)
        d *= 2
    return x


def _gla_chunk_terms(q, k, zg, wg, bg, rev):
    pre = (_dot(zg.astype(BF16), wg) + bg)[:, (QK if rev else 0):(2 * QK if rev else QK)]
    g = (jnp.minimum(pre, 0.0) - jnp.log(1.0 + jnp.exp(-jnp.abs(pre)))) * (1.0 / GLA_GATE_NORM)
    b = _cumsum_rows(g, rev)
    b_last = b[0:1, :] if rev else b[CHUNK - 1:CHUNK, :]
    b_mid = b[CHUNK // 2:CHUNK // 2 + 1, :]
    qs = q * (GLA_DK ** -0.5)
    terms = dict(
        pre=pre, qs=qs, k=k,
        eb=jnp.exp(b), ek=jnp.exp(b_last - b), e_last=jnp.exp(b_last),
        eqm=jnp.exp(jnp.minimum(b - b_mid, EXP_CLAMP)), ekm=jnp.exp(jnp.minimum(b_mid - b, EXP_CLAMP)))
    return terms


def _gla_mask(rev):
    t = lax.broadcasted_iota(jnp.int32, (CHUNK, CHUNK), 0)
    s = lax.broadcasted_iota(jnp.int32, (CHUNK, CHUNK), 1)
    return (s > t) if rev else (s <= t)


def _gla_head_ops(tm, hd):
    sl = slice(hd * GLA_DK, (hd + 1) * GLA_DK)
    q_b = (tm["qs"][:, sl] * tm["eb"][:, sl]).astype(BF16)
    k_e = (tm["k"][:, sl] * tm["ek"][:, sl]).astype(BF16)
    q_m = (tm["qs"][:, sl] * tm["eqm"][:, sl]).astype(BF16)
    k_m = (tm["k"][:, sl] * tm["ekm"][:, sl]).astype(BF16)
    return sl, q_b, k_e, q_m, k_m


def _gla_fwd(z_qk, z_v, z_zg, wg, bg, t_rows, name, comm=None):
    nc = t_rows // CHUNK

    def kern(qf, kf, vf, zf, qb, kb, vb, zb, wg_ref, bg_ref, of_ref, ob_ref, sf_ref, sb_ref, st_f, st_b):
        i = pl.program_id(0)

        @pl.when(i == 0)
        def _():
            st_f[...] = jnp.zeros_like(st_f)
            st_b[...] = jnp.zeros_like(st_b)

        wg_v, bg_v = wg_ref[...], bg_ref[...]
        for rev, (q_r, k_r, v_r, z_r, o_r, s_r, st) in ((False, (qf, kf, vf, zf, of_ref, sf_ref, st_f)),
                                                        (True, (qb, kb, vb, zb, ob_ref, sb_ref, st_b))):
            tm = _gla_chunk_terms(q_r[...], k_r[...], z_r[...], wg_v, bg_v, rev)
            mask = _gla_mask(rev)
            v = v_r[...]
            for hd in range(GLA_H):
                sl, q_b, k_e, q_m, k_m = _gla_head_ops(tm, hd)
                vs = slice(hd * GLA_DV, (hd + 1) * GLA_DV)
                v16 = v[:, vs].astype(BF16)
                a = jnp.where(mask, _dot(q_m, k_m, NT), 0.0).astype(BF16)
                s_in = st[hd]
                s_r[0, hd] = s_in
                o_r[:, vs] = _dot(a, v16) + _dot(q_b, s_in.astype(BF16), NT)
                st[hd] = s_in * tm["e_last"][:, sl] + _dot(v16, k_e, TN)

    fw = lambda c: (lambda i: (i, c))
    rv = lambda c: (lambda i: (nc - 1 - i, c))
    def side(ix):
        return [pl.BlockSpec((CHUNK, QK), ix(0)), pl.BlockSpec((CHUNK, QK), ix(1)),
                pl.BlockSpec((CHUNK, GLA_W), ix(0)), pl.BlockSpec((CHUNK, LANES), ix(0))]
    st_shape = (nc, GLA_H, GLA_DV, GLA_DK)
    res, comm_res = _pcall(
        kern, name, (nc,),
        side(fw) + side(rv) + [pl.BlockSpec(wg.shape, lambda i: (0, 0)), pl.BlockSpec(bg.shape, lambda i: (0, 0))],
        [pl.BlockSpec((CHUNK, GLA_W), fw(0)), pl.BlockSpec((CHUNK, GLA_W), rv(0)),
         pl.BlockSpec((1,) + st_shape[1:], lambda i: (i, 0, 0, 0)),
         pl.BlockSpec((1,) + st_shape[1:], lambda i: (nc - 1 - i, 0, 0, 0))],
        [jax.ShapeDtypeStruct((t_rows, GLA_W), F32)] * 2 + [jax.ShapeDtypeStruct(st_shape, F32)] * 2,
        [pltpu.VMEM(st_shape[1:], F32)] * 2, ("arbitrary",),
        (z_qk, z_qk, z_v, z_zg, z_qk, z_qk, z_v, z_zg, wg, bg), comm)
    return res if comm is None else (res, comm_res)


def _gla_bwd(z_qk, z_v, z_zg, wg, bg, do, s_f, s_b, t_rows, name, comm=None):
    nc = t_rows // CHUNK

    def kern(qf, kf, vf, zf, dof, ssf, qb, kb, vb, zb, dob, ssb, wg_ref, bg_ref,
             dqf, dkf, dvf, dpf, dqb, dkb, dvb, dpb, ds_f, ds_b):
        i = pl.program_id(0)

        @pl.when(i == 0)
        def _():
            ds_f[...] = jnp.zeros_like(ds_f)
            ds_b[...] = jnp.zeros_like(ds_b)

        wg_v, bg_v = wg_ref[...], bg_ref[...]
        sides = ((False, (qf, kf, vf, zf, dof, ssf, dqf, dkf, dvf, dpf, ds_f)),
                 (True, (qb, kb, vb, zb, dob, ssb, dqb, dkb, dvb, dpb, ds_b)))
        for rev, (q_r, k_r, v_r, z_r, do_r, ss_r, dq_r, dk_r, dv_r, dp_r, ds) in sides:
            tm = _gla_chunk_terms(q_r[...], k_r[...], z_r[...], wg_v, bg_v, rev)
            mask = _gla_mask(rev)
            v = v_r[...]
            d_o = do_r[...]
            db_parts, cross_parts = [], []
            for hd in range(GLA_H):
                sl, q_b, k_e, q_m, k_m = _gla_head_ops(tm, hd)
                vs = slice(hd * GLA_DV, (hd + 1) * GLA_DV)
                v16, do16 = v[:, vs].astype(BF16), d_o[:, vs].astype(BF16)
                a = jnp.where(mask, _dot(q_m, k_m, NT), 0.0).astype(BF16)
                da = jnp.where(mask, _dot(do16, v16, NT), 0.0).astype(BF16)
                s_in = ss_r[0, hd]
                s_in16 = s_in.astype(BF16)
                ds_out = ds[hd]
                ds16 = ds_out.astype(BF16)
                dv_r[:, vs] = _dot(a, do16, TN) + _dot(k_e, ds16, NT)
                dqs = _dot(da, k_m) * tm["eqm"][:, sl] + _dot(do16, s_in16) * tm["eb"][:, sl]
                dk = _dot(da, q_m, TN) * tm["ekm"][:, sl] + _dot(v16, ds16) * tm["ek"][:, sl]
                ds[hd] = ds_out * tm["e_last"][:, sl] + _dot(do16, q_b, TN)
                dq_r[:, sl] = dqs
                dk_r[:, sl] = dk
                db_parts.append(tm["qs"][:, sl] * dqs - tm["k"][:, sl] * dk)
                s_out = s_in * tm["e_last"][:, sl] + _dot(v16, k_e, TN)
                cross_parts.append(jnp.sum(s_out * ds_out, axis=0, keepdims=True))
            d_b = jnp.concatenate(db_parts, axis=1)
            dg = _cumsum_rows(d_b, not rev) + jnp.concatenate(cross_parts, axis=1)
            dp_r[...] = dg * (1.0 / GLA_GATE_NORM) * _sigmoid(-tm["pre"])

    fw = lambda c: (lambda i: (nc - 1 - i, c))
    rv = lambda c: (lambda i: (i, c))
    st_blk = (1, GLA_H, GLA_DV, GLA_DK)
    def side(ix, st_ix):
        return [pl.BlockSpec((CHUNK, QK), ix(0)), pl.BlockSpec((CHUNK, QK), ix(1)),
                pl.BlockSpec((CHUNK, GLA_W), ix(0)), pl.BlockSpec((CHUNK, LANES), ix(0)),
                pl.BlockSpec((CHUNK, GLA_W), ix(0)), pl.BlockSpec(st_blk, st_ix)]
    def oside(ix):
        return [pl.BlockSpec((CHUNK, QK), ix(0)), pl.BlockSpec((CHUNK, QK), ix(0)),
                pl.BlockSpec((CHUNK, GLA_W), ix(0)), pl.BlockSpec((CHUNK, QK), ix(0))]
    o_shapes = [jax.ShapeDtypeStruct((t_rows, QK), F32), jax.ShapeDtypeStruct((t_rows, QK), F32),
                jax.ShapeDtypeStruct((t_rows, GLA_W), F32), jax.ShapeDtypeStruct((t_rows, QK), F32)]
    res, comm_res = _pcall(
        kern, name, (nc,),
        (side(fw, lambda i: (nc - 1 - i, 0, 0, 0)) + side(rv, lambda i: (i, 0, 0, 0))
         + [pl.BlockSpec(wg.shape, lambda i: (0, 0)), pl.BlockSpec(bg.shape, lambda i: (0, 0))]),
        oside(fw) + oside(rv), o_shapes * 2, [pltpu.VMEM((GLA_H, GLA_DV, GLA_DK), F32)] * 2, ("arbitrary",),
        (z_qk, z_qk, z_v, z_zg, do, s_f, z_qk, z_qk, z_v, z_zg, do, s_b, wg, bg), comm)
    return res if comm is None else (res, comm_res)


FLAT_W = 1024


def _adam_math(wv, gv, mv, vv):
    bc1 = 1.0 - ADAM_B1 ** ADAM_STEP
    bc2 = 1.0 - ADAM_B2 ** ADAM_STEP
    m_new = ADAM_B1 * mv + (1.0 - ADAM_B1) * gv
    v_new = ADAM_B2 * vv + (1.0 - ADAM_B2) * (gv * gv)
    delta = -ADAM_LR * ((m_new / bc1) / (jnp.sqrt(v_new / bc2) + ADAM_EPS) + ADAM_WD * wv)
    return delta, m_new, v_new


def _row_tile(rows, cap=256):
    t = cap
    while rows % t:
        t //= 2
    assert t >= SUBLANES, rows
    return t


def _adamw_layers(w, m, v, g_layers, name):
    depth, rows, cols = w.shape
    tr = _row_tile(rows)

    def kern(w_ref, m_ref, v_ref, *rest):
        g_refs, (go_ref, d_ref, mo_ref, vo_ref) = rest[:depth], rest[depth:]
        l = pl.program_id(0)
        gv = g_refs[0][...]
        for k in range(1, depth):
            gv = jnp.where(l == k, g_refs[k][...], gv)
        delta, m_new, v_new = _adam_math(w_ref[...], gv, m_ref[...], v_ref[...])
        go_ref[...] = gv
        d_ref[...] = delta
        mo_ref[...] = m_new
        vo_ref[...] = v_new

    stacked = pl.BlockSpec((None, tr, cols), lambda l, i: (l, i, 0))
    return pl.pallas_call(
        kern, name=name, grid=(depth, rows // tr),
        in_specs=[stacked] * 3 + [pl.BlockSpec((tr, cols), lambda l, i: (i, 0))] * depth,
        out_specs=[stacked] * 4, out_shape=[jax.ShapeDtypeStruct(w.shape, F32)] * 4,
        compiler_params=_cparams(("arbitrary", "arbitrary")),
    )(w, m, v, *g_layers)


def _adamw_flat(w, g, m, v):
    rows = w.shape[0]

    def body(i, ins, consts, outs, accs):
        delta, m_new, v_new = _adam_math(*[r[...] for r in ins])
        outs[0][...] = delta
        outs[1][...] = m_new
        outs[2][...] = v_new

    w_, g_, m_, v_ = w, g, m, v
    return _rows(body, rows, _row_tile(rows), [(w_, FLAT_W, 0), (g_, FLAT_W, 0), (m_, FLAT_W, 0), (v_, FLAT_W, 0)],
                 [], [(FLAT_W, F32)] * 3, [], "adamw_small")


def _add_sibling(g, got, c_idx, name):
    _, _, rows, cols = g.shape
    tr = _row_tile(rows)

    def kern(c_ref, g_ref, got_ref, o_ref):
        o_ref[...] = (g_ref[...] + got_ref[...]).astype(BF16)

    grid_spec = pltpu.PrefetchScalarGridSpec(
        num_scalar_prefetch=1, grid=(N_CHIPS, rows // tr),
        in_specs=[pl.BlockSpec((None, None, tr, cols), lambda q, i, c_ref: (q, c_ref[0], i, 0)),
                  pl.BlockSpec((None, tr, cols), lambda q, i, c_ref: (q, i, 0))],
        out_specs=pl.BlockSpec((None, tr, cols), lambda q, i, c_ref: (q, i, 0)))
    return pl.pallas_call(
        kern, name=name, grid_spec=grid_spec, out_shape=jax.ShapeDtypeStruct((N_CHIPS, rows, cols), BF16),
        compiler_params=_cparams(("arbitrary", "arbitrary")),
    )(c_idx, g, got)


def _sum_chips(x, name):
    _, rows, cols = x.shape
    tr = _row_tile(rows)

    def kern(x_ref, o_ref):
        xv = x_ref[...].astype(F32)
        o_ref[...] = ((xv[0] + xv[1]) + xv[2]) + xv[3]

    return pl.pallas_call(
        kern, name=name, grid=(rows // tr,),
        in_specs=[pl.BlockSpec((N_CHIPS, tr, cols), lambda i: (0, i, 0))],
        out_specs=pl.BlockSpec((tr, cols), lambda i: (i, 0)),
        out_shape=jax.ShapeDtypeStruct((rows, cols), F32),
        compiler_params=_cparams(("arbitrary",)),
    )(x)


def _place():
    x, y, c = lax.axis_index("x"), lax.axis_index("y"), lax.axis_index("c")
    return x, y, c


def _other_chips(x, y):
    return [(1 - x, y), (x, 1 - y), (1 - x, 1 - y)]


class _Comm:
    def __init__(self, ins, out_shapes, n_sems, stage, body):
        self.ins, self.out_shapes, self.body = list(ins), list(out_shapes), body
        self.scratch = [pltpu.SemaphoreType.DMA((n,)) for n in n_sems] + [pltpu.VMEM(s, d) for s, d in stage]


ANY_SPEC = pl.BlockSpec(memory_space=pl.ANY)


def _run_comm(comm, name):
    def kern(*refs):
        comm.body(refs, True, True)

    return pl.pallas_call(
        kern, name=name, in_specs=[ANY_SPEC] * len(comm.ins), out_specs=[ANY_SPEC] * len(comm.out_shapes),
        out_shape=comm.out_shapes, scratch_shapes=comm.scratch,
        compiler_params=pltpu.CompilerParams(has_side_effects=True, vmem_limit_bytes=VMEM_LIMIT),
    )(*comm.ins)


def _pcall(kern, name, grid, in_specs, out_specs, out_shape, scratch_shapes, sem, args, comm=None):
    if comm is None:
        return pl.pallas_call(kern, name=name, grid=grid, in_specs=in_specs, out_specs=out_specs, out_shape=out_shape,
                              scratch_shapes=scratch_shapes, compiler_params=_cparams(sem))(*args), None
    n_in, n_out, n_scr = len(in_specs), len(out_specs), len(scratch_shapes)
    c_in, c_out = len(comm.ins), len(comm.out_shapes)

    def hosted(*refs):
        a = n_in + c_in
        b = a + n_out + c_out
        main = refs[:n_in] + refs[a:a + n_out] + refs[b:b + n_scr]
        extra = refs[n_in:a] + refs[a + n_out:b] + refs[b + n_scr:]
        ids = [pl.program_id(d) for d in range(len(grid))]
        first, last = ids[0] == 0, ids[0] == grid[0] - 1
        for i, g in zip(ids[1:], grid[1:]):
            first, last = jnp.logical_and(first, i == 0), jnp.logical_and(last, i == g - 1)

        @pl.when(first)
        def _():
            comm.body(extra, True, False)

        kern(*main)

        @pl.when(last)
        def _():
            comm.body(extra, False, True)

    res = pl.pallas_call(
        hosted, name=name, grid=grid, in_specs=list(in_specs) + [ANY_SPEC] * c_in,
        out_specs=list(out_specs) + [ANY_SPEC] * c_out, out_shape=list(out_shape) + comm.out_shapes,
        scratch_shapes=list(scratch_shapes) + comm.scratch,
        compiler_params=pltpu.CompilerParams(dimension_semantics=("arbitrary",) * len(grid),
                                             vmem_limit_bytes=VMEM_LIMIT, has_side_effects=True),
    )(*args, *comm.ins)
    return res[:n_out], res[n_out:]


class _StagedCopies:
    def __init__(self, pairs, bufs, sem_in, sem_out):
        self.ins = [pltpu.make_async_copy(src, buf, sem_in.at[i]) for i, ((src, _), buf) in enumerate(zip(pairs, bufs))]
        self.outs = [pltpu.make_async_copy(buf, dst, sem_out.at[i]) for i, ((_, dst), buf) in enumerate(zip(pairs, bufs))]

    def load(self):
        for cp in self.ins:
            cp.start()

    def store(self):
        for cp in self.ins:
            cp.wait()
        for cp in self.outs:
            cp.start()

    def finish(self):
        for cp in self.outs:
            cp.wait()


DMA_CHUNK_BYTES = 1 << 20
DMA_MAX_CHUNKS = 4


def _row_chunks(rows, row_bytes, align=16):
    units = rows // align
    k = max(1, min(DMA_MAX_CHUNKS, -(-rows * row_bytes // DMA_CHUNK_BYTES), units))
    out, start = [], 0
    for i in range(k):
        size = (units // k + (1 if i < units % k else 0)) * align
        out.append((start, size))
        start += size
    if start < rows:
        out[-1] = (out[-1][0], out[-1][1] + rows - start)
    return out


def _row_bytes(shape, dtype):
    return math.prod(shape[1:]) * jnp.dtype(dtype).itemsize


def _remote(src, dst, send, recv, idx, dev):
    return pltpu.make_async_remote_copy(src, dst, send.at[idx], recv.at[idx], device_id=dev, device_id_type=MESH_ID)


def _allgather_chips(xs):
    na = len(xs)
    chunks = [_row_chunks(x.shape[1], _row_bytes(x.shape[1:], x.dtype)) for x in xs]
    n_cp = 3 * sum(len(ch) for ch in chunks)

    def body(refs, start, finish):
        x_refs, o_refs = refs[:na], refs[na:2 * na]
        send1, recv1, send2, recv2, loc_in, loc_out = refs[2 * na:2 * na + 6]
        px, py, c = _place()
        me = 2 * px + py
        sib = (px, py, 1 - c)
        own = _StagedCopies([(x, o.at[me]) for x, o in zip(x_refs, o_refs)], refs[2 * na + 6:], loc_in, loc_out)
        plan = [(a, qx, qy, pl.ds(s, n)) for a in range(na) for (qx, qy) in _other_chips(px, py) for (s, n) in chunks[a]]
        first = [_remote(x_refs[a].at[c, rows], o_refs[a].at[me, c, rows], send1, recv1, i, (qx, qy, c))
                 for i, (a, qx, qy, rows) in enumerate(plan)]
        if start:
            own.load()
            for cp in first:
                cp.start()
        if finish:
            own.store()
            passed = []
            for i, (a, qx, qy, rows) in enumerate(plan):
                slot = o_refs[a].at[2 * qx + qy, c, rows]
                _remote(slot, slot, send1, recv1, i, (qx, qy, c)).wait_recv()
                fwd = _remote(slot, slot, send2, recv2, i, sib)
                fwd.start()
                passed.append(fwd)
            for i, (a, qx, qy, rows) in enumerate(plan):
                slot = o_refs[a].at[2 * qx + qy, 1 - c, rows]
                _remote(slot, slot, send2, recv2, i, sib).wait_recv()
            for cp in first + passed:
                cp.wait_send()
            own.finish()

    outs = [jax.ShapeDtypeStruct((N_CHIPS,) + x.shape, x.dtype) for x in xs]
    return _Comm(xs, outs, [n_cp, n_cp, n_cp, n_cp, na, na], [(x.shape, x.dtype) for x in xs], body)


def _sibling_halves(gs):
    na = len(gs)
    chunks = [_row_chunks(g.shape[2], _row_bytes(g.shape[2:], g.dtype)) for g in gs]
    n_cp = N_CHIPS * sum(len(ch) for ch in chunks)

    def body(refs, start, finish):
        g_refs, o_refs = refs[:na], refs[na:2 * na]
        send, recv = refs[2 * na:]
        px, py, c = _place()
        plan = [(a, q, pl.ds(s, n)) for a in range(na) for q in range(N_CHIPS) for (s, n) in chunks[a]]
        cps = [_remote(g_refs[a].at[q, 1 - c, rows], o_refs[a].at[q, rows], send, recv, i, (px, py, 1 - c))
               for i, (a, q, rows) in enumerate(plan)]
        if start:
            for cp in cps:
                cp.start()
        if finish:
            for cp in cps:
                cp.wait()

    outs = [jax.ShapeDtypeStruct((N_CHIPS,) + g.shape[2:], g.dtype) for g in gs]
    return _Comm(gs, outs, [n_cp, n_cp], [], body)


def _chip_exchange(ps):
    na = len(ps)
    chunks = [_row_chunks(p.shape[1], _row_bytes(p.shape[1:], p.dtype)) for p in ps]
    n_cp = 3 * sum(len(ch) for ch in chunks)

    def body(refs, start, finish):
        p_refs, o_refs = refs[:na], refs[na:2 * na]
        send, recv, loc_in, loc_out = refs[2 * na:2 * na + 4]
        px, py, c = _place()
        me = 2 * px + py
        own = _StagedCopies([(p.at[me], o.at[me]) for p, o in zip(p_refs, o_refs)], refs[2 * na + 4:], loc_in, loc_out)
        plan = [(a, qx, qy, pl.ds(s, n)) for a in range(na) for (qx, qy) in _other_chips(px, py) for (s, n) in chunks[a]]
        cps = [_remote(p_refs[a].at[2 * qx + qy, rows], o_refs[a].at[me, rows], send, recv, i, (qx, qy, c))
               for i, (a, qx, qy, rows) in enumerate(plan)]
        if start:
            own.load()
            for cp in cps:
                cp.start()
        if finish:
            own.store()
            for cp in cps:
                cp.wait()
            own.finish()

    outs = [jax.ShapeDtypeStruct(p.shape, p.dtype) for p in ps]
    return _Comm(ps, outs, [n_cp, n_cp, na, na], [(p.shape[1:], p.dtype) for p in ps], body)


def _sibling_join(rs):
    na = len(rs)
    chunks = [_row_chunks(r.shape[0], _row_bytes(r.shape, r.dtype)) for r in rs]
    n_cp = sum(len(ch) for ch in chunks)

    def body(refs, start, finish):
        r_refs, o_refs = refs[:na], refs[na:2 * na]
        send, recv, loc_in, loc_out = refs[2 * na:2 * na + 4]
        px, py, c = _place()
        own = _StagedCopies([(r, o.at[c]) for r, o in zip(r_refs, o_refs)], refs[2 * na + 4:], loc_in, loc_out)
        plan = [(a, pl.ds(s, n)) for a in range(na) for (s, n) in chunks[a]]
        cps = [_remote(r_refs[a].at[rows], o_refs[a].at[c, rows], send, recv, i, (px, py, 1 - c))
               for i, (a, rows) in enumerate(plan)]
        if start:
            own.load()
            for cp in cps:
                cp.start()
        if finish:
            own.store()
            for cp in cps:
                cp.wait()
            own.finish()

    outs = [jax.ShapeDtypeStruct((2,) + r.shape, r.dtype) for r in rs]
    return _Comm(rs, outs, [n_cp, n_cp, na, na], [(r.shape, r.dtype) for r in rs], body)


def _reduce_stages(gs, c_idx, tag):
    got = yield _sibling_halves(gs)
    part = [_add_sibling(g, o, c_idx, f"reduce_add_sibling{tag}_{a}") for a, (g, o) in enumerate(zip(gs, got))]
    landed = yield _chip_exchange(part)
    mine = [_sum_chips(x, f"reduce_sum_chips{tag}_{a}") for a, x in enumerate(landed)]
    return (yield _sibling_join(mine))


class _Staged:
    def __init__(self, gen, tag):
        self.gen, self.tag, self.k, self.value = gen, tag, 0, None
        self.cur = next(gen)

    def stage(self):
        return self.cur

    def done(self, results):
        self.k += 1
        try:
            self.cur = self.gen.send(results)
        except StopIteration as stop:
            self.cur, self.value = None, stop.value

    def drain(self):
        while self.cur is not None:
            self.done(_run_comm(self.cur, f"comm{self.tag}_{self.k}"))
        return self.value


WEIGHTS = ["meta_tokens", "norm_mix_pre", "norm_mix_post", "norm_mlp_pre", "norm_mlp_post", "w_in", "conv_w",
           "conv_b", "lru_wa_f", "lru_ba_f", "lru_wx_f", "lru_bx_f", "lru_lambda_f", "lru_wa_b", "lru_ba_b",
           "lru_wx_b", "lru_bx_b", "lru_lambda_b", "gla_wg_f", "gla_bg_f", "gla_wg_b", "gla_bg_b", "gla_head_norm",
           "w_out", "w_mlp_up", "w_mlp_down"]
BIG = ("w_in", "w_out", "w_mlp_up", "w_mlp_down")
SMALL = [n for n in WEIGHTS if n not in BIG]
SMALL_SHARDED = {"meta_tokens": 1, "conv_w": 2, "gla_wg_f": 2, "gla_wg_b": 2}
N_CHIPS = 4
SMALL_RAW = [("g1", (1, D_MODEL)), ("g2", (1, D_MODEL)), ("g3", (1, D_MODEL)), ("g4", (1, D_MODEL)),
             ("conv_w", (4, LRU_W)), ("conv_b", (1, LRU_W)), ("lru_wg", (4, LANES, 4 * LANES)),
             ("lru_pb", (SUBLANES, LRU_W)), ("gla_wg", (2 * GLA_RANK, 2 * QK)), ("gla_bg", (1, 2 * QK)),
             ("head_norm", (1, GLA_W))]
META_SHAPE = (N_META, D_MODEL)


def _pad_to(v, n):
    return jnp.pad(v, (0, n - v.shape[0]))


def _round_up(n, m):
    return -(-n // m) * m


def _flat_cat(arrs, total):
    return _pad_to(jnp.concatenate([a.reshape(-1) for a in arrs]), total)


def _split_flat(flat, shapes):
    out, off = [], 0
    for s in shapes:
        n = math.prod(s)
        out.append(flat[off:off + n].reshape(s))
        off += n
    return out


def _my_shard(full, axis, chip):
    n = full.shape[axis] // N_CHIPS
    return lax.dynamic_slice_in_dim(full, chip * n, n, axis=axis)


def _block_diag_gates(wa_f, wx_f, wa_b, wx_b):
    def bd(w):
        z = jnp.zeros((LRU_HD, LRU_HD), w.dtype)
        return jnp.stack([jnp.block([[w[2 * j], z], [z, w[2 * j + 1]]]) for j in range(4)])
    return jnp.concatenate([bd(wa_f), bd(wx_f), bd(wa_b), bd(wx_b)], axis=2).astype(BF16)


def _gate_diag_blocks(dwg, o):
    blk = dwg[:, :, o * LANES:(o + 1) * LANES]
    return jnp.stack([blk[j, hh * LRU_HD:(hh + 1) * LRU_HD, hh * LRU_HD:(hh + 1) * LRU_HD]
                      for j in range(4) for hh in range(2)])


def kernel(x, meta_tokens, norm_mix_pre, norm_mix_post, norm_mlp_pre, norm_mlp_post, w_in, conv_w, conv_b, lru_wa_f, lru_ba_f, lru_wx_f, lru_bx_f, lru_lambda_f, lru_wa_b, lru_ba_b, lru_wx_b, lru_bx_b, lru_lambda_b, gla_wg_f, gla_bg_f, gla_wg_b, gla_bg_b, gla_head_norm, w_out, w_mlp_up, w_mlp_down, loss_target, m_meta_tokens, m_norm_mix_pre, m_norm_mix_post, m_norm_mlp_pre, m_norm_mlp_post, m_w_in, m_conv_w, m_conv_b, m_lru_wa_f, m_lru_ba_f, m_lru_wx_f, m_lru_bx_f, m_lru_lambda_f, m_lru_wa_b, m_lru_ba_b, m_lru_wx_b, m_lru_bx_b, m_lru_lambda_b, m_gla_wg_f, m_gla_bg_f, m_gla_wg_b, m_gla_bg_b, m_gla_head_norm, m_w_out, m_w_mlp_up, m_w_mlp_down, v_meta_tokens, v_norm_mix_pre, v_norm_mix_post, v_norm_mlp_pre, v_norm_mlp_post, v_w_in, v_conv_w, v_conv_b, v_lru_wa_f, v_lru_ba_f, v_lru_wx_f, v_lru_bx_f, v_lru_lambda_f, v_lru_wa_b, v_lru_ba_b, v_lru_wx_b, v_lru_bx_b, v_lru_lambda_b, v_gla_wg_f, v_gla_bg_f, v_gla_wg_b, v_gla_bg_b, v_gla_head_norm, v_w_out, v_w_mlp_up, v_w_mlp_down):
    args = dict(locals())
    w_loc = {n: args[n] for n in WEIGHTS}
    m_loc = {n: args["m_" + n] for n in WEIGHTS}
    v_loc = {n: args["v_" + n] for n in WEIGHTS}
    seq = x.shape[1]
    t_rows = FRONT + seq + BACK
    assert t_rows % 768 == 0 and seq % FRONT == 0, seq
    chip = 2 * lax.axis_index("x") + lax.axis_index("y")
    c_idx = lax.axis_index("c").astype(jnp.int32).reshape(1)

    def halves(a):
        return a.reshape((2, a.shape[0] // 2) + a.shape[1:])

    big16 = {n: w_loc[n].astype(BF16) for n in BIG}
    sm_names = list(SMALL_SHARDED)
    sm_len = _round_up(sum(w_loc[n].size for n in sm_names), 2 * SUBLANES * FLAT_W)
    sm_pack = _flat_cat([w_loc[n] for n in sm_names], sm_len).reshape(2, -1, FLAT_W)
    g_in0, g_small = _run_comm(_allgather_chips([halves(big16["w_in"][0]), sm_pack]), "gather_first")
    sm_parts = [_split_flat(g_small[q].reshape(-1), [w_loc[n].shape for n in sm_names]) for q in range(N_CHIPS)]
    full = {n: jnp.concatenate([sm_parts[q][i] for q in range(N_CHIPS)], axis=SMALL_SHARDED[n])
            for i, n in enumerate(sm_names)}
    for n in SMALL:
        if n not in SMALL_SHARDED:
            full[n] = w_loc[n]

    def full_w_in(g):
        w = jnp.transpose(g.reshape(N_CHIPS, D_MODEL, -1), (1, 0, 2)).reshape(D_MODEL, D_IN)
        return jnp.pad(w, ((0, 0), (0, D_IN_PAD - D_IN)))

    def full_rest(g_out, g_up, g_down):
        return dict(w_out=g_out.reshape(-1, D_MODEL),
                    w_up=jnp.transpose(g_up.reshape(N_CHIPS, D_MODEL, -1), (1, 0, 2)).reshape(D_MODEL, D_FF),
                    w_down=g_down.reshape(D_FF, D_MODEL))

    later_gathers = [_allgather_chips([halves(big16[n][0]) for n in BIG[1:]]),
                     _allgather_chips([halves(big16[n][l]) for l in range(1, DEPTH) for n in BIG])]

    meta_blk = jnp.concatenate([jnp.zeros((FRONT - N_META, D_MODEL), F32), full["meta_tokens"]], axis=0)
    h = jnp.concatenate([meta_blk, x[0], jnp.zeros((BACK, D_MODEL), F32)], axis=0)
    target = loss_target[0]

    layer_w = []
    for l in range(DEPTH):
        wg = jnp.zeros((LANES, 2 * QK), F32)
        wg = wg.at[0:GLA_RANK, 0:QK].set(full["gla_wg_f"][l]).at[GLA_RANK:2 * GLA_RANK, QK:].set(full["gla_wg_b"][l])
        pb = jnp.stack([full["lru_ba_f"][l], full["lru_bx_f"][l], full["lru_ba_b"][l], full["lru_bx_b"][l],
                        full["lru_lambda_f"][l], full["lru_lambda_b"][l], jnp.zeros((LRU_W,), F32),
                        jnp.zeros((LRU_W,), F32)])
        layer_w.append(dict(
            g1=full["norm_mix_pre"][l][None], g2=full["norm_mix_post"][l][None],
            g3=full["norm_mlp_pre"][l][None], g4=full["norm_mlp_post"][l][None],
            conv_w=full["conv_w"][l], conv_b=full["conv_b"][l][None],
            lru_wg=_block_diag_gates(full["lru_wa_f"][l], full["lru_wx_f"][l], full["lru_wa_b"][l], full["lru_wx_b"][l]),
            lru_pb=pb, gla_wg=wg.astype(BF16),
            gla_bg=jnp.concatenate([full["gla_bg_f"][l], full["gla_bg_b"][l]])[None],
            head_norm=full["gla_head_norm"][l][None]))
    layer_w[0]["w_in"] = full_w_in(g_in0)

    def split_z(acc):
        return (acc[:, 0:1024], acc[:, 1024:1536], acc[:, 1536:2048], acc[:, 2048:2560], acc[:, ZG_OFF:ZG_OFF + LANES])

    def relu2(acc):
        r = jnp.maximum(acc, 0.0)
        return acc, r * r

    tm_e = 384
    tm_ff = 1408 if t_rows % 1408 == 0 else 768

    def post_mix(acc, hh, g2, g3):
        h1 = hh + _rms_fwd(acc, g2)
        return acc, h1, _rms_fwd(h1, g3)

    def post_mlp(acc, h1, g4, g1_next):
        h2 = h1 + _rms_fwd(acc, g4)
        return acc, h2, _rms_fwd(h2, g1_next)

    def post_mlp_loss(acc, h1, tgt, g4):
        h2 = h1 + _rms_fwd(acc, g4)
        rows = pl.program_id(0) * tm_e + lax.broadcasted_iota(jnp.int32, (tm_e, 1), 0)
        err = jnp.where(jnp.logical_and(rows >= FRONT, rows < FRONT + seq), h2 - tgt, 0.0)
        d_h = err * (1.0 / D_MODEL)
        dff, dg4 = _rms_bwd(acc, g4, d_h)
        loss_part = jnp.zeros((SUBLANES, LANES), F32) + jnp.sum(err * err) * (0.5 / D_MODEL)
        return d_h, dff, loss_part, dg4

    row3 = [(D_MODEL, F32), (D_MODEL, F32), (D_MODEL, BF16)]
    vec = ((1, D_MODEL), F32)
    target_p = jnp.concatenate([jnp.zeros((FRONT, D_MODEL), F32), target, jnp.zeros((BACK, D_MODEL), F32)], axis=0)
    saved = []
    n1 = _norm_cast(h, layer_w[0]["g1"], t_rows, "norm_mix_pre_l0")
    for l in range(DEPTH):
        lw = layer_w[l]
        tag = f"_l{l}"
        z_lru, z_qk, z_v, z_go, z_zg = _mm(
            n1, lw["w_in"], "nn", 384, D_IN_PAD, D_MODEL,
            [(1024, F32), (512, F32), (512, F32), (512, F32), (LANES, F32)], "in_proj" + tag, epilogue=split_z)
        lru_args = (z_lru, lw["conv_w"], lw["conv_b"], lw["lru_wg"], lw["lru_pb"], t_rows, seq, "lru_fwd" + tag)
        gla_args = (z_qk, z_v, z_zg, lw["gla_wg"], lw["gla_bg"], t_rows, "gla_fwd" + tag)
        if l == 0:
            (h_f, h_b), g_rest = _lru_fwd(*lru_args, comm=later_gathers[0])
            lw.update(full_rest(*g_rest))
            (o_f, o_b, s_f, s_b), g_next = _gla_fwd(*gla_args, comm=later_gathers[1])
            for l2 in range(1, DEPTH):
                g_l2 = g_next[(l2 - 1) * len(BIG):l2 * len(BIG)]
                layer_w[l2].update(full_rest(*g_l2[1:]), w_in=full_w_in(g_l2[0]))
        else:
            h_f, h_b = _lru_fwd(*lru_args)
            o_f, o_b, s_f, s_b = _gla_fwd(*gla_args)
        ymix = _mix_fwd(h_f, h_b, z_lru, o_f, o_b, z_go, lw["head_norm"], t_rows, "mix_fwd" + tag)
        mix, h1, n3 = _mm(ymix, lw["w_out"], "nn", tm_e, D_MODEL, D_MODEL, row3, "out_proj" + tag, epilogue=post_mix,
                          rows_in=(h,), consts=(lw["g2"], lw["g3"]))
        u, act = _mm(n3, lw["w_up"], "nn", tm_ff, 1024, D_MODEL, [(D_FF, BF16), (D_FF, BF16)], "mlp_up" + tag,
                     epilogue=relu2)
        sv = dict(h=h, n1=n1, z_lru=z_lru, z_qk=z_qk, z_v=z_v, z_go=z_go, z_zg=z_zg, h_f=h_f, h_b=h_b,
                  o_f=o_f, o_b=o_b, s_f=s_f, s_b=s_b, ymix=ymix, mix=mix, h1=h1, n3=n3, u=u, act=act)
        if l + 1 < DEPTH:
            sv["ff"], h, n1 = _mm(act, lw["w_down"], "nn", tm_e, D_MODEL, D_FF, row3, "mlp_down" + tag,
                                  epilogue=post_mlp, rows_in=(h1,), consts=(lw["g4"], layer_w[l + 1]["g1"]))
        else:
            dh, dff, loss_part, dg4 = _mm(act, lw["w_down"], "nn", tm_e, D_MODEL, D_FF,
                                          [(D_MODEL, F32), (D_MODEL, BF16)], "mlp_down_loss" + tag,
                                          epilogue=post_mlp_loss, rows_in=(h1, target_p), consts=(lw["g4"],),
                                          accs=[((SUBLANES, LANES), F32), vec])
        saved.append(sv)

    loss = lax.psum(loss_part[0, 0], ("x", "y", "c"))

    small_len = _round_up(sum(math.prod(s) for _, s in SMALL_RAW) + math.prod(META_SHAPE),
                          N_CHIPS * 2 * 2 * SUBLANES * FLAT_W)
    totals = [None] * DEPTH
    pend = None

    def with_stage(fn):
        comm = pend.stage() if pend is not None else None
        if comm is None:
            return fn(None)
        res, comm_res = fn(comm)
        pend.done(comm_res)
        return res

    for l in reversed(range(DEPTH)):
        lw, sv = layer_w[l], saved[l]
        tag = f"_l{l}"
        raw = {"g4": dg4}
        gw_down = with_stage(lambda comm: _mm(sv["act"], dff, "tn", 1024, D_MODEL, 768, [(D_MODEL, F32)],
                                              "dw_down" + tag, comm=comm))[0]
        du = _mm(dff, lw["w_down"], "nt", tm_ff, 1024, D_MODEL, [(D_FF, BF16)], "d_act" + tag,
                 epilogue=lambda acc, uu: (acc * 2.0 * jnp.maximum(uu.astype(F32), 0.0),), extras=(sv["u"],))[0]
        gw_up = _mm(sv["n3"], du, "tn", D_MODEL, D_FF // N_CHIPS, 768, [(D_FF, F32)], "dw_up" + tag,
                    col_blocks_first=True)[0]

        def pre_mlp_bwd(acc, h1, d_h, mix, g3, g2):
            dx, dg3 = _rms_bwd(h1, g3, acc)
            d_h1 = d_h + dx
            d_mix, dg2 = _rms_bwd(mix, g2, d_h1)
            return d_h1, d_mix, dg3, dg2

        dh1, dmix, raw["g3"], raw["g2"] = _mm(
            du, lw["w_up"], "nt", tm_e, D_MODEL, D_FF, [(D_MODEL, F32), (D_MODEL, BF16)], "d_n3" + tag,
            epilogue=pre_mlp_bwd, rows_in=(sv["h1"], dh, sv["mix"]), consts=(lw["g3"], lw["g2"]), accs=[vec, vec])
        gw_out = _mm(sv["ymix"], dmix, "tn", D_MODEL, D_MODEL, 768, [(D_MODEL, F32)], "dw_out" + tag)[0]
        dymix = _mm(dmix, lw["w_out"], "nt", 768, D_MODEL, D_MODEL, [(D_MODEL, F32)], "d_ymix" + tag)[0]
        dh_lru, dgate, do, dgo, raw["head_norm"] = _mix_bwd(
            dymix, sv["h_f"], sv["h_b"], sv["z_lru"], sv["o_f"], sv["o_b"], sv["z_go"], lw["head_norm"], t_rows,
            "mix_bwd" + tag)
        gla_grads = with_stage(lambda comm: _gla_bwd(sv["z_qk"], sv["z_v"], sv["z_zg"], lw["gla_wg"], lw["gla_bg"], do,
                                                     sv["s_f"], sv["s_b"], t_rows, "gla_bwd" + tag, comm=comm))
        dx_br, raw["conv_w"], raw["conv_b"], raw["lru_wg"], raw["lru_pb"] = _lru_bwd(
            sv["z_lru"], lw["conv_w"], lw["conv_b"], lw["lru_wg"], lw["lru_pb"], sv["h_f"], sv["h_b"], dh_lru,
            t_rows, seq, "lru_bwd" + tag)
        dz, dwg_gla, raw["gla_bg"] = _dz_assemble(dx_br, dgate, gla_grads, dgo, sv["z_zg"], lw["gla_wg"], t_rows,
                                                  "dz_assemble" + tag)
        raw["gla_wg"] = dwg_gla[0:2 * GLA_RANK]
        gw_in = with_stage(lambda comm: _mm(sv["n1"], dz, "tn", D_MODEL, 896, 768, [(D_IN_PAD, F32)], "dw_in" + tag,
                                            comm=comm))[0]
        if l > 0:
            def pre_mix_bwd(acc, hh, d_h1, ff_prev, g1, g4_prev):
                dx, dg1 = _rms_bwd(hh, g1, acc)
                d_h = d_h1 + dx
                d_ff, dg4_prev = _rms_bwd(ff_prev, g4_prev, d_h)
                return d_h, d_ff, dg1, dg4_prev

            dh, dff, raw["g1"], dg4 = _mm(
                dz, lw["w_in"], "nt", tm_e, D_MODEL, D_IN_PAD, [(D_MODEL, F32), (D_MODEL, BF16)], "d_n1" + tag,
                epilogue=pre_mix_bwd, rows_in=(sv["h"], dh1, saved[l - 1]["ff"]),
                consts=(lw["g1"], layer_w[l - 1]["g4"]), accs=[vec, vec])
        else:
            def pre_mix_bwd0(acc, hh, d_h1, g1):
                dx, dg1 = _rms_bwd(hh, g1, acc)
                return d_h1 + dx, dg1

            dh, raw["g1"] = _mm(dz, lw["w_in"], "nt", tm_e, D_MODEL, D_IN_PAD, [(D_MODEL, F32)], "d_n1" + tag,
                                epilogue=pre_mix_bwd0, rows_in=(sv["h"], dh1), consts=(lw["g1"],), accs=[vec])

        cols = D_IN // N_CHIPS
        in_sh = jnp.stack([gw_in[:, q * cols:(q + 1) * cols] for q in range(N_CHIPS)])
        meta_g = dh[FRONT - N_META:FRONT] if l == 0 else jnp.zeros(META_SHAPE, F32)
        small = _flat_cat([raw[n] for n, _ in SMALL_RAW] + [meta_g], small_len)
        pack = [in_sh.reshape(N_CHIPS, 2, D_MODEL // 2, cols),
                gw_out.reshape(N_CHIPS, 2, -1, D_MODEL),
                gw_up.reshape(N_CHIPS, 2, D_MODEL // 2, D_FF // N_CHIPS),
                gw_down.reshape(N_CHIPS, 2, -1, D_MODEL),
                small.reshape(N_CHIPS, 2, -1, FLAT_W)]
        if pend is not None:
            totals[l + 1] = pend.drain()
        pend = _Staged(_reduce_stages(pack, c_idx, tag), tag)
    totals[0] = pend.drain()

    grad_x = dh[FRONT:FRONT + seq][None]

    sm_all = _run_comm(_allgather_chips([totals[l][4] for l in range(DEPTH)]), "gather_small_grads")
    g_tot = {}
    per_layer = []
    for l in range(DEPTH):
        pieces = _split_flat(sm_all[l].reshape(-1), [s for _, s in SMALL_RAW] + [META_SHAPE])
        per_layer.append(dict(zip([n for n, _ in SMALL_RAW] + ["meta"], pieces)))
    stack = lambda f: jnp.stack([f(per_layer[l]) for l in range(DEPTH)])
    g_tot["meta_tokens"] = _my_shard(per_layer[0]["meta"], 1, chip)
    for n, key in (("norm_mix_pre", "g1"), ("norm_mix_post", "g2"), ("norm_mlp_pre", "g3"), ("norm_mlp_post", "g4"),
                   ("conv_b", "conv_b"), ("gla_head_norm", "head_norm")):
        g_tot[n] = stack(lambda d, key=key: d[key][0])
    g_tot["conv_w"] = _my_shard(stack(lambda d: d["conv_w"]), 2, chip)
    for o, n in enumerate(("lru_wa_f", "lru_wx_f", "lru_wa_b", "lru_wx_b")):
        g_tot[n] = stack(lambda d, o=o: _gate_diag_blocks(d["lru_wg"], o))
    for row, n in enumerate(("lru_ba_f", "lru_bx_f", "lru_ba_b", "lru_bx_b", "lru_lambda_f", "lru_lambda_b")):
        g_tot[n] = stack(lambda d, row=row: d["lru_pb"][row])
    g_tot["gla_wg_f"] = _my_shard(stack(lambda d: d["gla_wg"][0:GLA_RANK, 0:QK]), 2, chip)
    g_tot["gla_wg_b"] = _my_shard(stack(lambda d: d["gla_wg"][GLA_RANK:, QK:]), 2, chip)
    g_tot["gla_bg_f"] = stack(lambda d: d["gla_bg"][0, 0:QK])
    g_tot["gla_bg_b"] = stack(lambda d: d["gla_bg"][0, QK:])

    delta, new_m, new_v = {}, {}, {}
    for a, n in enumerate(BIG):
        shp = w_loc[n].shape
        flat3 = lambda t, shp=shp: t.reshape(DEPTH, -1, shp[-1])
        g_l = [totals[l][a].reshape(-1, shp[-1]) for l in range(DEPTH)]
        outs = _adamw_layers(flat3(w_loc[n]), flat3(m_loc[n]), flat3(v_loc[n]), g_l, "adamw_" + n)
        g_tot[n], delta[n], new_m[n], new_v[n] = [o.reshape(shp) for o in outs]
    a_rows = _round_up(sum(w_loc[n].size for n in SMALL), SUBLANES * FLAT_W) // FLAT_W
    pack_small = lambda d: _flat_cat([d[n] for n in SMALL], a_rows * FLAT_W).reshape(a_rows, FLAT_W)
    outs = _adamw_flat(pack_small(w_loc), pack_small(g_tot), pack_small(m_loc), pack_small(v_loc))
    shapes = [w_loc[n].shape for n in SMALL]
    for d, o in zip((delta, new_m, new_v), outs):
        d.update(zip(SMALL, _split_flat(o.reshape(-1), shapes)))
    return (loss, grad_x, *[g_tot[n] for n in WEIGHTS], *[delta[n] for n in WEIGHTS],
            *[new_m[n] for n in WEIGHTS], *[new_v[n] for n in WEIGHTS])
```

```python
import math

import jax
import jax.numpy as jnp
from jax import lax
from jax.experimental import pallas as pl
from jax.experimental.pallas import tpu as pltpu

F32 = jnp.float32
BF16 = jnp.bfloat16

D_MODEL = 1024
DEPTH = 2
N_META = 16
LRU_W = 512
LRU_HEADS = 8
LRU_HD = 64
LRU_C = 8.0
GLA_W = 512
GLA_H = 4
GLA_DK = 64
GLA_DV = 128
GLA_RANK = 16
GLA_GATE_NORM = 16.0
D_FF = 4096
D_IN = 2592
EPS = 1e-6
ADAM_LR, ADAM_B1, ADAM_B2, ADAM_EPS, ADAM_WD, ADAM_STEP = 0.001, 0.9, 0.999, 1e-08, 0.01, 10

LANES = 128
SUBLANES = 8
FRONT = 128
BACK = 128
D_IN_PAD = 2688
ZG_OFF = 2560
CHUNK = 64
EXP_CLAMP = 80.0
VMEM_LIMIT = 56 * 1024 * 1024
MESH_ID = pl.DeviceIdType.MESH

NN = (((1,), (0,)), ((), ()))
NT = (((1,), (1,)), ((), ()))
TN = (((0,), (0,)), ((), ()))


def _dot(a, b, dims=NN):
    return lax.dot_general(a, b, dims, preferred_element_type=F32)


def _cparams(sem):
    return pltpu.CompilerParams(dimension_semantics=sem, vmem_limit_bytes=VMEM_LIMIT)


def _sigmoid(x):
    return 1.0 / (1.0 + jnp.exp(-x))


def _gelu(x):
    c = math.sqrt(2.0 / math.pi)
    return 0.5 * x * (1.0 + jnp.tanh(c * (x + 0.044715 * x * x * x)))


def _gelu_grad(x):
    c = math.sqrt(2.0 / math.pi)
    t = jnp.tanh(c * (x + 0.044715 * x * x * x))
    return 0.5 * (1.0 + t) + 0.5 * x * (1.0 - t * t) * c * (1.0 + 3.0 * 0.044715 * x * x)


def _rms_fwd(x, g):
    rstd = lax.rsqrt(jnp.mean(x * x, axis=1, keepdims=True) + EPS)
    return (x * rstd) * g


def _rms_bwd(x, g, dy):
    rstd = lax.rsqrt(jnp.mean(x * x, axis=1, keepdims=True) + EPS)
    xh = x * rstd
    dxh = dy * g
    dx = rstd * (dxh - xh * jnp.mean(dxh * xh, axis=1, keepdims=True))
    dg = jnp.sum(dy * xh, axis=0, keepdims=True)
    return dx, dg


def _mm(a, b, mode, tm, tn, tk, outs, name, epilogue=None, extras=(), col_blocks_first=False,
        rows_in=(), consts=(), accs=(), comm=None, row_split=1):
    if mode == "nn":
        (m, k), n = a.shape, b.shape[1]
        a_spec = pl.BlockSpec((tm, tk), lambda i, j, kk: (i, kk))
        b_spec = pl.BlockSpec((tk, tn), lambda i, j, kk: (kk, j))
        dims = NN
    elif mode == "nt":
        (m, k), n = a.shape, b.shape[0]
        a_spec = pl.BlockSpec((tm, tk), lambda i, j, kk: (i, kk))
        b_spec = pl.BlockSpec((tn, tk), lambda i, j, kk: (j, kk))
        dims = NT
    else:
        (k, m), n = a.shape, b.shape[1]
        a_spec = pl.BlockSpec((tk, tm), lambda i, j, kk: (kk, i))
        b_spec = pl.BlockSpec((tk, tn), lambda i, j, kk: (kk, j))
        dims = TN
    assert m % tm == 0 and n % tn == 0 and k % tk == 0, (name, m, n, k)
    nk = k // tk
    ne = len(extras) + len(rows_in) + len(consts)
    e_specs = [pl.BlockSpec((tm, tn), lambda i, j, kk: (i, j)) for _ in extras]
    e_specs += [pl.BlockSpec((tm, r.shape[1]), lambda i, j, kk: (i, 0)) for r in rows_in]
    e_specs += [pl.BlockSpec(c.shape, lambda i, j, kk: (0, 0)) for c in consts]
    n_out = len(outs)
    o_specs, o_shapes = [], []
    for width, dtype in outs:
        if col_blocks_first:
            assert width == n, name
            o_specs.append(pl.BlockSpec((None, tm, tn), lambda i, j, kk: (j, i, 0)))
            o_shapes.append(jax.ShapeDtypeStruct((n // tn, m, tn), dtype))
            continue
        if width == n:
            o_specs.append(pl.BlockSpec((tm, tn), lambda i, j, kk: (i, j)))
        else:
            assert n == tn, name
            o_specs.append(pl.BlockSpec((tm, width), lambda i, j, kk: (i, 0)))
        o_shapes.append(jax.ShapeDtypeStruct((m, width), dtype))
    o_specs += [pl.BlockSpec(s, lambda i, j, kk: (0, 0)) for s, _ in accs]
    o_shapes += [jax.ShapeDtypeStruct(s, d) for s, d in accs]

    n_tiled = len(extras) + len(rows_in)

    def finish(acc, extra_refs, out_refs, rs=slice(None), zero_accs=True):
        tiles = [e[rs, :] for e in extra_refs[:n_tiled]] + [e[...] for e in extra_refs[n_tiled:]]
        res = epilogue(acc, *tiles) if epilogue else (acc,)
        for o, r in zip(out_refs[:n_out], res[:n_out]):
            o[rs, :] = r.astype(o.dtype)
        first = jnp.logical_and(pl.program_id(0) == 0, pl.program_id(1) == 0)
        for o, r in zip(out_refs[n_out:], res[n_out:]):
            if zero_accs:
                @pl.when(first)
                def _(o=o):
                    o[...] = jnp.zeros_like(o)

            o[...] += r

    if nk == 1 and row_split > 1:
        assert mode != "tn" and tm % row_split == 0, name
        sub = tm // row_split

        def body(a_ref, b_ref, *rest):
            for s in range(row_split):
                rs = slice(s * sub, (s + 1) * sub)
                finish(_dot(a_ref[rs, :], b_ref[...], dims), rest[:ne], rest[ne:], rs, zero_accs=(s == 0))
        scratch = []
    elif nk == 1:
        def body(a_ref, b_ref, *rest):
            finish(_dot(a_ref[...], b_ref[...], dims), rest[:ne], rest[ne:])
        scratch = []
    else:
        def body(a_ref, b_ref, *rest):
            acc_ref = rest[-1]
            kk = pl.program_id(2)

            @pl.when(kk == 0)
            def _():
                acc_ref[...] = jnp.zeros_like(acc_ref)

            acc_ref[...] += _dot(a_ref[...], b_ref[...], dims)

            @pl.when(kk == nk - 1)
            def _():
                finish(acc_ref[...], rest[:ne], rest[ne:-1])
        scratch = [pltpu.VMEM((tm, tn), F32)]

    sem = ("arbitrary", "arbitrary", "arbitrary") if accs else ("parallel", "parallel", "arbitrary")
    res, comm_res = _pcall(body, name, (m // tm, n // tn, nk), [a_spec, b_spec] + e_specs, o_specs, o_shapes, scratch,
                           sem, (a, b, *extras, *rows_in, *consts), comm)
    return res if comm is None else (res, comm_res)


def _rows(body, n_rows, tm, ins, consts, outs, accs, name):
    assert n_rows % tm == 0, (name, n_rows, tm)
    in_specs = [pl.BlockSpec((tm, w), (lambda i, cb=cb: (i, cb))) for _, w, cb in ins]
    in_specs += [pl.BlockSpec(c.shape, lambda i: (0, 0)) for c in consts]
    out_specs = [pl.BlockSpec((tm, w), lambda i: (i, 0)) for w, _ in outs]
    out_specs += [pl.BlockSpec(s, lambda i: (0, 0)) for s, _ in accs]
    out_shape = [jax.ShapeDtypeStruct((n_rows, w), d) for w, d in outs]
    out_shape += [jax.ShapeDtypeStruct(s, d) for s, d in accs]
    ni, nc, no = len(ins), len(consts), len(outs)

    def kern(*refs):
        body(pl.program_id(0), refs[:ni], refs[ni:ni + nc], refs[ni + nc:ni + nc + no], refs[ni + nc + no:])

    res = pl.pallas_call(
        kern, name=name, grid=(n_rows // tm,), in_specs=in_specs, out_specs=out_specs, out_shape=out_shape,
        compiler_params=_cparams(("arbitrary",)),
    )(*[a for a, _, _ in ins], *consts)
    return res


def _acc_add(i, ref, val):
    @pl.when(i == 0)
    def _():
        ref[...] = jnp.zeros_like(ref)

    ref[...] += val


def _norm_cast(h, g, t_rows, name):
    def body(i, ins, consts, outs, accs):
        outs[0][...] = _rms_fwd(ins[0][...], consts[0][...]).astype(BF16)
    return _rows(body, t_rows, 384, [(h, D_MODEL, 0)], [g], [(D_MODEL, BF16)], [], name)[0]


def _head_norm_parts(o):
    ns, rs = [], []
    for hd in range(GLA_H):
        oh = o[:, hd * GLA_DV:(hd + 1) * GLA_DV]
        r = lax.rsqrt(jnp.mean(oh * oh, axis=1, keepdims=True) + EPS)
        ns.append(oh * r)
        rs.append(r)
    return ns, rs


def _mix_fwd(h_f, h_b, z_lru, o_f, o_b, z_go, head_norm, t_rows, name):
    def body(i, ins, consts, outs, accs):
        hf, hb, gate, of, ob, go = [r[...] for r in ins]
        hn = consts[0][...]
        outs[0][:, 0:LRU_W] = ((hf + hb) * _gelu(gate)).astype(BF16)
        ns, _ = _head_norm_parts(of + ob)
        silu = go * _sigmoid(go)
        for hd in range(GLA_H):
            sl = slice(hd * GLA_DV, (hd + 1) * GLA_DV)
            outs[0][:, LRU_W + hd * GLA_DV:LRU_W + (hd + 1) * GLA_DV] = (ns[hd] * hn[:, sl] * silu[:, sl]).astype(BF16)
    ins = [(h_f, LRU_W, 0), (h_b, LRU_W, 0), (z_lru, LRU_W, 1), (o_f, GLA_W, 0), (o_b, GLA_W, 0), (z_go, GLA_W, 0)]
    return _rows(body, t_rows, 384, ins, [head_norm], [(D_MODEL, BF16)], [], name)[0]


def _mix_bwd(dymix, h_f, h_b, z_lru, o_f, o_b, z_go, head_norm, t_rows, name):
    def body(i, ins, consts, outs, accs):
        dyl, dyg, hf, hb, gate, of, ob, go = [r[...] for r in ins]
        hn = consts[0][...]
        outs[0][...] = dyl * _gelu(gate)
        outs[1][...] = dyl * (hf + hb) * _gelu_grad(gate)
        ns, rs = _head_norm_parts(of + ob)
        sg = _sigmoid(go)
        silu = go * sg
        dsilu = sg * (1.0 + go * (1.0 - sg))
        dhn = []
        for hd in range(GLA_H):
            sl = slice(hd * GLA_DV, (hd + 1) * GLA_DV)
            dy = dyg[:, sl]
            e = dy * hn[:, sl] * silu[:, sl]
            outs[2][:, sl] = rs[hd] * (e - ns[hd] * jnp.mean(e * ns[hd], axis=1, keepdims=True))
            outs[3][:, sl] = dy * ns[hd] * hn[:, sl] * dsilu[:, sl]
            dhn.append(jnp.sum(dy * ns[hd] * silu[:, sl], axis=0, keepdims=True))
        _acc_add(i, accs[0], jnp.concatenate(dhn, axis=1))
    ins = [(dymix, LRU_W, 0), (dymix, GLA_W, 1), (h_f, LRU_W, 0), (h_b, LRU_W, 0), (z_lru, LRU_W, 1),
           (o_f, GLA_W, 0), (o_b, GLA_W, 0), (z_go, GLA_W, 0)]
    return _rows(body, t_rows, 384, ins, [head_norm],
                 [(LRU_W, F32), (LRU_W, F32), (GLA_W, F32), (GLA_W, F32)], [((1, GLA_W), F32)], name)


def _dz_assemble(dx_br, dgate, gla_grads, dgo, z_zg, wg, t_rows, name):
    dq_f, dk_f, dv_f, dpre_f, dq_b, dk_b, dv_b, dpre_b = gla_grads
    qk = GLA_H * GLA_DK

    def body(i, ins, consts, outs, accs):
        (dxb, dgt, dqf, dqb, dkf, dkb, dvf, dvb, dg_o, dpf, dpb, zg) = [r[...] for r in ins]
        w = consts[0][...]
        o = outs[0]
        o[:, 0:LRU_W] = dxb.astype(BF16)
        o[:, LRU_W:2 * LRU_W] = dgt.astype(BF16)
        o[:, 1024:1024 + qk] = ((dqf + dqb) * (GLA_DK ** -0.5)).astype(BF16)
        o[:, 1280:1280 + qk] = (dkf + dkb).astype(BF16)
        o[:, 1536:1536 + GLA_W] = (dvf + dvb).astype(BF16)
        o[:, 2048:2048 + GLA_W] = dg_o.astype(BF16)
        dpre = jnp.concatenate([dpf, dpb], axis=1)
        dpre16 = dpre.astype(BF16)
        o[:, ZG_OFF:ZG_OFF + LANES] = _dot(dpre16, w, NT).astype(BF16)
        _acc_add(i, accs[0], _dot(zg.astype(BF16), dpre16, TN))
        _acc_add(i, accs[1], jnp.sum(dpre, axis=0, keepdims=True))

    ins = [(dx_br, LRU_W, 0), (dgate, LRU_W, 0), (dq_f, qk, 0), (dq_b, qk, 0), (dk_f, qk, 0), (dk_b, qk, 0),
           (dv_f, GLA_W, 0), (dv_b, GLA_W, 0), (dgo, GLA_W, 0), (dpre_f, qk, 0), (dpre_b, qk, 0), (z_zg, LANES, 0)]
    return _rows(body, t_rows, 384, ins, [wg], [(D_IN_PAD, BF16)],
                 [((LANES, 2 * qk), F32), ((1, 2 * qk), F32)], name)


LRU_RB = 64
HALO = SUBLANES


def _scan8(a, u, rev):
    rows = lax.broadcasted_iota(jnp.int32, a.shape, 0)
    for d in (1, 2, 4):
        if rev:
            a_s, u_s, ok = pltpu.roll(a, SUBLANES - d, 0), pltpu.roll(u, SUBLANES - d, 0), rows < SUBLANES - d
        else:
            a_s, u_s, ok = pltpu.roll(a, d, 0), pltpu.roll(u, d, 0), rows >= d
        u = jnp.where(ok, a * u_s + u, u)
        a = jnp.where(ok, a * a_s, a)
    return a, u


def _edge_row(x, rev):
    r = x[0:1, :] if rev else x[SUBLANES - 1:SUBLANES, :]
    return jnp.broadcast_to(r, x.shape)


def _scan_block(a, u, carry, rev):
    nsub = a.shape[0] // SUBLANES
    loc = [_scan8(a[sb * SUBLANES:(sb + 1) * SUBLANES], u[sb * SUBLANES:(sb + 1) * SUBLANES], rev) for sb in range(nsub)]
    edges = [(_edge_row(a_c, rev), _edge_row(h_l, rev)) for a_c, h_l in loc]
    carries = [None] * nsub
    for sb in (range(nsub - 1, -1, -1) if rev else range(nsub)):
        carries[sb] = carry
        carry = edges[sb][0] * carry + edges[sb][1]
    h = jnp.concatenate([h_l + a_c * cin for (a_c, h_l), cin in zip(loc, carries)], axis=0)
    return h, carry


def _lru_gates(xc, pre_a, pre_x, ba, bx, sp8):
    r = _sigmoid(pre_a + ba)
    i = _sigmoid(pre_x + bx)
    log_a = -(r * sp8)
    a = jnp.exp(log_a)
    y = 2.0 * log_a
    series = -y * (1.0 + y * (0.5 + y * (1.0 / 6.0 + y * (1.0 / 24.0 + y * (1.0 / 120.0)))))
    om = jnp.where(y > -0.25, series, 1.0 - a * a)
    s = jnp.sqrt(om)
    return r, i, a, s


def _softplus(x):
    return jnp.maximum(x, 0.0) + jnp.log(1.0 + jnp.exp(-jnp.abs(x)))


def _conv_taps(xpad_ref, r0, nrows, shifts=(-2, -1, 0, 1)):
    ext = xpad_ref[pl.ds(r0, nrows + 2 * HALO), :]
    taps = []
    for s in shifts:
        sh = ext if s == 0 else pltpu.roll(ext, (-s) % (nrows + 2 * HALO), 0)
        taps.append(sh[HALO:HALO + nrows])
    return taps


def _row_mask(r0, nrows, seq):
    rows = r0 + lax.broadcasted_iota(jnp.int32, (nrows, LANES), 0)
    return jnp.logical_and(rows >= FRONT - N_META, rows < FRONT + seq).astype(F32)


def _lru_load_conv(z_hbm, xpad, xc, sem, cw_ref, cb_ref, t_rows, seq):
    j = pl.program_id(0)
    zeros = jnp.zeros((HALO, LANES), F32)
    xpad[0:HALO, :] = zeros
    xpad[t_rows + HALO:t_rows + 2 * HALO, :] = zeros
    cp = pltpu.make_async_copy(z_hbm.at[:, pl.ds(pl.multiple_of(j * LANES, LANES), LANES)],
                               xpad.at[pl.ds(HALO, t_rows)], sem)
    cp.start()
    cp.wait()
    cw = cw_ref[...]
    cb = cb_ref[...]

    def conv_blk(r, carry):
        r0 = pl.multiple_of(r * LRU_RB, LRU_RB)
        taps = _conv_taps(xpad, r0, LRU_RB)
        acc = cb + cw[0:1, :] * taps[0]
        for k in range(1, 4):
            acc = acc + cw[k:k + 1, :] * taps[k]
        xc[pl.ds(r0, LRU_RB), :] = acc * _row_mask(r0, LRU_RB, seq)
        return carry

    lax.fori_loop(0, t_rows // LRU_RB, conv_blk, 0)


def _lru_specs(t_rows):
    return [pl.BlockSpec(memory_space=pl.ANY),
            pl.BlockSpec((4, LANES), lambda j: (0, j)),
            pl.BlockSpec((1, LANES), lambda j: (0, j)),
            pl.BlockSpec((1, LANES, 4 * LANES), lambda j: (j, 0, 0)),
            pl.BlockSpec((SUBLANES, LANES), lambda j: (0, j))]


def _lru_fwd(z_lru, conv_w, conv_b, wg, pb, t_rows, seq, name, comm=None):
    nblk = t_rows // LRU_RB
    nsub = LRU_RB // SUBLANES

    def kern(z_hbm, cw_ref, cb_ref, wg_ref, pb_ref, hf_ref, hb_ref, xpad, xc, sem):
        _lru_load_conv(z_hbm, xpad, xc, sem, cw_ref, cb_ref, t_rows, seq)
        w = wg_ref[0]
        pb_v = pb_ref[...]
        dirs = []
        for rev in (False, True):
            o = 2 if rev else 0
            dirs.append(dict(rev=rev, ba=pb_v[o:o + 1, :], bx=pb_v[o + 1:o + 2, :],
                             sp8=LRU_C * _softplus(-pb_v[5:6, :] if rev else -pb_v[4:5, :]),
                             w=w[:, o * LANES:(o + 2) * LANES], out=hb_ref if rev else hf_ref))

        def blk(it, carries):
            new = []
            for d, carry in zip(dirs, carries):
                r = (nblk - 1 - it) if d["rev"] else it
                r0 = pl.multiple_of(r * LRU_RB, LRU_RB)
                xcb = xc[pl.ds(r0, LRU_RB), :]
                pre = _dot(xcb.astype(BF16), d["w"])
                _, gi, a, s = _lru_gates(xcb, pre[:, 0:LANES], pre[:, LANES:2 * LANES], d["ba"], d["bx"], d["sp8"])
                h, carry = _scan_block(a, s * (gi * xcb), carry, d["rev"])
                d["out"][pl.ds(r0, LRU_RB), :] = h
                new.append(carry)
            return tuple(new)

        z8 = jnp.zeros((SUBLANES, LANES), F32)
        lax.fori_loop(0, nblk, blk, (z8, z8), unroll=2)

    res, comm_res = _pcall(
        kern, name, (LRU_W // LANES,), _lru_specs(t_rows), [pl.BlockSpec((t_rows, LANES), lambda j: (0, j))] * 2,
        [jax.ShapeDtypeStruct((t_rows, LRU_W), F32)] * 2,
        [pltpu.VMEM((t_rows + 2 * HALO, LANES), F32), pltpu.VMEM((t_rows, LANES), F32), pltpu.SemaphoreType.DMA],
        ("arbitrary",), (z_lru, conv_w, conv_b, wg, pb), comm)
    return res if comm is None else (res, comm_res)


def _lru_bwd(z_lru, conv_w, conv_b, wg, pb, h_f, h_b, dh, t_rows, seq, name):
    nblk = t_rows // LRU_RB
    nsub = LRU_RB // SUBLANES

    def kern(z_hbm, cw_ref, cb_ref, wg_ref, pb_ref, hf_ref, hb_ref, dh_ref,
             dx_ref, dcw_ref, dcb_ref, dwg_ref, dpb_ref, xpad, xc, dxc, sem):
        _lru_load_conv(z_hbm, xpad, xc, sem, cw_ref, cb_ref, t_rows, seq)
        w = wg_ref[0]
        pb_v = pb_ref[...]
        zeros = jnp.zeros((HALO, LANES), F32)
        dxc[0:HALO, :] = zeros
        dxc[t_rows + HALO:t_rows + 2 * HALO, :] = zeros
        dwg_ref[...] = jnp.zeros_like(dwg_ref)

        def zero_blk(r, carry):
            dxc[pl.ds(pl.multiple_of(r * LRU_RB, LRU_RB) + HALO, LRU_RB), :] = jnp.zeros((LRU_RB, LANES), F32)
            return carry

        lax.fori_loop(0, nblk, zero_blk, 0)
        dirs = []
        for rev in (False, True):
            o = 2 if rev else 0
            lam = pb_v[5:6, :] if rev else pb_v[4:5, :]
            dirs.append(dict(rev=rev, o=o, ba=pb_v[o:o + 1, :], bx=pb_v[o + 1:o + 2, :], sp8=LRU_C * _softplus(-lam),
                             dlam_scale=LRU_C * _sigmoid(-lam), h_ref=hb_ref if rev else hf_ref,
                             w=w[:, o * LANES:(o + 2) * LANES]))
        rows_b = lax.broadcasted_iota(jnp.int32, (LRU_RB, LANES), 0)

        def blk(it, carries):
            new = []
            for d, (nu_c, acc_ba, acc_bx, acc_lam) in zip(dirs, carries):
                rev, o, h_ref = d["rev"], d["o"], d["h_ref"]
                adj_rev = not rev
                r = (nblk - 1 - it) if adj_rev else it
                r0 = pl.multiple_of(r * LRU_RB, LRU_RB)
                xcb = xc[pl.ds(r0, LRU_RB), :]
                xc16 = xcb.astype(BF16)
                pre = _dot(xc16, d["w"])
                gr, gi, a, s = _lru_gates(xcb, pre[:, 0:LANES], pre[:, LANES:2 * LANES], d["ba"], d["bx"], d["sp8"])
                dhb = dh_ref[pl.ds(r0, LRU_RB), :]
                hb = h_ref[pl.ds(r0, LRU_RB), :]
                if rev:
                    nxt0 = pl.multiple_of(jnp.minimum(r0 + LRU_RB, t_rows - SUBLANES), SUBLANES)
                    edge = h_ref[pl.ds(nxt0, SUBLANES), :][0:1, :] * (r < nblk - 1).astype(F32)
                    h_prev = jnp.concatenate([hb[1:], edge], axis=0)
                else:
                    prv0 = pl.multiple_of(jnp.maximum(r0 - SUBLANES, 0), SUBLANES)
                    edge = h_ref[pl.ds(prv0, SUBLANES), :][SUBLANES - 1:SUBLANES, :] * (r > 0).astype(F32)
                    h_prev = jnp.concatenate([edge, hb[:-1]], axis=0)
                nu, nu_out = _scan_block(a, a * dhb, nu_c, adj_rev)
                cin = jnp.broadcast_to(nu_c[0:1, :], (LRU_RB, LANES))
                if adj_rev:
                    nu_next = jnp.where(rows_b == LRU_RB - 1, cin, pltpu.roll(nu, LRU_RB - 1, 0))
                else:
                    nu_next = jnp.where(rows_b == 0, cin, pltpu.roll(nu, 1, 0))
                du = dhb + nu_next
                da = du * h_prev
                ix = gi * xcb
                dlog = da * a - du * ix * (a * a) / s
                d_i = du * s * xcb
                dr = -dlog * d["sp8"]
                dpa = dr * gr * (1.0 - gr)
                dpx = d_i * gi * (1.0 - gi)
                dp16 = jnp.concatenate([dpa, dpx], axis=1).astype(BF16)
                dxc[pl.ds(r0 + HALO, LRU_RB), :] += du * s * gi + _dot(dp16, d["w"], NT)
                dwg_ref[0, :, o * LANES:(o + 2) * LANES] += _dot(xc16, dp16, TN)
                new.append((nu_out, acc_ba + jnp.sum(dpa, axis=0, keepdims=True),
                            acc_bx + jnp.sum(dpx, axis=0, keepdims=True),
                            acc_lam + jnp.sum(dlog * gr, axis=0, keepdims=True)))
            return tuple(new)

        z1 = jnp.zeros((1, LANES), F32)
        init = (jnp.zeros((SUBLANES, LANES), F32), z1, z1, z1)
        (_, ba_f, bx_f, lam_f), (_, ba_b, bx_b, lam_b) = lax.fori_loop(0, nblk, blk, (init, init), unroll=2)
        dpb_ref[...] = jnp.concatenate([ba_f, bx_f, ba_b, bx_b, lam_f * dirs[0]["dlam_scale"],
                                        lam_b * dirs[1]["dlam_scale"], z1, z1], axis=0)

        def mask_blk(r, carry):
            r0 = pl.multiple_of(r * LRU_RB, LRU_RB)
            dxc[pl.ds(r0 + HALO, LRU_RB), :] = dxc[pl.ds(r0 + HALO, LRU_RB), :] * _row_mask(r0, LRU_RB, seq)
            return carry

        lax.fori_loop(0, nblk, mask_blk, 0)

        cw = cw_ref[...]

        def conv_bwd(r, carry):
            r0 = pl.multiple_of(r * LRU_RB, LRU_RB)
            gt = _conv_taps(dxc, r0, LRU_RB, (2, 1, 0, -1))
            xt = _conv_taps(xpad, r0, LRU_RB)
            dx_ref[pl.ds(r0, LRU_RB), :] = (cw[0:1, :] * gt[0] + cw[1:2, :] * gt[1] + cw[2:3, :] * gt[2]
                                           + cw[3:4, :] * gt[3])
            g = gt[2]
            new = [carry[k] + jnp.sum(g * xt[k], axis=0, keepdims=True) for k in range(4)]
            new.append(carry[4] + jnp.sum(g, axis=0, keepdims=True))
            return tuple(new)

        z1 = jnp.zeros((1, LANES), F32)
        sums = lax.fori_loop(0, nblk, conv_bwd, (z1, z1, z1, z1, z1))
        dcw_ref[...] = jnp.concatenate(sums[:4], axis=0)
        dcb_ref[...] = sums[4]

    col = lambda j: (0, j)
    return pl.pallas_call(
        kern, name=name, grid=(LRU_W // LANES,),
        in_specs=_lru_specs(t_rows) + [pl.BlockSpec((t_rows, LANES), col)] * 3,
        out_specs=[pl.BlockSpec((t_rows, LANES), col), pl.BlockSpec((4, LANES), col), pl.BlockSpec((1, LANES), col),
                   pl.BlockSpec((1, LANES, 4 * LANES), lambda j: (j, 0, 0)), pl.BlockSpec((SUBLANES, LANES), col)],
        out_shape=[jax.ShapeDtypeStruct((t_rows, LRU_W), F32), jax.ShapeDtypeStruct((4, LRU_W), F32),
                   jax.ShapeDtypeStruct((1, LRU_W), F32), jax.ShapeDtypeStruct((4, LANES, 4 * LANES), F32),
                   jax.ShapeDtypeStruct((SUBLANES, LRU_W), F32)],
        scratch_shapes=[pltpu.VMEM((t_rows + 2 * HALO, LANES), F32), pltpu.VMEM((t_rows, LANES), F32),
                        pltpu.VMEM((t_rows + 2 * HALO, LANES), F32), pltpu.SemaphoreType.DMA],
        compiler_params=_cparams(("arbitrary",)),
    )(z_lru, conv_w, conv_b, wg, pb, h_f, h_b, dh)


QK = GLA_H * GLA_DK
GLA_G = 4


def _cumsum_rows(x, rev):
    n = x.shape[0]
    rows = lax.broadcasted_iota(jnp.int32, x.shape, 0)
    d = 1
    while d < n:
        if rev:
            x = x + jnp.where(rows < n - d, pltpu.roll(x, n - d, 0), 0.0)
        else:
            x = x + jnp.where(rows >= d, pltpu.roll(x, d, 0), 0.0)
        d *= 2
    return x


def _gla_chunk_terms(q, k, zg, wg, bg, rev):
    pre = (_dot(zg.astype(BF16), wg) + bg)[:, (QK if rev else 0):(2 * QK if rev else QK)]
    g = (jnp.minimum(pre, 0.0) - jnp.log(1.0 + jnp.exp(-jnp.abs(pre)))) * (1.0 / GLA_GATE_NORM)
    b = _cumsum_rows(g, rev)
    b_last = b[0:1, :] if rev else b[CHUNK - 1:CHUNK, :]
    b_mid = b[CHUNK // 2:CHUNK // 2 + 1, :]
    qs = q * (GLA_DK ** -0.5)
    terms = dict(
        pre=pre, qs=qs, k=k,
        eb=jnp.exp(b), ek=jnp.exp(b_last - b), e_last=jnp.exp(b_last),
        eqm=jnp.exp(jnp.minimum(b - b_mid, EXP_CLAMP)), ekm=jnp.exp(jnp.minimum(b_mid - b, EXP_CLAMP)))
    return terms


def _gla_mask(rev):
    t = lax.broadcasted_iota(jnp.int32, (CHUNK, CHUNK), 0)
    s = lax.broadcasted_iota(jnp.int32, (CHUNK, CHUNK), 1)
    return (s > t) if rev else (s <= t)


def _gla_head_ops(tm, hd):
    sl = slice(hd * GLA_DK, (hd + 1) * GLA_DK)
    q_b = (tm["qs"][:, sl] * tm["eb"][:, sl]).astype(BF16)
    k_e = (tm["k"][:, sl] * tm["ek"][:, sl]).astype(BF16)
    q_m = (tm["qs"][:, sl] * tm["eqm"][:, sl]).astype(BF16)
    k_m = (tm["k"][:, sl] * tm["ekm"][:, sl]).astype(BF16)
    return sl, q_b, k_e, q_m, k_m


def _gla_fwd(z_qk, z_v, z_zg, wg, bg, t_rows, name, comm=None):
    nc = t_rows // CHUNK
    ng = nc // GLA_G
    rows = CHUNK * GLA_G

    def kern(qf, kf, vf, zf, qb, kb, vb, zb, wg_ref, bg_ref, of_ref, ob_ref, sf_ref, sb_ref, st_f, st_b):
        i = pl.program_id(0)

        @pl.when(i == 0)
        def _():
            st_f[...] = jnp.zeros_like(st_f)
            st_b[...] = jnp.zeros_like(st_b)

        wg_v, bg_v = wg_ref[...], bg_ref[...]
        for rev, (q_r, k_r, v_r, z_r, o_r, s_r, st) in ((False, (qf, kf, vf, zf, of_ref, sf_ref, st_f)),
                                                        (True, (qb, kb, vb, zb, ob_ref, sb_ref, st_b))):
            mask = _gla_mask(rev)
            for ck in (range(GLA_G - 1, -1, -1) if rev else range(GLA_G)):
                cr = slice(ck * CHUNK, (ck + 1) * CHUNK)
                tm = _gla_chunk_terms(q_r[cr, :], k_r[cr, :], z_r[cr, :], wg_v, bg_v, rev)
                v = v_r[cr, :]
                for hd in range(GLA_H):
                    sl, q_b, k_e, q_m, k_m = _gla_head_ops(tm, hd)
                    vs = slice(hd * GLA_DV, (hd + 1) * GLA_DV)
                    v16 = v[:, vs].astype(BF16)
                    a = jnp.where(mask, _dot(q_m, k_m, NT), 0.0).astype(BF16)
                    s_in = st[hd]
                    s_r[ck, hd] = s_in
                    o_r[cr, vs] = _dot(a, v16) + _dot(q_b, s_in.astype(BF16), NT)
                    st[hd] = s_in * tm["e_last"][:, sl] + _dot(v16, k_e, TN)

    fw = lambda c: (lambda i: (i, c))
    rv = lambda c: (lambda i: (ng - 1 - i, c))
    def side(ix):
        return [pl.BlockSpec((rows, QK), ix(0)), pl.BlockSpec((rows, QK), ix(1)),
                pl.BlockSpec((rows, GLA_W), ix(0)), pl.BlockSpec((rows, LANES), ix(0))]
    st_shape = (nc, GLA_H, GLA_DV, GLA_DK)
    res, comm_res = _pcall(
        kern, name, (ng,),
        side(fw) + side(rv) + [pl.BlockSpec(wg.shape, lambda i: (0, 0)), pl.BlockSpec(bg.shape, lambda i: (0, 0))],
        [pl.BlockSpec((rows, GLA_W), fw(0)), pl.BlockSpec((rows, GLA_W), rv(0)),
         pl.BlockSpec((GLA_G,) + st_shape[1:], lambda i: (i, 0, 0, 0)),
         pl.BlockSpec((GLA_G,) + st_shape[1:], lambda i: (ng - 1 - i, 0, 0, 0))],
        [jax.ShapeDtypeStruct((t_rows, GLA_W), F32)] * 2 + [jax.ShapeDtypeStruct(st_shape, F32)] * 2,
        [pltpu.VMEM(st_shape[1:], F32)] * 2, ("arbitrary",),
        (z_qk, z_qk, z_v, z_zg, z_qk, z_qk, z_v, z_zg, wg, bg), comm)
    return res if comm is None else (res, comm_res)


def _gla_bwd(z_qk, z_v, z_zg, wg, bg, do, s_f, s_b, t_rows, name, comm=None):
    nc = t_rows // CHUNK

    def kern(qf, kf, vf, zf, dof, ssf, qb, kb, vb, zb, dob, ssb, wg_ref, bg_ref,
             dqf, dkf, dvf, dpf, dqb, dkb, dvb, dpb, ds_f, ds_b):
        i = pl.program_id(0)

        @pl.when(i == 0)
        def _():
            ds_f[...] = jnp.zeros_like(ds_f)
            ds_b[...] = jnp.zeros_like(ds_b)

        wg_v, bg_v = wg_ref[...], bg_ref[...]
        sides = ((False, (qf, kf, vf, zf, dof, ssf, dqf, dkf, dvf, dpf, ds_f)),
                 (True, (qb, kb, vb, zb, dob, ssb, dqb, dkb, dvb, dpb, ds_b)))
        for rev, (q_r, k_r, v_r, z_r, do_r, ss_r, dq_r, dk_r, dv_r, dp_r, ds) in sides:
            mask = _gla_mask(rev)
            for ck in (range(GLA_G) if rev else range(GLA_G - 1, -1, -1)):
                cr = slice(ck * CHUNK, (ck + 1) * CHUNK)
                tm = _gla_chunk_terms(q_r[cr, :], k_r[cr, :], z_r[cr, :], wg_v, bg_v, rev)
                v = v_r[cr, :]
                d_o = do_r[cr, :]
                db_parts, cross_parts = [], []
                for hd in range(GLA_H):
                    sl, q_b, k_e, q_m, k_m = _gla_head_ops(tm, hd)
                    vs = slice(hd * GLA_DV, (hd + 1) * GLA_DV)
                    v16, do16 = v[:, vs].astype(BF16), d_o[:, vs].astype(BF16)
                    a = jnp.where(mask, _dot(q_m, k_m, NT), 0.0).astype(BF16)
                    da = jnp.where(mask, _dot(do16, v16, NT), 0.0).astype(BF16)
                    s_in = ss_r[ck, hd]
                    s_in16 = s_in.astype(BF16)
                    ds_out = ds[hd]
                    ds16 = ds_out.astype(BF16)
                    dv_r[cr, vs] = _dot(a, do16, TN) + _dot(k_e, ds16, NT)
                    dqs = _dot(da, k_m) * tm["eqm"][:, sl] + _dot(do16, s_in16) * tm["eb"][:, sl]
                    dk = _dot(da, q_m, TN) * tm["ekm"][:, sl] + _dot(v16, ds16) * tm["ek"][:, sl]
                    ds[hd] = ds_out * tm["e_last"][:, sl] + _dot(do16, q_b, TN)
                    dq_r[cr, sl] = dqs
                    dk_r[cr, sl] = dk
                    db_parts.append(tm["qs"][:, sl] * dqs - tm["k"][:, sl] * dk)
                    s_out = s_in * tm["e_last"][:, sl] + _dot(v16, k_e, TN)
                    cross_parts.append(jnp.sum(s_out * ds_out, axis=0, keepdims=True))
                d_b = jnp.concatenate(db_parts, axis=1)
                dg = _cumsum_rows(d_b, not rev) + jnp.concatenate(cross_parts, axis=1)
                dp_r[cr, :] = dg * (1.0 / GLA_GATE_NORM) * _sigmoid(-tm["pre"])

    ng = nc // GLA_G
    rows = CHUNK * GLA_G
    fw = lambda c: (lambda i: (ng - 1 - i, c))
    rv = lambda c: (lambda i: (i, c))
    st_blk = (GLA_G, GLA_H, GLA_DV, GLA_DK)
    def side(ix, st_ix):
        return [pl.BlockSpec((rows, QK), ix(0)), pl.BlockSpec((rows, QK), ix(1)),
                pl.BlockSpec((rows, GLA_W), ix(0)), pl.BlockSpec((rows, LANES), ix(0)),
                pl.BlockSpec((rows, GLA_W), ix(0)), pl.BlockSpec(st_blk, st_ix)]
    def oside(ix):
        return [pl.BlockSpec((rows, QK), ix(0)), pl.BlockSpec((rows, QK), ix(0)),
                pl.BlockSpec((rows, GLA_W), ix(0)), pl.BlockSpec((rows, QK), ix(0))]
    o_shapes = [jax.ShapeDtypeStruct((t_rows, QK), F32), jax.ShapeDtypeStruct((t_rows, QK), F32),
                jax.ShapeDtypeStruct((t_rows, GLA_W), F32), jax.ShapeDtypeStruct((t_rows, QK), F32)]
    res, comm_res = _pcall(
        kern, name, (ng,),
        (side(fw, lambda i: (ng - 1 - i, 0, 0, 0)) + side(rv, lambda i: (i, 0, 0, 0))
         + [pl.BlockSpec(wg.shape, lambda i: (0, 0)), pl.BlockSpec(bg.shape, lambda i: (0, 0))]),
        oside(fw) + oside(rv), o_shapes * 2, [pltpu.VMEM((GLA_H, GLA_DV, GLA_DK), F32)] * 2, ("arbitrary",),
        (z_qk, z_qk, z_v, z_zg, do, s_f, z_qk, z_qk, z_v, z_zg, do, s_b, wg, bg), comm)
    return res if comm is None else (res, comm_res)


FLAT_W = 1024


def _adam_math(wv, gv, mv, vv):
    bc1 = 1.0 - ADAM_B1 ** ADAM_STEP
    bc2 = 1.0 - ADAM_B2 ** ADAM_STEP
    m_new = ADAM_B1 * mv + (1.0 - ADAM_B1) * gv
    v_new = ADAM_B2 * vv + (1.0 - ADAM_B2) * (gv * gv)
    delta = -ADAM_LR * ((m_new / bc1) / (jnp.sqrt(v_new / bc2) + ADAM_EPS) + ADAM_WD * wv)
    return delta, m_new, v_new


def _row_tile(rows, cap=256):
    t = cap
    while rows % t:
        t //= 2
    assert t >= SUBLANES, rows
    return t


def _adamw_layers(w, m, v, g_layers, name):
    depth, rows, cols = w.shape
    tr = _row_tile(rows)

    def kern(w_ref, m_ref, v_ref, *rest):
        g_refs, (go_ref, d_ref, mo_ref, vo_ref) = rest[:depth], rest[depth:]
        l = pl.program_id(0)
        gv = g_refs[0][...]
        for k in range(1, depth):
            gv = jnp.where(l == k, g_refs[k][...], gv)
        delta, m_new, v_new = _adam_math(w_ref[...], gv, m_ref[...], v_ref[...])
        go_ref[...] = gv
        d_ref[...] = delta
        mo_ref[...] = m_new
        vo_ref[...] = v_new

    stacked = pl.BlockSpec((None, tr, cols), lambda l, i: (l, i, 0))
    return pl.pallas_call(
        kern, name=name, grid=(depth, rows // tr),
        in_specs=[stacked] * 3 + [pl.BlockSpec((tr, cols), lambda l, i: (i, 0))] * depth,
        out_specs=[stacked] * 4, out_shape=[jax.ShapeDtypeStruct(w.shape, F32)] * 4,
        compiler_params=_cparams(("arbitrary", "arbitrary")),
    )(w, m, v, *g_layers)


def _adamw_flat(w, g, m, v):
    rows = w.shape[0]

    def body(i, ins, consts, outs, accs):
        delta, m_new, v_new = _adam_math(*[r[...] for r in ins])
        outs[0][...] = delta
        outs[1][...] = m_new
        outs[2][...] = v_new

    w_, g_, m_, v_ = w, g, m, v
    return _rows(body, rows, _row_tile(rows), [(w_, FLAT_W, 0), (g_, FLAT_W, 0), (m_, FLAT_W, 0), (v_, FLAT_W, 0)],
                 [], [(FLAT_W, F32)] * 3, [], "adamw_small")


def _add_sibling(g, got, c_idx, name):
    _, _, rows, cols = g.shape
    tr = _row_tile(rows)

    def kern(c_ref, g_ref, got_ref, o_ref):
        o_ref[...] = (g_ref[...] + got_ref[...]).astype(BF16)

    grid_spec = pltpu.PrefetchScalarGridSpec(
        num_scalar_prefetch=1, grid=(N_CHIPS, rows // tr),
        in_specs=[pl.BlockSpec((None, None, tr, cols), lambda q, i, c_ref: (q, c_ref[0], i, 0)),
                  pl.BlockSpec((None, tr, cols), lambda q, i, c_ref: (q, i, 0))],
        out_specs=pl.BlockSpec((None, tr, cols), lambda q, i, c_ref: (q, i, 0)))
    return pl.pallas_call(
        kern, name=name, grid_spec=grid_spec, out_shape=jax.ShapeDtypeStruct((N_CHIPS, rows, cols), BF16),
        compiler_params=_cparams(("arbitrary", "arbitrary")),
    )(c_idx, g, got)


def _sum_chips(x, name):
    _, rows, cols = x.shape
    tr = _row_tile(rows)

    def kern(x_ref, o_ref):
        xv = x_ref[...].astype(F32)
        o_ref[...] = ((xv[0] + xv[1]) + xv[2]) + xv[3]

    return pl.pallas_call(
        kern, name=name, grid=(rows // tr,),
        in_specs=[pl.BlockSpec((N_CHIPS, tr, cols), lambda i: (0, i, 0))],
        out_specs=pl.BlockSpec((tr, cols), lambda i: (i, 0)),
        out_shape=jax.ShapeDtypeStruct((rows, cols), F32),
        compiler_params=_cparams(("arbitrary",)),
    )(x)


def _place():
    x, y, c = lax.axis_index("x"), lax.axis_index("y"), lax.axis_index("c")
    return x, y, c


def _other_chips(x, y):
    return [(1 - x, y), (x, 1 - y), (1 - x, 1 - y)]


class _Comm:
    def __init__(self, ins, out_shapes, n_sems, stage, body):
        self.ins, self.out_shapes, self.body = list(ins), list(out_shapes), body
        self.scratch = [pltpu.SemaphoreType.DMA((n,)) for n in n_sems] + [pltpu.VMEM(s, d) for s, d in stage]


ANY_SPEC = pl.BlockSpec(memory_space=pl.ANY)


def _run_comm(comm, name):
    def kern(*refs):
        comm.body(refs, True, True)

    return pl.pallas_call(
        kern, name=name, in_specs=[ANY_SPEC] * len(comm.ins), out_specs=[ANY_SPEC] * len(comm.out_shapes),
        out_shape=comm.out_shapes, scratch_shapes=comm.scratch,
        compiler_params=pltpu.CompilerParams(has_side_effects=True, vmem_limit_bytes=VMEM_LIMIT),
    )(*comm.ins)


def _pcall(kern, name, grid, in_specs, out_specs, out_shape, scratch_shapes, sem, args, comm=None):
    if comm is None:
        return pl.pallas_call(kern, name=name, grid=grid, in_specs=in_specs, out_specs=out_specs, out_shape=out_shape,
                              scratch_shapes=scratch_shapes, compiler_params=_cparams(sem))(*args), None
    n_in, n_out, n_scr = len(in_specs), len(out_specs), len(scratch_shapes)
    c_in, c_out = len(comm.ins), len(comm.out_shapes)

    def hosted(*refs):
        a = n_in + c_in
        b = a + n_out + c_out
        main = refs[:n_in] + refs[a:a + n_out] + refs[b:b + n_scr]
        extra = refs[n_in:a] + refs[a + n_out:b] + refs[b + n_scr:]
        ids = [pl.program_id(d) for d in range(len(grid))]
        first, last = ids[0] == 0, ids[0] == grid[0] - 1
        for i, g in zip(ids[1:], grid[1:]):
            first, last = jnp.logical_and(first, i == 0), jnp.logical_and(last, i == g - 1)

        @pl.when(first)
        def _():
            comm.body(extra, True, False)

        kern(*main)

        @pl.when(last)
        def _():
            comm.body(extra, False, True)

    res = pl.pallas_call(
        hosted, name=name, grid=grid, in_specs=list(in_specs) + [ANY_SPEC] * c_in,
        out_specs=list(out_specs) + [ANY_SPEC] * c_out, out_shape=list(out_shape) + comm.out_shapes,
        scratch_shapes=list(scratch_shapes) + comm.scratch,
        compiler_params=pltpu.CompilerParams(dimension_semantics=("arbitrary",) * len(grid),
                                             vmem_limit_bytes=VMEM_LIMIT, has_side_effects=True),
    )(*args, *comm.ins)
    return res[:n_out], res[n_out:]


class _StagedCopies:
    def __init__(self, pairs, bufs, sem_in, sem_out):
        self.ins = [pltpu.make_async_copy(src, buf, sem_in.at[i]) for i, ((src, _), buf) in enumerate(zip(pairs, bufs))]
        self.outs = [pltpu.make_async_copy(buf, dst, sem_out.at[i]) for i, ((_, dst), buf) in enumerate(zip(pairs, bufs))]

    def load(self):
        for cp in self.ins:
            cp.start()

    def store(self):
        for cp in self.ins:
            cp.wait()
        for cp in self.outs:
            cp.start()

    def finish(self):
        for cp in self.outs:
            cp.wait()


DMA_CHUNK_BYTES = 1 << 20
DMA_MAX_CHUNKS = 4


def _row_chunks(rows, row_bytes, align=16):
    units = rows // align
    k = max(1, min(DMA_MAX_CHUNKS, -(-rows * row_bytes // DMA_CHUNK_BYTES), units))
    out, start = [], 0
    for i in range(k):
        size = (units // k + (1 if i < units % k else 0)) * align
        out.append((start, size))
        start += size
    if start < rows:
        out[-1] = (out[-1][0], out[-1][1] + rows - start)
    return out


def _row_bytes(shape, dtype):
    return math.prod(shape[1:]) * jnp.dtype(dtype).itemsize


def _remote(src, dst, send, recv, idx, dev):
    return pltpu.make_async_remote_copy(src, dst, send.at[idx], recv.at[idx], device_id=dev, device_id_type=MESH_ID)


def _allgather_chips(xs):
    na = len(xs)
    chunks = [_row_chunks(x.shape[1], _row_bytes(x.shape[1:], x.dtype)) for x in xs]
    n_cp = 3 * sum(len(ch) for ch in chunks)

    def body(refs, start, finish):
        x_refs, o_refs = refs[:na], refs[na:2 * na]
        send1, recv1, send2, recv2, loc_in, loc_out = refs[2 * na:2 * na + 6]
        px, py, c = _place()
        me = 2 * px + py
        sib = (px, py, 1 - c)
        own = _StagedCopies([(x, o.at[me]) for x, o in zip(x_refs, o_refs)], refs[2 * na + 6:], loc_in, loc_out)
        plan = [(a, qx, qy, pl.ds(s, n)) for a in range(na) for (qx, qy) in _other_chips(px, py) for (s, n) in chunks[a]]
        first = [_remote(x_refs[a].at[c, rows], o_refs[a].at[me, c, rows], send1, recv1, i, (qx, qy, c))
                 for i, (a, qx, qy, rows) in enumerate(plan)]
        if start:
            own.load()
            for cp in first:
                cp.start()
        if finish:
            own.store()
            passed = []
            for i, (a, qx, qy, rows) in enumerate(plan):
                slot = o_refs[a].at[2 * qx + qy, c, rows]
                _remote(slot, slot, send1, recv1, i, (qx, qy, c)).wait_recv()
                fwd = _remote(slot, slot, send2, recv2, i, sib)
                fwd.start()
                passed.append(fwd)
            for i, (a, qx, qy, rows) in enumerate(plan):
                slot = o_refs[a].at[2 * qx + qy, 1 - c, rows]
                _remote(slot, slot, send2, recv2, i, sib).wait_recv()
            for cp in first + passed:
                cp.wait_send()
            own.finish()

    outs = [jax.ShapeDtypeStruct((N_CHIPS,) + x.shape, x.dtype) for x in xs]
    return _Comm(xs, outs, [n_cp, n_cp, n_cp, n_cp, na, na], [(x.shape, x.dtype) for x in xs], body)


def _sibling_halves(gs):
    na = len(gs)
    chunks = [_row_chunks(g.shape[2], _row_bytes(g.shape[2:], g.dtype)) for g in gs]
    n_cp = N_CHIPS * sum(len(ch) for ch in chunks)

    def body(refs, start, finish):
        g_refs, o_refs = refs[:na], refs[na:2 * na]
        send, recv = refs[2 * na:]
        px, py, c = _place()
        plan = [(a, q, pl.ds(s, n)) for a in range(na) for q in range(N_CHIPS) for (s, n) in chunks[a]]
        cps = [_remote(g_refs[a].at[q, 1 - c, rows], o_refs[a].at[q, rows], send, recv, i, (px, py, 1 - c))
               for i, (a, q, rows) in enumerate(plan)]
        if start:
            for cp in cps:
                cp.start()
        if finish:
            for cp in cps:
                cp.wait()

    outs = [jax.ShapeDtypeStruct((N_CHIPS,) + g.shape[2:], g.dtype) for g in gs]
    return _Comm(gs, outs, [n_cp, n_cp], [], body)


def _chip_exchange(ps):
    na = len(ps)
    chunks = [_row_chunks(p.shape[1], _row_bytes(p.shape[1:], p.dtype)) for p in ps]
    n_cp = 3 * sum(len(ch) for ch in chunks)

    def body(refs, start, finish):
        p_refs, o_refs = refs[:na], refs[na:2 * na]
        send, recv, loc_in, loc_out = refs[2 * na:2 * na + 4]
        px, py, c = _place()
        me = 2 * px + py
        own = _StagedCopies([(p.at[me], o.at[me]) for p, o in zip(p_refs, o_refs)], refs[2 * na + 4:], loc_in, loc_out)
        plan = [(a, qx, qy, pl.ds(s, n)) for a in range(na) for (qx, qy) in _other_chips(px, py) for (s, n) in chunks[a]]
        cps = [_remote(p_refs[a].at[2 * qx + qy, rows], o_refs[a].at[me, rows], send, recv, i, (qx, qy, c))
               for i, (a, qx, qy, rows) in enumerate(plan)]
        if start:
            own.load()
            for cp in cps:
                cp.start()
        if finish:
            own.store()
            for cp in cps:
                cp.wait()
            own.finish()

    outs = [jax.ShapeDtypeStruct(p.shape, p.dtype) for p in ps]
    return _Comm(ps, outs, [n_cp, n_cp, na, na], [(p.shape[1:], p.dtype) for p in ps], body)


def _sibling_join(rs):
    na = len(rs)
    chunks = [_row_chunks(r.shape[0], _row_bytes(r.shape, r.dtype)) for r in rs]
    n_cp = sum(len(ch) for ch in chunks)

    def body(refs, start, finish):
        r_refs, o_refs = refs[:na], refs[na:2 * na]
        send, recv, loc_in, loc_out = refs[2 * na:2 * na + 4]
        px, py, c = _place()
        own = _StagedCopies([(r, o.at[c]) for r, o in zip(r_refs, o_refs)], refs[2 * na + 4:], loc_in, loc_out)
        plan = [(a, pl.ds(s, n)) for a in range(na) for (s, n) in chunks[a]]
        cps = [_remote(r_refs[a].at[rows], o_refs[a].at[c, rows], send, recv, i, (px, py, 1 - c))
               for i, (a, rows) in enumerate(plan)]
        if start:
            own.load()
            for cp in cps:
                cp.start()
        if finish:
            own.store()
            for cp in cps:
                cp.wait()
            own.finish()

    outs = [jax.ShapeDtypeStruct((2,) + r.shape, r.dtype) for r in rs]
    return _Comm(rs, outs, [n_cp, n_cp, na, na], [(r.shape, r.dtype) for r in rs], body)


def _reduce_stages(gs, c_idx, tag):
    got = yield _sibling_halves(gs)
    part = [_add_sibling(g, o, c_idx, f"reduce_add_sibling{tag}_{a}") for a, (g, o) in enumerate(zip(gs, got))]
    landed = yield _chip_exchange(part)
    mine = [_sum_chips(x, f"reduce_sum_chips{tag}_{a}") for a, x in enumerate(landed)]
    return (yield _sibling_join(mine))


class _Staged:
    def __init__(self, gen, tag):
        self.gen, self.tag, self.k, self.value = gen, tag, 0, None
        self.cur = next(gen)

    def stage(self):
        return self.cur

    def done(self, results):
        self.k += 1
        try:
            self.cur = self.gen.send(results)
        except StopIteration as stop:
            self.cur, self.value = None, stop.value

    def drain(self):
        while self.cur is not None:
            self.done(_run_comm(self.cur, f"comm{self.tag}_{self.k}"))
        return self.value


WEIGHTS = ["meta_tokens", "norm_mix_pre", "norm_mix_post", "norm_mlp_pre", "norm_mlp_post", "w_in", "conv_w",
           "conv_b", "lru_wa_f", "lru_ba_f", "lru_wx_f", "lru_bx_f", "lru_lambda_f", "lru_wa_b", "lru_ba_b",
           "lru_wx_b", "lru_bx_b", "lru_lambda_b", "gla_wg_f", "gla_bg_f", "gla_wg_b", "gla_bg_b", "gla_head_norm",
           "w_out", "w_mlp_up", "w_mlp_down"]
BIG = ("w_in", "w_out", "w_mlp_up", "w_mlp_down")
SMALL = [n for n in WEIGHTS if n not in BIG]
SMALL_SHARDED = {"meta_tokens": 1, "conv_w": 2, "gla_wg_f": 2, "gla_wg_b": 2}
N_CHIPS = 4
SMALL_RAW = [("g1", (1, D_MODEL)), ("g2", (1, D_MODEL)), ("g3", (1, D_MODEL)), ("g4", (1, D_MODEL)),
             ("conv_w", (4, LRU_W)), ("conv_b", (1, LRU_W)), ("lru_wg", (4, LANES, 4 * LANES)),
             ("lru_pb", (SUBLANES, LRU_W)), ("gla_wg", (2 * GLA_RANK, 2 * QK)), ("gla_bg", (1, 2 * QK)),
             ("head_norm", (1, GLA_W))]
META_SHAPE = (N_META, D_MODEL)


def _pad_to(v, n):
    return jnp.pad(v, (0, n - v.shape[0]))


def _round_up(n, m):
    return -(-n // m) * m


def _flat_cat(arrs, total):
    return _pad_to(jnp.concatenate([a.reshape(-1) for a in arrs]), total)


def _split_flat(flat, shapes):
    out, off = [], 0
    for s in shapes:
        n = math.prod(s)
        out.append(flat[off:off + n].reshape(s))
        off += n
    return out


def _my_shard(full, axis, chip):
    n = full.shape[axis] // N_CHIPS
    return lax.dynamic_slice_in_dim(full, chip * n, n, axis=axis)


def _block_diag_gates(wa_f, wx_f, wa_b, wx_b):
    def bd(w):
        z = jnp.zeros((LRU_HD, LRU_HD), w.dtype)
        return jnp.stack([jnp.block([[w[2 * j], z], [z, w[2 * j + 1]]]) for j in range(4)])
    return jnp.concatenate([bd(wa_f), bd(wx_f), bd(wa_b), bd(wx_b)], axis=2).astype(BF16)


def _gate_diag_blocks(dwg, o):
    blk = dwg[:, :, o * LANES:(o + 1) * LANES]
    return jnp.stack([blk[j, hh * LRU_HD:(hh + 1) * LRU_HD, hh * LRU_HD:(hh + 1) * LRU_HD]
                      for j in range(4) for hh in range(2)])


def kernel(x, meta_tokens, norm_mix_pre, norm_mix_post, norm_mlp_pre, norm_mlp_post, w_in, conv_w, conv_b, lru_wa_f, lru_ba_f, lru_wx_f, lru_bx_f, lru_lambda_f, lru_wa_b, lru_ba_b, lru_wx_b, lru_bx_b, lru_lambda_b, gla_wg_f, gla_bg_f, gla_wg_b, gla_bg_b, gla_head_norm, w_out, w_mlp_up, w_mlp_down, loss_target, m_meta_tokens, m_norm_mix_pre, m_norm_mix_post, m_norm_mlp_pre, m_norm_mlp_post, m_w_in, m_conv_w, m_conv_b, m_lru_wa_f, m_lru_ba_f, m_lru_wx_f, m_lru_bx_f, m_lru_lambda_f, m_lru_wa_b, m_lru_ba_b, m_lru_wx_b, m_lru_bx_b, m_lru_lambda_b, m_gla_wg_f, m_gla_bg_f, m_gla_wg_b, m_gla_bg_b, m_gla_head_norm, m_w_out, m_w_mlp_up, m_w_mlp_down, v_meta_tokens, v_norm_mix_pre, v_norm_mix_post, v_norm_mlp_pre, v_norm_mlp_post, v_w_in, v_conv_w, v_conv_b, v_lru_wa_f, v_lru_ba_f, v_lru_wx_f, v_lru_bx_f, v_lru_lambda_f, v_lru_wa_b, v_lru_ba_b, v_lru_wx_b, v_lru_bx_b, v_lru_lambda_b, v_gla_wg_f, v_gla_bg_f, v_gla_wg_b, v_gla_bg_b, v_gla_head_norm, v_w_out, v_w_mlp_up, v_w_mlp_down):
    args = dict(locals())
    w_loc = {n: args[n] for n in WEIGHTS}
    m_loc = {n: args["m_" + n] for n in WEIGHTS}
    v_loc = {n: args["v_" + n] for n in WEIGHTS}
    seq = x.shape[1]
    t_rows = FRONT + seq + BACK
    assert t_rows % 768 == 0 and seq % FRONT == 0, seq
    chip = 2 * lax.axis_index("x") + lax.axis_index("y")
    c_idx = lax.axis_index("c").astype(jnp.int32).reshape(1)

    def halves(a):
        return a.reshape((2, a.shape[0] // 2) + a.shape[1:])

    big16 = {n: w_loc[n].astype(BF16) for n in BIG}
    sm_names = list(SMALL_SHARDED)
    sm_len = _round_up(sum(w_loc[n].size for n in sm_names), 2 * SUBLANES * FLAT_W)
    sm_pack = _flat_cat([w_loc[n] for n in sm_names], sm_len).reshape(2, -1, FLAT_W)
    g_in0, g_small = _run_comm(_allgather_chips([halves(big16["w_in"][0]), sm_pack]), "gather_first")
    sm_parts = [_split_flat(g_small[q].reshape(-1), [w_loc[n].shape for n in sm_names]) for q in range(N_CHIPS)]
    full = {n: jnp.concatenate([sm_parts[q][i] for q in range(N_CHIPS)], axis=SMALL_SHARDED[n])
            for i, n in enumerate(sm_names)}
    for n in SMALL:
        if n not in SMALL_SHARDED:
            full[n] = w_loc[n]

    def full_w_in(g):
        w = jnp.transpose(g.reshape(N_CHIPS, D_MODEL, -1), (1, 0, 2)).reshape(D_MODEL, D_IN)
        return jnp.pad(w, ((0, 0), (0, D_IN_PAD - D_IN)))

    def full_rest(g_out, g_up, g_down):
        return dict(w_out=g_out.reshape(-1, D_MODEL),
                    w_up=jnp.transpose(g_up.reshape(N_CHIPS, D_MODEL, -1), (1, 0, 2)).reshape(D_MODEL, D_FF),
                    w_down=g_down.reshape(D_FF, D_MODEL))

    later_gathers = [_allgather_chips([halves(big16[n][0]) for n in BIG[1:]]),
                     _allgather_chips([halves(big16[n][l]) for l in range(1, DEPTH) for n in BIG])]

    meta_blk = jnp.concatenate([jnp.zeros((FRONT - N_META, D_MODEL), F32), full["meta_tokens"]], axis=0)
    h = jnp.concatenate([meta_blk, x[0], jnp.zeros((BACK, D_MODEL), F32)], axis=0)
    target = loss_target[0]

    layer_w = []
    for l in range(DEPTH):
        wg = jnp.zeros((LANES, 2 * QK), F32)
        wg = wg.at[0:GLA_RANK, 0:QK].set(full["gla_wg_f"][l]).at[GLA_RANK:2 * GLA_RANK, QK:].set(full["gla_wg_b"][l])
        pb = jnp.stack([full["lru_ba_f"][l], full["lru_bx_f"][l], full["lru_ba_b"][l], full["lru_bx_b"][l],
                        full["lru_lambda_f"][l], full["lru_lambda_b"][l], jnp.zeros((LRU_W,), F32),
                        jnp.zeros((LRU_W,), F32)])
        layer_w.append(dict(
            g1=full["norm_mix_pre"][l][None], g2=full["norm_mix_post"][l][None],
            g3=full["norm_mlp_pre"][l][None], g4=full["norm_mlp_post"][l][None],
            conv_w=full["conv_w"][l], conv_b=full["conv_b"][l][None],
            lru_wg=_block_diag_gates(full["lru_wa_f"][l], full["lru_wx_f"][l], full["lru_wa_b"][l], full["lru_wx_b"][l]),
            lru_pb=pb, gla_wg=wg.astype(BF16),
            gla_bg=jnp.concatenate([full["gla_bg_f"][l], full["gla_bg_b"][l]])[None],
            head_norm=full["gla_head_norm"][l][None]))
    layer_w[0]["w_in"] = full_w_in(g_in0)

    def split_z(acc):
        return (acc[:, 0:1024], acc[:, 1024:1536], acc[:, 1536:2048], acc[:, 2048:2560], acc[:, ZG_OFF:ZG_OFF + LANES])

    def relu2(acc):
        r = jnp.maximum(acc, 0.0)
        return acc, r * r

    tm_e = 384
    tm_ff = 1408 if t_rows % 1408 == 0 else 768

    def post_mix(acc, hh, g2, g3):
        h1 = hh + _rms_fwd(acc, g2)
        return acc, h1, _rms_fwd(h1, g3)

    def post_mlp(acc, h1, g4, g1_next):
        h2 = h1 + _rms_fwd(acc, g4)
        return acc, h2, _rms_fwd(h2, g1_next)

    def post_mlp_loss(acc, h1, tgt, x_rows, g4):
        h2 = h1 + _rms_fwd(acc, g4)
        err = (h2 - tgt) * x_rows[:, 0:1]
        d_h = err * (1.0 / D_MODEL)
        dff, dg4 = _rms_bwd(acc, g4, d_h)
        loss_part = jnp.zeros((SUBLANES, LANES), F32) + jnp.sum(err * err) * (0.5 / D_MODEL)
        return d_h, dff, loss_part, dg4

    row3 = [(D_MODEL, F32), (D_MODEL, F32), (D_MODEL, BF16)]
    vec = ((1, D_MODEL), F32)
    target_p = jnp.concatenate([jnp.zeros((FRONT, D_MODEL), F32), target, jnp.zeros((BACK, D_MODEL), F32)], axis=0)
    x_rows = jnp.concatenate([jnp.zeros((FRONT, LANES), F32), jnp.ones((seq, LANES), F32), jnp.zeros((BACK, LANES), F32)])
    n_sub = 1
    saved = []
    n1 = _norm_cast(h, layer_w[0]["g1"], t_rows, "norm_mix_pre_l0")
    for l in range(DEPTH):
        lw = layer_w[l]
        tag = f"_l{l}"
        z_lru, z_qk, z_v, z_go, z_zg = _mm(
            n1, lw["w_in"], "nn", 768, D_IN_PAD, D_MODEL,
            [(1024, F32), (512, F32), (512, F32), (512, F32), (LANES, F32)], "in_proj" + tag, epilogue=split_z)
        lru_args = (z_lru, lw["conv_w"], lw["conv_b"], lw["lru_wg"], lw["lru_pb"], t_rows, seq, "lru_fwd" + tag)
        gla_args = (z_qk, z_v, z_zg, lw["gla_wg"], lw["gla_bg"], t_rows, "gla_fwd" + tag)
        if l == 0:
            (h_f, h_b), g_rest = _lru_fwd(*lru_args, comm=later_gathers[0])
            lw.update(full_rest(*g_rest))
            (o_f, o_b, s_f, s_b), g_next = _gla_fwd(*gla_args, comm=later_gathers[1])
            for l2 in range(1, DEPTH):
                g_l2 = g_next[(l2 - 1) * len(BIG):l2 * len(BIG)]
                layer_w[l2].update(full_rest(*g_l2[1:]), w_in=full_w_in(g_l2[0]))
        else:
            h_f, h_b = _lru_fwd(*lru_args)
            o_f, o_b, s_f, s_b = _gla_fwd(*gla_args)
        ymix = _mix_fwd(h_f, h_b, z_lru, o_f, o_b, z_go, lw["head_norm"], t_rows, "mix_fwd" + tag)
        mix, h1, n3 = _mm(ymix, lw["w_out"], "nn", 768, D_MODEL, D_MODEL, row3, "out_proj" + tag, epilogue=post_mix,
                          rows_in=(h,), consts=(lw["g2"], lw["g3"]), row_split=n_sub)
        u, act = _mm(n3, lw["w_up"], "nn", tm_ff, 1024, D_MODEL, [(D_FF, BF16), (D_FF, BF16)], "mlp_up" + tag,
                     epilogue=relu2)
        sv = dict(h=h, n1=n1, z_lru=z_lru, z_qk=z_qk, z_v=z_v, z_go=z_go, z_zg=z_zg, h_f=h_f, h_b=h_b,
                  o_f=o_f, o_b=o_b, s_f=s_f, s_b=s_b, ymix=ymix, mix=mix, h1=h1, n3=n3, u=u, act=act)
        if l + 1 < DEPTH:
            sv["ff"], h, n1 = _mm(act, lw["w_down"], "nn", tm_e, D_MODEL, D_FF, row3, "mlp_down" + tag,
                                  epilogue=post_mlp, rows_in=(h1,), consts=(lw["g4"], layer_w[l + 1]["g1"]),
                                  row_split=n_sub)
        else:
            dh, dff, loss_part, dg4 = _mm(act, lw["w_down"], "nn", tm_e, D_MODEL, D_FF,
                                          [(D_MODEL, F32), (D_MODEL, BF16)], "mlp_down_loss" + tag,
                                          epilogue=post_mlp_loss, rows_in=(h1, target_p, x_rows), consts=(lw["g4"],),
                                          accs=[((SUBLANES, LANES), F32), vec], row_split=n_sub)
        saved.append(sv)

    loss = lax.psum(loss_part[0, 0], ("x", "y", "c"))

    small_len = _round_up(sum(math.prod(s) for _, s in SMALL_RAW) + math.prod(META_SHAPE),
                          N_CHIPS * 2 * 2 * SUBLANES * FLAT_W)
    totals = [None] * DEPTH
    pend = None

    def with_stage(fn):
        comm = pend.stage() if pend is not None else None
        if comm is None:
            return fn(None)
        res, comm_res = fn(comm)
        pend.done(comm_res)
        return res

    for l in reversed(range(DEPTH)):
        lw, sv = layer_w[l], saved[l]
        tag = f"_l{l}"
        raw = {"g4": dg4}
        gw_down = with_stage(lambda comm: _mm(sv["act"], dff, "tn", 1024, D_MODEL, 768, [(D_MODEL, F32)],
                                              "dw_down" + tag, comm=comm))[0]
        du = _mm(dff, lw["w_down"], "nt", tm_ff, 1024, D_MODEL, [(D_FF, BF16)], "d_act" + tag,
                 epilogue=lambda acc, uu: (acc * 2.0 * jnp.maximum(uu.astype(F32), 0.0),), extras=(sv["u"],))[0]
        gw_up = _mm(sv["n3"], du, "tn", D_MODEL, D_FF // N_CHIPS, 768, [(D_FF, F32)], "dw_up" + tag,
                    col_blocks_first=True)[0]

        def pre_mlp_bwd(acc, h1, d_h, mix, g3, g2):
            dx, dg3 = _rms_bwd(h1, g3, acc)
            d_h1 = d_h + dx
            d_mix, dg2 = _rms_bwd(mix, g2, d_h1)
            return d_h1, d_mix, dg3, dg2

        dh1, dmix, raw["g3"], raw["g2"] = _mm(
            du, lw["w_up"], "nt", tm_e, D_MODEL, D_FF, [(D_MODEL, F32), (D_MODEL, BF16)], "d_n3" + tag,
            epilogue=pre_mlp_bwd, rows_in=(sv["h1"], dh, sv["mix"]), consts=(lw["g3"], lw["g2"]), accs=[vec, vec],
            row_split=n_sub)
        gw_out = _mm(sv["ymix"], dmix, "tn", D_MODEL, D_MODEL, 768, [(D_MODEL, F32)], "dw_out" + tag)[0]
        dymix = _mm(dmix, lw["w_out"], "nt", 768, D_MODEL, D_MODEL, [(D_MODEL, F32)], "d_ymix" + tag)[0]
        dh_lru, dgate, do, dgo, raw["head_norm"] = _mix_bwd(
            dymix, sv["h_f"], sv["h_b"], sv["z_lru"], sv["o_f"], sv["o_b"], sv["z_go"], lw["head_norm"], t_rows,
            "mix_bwd" + tag)
        gla_grads = with_stage(lambda comm: _gla_bwd(sv["z_qk"], sv["z_v"], sv["z_zg"], lw["gla_wg"], lw["gla_bg"], do,
                                                     sv["s_f"], sv["s_b"], t_rows, "gla_bwd" + tag, comm=comm))
        dx_br, raw["conv_w"], raw["conv_b"], raw["lru_wg"], raw["lru_pb"] = _lru_bwd(
            sv["z_lru"], lw["conv_w"], lw["conv_b"], lw["lru_wg"], lw["lru_pb"], sv["h_f"], sv["h_b"], dh_lru,
            t_rows, seq, "lru_bwd" + tag)
        dz, dwg_gla, raw["gla_bg"] = _dz_assemble(dx_br, dgate, gla_grads, dgo, sv["z_zg"], lw["gla_wg"], t_rows,
                                                  "dz_assemble" + tag)
        raw["gla_wg"] = dwg_gla[0:2 * GLA_RANK]
        gw_in = with_stage(lambda comm: _mm(sv["n1"], dz, "tn", D_MODEL, 896, 768, [(D_IN_PAD, F32)], "dw_in" + tag,
                                            comm=comm))[0]
        if l > 0:
            def pre_mix_bwd(acc, hh, d_h1, ff_prev, g1, g4_prev):
                dx, dg1 = _rms_bwd(hh, g1, acc)
                d_h = d_h1 + dx
                d_ff, dg4_prev = _rms_bwd(ff_prev, g4_prev, d_h)
                return d_h, d_ff, dg1, dg4_prev

            dh, dff, raw["g1"], dg4 = _mm(
                dz, lw["w_in"], "nt", tm_e, D_MODEL, D_IN_PAD, [(D_MODEL, F32), (D_MODEL, BF16)], "d_n1" + tag,
                epilogue=pre_mix_bwd, rows_in=(sv["h"], dh1, saved[l - 1]["ff"]),
                consts=(lw["g1"], layer_w[l - 1]["g4"]), accs=[vec, vec], row_split=n_sub)
        else:
            def pre_mix_bwd0(acc, hh, d_h1, g1):
                dx, dg1 = _rms_bwd(hh, g1, acc)
                return d_h1 + dx, dg1

            dh, raw["g1"] = _mm(dz, lw["w_in"], "nt", tm_e, D_MODEL, D_IN_PAD, [(D_MODEL, F32)], "d_n1" + tag,
                                epilogue=pre_mix_bwd0, rows_in=(sv["h"], dh1), consts=(lw["g1"],), accs=[vec],
                                row_split=n_sub)

        cols = D_IN // N_CHIPS
        in_sh = jnp.stack([gw_in[:, q * cols:(q + 1) * cols] for q in range(N_CHIPS)])
        meta_g = dh[FRONT - N_META:FRONT] if l == 0 else jnp.zeros(META_SHAPE, F32)
        small = _flat_cat([raw[n] for n, _ in SMALL_RAW] + [meta_g], small_len)
        pack = [in_sh.reshape(N_CHIPS, 2, D_MODEL // 2, cols),
                gw_out.reshape(N_CHIPS, 2, -1, D_MODEL),
                gw_up.reshape(N_CHIPS, 2, D_MODEL // 2, D_FF // N_CHIPS),
                gw_down.reshape(N_CHIPS, 2, -1, D_MODEL),
                small.reshape(N_CHIPS, 2, -1, FLAT_W)]
        if pend is not None:
            totals[l + 1] = pend.drain()
        pend = _Staged(_reduce_stages(pack, c_idx, tag), tag)
    totals[0] = pend.drain()

    grad_x = dh[FRONT:FRONT + seq][None]

    sm_all = _run_comm(_allgather_chips([totals[l][4] for l in range(DEPTH)]), "gather_small_grads")
    g_tot = {}
    per_layer = []
    for l in range(DEPTH):
        pieces = _split_flat(sm_all[l].reshape(-1), [s for _, s in SMALL_RAW] + [META_SHAPE])
        per_layer.append(dict(zip([n for n, _ in SMALL_RAW] + ["meta"], pieces)))
    stack = lambda f: jnp.stack([f(per_layer[l]) for l in range(DEPTH)])
    g_tot["meta_tokens"] = _my_shard(per_layer[0]["meta"], 1, chip)
    for n, key in (("norm_mix_pre", "g1"), ("norm_mix_post", "g2"), ("norm_mlp_pre", "g3"), ("norm_mlp_post", "g4"),
                   ("conv_b", "conv_b"), ("gla_head_norm", "head_norm")):
        g_tot[n] = stack(lambda d, key=key: d[key][0])
    g_tot["conv_w"] = _my_shard(stack(lambda d: d["conv_w"]), 2, chip)
    for o, n in enumerate(("lru_wa_f", "lru_wx_f", "lru_wa_b", "lru_wx_b")):
        g_tot[n] = stack(lambda d, o=o: _gate_diag_blocks(d["lru_wg"], o))
    for row, n in enumerate(("lru_ba_f", "lru_bx_f", "lru_ba_b", "lru_bx_b", "lru_lambda_f", "lru_lambda_b")):
        g_tot[n] = stack(lambda d, row=row: d["lru_pb"][row])
    g_tot["gla_wg_f"] = _my_shard(stack(lambda d: d["gla_wg"][0:GLA_RANK, 0:QK]), 2, chip)
    g_tot["gla_wg_b"] = _my_shard(stack(lambda d: d["gla_wg"][GLA_RANK:, QK:]), 2, chip)
    g_tot["gla_bg_f"] = stack(lambda d: d["gla_bg"][0, 0:QK])
    g_tot["gla_bg_b"] = stack(lambda d: d["gla_bg"][0, QK:])

    delta, new_m, new_v = {}, {}, {}
    for a, n in enumerate(BIG):
        shp = w_loc[n].shape
        flat3 = lambda t, shp=shp: t.reshape(DEPTH, -1, shp[-1])
        g_l = [totals[l][a].reshape(-1, shp[-1]) for l in range(DEPTH)]
        outs = _adamw_layers(flat3(w_loc[n]), flat3(m_loc[n]), flat3(v_loc[n]), g_l, "adamw_" + n)
        g_tot[n], delta[n], new_m[n], new_v[n] = [o.reshape(shp) for o in outs]
    a_rows = _round_up(sum(w_loc[n].size for n in SMALL), SUBLANES * FLAT_W) // FLAT_W
    pack_small = lambda d: _flat_cat([d[n] for n in SMALL], a_rows * FLAT_W).reshape(a_rows, FLAT_W)
    outs = _adamw_flat(pack_small(w_loc), pack_small(g_tot), pack_small(m_loc), pack_small(v_loc))
    shapes = [w_loc[n].shape for n in SMALL]
    for d, o in zip((delta, new_m, new_v), outs):
        d.update(zip(SMALL, _split_flat(o.reshape(-1), shapes)))
    return (loss, grad_x, *[g_tot[n] for n in WEIGHTS], *[delta[n] for n in WEIGHTS],
            *[new_m[n] for n in WEIGHTS], *[new_v[n] for n in WEIGHTS])
```

```python
import math

import jax
import jax.numpy as jnp
from jax import lax
from jax.experimental import pallas as pl
from jax.experimental.pallas import tpu as pltpu

F32 = jnp.float32
BF16 = jnp.bfloat16

D_MODEL = 1024
DEPTH = 2
N_META = 16
LRU_W = 512
LRU_HEADS = 8
LRU_HD = 64
LRU_C = 8.0
GLA_W = 512
GLA_H = 4
GLA_DK = 64
GLA_DV = 128
GLA_RANK = 16
GLA_GATE_NORM = 16.0
D_FF = 4096
D_IN = 2592
EPS = 1e-6
ADAM_LR, ADAM_B1, ADAM_B2, ADAM_EPS, ADAM_WD, ADAM_STEP = 0.001, 0.9, 0.999, 1e-08, 0.01, 10

LANES = 128
SUBLANES = 8
FRONT = 128
BACK = 128
D_IN_PAD = 2688
ZG_OFF = 2560
CHUNK = 64
EXP_CLAMP = 80.0
VMEM_LIMIT = 56 * 1024 * 1024
MESH_ID = pl.DeviceIdType.MESH

NN = (((1,), (0,)), ((), ()))
NT = (((1,), (1,)), ((), ()))
TN = (((0,), (0,)), ((), ()))


def _dot(a, b, dims=NN):
    return lax.dot_general(a, b, dims, preferred_element_type=F32)


def _cparams(sem):
    return pltpu.CompilerParams(dimension_semantics=sem, vmem_limit_bytes=VMEM_LIMIT)


def _sigmoid(x):
    return 1.0 / (1.0 + jnp.exp(-x))


def _gelu(x):
    c = math.sqrt(2.0 / math.pi)
    return 0.5 * x * (1.0 + jnp.tanh(c * (x + 0.044715 * x * x * x)))


def _gelu_grad(x):
    c = math.sqrt(2.0 / math.pi)
    t = jnp.tanh(c * (x + 0.044715 * x * x * x))
    return 0.5 * (1.0 + t) + 0.5 * x * (1.0 - t * t) * c * (1.0 + 3.0 * 0.044715 * x * x)


def _rms_fwd(x, g):
    rstd = lax.rsqrt(jnp.mean(x * x, axis=1, keepdims=True) + EPS)
    return (x * rstd) * g


def _rms_bwd(x, g, dy):
    rstd = lax.rsqrt(jnp.mean(x * x, axis=1, keepdims=True) + EPS)
    xh = x * rstd
    dxh = dy * g
    dx = rstd * (dxh - xh * jnp.mean(dxh * xh, axis=1, keepdims=True))
    dg = jnp.sum(dy * xh, axis=0, keepdims=True)
    return dx, dg


def _mm(a, b, mode, tm, tn, tk, outs, name, epilogue=None, extras=(), col_blocks_first=False,
        rows_in=(), consts=(), accs=(), comm=None, row_split=1):
    if mode == "nn":
        (m, k), n = a.shape, b.shape[1]
        a_spec = pl.BlockSpec((tm, tk), lambda i, j, kk: (i, kk))
        b_spec = pl.BlockSpec((tk, tn), lambda i, j, kk: (kk, j))
        dims = NN
    elif mode == "nt":
        (m, k), n = a.shape, b.shape[0]
        a_spec = pl.BlockSpec((tm, tk), lambda i, j, kk: (i, kk))
        b_spec = pl.BlockSpec((tn, tk), lambda i, j, kk: (j, kk))
        dims = NT
    else:
        (k, m), n = a.shape, b.shape[1]
        a_spec = pl.BlockSpec((tk, tm), lambda i, j, kk: (kk, i))
        b_spec = pl.BlockSpec((tk, tn), lambda i, j, kk: (kk, j))
        dims = TN
    assert m % tm == 0 and n % tn == 0 and k % tk == 0, (name, m, n, k)
    nk = k // tk
    if n == tn and nk == 1:
        b_spec = pl.BlockSpec(b_spec.block_shape, b_spec.index_map, pipeline_mode=pl.Buffered(1))
    ne = len(extras) + len(rows_in) + len(consts)
    e_specs = [pl.BlockSpec((tm, tn), lambda i, j, kk: (i, j)) for _ in extras]
    e_specs += [pl.BlockSpec((tm, r.shape[1]), lambda i, j, kk: (i, 0)) for r in rows_in]
    e_specs += [pl.BlockSpec(c.shape, lambda i, j, kk: (0, 0)) for c in consts]
    n_out = len(outs)
    o_specs, o_shapes = [], []
    for width, dtype in outs:
        if col_blocks_first:
            assert width == n, name
            o_specs.append(pl.BlockSpec((None, tm, tn), lambda i, j, kk: (j, i, 0)))
            o_shapes.append(jax.ShapeDtypeStruct((n // tn, m, tn), dtype))
            continue
        if width == n:
            o_specs.append(pl.BlockSpec((tm, tn), lambda i, j, kk: (i, j)))
        else:
            assert n == tn, name
            o_specs.append(pl.BlockSpec((tm, width), lambda i, j, kk: (i, 0)))
        o_shapes.append(jax.ShapeDtypeStruct((m, width), dtype))
    o_specs += [pl.BlockSpec(s, lambda i, j, kk: (0, 0)) for s, _ in accs]
    o_shapes += [jax.ShapeDtypeStruct(s, d) for s, d in accs]

    n_tiled = len(extras) + len(rows_in)

    def finish(acc, extra_refs, out_refs, rs=slice(None), zero_accs=True):
        tiles = [e[rs, :] for e in extra_refs[:n_tiled]] + [e[...] for e in extra_refs[n_tiled:]]
        res = epilogue(acc, *tiles) if epilogue else (acc,)
        for o, r in zip(out_refs[:n_out], res[:n_out]):
            o[rs, :] = r.astype(o.dtype)
        first = jnp.logical_and(pl.program_id(0) == 0, pl.program_id(1) == 0)
        for o, r in zip(out_refs[n_out:], res[n_out:]):
            if zero_accs:
                @pl.when(first)
                def _(o=o):
                    o[...] = jnp.zeros_like(o)

            o[...] += r

    if nk == 1 and row_split > 1:
        assert mode != "tn" and tm % row_split == 0, name
        sub = tm // row_split

        def body(a_ref, b_ref, *rest):
            for s in range(row_split):
                rs = slice(s * sub, (s + 1) * sub)
                finish(_dot(a_ref[rs, :], b_ref[...], dims), rest[:ne], rest[ne:], rs, zero_accs=(s == 0))
        scratch = []
    elif nk == 1:
        def body(a_ref, b_ref, *rest):
            finish(_dot(a_ref[...], b_ref[...], dims), rest[:ne], rest[ne:])
        scratch = []
    else:
        def body(a_ref, b_ref, *rest):
            acc_ref = rest[-1]
            kk = pl.program_id(2)

            @pl.when(kk == 0)
            def _():
                acc_ref[...] = jnp.zeros_like(acc_ref)

            acc_ref[...] += _dot(a_ref[...], b_ref[...], dims)

            @pl.when(kk == nk - 1)
            def _():
                finish(acc_ref[...], rest[:ne], rest[ne:-1])
        scratch = [pltpu.VMEM((tm, tn), F32)]

    sem = ("arbitrary", "arbitrary", "arbitrary") if accs else ("parallel", "parallel", "arbitrary")
    res, comm_res = _pcall(body, name, (m // tm, n // tn, nk), [a_spec, b_spec] + e_specs, o_specs, o_shapes, scratch,
                           sem, (a, b, *extras, *rows_in, *consts), comm)
    return res if comm is None else (res, comm_res)


def _rows(body, n_rows, tm, ins, consts, outs, accs, name):
    assert n_rows % tm == 0, (name, n_rows, tm)
    in_specs = [pl.BlockSpec((tm, w), (lambda i, cb=cb: (i, cb))) for _, w, cb in ins]
    in_specs += [pl.BlockSpec(c.shape, lambda i: (0, 0)) for c in consts]
    out_specs = [pl.BlockSpec((tm, w), lambda i: (i, 0)) for w, _ in outs]
    out_specs += [pl.BlockSpec(s, lambda i: (0, 0)) for s, _ in accs]
    out_shape = [jax.ShapeDtypeStruct((n_rows, w), d) for w, d in outs]
    out_shape += [jax.ShapeDtypeStruct(s, d) for s, d in accs]
    ni, nc, no = len(ins), len(consts), len(outs)

    def kern(*refs):
        body(pl.program_id(0), refs[:ni], refs[ni:ni + nc], refs[ni + nc:ni + nc + no], refs[ni + nc + no:])

    res = pl.pallas_call(
        kern, name=name, grid=(n_rows // tm,), in_specs=in_specs, out_specs=out_specs, out_shape=out_shape,
        compiler_params=_cparams(("arbitrary",)),
    )(*[a for a, _, _ in ins], *consts)
    return res


def _acc_add(i, ref, val):
    @pl.when(i == 0)
    def _():
        ref[...] = jnp.zeros_like(ref)

    ref[...] += val


def _norm_cast(h, g, t_rows, name):
    def body(i, ins, consts, outs, accs):
        outs[0][...] = _rms_fwd(ins[0][...], consts[0][...]).astype(BF16)
    return _rows(body, t_rows, 384, [(h, D_MODEL, 0)], [g], [(D_MODEL, BF16)], [], name)[0]


def _head_norm_parts(o):
    ns, rs = [], []
    for hd in range(GLA_H):
        oh = o[:, hd * GLA_DV:(hd + 1) * GLA_DV]
        r = lax.rsqrt(jnp.mean(oh * oh, axis=1, keepdims=True) + EPS)
        ns.append(oh * r)
        rs.append(r)
    return ns, rs


def _mix_fwd(h_f, h_b, z_lru, o_f, o_b, z_go, head_norm, t_rows, name):
    def body(i, ins, consts, outs, accs):
        hf, hb, gate, of, ob, go = [r[...] for r in ins]
        hn = consts[0][...]
        outs[0][:, 0:LRU_W] = ((hf + hb) * _gelu(gate)).astype(BF16)
        ns, _ = _head_norm_parts(of + ob)
        silu = go * _sigmoid(go)
        for hd in range(GLA_H):
            sl = slice(hd * GLA_DV, (hd + 1) * GLA_DV)
            outs[0][:, LRU_W + hd * GLA_DV:LRU_W + (hd + 1) * GLA_DV] = (ns[hd] * hn[:, sl] * silu[:, sl]).astype(BF16)
    ins = [(h_f, LRU_W, 0), (h_b, LRU_W, 0), (z_lru, LRU_W, 1), (o_f, GLA_W, 0), (o_b, GLA_W, 0), (z_go, GLA_W, 0)]
    return _rows(body, t_rows, 384, ins, [head_norm], [(D_MODEL, BF16)], [], name)[0]


def _mix_bwd(dymix, h_f, h_b, z_lru, o_f, o_b, z_go, head_norm, t_rows, name):
    def body(i, ins, consts, outs, accs):
        dyl, dyg, hf, hb, gate, of, ob, go = [r[...] for r in ins]
        hn = consts[0][...]
        outs[0][...] = dyl * _gelu(gate)
        outs[1][...] = dyl * (hf + hb) * _gelu_grad(gate)
        ns, rs = _head_norm_parts(of + ob)
        sg = _sigmoid(go)
        silu = go * sg
        dsilu = sg * (1.0 + go * (1.0 - sg))
        dhn = []
        for hd in range(GLA_H):
            sl = slice(hd * GLA_DV, (hd + 1) * GLA_DV)
            dy = dyg[:, sl]
            e = dy * hn[:, sl] * silu[:, sl]
            outs[2][:, sl] = rs[hd] * (e - ns[hd] * jnp.mean(e * ns[hd], axis=1, keepdims=True))
            outs[3][:, sl] = dy * ns[hd] * hn[:, sl] * dsilu[:, sl]
            dhn.append(jnp.sum(dy * ns[hd] * silu[:, sl], axis=0, keepdims=True))
        _acc_add(i, accs[0], jnp.concatenate(dhn, axis=1))
    ins = [(dymix, LRU_W, 0), (dymix, GLA_W, 1), (h_f, LRU_W, 0), (h_b, LRU_W, 0), (z_lru, LRU_W, 1),
           (o_f, GLA_W, 0), (o_b, GLA_W, 0), (z_go, GLA_W, 0)]
    return _rows(body, t_rows, 384, ins, [head_norm],
                 [(LRU_W, F32), (LRU_W, F32), (GLA_W, F32), (GLA_W, F32)], [((1, GLA_W), F32)], name)


def _dz_assemble(dx_br, dgate, gla_grads, dgo, z_zg, wg, t_rows, name):
    dq_f, dk_f, dv_f, dpre_f, dq_b, dk_b, dv_b, dpre_b = gla_grads
    qk = GLA_H * GLA_DK

    def body(i, ins, consts, outs, accs):
        (dxb, dgt, dqf, dqb, dkf, dkb, dvf, dvb, dg_o, dpf, dpb, zg) = [r[...] for r in ins]
        w = consts[0][...]
        o = outs[0]
        o[:, 0:LRU_W] = dxb.astype(BF16)
        o[:, LRU_W:2 * LRU_W] = dgt.astype(BF16)
        o[:, 1024:1024 + qk] = ((dqf + dqb) * (GLA_DK ** -0.5)).astype(BF16)
        o[:, 1280:1280 + qk] = (dkf + dkb).astype(BF16)
        o[:, 1536:1536 + GLA_W] = (dvf + dvb).astype(BF16)
        o[:, 2048:2048 + GLA_W] = dg_o.astype(BF16)
        dpre = jnp.concatenate([dpf, dpb], axis=1)
        dpre16 = dpre.astype(BF16)
        o[:, ZG_OFF:ZG_OFF + LANES] = _dot(dpre16, w, NT).astype(BF16)
        _acc_add(i, accs[0], _dot(zg.astype(BF16), dpre16, TN))
        _acc_add(i, accs[1], jnp.sum(dpre, axis=0, keepdims=True))

    ins = [(dx_br, LRU_W, 0), (dgate, LRU_W, 0), (dq_f, qk, 0), (dq_b, qk, 0), (dk_f, qk, 0), (dk_b, qk, 0),
           (dv_f, GLA_W, 0), (dv_b, GLA_W, 0), (dgo, GLA_W, 0), (dpre_f, qk, 0), (dpre_b, qk, 0), (z_zg, LANES, 0)]
    return _rows(body, t_rows, 384, ins, [wg], [(D_IN_PAD, BF16)],
                 [((LANES, 2 * qk), F32), ((1, 2 * qk), F32)], name)


LRU_RB = 64
HALO = SUBLANES


def _scan8(a, u, rev):
    rows = lax.broadcasted_iota(jnp.int32, a.shape, 0)
    for d in (1, 2, 4):
        if rev:
            a_s, u_s, ok = pltpu.roll(a, SUBLANES - d, 0), pltpu.roll(u, SUBLANES - d, 0), rows < SUBLANES - d
        else:
            a_s, u_s, ok = pltpu.roll(a, d, 0), pltpu.roll(u, d, 0), rows >= d
        u = jnp.where(ok, a * u_s + u, u)
        a = jnp.where(ok, a * a_s, a)
    return a, u


def _edge_row(x, rev):
    r = x[0:1, :] if rev else x[SUBLANES - 1:SUBLANES, :]
    return jnp.broadcast_to(r, x.shape)


def _scan_block(a, u, carry, rev):
    nsub = a.shape[0] // SUBLANES
    loc = [_scan8(a[sb * SUBLANES:(sb + 1) * SUBLANES], u[sb * SUBLANES:(sb + 1) * SUBLANES], rev) for sb in range(nsub)]
    edges = [(_edge_row(a_c, rev), _edge_row(h_l, rev)) for a_c, h_l in loc]
    carries = [None] * nsub
    for sb in (range(nsub - 1, -1, -1) if rev else range(nsub)):
        carries[sb] = carry
        carry = edges[sb][0] * carry + edges[sb][1]
    h = jnp.concatenate([h_l + a_c * cin for (a_c, h_l), cin in zip(loc, carries)], axis=0)
    return h, carry


def _lru_gates(xc, pre_a, pre_x, ba, bx, sp8):
    r = _sigmoid(pre_a + ba)
    i = _sigmoid(pre_x + bx)
    log_a = -(r * sp8)
    a = jnp.exp(log_a)
    y = 2.0 * log_a
    series = -y * (1.0 + y * (0.5 + y * (1.0 / 6.0 + y * (1.0 / 24.0 + y * (1.0 / 120.0)))))
    om = jnp.where(y > -0.25, series, 1.0 - a * a)
    s = jnp.sqrt(om)
    return r, i, a, s


def _softplus(x):
    return jnp.maximum(x, 0.0) + jnp.log(1.0 + jnp.exp(-jnp.abs(x)))


def _conv_taps(xpad_ref, r0, nrows, shifts=(-2, -1, 0, 1)):
    ext = xpad_ref[pl.ds(r0, nrows + 2 * HALO), :]
    taps = []
    for s in shifts:
        sh = ext if s == 0 else pltpu.roll(ext, (-s) % (nrows + 2 * HALO), 0)
        taps.append(sh[HALO:HALO + nrows])
    return taps


def _row_mask(r0, nrows, seq):
    rows = r0 + lax.broadcasted_iota(jnp.int32, (nrows, LANES), 0)
    return jnp.logical_and(rows >= FRONT - N_META, rows < FRONT + seq).astype(F32)


def _lru_load_conv(z_hbm, xpad, xc, sem, cw_ref, cb_ref, t_rows, seq):
    j = pl.program_id(0)
    zeros = jnp.zeros((HALO, LANES), F32)
    xpad[0:HALO, :] = zeros
    xpad[t_rows + HALO:t_rows + 2 * HALO, :] = zeros
    cp = pltpu.make_async_copy(z_hbm.at[:, pl.ds(pl.multiple_of(j * LANES, LANES), LANES)],
                               xpad.at[pl.ds(HALO, t_rows)], sem)
    cp.start()
    cp.wait()
    cw = cw_ref[...]
    cb = cb_ref[...]

    def conv_blk(r, carry):
        r0 = pl.multiple_of(r * LRU_RB, LRU_RB)
        taps = _conv_taps(xpad, r0, LRU_RB)
        acc = cb + cw[0:1, :] * taps[0]
        for k in range(1, 4):
            acc = acc + cw[k:k + 1, :] * taps[k]
        xc[pl.ds(r0, LRU_RB), :] = acc * _row_mask(r0, LRU_RB, seq)
        return carry

    lax.fori_loop(0, t_rows // LRU_RB, conv_blk, 0)


def _lru_specs(t_rows):
    return [pl.BlockSpec(memory_space=pl.ANY),
            pl.BlockSpec((4, LANES), lambda j: (0, j)),
            pl.BlockSpec((1, LANES), lambda j: (0, j)),
            pl.BlockSpec((1, LANES, 4 * LANES), lambda j: (j, 0, 0)),
            pl.BlockSpec((SUBLANES, LANES), lambda j: (0, j))]


def _lru_fwd(z_lru, conv_w, conv_b, wg, pb, t_rows, seq, name, comm=None):
    nblk = t_rows // LRU_RB
    nsub = LRU_RB // SUBLANES

    def kern(z_hbm, cw_ref, cb_ref, wg_ref, pb_ref, hf_ref, hb_ref, xpad, xc, sem):
        _lru_load_conv(z_hbm, xpad, xc, sem, cw_ref, cb_ref, t_rows, seq)
        w = wg_ref[0]
        pb_v = pb_ref[...]
        dirs = []
        for rev in (False, True):
            o = 2 if rev else 0
            dirs.append(dict(rev=rev, ba=pb_v[o:o + 1, :], bx=pb_v[o + 1:o + 2, :],
                             sp8=LRU_C * _softplus(-pb_v[5:6, :] if rev else -pb_v[4:5, :]),
                             w=w[:, o * LANES:(o + 2) * LANES], out=hb_ref if rev else hf_ref))

        def blk(it, carries):
            new = []
            for d, carry in zip(dirs, carries):
                r = (nblk - 1 - it) if d["rev"] else it
                r0 = pl.multiple_of(r * LRU_RB, LRU_RB)
                xcb = xc[pl.ds(r0, LRU_RB), :]
                pre = _dot(xcb.astype(BF16), d["w"])
                _, gi, a, s = _lru_gates(xcb, pre[:, 0:LANES], pre[:, LANES:2 * LANES], d["ba"], d["bx"], d["sp8"])
                h, carry = _scan_block(a, s * (gi * xcb), carry, d["rev"])
                d["out"][pl.ds(r0, LRU_RB), :] = h
                new.append(carry)
            return tuple(new)

        z8 = jnp.zeros((SUBLANES, LANES), F32)
        lax.fori_loop(0, nblk, blk, (z8, z8), unroll=2)

    res, comm_res = _pcall(
        kern, name, (LRU_W // LANES,), _lru_specs(t_rows), [pl.BlockSpec((t_rows, LANES), lambda j: (0, j))] * 2,
        [jax.ShapeDtypeStruct((t_rows, LRU_W), F32)] * 2,
        [pltpu.VMEM((t_rows + 2 * HALO, LANES), F32), pltpu.VMEM((t_rows, LANES), F32), pltpu.SemaphoreType.DMA],
        ("arbitrary",), (z_lru, conv_w, conv_b, wg, pb), comm)
    return res if comm is None else (res, comm_res)


def _lru_bwd(z_lru, conv_w, conv_b, wg, pb, h_f, h_b, dh, t_rows, seq, name):
    nblk = t_rows // LRU_RB
    nsub = LRU_RB // SUBLANES

    def kern(z_hbm, cw_ref, cb_ref, wg_ref, pb_ref, hf_ref, hb_ref, dh_ref,
             dx_ref, dcw_ref, dcb_ref, dwh_ref, dpb_ref, xpad, xc, dxc, dwg, sem):
        _lru_load_conv(z_hbm, xpad, xc, sem, cw_ref, cb_ref, t_rows, seq)
        w = wg_ref[0]
        pb_v = pb_ref[...]
        zeros = jnp.zeros((HALO, LANES), F32)
        dxc[0:HALO, :] = zeros
        dxc[t_rows + HALO:t_rows + 2 * HALO, :] = zeros
        dwg[...] = jnp.zeros_like(dwg)

        def zero_blk(r, carry):
            dxc[pl.ds(pl.multiple_of(r * LRU_RB, LRU_RB) + HALO, LRU_RB), :] = jnp.zeros((LRU_RB, LANES), F32)
            return carry

        lax.fori_loop(0, nblk, zero_blk, 0)
        dirs = []
        for rev in (False, True):
            o = 2 if rev else 0
            lam = pb_v[5:6, :] if rev else pb_v[4:5, :]
            dirs.append(dict(rev=rev, o=o, ba=pb_v[o:o + 1, :], bx=pb_v[o + 1:o + 2, :], sp8=LRU_C * _softplus(-lam),
                             dlam_scale=LRU_C * _sigmoid(-lam), h_ref=hb_ref if rev else hf_ref,
                             w=w[:, o * LANES:(o + 2) * LANES]))
        rows_b = lax.broadcasted_iota(jnp.int32, (LRU_RB, LANES), 0)

        def blk(it, carries):
            new = []
            for d, (nu_c, acc_ba, acc_bx, acc_lam) in zip(dirs, carries):
                rev, o, h_ref = d["rev"], d["o"], d["h_ref"]
                adj_rev = not rev
                r = (nblk - 1 - it) if adj_rev else it
                r0 = pl.multiple_of(r * LRU_RB, LRU_RB)
                xcb = xc[pl.ds(r0, LRU_RB), :]
                xc16 = xcb.astype(BF16)
                pre = _dot(xc16, d["w"])
                gr, gi, a, s = _lru_gates(xcb, pre[:, 0:LANES], pre[:, LANES:2 * LANES], d["ba"], d["bx"], d["sp8"])
                dhb = dh_ref[pl.ds(r0, LRU_RB), :]
                hb = h_ref[pl.ds(r0, LRU_RB), :]
                if rev:
                    nxt0 = pl.multiple_of(jnp.minimum(r0 + LRU_RB, t_rows - SUBLANES), SUBLANES)
                    edge = h_ref[pl.ds(nxt0, SUBLANES), :][0:1, :] * (r < nblk - 1).astype(F32)
                    h_prev = jnp.concatenate([hb[1:], edge], axis=0)
                else:
                    prv0 = pl.multiple_of(jnp.maximum(r0 - SUBLANES, 0), SUBLANES)
                    edge = h_ref[pl.ds(prv0, SUBLANES), :][SUBLANES - 1:SUBLANES, :] * (r > 0).astype(F32)
                    h_prev = jnp.concatenate([edge, hb[:-1]], axis=0)
                nu, nu_out = _scan_block(a, a * dhb, nu_c, adj_rev)
                cin = jnp.broadcast_to(nu_c[0:1, :], (LRU_RB, LANES))
                if adj_rev:
                    nu_next = jnp.where(rows_b == LRU_RB - 1, cin, pltpu.roll(nu, LRU_RB - 1, 0))
                else:
                    nu_next = jnp.where(rows_b == 0, cin, pltpu.roll(nu, 1, 0))
                du = dhb + nu_next
                da = du * h_prev
                ix = gi * xcb
                dlog = da * a - du * ix * (a * a) / s
                d_i = du * s * xcb
                dr = -dlog * d["sp8"]
                dpa = dr * gr * (1.0 - gr)
                dpx = d_i * gi * (1.0 - gi)
                dp16 = jnp.concatenate([dpa, dpx], axis=1).astype(BF16)
                dxc[pl.ds(r0 + HALO, LRU_RB), :] += du * s * gi + _dot(dp16, d["w"], NT)
                dwg[:, o * LANES:(o + 2) * LANES] += _dot(xc16, dp16, TN)
                new.append((nu_out, acc_ba + jnp.sum(dpa, axis=0, keepdims=True),
                            acc_bx + jnp.sum(dpx, axis=0, keepdims=True),
                            acc_lam + jnp.sum(dlog * gr, axis=0, keepdims=True)))
            return tuple(new)

        z1 = jnp.zeros((1, LANES), F32)
        init = (jnp.zeros((SUBLANES, LANES), F32), z1, z1, z1)
        (_, ba_f, bx_f, lam_f), (_, ba_b, bx_b, lam_b) = lax.fori_loop(0, nblk, blk, (init, init), unroll=2)
        dpb_ref[...] = jnp.concatenate([ba_f, bx_f, ba_b, bx_b, lam_f * dirs[0]["dlam_scale"],
                                        lam_b * dirs[1]["dlam_scale"], z1, z1], axis=0)
        for hh in range(2):
            for o in range(4):
                c0 = o * LANES + hh * LRU_HD
                dwh_ref[hh, :, o * LRU_HD:(o + 1) * LRU_HD] = dwg[hh * LRU_HD:(hh + 1) * LRU_HD, c0:c0 + LRU_HD]

        def mask_blk(r, carry):
            r0 = pl.multiple_of(r * LRU_RB, LRU_RB)
            dxc[pl.ds(r0 + HALO, LRU_RB), :] = dxc[pl.ds(r0 + HALO, LRU_RB), :] * _row_mask(r0, LRU_RB, seq)
            return carry

        lax.fori_loop(0, nblk, mask_blk, 0)

        cw = cw_ref[...]

        def conv_bwd(r, carry):
            r0 = pl.multiple_of(r * LRU_RB, LRU_RB)
            gt = _conv_taps(dxc, r0, LRU_RB, (2, 1, 0, -1))
            xt = _conv_taps(xpad, r0, LRU_RB)
            dx_ref[pl.ds(r0, LRU_RB), :] = (cw[0:1, :] * gt[0] + cw[1:2, :] * gt[1] + cw[2:3, :] * gt[2]
                                           + cw[3:4, :] * gt[3])
            g = gt[2]
            new = [carry[k] + jnp.sum(g * xt[k], axis=0, keepdims=True) for k in range(4)]
            new.append(carry[4] + jnp.sum(g, axis=0, keepdims=True))
            return tuple(new)

        z1 = jnp.zeros((1, LANES), F32)
        sums = lax.fori_loop(0, nblk, conv_bwd, (z1, z1, z1, z1, z1))
        dcw_ref[...] = jnp.concatenate(sums[:4], axis=0)
        dcb_ref[...] = sums[4]

    col = lambda j: (0, j)
    return pl.pallas_call(
        kern, name=name, grid=(LRU_W // LANES,),
        in_specs=_lru_specs(t_rows) + [pl.BlockSpec((t_rows, LANES), col)] * 3,
        out_specs=[pl.BlockSpec((t_rows, LANES), col), pl.BlockSpec((4, LANES), col), pl.BlockSpec((1, LANES), col),
                   pl.BlockSpec((2, LRU_HD, 4 * LRU_HD), lambda j: (j, 0, 0)), pl.BlockSpec((SUBLANES, LANES), col)],
        out_shape=[jax.ShapeDtypeStruct((t_rows, LRU_W), F32), jax.ShapeDtypeStruct((4, LRU_W), F32),
                   jax.ShapeDtypeStruct((1, LRU_W), F32), jax.ShapeDtypeStruct((LRU_HEADS, LRU_HD, 4 * LRU_HD), F32),
                   jax.ShapeDtypeStruct((SUBLANES, LRU_W), F32)],
        scratch_shapes=[pltpu.VMEM((t_rows + 2 * HALO, LANES), F32), pltpu.VMEM((t_rows, LANES), F32),
                        pltpu.VMEM((t_rows + 2 * HALO, LANES), F32), pltpu.VMEM((LANES, 4 * LANES), F32),
                        pltpu.SemaphoreType.DMA],
        compiler_params=_cparams(("arbitrary",)),
    )(z_lru, conv_w, conv_b, wg, pb, h_f, h_b, dh)


QK = GLA_H * GLA_DK
GLA_G = 4


def _cumsum_rows(x, rev):
    n = x.shape[0]
    rows = lax.broadcasted_iota(jnp.int32, x.shape, 0)
    d = 1
    while d < n:
        if rev:
            x = x + jnp.where(rows < n - d, pltpu.roll(x, n - d, 0), 0.0)
        else:
            x = x + jnp.where(rows >= d, pltpu.roll(x, d, 0), 0.0)
        d *= 2
    return x


def _gla_chunk_terms(q, k, zg, wg, bg, rev):
    pre = (_dot(zg.astype(BF16), wg) + bg)[:, (QK if rev else 0):(2 * QK if rev else QK)]
    g = (jnp.minimum(pre, 0.0) - jnp.log(1.0 + jnp.exp(-jnp.abs(pre)))) * (1.0 / GLA_GATE_NORM)
    b = _cumsum_rows(g, rev)
    b_last = b[0:1, :] if rev else b[CHUNK - 1:CHUNK, :]
    b_mid = b[CHUNK // 2:CHUNK // 2 + 1, :]
    qs = q * (GLA_DK ** -0.5)
    terms = dict(
        pre=pre, qs=qs, k=k,
        eb=jnp.exp(b), ek=jnp.exp(b_last - b), e_last=jnp.exp(b_last),
        eqm=jnp.exp(jnp.minimum(b - b_mid, EXP_CLAMP)), ekm=jnp.exp(jnp.minimum(b_mid - b, EXP_CLAMP)))
    return terms


def _gla_mask(rev):
    t = lax.broadcasted_iota(jnp.int32, (CHUNK, CHUNK), 0)
    s = lax.broadcasted_iota(jnp.int32, (CHUNK, CHUNK), 1)
    return (s > t) if rev else (s <= t)


def _gla_head_ops(tm, hd):
    sl = slice(hd * GLA_DK, (hd + 1) * GLA_DK)
    q_b = (tm["qs"][:, sl] * tm["eb"][:, sl]).astype(BF16)
    k_e = (tm["k"][:, sl] * tm["ek"][:, sl]).astype(BF16)
    q_m = (tm["qs"][:, sl] * tm["eqm"][:, sl]).astype(BF16)
    k_m = (tm["k"][:, sl] * tm["ekm"][:, sl]).astype(BF16)
    return sl, q_b, k_e, q_m, k_m


def _gla_fwd(z_qk, z_v, z_zg, wg, bg, t_rows, name, comm=None):
    nc = t_rows // CHUNK
    ng = nc // GLA_G
    rows = CHUNK * GLA_G

    def kern(qf, kf, vf, zf, qb, kb, vb, zb, wg_ref, bg_ref, of_ref, ob_ref, sf_ref, sb_ref, st_f, st_b):
        i = pl.program_id(0)

        @pl.when(i == 0)
        def _():
            st_f[...] = jnp.zeros_like(st_f)
            st_b[...] = jnp.zeros_like(st_b)

        wg_v, bg_v = wg_ref[...], bg_ref[...]
        for rev, (q_r, k_r, v_r, z_r, o_r, s_r, st) in ((False, (qf, kf, vf, zf, of_ref, sf_ref, st_f)),
                                                        (True, (qb, kb, vb, zb, ob_ref, sb_ref, st_b))):
            mask = _gla_mask(rev)
            for ck in (range(GLA_G - 1, -1, -1) if rev else range(GLA_G)):
                cr = slice(ck * CHUNK, (ck + 1) * CHUNK)
                tm = _gla_chunk_terms(q_r[cr, :], k_r[cr, :], z_r[cr, :], wg_v, bg_v, rev)
                v = v_r[cr, :]
                for hd in range(GLA_H):
                    sl, q_b, k_e, q_m, k_m = _gla_head_ops(tm, hd)
                    vs = slice(hd * GLA_DV, (hd + 1) * GLA_DV)
                    v16 = v[:, vs].astype(BF16)
                    a = jnp.where(mask, _dot(q_m, k_m, NT), 0.0).astype(BF16)
                    s_in = st[hd]
                    s_r[ck, hd] = s_in
                    o_r[cr, vs] = _dot(a, v16) + _dot(q_b, s_in.astype(BF16), NT)
                    st[hd] = s_in * tm["e_last"][:, sl] + _dot(v16, k_e, TN)

    fw = lambda c: (lambda i: (i, c))
    rv = lambda c: (lambda i: (ng - 1 - i, c))
    def side(ix):
        return [pl.BlockSpec((rows, QK), ix(0)), pl.BlockSpec((rows, QK), ix(1)),
                pl.BlockSpec((rows, GLA_W), ix(0)), pl.BlockSpec((rows, LANES), ix(0))]
    st_shape = (nc, GLA_H, GLA_DV, GLA_DK)
    res, comm_res = _pcall(
        kern, name, (ng,),
        side(fw) + side(rv) + [pl.BlockSpec(wg.shape, lambda i: (0, 0)), pl.BlockSpec(bg.shape, lambda i: (0, 0))],
        [pl.BlockSpec((rows, GLA_W), fw(0)), pl.BlockSpec((rows, GLA_W), rv(0)),
         pl.BlockSpec((GLA_G,) + st_shape[1:], lambda i: (i, 0, 0, 0)),
         pl.BlockSpec((GLA_G,) + st_shape[1:], lambda i: (ng - 1 - i, 0, 0, 0))],
        [jax.ShapeDtypeStruct((t_rows, GLA_W), F32)] * 2 + [jax.ShapeDtypeStruct(st_shape, F32)] * 2,
        [pltpu.VMEM(st_shape[1:], F32)] * 2, ("arbitrary",),
        (z_qk, z_qk, z_v, z_zg, z_qk, z_qk, z_v, z_zg, wg, bg), comm)
    return res if comm is None else (res, comm_res)


def _gla_bwd(z_qk, z_v, z_zg, wg, bg, do, s_f, s_b, t_rows, name, comm=None):
    nc = t_rows // CHUNK

    def kern(qf, kf, vf, zf, dof, ssf, qb, kb, vb, zb, dob, ssb, wg_ref, bg_ref,
             dqf, dkf, dvf, dpf, dqb, dkb, dvb, dpb, ds_f, ds_b):
        i = pl.program_id(0)

        @pl.when(i == 0)
        def _():
            ds_f[...] = jnp.zeros_like(ds_f)
            ds_b[...] = jnp.zeros_like(ds_b)

        wg_v, bg_v = wg_ref[...], bg_ref[...]
        sides = ((False, (qf, kf, vf, zf, dof, ssf, dqf, dkf, dvf, dpf, ds_f)),
                 (True, (qb, kb, vb, zb, dob, ssb, dqb, dkb, dvb, dpb, ds_b)))
        for rev, (q_r, k_r, v_r, z_r, do_r, ss_r, dq_r, dk_r, dv_r, dp_r, ds) in sides:
            mask = _gla_mask(rev)
            for ck in (range(GLA_G) if rev else range(GLA_G - 1, -1, -1)):
                cr = slice(ck * CHUNK, (ck + 1) * CHUNK)
                tm = _gla_chunk_terms(q_r[cr, :], k_r[cr, :], z_r[cr, :], wg_v, bg_v, rev)
                v = v_r[cr, :]
                d_o = do_r[cr, :]
                db_parts, cross_parts = [], []
                for hd in range(GLA_H):
                    sl, q_b, k_e, q_m, k_m = _gla_head_ops(tm, hd)
                    vs = slice(hd * GLA_DV, (hd + 1) * GLA_DV)
                    v16, do16 = v[:, vs].astype(BF16), d_o[:, vs].astype(BF16)
                    a = jnp.where(mask, _dot(q_m, k_m, NT), 0.0).astype(BF16)
                    da = jnp.where(mask, _dot(do16, v16, NT), 0.0).astype(BF16)
                    s_in = ss_r[ck, hd]
                    s_in16 = s_in.astype(BF16)
                    ds_out = ds[hd]
                    ds16 = ds_out.astype(BF16)
                    dv_r[cr, vs] = _dot(a, do16, TN) + _dot(k_e, ds16, NT)
                    dqs = _dot(da, k_m) * tm["eqm"][:, sl] + _dot(do16, s_in16) * tm["eb"][:, sl]
                    dk = _dot(da, q_m, TN) * tm["ekm"][:, sl] + _dot(v16, ds16) * tm["ek"][:, sl]
                    ds[hd] = ds_out * tm["e_last"][:, sl] + _dot(do16, q_b, TN)
                    dq_r[cr, sl] = dqs
                    dk_r[cr, sl] = dk
                    db_parts.append(tm["qs"][:, sl] * dqs - tm["k"][:, sl] * dk)
                    s_out = s_in * tm["e_last"][:, sl] + _dot(v16, k_e, TN)
                    cross_parts.append(jnp.sum(s_out * ds_out, axis=0, keepdims=True))
                d_b = jnp.concatenate(db_parts, axis=1)
                dg = _cumsum_rows(d_b, not rev) + jnp.concatenate(cross_parts, axis=1)
                dp_r[cr, :] = dg * (1.0 / GLA_GATE_NORM) * _sigmoid(-tm["pre"])

    ng = nc // GLA_G
    rows = CHUNK * GLA_G
    fw = lambda c: (lambda i: (ng - 1 - i, c))
    rv = lambda c: (lambda i: (i, c))
    st_blk = (GLA_G, GLA_H, GLA_DV, GLA_DK)
    def side(ix, st_ix):
        return [pl.BlockSpec((rows, QK), ix(0)), pl.BlockSpec((rows, QK), ix(1)),
                pl.BlockSpec((rows, GLA_W), ix(0)), pl.BlockSpec((rows, LANES), ix(0)),
                pl.BlockSpec((rows, GLA_W), ix(0)), pl.BlockSpec(st_blk, st_ix)]
    def oside(ix):
        return [pl.BlockSpec((rows, QK), ix(0)), pl.BlockSpec((rows, QK), ix(0)),
                pl.BlockSpec((rows, GLA_W), ix(0)), pl.BlockSpec((rows, QK), ix(0))]
    o_shapes = [jax.ShapeDtypeStruct((t_rows, QK), F32), jax.ShapeDtypeStruct((t_rows, QK), F32),
                jax.ShapeDtypeStruct((t_rows, GLA_W), F32), jax.ShapeDtypeStruct((t_rows, QK), F32)]
    res, comm_res = _pcall(
        kern, name, (ng,),
        (side(fw, lambda i: (ng - 1 - i, 0, 0, 0)) + side(rv, lambda i: (i, 0, 0, 0))
         + [pl.BlockSpec(wg.shape, lambda i: (0, 0)), pl.BlockSpec(bg.shape, lambda i: (0, 0))]),
        oside(fw) + oside(rv), o_shapes * 2, [pltpu.VMEM((GLA_H, GLA_DV, GLA_DK), F32)] * 2, ("arbitrary",),
        (z_qk, z_qk, z_v, z_zg, do, s_f, z_qk, z_qk, z_v, z_zg, do, s_b, wg, bg), comm)
    return res if comm is None else (res, comm_res)


FLAT_W = 1024


def _adam_math(wv, gv, mv, vv):
    bc1 = 1.0 - ADAM_B1 ** ADAM_STEP
    bc2 = 1.0 - ADAM_B2 ** ADAM_STEP
    m_new = ADAM_B1 * mv + (1.0 - ADAM_B1) * gv
    v_new = ADAM_B2 * vv + (1.0 - ADAM_B2) * (gv * gv)
    delta = -ADAM_LR * ((m_new / bc1) / (jnp.sqrt(v_new / bc2) + ADAM_EPS) + ADAM_WD * wv)
    return delta, m_new, v_new


def _row_tile(rows, cap=256):
    t = cap
    while rows % t:
        t //= 2
    assert t >= SUBLANES, rows
    return t


def _adamw_layers(w, m, v, g_layers, name):
    depth, rows, cols = w.shape
    tr = _row_tile(rows)

    def kern(w_ref, m_ref, v_ref, *rest):
        g_refs, (go_ref, d_ref, mo_ref, vo_ref) = rest[:depth], rest[depth:]
        l = pl.program_id(0)
        gv = g_refs[0][...]
        for k in range(1, depth):
            gv = jnp.where(l == k, g_refs[k][...], gv)
        delta, m_new, v_new = _adam_math(w_ref[...], gv, m_ref[...], v_ref[...])
        go_ref[...] = gv
        d_ref[...] = delta
        mo_ref[...] = m_new
        vo_ref[...] = v_new

    stacked = pl.BlockSpec((None, tr, cols), lambda l, i: (l, i, 0))
    return pl.pallas_call(
        kern, name=name, grid=(depth, rows // tr),
        in_specs=[stacked] * 3 + [pl.BlockSpec((tr, cols), lambda l, i: (i, 0))] * depth,
        out_specs=[stacked] * 4, out_shape=[jax.ShapeDtypeStruct(w.shape, F32)] * 4,
        compiler_params=_cparams(("arbitrary", "arbitrary")),
    )(w, m, v, *g_layers)


def _adamw_small(ws, gs, ms, vs):
    n = len(ws)

    def kern(*refs):
        ins, outs = refs[:4 * n], refs[4 * n:]
        for k in range(n):
            delta, m_new, v_new = _adam_math(ins[k][...], ins[n + k][...], ins[2 * n + k][...], ins[3 * n + k][...])
            outs[k][...] = delta
            outs[n + k][...] = m_new
            outs[2 * n + k][...] = v_new

    shapes = [jax.ShapeDtypeStruct(w.shape, F32) for w in ws]
    res = pl.pallas_call(kern, name="adamw_small", out_shape=shapes * 3,
                         compiler_params=pltpu.CompilerParams(vmem_limit_bytes=VMEM_LIMIT))(*ws, *gs, *ms, *vs)
    return res[:n], res[n:2 * n], res[2 * n:]


def _add_sibling(g, got, c_idx, name):
    _, _, rows, cols = g.shape
    tr = _row_tile(rows)

    def kern(c_ref, g_ref, got_ref, o_ref):
        o_ref[...] = (g_ref[...] + got_ref[...]).astype(BF16)

    grid_spec = pltpu.PrefetchScalarGridSpec(
        num_scalar_prefetch=1, grid=(N_CHIPS, rows // tr),
        in_specs=[pl.BlockSpec((None, None, tr, cols), lambda q, i, c_ref: (q, c_ref[0], i, 0)),
                  pl.BlockSpec((None, tr, cols), lambda q, i, c_ref: (q, i, 0))],
        out_specs=pl.BlockSpec((None, tr, cols), lambda q, i, c_ref: (q, i, 0)))
    return pl.pallas_call(
        kern, name=name, grid_spec=grid_spec, out_shape=jax.ShapeDtypeStruct((N_CHIPS, rows, cols), BF16),
        compiler_params=_cparams(("arbitrary", "arbitrary")),
    )(c_idx, g, got)


def _sum_chips(x, name):
    _, rows, cols = x.shape
    tr = _row_tile(rows)

    def kern(x_ref, o_ref):
        xv = x_ref[...].astype(F32)
        o_ref[...] = ((xv[0] + xv[1]) + xv[2]) + xv[3]

    return pl.pallas_call(
        kern, name=name, grid=(rows // tr,),
        in_specs=[pl.BlockSpec((N_CHIPS, tr, cols), lambda i: (0, i, 0))],
        out_specs=pl.BlockSpec((tr, cols), lambda i: (i, 0)),
        out_shape=jax.ShapeDtypeStruct((rows, cols), F32),
        compiler_params=_cparams(("arbitrary",)),
    )(x)


def _place():
    x, y, c = lax.axis_index("x"), lax.axis_index("y"), lax.axis_index("c")
    return x, y, c


def _other_chips(x, y):
    return [(1 - x, y), (x, 1 - y), (1 - x, 1 - y)]


class _Comm:
    def __init__(self, ins, out_shapes, n_sems, stage, body):
        self.ins, self.out_shapes, self.body = list(ins), list(out_shapes), body
        self.scratch = [pltpu.SemaphoreType.DMA((n,)) for n in n_sems] + [pltpu.VMEM(s, d) for s, d in stage]


ANY_SPEC = pl.BlockSpec(memory_space=pl.ANY)


def _run_comm(comm, name):
    def kern(*refs):
        comm.body(refs, True, True)

    return pl.pallas_call(
        kern, name=name, in_specs=[ANY_SPEC] * len(comm.ins), out_specs=[ANY_SPEC] * len(comm.out_shapes),
        out_shape=comm.out_shapes, scratch_shapes=comm.scratch,
        compiler_params=pltpu.CompilerParams(has_side_effects=True, vmem_limit_bytes=VMEM_LIMIT),
    )(*comm.ins)


def _pcall(kern, name, grid, in_specs, out_specs, out_shape, scratch_shapes, sem, args, comm=None):
    if comm is None:
        return pl.pallas_call(kern, name=name, grid=grid, in_specs=in_specs, out_specs=out_specs, out_shape=out_shape,
                              scratch_shapes=scratch_shapes, compiler_params=_cparams(sem))(*args), None
    n_in, n_out, n_scr = len(in_specs), len(out_specs), len(scratch_shapes)
    c_in, c_out = len(comm.ins), len(comm.out_shapes)

    def hosted(*refs):
        a = n_in + c_in
        b = a + n_out + c_out
        main = refs[:n_in] + refs[a:a + n_out] + refs[b:b + n_scr]
        extra = refs[n_in:a] + refs[a + n_out:b] + refs[b + n_scr:]
        ids = [pl.program_id(d) for d in range(len(grid))]
        first, last = ids[0] == 0, ids[0] == grid[0] - 1
        for i, g in zip(ids[1:], grid[1:]):
            first, last = jnp.logical_and(first, i == 0), jnp.logical_and(last, i == g - 1)

        @pl.when(first)
        def _():
            comm.body(extra, True, False)

        kern(*main)

        @pl.when(last)
        def _():
            comm.body(extra, False, True)

    res = pl.pallas_call(
        hosted, name=name, grid=grid, in_specs=list(in_specs) + [ANY_SPEC] * c_in,
        out_specs=list(out_specs) + [ANY_SPEC] * c_out, out_shape=list(out_shape) + comm.out_shapes,
        scratch_shapes=list(scratch_shapes) + comm.scratch,
        compiler_params=pltpu.CompilerParams(dimension_semantics=("arbitrary",) * len(grid),
                                             vmem_limit_bytes=VMEM_LIMIT, has_side_effects=True),
    )(*args, *comm.ins)
    return res[:n_out], res[n_out:]


class _StagedCopies:
    def __init__(self, pairs, bufs, sem_in, sem_out):
        self.ins = [pltpu.make_async_copy(src, buf, sem_in.at[i]) for i, ((src, _), buf) in enumerate(zip(pairs, bufs))]
        self.outs = [pltpu.make_async_copy(buf, dst, sem_out.at[i]) for i, ((_, dst), buf) in enumerate(zip(pairs, bufs))]

    def load(self):
        for cp in self.ins:
            cp.start()

    def store(self):
        for cp in self.ins:
            cp.wait()
        for cp in self.outs:
            cp.start()

    def finish(self):
        for cp in self.outs:
            cp.wait()


DMA_CHUNK_BYTES = 1 << 20
DMA_MAX_CHUNKS = 4


def _row_chunks(rows, row_bytes, align=16):
    units = rows // align
    k = max(1, min(DMA_MAX_CHUNKS, -(-rows * row_bytes // DMA_CHUNK_BYTES), units))
    out, start = [], 0
    for i in range(k):
        size = (units // k + (1 if i < units % k else 0)) * align
        out.append((start, size))
        start += size
    if start < rows:
        out[-1] = (out[-1][0], out[-1][1] + rows - start)
    return out


def _row_bytes(shape, dtype):
    return math.prod(shape[1:]) * jnp.dtype(dtype).itemsize


def _remote(src, dst, send, recv, idx, dev):
    return pltpu.make_async_remote_copy(src, dst, send.at[idx], recv.at[idx], device_id=dev, device_id_type=MESH_ID)


def _allgather_chips(xs):
    na = len(xs)
    chunks = [_row_chunks(x.shape[1], _row_bytes(x.shape[1:], x.dtype)) for x in xs]
    n_cp = 3 * sum(len(ch) for ch in chunks)

    def body(refs, start, finish):
        x_refs, o_refs = refs[:na], refs[na:2 * na]
        send1, recv1, send2, recv2, loc_in, loc_out = refs[2 * na:2 * na + 6]
        px, py, c = _place()
        me = 2 * px + py
        sib = (px, py, 1 - c)
        own = _StagedCopies([(x, o.at[me]) for x, o in zip(x_refs, o_refs)], refs[2 * na + 6:], loc_in, loc_out)
        plan = [(a, qx, qy, pl.ds(s, n)) for a in range(na) for (qx, qy) in _other_chips(px, py) for (s, n) in chunks[a]]
        first = [_remote(x_refs[a].at[c, rows], o_refs[a].at[me, c, rows], send1, recv1, i, (qx, qy, c))
                 for i, (a, qx, qy, rows) in enumerate(plan)]
        if start:
            own.load()
            for cp in first:
                cp.start()
        if finish:
            own.store()
            passed = []
            for i, (a, qx, qy, rows) in enumerate(plan):
                slot = o_refs[a].at[2 * qx + qy, c, rows]
                _remote(slot, slot, send1, recv1, i, (qx, qy, c)).wait_recv()
                fwd = _remote(slot, slot, send2, recv2, i, sib)
                fwd.start()
                passed.append(fwd)
            for i, (a, qx, qy, rows) in enumerate(plan):
                slot = o_refs[a].at[2 * qx + qy, 1 - c, rows]
                _remote(slot, slot, send2, recv2, i, sib).wait_recv()
            for cp in first + passed:
                cp.wait_send()
            own.finish()

    outs = [jax.ShapeDtypeStruct((N_CHIPS,) + x.shape, x.dtype) for x in xs]
    return _Comm(xs, outs, [n_cp, n_cp, n_cp, n_cp, na, na], [(x.shape, x.dtype) for x in xs], body)


def _sibling_halves(gs):
    na = len(gs)
    chunks = [_row_chunks(g.shape[2], _row_bytes(g.shape[2:], g.dtype)) for g in gs]
    n_cp = N_CHIPS * sum(len(ch) for ch in chunks)

    def body(refs, start, finish):
        g_refs, o_refs = refs[:na], refs[na:2 * na]
        send, recv = refs[2 * na:]
        px, py, c = _place()
        plan = [(a, q, pl.ds(s, n)) for a in range(na) for q in range(N_CHIPS) for (s, n) in chunks[a]]
        cps = [_remote(g_refs[a].at[q, 1 - c, rows], o_refs[a].at[q, rows], send, recv, i, (px, py, 1 - c))
               for i, (a, q, rows) in enumerate(plan)]
        if start:
            for cp in cps:
                cp.start()
        if finish:
            for cp in cps:
                cp.wait()

    outs = [jax.ShapeDtypeStruct((N_CHIPS,) + g.shape[2:], g.dtype) for g in gs]
    return _Comm(gs, outs, [n_cp, n_cp], [], body)


def _chip_exchange(ps):
    na = len(ps)
    chunks = [_row_chunks(p.shape[1], _row_bytes(p.shape[1:], p.dtype)) for p in ps]
    n_cp = 3 * sum(len(ch) for ch in chunks)

    def body(refs, start, finish):
        p_refs, o_refs = refs[:na], refs[na:2 * na]
        send, recv, loc_in, loc_out = refs[2 * na:2 * na + 4]
        px, py, c = _place()
        me = 2 * px + py
        own = _StagedCopies([(p.at[me], o.at[me]) for p, o in zip(p_refs, o_refs)], refs[2 * na + 4:], loc_in, loc_out)
        plan = [(a, qx, qy, pl.ds(s, n)) for a in range(na) for (qx, qy) in _other_chips(px, py) for (s, n) in chunks[a]]
        cps = [_remote(p_refs[a].at[2 * qx + qy, rows], o_refs[a].at[me, rows], send, recv, i, (qx, qy, c))
               for i, (a, qx, qy, rows) in enumerate(plan)]
        if start:
            own.load()
            for cp in cps:
                cp.start()
        if finish:
            own.store()
            for cp in cps:
                cp.wait()
            own.finish()

    outs = [jax.ShapeDtypeStruct(p.shape, p.dtype) for p in ps]
    return _Comm(ps, outs, [n_cp, n_cp, na, na], [(p.shape[1:], p.dtype) for p in ps], body)


def _sibling_join(rs):
    na = len(rs)
    chunks = [_row_chunks(r.shape[0], _row_bytes(r.shape, r.dtype)) for r in rs]
    n_cp = sum(len(ch) for ch in chunks)

    def body(refs, start, finish):
        r_refs, o_refs = refs[:na], refs[na:2 * na]
        send, recv, loc_in, loc_out = refs[2 * na:2 * na + 4]
        px, py, c = _place()
        own = _StagedCopies([(r, o.at[c]) for r, o in zip(r_refs, o_refs)], refs[2 * na + 4:], loc_in, loc_out)
        plan = [(a, pl.ds(s, n)) for a in range(na) for (s, n) in chunks[a]]
        cps = [_remote(r_refs[a].at[rows], o_refs[a].at[c, rows], send, recv, i, (px, py, 1 - c))
               for i, (a, rows) in enumerate(plan)]
        if start:
            own.load()
            for cp in cps:
                cp.start()
        if finish:
            own.store()
            for cp in cps:
                cp.wait()
            own.finish()

    outs = [jax.ShapeDtypeStruct((2,) + r.shape, r.dtype) for r in rs]
    return _Comm(rs, outs, [n_cp, n_cp, na, na], [(r.shape, r.dtype) for r in rs], body)


def _reduce_stages(gs, c_idx, tag):
    got = yield _sibling_halves(gs)
    part = [_add_sibling(g, o, c_idx, f"reduce_add_sibling{tag}_{a}") for a, (g, o) in enumerate(zip(gs, got))]
    landed = yield _chip_exchange(part)
    mine = [_sum_chips(x, f"reduce_sum_chips{tag}_{a}") for a, x in enumerate(landed)]
    return (yield _sibling_join(mine))


class _Staged:
    def __init__(self, gen, tag):
        self.gen, self.tag, self.k, self.value = gen, tag, 0, None
        self.cur = next(gen)

    def stage(self):
        return self.cur

    def done(self, results):
        self.k += 1
        try:
            self.cur = self.gen.send(results)
        except StopIteration as stop:
            self.cur, self.value = None, stop.value

    def drain(self):
        while self.cur is not None:
            self.done(_run_comm(self.cur, f"comm{self.tag}_{self.k}"))
        return self.value


WEIGHTS = ["meta_tokens", "norm_mix_pre", "norm_mix_post", "norm_mlp_pre", "norm_mlp_post", "w_in", "conv_w",
           "conv_b", "lru_wa_f", "lru_ba_f", "lru_wx_f", "lru_bx_f", "lru_lambda_f", "lru_wa_b", "lru_ba_b",
           "lru_wx_b", "lru_bx_b", "lru_lambda_b", "gla_wg_f", "gla_bg_f", "gla_wg_b", "gla_bg_b", "gla_head_norm",
           "w_out", "w_mlp_up", "w_mlp_down"]
BIG = ("w_in", "w_out", "w_mlp_up", "w_mlp_down")
SMALL = [n for n in WEIGHTS if n not in BIG]
SMALL_SHARDED = {"meta_tokens": 1, "conv_w": 2, "gla_wg_f": 2, "gla_wg_b": 2}
N_CHIPS = 4
SMALL_RAW = [("g1", (1, D_MODEL)), ("g2", (1, D_MODEL)), ("g3", (1, D_MODEL)), ("g4", (1, D_MODEL)),
             ("conv_w", (4, LRU_W)), ("conv_b", (1, LRU_W)), ("lru_wg", (LRU_HEADS, LRU_HD, 4 * LRU_HD)),
             ("lru_pb", (SUBLANES, LRU_W)), ("gla_wg", (2 * GLA_RANK, 2 * QK)), ("gla_bg", (1, 2 * QK)),
             ("head_norm", (1, GLA_W))]
META_SHAPE = (N_META, D_MODEL)


def _pad_to(v, n):
    return jnp.pad(v, (0, n - v.shape[0]))


def _round_up(n, m):
    return -(-n // m) * m


def _flat_cat(arrs, total):
    return _pad_to(jnp.concatenate([a.reshape(-1) for a in arrs]), total)


def _split_flat(flat, shapes):
    out, off = [], 0
    for s in shapes:
        n = math.prod(s)
        out.append(flat[off:off + n].reshape(s))
        off += n
    return out


def _my_shard(full, axis, chip):
    n = full.shape[axis] // N_CHIPS
    return lax.dynamic_slice_in_dim(full, chip * n, n, axis=axis)


def _block_diag_gates(wa_f, wx_f, wa_b, wx_b):
    eye2 = jnp.eye(2, dtype=wa_f.dtype)

    def bd(w):
        w4 = w.reshape(4, 2, LRU_HD, LRU_HD)
        return (w4[:, :, :, None, :] * eye2[None, :, None, :, None]).reshape(4, LANES, LANES)
    return jnp.concatenate([bd(wa_f), bd(wx_f), bd(wa_b), bd(wx_b)], axis=2).astype(BF16)


def _gate_diag_blocks(dwh, o):
    return dwh[:, :, o * LRU_HD:(o + 1) * LRU_HD]


def kernel(x, meta_tokens, norm_mix_pre, norm_mix_post, norm_mlp_pre, norm_mlp_post, w_in, conv_w, conv_b, lru_wa_f, lru_ba_f, lru_wx_f, lru_bx_f, lru_lambda_f, lru_wa_b, lru_ba_b, lru_wx_b, lru_bx_b, lru_lambda_b, gla_wg_f, gla_bg_f, gla_wg_b, gla_bg_b, gla_head_norm, w_out, w_mlp_up, w_mlp_down, loss_target, m_meta_tokens, m_norm_mix_pre, m_norm_mix_post, m_norm_mlp_pre, m_norm_mlp_post, m_w_in, m_conv_w, m_conv_b, m_lru_wa_f, m_lru_ba_f, m_lru_wx_f, m_lru_bx_f, m_lru_lambda_f, m_lru_wa_b, m_lru_ba_b, m_lru_wx_b, m_lru_bx_b, m_lru_lambda_b, m_gla_wg_f, m_gla_bg_f, m_gla_wg_b, m_gla_bg_b, m_gla_head_norm, m_w_out, m_w_mlp_up, m_w_mlp_down, v_meta_tokens, v_norm_mix_pre, v_norm_mix_post, v_norm_mlp_pre, v_norm_mlp_post, v_w_in, v_conv_w, v_conv_b, v_lru_wa_f, v_lru_ba_f, v_lru_wx_f, v_lru_bx_f, v_lru_lambda_f, v_lru_wa_b, v_lru_ba_b, v_lru_wx_b, v_lru_bx_b, v_lru_lambda_b, v_gla_wg_f, v_gla_bg_f, v_gla_wg_b, v_gla_bg_b, v_gla_head_norm, v_w_out, v_w_mlp_up, v_w_mlp_down):
    args = dict(locals())
    w_loc = {n: args[n] for n in WEIGHTS}
    m_loc = {n: args["m_" + n] for n in WEIGHTS}
    v_loc = {n: args["v_" + n] for n in WEIGHTS}
    seq = x.shape[1]
    t_rows = FRONT + seq + BACK
    assert t_rows % 768 == 0 and seq % FRONT == 0, seq
    chip = 2 * lax.axis_index("x") + lax.axis_index("y")
    c_idx = lax.axis_index("c").astype(jnp.int32).reshape(1)

    def halves(a):
        return a.reshape((2, a.shape[0] // 2) + a.shape[1:])

    big16 = {n: w_loc[n].astype(BF16) for n in BIG}
    sm_names = list(SMALL_SHARDED)
    sm_len = _round_up(sum(w_loc[n].size for n in sm_names), 2 * SUBLANES * FLAT_W)
    sm_pack = _flat_cat([w_loc[n] for n in sm_names], sm_len).reshape(2, -1, FLAT_W)
    g_in0, g_small = _run_comm(_allgather_chips([halves(big16["w_in"][0]), sm_pack]), "gather_first")
    sm_parts = [_split_flat(g_small[q].reshape(-1), [w_loc[n].shape for n in sm_names]) for q in range(N_CHIPS)]
    full = {n: jnp.concatenate([sm_parts[q][i] for q in range(N_CHIPS)], axis=SMALL_SHARDED[n])
            for i, n in enumerate(sm_names)}
    for n in SMALL:
        if n not in SMALL_SHARDED:
            full[n] = w_loc[n]

    def full_w_in(g):
        w = jnp.transpose(g.reshape(N_CHIPS, D_MODEL, -1), (1, 0, 2)).reshape(D_MODEL, D_IN)
        return jnp.pad(w, ((0, 0), (0, D_IN_PAD - D_IN)))

    def full_rest(g_out, g_up, g_down):
        return dict(w_out=g_out.reshape(-1, D_MODEL),
                    w_up=jnp.transpose(g_up.reshape(N_CHIPS, D_MODEL, -1), (1, 0, 2)).reshape(D_MODEL, D_FF),
                    w_down=g_down.reshape(D_FF, D_MODEL))

    later_gathers = [_allgather_chips([halves(big16[n][0]) for n in BIG[1:]]),
                     _allgather_chips([halves(big16[n][l]) for l in range(1, DEPTH) for n in BIG])]

    meta_blk = jnp.concatenate([jnp.zeros((FRONT - N_META, D_MODEL), F32), full["meta_tokens"]], axis=0)
    h = jnp.concatenate([meta_blk, x[0], jnp.zeros((BACK, D_MODEL), F32)], axis=0)
    target = loss_target[0]

    layer_w = []
    for l in range(DEPTH):
        wg = jnp.zeros((LANES, 2 * QK), F32)
        wg = wg.at[0:GLA_RANK, 0:QK].set(full["gla_wg_f"][l]).at[GLA_RANK:2 * GLA_RANK, QK:].set(full["gla_wg_b"][l])
        pb = jnp.stack([full["lru_ba_f"][l], full["lru_bx_f"][l], full["lru_ba_b"][l], full["lru_bx_b"][l],
                        full["lru_lambda_f"][l], full["lru_lambda_b"][l], jnp.zeros((LRU_W,), F32),
                        jnp.zeros((LRU_W,), F32)])
        layer_w.append(dict(
            g1=full["norm_mix_pre"][l][None], g2=full["norm_mix_post"][l][None],
            g3=full["norm_mlp_pre"][l][None], g4=full["norm_mlp_post"][l][None],
            conv_w=full["conv_w"][l], conv_b=full["conv_b"][l][None],
            lru_wg=_block_diag_gates(full["lru_wa_f"][l], full["lru_wx_f"][l], full["lru_wa_b"][l], full["lru_wx_b"][l]),
            lru_pb=pb, gla_wg=wg.astype(BF16),
            gla_bg=jnp.concatenate([full["gla_bg_f"][l], full["gla_bg_b"][l]])[None],
            head_norm=full["gla_head_norm"][l][None]))
    layer_w[0]["w_in"] = full_w_in(g_in0)

    def split_z(acc):
        return (acc[:, 0:1024], acc[:, 1024:1536], acc[:, 1536:2048], acc[:, 2048:2560], acc[:, ZG_OFF:ZG_OFF + LANES])

    def relu2(acc):
        r = jnp.maximum(acc, 0.0)
        return acc, r * r

    tm_e = 768
    tm_ff = 1408 if t_rows % 1408 == 0 else 768

    def post_mix(acc, hh, g2, g3):
        h1 = hh + _rms_fwd(acc, g2)
        return acc, h1, _rms_fwd(h1, g3)

    def post_mlp(acc, h1, g4, g1_next):
        h2 = h1 + _rms_fwd(acc, g4)
        return acc, h2, _rms_fwd(h2, g1_next)

    def post_mlp_loss(acc, h1, tgt, x_rows, g4):
        h2 = h1 + _rms_fwd(acc, g4)
        err = (h2 - tgt) * x_rows[:, 0:1]
        d_h = err * (1.0 / D_MODEL)
        dff, dg4 = _rms_bwd(acc, g4, d_h)
        loss_part = jnp.zeros((SUBLANES, LANES), F32) + jnp.sum(err * err) * (0.5 / D_MODEL)
        return d_h, dff, loss_part, dg4

    row3 = [(D_MODEL, F32), (D_MODEL, F32), (D_MODEL, BF16)]
    vec = ((1, D_MODEL), F32)
    target_p = jnp.concatenate([jnp.zeros((FRONT, D_MODEL), F32), target, jnp.zeros((BACK, D_MODEL), F32)], axis=0)
    x_rows = jnp.concatenate([jnp.zeros((FRONT, LANES), F32), jnp.ones((seq, LANES), F32), jnp.zeros((BACK, LANES), F32)])
    n_sub = 1
    saved = []
    n1 = _norm_cast(h, layer_w[0]["g1"], t_rows, "norm_mix_pre_l0")
    for l in range(DEPTH):
        lw = layer_w[l]
        tag = f"_l{l}"
        z_lru, z_qk, z_v, z_go, z_zg = _mm(
            n1, lw["w_in"], "nn", 768, D_IN_PAD, D_MODEL,
            [(1024, F32), (512, F32), (512, F32), (512, F32), (LANES, F32)], "in_proj" + tag, epilogue=split_z)
        lru_args = (z_lru, lw["conv_w"], lw["conv_b"], lw["lru_wg"], lw["lru_pb"], t_rows, seq, "lru_fwd" + tag)
        gla_args = (z_qk, z_v, z_zg, lw["gla_wg"], lw["gla_bg"], t_rows, "gla_fwd" + tag)
        if l == 0:
            (h_f, h_b), g_rest = _lru_fwd(*lru_args, comm=later_gathers[0])
            lw.update(full_rest(*g_rest))
            (o_f, o_b, s_f, s_b), g_next = _gla_fwd(*gla_args, comm=later_gathers[1])
            for l2 in range(1, DEPTH):
                g_l2 = g_next[(l2 - 1) * len(BIG):l2 * len(BIG)]
                layer_w[l2].update(full_rest(*g_l2[1:]), w_in=full_w_in(g_l2[0]))
        else:
            h_f, h_b = _lru_fwd(*lru_args)
            o_f, o_b, s_f, s_b = _gla_fwd(*gla_args)
        ymix = _mix_fwd(h_f, h_b, z_lru, o_f, o_b, z_go, lw["head_norm"], t_rows, "mix_fwd" + tag)
        mix, h1, n3 = _mm(ymix, lw["w_out"], "nn", 768, D_MODEL, D_MODEL, row3, "out_proj" + tag, epilogue=post_mix,
                          rows_in=(h,), consts=(lw["g2"], lw["g3"]), row_split=n_sub)
        u, act = _mm(n3, lw["w_up"], "nn", tm_ff, 1024, D_MODEL, [(D_FF, BF16), (D_FF, BF16)], "mlp_up" + tag,
                     epilogue=relu2)
        sv = dict(h=h, n1=n1, z_lru=z_lru, z_qk=z_qk, z_v=z_v, z_go=z_go, z_zg=z_zg, h_f=h_f, h_b=h_b,
                  o_f=o_f, o_b=o_b, s_f=s_f, s_b=s_b, ymix=ymix, mix=mix, h1=h1, n3=n3, u=u, act=act)
        if l + 1 < DEPTH:
            sv["ff"], h, n1 = _mm(act, lw["w_down"], "nn", tm_e, D_MODEL, D_FF, row3, "mlp_down" + tag,
                                  epilogue=post_mlp, rows_in=(h1,), consts=(lw["g4"], layer_w[l + 1]["g1"]),
                                  row_split=n_sub)
        else:
            dh, dff, loss_part, dg4 = _mm(act, lw["w_down"], "nn", tm_e, D_MODEL, D_FF,
                                          [(D_MODEL, F32), (D_MODEL, BF16)], "mlp_down_loss" + tag,
                                          epilogue=post_mlp_loss, rows_in=(h1, target_p, x_rows), consts=(lw["g4"],),
                                          accs=[((SUBLANES, LANES), F32), vec], row_split=n_sub)
        saved.append(sv)

    loss = lax.psum(loss_part[0, 0], ("x", "y", "c"))

    small_len = _round_up(sum(math.prod(s) for _, s in SMALL_RAW) + math.prod(META_SHAPE),
                          N_CHIPS * 2 * 2 * SUBLANES * FLAT_W)
    totals = [None] * DEPTH
    pend = None

    def with_stage(fn):
        comm = pend.stage() if pend is not None else None
        if comm is None:
            return fn(None)
        res, comm_res = fn(comm)
        pend.done(comm_res)
        return res

    for l in reversed(range(DEPTH)):
        lw, sv = layer_w[l], saved[l]
        tag = f"_l{l}"
        raw = {"g4": dg4}
        gw_down = with_stage(lambda comm: _mm(sv["act"], dff, "tn", 1024, D_MODEL, tm_ff, [(D_MODEL, F32)],
                                              "dw_down" + tag, comm=comm))[0]
        du = _mm(dff, lw["w_down"], "nt", tm_ff, 1024, D_MODEL, [(D_FF, BF16)], "d_act" + tag,
                 epilogue=lambda acc, uu: (acc * 2.0 * jnp.maximum(uu.astype(F32), 0.0),), extras=(sv["u"],))[0]
        gw_up = _mm(sv["n3"], du, "tn", D_MODEL, D_FF // N_CHIPS, tm_ff, [(D_FF, F32)], "dw_up" + tag,
                    col_blocks_first=True)[0]

        def pre_mlp_bwd(acc, h1, d_h, mix, g3, g2):
            dx, dg3 = _rms_bwd(h1, g3, acc)
            d_h1 = d_h + dx
            d_mix, dg2 = _rms_bwd(mix, g2, d_h1)
            return d_h1, d_mix, dg3, dg2

        dh1, dmix, raw["g3"], raw["g2"] = _mm(
            du, lw["w_up"], "nt", tm_e, D_MODEL, D_FF, [(D_MODEL, F32), (D_MODEL, BF16)], "d_n3" + tag,
            epilogue=pre_mlp_bwd, rows_in=(sv["h1"], dh, sv["mix"]), consts=(lw["g3"], lw["g2"]), accs=[vec, vec],
            row_split=n_sub)
        gw_out = _mm(sv["ymix"], dmix, "tn", D_MODEL, D_MODEL, tm_ff, [(D_MODEL, F32)], "dw_out" + tag)[0]
        dymix = _mm(dmix, lw["w_out"], "nt", 768, D_MODEL, D_MODEL, [(D_MODEL, F32)], "d_ymix" + tag)[0]
        dh_lru, dgate, do, dgo, raw["head_norm"] = _mix_bwd(
            dymix, sv["h_f"], sv["h_b"], sv["z_lru"], sv["o_f"], sv["o_b"], sv["z_go"], lw["head_norm"], t_rows,
            "mix_bwd" + tag)
        gla_grads = with_stage(lambda comm: _gla_bwd(sv["z_qk"], sv["z_v"], sv["z_zg"], lw["gla_wg"], lw["gla_bg"], do,
                                                     sv["s_f"], sv["s_b"], t_rows, "gla_bwd" + tag, comm=comm))
        dx_br, raw["conv_w"], raw["conv_b"], raw["lru_wg"], raw["lru_pb"] = _lru_bwd(
            sv["z_lru"], lw["conv_w"], lw["conv_b"], lw["lru_wg"], lw["lru_pb"], sv["h_f"], sv["h_b"], dh_lru,
            t_rows, seq, "lru_bwd" + tag)
        dz, dwg_gla, raw["gla_bg"] = _dz_assemble(dx_br, dgate, gla_grads, dgo, sv["z_zg"], lw["gla_wg"], t_rows,
                                                  "dz_assemble" + tag)
        raw["gla_wg"] = dwg_gla[0:2 * GLA_RANK]
        gw_in = with_stage(lambda comm: _mm(sv["n1"], dz, "tn", D_MODEL, 896, tm_ff, [(D_IN_PAD, F32)], "dw_in" + tag,
                                            comm=comm))[0]
        if l > 0:
            def pre_mix_bwd(acc, hh, d_h1, ff_prev, g1, g4_prev):
                dx, dg1 = _rms_bwd(hh, g1, acc)
                d_h = d_h1 + dx
                d_ff, dg4_prev = _rms_bwd(ff_prev, g4_prev, d_h)
                return d_h, d_ff, dg1, dg4_prev

            dh, dff, raw["g1"], dg4 = _mm(
                dz, lw["w_in"], "nt", tm_e, D_MODEL, D_IN_PAD, [(D_MODEL, F32), (D_MODEL, BF16)], "d_n1" + tag,
                epilogue=pre_mix_bwd, rows_in=(sv["h"], dh1, saved[l - 1]["ff"]),
                consts=(lw["g1"], layer_w[l - 1]["g4"]), accs=[vec, vec], row_split=n_sub)
        else:
            def pre_mix_bwd0(acc, hh, d_h1, g1):
                dx, dg1 = _rms_bwd(hh, g1, acc)
                return d_h1 + dx, dg1

            dh, raw["g1"] = _mm(dz, lw["w_in"], "nt", tm_e, D_MODEL, D_IN_PAD, [(D_MODEL, F32)], "d_n1" + tag,
                                epilogue=pre_mix_bwd0, rows_in=(sv["h"], dh1), consts=(lw["g1"],), accs=[vec],
                                row_split=n_sub)

        cols = D_IN // N_CHIPS
        in_sh = jnp.stack([gw_in[:, q * cols:(q + 1) * cols] for q in range(N_CHIPS)])
        meta_g = dh[FRONT - N_META:FRONT] if l == 0 else jnp.zeros(META_SHAPE, F32)
        small = _flat_cat([raw[n] for n, _ in SMALL_RAW] + [meta_g], small_len)
        pack = [in_sh.reshape(N_CHIPS, 2, D_MODEL // 2, cols),
                gw_out.reshape(N_CHIPS, 2, -1, D_MODEL),
                gw_up.reshape(N_CHIPS, 2, D_MODEL // 2, D_FF // N_CHIPS),
                gw_down.reshape(N_CHIPS, 2, -1, D_MODEL),
                small.reshape(N_CHIPS, 2, -1, FLAT_W)]
        if pend is not None:
            totals[l + 1] = pend.drain()
        pend = _Staged(_reduce_stages(pack, c_idx, tag), tag)
    totals[0] = pend.drain()

    grad_x = dh[FRONT:FRONT + seq][None]

    sm_all = _run_comm(_allgather_chips([totals[l][4] for l in range(DEPTH)]), "gather_small_grads")
    g_tot = {}
    per_layer = []
    for l in range(DEPTH):
        pieces = _split_flat(sm_all[l].reshape(-1), [s for _, s in SMALL_RAW] + [META_SHAPE])
        per_layer.append(dict(zip([n for n, _ in SMALL_RAW] + ["meta"], pieces)))
    stack = lambda f: jnp.stack([f(per_layer[l]) for l in range(DEPTH)])
    g_tot["meta_tokens"] = _my_shard(per_layer[0]["meta"], 1, chip)
    for n, key in (("norm_mix_pre", "g1"), ("norm_mix_post", "g2"), ("norm_mlp_pre", "g3"), ("norm_mlp_post", "g4"),
                   ("conv_b", "conv_b"), ("gla_head_norm", "head_norm")):
        g_tot[n] = stack(lambda d, key=key: d[key][0])
    g_tot["conv_w"] = _my_shard(stack(lambda d: d["conv_w"]), 2, chip)
    for o, n in enumerate(("lru_wa_f", "lru_wx_f", "lru_wa_b", "lru_wx_b")):
        g_tot[n] = stack(lambda d, o=o: _gate_diag_blocks(d["lru_wg"], o))
    for row, n in enumerate(("lru_ba_f", "lru_bx_f", "lru_ba_b", "lru_bx_b", "lru_lambda_f", "lru_lambda_b")):
        g_tot[n] = stack(lambda d, row=row: d["lru_pb"][row])
    g_tot["gla_wg_f"] = _my_shard(stack(lambda d: d["gla_wg"][0:GLA_RANK, 0:QK]), 2, chip)
    g_tot["gla_wg_b"] = _my_shard(stack(lambda d: d["gla_wg"][GLA_RANK:, QK:]), 2, chip)
    g_tot["gla_bg_f"] = stack(lambda d: d["gla_bg"][0, 0:QK])
    g_tot["gla_bg_b"] = stack(lambda d: d["gla_bg"][0, QK:])

    delta, new_m, new_v = {}, {}, {}
    for a, n in enumerate(BIG):
        shp = w_loc[n].shape
        flat3 = lambda t, shp=shp: t.reshape(DEPTH, -1, shp[-1])
        g_l = [totals[l][a].reshape(-1, shp[-1]) for l in range(DEPTH)]
        outs = _adamw_layers(flat3(w_loc[n]), flat3(m_loc[n]), flat3(v_loc[n]), g_l, "adamw_" + n)
        g_tot[n], delta[n], new_m[n], new_v[n] = [o.reshape(shp) for o in outs]
    outs = _adamw_small(*[[d[n] for n in SMALL] for d in (w_loc, g_tot, m_loc, v_loc)])
    for d, o in zip((delta, new_m, new_v), outs):
        d.update(zip(SMALL, o))
    return (loss, grad_x, *[g_tot[n] for n in WEIGHTS], *[delta[n] for n in WEIGHTS],
            *[new_m[n] for n in WEIGHTS], *[new_v[n] for n in WEIGHTS])
```

```python
import math

import jax
import jax.numpy as jnp
from jax import lax
from jax.experimental import pallas as pl
from jax.experimental.pallas import tpu as pltpu

F32 = jnp.float32
BF16 = jnp.bfloat16

D_MODEL = 1024
DEPTH = 2
N_META = 16
LRU_W = 512
LRU_HEADS = 8
LRU_HD = 64
LRU_C = 8.0
GLA_W = 512
GLA_H = 4
GLA_DK = 64
GLA_DV = 128
GLA_RANK = 16
GLA_GATE_NORM = 16.0
D_FF = 4096
D_IN = 2592
EPS = 1e-6
ADAM_LR, ADAM_B1, ADAM_B2, ADAM_EPS, ADAM_WD, ADAM_STEP = 0.001, 0.9, 0.999, 1e-08, 0.01, 10

LANES = 128
SUBLANES = 8
FRONT = 128
BACK = 128
D_IN_PAD = 2688
ZG_OFF = 2560
CHUNK = 64
EXP_CLAMP = 80.0
VMEM_LIMIT = 56 * 1024 * 1024
MESH_ID = pl.DeviceIdType.MESH

NN = (((1,), (0,)), ((), ()))
NT = (((1,), (1,)), ((), ()))
TN = (((0,), (0,)), ((), ()))


def _dot(a, b, dims=NN):
    return lax.dot_general(a, b, dims, preferred_element_type=F32)


def _cparams(sem):
    return pltpu.CompilerParams(dimension_semantics=sem, vmem_limit_bytes=VMEM_LIMIT)


def _sigmoid(x):
    return 1.0 / (1.0 + jnp.exp(-x))


def _gelu(x):
    c = math.sqrt(2.0 / math.pi)
    return 0.5 * x * (1.0 + jnp.tanh(c * (x + 0.044715 * x * x * x)))


def _gelu_grad(x):
    c = math.sqrt(2.0 / math.pi)
    t = jnp.tanh(c * (x + 0.044715 * x * x * x))
    return 0.5 * (1.0 + t) + 0.5 * x * (1.0 - t * t) * c * (1.0 + 3.0 * 0.044715 * x * x)


def _rms_fwd(x, g):
    rstd = lax.rsqrt(jnp.mean(x * x, axis=1, keepdims=True) + EPS)
    return (x * rstd) * g


def _rms_bwd(x, g, dy):
    rstd = lax.rsqrt(jnp.mean(x * x, axis=1, keepdims=True) + EPS)
    xh = x * rstd
    dxh = dy * g
    dx = rstd * (dxh - xh * jnp.mean(dxh * xh, axis=1, keepdims=True))
    dg = jnp.sum(dy * xh, axis=0, keepdims=True)
    return dx, dg


def _mm(a, b, mode, tm, tn, tk, outs, name, epilogue=None, extras=(), col_blocks_first=False,
        rows_in=(), consts=(), accs=(), comm=None, row_split=1):
    if mode == "nn":
        (m, k), n = a.shape, b.shape[1]
        a_spec = pl.BlockSpec((tm, tk), lambda i, j, kk: (i, kk))
        b_spec = pl.BlockSpec((tk, tn), lambda i, j, kk: (kk, j))
        dims = NN
    elif mode == "nt":
        (m, k), n = a.shape, b.shape[0]
        a_spec = pl.BlockSpec((tm, tk), lambda i, j, kk: (i, kk))
        b_spec = pl.BlockSpec((tn, tk), lambda i, j, kk: (j, kk))
        dims = NT
    else:
        (k, m), n = a.shape, b.shape[1]
        a_spec = pl.BlockSpec((tk, tm), lambda i, j, kk: (kk, i))
        b_spec = pl.BlockSpec((tk, tn), lambda i, j, kk: (kk, j))
        dims = TN
    assert m % tm == 0 and n % tn == 0 and k % tk == 0, (name, m, n, k)
    nk = k // tk
    if n == tn and nk == 1:
        b_spec = pl.BlockSpec(b_spec.block_shape, b_spec.index_map, pipeline_mode=pl.Buffered(1))
    ne = len(extras) + len(rows_in) + len(consts)
    e_specs = [pl.BlockSpec((tm, tn), lambda i, j, kk: (i, j)) for _ in extras]
    e_specs += [pl.BlockSpec((tm, r.shape[1]), lambda i, j, kk: (i, 0)) for r in rows_in]
    e_specs += [pl.BlockSpec(c.shape, lambda i, j, kk: (0, 0)) for c in consts]
    n_out = len(outs)
    o_specs, o_shapes = [], []
    for width, dtype in outs:
        if col_blocks_first:
            assert width == n, name
            o_specs.append(pl.BlockSpec((None, tm, tn), lambda i, j, kk: (j, i, 0)))
            o_shapes.append(jax.ShapeDtypeStruct((n // tn, m, tn), dtype))
            continue
        if width == n:
            o_specs.append(pl.BlockSpec((tm, tn), lambda i, j, kk: (i, j)))
        else:
            assert n == tn, name
            o_specs.append(pl.BlockSpec((tm, width), lambda i, j, kk: (i, 0)))
        o_shapes.append(jax.ShapeDtypeStruct((m, width), dtype))
    o_specs += [pl.BlockSpec(s, lambda i, j, kk: (0, 0)) for s, _ in accs]
    o_shapes += [jax.ShapeDtypeStruct(s, d) for s, d in accs]

    n_tiled = len(extras) + len(rows_in)

    def finish(acc, extra_refs, out_refs, rs=slice(None), zero_accs=True):
        tiles = [e[rs, :] for e in extra_refs[:n_tiled]] + [e[...] for e in extra_refs[n_tiled:]]
        res = epilogue(acc, *tiles) if epilogue else (acc,)
        for o, r in zip(out_refs[:n_out], res[:n_out]):
            o[rs, :] = r.astype(o.dtype)
        first = jnp.logical_and(pl.program_id(0) == 0, pl.program_id(1) == 0)
        for o, r in zip(out_refs[n_out:], res[n_out:]):
            if zero_accs:
                @pl.when(first)
                def _(o=o):
                    o[...] = jnp.zeros_like(o)

            o[...] += r

    if nk == 1 and row_split > 1:
        assert mode != "tn" and tm % row_split == 0, name
        sub = tm // row_split

        def body(a_ref, b_ref, *rest):
            for s in range(row_split):
                rs = slice(s * sub, (s + 1) * sub)
                finish(_dot(a_ref[rs, :], b_ref[...], dims), rest[:ne], rest[ne:], rs, zero_accs=(s == 0))
        scratch = []
    elif nk == 1:
        def body(a_ref, b_ref, *rest):
            finish(_dot(a_ref[...], b_ref[...], dims), rest[:ne], rest[ne:])
        scratch = []
    else:
        def body(a_ref, b_ref, *rest):
            acc_ref = rest[-1]
            kk = pl.program_id(2)

            @pl.when(kk == 0)
            def _():
                acc_ref[...] = jnp.zeros_like(acc_ref)

            acc_ref[...] += _dot(a_ref[...], b_ref[...], dims)

            @pl.when(kk == nk - 1)
            def _():
                finish(acc_ref[...], rest[:ne], rest[ne:-1])
        scratch = [pltpu.VMEM((tm, tn), F32)]

    sem = ("arbitrary", "arbitrary", "arbitrary") if accs else ("parallel", "parallel", "arbitrary")
    res, comm_res = _pcall(body, name, (m // tm, n // tn, nk), [a_spec, b_spec] + e_specs, o_specs, o_shapes, scratch,
                           sem, (a, b, *extras, *rows_in, *consts), comm)
    return res if comm is None else (res, comm_res)


def _rows(body, n_rows, tm, ins, consts, outs, accs, name):
    assert n_rows % tm == 0, (name, n_rows, tm)
    in_specs = [pl.BlockSpec((tm, w), (lambda i, cb=cb: (i, cb))) for _, w, cb in ins]
    in_specs += [pl.BlockSpec(c.shape, lambda i: (0, 0)) for c in consts]
    out_specs = [pl.BlockSpec((tm, w), lambda i: (i, 0)) for w, _ in outs]
    out_specs += [pl.BlockSpec(s, lambda i: (0, 0)) for s, _ in accs]
    out_shape = [jax.ShapeDtypeStruct((n_rows, w), d) for w, d in outs]
    out_shape += [jax.ShapeDtypeStruct(s, d) for s, d in accs]
    ni, nc, no = len(ins), len(consts), len(outs)

    def kern(*refs):
        body(pl.program_id(0), refs[:ni], refs[ni:ni + nc], refs[ni + nc:ni + nc + no], refs[ni + nc + no:])

    res = pl.pallas_call(
        kern, name=name, grid=(n_rows // tm,), in_specs=in_specs, out_specs=out_specs, out_shape=out_shape,
        compiler_params=_cparams(("arbitrary",)),
    )(*[a for a, _, _ in ins], *consts)
    return res


def _acc_add(i, ref, val):
    @pl.when(i == 0)
    def _():
        ref[...] = jnp.zeros_like(ref)

    ref[...] += val


def _norm_cast(h, g, t_rows, name):
    def body(i, ins, consts, outs, accs):
        outs[0][...] = _rms_fwd(ins[0][...], consts[0][...]).astype(BF16)
    return _rows(body, t_rows, 384, [(h, D_MODEL, 0)], [g], [(D_MODEL, BF16)], [], name)[0]


def _head_norm_parts(o):
    ns, rs = [], []
    for hd in range(GLA_H):
        oh = o[:, hd * GLA_DV:(hd + 1) * GLA_DV]
        r = lax.rsqrt(jnp.mean(oh * oh, axis=1, keepdims=True) + EPS)
        ns.append(oh * r)
        rs.append(r)
    return ns, rs


def _mix_fwd(h_f, h_b, z_lru, o_f, o_b, z_go, head_norm, t_rows, name):
    def body(i, ins, consts, outs, accs):
        hf, hb, gate, of, ob, go = [r[...] for r in ins]
        hn = consts[0][...]
        outs[0][:, 0:LRU_W] = ((hf + hb) * _gelu(gate)).astype(BF16)
        ns, _ = _head_norm_parts(of + ob)
        silu = go * _sigmoid(go)
        for hd in range(GLA_H):
            sl = slice(hd * GLA_DV, (hd + 1) * GLA_DV)
            outs[0][:, LRU_W + hd * GLA_DV:LRU_W + (hd + 1) * GLA_DV] = (ns[hd] * hn[:, sl] * silu[:, sl]).astype(BF16)
    ins = [(h_f, LRU_W, 0), (h_b, LRU_W, 0), (z_lru, LRU_W, 1), (o_f, GLA_W, 0), (o_b, GLA_W, 0), (z_go, GLA_W, 0)]
    return _rows(body, t_rows, 384, ins, [head_norm], [(D_MODEL, BF16)], [], name)[0]


def _mix_bwd(dymix, h_f, h_b, z_lru, o_f, o_b, z_go, head_norm, t_rows, name):
    def body(i, ins, consts, outs, accs):
        dyl, dyg, hf, hb, gate, of, ob, go = [r[...] for r in ins]
        hn = consts[0][...]
        outs[0][...] = dyl * _gelu(gate)
        outs[1][...] = dyl * (hf + hb) * _gelu_grad(gate)
        ns, rs = _head_norm_parts(of + ob)
        sg = _sigmoid(go)
        silu = go * sg
        dsilu = sg * (1.0 + go * (1.0 - sg))
        dhn = []
        for hd in range(GLA_H):
            sl = slice(hd * GLA_DV, (hd + 1) * GLA_DV)
            dy = dyg[:, sl]
            e = dy * hn[:, sl] * silu[:, sl]
            outs[2][:, sl] = rs[hd] * (e - ns[hd] * jnp.mean(e * ns[hd], axis=1, keepdims=True))
            outs[3][:, sl] = dy * ns[hd] * hn[:, sl] * dsilu[:, sl]
            dhn.append(jnp.sum(dy * ns[hd] * silu[:, sl], axis=0, keepdims=True))
        _acc_add(i, accs[0], jnp.concatenate(dhn, axis=1))
    ins = [(dymix, LRU_W, 0), (dymix, GLA_W, 1), (h_f, LRU_W, 0), (h_b, LRU_W, 0), (z_lru, LRU_W, 1),
           (o_f, GLA_W, 0), (o_b, GLA_W, 0), (z_go, GLA_W, 0)]
    return _rows(body, t_rows, 384, ins, [head_norm],
                 [(LRU_W, F32), (LRU_W, F32), (GLA_W, F32), (GLA_W, F32)], [((1, GLA_W), F32)], name)


def _dz_assemble(dx_br, dgate, gla_grads, dgo, z_zg, wg, t_rows, name):
    dq_f, dk_f, dv_f, dpre_f, dq_b, dk_b, dv_b, dpre_b = gla_grads
    qk = GLA_H * GLA_DK

    def body(i, ins, consts, outs, accs):
        (dxb, dgt, dqf, dqb, dkf, dkb, dvf, dvb, dg_o, dpf, dpb, zg) = [r[...] for r in ins]
        w = consts[0][...]
        o = outs[0]
        o[:, 0:LRU_W] = dxb.astype(BF16)
        o[:, LRU_W:2 * LRU_W] = dgt.astype(BF16)
        o[:, 1024:1024 + qk] = ((dqf + dqb) * (GLA_DK ** -0.5)).astype(BF16)
        o[:, 1280:1280 + qk] = (dkf + dkb).astype(BF16)
        o[:, 1536:1536 + GLA_W] = (dvf + dvb).astype(BF16)
        o[:, 2048:2048 + GLA_W] = dg_o.astype(BF16)
        dpre = jnp.concatenate([dpf, dpb], axis=1)
        dpre16 = dpre.astype(BF16)
        o[:, ZG_OFF:ZG_OFF + LANES] = _dot(dpre16, w, NT).astype(BF16)
        _acc_add(i, accs[0], _dot(zg.astype(BF16), dpre16, TN))
        _acc_add(i, accs[1], jnp.sum(dpre, axis=0, keepdims=True))

    ins = [(dx_br, LRU_W, 0), (dgate, LRU_W, 0), (dq_f, qk, 0), (dq_b, qk, 0), (dk_f, qk, 0), (dk_b, qk, 0),
           (dv_f, GLA_W, 0), (dv_b, GLA_W, 0), (dgo, GLA_W, 0), (dpre_f, qk, 0), (dpre_b, qk, 0), (z_zg, LANES, 0)]
    return _rows(body, t_rows, 384, ins, [wg], [(D_IN_PAD, BF16)],
                 [((LANES, 2 * qk), F32), ((1, 2 * qk), F32)], name)


LRU_RB = 384
HALO = SUBLANES


def _scan8(a, u, rev):
    rows = lax.broadcasted_iota(jnp.int32, a.shape, 0)
    for d in (1, 2, 4):
        if rev:
            a_s, u_s, ok = pltpu.roll(a, SUBLANES - d, 0), pltpu.roll(u, SUBLANES - d, 0), rows < SUBLANES - d
        else:
            a_s, u_s, ok = pltpu.roll(a, d, 0), pltpu.roll(u, d, 0), rows >= d
        u = jnp.where(ok, a * u_s + u, u)
        a = jnp.where(ok, a * a_s, a)
    return a, u


def _edge_row(x, rev):
    r = x[0:1, :] if rev else x[SUBLANES - 1:SUBLANES, :]
    return jnp.broadcast_to(r, x.shape)


def _scan_block(a, u, carry, rev):
    nsub = a.shape[0] // SUBLANES
    loc = [_scan8(a[sb * SUBLANES:(sb + 1) * SUBLANES], u[sb * SUBLANES:(sb + 1) * SUBLANES], rev) for sb in range(nsub)]
    edges = [(_edge_row(a_c, rev), _edge_row(h_l, rev)) for a_c, h_l in loc]
    carries = [None] * nsub
    for sb in (range(nsub - 1, -1, -1) if rev else range(nsub)):
        carries[sb] = carry
        carry = edges[sb][0] * carry + edges[sb][1]
    h = jnp.concatenate([h_l + a_c * cin for (a_c, h_l), cin in zip(loc, carries)], axis=0)
    return h, carry


def _lru_gates(xc, pre_a, pre_x, ba, bx, sp8):
    r = _sigmoid(pre_a + ba)
    i = _sigmoid(pre_x + bx)
    log_a = -(r * sp8)
    a = jnp.exp(log_a)
    y = 2.0 * log_a
    series = -y * (1.0 + y * (0.5 + y * (1.0 / 6.0 + y * (1.0 / 24.0 + y * (1.0 / 120.0)))))
    om = jnp.where(y > -0.25, series, 1.0 - a * a)
    s = jnp.sqrt(om)
    return r, i, a, s


def _softplus(x):
    return jnp.maximum(x, 0.0) + jnp.log(1.0 + jnp.exp(-jnp.abs(x)))


def _conv_taps(xpad_ref, r0, nrows, shifts=(-2, -1, 0, 1)):
    ext = xpad_ref[pl.ds(r0, nrows + 2 * HALO), :]
    taps = []
    for s in shifts:
        sh = ext if s == 0 else pltpu.roll(ext, (-s) % (nrows + 2 * HALO), 0)
        taps.append(sh[HALO:HALO + nrows])
    return taps


def _row_mask(r0, nrows, seq):
    rows = r0 + lax.broadcasted_iota(jnp.int32, (nrows, LANES), 0)
    return jnp.logical_and(rows >= FRONT - N_META, rows < FRONT + seq).astype(F32)


def _lru_load_conv(z_hbm, xpad, xc, sem, cw_ref, cb_ref, t_rows, seq):
    j = pl.program_id(0)
    zeros = jnp.zeros((HALO, LANES), F32)
    xpad[0:HALO, :] = zeros
    xpad[t_rows + HALO:t_rows + 2 * HALO, :] = zeros
    cp = pltpu.make_async_copy(z_hbm.at[:, pl.ds(pl.multiple_of(j * LANES, LANES), LANES)],
                               xpad.at[pl.ds(HALO, t_rows)], sem)
    cp.start()
    cp.wait()
    cw = cw_ref[...]
    cb = cb_ref[...]

    def conv_blk(r, carry):
        r0 = pl.multiple_of(r * LRU_RB, LRU_RB)
        taps = _conv_taps(xpad, r0, LRU_RB)
        acc = cb + cw[0:1, :] * taps[0]
        for k in range(1, 4):
            acc = acc + cw[k:k + 1, :] * taps[k]
        xc[pl.ds(r0, LRU_RB), :] = acc * _row_mask(r0, LRU_RB, seq)
        return carry

    lax.fori_loop(0, t_rows // LRU_RB, conv_blk, 0)


def _lru_specs(t_rows):
    return [pl.BlockSpec(memory_space=pl.ANY),
            pl.BlockSpec((4, LANES), lambda j: (0, j)),
            pl.BlockSpec((1, LANES), lambda j: (0, j)),
            pl.BlockSpec((1, LANES, 4 * LANES), lambda j: (j, 0, 0)),
            pl.BlockSpec((SUBLANES, LANES), lambda j: (0, j))]


def _lru_fwd(z_lru, conv_w, conv_b, wg, pb, t_rows, seq, name, comm=None):
    nblk = t_rows // LRU_RB
    nsub = LRU_RB // SUBLANES

    def kern(z_hbm, cw_ref, cb_ref, wg_ref, pb_ref, hf_ref, hb_ref, xpad, xc, sem):
        _lru_load_conv(z_hbm, xpad, xc, sem, cw_ref, cb_ref, t_rows, seq)
        w = wg_ref[0]
        pb_v = pb_ref[...]
        dirs = []
        for rev in (False, True):
            o = 2 if rev else 0
            dirs.append(dict(rev=rev, ba=pb_v[o:o + 1, :], bx=pb_v[o + 1:o + 2, :],
                             sp8=LRU_C * _softplus(-pb_v[5:6, :] if rev else -pb_v[4:5, :]),
                             w=w[:, o * LANES:(o + 2) * LANES], out=hb_ref if rev else hf_ref))

        def blk(it, carries):
            new = []
            for d, carry in zip(dirs, carries):
                r = (nblk - 1 - it) if d["rev"] else it
                r0 = pl.multiple_of(r * LRU_RB, LRU_RB)
                xcb = xc[pl.ds(r0, LRU_RB), :]
                pre = _dot(xcb.astype(BF16), d["w"])
                _, gi, a, s = _lru_gates(xcb, pre[:, 0:LANES], pre[:, LANES:2 * LANES], d["ba"], d["bx"], d["sp8"])
                h, carry = _scan_block(a, s * (gi * xcb), carry, d["rev"])
                d["out"][pl.ds(r0, LRU_RB), :] = h
                new.append(carry)
            return tuple(new)

        z8 = jnp.zeros((SUBLANES, LANES), F32)
        lax.fori_loop(0, nblk, blk, (z8, z8), unroll=2)

    res, comm_res = _pcall(
        kern, name, (LRU_W // LANES,), _lru_specs(t_rows), [pl.BlockSpec((t_rows, LANES), lambda j: (0, j))] * 2,
        [jax.ShapeDtypeStruct((t_rows, LRU_W), F32)] * 2,
        [pltpu.VMEM((t_rows + 2 * HALO, LANES), F32), pltpu.VMEM((t_rows, LANES), F32), pltpu.SemaphoreType.DMA],
        ("arbitrary",), (z_lru, conv_w, conv_b, wg, pb), comm)
    return res if comm is None else (res, comm_res)


def _lru_bwd(z_lru, conv_w, conv_b, wg, pb, h_f, h_b, dh, t_rows, seq, name, comm=None):
    nblk = t_rows // LRU_RB
    nsub = LRU_RB // SUBLANES

    def kern(z_hbm, cw_ref, cb_ref, wg_ref, pb_ref, hf_ref, hb_ref, dh_ref,
             dx_ref, dcw_ref, dcb_ref, dwh_ref, dpb_ref, xpad, xc, dxc, dwg, sem):
        _lru_load_conv(z_hbm, xpad, xc, sem, cw_ref, cb_ref, t_rows, seq)
        w = wg_ref[0]
        pb_v = pb_ref[...]
        zeros = jnp.zeros((HALO, LANES), F32)
        dxc[0:HALO, :] = zeros
        dxc[t_rows + HALO:t_rows + 2 * HALO, :] = zeros
        dwg[...] = jnp.zeros_like(dwg)

        def zero_blk(r, carry):
            dxc[pl.ds(pl.multiple_of(r * LRU_RB, LRU_RB) + HALO, LRU_RB), :] = jnp.zeros((LRU_RB, LANES), F32)
            return carry

        lax.fori_loop(0, nblk, zero_blk, 0)
        dirs = []
        for rev in (False, True):
            o = 2 if rev else 0
            lam = pb_v[5:6, :] if rev else pb_v[4:5, :]
            dirs.append(dict(rev=rev, o=o, ba=pb_v[o:o + 1, :], bx=pb_v[o + 1:o + 2, :], sp8=LRU_C * _softplus(-lam),
                             dlam_scale=LRU_C * _sigmoid(-lam), h_ref=hb_ref if rev else hf_ref,
                             w=w[:, o * LANES:(o + 2) * LANES]))
        rows_b = lax.broadcasted_iota(jnp.int32, (LRU_RB, LANES), 0)

        def blk(it, carries):
            new = []
            for d, (nu_c, acc_ba, acc_bx, acc_lam) in zip(dirs, carries):
                rev, o, h_ref = d["rev"], d["o"], d["h_ref"]
                adj_rev = not rev
                r = (nblk - 1 - it) if adj_rev else it
                r0 = pl.multiple_of(r * LRU_RB, LRU_RB)
                xcb = xc[pl.ds(r0, LRU_RB), :]
                xc16 = xcb.astype(BF16)
                pre = _dot(xc16, d["w"])
                gr, gi, a, s = _lru_gates(xcb, pre[:, 0:LANES], pre[:, LANES:2 * LANES], d["ba"], d["bx"], d["sp8"])
                dhb = dh_ref[pl.ds(r0, LRU_RB), :]
                hb = h_ref[pl.ds(r0, LRU_RB), :]
                if rev:
                    nxt0 = pl.multiple_of(jnp.minimum(r0 + LRU_RB, t_rows - SUBLANES), SUBLANES)
                    edge = h_ref[pl.ds(nxt0, SUBLANES), :][0:1, :] * (r < nblk - 1).astype(F32)
                    h_prev = jnp.concatenate([hb[1:], edge], axis=0)
                else:
                    prv0 = pl.multiple_of(jnp.maximum(r0 - SUBLANES, 0), SUBLANES)
                    edge = h_ref[pl.ds(prv0, SUBLANES), :][SUBLANES - 1:SUBLANES, :] * (r > 0).astype(F32)
                    h_prev = jnp.concatenate([edge, hb[:-1]], axis=0)
                nu, nu_out = _scan_block(a, a * dhb, nu_c, adj_rev)
                cin = jnp.broadcast_to(nu_c[0:1, :], (LRU_RB, LANES))
                if adj_rev:
                    nu_next = jnp.where(rows_b == LRU_RB - 1, cin, pltpu.roll(nu, LRU_RB - 1, 0))
                else:
                    nu_next = jnp.where(rows_b == 0, cin, pltpu.roll(nu, 1, 0))
                du = dhb + nu_next
                da = du * h_prev
                ix = gi * xcb
                dlog = da * a - du * ix * (a * a) / s
                d_i = du * s * xcb
                dr = -dlog * d["sp8"]
                dpa = dr * gr * (1.0 - gr)
                dpx = d_i * gi * (1.0 - gi)
                dp16 = jnp.concatenate([dpa, dpx], axis=1).astype(BF16)
                dxc[pl.ds(r0 + HALO, LRU_RB), :] += du * s * gi + _dot(dp16, d["w"], NT)
                dwg[:, o * LANES:(o + 2) * LANES] += _dot(xc16, dp16, TN)
                new.append((nu_out, acc_ba + jnp.sum(dpa, axis=0, keepdims=True),
                            acc_bx + jnp.sum(dpx, axis=0, keepdims=True),
                            acc_lam + jnp.sum(dlog * gr, axis=0, keepdims=True)))
            return tuple(new)

        z1 = jnp.zeros((1, LANES), F32)
        init = (jnp.zeros((SUBLANES, LANES), F32), z1, z1, z1)
        (_, ba_f, bx_f, lam_f), (_, ba_b, bx_b, lam_b) = lax.fori_loop(0, nblk, blk, (init, init), unroll=2)
        dpb_ref[...] = jnp.concatenate([ba_f, bx_f, ba_b, bx_b, lam_f * dirs[0]["dlam_scale"],
                                        lam_b * dirs[1]["dlam_scale"], z1, z1], axis=0)
        for hh in range(2):
            for o in range(4):
                c0 = o * LANES + hh * LRU_HD
                dwh_ref[hh, :, o * LRU_HD:(o + 1) * LRU_HD] = dwg[hh * LRU_HD:(hh + 1) * LRU_HD, c0:c0 + LRU_HD]

        def mask_blk(r, carry):
            r0 = pl.multiple_of(r * LRU_RB, LRU_RB)
            dxc[pl.ds(r0 + HALO, LRU_RB), :] = dxc[pl.ds(r0 + HALO, LRU_RB), :] * _row_mask(r0, LRU_RB, seq)
            return carry

        lax.fori_loop(0, nblk, mask_blk, 0)

        cw = cw_ref[...]

        def conv_bwd(r, carry):
            r0 = pl.multiple_of(r * LRU_RB, LRU_RB)
            gt = _conv_taps(dxc, r0, LRU_RB, (2, 1, 0, -1))
            xt = _conv_taps(xpad, r0, LRU_RB)
            dx_ref[pl.ds(r0, LRU_RB), :] = (cw[0:1, :] * gt[0] + cw[1:2, :] * gt[1] + cw[2:3, :] * gt[2]
                                           + cw[3:4, :] * gt[3])
            g = gt[2]
            new = [carry[k] + jnp.sum(g * xt[k], axis=0, keepdims=True) for k in range(4)]
            new.append(carry[4] + jnp.sum(g, axis=0, keepdims=True))
            return tuple(new)

        z1 = jnp.zeros((1, LANES), F32)
        sums = lax.fori_loop(0, nblk, conv_bwd, (z1, z1, z1, z1, z1))
        dcw_ref[...] = jnp.concatenate(sums[:4], axis=0)
        dcb_ref[...] = sums[4]

    col = lambda j: (0, j)
    res, comm_res = _pcall(
        kern, name, (LRU_W // LANES,), _lru_specs(t_rows) + [pl.BlockSpec((t_rows, LANES), col)] * 3,
        [pl.BlockSpec((t_rows, LANES), col), pl.BlockSpec((4, LANES), col), pl.BlockSpec((1, LANES), col),
         pl.BlockSpec((2, LRU_HD, 4 * LRU_HD), lambda j: (j, 0, 0)), pl.BlockSpec((SUBLANES, LANES), col)],
        [jax.ShapeDtypeStruct((t_rows, LRU_W), F32), jax.ShapeDtypeStruct((4, LRU_W), F32),
         jax.ShapeDtypeStruct((1, LRU_W), F32), jax.ShapeDtypeStruct((LRU_HEADS, LRU_HD, 4 * LRU_HD), F32),
         jax.ShapeDtypeStruct((SUBLANES, LRU_W), F32)],
        [pltpu.VMEM((t_rows + 2 * HALO, LANES), F32), pltpu.VMEM((t_rows, LANES), F32),
         pltpu.VMEM((t_rows + 2 * HALO, LANES), F32), pltpu.VMEM((LANES, 4 * LANES), F32), pltpu.SemaphoreType.DMA],
        ("arbitrary",), (z_lru, conv_w, conv_b, wg, pb, h_f, h_b, dh), comm)
    return res if comm is None else (res, comm_res)


QK = GLA_H * GLA_DK
GLA_G = 4


def _cumsum_rows(x, rev):
    n = x.shape[0]
    rows = lax.broadcasted_iota(jnp.int32, x.shape, 0)
    d = 1
    while d < n:
        if rev:
            x = x + jnp.where(rows < n - d, pltpu.roll(x, n - d, 0), 0.0)
        else:
            x = x + jnp.where(rows >= d, pltpu.roll(x, d, 0), 0.0)
        d *= 2
    return x


def _gla_chunk_terms(q, k, zg, wg, bg, rev):
    pre = (_dot(zg.astype(BF16), wg) + bg)[:, (QK if rev else 0):(2 * QK if rev else QK)]
    g = (jnp.minimum(pre, 0.0) - jnp.log(1.0 + jnp.exp(-jnp.abs(pre)))) * (1.0 / GLA_GATE_NORM)
    b = _cumsum_rows(g, rev)
    b_last = b[0:1, :] if rev else b[CHUNK - 1:CHUNK, :]
    b_mid = b[CHUNK // 2:CHUNK // 2 + 1, :]
    qs = q * (GLA_DK ** -0.5)
    terms = dict(
        pre=pre, qs=qs, k=k,
        eb=jnp.exp(b), ek=jnp.exp(b_last - b), e_last=jnp.exp(b_last),
        eqm=jnp.exp(jnp.minimum(b - b_mid, EXP_CLAMP)), ekm=jnp.exp(jnp.minimum(b_mid - b, EXP_CLAMP)))
    return terms


def _gla_mask(rev):
    t = lax.broadcasted_iota(jnp.int32, (CHUNK, CHUNK), 0)
    s = lax.broadcasted_iota(jnp.int32, (CHUNK, CHUNK), 1)
    return (s > t) if rev else (s <= t)


def _gla_head_ops(tm, hd):
    sl = slice(hd * GLA_DK, (hd + 1) * GLA_DK)
    q_b = (tm["qs"][:, sl] * tm["eb"][:, sl]).astype(BF16)
    k_e = (tm["k"][:, sl] * tm["ek"][:, sl]).astype(BF16)
    q_m = (tm["qs"][:, sl] * tm["eqm"][:, sl]).astype(BF16)
    k_m = (tm["k"][:, sl] * tm["ekm"][:, sl]).astype(BF16)
    return sl, q_b, k_e, q_m, k_m


def _gla_fwd(z_qk, z_v, z_zg, wg, bg, t_rows, name, comm=None):
    nc = t_rows // CHUNK
    ng = nc // GLA_G
    rows = CHUNK * GLA_G

    def kern(qf, kf, vf, zf, qb, kb, vb, zb, wg_ref, bg_ref, of_ref, ob_ref, sf_ref, sb_ref, st_f, st_b):
        i = pl.program_id(0)

        @pl.when(i == 0)
        def _():
            st_f[...] = jnp.zeros_like(st_f)
            st_b[...] = jnp.zeros_like(st_b)

        wg_v, bg_v = wg_ref[...], bg_ref[...]
        for rev, (q_r, k_r, v_r, z_r, o_r, s_r, st) in ((False, (qf, kf, vf, zf, of_ref, sf_ref, st_f)),
                                                        (True, (qb, kb, vb, zb, ob_ref, sb_ref, st_b))):
            mask = _gla_mask(rev)
            for ck in (range(GLA_G - 1, -1, -1) if rev else range(GLA_G)):
                cr = slice(ck * CHUNK, (ck + 1) * CHUNK)
                tm = _gla_chunk_terms(q_r[cr, :], k_r[cr, :], z_r[cr, :], wg_v, bg_v, rev)
                v = v_r[cr, :]
                for hd in range(GLA_H):
                    sl, q_b, k_e, q_m, k_m = _gla_head_ops(tm, hd)
                    vs = slice(hd * GLA_DV, (hd + 1) * GLA_DV)
                    v16 = v[:, vs].astype(BF16)
                    a = jnp.where(mask, _dot(q_m, k_m, NT), 0.0).astype(BF16)
                    s_in = st[hd]
                    s_r[ck, hd] = s_in
                    o_r[cr, vs] = _dot(a, v16) + _dot(q_b, s_in.astype(BF16), NT)
                    st[hd] = s_in * tm["e_last"][:, sl] + _dot(v16, k_e, TN)

    fw = lambda c: (lambda i: (i, c))
    rv = lambda c: (lambda i: (ng - 1 - i, c))
    def side(ix):
        return [pl.BlockSpec((rows, QK), ix(0)), pl.BlockSpec((rows, QK), ix(1)),
                pl.BlockSpec((rows, GLA_W), ix(0)), pl.BlockSpec((rows, LANES), ix(0))]
    st_shape = (nc, GLA_H, GLA_DV, GLA_DK)
    res, comm_res = _pcall(
        kern, name, (ng,),
        side(fw) + side(rv) + [pl.BlockSpec(wg.shape, lambda i: (0, 0)), pl.BlockSpec(bg.shape, lambda i: (0, 0))],
        [pl.BlockSpec((rows, GLA_W), fw(0)), pl.BlockSpec((rows, GLA_W), rv(0)),
         pl.BlockSpec((GLA_G,) + st_shape[1:], lambda i: (i, 0, 0, 0)),
         pl.BlockSpec((GLA_G,) + st_shape[1:], lambda i: (ng - 1 - i, 0, 0, 0))],
        [jax.ShapeDtypeStruct((t_rows, GLA_W), F32)] * 2 + [jax.ShapeDtypeStruct(st_shape, F32)] * 2,
        [pltpu.VMEM(st_shape[1:], F32)] * 2, ("arbitrary",),
        (z_qk, z_qk, z_v, z_zg, z_qk, z_qk, z_v, z_zg, wg, bg), comm)
    return res if comm is None else (res, comm_res)


def _gla_bwd(z_qk, z_v, z_zg, wg, bg, do, s_f, s_b, t_rows, name, comm=None):
    nc = t_rows // CHUNK

    def kern(qf, kf, vf, zf, dof, ssf, qb, kb, vb, zb, dob, ssb, wg_ref, bg_ref,
             dqf, dkf, dvf, dpf, dqb, dkb, dvb, dpb, ds_f, ds_b):
        i = pl.program_id(0)

        @pl.when(i == 0)
        def _():
            ds_f[...] = jnp.zeros_like(ds_f)
            ds_b[...] = jnp.zeros_like(ds_b)

        wg_v, bg_v = wg_ref[...], bg_ref[...]
        sides = ((False, (qf, kf, vf, zf, dof, ssf, dqf, dkf, dvf, dpf, ds_f)),
                 (True, (qb, kb, vb, zb, dob, ssb, dqb, dkb, dvb, dpb, ds_b)))
        for rev, (q_r, k_r, v_r, z_r, do_r, ss_r, dq_r, dk_r, dv_r, dp_r, ds) in sides:
            mask = _gla_mask(rev)
            for ck in (range(GLA_G) if rev else range(GLA_G - 1, -1, -1)):
                cr = slice(ck * CHUNK, (ck + 1) * CHUNK)
                tm = _gla_chunk_terms(q_r[cr, :], k_r[cr, :], z_r[cr, :], wg_v, bg_v, rev)
                v = v_r[cr, :]
                d_o = do_r[cr, :]
                db_parts, cross_parts = [], []
                for hd in range(GLA_H):
                    sl, q_b, k_e, q_m, k_m = _gla_head_ops(tm, hd)
                    vs = slice(hd * GLA_DV, (hd + 1) * GLA_DV)
                    v16, do16 = v[:, vs].astype(BF16), d_o[:, vs].astype(BF16)
                    a = jnp.where(mask, _dot(q_m, k_m, NT), 0.0).astype(BF16)
                    da = jnp.where(mask, _dot(do16, v16, NT), 0.0).astype(BF16)
                    s_in = ss_r[ck, hd]
                    s_in16 = s_in.astype(BF16)
                    ds_out = ds[hd]
                    ds16 = ds_out.astype(BF16)
                    dv_r[cr, vs] = _dot(a, do16, TN) + _dot(k_e, ds16, NT)
                    dqs = _dot(da, k_m) * tm["eqm"][:, sl] + _dot(do16, s_in16) * tm["eb"][:, sl]
                    dk = _dot(da, q_m, TN) * tm["ekm"][:, sl] + _dot(v16, ds16) * tm["ek"][:, sl]
                    ds[hd] = ds_out * tm["e_last"][:, sl] + _dot(do16, q_b, TN)
                    dq_r[cr, sl] = dqs
                    dk_r[cr, sl] = dk
                    db_parts.append(tm["qs"][:, sl] * dqs - tm["k"][:, sl] * dk)
                    s_out = s_in * tm["e_last"][:, sl] + _dot(v16, k_e, TN)
                    cross_parts.append(jnp.sum(s_out * ds_out, axis=0, keepdims=True))
                d_b = jnp.concatenate(db_parts, axis=1)
                dg = _cumsum_rows(d_b, not rev) + jnp.concatenate(cross_parts, axis=1)
                dp_r[cr, :] = dg * (1.0 / GLA_GATE_NORM) * _sigmoid(-tm["pre"])

    ng = nc // GLA_G
    rows = CHUNK * GLA_G
    fw = lambda c: (lambda i: (ng - 1 - i, c))
    rv = lambda c: (lambda i: (i, c))
    st_blk = (GLA_G, GLA_H, GLA_DV, GLA_DK)
    def side(ix, st_ix):
        return [pl.BlockSpec((rows, QK), ix(0)), pl.BlockSpec((rows, QK), ix(1)),
                pl.BlockSpec((rows, GLA_W), ix(0)), pl.BlockSpec((rows, LANES), ix(0)),
                pl.BlockSpec((rows, GLA_W), ix(0)), pl.BlockSpec(st_blk, st_ix)]
    def oside(ix):
        return [pl.BlockSpec((rows, QK), ix(0)), pl.BlockSpec((rows, QK), ix(0)),
                pl.BlockSpec((rows, GLA_W), ix(0)), pl.BlockSpec((rows, QK), ix(0))]
    o_shapes = [jax.ShapeDtypeStruct((t_rows, QK), F32), jax.ShapeDtypeStruct((t_rows, QK), F32),
                jax.ShapeDtypeStruct((t_rows, GLA_W), F32), jax.ShapeDtypeStruct((t_rows, QK), F32)]
    res, comm_res = _pcall(
        kern, name, (ng,),
        (side(fw, lambda i: (ng - 1 - i, 0, 0, 0)) + side(rv, lambda i: (i, 0, 0, 0))
         + [pl.BlockSpec(wg.shape, lambda i: (0, 0)), pl.BlockSpec(bg.shape, lambda i: (0, 0))]),
        oside(fw) + oside(rv), o_shapes * 2, [pltpu.VMEM((GLA_H, GLA_DV, GLA_DK), F32)] * 2, ("arbitrary",),
        (z_qk, z_qk, z_v, z_zg, do, s_f, z_qk, z_qk, z_v, z_zg, do, s_b, wg, bg), comm)
    return res if comm is None else (res, comm_res)


FLAT_W = 1024


def _adam_math(wv, gv, mv, vv):
    bc1 = 1.0 - ADAM_B1 ** ADAM_STEP
    bc2 = 1.0 - ADAM_B2 ** ADAM_STEP
    m_new = ADAM_B1 * mv + (1.0 - ADAM_B1) * gv
    v_new = ADAM_B2 * vv + (1.0 - ADAM_B2) * (gv * gv)
    delta = -ADAM_LR * ((m_new / bc1) / (jnp.sqrt(v_new / bc2) + ADAM_EPS) + ADAM_WD * wv)
    return delta, m_new, v_new


def _row_tile(rows, cap=256):
    t = cap
    while rows % t:
        t //= 2
    assert t >= SUBLANES, rows
    return t


def _adamw_layers(w, m, v, g_layers, name):
    depth, rows, cols = w.shape
    tr = _row_tile(rows)

    def kern(w_ref, m_ref, v_ref, *rest):
        g_refs, (go_ref, d_ref, mo_ref, vo_ref) = rest[:depth], rest[depth:]
        l = pl.program_id(0)
        gv = g_refs[0][...]
        for k in range(1, depth):
            gv = jnp.where(l == k, g_refs[k][...], gv)
        delta, m_new, v_new = _adam_math(w_ref[...], gv, m_ref[...], v_ref[...])
        go_ref[...] = gv
        d_ref[...] = delta
        mo_ref[...] = m_new
        vo_ref[...] = v_new

    stacked = pl.BlockSpec((None, tr, cols), lambda l, i: (l, i, 0))
    return pl.pallas_call(
        kern, name=name, grid=(depth, rows // tr),
        in_specs=[stacked] * 3 + [pl.BlockSpec((tr, cols), lambda l, i: (i, 0))] * depth,
        out_specs=[stacked] * 4, out_shape=[jax.ShapeDtypeStruct(w.shape, F32)] * 4,
        compiler_params=_cparams(("arbitrary", "arbitrary")),
    )(w, m, v, *g_layers)


def _adamw_small(ws, gs, ms, vs):
    n = len(ws)

    def kern(*refs):
        ins, outs = refs[:4 * n], refs[4 * n:]
        for k in range(n):
            delta, m_new, v_new = _adam_math(ins[k][...], ins[n + k][...], ins[2 * n + k][...], ins[3 * n + k][...])
            outs[k][...] = delta
            outs[n + k][...] = m_new
            outs[2 * n + k][...] = v_new

    shapes = [jax.ShapeDtypeStruct(w.shape, F32) for w in ws]
    res = pl.pallas_call(kern, name="adamw_small", out_shape=shapes * 3,
                         compiler_params=pltpu.CompilerParams(vmem_limit_bytes=VMEM_LIMIT))(*ws, *gs, *ms, *vs)
    return res[:n], res[n:2 * n], res[2 * n:]


def _add_sibling(g, got, c_idx, name):
    _, _, rows, cols = g.shape
    tr = _row_tile(rows)

    def kern(c_ref, g_ref, got_ref, o_ref):
        o_ref[...] = (g_ref[...] + got_ref[...]).astype(BF16)

    grid_spec = pltpu.PrefetchScalarGridSpec(
        num_scalar_prefetch=1, grid=(N_CHIPS, rows // tr),
        in_specs=[pl.BlockSpec((None, None, tr, cols), lambda q, i, c_ref: (q, c_ref[0], i, 0)),
                  pl.BlockSpec((None, tr, cols), lambda q, i, c_ref: (q, i, 0))],
        out_specs=pl.BlockSpec((None, tr, cols), lambda q, i, c_ref: (q, i, 0)))
    return pl.pallas_call(
        kern, name=name, grid_spec=grid_spec, out_shape=jax.ShapeDtypeStruct((N_CHIPS, rows, cols), BF16),
        compiler_params=_cparams(("arbitrary", "arbitrary")),
    )(c_idx, g, got)


def _sum_chips(x, name):
    _, rows, cols = x.shape
    tr = _row_tile(rows)

    def kern(x_ref, o_ref):
        xv = x_ref[...].astype(F32)
        o_ref[...] = ((xv[0] + xv[1]) + xv[2]) + xv[3]

    return pl.pallas_call(
        kern, name=name, grid=(rows // tr,),
        in_specs=[pl.BlockSpec((N_CHIPS, tr, cols), lambda i: (0, i, 0))],
        out_specs=pl.BlockSpec((tr, cols), lambda i: (i, 0)),
        out_shape=jax.ShapeDtypeStruct((rows, cols), F32),
        compiler_params=_cparams(("arbitrary",)),
    )(x)


def _place():
    x, y, c = lax.axis_index("x"), lax.axis_index("y"), lax.axis_index("c")
    return x, y, c


def _other_chips(x, y):
    return [(1 - x, y), (x, 1 - y), (1 - x, 1 - y)]


class _Comm:
    def __init__(self, ins, out_shapes, n_sems, stage, body):
        self.ins, self.out_shapes, self.body = list(ins), list(out_shapes), body
        self.scratch = [pltpu.SemaphoreType.DMA((n,)) for n in n_sems] + [pltpu.VMEM(s, d) for s, d in stage]


ANY_SPEC = pl.BlockSpec(memory_space=pl.ANY)


def _run_comm(comm, name):
    def kern(*refs):
        comm.body(refs, True, True)

    return pl.pallas_call(
        kern, name=name, in_specs=[ANY_SPEC] * len(comm.ins), out_specs=[ANY_SPEC] * len(comm.out_shapes),
        out_shape=comm.out_shapes, scratch_shapes=comm.scratch,
        compiler_params=pltpu.CompilerParams(has_side_effects=True, vmem_limit_bytes=VMEM_LIMIT),
    )(*comm.ins)


def _pcall(kern, name, grid, in_specs, out_specs, out_shape, scratch_shapes, sem, args, comm=None):
    if comm is None:
        return pl.pallas_call(kern, name=name, grid=grid, in_specs=in_specs, out_specs=out_specs, out_shape=out_shape,
                              scratch_shapes=scratch_shapes, compiler_params=_cparams(sem))(*args), None
    n_in, n_out, n_scr = len(in_specs), len(out_specs), len(scratch_shapes)
    c_in, c_out = len(comm.ins), len(comm.out_shapes)

    def hosted(*refs):
        a = n_in + c_in
        b = a + n_out + c_out
        main = refs[:n_in] + refs[a:a + n_out] + refs[b:b + n_scr]
        extra = refs[n_in:a] + refs[a + n_out:b] + refs[b + n_scr:]
        ids = [pl.program_id(d) for d in range(len(grid))]
        first, last = ids[0] == 0, ids[0] == grid[0] - 1
        for i, g in zip(ids[1:], grid[1:]):
            first, last = jnp.logical_and(first, i == 0), jnp.logical_and(last, i == g - 1)

        @pl.when(first)
        def _():
            comm.body(extra, True, False)

        kern(*main)

        @pl.when(last)
        def _():
            comm.body(extra, False, True)

    res = pl.pallas_call(
        hosted, name=name, grid=grid, in_specs=list(in_specs) + [ANY_SPEC] * c_in,
        out_specs=list(out_specs) + [ANY_SPEC] * c_out, out_shape=list(out_shape) + comm.out_shapes,
        scratch_shapes=list(scratch_shapes) + comm.scratch,
        compiler_params=pltpu.CompilerParams(dimension_semantics=("arbitrary",) * len(grid),
                                             vmem_limit_bytes=VMEM_LIMIT, has_side_effects=True),
    )(*args, *comm.ins)
    return res[:n_out], res[n_out:]


class _StagedCopies:
    def __init__(self, pairs, bufs, sem_in, sem_out):
        self.ins = [pltpu.make_async_copy(src, buf, sem_in.at[i]) for i, ((src, _), buf) in enumerate(zip(pairs, bufs))]
        self.outs = [pltpu.make_async_copy(buf, dst, sem_out.at[i]) for i, ((_, dst), buf) in enumerate(zip(pairs, bufs))]

    def load(self):
        for cp in self.ins:
            cp.start()

    def store(self):
        for cp in self.ins:
            cp.wait()
        for cp in self.outs:
            cp.start()

    def finish(self):
        for cp in self.outs:
            cp.wait()


DMA_CHUNK_BYTES = 1 << 20
DMA_MAX_CHUNKS = 4


def _row_chunks(rows, row_bytes, align=16):
    units = rows // align
    k = max(1, min(DMA_MAX_CHUNKS, -(-rows * row_bytes // DMA_CHUNK_BYTES), units))
    out, start = [], 0
    for i in range(k):
        size = (units // k + (1 if i < units % k else 0)) * align
        out.append((start, size))
        start += size
    if start < rows:
        out[-1] = (out[-1][0], out[-1][1] + rows - start)
    return out


def _row_bytes(shape, dtype):
    return math.prod(shape[1:]) * jnp.dtype(dtype).itemsize


def _remote(src, dst, send, recv, idx, dev):
    return pltpu.make_async_remote_copy(src, dst, send.at[idx], recv.at[idx], device_id=dev, device_id_type=MESH_ID)


def _allgather_chips(xs):
    na = len(xs)
    chunks = [_row_chunks(x.shape[1], _row_bytes(x.shape[1:], x.dtype)) for x in xs]
    n_cp = 3 * sum(len(ch) for ch in chunks)

    def body(refs, start, finish):
        x_refs, o_refs = refs[:na], refs[na:2 * na]
        send1, recv1, send2, recv2, loc_in, loc_out = refs[2 * na:2 * na + 6]
        px, py, c = _place()
        me = 2 * px + py
        sib = (px, py, 1 - c)
        own = _StagedCopies([(x, o.at[me]) for x, o in zip(x_refs, o_refs)], refs[2 * na + 6:], loc_in, loc_out)
        plan = [(a, qx, qy, pl.ds(s, n)) for a in range(na) for (qx, qy) in _other_chips(px, py) for (s, n) in chunks[a]]
        first = [_remote(x_refs[a].at[c, rows], o_refs[a].at[me, c, rows], send1, recv1, i, (qx, qy, c))
                 for i, (a, qx, qy, rows) in enumerate(plan)]
        if start:
            own.load()
            for cp in first:
                cp.start()
        if finish:
            own.store()
            passed = []
            for i, (a, qx, qy, rows) in enumerate(plan):
                slot = o_refs[a].at[2 * qx + qy, c, rows]
                _remote(slot, slot, send1, recv1, i, (qx, qy, c)).wait_recv()
                fwd = _remote(slot, slot, send2, recv2, i, sib)
                fwd.start()
                passed.append(fwd)
            for i, (a, qx, qy, rows) in enumerate(plan):
                slot = o_refs[a].at[2 * qx + qy, 1 - c, rows]
                _remote(slot, slot, send2, recv2, i, sib).wait_recv()
            for cp in first + passed:
                cp.wait_send()
            own.finish()

    outs = [jax.ShapeDtypeStruct((N_CHIPS,) + x.shape, x.dtype) for x in xs]
    return _Comm(xs, outs, [n_cp, n_cp, n_cp, n_cp, na, na], [(x.shape, x.dtype) for x in xs], body)


def _sibling_halves(gs):
    na = len(gs)
    chunks = [_row_chunks(g.shape[2], _row_bytes(g.shape[2:], g.dtype)) for g in gs]
    n_cp = N_CHIPS * sum(len(ch) for ch in chunks)

    def body(refs, start, finish):
        g_refs, o_refs = refs[:na], refs[na:2 * na]
        send, recv = refs[2 * na:]
        px, py, c = _place()
        plan = [(a, q, pl.ds(s, n)) for a in range(na) for q in range(N_CHIPS) for (s, n) in chunks[a]]
        cps = [_remote(g_refs[a].at[q, 1 - c, rows], o_refs[a].at[q, rows], send, recv, i, (px, py, 1 - c))
               for i, (a, q, rows) in enumerate(plan)]
        if start:
            for cp in cps:
                cp.start()
        if finish:
            for cp in cps:
                cp.wait()

    outs = [jax.ShapeDtypeStruct((N_CHIPS,) + g.shape[2:], g.dtype) for g in gs]
    return _Comm(gs, outs, [n_cp, n_cp], [], body)


def _chip_exchange(ps):
    na = len(ps)
    chunks = [_row_chunks(p.shape[1], _row_bytes(p.shape[1:], p.dtype)) for p in ps]
    n_cp = 3 * sum(len(ch) for ch in chunks)

    def body(refs, start, finish):
        p_refs, o_refs = refs[:na], refs[na:2 * na]
        send, recv, loc_in, loc_out = refs[2 * na:2 * na + 4]
        px, py, c = _place()
        me = 2 * px + py
        own = _StagedCopies([(p.at[me], o.at[me]) for p, o in zip(p_refs, o_refs)], refs[2 * na + 4:], loc_in, loc_out)
        plan = [(a, qx, qy, pl.ds(s, n)) for a in range(na) for (qx, qy) in _other_chips(px, py) for (s, n) in chunks[a]]
        cps = [_remote(p_refs[a].at[2 * qx + qy, rows], o_refs[a].at[me, rows], send, recv, i, (qx, qy, c))
               for i, (a, qx, qy, rows) in enumerate(plan)]
        if start:
            own.load()
            for cp in cps:
                cp.start()
        if finish:
            own.store()
            for cp in cps:
                cp.wait()
            own.finish()

    outs = [jax.ShapeDtypeStruct(p.shape, p.dtype) for p in ps]
    return _Comm(ps, outs, [n_cp, n_cp, na, na], [(p.shape[1:], p.dtype) for p in ps], body)


def _sibling_join(rs):
    na = len(rs)
    chunks = [_row_chunks(r.shape[0], _row_bytes(r.shape, r.dtype)) for r in rs]
    n_cp = sum(len(ch) for ch in chunks)

    def body(refs, start, finish):
        r_refs, o_refs = refs[:na], refs[na:2 * na]
        send, recv, loc_in, loc_out = refs[2 * na:2 * na + 4]
        px, py, c = _place()
        own = _StagedCopies([(r, o.at[c]) for r, o in zip(r_refs, o_refs)], refs[2 * na + 4:], loc_in, loc_out)
        plan = [(a, pl.ds(s, n)) for a in range(na) for (s, n) in chunks[a]]
        cps = [_remote(r_refs[a].at[rows], o_refs[a].at[c, rows], send, recv, i, (px, py, 1 - c))
               for i, (a, rows) in enumerate(plan)]
        if start:
            own.load()
            for cp in cps:
                cp.start()
        if finish:
            own.store()
            for cp in cps:
                cp.wait()
            own.finish()

    outs = [jax.ShapeDtypeStruct((2,) + r.shape, r.dtype) for r in rs]
    return _Comm(rs, outs, [n_cp, n_cp, na, na], [(r.shape, r.dtype) for r in rs], body)


def _reduce_stages(gs, c_idx, tag):
    got = yield _sibling_halves(gs)
    part = [_add_sibling(g, o, c_idx, f"reduce_add_sibling{tag}_{a}") for a, (g, o) in enumerate(zip(gs, got))]
    landed = yield _chip_exchange(part)
    mine = [_sum_chips(x, f"reduce_sum_chips{tag}_{a}") for a, x in enumerate(landed)]
    return (yield _sibling_join(mine))


class _Staged:
    def __init__(self, gen, tag):
        self.gen, self.tag, self.k, self.value = gen, tag, 0, None
        self.cur = next(gen)

    def stage(self):
        return self.cur

    def done(self, results):
        self.k += 1
        try:
            self.cur = self.gen.send(results)
        except StopIteration as stop:
            self.cur, self.value = None, stop.value

    def drain(self):
        while self.cur is not None:
            self.done(_run_comm(self.cur, f"comm{self.tag}_{self.k}"))
        return self.value


WEIGHTS = ["meta_tokens", "norm_mix_pre", "norm_mix_post", "norm_mlp_pre", "norm_mlp_post", "w_in", "conv_w",
           "conv_b", "lru_wa_f", "lru_ba_f", "lru_wx_f", "lru_bx_f", "lru_lambda_f", "lru_wa_b", "lru_ba_b",
           "lru_wx_b", "lru_bx_b", "lru_lambda_b", "gla_wg_f", "gla_bg_f", "gla_wg_b", "gla_bg_b", "gla_head_norm",
           "w_out", "w_mlp_up", "w_mlp_down"]
BIG = ("w_in", "w_out", "w_mlp_up", "w_mlp_down")
SMALL = [n for n in WEIGHTS if n not in BIG]
SMALL_SHARDED = {"meta_tokens": 1, "conv_w": 2, "gla_wg_f": 2, "gla_wg_b": 2}
N_CHIPS = 4
SMALL_RAW = [("g1", (1, D_MODEL)), ("g2", (1, D_MODEL)), ("g3", (1, D_MODEL)), ("g4", (1, D_MODEL)),
             ("conv_w", (4, LRU_W)), ("conv_b", (1, LRU_W)), ("lru_wg", (LRU_HEADS, LRU_HD, 4 * LRU_HD)),
             ("lru_pb", (SUBLANES, LRU_W)), ("gla_wg", (2 * GLA_RANK, 2 * QK)), ("gla_bg", (1, 2 * QK)),
             ("head_norm", (1, GLA_W))]
META_SHAPE = (N_META, D_MODEL)


def _pad_to(v, n):
    return jnp.pad(v, (0, n - v.shape[0]))


def _round_up(n, m):
    return -(-n // m) * m


def _flat_cat(arrs, total):
    return _pad_to(jnp.concatenate([a.reshape(-1) for a in arrs]), total)


def _split_flat(flat, shapes):
    out, off = [], 0
    for s in shapes:
        n = math.prod(s)
        out.append(flat[off:off + n].reshape(s))
        off += n
    return out


def _my_shard(full, axis, chip):
    n = full.shape[axis] // N_CHIPS
    return lax.dynamic_slice_in_dim(full, chip * n, n, axis=axis)


def _block_diag_gates(wa_f, wx_f, wa_b, wx_b):
    eye2 = jnp.eye(2, dtype=wa_f.dtype)

    def bd(w):
        w4 = w.reshape(4, 2, LRU_HD, LRU_HD)
        return (w4[:, :, :, None, :] * eye2[None, :, None, :, None]).reshape(4, LANES, LANES)
    return jnp.concatenate([bd(wa_f), bd(wx_f), bd(wa_b), bd(wx_b)], axis=2).astype(BF16)


def _gate_diag_blocks(dwh, o):
    return dwh[:, :, o * LRU_HD:(o + 1) * LRU_HD]


def kernel(x, meta_tokens, norm_mix_pre, norm_mix_post, norm_mlp_pre, norm_mlp_post, w_in, conv_w, conv_b, lru_wa_f, lru_ba_f, lru_wx_f, lru_bx_f, lru_lambda_f, lru_wa_b, lru_ba_b, lru_wx_b, lru_bx_b, lru_lambda_b, gla_wg_f, gla_bg_f, gla_wg_b, gla_bg_b, gla_head_norm, w_out, w_mlp_up, w_mlp_down, loss_target, m_meta_tokens, m_norm_mix_pre, m_norm_mix_post, m_norm_mlp_pre, m_norm_mlp_post, m_w_in, m_conv_w, m_conv_b, m_lru_wa_f, m_lru_ba_f, m_lru_wx_f, m_lru_bx_f, m_lru_lambda_f, m_lru_wa_b, m_lru_ba_b, m_lru_wx_b, m_lru_bx_b, m_lru_lambda_b, m_gla_wg_f, m_gla_bg_f, m_gla_wg_b, m_gla_bg_b, m_gla_head_norm, m_w_out, m_w_mlp_up, m_w_mlp_down, v_meta_tokens, v_norm_mix_pre, v_norm_mix_post, v_norm_mlp_pre, v_norm_mlp_post, v_w_in, v_conv_w, v_conv_b, v_lru_wa_f, v_lru_ba_f, v_lru_wx_f, v_lru_bx_f, v_lru_lambda_f, v_lru_wa_b, v_lru_ba_b, v_lru_wx_b, v_lru_bx_b, v_lru_lambda_b, v_gla_wg_f, v_gla_bg_f, v_gla_wg_b, v_gla_bg_b, v_gla_head_norm, v_w_out, v_w_mlp_up, v_w_mlp_down):
    args = dict(locals())
    w_loc = {n: args[n] for n in WEIGHTS}
    m_loc = {n: args["m_" + n] for n in WEIGHTS}
    v_loc = {n: args["v_" + n] for n in WEIGHTS}
    seq = x.shape[1]
    t_rows = FRONT + seq + BACK
    assert t_rows % 768 == 0 and seq % FRONT == 0, seq
    chip = 2 * lax.axis_index("x") + lax.axis_index("y")
    c_idx = lax.axis_index("c").astype(jnp.int32).reshape(1)

    def halves(a):
        return a.reshape((2, a.shape[0] // 2) + a.shape[1:])

    big16 = {n: w_loc[n].astype(BF16) for n in BIG}
    sm_names = list(SMALL_SHARDED)
    sm_len = _round_up(sum(w_loc[n].size for n in sm_names), 2 * SUBLANES * FLAT_W)
    sm_pack = _flat_cat([w_loc[n] for n in sm_names], sm_len).reshape(2, -1, FLAT_W)
    g_in0, g_small = _run_comm(_allgather_chips([halves(big16["w_in"][0]), sm_pack]), "gather_first")
    sm_parts = [_split_flat(g_small[q].reshape(-1), [w_loc[n].shape for n in sm_names]) for q in range(N_CHIPS)]
    full = {n: jnp.concatenate([sm_parts[q][i] for q in range(N_CHIPS)], axis=SMALL_SHARDED[n])
            for i, n in enumerate(sm_names)}
    for n in SMALL:
        if n not in SMALL_SHARDED:
            full[n] = w_loc[n]

    def full_w_in(g):
        w = jnp.transpose(g.reshape(N_CHIPS, D_MODEL, -1), (1, 0, 2)).reshape(D_MODEL, D_IN)
        return jnp.pad(w, ((0, 0), (0, D_IN_PAD - D_IN)))

    def full_rest(g_out, g_up, g_down):
        return dict(w_out=g_out.reshape(-1, D_MODEL),
                    w_up=jnp.transpose(g_up.reshape(N_CHIPS, D_MODEL, -1), (1, 0, 2)).reshape(D_MODEL, D_FF),
                    w_down=g_down.reshape(D_FF, D_MODEL))

    later_gathers = [_allgather_chips([halves(big16[n][0]) for n in BIG[1:]]),
                     _allgather_chips([halves(big16[n][l]) for l in range(1, DEPTH) for n in BIG])]

    meta_blk = jnp.concatenate([jnp.zeros((FRONT - N_META, D_MODEL), F32), full["meta_tokens"]], axis=0)
    h = jnp.concatenate([meta_blk, x[0], jnp.zeros((BACK, D_MODEL), F32)], axis=0)
    target = loss_target[0]

    layer_w = []
    for l in range(DEPTH):
        wg = jnp.zeros((LANES, 2 * QK), F32)
        wg = wg.at[0:GLA_RANK, 0:QK].set(full["gla_wg_f"][l]).at[GLA_RANK:2 * GLA_RANK, QK:].set(full["gla_wg_b"][l])
        pb = jnp.stack([full["lru_ba_f"][l], full["lru_bx_f"][l], full["lru_ba_b"][l], full["lru_bx_b"][l],
                        full["lru_lambda_f"][l], full["lru_lambda_b"][l], jnp.zeros((LRU_W,), F32),
                        jnp.zeros((LRU_W,), F32)])
        layer_w.append(dict(
            g1=full["norm_mix_pre"][l][None], g2=full["norm_mix_post"][l][None],
            g3=full["norm_mlp_pre"][l][None], g4=full["norm_mlp_post"][l][None],
            conv_w=full["conv_w"][l], conv_b=full["conv_b"][l][None],
            lru_wg=_block_diag_gates(full["lru_wa_f"][l], full["lru_wx_f"][l], full["lru_wa_b"][l], full["lru_wx_b"][l]),
            lru_pb=pb, gla_wg=wg.astype(BF16),
            gla_bg=jnp.concatenate([full["gla_bg_f"][l], full["gla_bg_b"][l]])[None],
            head_norm=full["gla_head_norm"][l][None]))
    layer_w[0]["w_in"] = full_w_in(g_in0)

    def split_z(acc):
        return (acc[:, 0:1024], acc[:, 1024:1536], acc[:, 1536:2048], acc[:, 2048:2560], acc[:, ZG_OFF:ZG_OFF + LANES])

    def relu2(acc):
        r = jnp.maximum(acc, 0.0)
        return acc, r * r

    tm_e = 768
    tm_ff = 1408 if t_rows % 1408 == 0 else 768

    def post_mix(acc, hh, g2, g3):
        h1 = hh + _rms_fwd(acc, g2)
        return acc, h1, _rms_fwd(h1, g3)

    def post_mlp(acc, h1, g4, g1_next):
        h2 = h1 + _rms_fwd(acc, g4)
        return acc, h2, _rms_fwd(h2, g1_next)

    def post_mlp_loss(acc, h1, tgt, x_rows, g4):
        h2 = h1 + _rms_fwd(acc, g4)
        err = (h2 - tgt) * x_rows[:, 0:1]
        d_h = err * (1.0 / D_MODEL)
        dff, dg4 = _rms_bwd(acc, g4, d_h)
        loss_part = jnp.zeros((SUBLANES, LANES), F32) + jnp.sum(err * err) * (0.5 / D_MODEL)
        return d_h, dff, loss_part, dg4

    row3 = [(D_MODEL, F32), (D_MODEL, F32), (D_MODEL, BF16)]
    vec = ((1, D_MODEL), F32)
    target_p = jnp.concatenate([jnp.zeros((FRONT, D_MODEL), F32), target, jnp.zeros((BACK, D_MODEL), F32)], axis=0)
    x_rows = jnp.concatenate([jnp.zeros((FRONT, LANES), F32), jnp.ones((seq, LANES), F32), jnp.zeros((BACK, LANES), F32)])
    n_sub = 1
    saved = []
    n1 = _norm_cast(h, layer_w[0]["g1"], t_rows, "norm_mix_pre_l0")
    for l in range(DEPTH):
        lw = layer_w[l]
        tag = f"_l{l}"
        z_lru, z_qk, z_v, z_go, z_zg = _mm(
            n1, lw["w_in"], "nn", 768, D_IN_PAD, D_MODEL,
            [(1024, F32), (512, F32), (512, F32), (512, F32), (LANES, F32)], "in_proj" + tag, epilogue=split_z)
        lru_args = (z_lru, lw["conv_w"], lw["conv_b"], lw["lru_wg"], lw["lru_pb"], t_rows, seq, "lru_fwd" + tag)
        gla_args = (z_qk, z_v, z_zg, lw["gla_wg"], lw["gla_bg"], t_rows, "gla_fwd" + tag)
        if l == 0:
            (h_f, h_b), g_rest = _lru_fwd(*lru_args, comm=later_gathers[0])
            lw.update(full_rest(*g_rest))
            (o_f, o_b, s_f, s_b), g_next = _gla_fwd(*gla_args, comm=later_gathers[1])
            for l2 in range(1, DEPTH):
                g_l2 = g_next[(l2 - 1) * len(BIG):l2 * len(BIG)]
                layer_w[l2].update(full_rest(*g_l2[1:]), w_in=full_w_in(g_l2[0]))
        else:
            h_f, h_b = _lru_fwd(*lru_args)
            o_f, o_b, s_f, s_b = _gla_fwd(*gla_args)
        ymix = _mix_fwd(h_f, h_b, z_lru, o_f, o_b, z_go, lw["head_norm"], t_rows, "mix_fwd" + tag)
        mix, h1, n3 = _mm(ymix, lw["w_out"], "nn", 768, D_MODEL, D_MODEL, row3, "out_proj" + tag, epilogue=post_mix,
                          rows_in=(h,), consts=(lw["g2"], lw["g3"]), row_split=n_sub)
        u, act = _mm(n3, lw["w_up"], "nn", tm_ff, 1024, D_MODEL, [(D_FF, BF16), (D_FF, BF16)], "mlp_up" + tag,
                     epilogue=relu2)
        sv = dict(h=h, n1=n1, z_lru=z_lru, z_qk=z_qk, z_v=z_v, z_go=z_go, z_zg=z_zg, h_f=h_f, h_b=h_b,
                  o_f=o_f, o_b=o_b, s_f=s_f, s_b=s_b, ymix=ymix, mix=mix, h1=h1, n3=n3, u=u, act=act)
        if l + 1 < DEPTH:
            sv["ff"], h, n1 = _mm(act, lw["w_down"], "nn", tm_e, D_MODEL, D_FF, row3, "mlp_down" + tag,
                                  epilogue=post_mlp, rows_in=(h1,), consts=(lw["g4"], layer_w[l + 1]["g1"]),
                                  row_split=n_sub)
        else:
            dh, dff, loss_part, dg4 = _mm(act, lw["w_down"], "nn", tm_e, D_MODEL, D_FF,
                                          [(D_MODEL, F32), (D_MODEL, BF16)], "mlp_down_loss" + tag,
                                          epilogue=post_mlp_loss, rows_in=(h1, target_p, x_rows), consts=(lw["g4"],),
                                          accs=[((SUBLANES, LANES), F32), vec], row_split=n_sub)
        saved.append(sv)

    loss = lax.psum(loss_part[0, 0], ("x", "y", "c"))

    small_len = _round_up(sum(math.prod(s) for _, s in SMALL_RAW) + math.prod(META_SHAPE),
                          N_CHIPS * 2 * 2 * SUBLANES * FLAT_W)
    totals = [None] * DEPTH
    pend = None

    def with_stage(staged, fn):
        comm = staged.stage() if staged is not None else None
        if comm is None:
            return fn(None)
        res, comm_res = fn(comm)
        staged.done(comm_res)
        return res

    for l in reversed(range(DEPTH)):
        lw, sv = layer_w[l], saved[l]
        tag = f"_l{l}"
        raw = {"g4": dg4}
        gw_down = with_stage(pend, lambda comm: _mm(sv["act"], dff, "tn", 1024, D_MODEL, tm_ff, [(D_MODEL, F32)],
                                                    "dw_down" + tag, comm=comm))[0]
        du = _mm(dff, lw["w_down"], "nt", tm_ff, 1024, D_MODEL, [(D_FF, BF16)], "d_act" + tag,
                 epilogue=lambda acc, uu: (acc * 2.0 * jnp.maximum(uu.astype(F32), 0.0),), extras=(sv["u"],))[0]
        gw_up = _mm(sv["n3"], du, "tn", D_MODEL, D_FF // N_CHIPS, tm_ff, [(D_FF, F32)], "dw_up" + tag,
                    col_blocks_first=True)[0]

        def pre_mlp_bwd(acc, h1, d_h, mix, g3, g2):
            dx, dg3 = _rms_bwd(h1, g3, acc)
            d_h1 = d_h + dx
            d_mix, dg2 = _rms_bwd(mix, g2, d_h1)
            return d_h1, d_mix, dg3, dg2

        dh1, dmix, raw["g3"], raw["g2"] = _mm(
            du, lw["w_up"], "nt", tm_e, D_MODEL, D_FF, [(D_MODEL, F32), (D_MODEL, BF16)], "d_n3" + tag,
            epilogue=pre_mlp_bwd, rows_in=(sv["h1"], dh, sv["mix"]), consts=(lw["g3"], lw["g2"]), accs=[vec, vec],
            row_split=n_sub)
        gw_out = _mm(sv["ymix"], dmix, "tn", D_MODEL, D_MODEL, tm_ff, [(D_MODEL, F32)], "dw_out" + tag)[0]
        mats = [gw_out.reshape(N_CHIPS, 2, -1, D_MODEL), gw_up.reshape(N_CHIPS, 2, D_MODEL // 2, D_FF // N_CHIPS),
                gw_down.reshape(N_CHIPS, 2, -1, D_MODEL)]
        early = _Staged(_reduce_stages(mats, c_idx, tag + "e"), tag + "e") if l == 0 else None
        dymix = with_stage(early, lambda comm: _mm(dmix, lw["w_out"], "nt", 768, D_MODEL, D_MODEL, [(D_MODEL, F32)],
                                                   "d_ymix" + tag, comm=comm))[0]
        dh_lru, dgate, do, dgo, raw["head_norm"] = _mix_bwd(
            dymix, sv["h_f"], sv["h_b"], sv["z_lru"], sv["o_f"], sv["o_b"], sv["z_go"], lw["head_norm"], t_rows,
            "mix_bwd" + tag)
        gla_grads = with_stage(pend, lambda comm: _gla_bwd(sv["z_qk"], sv["z_v"], sv["z_zg"], lw["gla_wg"],
                                                           lw["gla_bg"], do, sv["s_f"], sv["s_b"], t_rows,
                                                           "gla_bwd" + tag, comm=comm))
        dx_br, raw["conv_w"], raw["conv_b"], raw["lru_wg"], raw["lru_pb"] = with_stage(early, lambda comm: _lru_bwd(
            sv["z_lru"], lw["conv_w"], lw["conv_b"], lw["lru_wg"], lw["lru_pb"], sv["h_f"], sv["h_b"], dh_lru,
            t_rows, seq, "lru_bwd" + tag, comm=comm))
        dz, dwg_gla, raw["gla_bg"] = _dz_assemble(dx_br, dgate, gla_grads, dgo, sv["z_zg"], lw["gla_wg"], t_rows,
                                                  "dz_assemble" + tag)
        raw["gla_wg"] = dwg_gla[0:2 * GLA_RANK]
        gw_in = with_stage(pend, lambda comm: _mm(sv["n1"], dz, "tn", D_MODEL, 896, tm_ff, [(D_IN_PAD, F32)],
                                                  "dw_in" + tag, comm=comm))[0]
        if l > 0:
            def pre_mix_bwd(acc, hh, d_h1, ff_prev, g1, g4_prev):
                dx, dg1 = _rms_bwd(hh, g1, acc)
                d_h = d_h1 + dx
                d_ff, dg4_prev = _rms_bwd(ff_prev, g4_prev, d_h)
                return d_h, d_ff, dg1, dg4_prev

            dh, dff, raw["g1"], dg4 = _mm(
                dz, lw["w_in"], "nt", tm_e, D_MODEL, D_IN_PAD, [(D_MODEL, F32), (D_MODEL, BF16)], "d_n1" + tag,
                epilogue=pre_mix_bwd, rows_in=(sv["h"], dh1, saved[l - 1]["ff"]),
                consts=(lw["g1"], layer_w[l - 1]["g4"]), accs=[vec, vec], row_split=n_sub)
        else:
            def pre_mix_bwd0(acc, hh, d_h1, g1):
                dx, dg1 = _rms_bwd(hh, g1, acc)
                return d_h1 + dx, dg1

            dh, raw["g1"] = with_stage(early, lambda comm: _mm(
                dz, lw["w_in"], "nt", tm_e, D_MODEL, D_IN_PAD, [(D_MODEL, F32)], "d_n1" + tag, epilogue=pre_mix_bwd0,
                rows_in=(sv["h"], dh1), consts=(lw["g1"],), accs=[vec], row_split=n_sub, comm=comm))

        cols = D_IN // N_CHIPS
        in_sh = jnp.stack([gw_in[:, q * cols:(q + 1) * cols] for q in range(N_CHIPS)])
        meta_g = dh[FRONT - N_META:FRONT] if l == 0 else jnp.zeros(META_SHAPE, F32)
        small = _flat_cat([raw[n] for n, _ in SMALL_RAW] + [meta_g], small_len)
        rest = [in_sh.reshape(N_CHIPS, 2, D_MODEL // 2, cols), small.reshape(N_CHIPS, 2, -1, FLAT_W)]
        if pend is not None:
            totals[l + 1] = pend.drain()
        if early is None:
            pend = _Staged(_reduce_stages([rest[0]] + mats + [rest[1]], c_idx, tag), tag)
        else:
            t_in, t_small = _Staged(_reduce_stages(rest, c_idx, tag), tag).drain()
            totals[l] = [t_in] + list(early.drain()) + [t_small]

    grad_x = dh[FRONT:FRONT + seq][None]

    sm_all = _run_comm(_allgather_chips([totals[l][4] for l in range(DEPTH)]), "gather_small_grads")
    g_tot = {}
    per_layer = []
    for l in range(DEPTH):
        pieces = _split_flat(sm_all[l].reshape(-1), [s for _, s in SMALL_RAW] + [META_SHAPE])
        per_layer.append(dict(zip([n for n, _ in SMALL_RAW] + ["meta"], pieces)))
    stack = lambda f: jnp.stack([f(per_layer[l]) for l in range(DEPTH)])
    g_tot["meta_tokens"] = _my_shard(per_layer[0]["meta"], 1, chip)
    for n, key in (("norm_mix_pre", "g1"), ("norm_mix_post", "g2"), ("norm_mlp_pre", "g3"), ("norm_mlp_post", "g4"),
                   ("conv_b", "conv_b"), ("gla_head_norm", "head_norm")):
        g_tot[n] = stack(lambda d, key=key: d[key][0])
    g_tot["conv_w"] = _my_shard(stack(lambda d: d["conv_w"]), 2, chip)
    for o, n in enumerate(("lru_wa_f", "lru_wx_f", "lru_wa_b", "lru_wx_b")):
        g_tot[n] = stack(lambda d, o=o: _gate_diag_blocks(d["lru_wg"], o))
    for row, n in enumerate(("lru_ba_f", "lru_bx_f", "lru_ba_b", "lru_bx_b", "lru_lambda_f", "lru_lambda_b")):
        g_tot[n] = stack(lambda d, row=row: d["lru_pb"][row])
    g_tot["gla_wg_f"] = _my_shard(stack(lambda d: d["gla_wg"][0:GLA_RANK, 0:QK]), 2, chip)
    g_tot["gla_wg_b"] = _my_shard(stack(lambda d: d["gla_wg"][GLA_RANK:, QK:]), 2, chip)
    g_tot["gla_bg_f"] = stack(lambda d: d["gla_bg"][0, 0:QK])
    g_tot["gla_bg_b"] = stack(lambda d: d["gla_bg"][0, QK:])

    delta, new_m, new_v = {}, {}, {}
    for a, n in enumerate(BIG):
        shp = w_loc[n].shape
        flat3 = lambda t, shp=shp: t.reshape(DEPTH, -1, shp[-1])
        g_l = [totals[l][a].reshape(-1, shp[-1]) for l in range(DEPTH)]
        outs = _adamw_layers(flat3(w_loc[n]), flat3(m_loc[n]), flat3(v_loc[n]), g_l, "adamw_" + n)
        g_tot[n], delta[n], new_m[n], new_v[n] = [o.reshape(shp) for o in outs]
    outs = _adamw_small(*[[d[n] for n in SMALL] for d in (w_loc, g_tot, m_loc, v_loc)])
    for d, o in zip((delta, new_m, new_v), outs):
        d.update(zip(SMALL, o))
    return (loss, grad_x, *[g_tot[n] for n in WEIGHTS], *[delta[n] for n in WEIGHTS],
            *[new_m[n] for n in WEIGHTS], *[new_v[n] for n in WEIGHTS])
```

```python
import math

import jax
import jax.numpy as jnp
from jax import lax
from jax.experimental import pallas as pl
from jax.experimental.pallas import tpu as pltpu

F32 = jnp.float32
BF16 = jnp.bfloat16

D_MODEL = 1024
DEPTH = 2
N_META = 16
LRU_W = 512
LRU_HEADS = 8
LRU_HD = 64
LRU_C = 8.0
GLA_W = 512
GLA_H = 4
GLA_DK = 64
GLA_DV = 128
GLA_RANK = 16
GLA_GATE_NORM = 16.0
D_FF = 4096
D_IN = 2592
EPS = 1e-6
ADAM_LR, ADAM_B1, ADAM_B2, ADAM_EPS, ADAM_WD, ADAM_STEP = 0.001, 0.9, 0.999, 1e-08, 0.01, 10

LANES = 128
SUBLANES = 8
FRONT = 128
BACK = 128
D_IN_PAD = 2688
ZG_OFF = 2560
CHUNK = 64
EXP_CLAMP = 80.0
VMEM_LIMIT = 56 * 1024 * 1024
MESH_ID = pl.DeviceIdType.MESH

NN = (((1,), (0,)), ((), ()))
NT = (((1,), (1,)), ((), ()))
TN = (((0,), (0,)), ((), ()))


def _dot(a, b, dims=NN):
    return lax.dot_general(a, b, dims, preferred_element_type=F32)


def _cparams(sem):
    return pltpu.CompilerParams(dimension_semantics=sem, vmem_limit_bytes=VMEM_LIMIT)


def _sigmoid(x):
    d = 1.0 + jnp.exp(jnp.minimum(-x, EXP_CLAMP))
    r = pl.reciprocal(d, approx=True)
    return r * (2.0 - d * r)


def _gelu(x):
    c = math.sqrt(2.0 / math.pi)
    return 0.5 * x * (1.0 + jnp.tanh(c * (x + 0.044715 * x * x * x)))


def _gelu_grad(x):
    c = math.sqrt(2.0 / math.pi)
    t = jnp.tanh(c * (x + 0.044715 * x * x * x))
    return 0.5 * (1.0 + t) + 0.5 * x * (1.0 - t * t) * c * (1.0 + 3.0 * 0.044715 * x * x)


def _rms_fwd(x, g):
    rstd = lax.rsqrt(jnp.mean(x * x, axis=1, keepdims=True) + EPS)
    return (x * rstd) * g


def _rms_bwd(x, g, dy):
    rstd = lax.rsqrt(jnp.mean(x * x, axis=1, keepdims=True) + EPS)
    xh = x * rstd
    dxh = dy * g
    dx = rstd * (dxh - xh * jnp.mean(dxh * xh, axis=1, keepdims=True))
    dg = jnp.sum(dy * xh, axis=0, keepdims=True)
    return dx, dg


def _mm(a, b, mode, tm, tn, tk, outs, name, epilogue=None, extras=(), col_blocks_first=False,
        rows_in=(), consts=(), accs=(), comm=None, row_split=1, lag=False):
    if mode == "nn":
        (m, k), n = a.shape, b.shape[1]
        a_spec = pl.BlockSpec((tm, tk), lambda i, j, kk: (i, kk))
        b_spec = pl.BlockSpec((tk, tn), lambda i, j, kk: (kk, j))
        dims = NN
    elif mode == "nt":
        (m, k), n = a.shape, b.shape[0]
        a_spec = pl.BlockSpec((tm, tk), lambda i, j, kk: (i, kk))
        b_spec = pl.BlockSpec((tn, tk), lambda i, j, kk: (j, kk))
        dims = NT
    else:
        (k, m), n = a.shape, b.shape[1]
        a_spec = pl.BlockSpec((tk, tm), lambda i, j, kk: (kk, i))
        b_spec = pl.BlockSpec((tk, tn), lambda i, j, kk: (kk, j))
        dims = TN
    assert m % tm == 0 and n % tn == 0 and k % tk == 0, (name, m, n, k)
    nk = k // tk
    n_mt = m // tm
    if n == tn and nk == 1:
        b_spec = pl.BlockSpec(b_spec.block_shape, b_spec.index_map, pipeline_mode=pl.Buffered(1))
    if lag:
        assert nk == 1 and n == tn and mode != "tn", name
        a_spec = pl.BlockSpec((tm, tk), lambda i, j, kk: (jnp.minimum(i, n_mt - 1), kk))
        row = lambda i: jnp.maximum(i - 1, 0)
    else:
        row = lambda i: i
    ne = len(extras) + len(rows_in) + len(consts)
    e_specs = [pl.BlockSpec((tm, tn), lambda i, j, kk: (row(i), j)) for _ in extras]
    e_specs += [pl.BlockSpec((tm, r.shape[1]), lambda i, j, kk: (row(i), 0)) for r in rows_in]
    e_specs += [pl.BlockSpec(c.shape, lambda i, j, kk: (0, 0)) for c in consts]
    n_out = len(outs)
    o_specs, o_shapes = [], []
    for width, dtype in outs:
        if col_blocks_first:
            assert width == n and not lag, name
            o_specs.append(pl.BlockSpec((None, tm, tn), lambda i, j, kk: (j, i, 0)))
            o_shapes.append(jax.ShapeDtypeStruct((n // tn, m, tn), dtype))
            continue
        if width == n:
            o_specs.append(pl.BlockSpec((tm, tn), lambda i, j, kk: (row(i), j)))
        else:
            assert n == tn, name
            o_specs.append(pl.BlockSpec((tm, width), lambda i, j, kk: (row(i), 0)))
        o_shapes.append(jax.ShapeDtypeStruct((m, width), dtype))
    o_specs += [pl.BlockSpec(s, lambda i, j, kk: (0, 0)) for s, _ in accs]
    o_shapes += [jax.ShapeDtypeStruct(s, d) for s, d in accs]

    n_tiled = len(extras) + len(rows_in)

    def finish(acc, extra_refs, out_refs, rs=slice(None), zero_accs=True, weight=None):
        tiles = [e[rs, :] for e in extra_refs[:n_tiled]] + [e[...] for e in extra_refs[n_tiled:]]
        res = epilogue(acc, *tiles) if epilogue else (acc,)
        for o, r in zip(out_refs[:n_out], res[:n_out]):
            o[rs, :] = r.astype(o.dtype)
        first = jnp.logical_and(pl.program_id(0) == 0, pl.program_id(1) == 0)
        for o, r in zip(out_refs[n_out:], res[n_out:]):
            if zero_accs:
                @pl.when(first)
                def _(o=o):
                    o[...] = jnp.zeros_like(o)

            o[...] += r if weight is None else r * weight

    if lag:
        def body(a_ref, b_ref, *rest):
            acc_ref = rest[-1]
            i = pl.program_id(0)

            @pl.when(i == 0)
            def _():
                acc_ref[...] = jnp.zeros_like(acc_ref)

            slot = lax.rem(i, 2)
            prev = acc_ref[1 - slot]
            acc_ref[slot] = _dot(a_ref[...], b_ref[...], dims)
            finish(prev, rest[:ne], rest[ne:-1], weight=(i >= 1).astype(F32))
        scratch = [pltpu.VMEM((2, tm, tn), F32)]
    elif nk == 1 and row_split > 1:
        assert mode != "tn" and tm % row_split == 0, name
        sub = tm // row_split

        def body(a_ref, b_ref, *rest):
            for s in range(row_split):
                rs = slice(s * sub, (s + 1) * sub)
                finish(_dot(a_ref[rs, :], b_ref[...], dims), rest[:ne], rest[ne:], rs, zero_accs=(s == 0))
        scratch = []
    elif nk == 1:
        def body(a_ref, b_ref, *rest):
            finish(_dot(a_ref[...], b_ref[...], dims), rest[:ne], rest[ne:])
        scratch = []
    else:
        def body(a_ref, b_ref, *rest):
            acc_ref = rest[-1]
            kk = pl.program_id(2)

            @pl.when(kk == 0)
            def _():
                acc_ref[...] = jnp.zeros_like(acc_ref)

            acc_ref[...] += _dot(a_ref[...], b_ref[...], dims)

            @pl.when(kk == nk - 1)
            def _():
                finish(acc_ref[...], rest[:ne], rest[ne:-1])
        scratch = [pltpu.VMEM((tm, tn), F32)]

    sem = ("arbitrary", "arbitrary", "arbitrary") if accs or lag else ("parallel", "parallel", "arbitrary")
    res, comm_res = _pcall(body, name, (n_mt + (1 if lag else 0), n // tn, nk), [a_spec, b_spec] + e_specs, o_specs,
                           o_shapes, scratch, sem, (a, b, *extras, *rows_in, *consts), comm)
    return res if comm is None else (res, comm_res)


def _rows(body, n_rows, tm, ins, consts, outs, accs, name):
    assert n_rows % tm == 0, (name, n_rows, tm)
    in_specs = [pl.BlockSpec((tm, w), (lambda i, cb=cb: (i, cb))) for _, w, cb in ins]
    in_specs += [pl.BlockSpec(c.shape, lambda i: (0, 0)) for c in consts]
    out_specs = [pl.BlockSpec((tm, w), lambda i: (i, 0)) for w, _ in outs]
    out_specs += [pl.BlockSpec(s, lambda i: (0, 0)) for s, _ in accs]
    out_shape = [jax.ShapeDtypeStruct((n_rows, w), d) for w, d in outs]
    out_shape += [jax.ShapeDtypeStruct(s, d) for s, d in accs]
    ni, nc, no = len(ins), len(consts), len(outs)

    def kern(*refs):
        body(pl.program_id(0), refs[:ni], refs[ni:ni + nc], refs[ni + nc:ni + nc + no], refs[ni + nc + no:])

    res = pl.pallas_call(
        kern, name=name, grid=(n_rows // tm,), in_specs=in_specs, out_specs=out_specs, out_shape=out_shape,
        compiler_params=_cparams(("arbitrary",)),
    )(*[a for a, _, _ in ins], *consts)
    return res


def _acc_add(i, ref, val):
    @pl.when(i == 0)
    def _():
        ref[...] = jnp.zeros_like(ref)

    ref[...] += val


def _norm_cast(h, g, t_rows, name):
    def body(i, ins, consts, outs, accs):
        outs[0][...] = _rms_fwd(ins[0][...], consts[0][...]).astype(BF16)
    return _rows(body, t_rows, 384, [(h, D_MODEL, 0)], [g], [(D_MODEL, BF16)], [], name)[0]


def _head_norm_parts(o):
    ns, rs = [], []
    for hd in range(GLA_H):
        oh = o[:, hd * GLA_DV:(hd + 1) * GLA_DV]
        r = lax.rsqrt(jnp.mean(oh * oh, axis=1, keepdims=True) + EPS)
        ns.append(oh * r)
        rs.append(r)
    return ns, rs


def _mix_fwd(h_f, h_b, z_lru, o_f, o_b, z_go, head_norm, t_rows, name):
    def body(i, ins, consts, outs, accs):
        hf, hb, gate, of, ob, go = [r[...] for r in ins]
        hn = consts[0][...]
        outs[0][:, 0:LRU_W] = ((hf + hb) * _gelu(gate)).astype(BF16)
        ns, _ = _head_norm_parts(of + ob)
        silu = go * _sigmoid(go)
        for hd in range(GLA_H):
            sl = slice(hd * GLA_DV, (hd + 1) * GLA_DV)
            outs[0][:, LRU_W + hd * GLA_DV:LRU_W + (hd + 1) * GLA_DV] = (ns[hd] * hn[:, sl] * silu[:, sl]).astype(BF16)
    ins = [(h_f, LRU_W, 0), (h_b, LRU_W, 0), (z_lru, LRU_W, 1), (o_f, GLA_W, 0), (o_b, GLA_W, 0), (z_go, GLA_W, 0)]
    return _rows(body, t_rows, 384, ins, [head_norm], [(D_MODEL, BF16)], [], name)[0]


def _mix_bwd_tile(dymix, hf, hb, z_lru, of, ob, go, hn):
    dyl, dyg, gate = dymix[:, 0:LRU_W], dymix[:, LRU_W:], z_lru[:, LRU_W:]
    dh_lru = dyl * _gelu(gate)
    dgate = dyl * (hf + hb) * _gelu_grad(gate)
    ns, rs = _head_norm_parts(of + ob)
    sg = _sigmoid(go)
    silu = go * sg
    dsilu = sg * (1.0 + go * (1.0 - sg))
    d_o, d_go, dhn = [], [], []
    for hd in range(GLA_H):
        sl = slice(hd * GLA_DV, (hd + 1) * GLA_DV)
        dy = dyg[:, sl]
        e = dy * hn[:, sl] * silu[:, sl]
        d_o.append(rs[hd] * (e - ns[hd] * jnp.mean(e * ns[hd], axis=1, keepdims=True)))
        d_go.append(dy * ns[hd] * hn[:, sl] * dsilu[:, sl])
        dhn.append(jnp.sum(dy * ns[hd] * silu[:, sl], axis=0, keepdims=True))
    return dh_lru, dgate, jnp.concatenate(d_o, axis=1), jnp.concatenate(d_go, axis=1), jnp.concatenate(dhn, axis=1)


def _dz_assemble(dx_br, dgate, gla_grads, dgo, z_zg, wg, t_rows, name):
    dq_f, dk_f, dv_f, dpre_f, dq_b, dk_b, dv_b, dpre_b = gla_grads
    qk = GLA_H * GLA_DK

    def body(i, ins, consts, outs, accs):
        (dxb, dgt, dqf, dqb, dkf, dkb, dvf, dvb, dg_o, dpf, dpb, zg) = [r[...] for r in ins]
        w = consts[0][...]
        o = outs[0]
        o[:, 0:LRU_W] = dxb.astype(BF16)
        o[:, LRU_W:2 * LRU_W] = dgt.astype(BF16)
        o[:, 1024:1024 + qk] = ((dqf + dqb) * (GLA_DK ** -0.5)).astype(BF16)
        o[:, 1280:1280 + qk] = (dkf + dkb).astype(BF16)
        o[:, 1536:1536 + GLA_W] = (dvf + dvb).astype(BF16)
        o[:, 2048:2048 + GLA_W] = dg_o.astype(BF16)
        dpre = jnp.concatenate([dpf, dpb], axis=1)
        dpre16 = dpre.astype(BF16)
        o[:, ZG_OFF:ZG_OFF + LANES] = _dot(dpre16, w, NT).astype(BF16)
        _acc_add(i, accs[0], _dot(zg.astype(BF16), dpre16, TN))
        _acc_add(i, accs[1], jnp.sum(dpre, axis=0, keepdims=True))

    ins = [(dx_br, LRU_W, 0), (dgate, LRU_W, 0), (dq_f, qk, 0), (dq_b, qk, 0), (dk_f, qk, 0), (dk_b, qk, 0),
           (dv_f, GLA_W, 0), (dv_b, GLA_W, 0), (dgo, GLA_W, 0), (dpre_f, qk, 0), (dpre_b, qk, 0), (z_zg, LANES, 0)]
    return _rows(body, t_rows, 384, ins, [wg], [(D_IN_PAD, BF16)],
                 [((LANES, 2 * qk), F32), ((1, 2 * qk), F32)], name)


LRU_RB = 384
HALO = SUBLANES


def _scan8(a, u, rev):
    rows = lax.broadcasted_iota(jnp.int32, a.shape, 0)
    for d in (1, 2, 4):
        if rev:
            a_s, u_s, ok = pltpu.roll(a, SUBLANES - d, 0), pltpu.roll(u, SUBLANES - d, 0), rows < SUBLANES - d
        else:
            a_s, u_s, ok = pltpu.roll(a, d, 0), pltpu.roll(u, d, 0), rows >= d
        u = jnp.where(ok, a * u_s + u, u)
        a = jnp.where(ok, a * a_s, a)
    return a, u


def _edge_row(x, rev):
    r = x[0:1, :] if rev else x[SUBLANES - 1:SUBLANES, :]
    return jnp.broadcast_to(r, x.shape)


def _scan_block(a, u, carry, rev):
    nsub = a.shape[0] // SUBLANES
    loc = [_scan8(a[sb * SUBLANES:(sb + 1) * SUBLANES], u[sb * SUBLANES:(sb + 1) * SUBLANES], rev) for sb in range(nsub)]
    edges = [(_edge_row(a_c, rev), _edge_row(h_l, rev)) for a_c, h_l in loc]
    carries = [None] * nsub
    for sb in (range(nsub - 1, -1, -1) if rev else range(nsub)):
        carries[sb] = carry
        carry = edges[sb][0] * carry + edges[sb][1]
    h = jnp.concatenate([h_l + a_c * cin for (a_c, h_l), cin in zip(loc, carries)], axis=0)
    return h, carry


def _lru_gates(xc, pre_a, pre_x, ba, bx, sp8):
    r = _sigmoid(pre_a + ba)
    i = _sigmoid(pre_x + bx)
    log_a = -(r * sp8)
    a = jnp.exp(log_a)
    y = 2.0 * log_a
    series = -y * (1.0 + y * (0.5 + y * (1.0 / 6.0 + y * (1.0 / 24.0 + y * (1.0 / 120.0)))))
    om = jnp.where(y > -0.25, series, 1.0 - a * a)
    s = jnp.sqrt(om)
    return r, i, a, s


def _softplus(x):
    return jnp.maximum(x, 0.0) + jnp.log(1.0 + jnp.exp(-jnp.abs(x)))


def _conv_taps(xpad_ref, r0, nrows, shifts=(-2, -1, 0, 1)):
    ext = xpad_ref[pl.ds(r0, nrows + 2 * HALO), :]
    taps = []
    for s in shifts:
        sh = ext if s == 0 else pltpu.roll(ext, (-s) % (nrows + 2 * HALO), 0)
        taps.append(sh[HALO:HALO + nrows])
    return taps


def _row_mask(r0, nrows, seq):
    rows = r0 + lax.broadcasted_iota(jnp.int32, (nrows, LANES), 0)
    return jnp.logical_and(rows >= FRONT - N_META, rows < FRONT + seq).astype(F32)


def _lru_load_conv(z_hbm, xpad, xc, sem, cw_ref, cb_ref, t_rows, seq):
    j = pl.program_id(0)
    zeros = jnp.zeros((HALO, LANES), F32)
    xpad[0:HALO, :] = zeros
    xpad[t_rows + HALO:t_rows + 2 * HALO, :] = zeros
    cp = pltpu.make_async_copy(z_hbm.at[:, pl.ds(pl.multiple_of(j * LANES, LANES), LANES)],
                               xpad.at[pl.ds(HALO, t_rows)], sem)
    cp.start()
    cp.wait()
    cw = cw_ref[...]
    cb = cb_ref[...]

    def conv_blk(r, carry):
        r0 = pl.multiple_of(r * LRU_RB, LRU_RB)
        taps = _conv_taps(xpad, r0, LRU_RB)
        acc = cb + cw[0:1, :] * taps[0]
        for k in range(1, 4):
            acc = acc + cw[k:k + 1, :] * taps[k]
        xc[pl.ds(r0, LRU_RB), :] = acc * _row_mask(r0, LRU_RB, seq)
        return carry

    lax.fori_loop(0, t_rows // LRU_RB, conv_blk, 0)


def _lru_specs(t_rows):
    return [pl.BlockSpec(memory_space=pl.ANY),
            pl.BlockSpec((4, LANES), lambda j: (0, j)),
            pl.BlockSpec((1, LANES), lambda j: (0, j)),
            pl.BlockSpec((1, LANES, 4 * LANES), lambda j: (j, 0, 0)),
            pl.BlockSpec((SUBLANES, LANES), lambda j: (0, j))]


def _lru_fwd(z_lru, conv_w, conv_b, wg, pb, t_rows, seq, name, comm=None):
    nblk = t_rows // LRU_RB
    nsub = LRU_RB // SUBLANES

    def kern(z_hbm, cw_ref, cb_ref, wg_ref, pb_ref, hf_ref, hb_ref, xpad, xc, sem):
        _lru_load_conv(z_hbm, xpad, xc, sem, cw_ref, cb_ref, t_rows, seq)
        w = wg_ref[0]
        pb_v = pb_ref[...]
        dirs = []
        for rev in (False, True):
            o = 2 if rev else 0
            dirs.append(dict(rev=rev, ba=pb_v[o:o + 1, :], bx=pb_v[o + 1:o + 2, :],
                             sp8=LRU_C * _softplus(-pb_v[5:6, :] if rev else -pb_v[4:5, :]),
                             w=w[:, o * LANES:(o + 2) * LANES], out=hb_ref if rev else hf_ref))

        def blk(it, carries):
            new = []
            for d, carry in zip(dirs, carries):
                r = (nblk - 1 - it) if d["rev"] else it
                r0 = pl.multiple_of(r * LRU_RB, LRU_RB)
                xcb = xc[pl.ds(r0, LRU_RB), :]
                pre = _dot(xcb.astype(BF16), d["w"])
                _, gi, a, s = _lru_gates(xcb, pre[:, 0:LANES], pre[:, LANES:2 * LANES], d["ba"], d["bx"], d["sp8"])
                h, carry = _scan_block(a, s * (gi * xcb), carry, d["rev"])
                d["out"][pl.ds(r0, LRU_RB), :] = h
                new.append(carry)
            return tuple(new)

        z8 = jnp.zeros((SUBLANES, LANES), F32)
        lax.fori_loop(0, nblk, blk, (z8, z8), unroll=2)

    res, comm_res = _pcall(
        kern, name, (LRU_W // LANES,), _lru_specs(t_rows), [pl.BlockSpec((t_rows, LANES), lambda j: (0, j))] * 2,
        [jax.ShapeDtypeStruct((t_rows, LRU_W), F32)] * 2,
        [pltpu.VMEM((t_rows + 2 * HALO, LANES), F32), pltpu.VMEM((t_rows, LANES), F32), pltpu.SemaphoreType.DMA],
        ("arbitrary",), (z_lru, conv_w, conv_b, wg, pb), comm)
    return res if comm is None else (res, comm_res)


def _lru_bwd(z_lru, conv_w, conv_b, wg, pb, h_f, h_b, dh, t_rows, seq, name, comm=None):
    nblk = t_rows // LRU_RB
    nsub = LRU_RB // SUBLANES

    def kern(z_hbm, cw_ref, cb_ref, wg_ref, pb_ref, hf_ref, hb_ref, dh_ref,
             dx_ref, dcw_ref, dcb_ref, dwh_ref, dpb_ref, xpad, xc, dxc, dwg, sem):
        _lru_load_conv(z_hbm, xpad, xc, sem, cw_ref, cb_ref, t_rows, seq)
        w = wg_ref[0]
        pb_v = pb_ref[...]
        zeros = jnp.zeros((HALO, LANES), F32)
        dxc[0:HALO, :] = zeros
        dxc[t_rows + HALO:t_rows + 2 * HALO, :] = zeros
        dwg[...] = jnp.zeros_like(dwg)

        def zero_blk(r, carry):
            dxc[pl.ds(pl.multiple_of(r * LRU_RB, LRU_RB) + HALO, LRU_RB), :] = jnp.zeros((LRU_RB, LANES), F32)
            return carry

        lax.fori_loop(0, nblk, zero_blk, 0)
        dirs = []
        for rev in (False, True):
            o = 2 if rev else 0
            lam = pb_v[5:6, :] if rev else pb_v[4:5, :]
            dirs.append(dict(rev=rev, o=o, ba=pb_v[o:o + 1, :], bx=pb_v[o + 1:o + 2, :], sp8=LRU_C * _softplus(-lam),
                             dlam_scale=LRU_C * _sigmoid(-lam), h_ref=hb_ref if rev else hf_ref,
                             w=w[:, o * LANES:(o + 2) * LANES]))
        rows_b = lax.broadcasted_iota(jnp.int32, (LRU_RB, LANES), 0)

        def blk(it, carries):
            new = []
            for d, (nu_c, acc_ba, acc_bx, acc_lam) in zip(dirs, carries):
                rev, o, h_ref = d["rev"], d["o"], d["h_ref"]
                adj_rev = not rev
                r = (nblk - 1 - it) if adj_rev else it
                r0 = pl.multiple_of(r * LRU_RB, LRU_RB)
                xcb = xc[pl.ds(r0, LRU_RB), :]
                xc16 = xcb.astype(BF16)
                pre = _dot(xc16, d["w"])
                gr, gi, a, s = _lru_gates(xcb, pre[:, 0:LANES], pre[:, LANES:2 * LANES], d["ba"], d["bx"], d["sp8"])
                dhb = dh_ref[pl.ds(r0, LRU_RB), :]
                hb = h_ref[pl.ds(r0, LRU_RB), :]
                if rev:
                    nxt0 = pl.multiple_of(jnp.minimum(r0 + LRU_RB, t_rows - SUBLANES), SUBLANES)
                    edge = h_ref[pl.ds(nxt0, SUBLANES), :][0:1, :] * (r < nblk - 1).astype(F32)
                    h_prev = jnp.concatenate([hb[1:], edge], axis=0)
                else:
                    prv0 = pl.multiple_of(jnp.maximum(r0 - SUBLANES, 0), SUBLANES)
                    edge = h_ref[pl.ds(prv0, SUBLANES), :][SUBLANES - 1:SUBLANES, :] * (r > 0).astype(F32)
                    h_prev = jnp.concatenate([edge, hb[:-1]], axis=0)
                nu, nu_out = _scan_block(a, a * dhb, nu_c, adj_rev)
                cin = jnp.broadcast_to(nu_c[0:1, :], (LRU_RB, LANES))
                if adj_rev:
                    nu_next = jnp.where(rows_b == LRU_RB - 1, cin, pltpu.roll(nu, LRU_RB - 1, 0))
                else:
                    nu_next = jnp.where(rows_b == 0, cin, pltpu.roll(nu, 1, 0))
                du = dhb + nu_next
                da = du * h_prev
                ix = gi * xcb
                dlog = da * a - du * ix * (a * a) / s
                d_i = du * s * xcb
                dr = -dlog * d["sp8"]
                dpa = dr * gr * (1.0 - gr)
                dpx = d_i * gi * (1.0 - gi)
                dp16 = jnp.concatenate([dpa, dpx], axis=1).astype(BF16)
                dxc[pl.ds(r0 + HALO, LRU_RB), :] += du * s * gi + _dot(dp16, d["w"], NT)
                dwg[:, o * LANES:(o + 2) * LANES] += _dot(xc16, dp16, TN)
                new.append((nu_out, acc_ba + jnp.sum(dpa, axis=0, keepdims=True),
                            acc_bx + jnp.sum(dpx, axis=0, keepdims=True),
                            acc_lam + jnp.sum(dlog * gr, axis=0, keepdims=True)))
            return tuple(new)

        z1 = jnp.zeros((1, LANES), F32)
        init = (jnp.zeros((SUBLANES, LANES), F32), z1, z1, z1)
        (_, ba_f, bx_f, lam_f), (_, ba_b, bx_b, lam_b) = lax.fori_loop(0, nblk, blk, (init, init), unroll=2)
        dpb_ref[...] = jnp.concatenate([ba_f, bx_f, ba_b, bx_b, lam_f * dirs[0]["dlam_scale"],
                                        lam_b * dirs[1]["dlam_scale"], z1, z1], axis=0)
        for hh in range(2):
            for o in range(4):
                c0 = o * LANES + hh * LRU_HD
                dwh_ref[hh, :, o * LRU_HD:(o + 1) * LRU_HD] = dwg[hh * LRU_HD:(hh + 1) * LRU_HD, c0:c0 + LRU_HD]

        def mask_blk(r, carry):
            r0 = pl.multiple_of(r * LRU_RB, LRU_RB)
            dxc[pl.ds(r0 + HALO, LRU_RB), :] = dxc[pl.ds(r0 + HALO, LRU_RB), :] * _row_mask(r0, LRU_RB, seq)
            return carry

        lax.fori_loop(0, nblk, mask_blk, 0)

        cw = cw_ref[...]

        def conv_bwd(r, carry):
            r0 = pl.multiple_of(r * LRU_RB, LRU_RB)
            gt = _conv_taps(dxc, r0, LRU_RB, (2, 1, 0, -1))
            xt = _conv_taps(xpad, r0, LRU_RB)
            dx_ref[pl.ds(r0, LRU_RB), :] = (cw[0:1, :] * gt[0] + cw[1:2, :] * gt[1] + cw[2:3, :] * gt[2]
                                           + cw[3:4, :] * gt[3])
            g = gt[2]
            new = [carry[k] + jnp.sum(g * xt[k], axis=0, keepdims=True) for k in range(4)]
            new.append(carry[4] + jnp.sum(g, axis=0, keepdims=True))
            return tuple(new)

        z1 = jnp.zeros((1, LANES), F32)
        sums = lax.fori_loop(0, nblk, conv_bwd, (z1, z1, z1, z1, z1))
        dcw_ref[...] = jnp.concatenate(sums[:4], axis=0)
        dcb_ref[...] = sums[4]

    col = lambda j: (0, j)
    res, comm_res = _pcall(
        kern, name, (LRU_W // LANES,), _lru_specs(t_rows) + [pl.BlockSpec((t_rows, LANES), col)] * 3,
        [pl.BlockSpec((t_rows, LANES), col), pl.BlockSpec((4, LANES), col), pl.BlockSpec((1, LANES), col),
         pl.BlockSpec((2, LRU_HD, 4 * LRU_HD), lambda j: (j, 0, 0)), pl.BlockSpec((SUBLANES, LANES), col)],
        [jax.ShapeDtypeStruct((t_rows, LRU_W), F32), jax.ShapeDtypeStruct((4, LRU_W), F32),
         jax.ShapeDtypeStruct((1, LRU_W), F32), jax.ShapeDtypeStruct((LRU_HEADS, LRU_HD, 4 * LRU_HD), F32),
         jax.ShapeDtypeStruct((SUBLANES, LRU_W), F32)],
        [pltpu.VMEM((t_rows + 2 * HALO, LANES), F32), pltpu.VMEM((t_rows, LANES), F32),
         pltpu.VMEM((t_rows + 2 * HALO, LANES), F32), pltpu.VMEM((LANES, 4 * LANES), F32), pltpu.SemaphoreType.DMA],
        ("arbitrary",), (z_lru, conv_w, conv_b, wg, pb, h_f, h_b, dh), comm)
    return res if comm is None else (res, comm_res)


QK = GLA_H * GLA_DK
GLA_G = 4


def _cumsum_rows(x, rev):
    n = x.shape[0]
    rows = lax.broadcasted_iota(jnp.int32, x.shape, 0)
    d = 1
    while d < n:
        if rev:
            x = x + jnp.where(rows < n - d, pltpu.roll(x, n - d, 0), 0.0)
        else:
            x = x + jnp.where(rows >= d, pltpu.roll(x, d, 0), 0.0)
        d *= 2
    return x


def _gla_chunk_terms(q, k, zg, wg, bg, rev):
    pre = (_dot(zg.astype(BF16), wg) + bg)[:, (QK if rev else 0):(2 * QK if rev else QK)]
    g = (jnp.minimum(pre, 0.0) - jnp.log(1.0 + jnp.exp(-jnp.abs(pre)))) * (1.0 / GLA_GATE_NORM)
    b = _cumsum_rows(g, rev)
    b_last = b[0:1, :] if rev else b[CHUNK - 1:CHUNK, :]
    b_mid = b[CHUNK // 2:CHUNK // 2 + 1, :]
    qs = q * (GLA_DK ** -0.5)
    terms = dict(
        pre=pre, qs=qs, k=k,
        eb=jnp.exp(b), ek=jnp.exp(b_last - b), e_last=jnp.exp(b_last),
        eqm=jnp.exp(jnp.minimum(b - b_mid, EXP_CLAMP)), ekm=jnp.exp(jnp.minimum(b_mid - b, EXP_CLAMP)))
    return terms


def _gla_mask(rev):
    t = lax.broadcasted_iota(jnp.int32, (CHUNK, CHUNK), 0)
    s = lax.broadcasted_iota(jnp.int32, (CHUNK, CHUNK), 1)
    return (s > t) if rev else (s <= t)


def _gla_head_ops(tm, hd):
    sl = slice(hd * GLA_DK, (hd + 1) * GLA_DK)
    q_b = (tm["qs"][:, sl] * tm["eb"][:, sl]).astype(BF16)
    k_e = (tm["k"][:, sl] * tm["ek"][:, sl]).astype(BF16)
    q_m = (tm["qs"][:, sl] * tm["eqm"][:, sl]).astype(BF16)
    k_m = (tm["k"][:, sl] * tm["ekm"][:, sl]).astype(BF16)
    return sl, q_b, k_e, q_m, k_m


def _gla_fwd(z_qk, z_v, z_zg, wg, bg, t_rows, name, comm=None):
    nc = t_rows // CHUNK
    ng = nc // GLA_G
    rows = CHUNK * GLA_G

    def kern(qf, kf, vf, zf, qb, kb, vb, zb, wg_ref, bg_ref, of_ref, ob_ref, sf_ref, sb_ref, st_f, st_b):
        i = pl.program_id(0)

        @pl.when(i == 0)
        def _():
            st_f[...] = jnp.zeros_like(st_f)
            st_b[...] = jnp.zeros_like(st_b)

        wg_v, bg_v = wg_ref[...], bg_ref[...]
        for rev, (q_r, k_r, v_r, z_r, o_r, s_r, st) in ((False, (qf, kf, vf, zf, of_ref, sf_ref, st_f)),
                                                        (True, (qb, kb, vb, zb, ob_ref, sb_ref, st_b))):
            mask = _gla_mask(rev)
            for ck in (range(GLA_G - 1, -1, -1) if rev else range(GLA_G)):
                cr = slice(ck * CHUNK, (ck + 1) * CHUNK)
                tm = _gla_chunk_terms(q_r[cr, :], k_r[cr, :], z_r[cr, :], wg_v, bg_v, rev)
                v = v_r[cr, :]
                for hd in range(GLA_H):
                    sl, q_b, k_e, q_m, k_m = _gla_head_ops(tm, hd)
                    vs = slice(hd * GLA_DV, (hd + 1) * GLA_DV)
                    v16 = v[:, vs].astype(BF16)
                    a = jnp.where(mask, _dot(q_m, k_m, NT), 0.0).astype(BF16)
                    s_in = st[hd]
                    s_r[ck, hd] = s_in
                    o_r[cr, vs] = _dot(a, v16) + _dot(q_b, s_in.astype(BF16), NT)
                    st[hd] = s_in * tm["e_last"][:, sl] + _dot(v16, k_e, TN)

    fw = lambda c: (lambda i: (i, c))
    rv = lambda c: (lambda i: (ng - 1 - i, c))
    def side(ix):
        return [pl.BlockSpec((rows, QK), ix(0)), pl.BlockSpec((rows, QK), ix(1)),
                pl.BlockSpec((rows, GLA_W), ix(0)), pl.BlockSpec((rows, LANES), ix(0))]
    st_shape = (nc, GLA_H, GLA_DV, GLA_DK)
    res, comm_res = _pcall(
        kern, name, (ng,),
        side(fw) + side(rv) + [pl.BlockSpec(wg.shape, lambda i: (0, 0)), pl.BlockSpec(bg.shape, lambda i: (0, 0))],
        [pl.BlockSpec((rows, GLA_W), fw(0)), pl.BlockSpec((rows, GLA_W), rv(0)),
         pl.BlockSpec((GLA_G,) + st_shape[1:], lambda i: (i, 0, 0, 0)),
         pl.BlockSpec((GLA_G,) + st_shape[1:], lambda i: (ng - 1 - i, 0, 0, 0))],
        [jax.ShapeDtypeStruct((t_rows, GLA_W), F32)] * 2 + [jax.ShapeDtypeStruct(st_shape, F32)] * 2,
        [pltpu.VMEM(st_shape[1:], F32)] * 2, ("arbitrary",),
        (z_qk, z_qk, z_v, z_zg, z_qk, z_qk, z_v, z_zg, wg, bg), comm)
    return res if comm is None else (res, comm_res)


def _gla_bwd(z_qk, z_v, z_zg, wg, bg, do, s_f, s_b, t_rows, name, comm=None):
    nc = t_rows // CHUNK

    def kern(qf, kf, vf, zf, dof, ssf, qb, kb, vb, zb, dob, ssb, wg_ref, bg_ref,
             dqf, dkf, dvf, dpf, dqb, dkb, dvb, dpb, ds_f, ds_b):
        i = pl.program_id(0)

        @pl.when(i == 0)
        def _():
            ds_f[...] = jnp.zeros_like(ds_f)
            ds_b[...] = jnp.zeros_like(ds_b)

        wg_v, bg_v = wg_ref[...], bg_ref[...]
        sides = ((False, (qf, kf, vf, zf, dof, ssf, dqf, dkf, dvf, dpf, ds_f)),
                 (True, (qb, kb, vb, zb, dob, ssb, dqb, dkb, dvb, dpb, ds_b)))
        for rev, (q_r, k_r, v_r, z_r, do_r, ss_r, dq_r, dk_r, dv_r, dp_r, ds) in sides:
            mask = _gla_mask(rev)
            for ck in (range(GLA_G) if rev else range(GLA_G - 1, -1, -1)):
                cr = slice(ck * CHUNK, (ck + 1) * CHUNK)
                tm = _gla_chunk_terms(q_r[cr, :], k_r[cr, :], z_r[cr, :], wg_v, bg_v, rev)
                v = v_r[cr, :]
                d_o = do_r[cr, :]
                db_parts, cross_parts = [], []
                for hd in range(GLA_H):
                    sl, q_b, k_e, q_m, k_m = _gla_head_ops(tm, hd)
                    vs = slice(hd * GLA_DV, (hd + 1) * GLA_DV)
                    v16, do16 = v[:, vs].astype(BF16), d_o[:, vs].astype(BF16)
                    a = jnp.where(mask, _dot(q_m, k_m, NT), 0.0).astype(BF16)
                    da = jnp.where(mask, _dot(do16, v16, NT), 0.0).astype(BF16)
                    s_in = ss_r[ck, hd]
                    s_in16 = s_in.astype(BF16)
                    ds_out = ds[hd]
                    ds16 = ds_out.astype(BF16)
                    dv_r[cr, vs] = _dot(a, do16, TN) + _dot(k_e, ds16, NT)
                    dqs = _dot(da, k_m) * tm["eqm"][:, sl] + _dot(do16, s_in16) * tm["eb"][:, sl]
                    dk = _dot(da, q_m, TN) * tm["ekm"][:, sl] + _dot(v16, ds16) * tm["ek"][:, sl]
                    ds[hd] = ds_out * tm["e_last"][:, sl] + _dot(do16, q_b, TN)
                    dq_r[cr, sl] = dqs
                    dk_r[cr, sl] = dk
                    db_parts.append(tm["qs"][:, sl] * dqs - tm["k"][:, sl] * dk)
                    s_out = s_in * tm["e_last"][:, sl] + _dot(v16, k_e, TN)
                    cross_parts.append(jnp.sum(s_out * ds_out, axis=0, keepdims=True))
                d_b = jnp.concatenate(db_parts, axis=1)
                dg = _cumsum_rows(d_b, not rev) + jnp.concatenate(cross_parts, axis=1)
                dp_r[cr, :] = dg * (1.0 / GLA_GATE_NORM) * _sigmoid(-tm["pre"])

    ng = nc // GLA_G
    rows = CHUNK * GLA_G
    fw = lambda c: (lambda i: (ng - 1 - i, c))
    rv = lambda c: (lambda i: (i, c))
    st_blk = (GLA_G, GLA_H, GLA_DV, GLA_DK)
    def side(ix, st_ix):
        return [pl.BlockSpec((rows, QK), ix(0)), pl.BlockSpec((rows, QK), ix(1)),
                pl.BlockSpec((rows, GLA_W), ix(0)), pl.BlockSpec((rows, LANES), ix(0)),
                pl.BlockSpec((rows, GLA_W), ix(0)), pl.BlockSpec(st_blk, st_ix)]
    def oside(ix):
        return [pl.BlockSpec((rows, QK), ix(0)), pl.BlockSpec((rows, QK), ix(0)),
                pl.BlockSpec((rows, GLA_W), ix(0)), pl.BlockSpec((rows, QK), ix(0))]
    o_shapes = [jax.ShapeDtypeStruct((t_rows, QK), F32), jax.ShapeDtypeStruct((t_rows, QK), F32),
                jax.ShapeDtypeStruct((t_rows, GLA_W), F32), jax.ShapeDtypeStruct((t_rows, QK), F32)]
    res, comm_res = _pcall(
        kern, name, (ng,),
        (side(fw, lambda i: (ng - 1 - i, 0, 0, 0)) + side(rv, lambda i: (i, 0, 0, 0))
         + [pl.BlockSpec(wg.shape, lambda i: (0, 0)), pl.BlockSpec(bg.shape, lambda i: (0, 0))]),
        oside(fw) + oside(rv), o_shapes * 2, [pltpu.VMEM((GLA_H, GLA_DV, GLA_DK), F32)] * 2, ("arbitrary",),
        (z_qk, z_qk, z_v, z_zg, do, s_f, z_qk, z_qk, z_v, z_zg, do, s_b, wg, bg), comm)
    return res if comm is None else (res, comm_res)


FLAT_W = 1024


def _adam_math(wv, gv, mv, vv):
    bc1 = 1.0 - ADAM_B1 ** ADAM_STEP
    bc2 = 1.0 - ADAM_B2 ** ADAM_STEP
    m_new = ADAM_B1 * mv + (1.0 - ADAM_B1) * gv
    v_new = ADAM_B2 * vv + (1.0 - ADAM_B2) * (gv * gv)
    delta = -ADAM_LR * ((m_new / bc1) / (jnp.sqrt(v_new / bc2) + ADAM_EPS) + ADAM_WD * wv)
    return delta, m_new, v_new


def _row_tile(rows, cap=256):
    t = cap
    while rows % t:
        t //= 2
    assert t >= SUBLANES, rows
    return t


def _adamw_layers(w, m, v, g_layers, name):
    depth, rows, cols = w.shape
    tr = _row_tile(rows)

    def kern(w_ref, m_ref, v_ref, *rest):
        g_refs, (go_ref, d_ref, mo_ref, vo_ref) = rest[:depth], rest[depth:]
        l = pl.program_id(0)
        gv = g_refs[0][...]
        for k in range(1, depth):
            gv = jnp.where(l == k, g_refs[k][...], gv)
        delta, m_new, v_new = _adam_math(w_ref[...], gv, m_ref[...], v_ref[...])
        go_ref[...] = gv
        d_ref[...] = delta
        mo_ref[...] = m_new
        vo_ref[...] = v_new

    stacked = pl.BlockSpec((None, tr, cols), lambda l, i: (l, i, 0))
    return pl.pallas_call(
        kern, name=name, grid=(depth, rows // tr),
        in_specs=[stacked] * 3 + [pl.BlockSpec((tr, cols), lambda l, i: (i, 0))] * depth,
        out_specs=[stacked] * 4, out_shape=[jax.ShapeDtypeStruct(w.shape, F32)] * 4,
        compiler_params=_cparams(("arbitrary", "arbitrary")),
    )(w, m, v, *g_layers)


def _adamw_small(ws, gs, ms, vs):
    n = len(ws)

    def kern(*refs):
        ins, outs = refs[:4 * n], refs[4 * n:]
        for k in range(n):
            delta, m_new, v_new = _adam_math(ins[k][...], ins[n + k][...], ins[2 * n + k][...], ins[3 * n + k][...])
            outs[k][...] = delta
            outs[n + k][...] = m_new
            outs[2 * n + k][...] = v_new

    shapes = [jax.ShapeDtypeStruct(w.shape, F32) for w in ws]
    res = pl.pallas_call(kern, name="adamw_small", out_shape=shapes * 3,
                         compiler_params=pltpu.CompilerParams(vmem_limit_bytes=VMEM_LIMIT))(*ws, *gs, *ms, *vs)
    return res[:n], res[n:2 * n], res[2 * n:]


def _add_sibling(g, got, c_idx, name):
    _, _, rows, cols = g.shape
    tr = _row_tile(rows)

    def kern(c_ref, g_ref, got_ref, o_ref):
        o_ref[...] = (g_ref[...] + got_ref[...]).astype(BF16)

    grid_spec = pltpu.PrefetchScalarGridSpec(
        num_scalar_prefetch=1, grid=(N_CHIPS, rows // tr),
        in_specs=[pl.BlockSpec((None, None, tr, cols), lambda q, i, c_ref: (q, c_ref[0], i, 0)),
                  pl.BlockSpec((None, tr, cols), lambda q, i, c_ref: (q, i, 0))],
        out_specs=pl.BlockSpec((None, tr, cols), lambda q, i, c_ref: (q, i, 0)))
    return pl.pallas_call(
        kern, name=name, grid_spec=grid_spec, out_shape=jax.ShapeDtypeStruct((N_CHIPS, rows, cols), BF16),
        compiler_params=_cparams(("arbitrary", "arbitrary")),
    )(c_idx, g, got)


def _sum_chips(x, name):
    _, rows, cols = x.shape
    tr = _row_tile(rows)

    def kern(x_ref, o_ref):
        xv = x_ref[...].astype(F32)
        o_ref[...] = ((xv[0] + xv[1]) + xv[2]) + xv[3]

    return pl.pallas_call(
        kern, name=name, grid=(rows // tr,),
        in_specs=[pl.BlockSpec((N_CHIPS, tr, cols), lambda i: (0, i, 0))],
        out_specs=pl.BlockSpec((tr, cols), lambda i: (i, 0)),
        out_shape=jax.ShapeDtypeStruct((rows, cols), F32),
        compiler_params=_cparams(("arbitrary",)),
    )(x)


def _place():
    x, y, c = lax.axis_index("x"), lax.axis_index("y"), lax.axis_index("c")
    return x, y, c


def _other_chips(x, y):
    return [(1 - x, y), (x, 1 - y), (1 - x, 1 - y)]


class _Comm:
    def __init__(self, ins, out_shapes, n_sems, stage, body):
        self.ins, self.out_shapes, self.body = list(ins), list(out_shapes), body
        self.scratch = [pltpu.SemaphoreType.DMA((n,)) for n in n_sems] + [pltpu.VMEM(s, d) for s, d in stage]


ANY_SPEC = pl.BlockSpec(memory_space=pl.ANY)


def _run_comm(comm, name):
    def kern(*refs):
        comm.body(refs, True, True)

    return pl.pallas_call(
        kern, name=name, in_specs=[ANY_SPEC] * len(comm.ins), out_specs=[ANY_SPEC] * len(comm.out_shapes),
        out_shape=comm.out_shapes, scratch_shapes=comm.scratch,
        compiler_params=pltpu.CompilerParams(has_side_effects=True, vmem_limit_bytes=VMEM_LIMIT),
    )(*comm.ins)


def _pcall(kern, name, grid, in_specs, out_specs, out_shape, scratch_shapes, sem, args, comm=None):
    if comm is None:
        return pl.pallas_call(kern, name=name, grid=grid, in_specs=in_specs, out_specs=out_specs, out_shape=out_shape,
                              scratch_shapes=scratch_shapes, compiler_params=_cparams(sem))(*args), None
    n_in, n_out, n_scr = len(in_specs), len(out_specs), len(scratch_shapes)
    c_in, c_out = len(comm.ins), len(comm.out_shapes)

    def hosted(*refs):
        a = n_in + c_in
        b = a + n_out + c_out
        main = refs[:n_in] + refs[a:a + n_out] + refs[b:b + n_scr]
        extra = refs[n_in:a] + refs[a + n_out:b] + refs[b + n_scr:]
        ids = [pl.program_id(d) for d in range(len(grid))]
        first, last = ids[0] == 0, ids[0] == grid[0] - 1
        for i, g in zip(ids[1:], grid[1:]):
            first, last = jnp.logical_and(first, i == 0), jnp.logical_and(last, i == g - 1)

        @pl.when(first)
        def _():
            comm.body(extra, True, False)

        kern(*main)

        @pl.when(last)
        def _():
            comm.body(extra, False, True)

    res = pl.pallas_call(
        hosted, name=name, grid=grid, in_specs=list(in_specs) + [ANY_SPEC] * c_in,
        out_specs=list(out_specs) + [ANY_SPEC] * c_out, out_shape=list(out_shape) + comm.out_shapes,
        scratch_shapes=list(scratch_shapes) + comm.scratch,
        compiler_params=pltpu.CompilerParams(dimension_semantics=("arbitrary",) * len(grid),
                                             vmem_limit_bytes=VMEM_LIMIT, has_side_effects=True),
    )(*args, *comm.ins)
    return res[:n_out], res[n_out:]


class _StagedCopies:
    def __init__(self, pairs, bufs, sem_in, sem_out):
        self.ins = [pltpu.make_async_copy(src, buf, sem_in.at[i]) for i, ((src, _), buf) in enumerate(zip(pairs, bufs))]
        self.outs = [pltpu.make_async_copy(buf, dst, sem_out.at[i]) for i, ((_, dst), buf) in enumerate(zip(pairs, bufs))]

    def load(self):
        for cp in self.ins:
            cp.start()

    def store(self):
        for cp in self.ins:
            cp.wait()
        for cp in self.outs:
            cp.start()

    def finish(self):
        for cp in self.outs:
            cp.wait()


DMA_CHUNK_BYTES = 1 << 20
DMA_MAX_CHUNKS = 4


def _row_chunks(rows, row_bytes, align=16):
    units = rows // align
    k = max(1, min(DMA_MAX_CHUNKS, -(-rows * row_bytes // DMA_CHUNK_BYTES), units))
    out, start = [], 0
    for i in range(k):
        size = (units // k + (1 if i < units % k else 0)) * align
        out.append((start, size))
        start += size
    if start < rows:
        out[-1] = (out[-1][0], out[-1][1] + rows - start)
    return out


def _row_bytes(shape, dtype):
    return math.prod(shape[1:]) * jnp.dtype(dtype).itemsize


def _remote(src, dst, send, recv, idx, dev):
    return pltpu.make_async_remote_copy(src, dst, send.at[idx], recv.at[idx], device_id=dev, device_id_type=MESH_ID)


def _allgather_chips(xs):
    na = len(xs)
    chunks = [_row_chunks(x.shape[1], _row_bytes(x.shape[1:], x.dtype)) for x in xs]
    n_cp = 3 * sum(len(ch) for ch in chunks)

    def body(refs, start, finish):
        x_refs, o_refs = refs[:na], refs[na:2 * na]
        send1, recv1, send2, recv2, loc_in, loc_out = refs[2 * na:2 * na + 6]
        px, py, c = _place()
        me = 2 * px + py
        sib = (px, py, 1 - c)
        own = _StagedCopies([(x, o.at[me]) for x, o in zip(x_refs, o_refs)], refs[2 * na + 6:], loc_in, loc_out)
        plan = [(a, qx, qy, pl.ds(s, n)) for a in range(na) for (qx, qy) in _other_chips(px, py) for (s, n) in chunks[a]]
        first = [_remote(x_refs[a].at[c, rows], o_refs[a].at[me, c, rows], send1, recv1, i, (qx, qy, c))
                 for i, (a, qx, qy, rows) in enumerate(plan)]
        if start:
            own.load()
            for cp in first:
                cp.start()
        if finish:
            own.store()
            passed = []
            for i, (a, qx, qy, rows) in enumerate(plan):
                slot = o_refs[a].at[2 * qx + qy, c, rows]
                _remote(slot, slot, send1, recv1, i, (qx, qy, c)).wait_recv()
                fwd = _remote(slot, slot, send2, recv2, i, sib)
                fwd.start()
                passed.append(fwd)
            for i, (a, qx, qy, rows) in enumerate(plan):
                slot = o_refs[a].at[2 * qx + qy, 1 - c, rows]
                _remote(slot, slot, send2, recv2, i, sib).wait_recv()
            for cp in first + passed:
                cp.wait_send()
            own.finish()

    outs = [jax.ShapeDtypeStruct((N_CHIPS,) + x.shape, x.dtype) for x in xs]
    return _Comm(xs, outs, [n_cp, n_cp, n_cp, n_cp, na, na], [(x.shape, x.dtype) for x in xs], body)


def _sibling_halves(gs):
    na = len(gs)
    chunks = [_row_chunks(g.shape[2], _row_bytes(g.shape[2:], g.dtype)) for g in gs]
    n_cp = N_CHIPS * sum(len(ch) for ch in chunks)

    def body(refs, start, finish):
        g_refs, o_refs = refs[:na], refs[na:2 * na]
        send, recv = refs[2 * na:]
        px, py, c = _place()
        plan = [(a, q, pl.ds(s, n)) for a in range(na) for q in range(N_CHIPS) for (s, n) in chunks[a]]
        cps = [_remote(g_refs[a].at[q, 1 - c, rows], o_refs[a].at[q, rows], send, recv, i, (px, py, 1 - c))
               for i, (a, q, rows) in enumerate(plan)]
        if start:
            for cp in cps:
                cp.start()
        if finish:
            for cp in cps:
                cp.wait()

    outs = [jax.ShapeDtypeStruct((N_CHIPS,) + g.shape[2:], g.dtype) for g in gs]
    return _Comm(gs, outs, [n_cp, n_cp], [], body)


def _chip_exchange(ps):
    na = len(ps)
    chunks = [_row_chunks(p.shape[1], _row_bytes(p.shape[1:], p.dtype)) for p in ps]
    n_cp = 3 * sum(len(ch) for ch in chunks)

    def body(refs, start, finish):
        p_refs, o_refs = refs[:na], refs[na:2 * na]
        send, recv, loc_in, loc_out = refs[2 * na:2 * na + 4]
        px, py, c = _place()
        me = 2 * px + py
        own = _StagedCopies([(p.at[me], o.at[me]) for p, o in zip(p_refs, o_refs)], refs[2 * na + 4:], loc_in, loc_out)
        plan = [(a, qx, qy, pl.ds(s, n)) for a in range(na) for (qx, qy) in _other_chips(px, py) for (s, n) in chunks[a]]
        cps = [_remote(p_refs[a].at[2 * qx + qy, rows], o_refs[a].at[me, rows], send, recv, i, (qx, qy, c))
               for i, (a, qx, qy, rows) in enumerate(plan)]
        if start:
            own.load()
            for cp in cps:
                cp.start()
        if finish:
            own.store()
            for cp in cps:
                cp.wait()
            own.finish()

    outs = [jax.ShapeDtypeStruct(p.shape, p.dtype) for p in ps]
    return _Comm(ps, outs, [n_cp, n_cp, na, na], [(p.shape[1:], p.dtype) for p in ps], body)


def _sibling_join(rs):
    na = len(rs)
    chunks = [_row_chunks(r.shape[0], _row_bytes(r.shape, r.dtype)) for r in rs]
    n_cp = sum(len(ch) for ch in chunks)

    def body(refs, start, finish):
        r_refs, o_refs = refs[:na], refs[na:2 * na]
        send, recv, loc_in, loc_out = refs[2 * na:2 * na + 4]
        px, py, c = _place()
        own = _StagedCopies([(r, o.at[c]) for r, o in zip(r_refs, o_refs)], refs[2 * na + 4:], loc_in, loc_out)
        plan = [(a, pl.ds(s, n)) for a in range(na) for (s, n) in chunks[a]]
        cps = [_remote(r_refs[a].at[rows], o_refs[a].at[c, rows], send, recv, i, (px, py, 1 - c))
               for i, (a, rows) in enumerate(plan)]
        if start:
            own.load()
            for cp in cps:
                cp.start()
        if finish:
            own.store()
            for cp in cps:
                cp.wait()
            own.finish()

    outs = [jax.ShapeDtypeStruct((2,) + r.shape, r.dtype) for r in rs]
    return _Comm(rs, outs, [n_cp, n_cp, na, na], [(r.shape, r.dtype) for r in rs], body)


def _reduce_stages(gs, c_idx, tag):
    got = yield _sibling_halves(gs)
    part = [_add_sibling(g, o, c_idx, f"reduce_add_sibling{tag}_{a}") for a, (g, o) in enumerate(zip(gs, got))]
    landed = yield _chip_exchange(part)
    mine = [_sum_chips(x, f"reduce_sum_chips{tag}_{a}") for a, x in enumerate(landed)]
    return (yield _sibling_join(mine))


class _Staged:
    def __init__(self, gen, tag):
        self.gen, self.tag, self.k, self.value = gen, tag, 0, None
        self.cur = next(gen)

    def stage(self):
        return self.cur

    def done(self, results):
        self.k += 1
        try:
            self.cur = self.gen.send(results)
        except StopIteration as stop:
            self.cur, self.value = None, stop.value

    def drain(self):
        while self.cur is not None:
            self.done(_run_comm(self.cur, f"comm{self.tag}_{self.k}"))
        return self.value


WEIGHTS = ["meta_tokens", "norm_mix_pre", "norm_mix_post", "norm_mlp_pre", "norm_mlp_post", "w_in", "conv_w",
           "conv_b", "lru_wa_f", "lru_ba_f", "lru_wx_f", "lru_bx_f", "lru_lambda_f", "lru_wa_b", "lru_ba_b",
           "lru_wx_b", "lru_bx_b", "lru_lambda_b", "gla_wg_f", "gla_bg_f", "gla_wg_b", "gla_bg_b", "gla_head_norm",
           "w_out", "w_mlp_up", "w_mlp_down"]
BIG = ("w_in", "w_out", "w_mlp_up", "w_mlp_down")
SMALL = [n for n in WEIGHTS if n not in BIG]
SMALL_SHARDED = {"meta_tokens": 1, "conv_w": 2, "gla_wg_f": 2, "gla_wg_b": 2}
N_CHIPS = 4
SMALL_RAW = [("g1", (1, D_MODEL)), ("g2", (1, D_MODEL)), ("g3", (1, D_MODEL)), ("g4", (1, D_MODEL)),
             ("conv_w", (4, LRU_W)), ("conv_b", (1, LRU_W)), ("lru_wg", (LRU_HEADS, LRU_HD, 4 * LRU_HD)),
             ("lru_pb", (SUBLANES, LRU_W)), ("gla_wg", (2 * GLA_RANK, 2 * QK)), ("gla_bg", (1, 2 * QK)),
             ("head_norm", (1, GLA_W))]
META_SHAPE = (N_META, D_MODEL)


def _pad_to(v, n):
    return jnp.pad(v, (0, n - v.shape[0]))


def _round_up(n, m):
    return -(-n // m) * m


def _flat_cat(arrs, total):
    return _pad_to(jnp.concatenate([a.reshape(-1) for a in arrs]), total)


def _split_flat(flat, shapes):
    out, off = [], 0
    for s in shapes:
        n = math.prod(s)
        out.append(flat[off:off + n].reshape(s))
        off += n
    return out


def _my_shard(full, axis, chip):
    n = full.shape[axis] // N_CHIPS
    return lax.dynamic_slice_in_dim(full, chip * n, n, axis=axis)


def _block_diag_gates(wa_f, wx_f, wa_b, wx_b):
    eye2 = jnp.eye(2, dtype=wa_f.dtype)

    def bd(w):
        w4 = w.reshape(4, 2, LRU_HD, LRU_HD)
        return (w4[:, :, :, None, :] * eye2[None, :, None, :, None]).reshape(4, LANES, LANES)
    return jnp.concatenate([bd(wa_f), bd(wx_f), bd(wa_b), bd(wx_b)], axis=2).astype(BF16)


def _gate_diag_blocks(dwh, o):
    return dwh[:, :, o * LRU_HD:(o + 1) * LRU_HD]


def kernel(x, meta_tokens, norm_mix_pre, norm_mix_post, norm_mlp_pre, norm_mlp_post, w_in, conv_w, conv_b, lru_wa_f, lru_ba_f, lru_wx_f, lru_bx_f, lru_lambda_f, lru_wa_b, lru_ba_b, lru_wx_b, lru_bx_b, lru_lambda_b, gla_wg_f, gla_bg_f, gla_wg_b, gla_bg_b, gla_head_norm, w_out, w_mlp_up, w_mlp_down, loss_target, m_meta_tokens, m_norm_mix_pre, m_norm_mix_post, m_norm_mlp_pre, m_norm_mlp_post, m_w_in, m_conv_w, m_conv_b, m_lru_wa_f, m_lru_ba_f, m_lru_wx_f, m_lru_bx_f, m_lru_lambda_f, m_lru_wa_b, m_lru_ba_b, m_lru_wx_b, m_lru_bx_b, m_lru_lambda_b, m_gla_wg_f, m_gla_bg_f, m_gla_wg_b, m_gla_bg_b, m_gla_head_norm, m_w_out, m_w_mlp_up, m_w_mlp_down, v_meta_tokens, v_norm_mix_pre, v_norm_mix_post, v_norm_mlp_pre, v_norm_mlp_post, v_w_in, v_conv_w, v_conv_b, v_lru_wa_f, v_lru_ba_f, v_lru_wx_f, v_lru_bx_f, v_lru_lambda_f, v_lru_wa_b, v_lru_ba_b, v_lru_wx_b, v_lru_bx_b, v_lru_lambda_b, v_gla_wg_f, v_gla_bg_f, v_gla_wg_b, v_gla_bg_b, v_gla_head_norm, v_w_out, v_w_mlp_up, v_w_mlp_down):
    args = dict(locals())
    w_loc = {n: args[n] for n in WEIGHTS}
    m_loc = {n: args["m_" + n] for n in WEIGHTS}
    v_loc = {n: args["v_" + n] for n in WEIGHTS}
    seq = x.shape[1]
    t_rows = FRONT + seq + BACK
    assert t_rows % 768 == 0 and seq % FRONT == 0, seq
    chip = 2 * lax.axis_index("x") + lax.axis_index("y")
    c_idx = lax.axis_index("c").astype(jnp.int32).reshape(1)

    def halves(a):
        return a.reshape((2, a.shape[0] // 2) + a.shape[1:])

    big16 = {n: w_loc[n].astype(BF16) for n in BIG}
    sm_names = list(SMALL_SHARDED)
    sm_len = _round_up(sum(w_loc[n].size for n in sm_names), 2 * SUBLANES * FLAT_W)
    sm_pack = _flat_cat([w_loc[n] for n in sm_names], sm_len).reshape(2, -1, FLAT_W)
    g_in0, g_small = _run_comm(_allgather_chips([halves(big16["w_in"][0]), sm_pack]), "gather_first")
    sm_parts = [_split_flat(g_small[q].reshape(-1), [w_loc[n].shape for n in sm_names]) for q in range(N_CHIPS)]
    full = {n: jnp.concatenate([sm_parts[q][i] for q in range(N_CHIPS)], axis=SMALL_SHARDED[n])
            for i, n in enumerate(sm_names)}
    for n in SMALL:
        if n not in SMALL_SHARDED:
            full[n] = w_loc[n]

    def full_w_in(g):
        w = jnp.transpose(g.reshape(N_CHIPS, D_MODEL, -1), (1, 0, 2)).reshape(D_MODEL, D_IN)
        return jnp.pad(w, ((0, 0), (0, D_IN_PAD - D_IN)))

    def full_rest(g_out, g_up, g_down):
        return dict(w_out=g_out.reshape(-1, D_MODEL),
                    w_up=jnp.transpose(g_up.reshape(N_CHIPS, D_MODEL, -1), (1, 0, 2)).reshape(D_MODEL, D_FF),
                    w_down=g_down.reshape(D_FF, D_MODEL))

    later_gathers = [_allgather_chips([halves(big16[n][0]) for n in BIG[1:]]),
                     _allgather_chips([halves(big16[n][l]) for l in range(1, DEPTH) for n in BIG])]

    meta_blk = jnp.concatenate([jnp.zeros((FRONT - N_META, D_MODEL), F32), full["meta_tokens"]], axis=0)
    h = jnp.concatenate([meta_blk, x[0], jnp.zeros((BACK, D_MODEL), F32)], axis=0)
    target = loss_target[0]

    layer_w = []
    for l in range(DEPTH):
        wg = jnp.zeros((LANES, 2 * QK), F32)
        wg = wg.at[0:GLA_RANK, 0:QK].set(full["gla_wg_f"][l]).at[GLA_RANK:2 * GLA_RANK, QK:].set(full["gla_wg_b"][l])
        pb = jnp.stack([full["lru_ba_f"][l], full["lru_bx_f"][l], full["lru_ba_b"][l], full["lru_bx_b"][l],
                        full["lru_lambda_f"][l], full["lru_lambda_b"][l], jnp.zeros((LRU_W,), F32),
                        jnp.zeros((LRU_W,), F32)])
        layer_w.append(dict(
            g1=full["norm_mix_pre"][l][None], g2=full["norm_mix_post"][l][None],
            g3=full["norm_mlp_pre"][l][None], g4=full["norm_mlp_post"][l][None],
            conv_w=full["conv_w"][l], conv_b=full["conv_b"][l][None],
            lru_wg=_block_diag_gates(full["lru_wa_f"][l], full["lru_wx_f"][l], full["lru_wa_b"][l], full["lru_wx_b"][l]),
            lru_pb=pb, gla_wg=wg.astype(BF16),
            gla_bg=jnp.concatenate([full["gla_bg_f"][l], full["gla_bg_b"][l]])[None],
            head_norm=full["gla_head_norm"][l][None]))
    layer_w[0]["w_in"] = full_w_in(g_in0)

    def split_z(acc):
        return (acc[:, 0:1024], acc[:, 1024:1536], acc[:, 1536:2048], acc[:, 2048:2560], acc[:, ZG_OFF:ZG_OFF + LANES])

    def relu2(acc):
        r = jnp.maximum(acc, 0.0)
        return acc, r * r

    tm_e = 768
    tm_ff = 1408 if t_rows % 1408 == 0 else 768
    tk_dw = 2816 if t_rows % 2816 == 0 else 768

    def post_mix(acc, hh, g2, g3):
        h1 = hh + _rms_fwd(acc, g2)
        return acc, h1, _rms_fwd(h1, g3)

    def post_mlp(acc, h1, g4, g1_next):
        h2 = h1 + _rms_fwd(acc, g4)
        return acc, h2, _rms_fwd(h2, g1_next)

    def post_mlp_loss(acc, h1, tgt, x_rows, g4):
        h2 = h1 + _rms_fwd(acc, g4)
        err = (h2 - tgt) * x_rows[:, 0:1]
        d_h = err * (1.0 / D_MODEL)
        dff, dg4 = _rms_bwd(acc, g4, d_h)
        loss_part = jnp.zeros((SUBLANES, LANES), F32) + jnp.sum(err * err) * (0.5 / D_MODEL)
        return d_h, dff, loss_part, dg4

    row3 = [(D_MODEL, F32), (D_MODEL, F32), (D_MODEL, BF16)]
    vec = ((1, D_MODEL), F32)
    target_p = jnp.concatenate([jnp.zeros((FRONT, D_MODEL), F32), target, jnp.zeros((BACK, D_MODEL), F32)], axis=0)
    x_rows = jnp.concatenate([jnp.zeros((FRONT, LANES), F32), jnp.ones((seq, LANES), F32), jnp.zeros((BACK, LANES), F32)])
    n_sub = 1
    saved = []
    n1 = _norm_cast(h, layer_w[0]["g1"], t_rows, "norm_mix_pre_l0")
    for l in range(DEPTH):
        lw = layer_w[l]
        tag = f"_l{l}"
        z_lru, z_qk, z_v, z_go, z_zg = _mm(
            n1, lw["w_in"], "nn", 768, D_IN_PAD, D_MODEL,
            [(1024, F32), (512, F32), (512, F32), (512, F32), (LANES, F32)], "in_proj" + tag, epilogue=split_z)
        lru_args = (z_lru, lw["conv_w"], lw["conv_b"], lw["lru_wg"], lw["lru_pb"], t_rows, seq, "lru_fwd" + tag)
        gla_args = (z_qk, z_v, z_zg, lw["gla_wg"], lw["gla_bg"], t_rows, "gla_fwd" + tag)
        if l == 0:
            (h_f, h_b), g_rest = _lru_fwd(*lru_args, comm=later_gathers[0])
            lw.update(full_rest(*g_rest))
            (o_f, o_b, s_f, s_b), g_next = _gla_fwd(*gla_args, comm=later_gathers[1])
            for l2 in range(1, DEPTH):
                g_l2 = g_next[(l2 - 1) * len(BIG):l2 * len(BIG)]
                layer_w[l2].update(full_rest(*g_l2[1:]), w_in=full_w_in(g_l2[0]))
        else:
            h_f, h_b = _lru_fwd(*lru_args)
            o_f, o_b, s_f, s_b = _gla_fwd(*gla_args)
        ymix = _mix_fwd(h_f, h_b, z_lru, o_f, o_b, z_go, lw["head_norm"], t_rows, "mix_fwd" + tag)
        mix, h1, n3 = _mm(ymix, lw["w_out"], "nn", 768, D_MODEL, D_MODEL, row3, "out_proj" + tag, epilogue=post_mix,
                          rows_in=(h,), consts=(lw["g2"], lw["g3"]), row_split=n_sub)
        u, act = _mm(n3, lw["w_up"], "nn", tm_ff, 1024, D_MODEL, [(D_FF, BF16), (D_FF, BF16)], "mlp_up" + tag,
                     epilogue=relu2)
        sv = dict(h=h, n1=n1, z_lru=z_lru, z_qk=z_qk, z_v=z_v, z_go=z_go, z_zg=z_zg, h_f=h_f, h_b=h_b,
                  o_f=o_f, o_b=o_b, s_f=s_f, s_b=s_b, ymix=ymix, mix=mix, h1=h1, n3=n3, u=u, act=act)
        if l + 1 < DEPTH:
            sv["ff"], h, n1 = _mm(act, lw["w_down"], "nn", tm_e, D_MODEL, D_FF, row3, "mlp_down" + tag,
                                  epilogue=post_mlp, rows_in=(h1,), consts=(lw["g4"], layer_w[l + 1]["g1"]),
                                  row_split=n_sub)
        else:
            dh, dff, loss_part, dg4 = _mm(act, lw["w_down"], "nn", tm_e, D_MODEL, D_FF,
                                          [(D_MODEL, F32), (D_MODEL, BF16)], "mlp_down_loss" + tag,
                                          epilogue=post_mlp_loss, rows_in=(h1, target_p, x_rows), consts=(lw["g4"],),
                                          accs=[((SUBLANES, LANES), F32), vec], row_split=n_sub)
        saved.append(sv)

    loss = lax.psum(loss_part[0, 0], ("x", "y", "c"))

    small_len = _round_up(sum(math.prod(s) for _, s in SMALL_RAW) + math.prod(META_SHAPE),
                          N_CHIPS * 2 * 2 * SUBLANES * FLAT_W)
    totals = [None] * DEPTH
    pend = None

    def with_stage(staged, fn):
        comm = staged.stage() if staged is not None else None
        if comm is None:
            return fn(None)
        res, comm_res = fn(comm)
        staged.done(comm_res)
        return res

    for l in reversed(range(DEPTH)):
        lw, sv = layer_w[l], saved[l]
        tag = f"_l{l}"
        raw = {"g4": dg4}
        gw_down = with_stage(pend, lambda comm: _mm(sv["act"], dff, "tn", 1024, D_MODEL, tk_dw, [(D_MODEL, F32)],
                                                    "dw_down" + tag, comm=comm))[0]
        du = _mm(dff, lw["w_down"], "nt", tm_ff, 1024, D_MODEL, [(D_FF, BF16)], "d_act" + tag,
                 epilogue=lambda acc, uu: (acc * 2.0 * jnp.maximum(uu.astype(F32), 0.0),), extras=(sv["u"],))[0]
        gw_up = _mm(sv["n3"], du, "tn", D_MODEL, D_FF // N_CHIPS, tk_dw, [(D_FF, F32)], "dw_up" + tag,
                    col_blocks_first=True)[0]

        def pre_mlp_bwd(acc, h1, d_h, mix, g3, g2):
            dx, dg3 = _rms_bwd(h1, g3, acc)
            d_h1 = d_h + dx
            d_mix, dg2 = _rms_bwd(mix, g2, d_h1)
            return d_h1, d_mix, dg3, dg2

        dh1, dmix, raw["g3"], raw["g2"] = _mm(
            du, lw["w_up"], "nt", tm_e, D_MODEL, D_FF, [(D_MODEL, F32), (D_MODEL, BF16)], "d_n3" + tag,
            epilogue=pre_mlp_bwd, rows_in=(sv["h1"], dh, sv["mix"]), consts=(lw["g3"], lw["g2"]), accs=[vec, vec],
            row_split=n_sub)
        gw_out = _mm(sv["ymix"], dmix, "tn", D_MODEL, D_MODEL, tk_dw, [(D_MODEL, F32)], "dw_out" + tag)[0]
        mats = [gw_out.reshape(N_CHIPS, 2, -1, D_MODEL), gw_up.reshape(N_CHIPS, 2, D_MODEL // 2, D_FF // N_CHIPS),
                gw_down.reshape(N_CHIPS, 2, -1, D_MODEL)]
        early = _Staged(_reduce_stages(mats, c_idx, tag + "e"), tag + "e") if l == 0 else None
        half4 = [(LRU_W, F32), (LRU_W, F32), (GLA_W, F32), (GLA_W, F32)]
        dh_lru, dgate, do, dgo, raw["head_norm"] = with_stage(early, lambda comm: _mm(
            dmix, lw["w_out"], "nt", 768, D_MODEL, D_MODEL, half4, "d_ymix" + tag, epilogue=_mix_bwd_tile,
            rows_in=(sv["h_f"], sv["h_b"], sv["z_lru"], sv["o_f"], sv["o_b"], sv["z_go"]), consts=(lw["head_norm"],),
            accs=[((1, GLA_W), F32)], comm=comm))
        gla_grads = with_stage(pend, lambda comm: _gla_bwd(sv["z_qk"], sv["z_v"], sv["z_zg"], lw["gla_wg"],
                                                           lw["gla_bg"], do, sv["s_f"], sv["s_b"], t_rows,
                                                           "gla_bwd" + tag, comm=comm))
        dx_br, raw["conv_w"], raw["conv_b"], raw["lru_wg"], raw["lru_pb"] = with_stage(early, lambda comm: _lru_bwd(
            sv["z_lru"], lw["conv_w"], lw["conv_b"], lw["lru_wg"], lw["lru_pb"], sv["h_f"], sv["h_b"], dh_lru,
            t_rows, seq, "lru_bwd" + tag, comm=comm))
        dz, dwg_gla, raw["gla_bg"] = _dz_assemble(dx_br, dgate, gla_grads, dgo, sv["z_zg"], lw["gla_wg"], t_rows,
                                                  "dz_assemble" + tag)
        raw["gla_wg"] = dwg_gla[0:2 * GLA_RANK]
        gw_in = with_stage(pend, lambda comm: _mm(sv["n1"], dz, "tn", D_MODEL, 896, tk_dw, [(D_IN_PAD, F32)],
                                                  "dw_in" + tag, comm=comm))[0]
        if l > 0:
            def pre_mix_bwd(acc, hh, d_h1, ff_prev, g1, g4_prev):
                dx, dg1 = _rms_bwd(hh, g1, acc)
                d_h = d_h1 + dx
                d_ff, dg4_prev = _rms_bwd(ff_prev, g4_prev, d_h)
                return d_h, d_ff, dg1, dg4_prev

            dh, dff, raw["g1"], dg4 = _mm(
                dz, lw["w_in"], "nt", tm_e, D_MODEL, D_IN_PAD, [(D_MODEL, F32), (D_MODEL, BF16)], "d_n1" + tag,
                epilogue=pre_mix_bwd, rows_in=(sv["h"], dh1, saved[l - 1]["ff"]),
                consts=(lw["g1"], layer_w[l - 1]["g4"]), accs=[vec, vec], row_split=n_sub)
        else:
            def pre_mix_bwd0(acc, hh, d_h1, g1):
                dx, dg1 = _rms_bwd(hh, g1, acc)
                return d_h1 + dx, dg1

            dh, raw["g1"] = with_stage(early, lambda comm: _mm(
                dz, lw["w_in"], "nt", tm_e, D_MODEL, D_IN_PAD, [(D_MODEL, F32)], "d_n1" + tag, epilogue=pre_mix_bwd0,
                rows_in=(sv["h"], dh1), consts=(lw["g1"],), accs=[vec], row_split=n_sub, comm=comm))

        cols = D_IN // N_CHIPS
        in_sh = jnp.stack([gw_in[:, q * cols:(q + 1) * cols] for q in range(N_CHIPS)])
        meta_g = dh[FRONT - N_META:FRONT] if l == 0 else jnp.zeros(META_SHAPE, F32)
        small = _flat_cat([raw[n] for n, _ in SMALL_RAW] + [meta_g], small_len)
        rest = [in_sh.reshape(N_CHIPS, 2, D_MODEL // 2, cols), small.reshape(N_CHIPS, 2, -1, FLAT_W)]
        if pend is not None:
            totals[l + 1] = pend.drain()
        if early is None:
            pend = _Staged(_reduce_stages([rest[0]] + mats + [rest[1]], c_idx, tag), tag)
        else:
            t_in, t_small = _Staged(_reduce_stages(rest, c_idx, tag), tag).drain()
            totals[l] = [t_in] + list(early.drain()) + [t_small]

    grad_x = dh[FRONT:FRONT + seq][None]

    sm_all = _run_comm(_allgather_chips([totals[l][4] for l in range(DEPTH)]), "gather_small_grads")
    g_tot = {}
    per_layer = []
    for l in range(DEPTH):
        pieces = _split_flat(sm_all[l].reshape(-1), [s for _, s in SMALL_RAW] + [META_SHAPE])
        per_layer.append(dict(zip([n for n, _ in SMALL_RAW] + ["meta"], pieces)))
    stack = lambda f: jnp.stack([f(per_layer[l]) for l in range(DEPTH)])
    g_tot["meta_tokens"] = _my_shard(per_layer[0]["meta"], 1, chip)
    for n, key in (("norm_mix_pre", "g1"), ("norm_mix_post", "g2"), ("norm_mlp_pre", "g3"), ("norm_mlp_post", "g4"),
                   ("conv_b", "conv_b"), ("gla_head_norm", "head_norm")):
        g_tot[n] = stack(lambda d, key=key: d[key][0])
    g_tot["conv_w"] = _my_shard(stack(lambda d: d["conv_w"]), 2, chip)
    for o, n in enumerate(("lru_wa_f", "lru_wx_f", "lru_wa_b", "lru_wx_b")):
        g_tot[n] = stack(lambda d, o=o: _gate_diag_blocks(d["lru_wg"], o))
    for row, n in enumerate(("lru_ba_f", "lru_bx_f", "lru_ba_b", "lru_bx_b", "lru_lambda_f", "lru_lambda_b")):
        g_tot[n] = stack(lambda d, row=row: d["lru_pb"][row])
    g_tot["gla_wg_f"] = _my_shard(stack(lambda d: d["gla_wg"][0:GLA_RANK, 0:QK]), 2, chip)
    g_tot["gla_wg_b"] = _my_shard(stack(lambda d: d["gla_wg"][GLA_RANK:, QK:]), 2, chip)
    g_tot["gla_bg_f"] = stack(lambda d: d["gla_bg"][0, 0:QK])
    g_tot["gla_bg_b"] = stack(lambda d: d["gla_bg"][0, QK:])

    delta, new_m, new_v = {}, {}, {}
    for a, n in enumerate(BIG):
        shp = w_loc[n].shape
        flat3 = lambda t, shp=shp: t.reshape(DEPTH, -1, shp[-1])
        g_l = [totals[l][a].reshape(-1, shp[-1]) for l in range(DEPTH)]
        outs = _adamw_layers(flat3(w_loc[n]), flat3(m_loc[n]), flat3(v_loc[n]), g_l, "adamw_" + n)
        g_tot[n], delta[n], new_m[n], new_v[n] = [o.reshape(shp) for o in outs]
    outs = _adamw_small(*[[d[n] for n in SMALL] for d in (w_loc, g_tot, m_loc, v_loc)])
    for d, o in zip((delta, new_m, new_v), outs):
        d.update(zip(SMALL, o))
    return (loss, grad_x, *[g_tot[n] for n in WEIGHTS], *[delta[n] for n in WEIGHTS],
            *[new_m[n] for n in WEIGHTS], *[new_v[n] for n in WEIGHTS])
```

```python
import math

import jax
import jax.numpy as jnp
from jax import lax
from jax.experimental import pallas as pl
from jax.experimental.pallas import tpu as pltpu

F32 = jnp.float32
BF16 = jnp.bfloat16

D_MODEL = 1024
DEPTH = 2
N_META = 16
LRU_W = 512
LRU_HEADS = 8
LRU_HD = 64
LRU_C = 8.0
GLA_W = 512
GLA_H = 4
GLA_DK = 64
GLA_DV = 128
GLA_RANK = 16
GLA_GATE_NORM = 16.0
D_FF = 4096
D_IN = 2592
EPS = 1e-6
ADAM_LR, ADAM_B1, ADAM_B2, ADAM_EPS, ADAM_WD, ADAM_STEP = 0.001, 0.9, 0.999, 1e-08, 0.01, 10

LANES = 128
SUBLANES = 8
FRONT = 128
BACK = 128
D_IN_PAD = 2688
ZG_OFF = 2560
CHUNK = 64
EXP_CLAMP = 80.0
VMEM_LIMIT = 56 * 1024 * 1024
MESH_ID = pl.DeviceIdType.MESH

NN = (((1,), (0,)), ((), ()))
NT = (((1,), (1,)), ((), ()))
TN = (((0,), (0,)), ((), ()))


def _dot(a, b, dims=NN):
    return lax.dot_general(a, b, dims, preferred_element_type=F32)


def _cparams(sem):
    return pltpu.CompilerParams(dimension_semantics=sem, vmem_limit_bytes=VMEM_LIMIT)


def _sigmoid(x):
    return 1.0 / (1.0 + jnp.exp(-x))


def _gelu(x):
    c = math.sqrt(2.0 / math.pi)
    return 0.5 * x * (1.0 + jnp.tanh(c * (x + 0.044715 * x * x * x)))


def _gelu_grad(x):
    c = math.sqrt(2.0 / math.pi)
    t = jnp.tanh(c * (x + 0.044715 * x * x * x))
    return 0.5 * (1.0 + t) + 0.5 * x * (1.0 - t * t) * c * (1.0 + 3.0 * 0.044715 * x * x)


def _rms_fwd(x, g):
    rstd = lax.rsqrt(jnp.mean(x * x, axis=1, keepdims=True) + EPS)
    return (x * rstd) * g


def _rms_bwd(x, g, dy):
    rstd = lax.rsqrt(jnp.mean(x * x, axis=1, keepdims=True) + EPS)
    xh = x * rstd
    dxh = dy * g
    dx = rstd * (dxh - xh * jnp.mean(dxh * xh, axis=1, keepdims=True))
    dg = jnp.sum(dy * xh, axis=0, keepdims=True)
    return dx, dg


def _mm(a, b, mode, tm, tn, tk, outs, name, epilogue=None, extras=(), col_blocks_first=False,
        rows_in=(), consts=(), accs=(), comm=None):
    if mode == "nn":
        (m, k), n = a.shape, b.shape[1]
        a_spec = pl.BlockSpec((tm, tk), lambda i, j, kk: (i, kk))
        b_spec = pl.BlockSpec((tk, tn), lambda i, j, kk: (kk, j))
        dims = NN
    elif mode == "nt":
        (m, k), n = a.shape, b.shape[0]
        a_spec = pl.BlockSpec((tm, tk), lambda i, j, kk: (i, kk))
        b_spec = pl.BlockSpec((tn, tk), lambda i, j, kk: (j, kk))
        dims = NT
    else:
        (k, m), n = a.shape, b.shape[1]
        a_spec = pl.BlockSpec((tk, tm), lambda i, j, kk: (kk, i))
        b_spec = pl.BlockSpec((tk, tn), lambda i, j, kk: (kk, j))
        dims = TN
    assert m % tm == 0 and n % tn == 0 and k % tk == 0, (name, m, n, k)
    nk = k // tk
    if n == tn and nk == 1:
        b_spec = pl.BlockSpec(b_spec.block_shape, b_spec.index_map, pipeline_mode=pl.Buffered(1))
    ne = len(extras) + len(rows_in) + len(consts)
    e_specs = [pl.BlockSpec((tm, tn), lambda i, j, kk: (i, j)) for _ in extras]
    e_specs += [pl.BlockSpec((tm, r.shape[1]), lambda i, j, kk: (i, 0)) for r in rows_in]
    e_specs += [pl.BlockSpec(c.shape, lambda i, j, kk: (0, 0)) for c in consts]
    n_out = len(outs)
    o_specs, o_shapes = [], []
    for width, dtype in outs:
        if col_blocks_first:
            assert width == n, name
            o_specs.append(pl.BlockSpec((None, tm, tn), lambda i, j, kk: (j, i, 0)))
            o_shapes.append(jax.ShapeDtypeStruct((n // tn, m, tn), dtype))
            continue
        if width == n:
            o_specs.append(pl.BlockSpec((tm, tn), lambda i, j, kk: (i, j)))
        else:
            assert n == tn, name
            o_specs.append(pl.BlockSpec((tm, width), lambda i, j, kk: (i, 0)))
        o_shapes.append(jax.ShapeDtypeStruct((m, width), dtype))
    o_specs += [pl.BlockSpec(s, lambda i, j, kk: (0, 0)) for s, _ in accs]
    o_shapes += [jax.ShapeDtypeStruct(s, d) for s, d in accs]

    def finish(acc, extra_refs, out_refs):
        res = epilogue(acc, *[e[...] for e in extra_refs]) if epilogue else (acc,)
        for o, r in zip(out_refs[:n_out], res[:n_out]):
            o[...] = r.astype(o.dtype)
        first = jnp.logical_and(pl.program_id(0) == 0, pl.program_id(1) == 0)
        for o, r in zip(out_refs[n_out:], res[n_out:]):
            @pl.when(first)
            def _(o=o):
                o[...] = jnp.zeros_like(o)

            o[...] += r

    if nk == 1:
        def body(a_ref, b_ref, *rest):
            finish(_dot(a_ref[...], b_ref[...], dims), rest[:ne], rest[ne:])
        scratch = []
    else:
        def body(a_ref, b_ref, *rest):
            acc_ref = rest[-1]
            kk = pl.program_id(2)

            @pl.when(kk == 0)
            def _():
                acc_ref[...] = jnp.zeros_like(acc_ref)

            acc_ref[...] += _dot(a_ref[...], b_ref[...], dims)

            @pl.when(kk == nk - 1)
            def _():
                finish(acc_ref[...], rest[:ne], rest[ne:-1])
        scratch = [pltpu.VMEM((tm, tn), F32)]

    sem = ("arbitrary", "arbitrary", "arbitrary") if accs else ("parallel", "parallel", "arbitrary")
    res, comm_res = _pcall(body, name, (m // tm, n // tn, nk), [a_spec, b_spec] + e_specs, o_specs, o_shapes, scratch,
                           sem, (a, b, *extras, *rows_in, *consts), comm)
    return res if comm is None else (res, comm_res)


def _rows(body, n_rows, tm, ins, consts, outs, accs, name):
    assert n_rows % tm == 0, (name, n_rows, tm)
    in_specs = [pl.BlockSpec((tm, w), (lambda i, cb=cb: (i, cb))) for _, w, cb in ins]
    in_specs += [pl.BlockSpec(c.shape, lambda i: (0, 0)) for c in consts]
    out_specs = [pl.BlockSpec((tm, w), lambda i: (i, 0)) for w, _ in outs]
    out_specs += [pl.BlockSpec(s, lambda i: (0, 0)) for s, _ in accs]
    out_shape = [jax.ShapeDtypeStruct((n_rows, w), d) for w, d in outs]
    out_shape += [jax.ShapeDtypeStruct(s, d) for s, d in accs]
    ni, nc, no = len(ins), len(consts), len(outs)

    def kern(*refs):
        body(pl.program_id(0), refs[:ni], refs[ni:ni + nc], refs[ni + nc:ni + nc + no], refs[ni + nc + no:])

    res = pl.pallas_call(
        kern, name=name, grid=(n_rows // tm,), in_specs=in_specs, out_specs=out_specs, out_shape=out_shape,
        compiler_params=_cparams(("arbitrary",)),
    )(*[a for a, _, _ in ins], *consts)
    return res


def _acc_add(i, ref, val):
    @pl.when(i == 0)
    def _():
        ref[...] = jnp.zeros_like(ref)

    ref[...] += val


def _norm_cast(h, g, t_rows, name):
    def body(i, ins, consts, outs, accs):
        outs[0][...] = _rms_fwd(ins[0][...], consts[0][...]).astype(BF16)
    return _rows(body, t_rows, 384, [(h, D_MODEL, 0)], [g], [(D_MODEL, BF16)], [], name)[0]


def _head_norm_parts(o):
    ns, rs = [], []
    for hd in range(GLA_H):
        oh = o[:, hd * GLA_DV:(hd + 1) * GLA_DV]
        r = lax.rsqrt(jnp.mean(oh * oh, axis=1, keepdims=True) + EPS)
        ns.append(oh * r)
        rs.append(r)
    return ns, rs


def _mix_fwd(h_f, h_b, z_lru, o_f, o_b, z_go, head_norm, t_rows, name):
    def body(i, ins, consts, outs, accs):
        hf, hb, gate, of, ob, go = [r[...] for r in ins]
        hn = consts[0][...]
        outs[0][:, 0:LRU_W] = ((hf + hb) * _gelu(gate)).astype(BF16)
        ns, _ = _head_norm_parts(of + ob)
        silu = go * _sigmoid(go)
        for hd in range(GLA_H):
            sl = slice(hd * GLA_DV, (hd + 1) * GLA_DV)
            outs[0][:, LRU_W + hd * GLA_DV:LRU_W + (hd + 1) * GLA_DV] = (ns[hd] * hn[:, sl] * silu[:, sl]).astype(BF16)
    ins = [(h_f, LRU_W, 0), (h_b, LRU_W, 0), (z_lru, LRU_W, 1), (o_f, GLA_W, 0), (o_b, GLA_W, 0), (z_go, GLA_W, 0)]
    return _rows(body, t_rows, 384, ins, [head_norm], [(D_MODEL, BF16)], [], name)[0]


def _mix_bwd_tile(dymix, hf, hb, z_lru, of, ob, go, hn):
    dyl, dyg, gate = dymix[:, 0:LRU_W], dymix[:, LRU_W:], z_lru[:, LRU_W:]
    dh_lru = dyl * _gelu(gate)
    dgate = dyl * (hf + hb) * _gelu_grad(gate)
    ns, rs = _head_norm_parts(of + ob)
    sg = _sigmoid(go)
    silu = go * sg
    dsilu = sg * (1.0 + go * (1.0 - sg))
    d_o, d_go, dhn = [], [], []
    for hd in range(GLA_H):
        sl = slice(hd * GLA_DV, (hd + 1) * GLA_DV)
        dy = dyg[:, sl]
        e = dy * hn[:, sl] * silu[:, sl]
        d_o.append(rs[hd] * (e - ns[hd] * jnp.mean(e * ns[hd], axis=1, keepdims=True)))
        d_go.append(dy * ns[hd] * hn[:, sl] * dsilu[:, sl])
        dhn.append(jnp.sum(dy * ns[hd] * silu[:, sl], axis=0, keepdims=True))
    return dh_lru, dgate, jnp.concatenate(d_o, axis=1), jnp.concatenate(d_go, axis=1), jnp.concatenate(dhn, axis=1)


def _dz_assemble(dx_br, dgate, gla_grads, dgo, z_zg, wg, t_rows, name):
    dq_f, dk_f, dv_f, dpre_f, dq_b, dk_b, dv_b, dpre_b = gla_grads
    qk = GLA_H * GLA_DK

    def body(i, ins, consts, outs, accs):
        (dxb, dgt, dqf, dqb, dkf, dkb, dvf, dvb, dg_o, dpf, dpb, zg) = [r[...] for r in ins]
        w = consts[0][...]
        o = outs[0]
        o[:, 0:LRU_W] = dxb.astype(BF16)
        o[:, LRU_W:2 * LRU_W] = dgt.astype(BF16)
        o[:, 1024:1024 + qk] = ((dqf + dqb) * (GLA_DK ** -0.5)).astype(BF16)
        o[:, 1280:1280 + qk] = (dkf + dkb).astype(BF16)
        o[:, 1536:1536 + GLA_W] = (dvf + dvb).astype(BF16)
        o[:, 2048:2048 + GLA_W] = dg_o.astype(BF16)
        dpre = jnp.concatenate([dpf, dpb], axis=1)
        dpre16 = dpre.astype(BF16)
        o[:, ZG_OFF:ZG_OFF + LANES] = _dot(dpre16, w, NT).astype(BF16)
        _acc_add(i, accs[0], _dot(zg.astype(BF16), dpre16, TN))
        _acc_add(i, accs[1], jnp.sum(dpre, axis=0, keepdims=True))

    ins = [(dx_br, LRU_W, 0), (dgate, LRU_W, 0), (dq_f, qk, 0), (dq_b, qk, 0), (dk_f, qk, 0), (dk_b, qk, 0),
           (dv_f, GLA_W, 0), (dv_b, GLA_W, 0), (dgo, GLA_W, 0), (dpre_f, qk, 0), (dpre_b, qk, 0), (z_zg, LANES, 0)]
    return _rows(body, t_rows, 384, ins, [wg], [(D_IN_PAD, BF16)],
                 [((LANES, 2 * qk), F32), ((1, 2 * qk), F32)], name)


LRU_RB = 384
HALO = SUBLANES


def _scan8(a, u, rev):
    rows = lax.broadcasted_iota(jnp.int32, a.shape, 0)
    for d in (1, 2, 4):
        if rev:
            a_s, u_s, ok = pltpu.roll(a, SUBLANES - d, 0), pltpu.roll(u, SUBLANES - d, 0), rows < SUBLANES - d
        else:
            a_s, u_s, ok = pltpu.roll(a, d, 0), pltpu.roll(u, d, 0), rows >= d
        u = jnp.where(ok, a * u_s + u, u)
        a = jnp.where(ok, a * a_s, a)
    return a, u


def _edge_row(x, rev):
    r = x[0:1, :] if rev else x[SUBLANES - 1:SUBLANES, :]
    return jnp.broadcast_to(r, x.shape)


def _scan_block(a, u, carry, rev):
    nsub = a.shape[0] // SUBLANES
    loc = [_scan8(a[sb * SUBLANES:(sb + 1) * SUBLANES], u[sb * SUBLANES:(sb + 1) * SUBLANES], rev) for sb in range(nsub)]
    edges = [(_edge_row(a_c, rev), _edge_row(h_l, rev)) for a_c, h_l in loc]
    carries = [None] * nsub
    for sb in (range(nsub - 1, -1, -1) if rev else range(nsub)):
        carries[sb] = carry
        carry = edges[sb][0] * carry + edges[sb][1]
    h = jnp.concatenate([h_l + a_c * cin for (a_c, h_l), cin in zip(loc, carries)], axis=0)
    return h, carry


def _lru_gates(xc, pre_a, pre_x, ba, bx, sp8):
    r = _sigmoid(pre_a + ba)
    i = _sigmoid(pre_x + bx)
    log_a = -(r * sp8)
    a = jnp.exp(log_a)
    y = 2.0 * log_a
    series = -y * (1.0 + y * (0.5 + y * (1.0 / 6.0 + y * (1.0 / 24.0 + y * (1.0 / 120.0)))))
    om = jnp.where(y > -0.25, series, 1.0 - a * a)
    s = jnp.sqrt(om)
    return r, i, a, s


def _softplus(x):
    return jnp.maximum(x, 0.0) + jnp.log(1.0 + jnp.exp(-jnp.abs(x)))


def _conv_taps(xpad_ref, r0, nrows, shifts=(-2, -1, 0, 1)):
    ext = xpad_ref[pl.ds(r0, nrows + 2 * HALO), :]
    taps = []
    for s in shifts:
        sh = ext if s == 0 else pltpu.roll(ext, (-s) % (nrows + 2 * HALO), 0)
        taps.append(sh[HALO:HALO + nrows])
    return taps


def _row_mask(r0, nrows, seq):
    rows = r0 + lax.broadcasted_iota(jnp.int32, (nrows, LANES), 0)
    return jnp.logical_and(rows >= FRONT - N_META, rows < FRONT + seq).astype(F32)


def _lru_load_conv(z_hbm, xpad, xc, sem, cw_ref, cb_ref, t_rows, seq):
    j = pl.program_id(0)
    zeros = jnp.zeros((HALO, LANES), F32)
    xpad[0:HALO, :] = zeros
    xpad[t_rows + HALO:t_rows + 2 * HALO, :] = zeros
    cp = pltpu.make_async_copy(z_hbm.at[:, pl.ds(pl.multiple_of(j * LANES, LANES), LANES)],
                               xpad.at[pl.ds(HALO, t_rows)], sem)
    cp.start()
    cp.wait()
    cw = cw_ref[...]
    cb = cb_ref[...]

    def conv_blk(r, carry):
        r0 = pl.multiple_of(r * LRU_RB, LRU_RB)
        taps = _conv_taps(xpad, r0, LRU_RB)
        acc = cb + cw[0:1, :] * taps[0]
        for k in range(1, 4):
            acc = acc + cw[k:k + 1, :] * taps[k]
        xc[pl.ds(r0, LRU_RB), :] = acc * _row_mask(r0, LRU_RB, seq)
        return carry

    lax.fori_loop(0, t_rows // LRU_RB, conv_blk, 0)


def _lru_specs(t_rows):
    return [pl.BlockSpec(memory_space=pl.ANY),
            pl.BlockSpec((4, LANES), lambda j: (0, j)),
            pl.BlockSpec((1, LANES), lambda j: (0, j)),
            pl.BlockSpec((1, LANES, 4 * LANES), lambda j: (j, 0, 0)),
            pl.BlockSpec((SUBLANES, LANES), lambda j: (0, j))]


def _lru_fwd(z_lru, conv_w, conv_b, wg, pb, t_rows, seq, name, comm=None):
    nblk = t_rows // LRU_RB

    def kern(z_hbm, cw_ref, cb_ref, wg_ref, pb_ref, hf_ref, hb_ref, xpad, xc, sem):
        _lru_load_conv(z_hbm, xpad, xc, sem, cw_ref, cb_ref, t_rows, seq)
        w = wg_ref[0]
        pb_v = pb_ref[...]
        dirs = []
        for rev in (False, True):
            o = 2 if rev else 0
            dirs.append(dict(rev=rev, ba=pb_v[o:o + 1, :], bx=pb_v[o + 1:o + 2, :],
                             sp8=LRU_C * _softplus(-pb_v[5:6, :] if rev else -pb_v[4:5, :]),
                             w=w[:, o * LANES:(o + 2) * LANES], out=hb_ref if rev else hf_ref))

        def blk(it, carries):
            new = []
            for d, carry in zip(dirs, carries):
                r = (nblk - 1 - it) if d["rev"] else it
                r0 = pl.multiple_of(r * LRU_RB, LRU_RB)
                xcb = xc[pl.ds(r0, LRU_RB), :]
                pre = _dot(xcb.astype(BF16), d["w"])
                _, gi, a, s = _lru_gates(xcb, pre[:, 0:LANES], pre[:, LANES:2 * LANES], d["ba"], d["bx"], d["sp8"])
                h, carry = _scan_block(a, s * (gi * xcb), carry, d["rev"])
                d["out"][pl.ds(r0, LRU_RB), :] = h
                new.append(carry)
            return tuple(new)

        z8 = jnp.zeros((SUBLANES, LANES), F32)
        lax.fori_loop(0, nblk, blk, (z8, z8), unroll=2)

    res, comm_res = _pcall(
        kern, name, (LRU_W // LANES,), _lru_specs(t_rows), [pl.BlockSpec((t_rows, LANES), lambda j: (0, j))] * 2,
        [jax.ShapeDtypeStruct((t_rows, LRU_W), F32)] * 2,
        [pltpu.VMEM((t_rows + 2 * HALO, LANES), F32), pltpu.VMEM((t_rows, LANES), F32), pltpu.SemaphoreType.DMA],
        ("arbitrary",), (z_lru, conv_w, conv_b, wg, pb), comm)
    return res if comm is None else (res, comm_res)


def _lru_bwd(z_lru, conv_w, conv_b, wg, pb, h_f, h_b, dh, t_rows, seq, name, comm=None):
    nblk = t_rows // LRU_RB

    def kern(z_hbm, cw_ref, cb_ref, wg_ref, pb_ref, hf_ref, hb_ref, dh_ref,
             dx_ref, dcw_ref, dcb_ref, dwh_ref, dpb_ref, xpad, xc, dxc, dwg, sem):
        _lru_load_conv(z_hbm, xpad, xc, sem, cw_ref, cb_ref, t_rows, seq)
        w = wg_ref[0]
        pb_v = pb_ref[...]
        zeros = jnp.zeros((HALO, LANES), F32)
        dxc[0:HALO, :] = zeros
        dxc[t_rows + HALO:t_rows + 2 * HALO, :] = zeros
        dwg[...] = jnp.zeros_like(dwg)

        def zero_blk(r, carry):
            dxc[pl.ds(pl.multiple_of(r * LRU_RB, LRU_RB) + HALO, LRU_RB), :] = jnp.zeros((LRU_RB, LANES), F32)
            return carry

        lax.fori_loop(0, nblk, zero_blk, 0)
        dirs = []
        for rev in (False, True):
            o = 2 if rev else 0
            lam = pb_v[5:6, :] if rev else pb_v[4:5, :]
            dirs.append(dict(rev=rev, o=o, ba=pb_v[o:o + 1, :], bx=pb_v[o + 1:o + 2, :], sp8=LRU_C * _softplus(-lam),
                             dlam_scale=LRU_C * _sigmoid(-lam), h_ref=hb_ref if rev else hf_ref,
                             w=w[:, o * LANES:(o + 2) * LANES]))
        rows_b = lax.broadcasted_iota(jnp.int32, (LRU_RB, LANES), 0)

        def blk(it, carries):
            new = []
            for d, (nu_c, acc_ba, acc_bx, acc_lam) in zip(dirs, carries):
                rev, o, h_ref = d["rev"], d["o"], d["h_ref"]
                adj_rev = not rev
                r = (nblk - 1 - it) if adj_rev else it
                r0 = pl.multiple_of(r * LRU_RB, LRU_RB)
                xcb = xc[pl.ds(r0, LRU_RB), :]
                xc16 = xcb.astype(BF16)
                pre = _dot(xc16, d["w"])
                gr, gi, a, s = _lru_gates(xcb, pre[:, 0:LANES], pre[:, LANES:2 * LANES], d["ba"], d["bx"], d["sp8"])
                dhb = dh_ref[pl.ds(r0, LRU_RB), :]
                hb = h_ref[pl.ds(r0, LRU_RB), :]
                if rev:
                    nxt0 = pl.multiple_of(jnp.minimum(r0 + LRU_RB, t_rows - SUBLANES), SUBLANES)
                    edge = h_ref[pl.ds(nxt0, SUBLANES), :][0:1, :] * (r < nblk - 1).astype(F32)
                    h_prev = jnp.concatenate([hb[1:], edge], axis=0)
                else:
                    prv0 = pl.multiple_of(jnp.maximum(r0 - SUBLANES, 0), SUBLANES)
                    edge = h_ref[pl.ds(prv0, SUBLANES), :][SUBLANES - 1:SUBLANES, :] * (r > 0).astype(F32)
                    h_prev = jnp.concatenate([edge, hb[:-1]], axis=0)
                nu, nu_out = _scan_block(a, a * dhb, nu_c, adj_rev)
                cin = jnp.broadcast_to(nu_c[0:1, :], (LRU_RB, LANES))
                if adj_rev:
                    nu_next = jnp.where(rows_b == LRU_RB - 1, cin, pltpu.roll(nu, LRU_RB - 1, 0))
                else:
                    nu_next = jnp.where(rows_b == 0, cin, pltpu.roll(nu, 1, 0))
                du = dhb + nu_next
                da = du * h_prev
                ix = gi * xcb
                dlog = da * a - du * ix * (a * a) / s
                d_i = du * s * xcb
                dr = -dlog * d["sp8"]
                dpa = dr * gr * (1.0 - gr)
                dpx = d_i * gi * (1.0 - gi)
                dp16 = jnp.concatenate([dpa, dpx], axis=1).astype(BF16)
                dxc[pl.ds(r0 + HALO, LRU_RB), :] += du * s * gi + _dot(dp16, d["w"], NT)
                dwg[:, o * LANES:(o + 2) * LANES] += _dot(xc16, dp16, TN)
                new.append((nu_out, acc_ba + jnp.sum(dpa, axis=0, keepdims=True),
                            acc_bx + jnp.sum(dpx, axis=0, keepdims=True),
                            acc_lam + jnp.sum(dlog * gr, axis=0, keepdims=True)))
            return tuple(new)

        z1 = jnp.zeros((1, LANES), F32)
        init = (jnp.zeros((SUBLANES, LANES), F32), z1, z1, z1)
        (_, ba_f, bx_f, lam_f), (_, ba_b, bx_b, lam_b) = lax.fori_loop(0, nblk, blk, (init, init), unroll=2)
        dpb_ref[...] = jnp.concatenate([ba_f, bx_f, ba_b, bx_b, lam_f * dirs[0]["dlam_scale"],
                                        lam_b * dirs[1]["dlam_scale"], z1, z1], axis=0)
        for hh in range(2):
            for o in range(4):
                c0 = o * LANES + hh * LRU_HD
                dwh_ref[hh, :, o * LRU_HD:(o + 1) * LRU_HD] = dwg[hh * LRU_HD:(hh + 1) * LRU_HD, c0:c0 + LRU_HD]

        def mask_blk(r, carry):
            r0 = pl.multiple_of(r * LRU_RB, LRU_RB)
            dxc[pl.ds(r0 + HALO, LRU_RB), :] = dxc[pl.ds(r0 + HALO, LRU_RB), :] * _row_mask(r0, LRU_RB, seq)
            return carry

        lax.fori_loop(0, nblk, mask_blk, 0)

        cw = cw_ref[...]

        def conv_bwd(r, carry):
            r0 = pl.multiple_of(r * LRU_RB, LRU_RB)
            gt = _conv_taps(dxc, r0, LRU_RB, (2, 1, 0, -1))
            xt = _conv_taps(xpad, r0, LRU_RB)
            dx_ref[pl.ds(r0, LRU_RB), :] = (cw[0:1, :] * gt[0] + cw[1:2, :] * gt[1] + cw[2:3, :] * gt[2]
                                           + cw[3:4, :] * gt[3])
            g = gt[2]
            new = [carry[k] + jnp.sum(g * xt[k], axis=0, keepdims=True) for k in range(4)]
            new.append(carry[4] + jnp.sum(g, axis=0, keepdims=True))
            return tuple(new)

        z1 = jnp.zeros((1, LANES), F32)
        sums = lax.fori_loop(0, nblk, conv_bwd, (z1, z1, z1, z1, z1))
        dcw_ref[...] = jnp.concatenate(sums[:4], axis=0)
        dcb_ref[...] = sums[4]

    col = lambda j: (0, j)
    res, comm_res = _pcall(
        kern, name, (LRU_W // LANES,), _lru_specs(t_rows) + [pl.BlockSpec((t_rows, LANES), col)] * 3,
        [pl.BlockSpec((t_rows, LANES), col), pl.BlockSpec((4, LANES), col), pl.BlockSpec((1, LANES), col),
         pl.BlockSpec((2, LRU_HD, 4 * LRU_HD), lambda j: (j, 0, 0)), pl.BlockSpec((SUBLANES, LANES), col)],
        [jax.ShapeDtypeStruct((t_rows, LRU_W), F32), jax.ShapeDtypeStruct((4, LRU_W), F32),
         jax.ShapeDtypeStruct((1, LRU_W), F32), jax.ShapeDtypeStruct((LRU_HEADS, LRU_HD, 4 * LRU_HD), F32),
         jax.ShapeDtypeStruct((SUBLANES, LRU_W), F32)],
        [pltpu.VMEM((t_rows + 2 * HALO, LANES), F32), pltpu.VMEM((t_rows, LANES), F32),
         pltpu.VMEM((t_rows + 2 * HALO, LANES), F32), pltpu.VMEM((LANES, 4 * LANES), F32), pltpu.SemaphoreType.DMA],
        ("arbitrary",), (z_lru, conv_w, conv_b, wg, pb, h_f, h_b, dh), comm)
    return res if comm is None else (res, comm_res)


QK = GLA_H * GLA_DK
GLA_G = 6


def _cumsum_rows(x, rev):
    n = x.shape[0]
    rows = lax.broadcasted_iota(jnp.int32, x.shape, 0)
    d = 1
    while d < n:
        if rev:
            x = x + jnp.where(rows < n - d, pltpu.roll(x, n - d, 0), 0.0)
        else:
            x = x + jnp.where(rows >= d, pltpu.roll(x, d, 0), 0.0)
        d *= 2
    return x


def _gla_chunk_terms(q, k, zg, wg, bg, rev):
    pre = (_dot(zg.astype(BF16), wg) + bg)[:, (QK if rev else 0):(2 * QK if rev else QK)]
    g = (jnp.minimum(pre, 0.0) - jnp.log(1.0 + jnp.exp(-jnp.abs(pre)))) * (1.0 / GLA_GATE_NORM)
    b = _cumsum_rows(g, rev)
    b_last = b[0:1, :] if rev else b[CHUNK - 1:CHUNK, :]
    b_mid = b[CHUNK // 2:CHUNK // 2 + 1, :]
    qs = q * (GLA_DK ** -0.5)
    terms = dict(
        pre=pre, qs=qs, k=k,
        eb=jnp.exp(b), ek=jnp.exp(b_last - b), e_last=jnp.exp(b_last),
        eqm=jnp.exp(jnp.minimum(b - b_mid, EXP_CLAMP)), ekm=jnp.exp(jnp.minimum(b_mid - b, EXP_CLAMP)))
    return terms


def _gla_mask(rev):
    t = lax.broadcasted_iota(jnp.int32, (CHUNK, CHUNK), 0)
    s = lax.broadcasted_iota(jnp.int32, (CHUNK, CHUNK), 1)
    return (s > t) if rev else (s <= t)


def _gla_head_ops(tm, hd):
    sl = slice(hd * GLA_DK, (hd + 1) * GLA_DK)
    q_b = (tm["qs"][:, sl] * tm["eb"][:, sl]).astype(BF16)
    k_e = (tm["k"][:, sl] * tm["ek"][:, sl]).astype(BF16)
    q_m = (tm["qs"][:, sl] * tm["eqm"][:, sl]).astype(BF16)
    k_m = (tm["k"][:, sl] * tm["ekm"][:, sl]).astype(BF16)
    return sl, q_b, k_e, q_m, k_m


def _gla_fwd(z_qk, z_v, z_zg, wg, bg, t_rows, name, comm=None):
    nc = t_rows // CHUNK
    ng = nc // GLA_G
    rows = CHUNK * GLA_G

    def kern(qf, kf, vf, zf, qb, kb, vb, zb, wg_ref, bg_ref, of_ref, ob_ref, sf_ref, sb_ref, st_f, st_b):
        i = pl.program_id(0)

        @pl.when(i == 0)
        def _():
            st_f[...] = jnp.zeros_like(st_f)
            st_b[...] = jnp.zeros_like(st_b)

        wg_v, bg_v = wg_ref[...], bg_ref[...]
        for rev, (q_r, k_r, v_r, z_r, o_r, s_r, st) in ((False, (qf, kf, vf, zf, of_ref, sf_ref, st_f)),
                                                        (True, (qb, kb, vb, zb, ob_ref, sb_ref, st_b))):
            mask = _gla_mask(rev)
            for ck in (range(GLA_G - 1, -1, -1) if rev else range(GLA_G)):
                cr = slice(ck * CHUNK, (ck + 1) * CHUNK)
                tm = _gla_chunk_terms(q_r[cr, :], k_r[cr, :], z_r[cr, :], wg_v, bg_v, rev)
                v = v_r[cr, :]
                for hd in range(GLA_H):
                    sl, q_b, k_e, q_m, k_m = _gla_head_ops(tm, hd)
                    vs = slice(hd * GLA_DV, (hd + 1) * GLA_DV)
                    v16 = v[:, vs].astype(BF16)
                    a = jnp.where(mask, _dot(q_m, k_m, NT), 0.0).astype(BF16)
                    s_in = st[hd]
                    s_r[ck, hd] = s_in
                    o_r[cr, vs] = _dot(a, v16) + _dot(q_b, s_in.astype(BF16), NT)
                    st[hd] = s_in * tm["e_last"][:, sl] + _dot(v16, k_e, TN)

    fw = lambda c: (lambda i: (i, c))
    rv = lambda c: (lambda i: (ng - 1 - i, c))
    def side(ix):
        return [pl.BlockSpec((rows, QK), ix(0)), pl.BlockSpec((rows, QK), ix(1)),
                pl.BlockSpec((rows, GLA_W), ix(0)), pl.BlockSpec((rows, LANES), ix(0))]
    st_shape = (nc, GLA_H, GLA_DV, GLA_DK)
    res, comm_res = _pcall(
        kern, name, (ng,),
        side(fw) + side(rv) + [pl.BlockSpec(wg.shape, lambda i: (0, 0)), pl.BlockSpec(bg.shape, lambda i: (0, 0))],
        [pl.BlockSpec((rows, GLA_W), fw(0)), pl.BlockSpec((rows, GLA_W), rv(0)),
         pl.BlockSpec((GLA_G,) + st_shape[1:], lambda i: (i, 0, 0, 0)),
         pl.BlockSpec((GLA_G,) + st_shape[1:], lambda i: (ng - 1 - i, 0, 0, 0))],
        [jax.ShapeDtypeStruct((t_rows, GLA_W), F32)] * 2 + [jax.ShapeDtypeStruct(st_shape, F32)] * 2,
        [pltpu.VMEM(st_shape[1:], F32)] * 2, ("arbitrary",),
        (z_qk, z_qk, z_v, z_zg, z_qk, z_qk, z_v, z_zg, wg, bg), comm)
    return res if comm is None else (res, comm_res)


def _gla_bwd(z_qk, z_v, z_zg, wg, bg, do, s_f, s_b, t_rows, name, comm=None):
    nc = t_rows // CHUNK

    def kern(qf, kf, vf, zf, dof, ssf, qb, kb, vb, zb, dob, ssb, wg_ref, bg_ref,
             dqf, dkf, dvf, dpf, dqb, dkb, dvb, dpb, ds_f, ds_b):
        i = pl.program_id(0)

        @pl.when(i == 0)
        def _():
            ds_f[...] = jnp.zeros_like(ds_f)
            ds_b[...] = jnp.zeros_like(ds_b)

        wg_v, bg_v = wg_ref[...], bg_ref[...]
        sides = ((False, (qf, kf, vf, zf, dof, ssf, dqf, dkf, dvf, dpf, ds_f)),
                 (True, (qb, kb, vb, zb, dob, ssb, dqb, dkb, dvb, dpb, ds_b)))
        for rev, (q_r, k_r, v_r, z_r, do_r, ss_r, dq_r, dk_r, dv_r, dp_r, ds) in sides:
            mask = _gla_mask(rev)
            for ck in (range(GLA_G) if rev else range(GLA_G - 1, -1, -1)):
                cr = slice(ck * CHUNK, (ck + 1) * CHUNK)
                tm = _gla_chunk_terms(q_r[cr, :], k_r[cr, :], z_r[cr, :], wg_v, bg_v, rev)
                v = v_r[cr, :]
                d_o = do_r[cr, :]
                db_parts, cross_parts = [], []
                for hd in range(GLA_H):
                    sl, q_b, k_e, q_m, k_m = _gla_head_ops(tm, hd)
                    vs = slice(hd * GLA_DV, (hd + 1) * GLA_DV)
                    v16, do16 = v[:, vs].astype(BF16), d_o[:, vs].astype(BF16)
                    a = jnp.where(mask, _dot(q_m, k_m, NT), 0.0).astype(BF16)
                    da = jnp.where(mask, _dot(do16, v16, NT), 0.0).astype(BF16)
                    s_in = ss_r[ck, hd]
                    s_in16 = s_in.astype(BF16)
                    ds_out = ds[hd]
                    ds16 = ds_out.astype(BF16)
                    dv_r[cr, vs] = _dot(a, do16, TN) + _dot(k_e, ds16, NT)
                    dqs = _dot(da, k_m) * tm["eqm"][:, sl] + _dot(do16, s_in16) * tm["eb"][:, sl]
                    dk = _dot(da, q_m, TN) * tm["ekm"][:, sl] + _dot(v16, ds16) * tm["ek"][:, sl]
                    ds[hd] = ds_out * tm["e_last"][:, sl] + _dot(do16, q_b, TN)
                    dq_r[cr, sl] = dqs
                    dk_r[cr, sl] = dk
                    db_parts.append(tm["qs"][:, sl] * dqs - tm["k"][:, sl] * dk)
                    s_out = s_in * tm["e_last"][:, sl] + _dot(v16, k_e, TN)
                    cross_parts.append(jnp.sum(s_out * ds_out, axis=0, keepdims=True))
                d_b = jnp.concatenate(db_parts, axis=1)
                dg = _cumsum_rows(d_b, not rev) + jnp.concatenate(cross_parts, axis=1)
                dp_r[cr, :] = dg * (1.0 / GLA_GATE_NORM) * _sigmoid(-tm["pre"])

    ng = nc // GLA_G
    rows = CHUNK * GLA_G
    fw = lambda c: (lambda i: (ng - 1 - i, c))
    rv = lambda c: (lambda i: (i, c))
    st_blk = (GLA_G, GLA_H, GLA_DV, GLA_DK)
    def side(ix, st_ix):
        return [pl.BlockSpec((rows, QK), ix(0)), pl.BlockSpec((rows, QK), ix(1)),
                pl.BlockSpec((rows, GLA_W), ix(0)), pl.BlockSpec((rows, LANES), ix(0)),
                pl.BlockSpec((rows, GLA_W), ix(0)), pl.BlockSpec(st_blk, st_ix)]
    def oside(ix):
        return [pl.BlockSpec((rows, QK), ix(0)), pl.BlockSpec((rows, QK), ix(0)),
                pl.BlockSpec((rows, GLA_W), ix(0)), pl.BlockSpec((rows, QK), ix(0))]
    o_shapes = [jax.ShapeDtypeStruct((t_rows, QK), F32), jax.ShapeDtypeStruct((t_rows, QK), F32),
                jax.ShapeDtypeStruct((t_rows, GLA_W), F32), jax.ShapeDtypeStruct((t_rows, QK), F32)]
    res, comm_res = _pcall(
        kern, name, (ng,),
        (side(fw, lambda i: (ng - 1 - i, 0, 0, 0)) + side(rv, lambda i: (i, 0, 0, 0))
         + [pl.BlockSpec(wg.shape, lambda i: (0, 0)), pl.BlockSpec(bg.shape, lambda i: (0, 0))]),
        oside(fw) + oside(rv), o_shapes * 2, [pltpu.VMEM((GLA_H, GLA_DV, GLA_DK), F32)] * 2, ("arbitrary",),
        (z_qk, z_qk, z_v, z_zg, do, s_f, z_qk, z_qk, z_v, z_zg, do, s_b, wg, bg), comm)
    return res if comm is None else (res, comm_res)


FLAT_W = 1024


def _adam_math(wv, gv, mv, vv):
    bc1 = 1.0 - ADAM_B1 ** ADAM_STEP
    bc2 = 1.0 - ADAM_B2 ** ADAM_STEP
    m_new = ADAM_B1 * mv + (1.0 - ADAM_B1) * gv
    v_new = ADAM_B2 * vv + (1.0 - ADAM_B2) * (gv * gv)
    delta = -ADAM_LR * ((m_new / bc1) / (jnp.sqrt(v_new / bc2) + ADAM_EPS) + ADAM_WD * wv)
    return delta, m_new, v_new


def _row_tile(rows, cap=256):
    t = cap
    while rows % t:
        t //= 2
    assert t >= SUBLANES, rows
    return t


def _adamw_layers(w, m, v, g_layers, name):
    depth, rows, cols = w.shape
    tr = _row_tile(rows)

    def kern(w_ref, m_ref, v_ref, *rest):
        g_refs, (go_ref, d_ref, mo_ref, vo_ref) = rest[:depth], rest[depth:]
        l = pl.program_id(0)
        gv = g_refs[0][...]
        for k in range(1, depth):
            gv = jnp.where(l == k, g_refs[k][...], gv)
        delta, m_new, v_new = _adam_math(w_ref[...], gv, m_ref[...], v_ref[...])
        go_ref[...] = gv
        d_ref[...] = delta
        mo_ref[...] = m_new
        vo_ref[...] = v_new

    stacked = pl.BlockSpec((None, tr, cols), lambda l, i: (l, i, 0))
    return pl.pallas_call(
        kern, name=name, grid=(depth, rows // tr),
        in_specs=[stacked] * 3 + [pl.BlockSpec((tr, cols), lambda l, i: (i, 0))] * depth,
        out_specs=[stacked] * 4, out_shape=[jax.ShapeDtypeStruct(w.shape, F32)] * 4,
        compiler_params=_cparams(("arbitrary", "arbitrary")),
    )(w, m, v, *g_layers)


def _adamw_small(ws, gs, ms, vs):
    n = len(ws)

    def kern(*refs):
        ins, outs = refs[:4 * n], refs[4 * n:]
        for k in range(n):
            delta, m_new, v_new = _adam_math(ins[k][...], ins[n + k][...], ins[2 * n + k][...], ins[3 * n + k][...])
            outs[k][...] = delta
            outs[n + k][...] = m_new
            outs[2 * n + k][...] = v_new

    shapes = [jax.ShapeDtypeStruct(w.shape, F32) for w in ws]
    res = pl.pallas_call(kern, name="adamw_small", out_shape=shapes * 3,
                         compiler_params=pltpu.CompilerParams(vmem_limit_bytes=VMEM_LIMIT))(*ws, *gs, *ms, *vs)
    return res[:n], res[n:2 * n], res[2 * n:]


def _add_sibling(g, got, c_idx, name):
    _, _, rows, cols = g.shape
    tr = _row_tile(rows)

    def kern(c_ref, g_ref, got_ref, o_ref):
        o_ref[...] = (g_ref[...] + got_ref[...]).astype(BF16)

    grid_spec = pltpu.PrefetchScalarGridSpec(
        num_scalar_prefetch=1, grid=(N_CHIPS, rows // tr),
        in_specs=[pl.BlockSpec((None, None, tr, cols), lambda q, i, c_ref: (q, c_ref[0], i, 0)),
                  pl.BlockSpec((None, tr, cols), lambda q, i, c_ref: (q, i, 0))],
        out_specs=pl.BlockSpec((None, tr, cols), lambda q, i, c_ref: (q, i, 0)))
    return pl.pallas_call(
        kern, name=name, grid_spec=grid_spec, out_shape=jax.ShapeDtypeStruct((N_CHIPS, rows, cols), BF16),
        compiler_params=_cparams(("arbitrary", "arbitrary")),
    )(c_idx, g, got)


def _sum_chips(x, name):
    _, rows, cols = x.shape
    tr = _row_tile(rows)

    def kern(x_ref, o_ref):
        xv = x_ref[...].astype(F32)
        o_ref[...] = ((xv[0] + xv[1]) + xv[2]) + xv[3]

    return pl.pallas_call(
        kern, name=name, grid=(rows // tr,),
        in_specs=[pl.BlockSpec((N_CHIPS, tr, cols), lambda i: (0, i, 0))],
        out_specs=pl.BlockSpec((tr, cols), lambda i: (i, 0)),
        out_shape=jax.ShapeDtypeStruct((rows, cols), F32),
        compiler_params=_cparams(("arbitrary",)),
    )(x)


def _place():
    x, y, c = lax.axis_index("x"), lax.axis_index("y"), lax.axis_index("c")
    return x, y, c


def _other_chips(x, y):
    return [(1 - x, y), (x, 1 - y), (1 - x, 1 - y)]


class _Comm:
    def __init__(self, ins, out_shapes, n_sems, stage, body):
        self.ins, self.out_shapes, self.body = list(ins), list(out_shapes), body
        self.scratch = [pltpu.SemaphoreType.DMA((n,)) for n in n_sems] + [pltpu.VMEM(s, d) for s, d in stage]


ANY_SPEC = pl.BlockSpec(memory_space=pl.ANY)


def _run_comm(comm, name):
    def kern(*refs):
        comm.body(refs, True, True)

    return pl.pallas_call(
        kern, name=name, in_specs=[ANY_SPEC] * len(comm.ins), out_specs=[ANY_SPEC] * len(comm.out_shapes),
        out_shape=comm.out_shapes, scratch_shapes=comm.scratch,
        compiler_params=pltpu.CompilerParams(has_side_effects=True, vmem_limit_bytes=VMEM_LIMIT),
    )(*comm.ins)


def _pcall(kern, name, grid, in_specs, out_specs, out_shape, scratch_shapes, sem, args, comm=None):
    if comm is None:
        return pl.pallas_call(kern, name=name, grid=grid, in_specs=in_specs, out_specs=out_specs, out_shape=out_shape,
                              scratch_shapes=scratch_shapes, compiler_params=_cparams(sem))(*args), None
    n_in, n_out, n_scr = len(in_specs), len(out_specs), len(scratch_shapes)
    c_in, c_out = len(comm.ins), len(comm.out_shapes)

    def hosted(*refs):
        a = n_in + c_in
        b = a + n_out + c_out
        main = refs[:n_in] + refs[a:a + n_out] + refs[b:b + n_scr]
        extra = refs[n_in:a] + refs[a + n_out:b] + refs[b + n_scr:]
        ids = [pl.program_id(d) for d in range(len(grid))]
        first, last = ids[0] == 0, ids[0] == grid[0] - 1
        for i, g in zip(ids[1:], grid[1:]):
            first, last = jnp.logical_and(first, i == 0), jnp.logical_and(last, i == g - 1)

        @pl.when(first)
        def _():
            comm.body(extra, True, False)

        kern(*main)

        @pl.when(last)
        def _():
            comm.body(extra, False, True)

    res = pl.pallas_call(
        hosted, name=name, grid=grid, in_specs=list(in_specs) + [ANY_SPEC] * c_in,
        out_specs=list(out_specs) + [ANY_SPEC] * c_out, out_shape=list(out_shape) + comm.out_shapes,
        scratch_shapes=list(scratch_shapes) + comm.scratch,
        compiler_params=pltpu.CompilerParams(dimension_semantics=("arbitrary",) * len(grid),
                                             vmem_limit_bytes=VMEM_LIMIT, has_side_effects=True),
    )(*args, *comm.ins)
    return res[:n_out], res[n_out:]


class _StagedCopies:
    def __init__(self, pairs, bufs, sem_in, sem_out):
        self.ins = [pltpu.make_async_copy(src, buf, sem_in.at[i]) for i, ((src, _), buf) in enumerate(zip(pairs, bufs))]
        self.outs = [pltpu.make_async_copy(buf, dst, sem_out.at[i]) for i, ((_, dst), buf) in enumerate(zip(pairs, bufs))]

    def load(self):
        for cp in self.ins:
            cp.start()

    def store(self):
        for cp in self.ins:
            cp.wait()
        for cp in self.outs:
            cp.start()

    def finish(self):
        for cp in self.outs:
            cp.wait()


DMA_CHUNK_BYTES = 1 << 20
DMA_MAX_CHUNKS = 4


def _row_chunks(rows, row_bytes, align=16):
    units = rows // align
    k = max(1, min(DMA_MAX_CHUNKS, -(-rows * row_bytes // DMA_CHUNK_BYTES), units))
    out, start = [], 0
    for i in range(k):
        size = (units // k + (1 if i < units % k else 0)) * align
        out.append((start, size))
        start += size
    if start < rows:
        out[-1] = (out[-1][0], out[-1][1] + rows - start)
    return out


def _row_bytes(shape, dtype):
    return math.prod(shape[1:]) * jnp.dtype(dtype).itemsize


def _remote(src, dst, send, recv, idx, dev):
    return pltpu.make_async_remote_copy(src, dst, send.at[idx], recv.at[idx], device_id=dev, device_id_type=MESH_ID)


def _allgather_chips(xs):
    na = len(xs)
    chunks = [_row_chunks(x.shape[1], _row_bytes(x.shape[1:], x.dtype)) for x in xs]
    n_cp = 3 * sum(len(ch) for ch in chunks)

    def body(refs, start, finish):
        x_refs, o_refs = refs[:na], refs[na:2 * na]
        send1, recv1, send2, recv2, loc_in, loc_out = refs[2 * na:2 * na + 6]
        px, py, c = _place()
        me = 2 * px + py
        sib = (px, py, 1 - c)
        own = _StagedCopies([(x, o.at[me]) for x, o in zip(x_refs, o_refs)], refs[2 * na + 6:], loc_in, loc_out)
        plan = [(a, qx, qy, pl.ds(s, n)) for a in range(na) for (qx, qy) in _other_chips(px, py) for (s, n) in chunks[a]]
        first = [_remote(x_refs[a].at[c, rows], o_refs[a].at[me, c, rows], send1, recv1, i, (qx, qy, c))
                 for i, (a, qx, qy, rows) in enumerate(plan)]
        if start:
            own.load()
            for cp in first:
                cp.start()
        if finish:
            own.store()
            passed = []
            for i, (a, qx, qy, rows) in enumerate(plan):
                slot = o_refs[a].at[2 * qx + qy, c, rows]
                _remote(slot, slot, send1, recv1, i, (qx, qy, c)).wait_recv()
                fwd = _remote(slot, slot, send2, recv2, i, sib)
                fwd.start()
                passed.append(fwd)
            for i, (a, qx, qy, rows) in enumerate(plan):
                slot = o_refs[a].at[2 * qx + qy, 1 - c, rows]
                _remote(slot, slot, send2, recv2, i, sib).wait_recv()
            for cp in first + passed:
                cp.wait_send()
            own.finish()

    outs = [jax.ShapeDtypeStruct((N_CHIPS,) + x.shape, x.dtype) for x in xs]
    return _Comm(xs, outs, [n_cp, n_cp, n_cp, n_cp, na, na], [(x.shape, x.dtype) for x in xs], body)


def _sibling_halves(gs):
    na = len(gs)
    chunks = [_row_chunks(g.shape[2], _row_bytes(g.shape[2:], g.dtype)) for g in gs]
    n_cp = N_CHIPS * sum(len(ch) for ch in chunks)

    def body(refs, start, finish):
        g_refs, o_refs = refs[:na], refs[na:2 * na]
        send, recv = refs[2 * na:]
        px, py, c = _place()
        plan = [(a, q, pl.ds(s, n)) for a in range(na) for q in range(N_CHIPS) for (s, n) in chunks[a]]
        cps = [_remote(g_refs[a].at[q, 1 - c, rows], o_refs[a].at[q, rows], send, recv, i, (px, py, 1 - c))
               for i, (a, q, rows) in enumerate(plan)]
        if start:
            for cp in cps:
                cp.start()
        if finish:
            for cp in cps:
                cp.wait()

    outs = [jax.ShapeDtypeStruct((N_CHIPS,) + g.shape[2:], g.dtype) for g in gs]
    return _Comm(gs, outs, [n_cp, n_cp], [], body)


def _chip_exchange(ps):
    na = len(ps)
    chunks = [_row_chunks(p.shape[1], _row_bytes(p.shape[1:], p.dtype)) for p in ps]
    n_cp = 3 * sum(len(ch) for ch in chunks)

    def body(refs, start, finish):
        p_refs, o_refs = refs[:na], refs[na:2 * na]
        send, recv, loc_in, loc_out = refs[2 * na:2 * na + 4]
        px, py, c = _place()
        me = 2 * px + py
        own = _StagedCopies([(p.at[me], o.at[me]) for p, o in zip(p_refs, o_refs)], refs[2 * na + 4:], loc_in, loc_out)
        plan = [(a, qx, qy, pl.ds(s, n)) for a in range(na) for (qx, qy) in _other_chips(px, py) for (s, n) in chunks[a]]
        cps = [_remote(p_refs[a].at[2 * qx + qy, rows], o_refs[a].at[me, rows], send, recv, i, (qx, qy, c))
               for i, (a, qx, qy, rows) in enumerate(plan)]
        if start:
            own.load()
            for cp in cps:
                cp.start()
        if finish:
            own.store()
            for cp in cps:
                cp.wait()
            own.finish()

    outs = [jax.ShapeDtypeStruct(p.shape, p.dtype) for p in ps]
    return _Comm(ps, outs, [n_cp, n_cp, na, na], [(p.shape[1:], p.dtype) for p in ps], body)


def _sibling_join(rs):
    na = len(rs)
    chunks = [_row_chunks(r.shape[0], _row_bytes(r.shape, r.dtype)) for r in rs]
    n_cp = sum(len(ch) for ch in chunks)

    def body(refs, start, finish):
        r_refs, o_refs = refs[:na], refs[na:2 * na]
        send, recv, loc_in, loc_out = refs[2 * na:2 * na + 4]
        px, py, c = _place()
        own = _StagedCopies([(r, o.at[c]) for r, o in zip(r_refs, o_refs)], refs[2 * na + 4:], loc_in, loc_out)
        plan = [(a, pl.ds(s, n)) for a in range(na) for (s, n) in chunks[a]]
        cps = [_remote(r_refs[a].at[rows], o_refs[a].at[c, rows], send, recv, i, (px, py, 1 - c))
               for i, (a, rows) in enumerate(plan)]
        if start:
            own.load()
            for cp in cps:
                cp.start()
        if finish:
            own.store()
            for cp in cps:
                cp.wait()
            own.finish()

    outs = [jax.ShapeDtypeStruct((2,) + r.shape, r.dtype) for r in rs]
    return _Comm(rs, outs, [n_cp, n_cp, na, na], [(r.shape, r.dtype) for r in rs], body)


def _reduce_stages(gs, c_idx, tag):
    got = yield _sibling_halves(gs)
    part = [_add_sibling(g, o, c_idx, f"reduce_add_sibling{tag}_{a}") for a, (g, o) in enumerate(zip(gs, got))]
    landed = yield _chip_exchange(part)
    mine = [_sum_chips(x, f"reduce_sum_chips{tag}_{a}") for a, x in enumerate(landed)]
    return (yield _sibling_join(mine))


class _Staged:
    def __init__(self, gen, tag):
        self.gen, self.tag, self.k, self.value = gen, tag, 0, None
        self.cur = next(gen)

    def stage(self):
        return self.cur

    def done(self, results):
        self.k += 1
        try:
            self.cur = self.gen.send(results)
        except StopIteration as stop:
            self.cur, self.value = None, stop.value

    def drain(self):
        while self.cur is not None:
            self.done(_run_comm(self.cur, f"comm{self.tag}_{self.k}"))
        return self.value


WEIGHTS = ["meta_tokens", "norm_mix_pre", "norm_mix_post", "norm_mlp_pre", "norm_mlp_post", "w_in", "conv_w",
           "conv_b", "lru_wa_f", "lru_ba_f", "lru_wx_f", "lru_bx_f", "lru_lambda_f", "lru_wa_b", "lru_ba_b",
           "lru_wx_b", "lru_bx_b", "lru_lambda_b", "gla_wg_f", "gla_bg_f", "gla_wg_b", "gla_bg_b", "gla_head_norm",
           "w_out", "w_mlp_up", "w_mlp_down"]
BIG = ("w_in", "w_out", "w_mlp_up", "w_mlp_down")
SMALL = [n for n in WEIGHTS if n not in BIG]
SMALL_SHARDED = {"meta_tokens": 1, "conv_w": 2, "gla_wg_f": 2, "gla_wg_b": 2}
N_CHIPS = 4
SMALL_RAW = [("g1", (1, D_MODEL)), ("g2", (1, D_MODEL)), ("g3", (1, D_MODEL)), ("g4", (1, D_MODEL)),
             ("conv_w", (4, LRU_W)), ("conv_b", (1, LRU_W)), ("lru_wg", (LRU_HEADS, LRU_HD, 4 * LRU_HD)),
             ("lru_pb", (SUBLANES, LRU_W)), ("gla_wg", (2 * GLA_RANK, 2 * QK)), ("gla_bg", (1, 2 * QK)),
             ("head_norm", (1, GLA_W))]
META_SHAPE = (N_META, D_MODEL)


def _pad_to(v, n):
    return jnp.pad(v, (0, n - v.shape[0]))


def _round_up(n, m):
    return -(-n // m) * m


def _flat_cat(arrs, total):
    return _pad_to(jnp.concatenate([a.reshape(-1) for a in arrs]), total)


def _split_flat(flat, shapes):
    out, off = [], 0
    for s in shapes:
        n = math.prod(s)
        out.append(flat[off:off + n].reshape(s))
        off += n
    return out


def _my_shard(full, axis, chip):
    n = full.shape[axis] // N_CHIPS
    return lax.dynamic_slice_in_dim(full, chip * n, n, axis=axis)


def _block_diag_gates(wa_f, wx_f, wa_b, wx_b):
    eye2 = jnp.eye(2, dtype=wa_f.dtype)

    def bd(w):
        w4 = w.reshape(4, 2, LRU_HD, LRU_HD)
        return (w4[:, :, :, None, :] * eye2[None, :, None, :, None]).reshape(4, LANES, LANES)
    return jnp.concatenate([bd(wa_f), bd(wx_f), bd(wa_b), bd(wx_b)], axis=2).astype(BF16)


def _gate_diag_blocks(dwh, o):
    return dwh[:, :, o * LRU_HD:(o + 1) * LRU_HD]


def kernel(x, meta_tokens, norm_mix_pre, norm_mix_post, norm_mlp_pre, norm_mlp_post, w_in, conv_w, conv_b, lru_wa_f, lru_ba_f, lru_wx_f, lru_bx_f, lru_lambda_f, lru_wa_b, lru_ba_b, lru_wx_b, lru_bx_b, lru_lambda_b, gla_wg_f, gla_bg_f, gla_wg_b, gla_bg_b, gla_head_norm, w_out, w_mlp_up, w_mlp_down, loss_target, m_meta_tokens, m_norm_mix_pre, m_norm_mix_post, m_norm_mlp_pre, m_norm_mlp_post, m_w_in, m_conv_w, m_conv_b, m_lru_wa_f, m_lru_ba_f, m_lru_wx_f, m_lru_bx_f, m_lru_lambda_f, m_lru_wa_b, m_lru_ba_b, m_lru_wx_b, m_lru_bx_b, m_lru_lambda_b, m_gla_wg_f, m_gla_bg_f, m_gla_wg_b, m_gla_bg_b, m_gla_head_norm, m_w_out, m_w_mlp_up, m_w_mlp_down, v_meta_tokens, v_norm_mix_pre, v_norm_mix_post, v_norm_mlp_pre, v_norm_mlp_post, v_w_in, v_conv_w, v_conv_b, v_lru_wa_f, v_lru_ba_f, v_lru_wx_f, v_lru_bx_f, v_lru_lambda_f, v_lru_wa_b, v_lru_ba_b, v_lru_wx_b, v_lru_bx_b, v_lru_lambda_b, v_gla_wg_f, v_gla_bg_f, v_gla_wg_b, v_gla_bg_b, v_gla_head_norm, v_w_out, v_w_mlp_up, v_w_mlp_down):
    args = dict(locals())
    w_loc = {n: args[n] for n in WEIGHTS}
    m_loc = {n: args["m_" + n] for n in WEIGHTS}
    v_loc = {n: args["v_" + n] for n in WEIGHTS}
    seq = x.shape[1]
    t_rows = FRONT + seq + BACK
    assert t_rows % 768 == 0 and seq % FRONT == 0, seq
    chip = 2 * lax.axis_index("x") + lax.axis_index("y")
    c_idx = lax.axis_index("c").astype(jnp.int32).reshape(1)

    def halves(a):
        return a.reshape((2, a.shape[0] // 2) + a.shape[1:])

    big16 = {n: w_loc[n].astype(BF16) for n in BIG}
    sm_names = list(SMALL_SHARDED)
    sm_len = _round_up(sum(w_loc[n].size for n in sm_names), 2 * SUBLANES * FLAT_W)
    sm_pack = _flat_cat([w_loc[n] for n in sm_names], sm_len).reshape(2, -1, FLAT_W)
    g_in0, g_small = _run_comm(_allgather_chips([halves(big16["w_in"][0]), sm_pack]), "gather_first")
    sm_parts = [_split_flat(g_small[q].reshape(-1), [w_loc[n].shape for n in sm_names]) for q in range(N_CHIPS)]
    full = {n: jnp.concatenate([sm_parts[q][i] for q in range(N_CHIPS)], axis=SMALL_SHARDED[n])
            for i, n in enumerate(sm_names)}
    for n in SMALL:
        if n not in SMALL_SHARDED:
            full[n] = w_loc[n]

    def full_w_in(g):
        w = jnp.transpose(g.reshape(N_CHIPS, D_MODEL, -1), (1, 0, 2)).reshape(D_MODEL, D_IN)
        return jnp.pad(w, ((0, 0), (0, D_IN_PAD - D_IN)))

    def full_rest(g_out, g_up, g_down):
        return dict(w_out=g_out.reshape(-1, D_MODEL),
                    w_up=jnp.transpose(g_up.reshape(N_CHIPS, D_MODEL, -1), (1, 0, 2)).reshape(D_MODEL, D_FF),
                    w_down=g_down.reshape(D_FF, D_MODEL))

    later_gathers = [_allgather_chips([halves(big16[n][0]) for n in BIG[1:]]),
                     _allgather_chips([halves(big16[n][l]) for l in range(1, DEPTH) for n in BIG])]

    meta_blk = jnp.concatenate([jnp.zeros((FRONT - N_META, D_MODEL), F32), full["meta_tokens"]], axis=0)
    h = jnp.concatenate([meta_blk, x[0], jnp.zeros((BACK, D_MODEL), F32)], axis=0)
    target = loss_target[0]

    layer_w = []
    for l in range(DEPTH):
        wg = jnp.zeros((LANES, 2 * QK), F32)
        wg = wg.at[0:GLA_RANK, 0:QK].set(full["gla_wg_f"][l]).at[GLA_RANK:2 * GLA_RANK, QK:].set(full["gla_wg_b"][l])
        pb = jnp.stack([full["lru_ba_f"][l], full["lru_bx_f"][l], full["lru_ba_b"][l], full["lru_bx_b"][l],
                        full["lru_lambda_f"][l], full["lru_lambda_b"][l], jnp.zeros((LRU_W,), F32),
                        jnp.zeros((LRU_W,), F32)])
        layer_w.append(dict(
            g1=full["norm_mix_pre"][l][None], g2=full["norm_mix_post"][l][None],
            g3=full["norm_mlp_pre"][l][None], g4=full["norm_mlp_post"][l][None],
            conv_w=full["conv_w"][l], conv_b=full["conv_b"][l][None],
            lru_wg=_block_diag_gates(full["lru_wa_f"][l], full["lru_wx_f"][l], full["lru_wa_b"][l], full["lru_wx_b"][l]),
            lru_pb=pb, gla_wg=wg.astype(BF16),
            gla_bg=jnp.concatenate([full["gla_bg_f"][l], full["gla_bg_b"][l]])[None],
            head_norm=full["gla_head_norm"][l][None]))
    layer_w[0]["w_in"] = full_w_in(g_in0)

    def split_z(acc):
        return (acc[:, 0:1024], acc[:, 1024:1536], acc[:, 1536:2048], acc[:, 2048:2560], acc[:, ZG_OFF:ZG_OFF + LANES])

    def relu2(acc):
        r = jnp.maximum(acc, 0.0)
        return acc, r * r

    tm_e = 768
    tm_ff = 1408 if t_rows % 1408 == 0 else 768
    tk_dw = 2816 if t_rows % 2816 == 0 else 768

    def post_mix(acc, hh, g2, g3):
        h1 = hh + _rms_fwd(acc, g2)
        return acc, h1, _rms_fwd(h1, g3)

    def post_mlp(acc, h1, g4, g1_next):
        h2 = h1 + _rms_fwd(acc, g4)
        return acc, h2, _rms_fwd(h2, g1_next)

    def post_mlp_loss(acc, h1, tgt, x_rows, g4):
        h2 = h1 + _rms_fwd(acc, g4)
        err = (h2 - tgt) * x_rows[:, 0:1]
        d_h = err * (1.0 / D_MODEL)
        dff, dg4 = _rms_bwd(acc, g4, d_h)
        loss_part = jnp.zeros((SUBLANES, LANES), F32) + jnp.sum(err * err) * (0.5 / D_MODEL)
        return d_h, dff, loss_part, dg4

    row3 = [(D_MODEL, F32), (D_MODEL, F32), (D_MODEL, BF16)]
    vec = ((1, D_MODEL), F32)
    target_p = jnp.concatenate([jnp.zeros((FRONT, D_MODEL), F32), target, jnp.zeros((BACK, D_MODEL), F32)], axis=0)
    x_rows = jnp.concatenate([jnp.zeros((FRONT, LANES), F32), jnp.ones((seq, LANES), F32), jnp.zeros((BACK, LANES), F32)])
    saved = []
    n1 = _norm_cast(h, layer_w[0]["g1"], t_rows, "norm_mix_pre_l0")
    for l in range(DEPTH):
        lw = layer_w[l]
        tag = f"_l{l}"
        z_lru, z_qk, z_v, z_go, z_zg = _mm(
            n1, lw["w_in"], "nn", 768, D_IN_PAD, D_MODEL,
            [(1024, F32), (512, F32), (512, F32), (512, F32), (LANES, F32)], "in_proj" + tag, epilogue=split_z)
        lru_args = (z_lru, lw["conv_w"], lw["conv_b"], lw["lru_wg"], lw["lru_pb"], t_rows, seq, "lru_fwd" + tag)
        gla_args = (z_qk, z_v, z_zg, lw["gla_wg"], lw["gla_bg"], t_rows, "gla_fwd" + tag)
        if l == 0:
            (h_f, h_b), g_rest = _lru_fwd(*lru_args, comm=later_gathers[0])
            lw.update(full_rest(*g_rest))
            (o_f, o_b, s_f, s_b), g_next = _gla_fwd(*gla_args, comm=later_gathers[1])
            for l2 in range(1, DEPTH):
                g_l2 = g_next[(l2 - 1) * len(BIG):l2 * len(BIG)]
                layer_w[l2].update(full_rest(*g_l2[1:]), w_in=full_w_in(g_l2[0]))
        else:
            h_f, h_b = _lru_fwd(*lru_args)
            o_f, o_b, s_f, s_b = _gla_fwd(*gla_args)
        ymix = _mix_fwd(h_f, h_b, z_lru, o_f, o_b, z_go, lw["head_norm"], t_rows, "mix_fwd" + tag)
        mix, h1, n3 = _mm(ymix, lw["w_out"], "nn", 768, D_MODEL, D_MODEL, row3, "out_proj" + tag, epilogue=post_mix,
                          rows_in=(h,), consts=(lw["g2"], lw["g3"]))
        u, act = _mm(n3, lw["w_up"], "nn", tm_ff, 1024, D_MODEL, [(D_FF, BF16), (D_FF, BF16)], "mlp_up" + tag,
                     epilogue=relu2)
        sv = dict(h=h, n1=n1, z_lru=z_lru, z_qk=z_qk, z_v=z_v, z_go=z_go, z_zg=z_zg, h_f=h_f, h_b=h_b,
                  o_f=o_f, o_b=o_b, s_f=s_f, s_b=s_b, ymix=ymix, mix=mix, h1=h1, n3=n3, u=u, act=act)
        if l + 1 < DEPTH:
            sv["ff"], h, n1 = _mm(act, lw["w_down"], "nn", tm_e, D_MODEL, D_FF, row3, "mlp_down" + tag,
                                  epilogue=post_mlp, rows_in=(h1,), consts=(lw["g4"], layer_w[l + 1]["g1"]))
        else:
            dh, dff, loss_part, dg4 = _mm(act, lw["w_down"], "nn", tm_e, D_MODEL, D_FF,
                                          [(D_MODEL, F32), (D_MODEL, BF16)], "mlp_down_loss" + tag,
                                          epilogue=post_mlp_loss, rows_in=(h1, target_p, x_rows), consts=(lw["g4"],),
                                          accs=[((SUBLANES, LANES), F32), vec])
        saved.append(sv)

    loss = lax.psum(loss_part[0, 0], ("x", "y", "c"))

    small_len = _round_up(sum(math.prod(s) for _, s in SMALL_RAW) + math.prod(META_SHAPE),
                          N_CHIPS * 2 * 2 * SUBLANES * FLAT_W)
    totals = [None] * DEPTH
    pend = None

    def with_stage(staged, fn):
        comm = staged.stage() if staged is not None else None
        if comm is None:
            return fn(None)
        res, comm_res = fn(comm)
        staged.done(comm_res)
        return res

    for l in reversed(range(DEPTH)):
        lw, sv = layer_w[l], saved[l]
        tag = f"_l{l}"
        raw = {"g4": dg4}
        gw_down = with_stage(pend, lambda comm: _mm(sv["act"], dff, "tn", 1024, D_MODEL, tk_dw, [(D_MODEL, F32)],
                                                    "dw_down" + tag, comm=comm))[0]
        du = _mm(dff, lw["w_down"], "nt", tm_ff, 1024, D_MODEL, [(D_FF, BF16)], "d_act" + tag,
                 epilogue=lambda acc, uu: (acc * 2.0 * jnp.maximum(uu.astype(F32), 0.0),), extras=(sv["u"],))[0]
        gw_up = _mm(sv["n3"], du, "tn", D_MODEL, D_FF // N_CHIPS, tk_dw, [(D_FF, F32)], "dw_up" + tag,
                    col_blocks_first=True)[0]

        def pre_mlp_bwd(acc, h1, d_h, mix, g3, g2):
            dx, dg3 = _rms_bwd(h1, g3, acc)
            d_h1 = d_h + dx
            d_mix, dg2 = _rms_bwd(mix, g2, d_h1)
            return d_h1, d_mix, dg3, dg2

        dh1, dmix, raw["g3"], raw["g2"] = _mm(
            du, lw["w_up"], "nt", tm_e, D_MODEL, D_FF, [(D_MODEL, F32), (D_MODEL, BF16)], "d_n3" + tag,
            epilogue=pre_mlp_bwd, rows_in=(sv["h1"], dh, sv["mix"]), consts=(lw["g3"], lw["g2"]), accs=[vec, vec])
        gw_out = _mm(sv["ymix"], dmix, "tn", D_MODEL, D_MODEL, tk_dw, [(D_MODEL, F32)], "dw_out" + tag)[0]
        mats = [gw_out.reshape(N_CHIPS, 2, -1, D_MODEL), gw_up.reshape(N_CHIPS, 2, D_MODEL // 2, D_FF // N_CHIPS),
                gw_down.reshape(N_CHIPS, 2, -1, D_MODEL)]
        early = _Staged(_reduce_stages(mats, c_idx, tag + "e"), tag + "e") if l == 0 else None
        half4 = [(LRU_W, F32), (LRU_W, F32), (GLA_W, F32), (GLA_W, F32)]
        dh_lru, dgate, do, dgo, raw["head_norm"] = with_stage(early, lambda comm: _mm(
            dmix, lw["w_out"], "nt", 768, D_MODEL, D_MODEL, half4, "d_ymix" + tag, epilogue=_mix_bwd_tile,
            rows_in=(sv["h_f"], sv["h_b"], sv["z_lru"], sv["o_f"], sv["o_b"], sv["z_go"]), consts=(lw["head_norm"],),
            accs=[((1, GLA_W), F32)], comm=comm))
        gla_grads = with_stage(pend, lambda comm: _gla_bwd(sv["z_qk"], sv["z_v"], sv["z_zg"], lw["gla_wg"],
                                                           lw["gla_bg"], do, sv["s_f"], sv["s_b"], t_rows,
                                                           "gla_bwd" + tag, comm=comm))
        dx_br, raw["conv_w"], raw["conv_b"], raw["lru_wg"], raw["lru_pb"] = with_stage(early, lambda comm: _lru_bwd(
            sv["z_lru"], lw["conv_w"], lw["conv_b"], lw["lru_wg"], lw["lru_pb"], sv["h_f"], sv["h_b"], dh_lru,
            t_rows, seq, "lru_bwd" + tag, comm=comm))
        dz, dwg_gla, raw["gla_bg"] = _dz_assemble(dx_br, dgate, gla_grads, dgo, sv["z_zg"], lw["gla_wg"], t_rows,
                                                  "dz_assemble" + tag)
        raw["gla_wg"] = dwg_gla[0:2 * GLA_RANK]
        gw_in = with_stage(pend, lambda comm: _mm(sv["n1"], dz, "tn", D_MODEL, 896, tk_dw, [(D_IN_PAD, F32)],
                                                  "dw_in" + tag, comm=comm))[0]
        if l > 0:
            def pre_mix_bwd(acc, hh, d_h1, ff_prev, g1, g4_prev):
                dx, dg1 = _rms_bwd(hh, g1, acc)
                d_h = d_h1 + dx
                d_ff, dg4_prev = _rms_bwd(ff_prev, g4_prev, d_h)
                return d_h, d_ff, dg1, dg4_prev

            dh, dff, raw["g1"], dg4 = _mm(
                dz, lw["w_in"], "nt", tm_e, D_MODEL, D_IN_PAD, [(D_MODEL, F32), (D_MODEL, BF16)], "d_n1" + tag,
                epilogue=pre_mix_bwd, rows_in=(sv["h"], dh1, saved[l - 1]["ff"]),
                consts=(lw["g1"], layer_w[l - 1]["g4"]), accs=[vec, vec])
        else:
            def pre_mix_bwd0(acc, hh, d_h1, g1):
                dx, dg1 = _rms_bwd(hh, g1, acc)
                return d_h1 + dx, dg1

            dh, raw["g1"] = with_stage(early, lambda comm: _mm(
                dz, lw["w_in"], "nt", tm_e, D_MODEL, D_IN_PAD, [(D_MODEL, F32)], "d_n1" + tag, epilogue=pre_mix_bwd0,
                rows_in=(sv["h"], dh1), consts=(lw["g1"],), accs=[vec], comm=comm))

        cols = D_IN // N_CHIPS
        in_sh = jnp.stack([gw_in[:, q * cols:(q + 1) * cols] for q in range(N_CHIPS)])
        meta_g = dh[FRONT - N_META:FRONT] if l == 0 else jnp.zeros(META_SHAPE, F32)
        small = _flat_cat([raw[n] for n, _ in SMALL_RAW] + [meta_g], small_len)
        rest = [in_sh.reshape(N_CHIPS, 2, D_MODEL // 2, cols), small.reshape(N_CHIPS, 2, -1, FLAT_W)]
        if pend is not None:
            totals[l + 1] = pend.drain()
        if early is None:
            pend = _Staged(_reduce_stages([rest[0]] + mats + [rest[1]], c_idx, tag), tag)
        else:
            t_in, t_small = _Staged(_reduce_stages(rest, c_idx, tag), tag).drain()
            totals[l] = [t_in] + list(early.drain()) + [t_small]

    grad_x = dh[FRONT:FRONT + seq][None]

    sm_all = _run_comm(_allgather_chips([totals[l][4] for l in range(DEPTH)]), "gather_small_grads")
    g_tot = {}
    per_layer = []
    for l in range(DEPTH):
        pieces = _split_flat(sm_all[l].reshape(-1), [s for _, s in SMALL_RAW] + [META_SHAPE])
        per_layer.append(dict(zip([n for n, _ in SMALL_RAW] + ["meta"], pieces)))
    stack = lambda f: jnp.stack([f(per_layer[l]) for l in range(DEPTH)])
    g_tot["meta_tokens"] = _my_shard(per_layer[0]["meta"], 1, chip)
    for n, key in (("norm_mix_pre", "g1"), ("norm_mix_post", "g2"), ("norm_mlp_pre", "g3"), ("norm_mlp_post", "g4"),
                   ("conv_b", "conv_b"), ("gla_head_norm", "head_norm")):
        g_tot[n] = stack(lambda d, key=key: d[key][0])
    g_tot["conv_w"] = _my_shard(stack(lambda d: d["conv_w"]), 2, chip)
    for o, n in enumerate(("lru_wa_f", "lru_wx_f", "lru_wa_b", "lru_wx_b")):
        g_tot[n] = stack(lambda d, o=o: _gate_diag_blocks(d["lru_wg"], o))
    for row, n in enumerate(("lru_ba_f", "lru_bx_f", "lru_ba_b", "lru_bx_b", "lru_lambda_f", "lru_lambda_b")):
        g_tot[n] = stack(lambda d, row=row: d["lru_pb"][row])
    g_tot["gla_wg_f"] = _my_shard(stack(lambda d: d["gla_wg"][0:GLA_RANK, 0:QK]), 2, chip)
    g_tot["gla_wg_b"] = _my_shard(stack(lambda d: d["gla_wg"][GLA_RANK:, QK:]), 2, chip)
    g_tot["gla_bg_f"] = stack(lambda d: d["gla_bg"][0, 0:QK])
    g_tot["gla_bg_b"] = stack(lambda d: d["gla_bg"][0, QK:])

    delta, new_m, new_v = {}, {}, {}
    for a, n in enumerate(BIG):
        shp = w_loc[n].shape
        flat3 = lambda t, shp=shp: t.reshape(DEPTH, -1, shp[-1])
        g_l = [totals[l][a].reshape(-1, shp[-1]) for l in range(DEPTH)]
        outs = _adamw_layers(flat3(w_loc[n]), flat3(m_loc[n]), flat3(v_loc[n]), g_l, "adamw_" + n)
        g_tot[n], delta[n], new_m[n], new_v[n] = [o.reshape(shp) for o in outs]
    outs = _adamw_small(*[[d[n] for n in SMALL] for d in (w_loc, g_tot, m_loc, v_loc)])
    for d, o in zip((delta, new_m, new_v), outs):
        d.update(zip(SMALL, o))
    return (loss, grad_x, *[g_tot[n] for n in WEIGHTS], *[delta[n] for n in WEIGHTS],
            *[new_m[n] for n in WEIGHTS], *[new_v[n] for n in WEIGHTS])
```

```python
import math

import jax
import jax.numpy as jnp
from jax import lax
from jax.experimental import pallas as pl
from jax.experimental.pallas import tpu as pltpu

F32 = jnp.float32
BF16 = jnp.bfloat16

D_MODEL = 1024
DEPTH = 2
N_META = 16
LRU_W = 512
LRU_HEADS = 8
LRU_HD = 64
LRU_C = 8.0
GLA_W = 512
GLA_H = 4
GLA_DK = 64
GLA_DV = 128
GLA_RANK = 16
GLA_GATE_NORM = 16.0
D_FF = 4096
D_IN = 2592
EPS = 1e-6
ADAM_LR, ADAM_B1, ADAM_B2, ADAM_EPS, ADAM_WD, ADAM_STEP = 0.001, 0.9, 0.999, 1e-08, 0.01, 10

LANES = 128
SUBLANES = 8
FRONT = 128
BACK = 128
D_IN_PAD = 2688
ZG_OFF = 2560
CHUNK = 64
EXP_CLAMP = 80.0
VMEM_LIMIT = 56 * 1024 * 1024
MESH_ID = pl.DeviceIdType.MESH

NN = (((1,), (0,)), ((), ()))
NT = (((1,), (1,)), ((), ()))
TN = (((0,), (0,)), ((), ()))


def _dot(a, b, dims=NN):
    return lax.dot_general(a, b, dims, preferred_element_type=F32)


def _cparams(sem):
    return pltpu.CompilerParams(dimension_semantics=sem, vmem_limit_bytes=VMEM_LIMIT)


def _sigmoid(x):
    return 1.0 / (1.0 + jnp.exp(-x))


def _gelu(x):
    c = math.sqrt(2.0 / math.pi)
    return 0.5 * x * (1.0 + jnp.tanh(c * (x + 0.044715 * x * x * x)))


def _gelu_grad(x):
    c = math.sqrt(2.0 / math.pi)
    t = jnp.tanh(c * (x + 0.044715 * x * x * x))
    return 0.5 * (1.0 + t) + 0.5 * x * (1.0 - t * t) * c * (1.0 + 3.0 * 0.044715 * x * x)


def _rms_fwd(x, g):
    rstd = lax.rsqrt(jnp.mean(x * x, axis=1, keepdims=True) + EPS)
    return (x * rstd) * g


def _rms_bwd(x, g, dy):
    rstd = lax.rsqrt(jnp.mean(x * x, axis=1, keepdims=True) + EPS)
    xh = x * rstd
    dxh = dy * g
    dx = rstd * (dxh - xh * jnp.mean(dxh * xh, axis=1, keepdims=True))
    dg = jnp.sum(dy * xh, axis=0, keepdims=True)
    return dx, dg


def _mm(a, b, mode, tm, tn, tk, outs, name, epilogue=None, extras=(), col_blocks_first=False,
        rows_in=(), consts=(), accs=(), comm=None):
    if mode == "nn":
        (m, k), n = a.shape, b.shape[1]
        a_spec = pl.BlockSpec((tm, tk), lambda i, j, kk: (i, kk))
        b_spec = pl.BlockSpec((tk, tn), lambda i, j, kk: (kk, j))
        dims = NN
    elif mode == "nt":
        (m, k), n = a.shape, b.shape[0]
        a_spec = pl.BlockSpec((tm, tk), lambda i, j, kk: (i, kk))
        b_spec = pl.BlockSpec((tn, tk), lambda i, j, kk: (j, kk))
        dims = NT
    else:
        (k, m), n = a.shape, b.shape[1]
        a_spec = pl.BlockSpec((tk, tm), lambda i, j, kk: (kk, i))
        b_spec = pl.BlockSpec((tk, tn), lambda i, j, kk: (kk, j))
        dims = TN
    assert m % tm == 0 and n % tn == 0 and k % tk == 0, (name, m, n, k)
    nk = k // tk
    if n == tn and nk == 1:
        b_spec = pl.BlockSpec(b_spec.block_shape, b_spec.index_map, pipeline_mode=pl.Buffered(1))
    ne = len(extras) + len(rows_in) + len(consts)
    e_specs = [pl.BlockSpec((tm, tn), lambda i, j, kk: (i, j)) for _ in extras]
    e_specs += [pl.BlockSpec((tm, r.shape[1]), lambda i, j, kk: (i, 0)) for r in rows_in]
    e_specs += [pl.BlockSpec(c.shape, lambda i, j, kk: (0, 0)) for c in consts]
    n_out = len(outs)
    o_specs, o_shapes = [], []
    for width, dtype in outs:
        if col_blocks_first:
            assert width == n, name
            o_specs.append(pl.BlockSpec((None, tm, tn), lambda i, j, kk: (j, i, 0)))
            o_shapes.append(jax.ShapeDtypeStruct((n // tn, m, tn), dtype))
            continue
        if width == n:
            o_specs.append(pl.BlockSpec((tm, tn), lambda i, j, kk: (i, j)))
        else:
            assert n == tn, name
            o_specs.append(pl.BlockSpec((tm, width), lambda i, j, kk: (i, 0)))
        o_shapes.append(jax.ShapeDtypeStruct((m, width), dtype))
    o_specs += [pl.BlockSpec(s, lambda i, j, kk: (0, 0)) for s, _ in accs]
    o_shapes += [jax.ShapeDtypeStruct(s, d) for s, d in accs]

    def finish(acc, extra_refs, out_refs):
        res = epilogue(acc, *[e[...] for e in extra_refs]) if epilogue else (acc,)
        for o, r in zip(out_refs[:n_out], res[:n_out]):
            o[...] = r.astype(o.dtype)
        first = jnp.logical_and(pl.program_id(0) == 0, pl.program_id(1) == 0)
        for o, r in zip(out_refs[n_out:], res[n_out:]):
            @pl.when(first)
            def _(o=o):
                o[...] = jnp.zeros_like(o)

            o[...] += r

    if nk == 1:
        def body(a_ref, b_ref, *rest):
            finish(_dot(a_ref[...], b_ref[...], dims), rest[:ne], rest[ne:])
        scratch = []
    else:
        def body(a_ref, b_ref, *rest):
            acc_ref = rest[-1]
            kk = pl.program_id(2)

            @pl.when(kk == 0)
            def _():
                acc_ref[...] = jnp.zeros_like(acc_ref)

            acc_ref[...] += _dot(a_ref[...], b_ref[...], dims)

            @pl.when(kk == nk - 1)
            def _():
                finish(acc_ref[...], rest[:ne], rest[ne:-1])
        scratch = [pltpu.VMEM((tm, tn), F32)]

    sem = ("arbitrary", "arbitrary", "arbitrary") if accs else ("parallel", "parallel", "arbitrary")
    res, comm_res = _pcall(body, name, (m // tm, n // tn, nk), [a_spec, b_spec] + e_specs, o_specs, o_shapes, scratch,
                           sem, (a, b, *extras, *rows_in, *consts), comm)
    return res if comm is None else (res, comm_res)


def _rows(body, n_rows, tm, ins, consts, outs, accs, name):
    assert n_rows % tm == 0, (name, n_rows, tm)
    in_specs = [pl.BlockSpec((tm, w), (lambda i, cb=cb: (i, cb))) for _, w, cb in ins]
    in_specs += [pl.BlockSpec(c.shape, lambda i: (0, 0)) for c in consts]
    out_specs = [pl.BlockSpec((tm, w), lambda i: (i, 0)) for w, _ in outs]
    out_specs += [pl.BlockSpec(s, lambda i: (0, 0)) for s, _ in accs]
    out_shape = [jax.ShapeDtypeStruct((n_rows, w), d) for w, d in outs]
    out_shape += [jax.ShapeDtypeStruct(s, d) for s, d in accs]
    ni, nc, no = len(ins), len(consts), len(outs)

    def kern(*refs):
        body(pl.program_id(0), refs[:ni], refs[ni:ni + nc], refs[ni + nc:ni + nc + no], refs[ni + nc + no:])

    res = pl.pallas_call(
        kern, name=name, grid=(n_rows // tm,), in_specs=in_specs, out_specs=out_specs, out_shape=out_shape,
        compiler_params=_cparams(("arbitrary",)),
    )(*[a for a, _, _ in ins], *consts)
    return res


def _acc_add(i, ref, val):
    @pl.when(i == 0)
    def _():
        ref[...] = jnp.zeros_like(ref)

    ref[...] += val


def _norm_cast(h, g, t_rows, name):
    def body(i, ins, consts, outs, accs):
        outs[0][...] = _rms_fwd(ins[0][...], consts[0][...]).astype(BF16)
    return _rows(body, t_rows, 384, [(h, D_MODEL, 0)], [g], [(D_MODEL, BF16)], [], name)[0]


def _head_norm_parts(o):
    ns, rs = [], []
    for hd in range(GLA_H):
        oh = o[:, hd * GLA_DV:(hd + 1) * GLA_DV]
        r = lax.rsqrt(jnp.mean(oh * oh, axis=1, keepdims=True) + EPS)
        ns.append(oh * r)
        rs.append(r)
    return ns, rs


def _mix_fwd(h_f, h_b, z_lru, o_f, o_b, z_go, head_norm, t_rows, name):
    def body(i, ins, consts, outs, accs):
        hf, hb, gate, of, ob, go = [r[...] for r in ins]
        hn = consts[0][...]
        outs[0][:, 0:LRU_W] = ((hf + hb) * _gelu(gate)).astype(BF16)
        ns, _ = _head_norm_parts(of + ob)
        silu = go * _sigmoid(go)
        for hd in range(GLA_H):
            sl = slice(hd * GLA_DV, (hd + 1) * GLA_DV)
            outs[0][:, LRU_W + hd * GLA_DV:LRU_W + (hd + 1) * GLA_DV] = (ns[hd] * hn[:, sl] * silu[:, sl]).astype(BF16)
    ins = [(h_f, LRU_W, 0), (h_b, LRU_W, 0), (z_lru, LRU_W, 1), (o_f, GLA_W, 0), (o_b, GLA_W, 0), (z_go, GLA_W, 0)]
    return _rows(body, t_rows, 384, ins, [head_norm], [(D_MODEL, BF16)], [], name)[0]


def _mix_bwd_tile(dymix, hf, hb, z_lru, of, ob, go, hn):
    dyl, dyg, gate = dymix[:, 0:LRU_W], dymix[:, LRU_W:], z_lru[:, LRU_W:]
    dh_lru = dyl * _gelu(gate)
    dgate = dyl * (hf + hb) * _gelu_grad(gate)
    ns, rs = _head_norm_parts(of + ob)
    sg = _sigmoid(go)
    silu = go * sg
    dsilu = sg * (1.0 + go * (1.0 - sg))
    d_o, d_go, dhn = [], [], []
    for hd in range(GLA_H):
        sl = slice(hd * GLA_DV, (hd + 1) * GLA_DV)
        dy = dyg[:, sl]
        e = dy * hn[:, sl] * silu[:, sl]
        d_o.append(rs[hd] * (e - ns[hd] * jnp.mean(e * ns[hd], axis=1, keepdims=True)))
        d_go.append(dy * ns[hd] * hn[:, sl] * dsilu[:, sl])
        dhn.append(jnp.sum(dy * ns[hd] * silu[:, sl], axis=0, keepdims=True))
    return dh_lru, dgate, jnp.concatenate(d_o, axis=1), jnp.concatenate(d_go, axis=1), jnp.concatenate(dhn, axis=1)


def _dz_assemble(dx_br, dgate, gla_grads, dgo, z_zg, wg, t_rows, name):
    dq_f, dk_f, dv_f, dpre_f, dq_b, dk_b, dv_b, dpre_b = gla_grads
    qk = GLA_H * GLA_DK

    def body(i, ins, consts, outs, accs):
        (dxb, dgt, dqf, dqb, dkf, dkb, dvf, dvb, dg_o, dpf, dpb, zg) = [r[...] for r in ins]
        w = consts[0][...]
        o = outs[0]
        o[:, 0:LRU_W] = dxb.astype(BF16)
        o[:, LRU_W:2 * LRU_W] = dgt.astype(BF16)
        o[:, 1024:1024 + qk] = ((dqf + dqb) * (GLA_DK ** -0.5)).astype(BF16)
        o[:, 1280:1280 + qk] = (dkf + dkb).astype(BF16)
        o[:, 1536:1536 + GLA_W] = (dvf + dvb).astype(BF16)
        o[:, 2048:2048 + GLA_W] = dg_o.astype(BF16)
        dpre = jnp.concatenate([dpf, dpb], axis=1)
        dpre16 = dpre.astype(BF16)
        o[:, ZG_OFF:ZG_OFF + LANES] = _dot(dpre16, w, NT).astype(BF16)
        _acc_add(i, accs[0], _dot(zg.astype(BF16), dpre16, TN))
        _acc_add(i, accs[1], jnp.sum(dpre, axis=0, keepdims=True))

    ins = [(dx_br, LRU_W, 0), (dgate, LRU_W, 0), (dq_f, qk, 0), (dq_b, qk, 0), (dk_f, qk, 0), (dk_b, qk, 0),
           (dv_f, GLA_W, 0), (dv_b, GLA_W, 0), (dgo, GLA_W, 0), (dpre_f, qk, 0), (dpre_b, qk, 0), (z_zg, LANES, 0)]
    return _rows(body, t_rows, 384, ins, [wg], [(D_IN_PAD, BF16)],
                 [((LANES, 2 * qk), F32), ((1, 2 * qk), F32)], name)


LRU_RB = 768
HALO = SUBLANES


def _scan8(a, u, rev):
    rows = lax.broadcasted_iota(jnp.int32, a.shape, 0)
    for d in (1, 2, 4):
        if rev:
            a_s, u_s, ok = pltpu.roll(a, SUBLANES - d, 0), pltpu.roll(u, SUBLANES - d, 0), rows < SUBLANES - d
        else:
            a_s, u_s, ok = pltpu.roll(a, d, 0), pltpu.roll(u, d, 0), rows >= d
        u = jnp.where(ok, a * u_s + u, u)
        a = jnp.where(ok, a * a_s, a)
    return a, u


def _edge_row(x, rev):
    r = x[0:1, :] if rev else x[SUBLANES - 1:SUBLANES, :]
    return jnp.broadcast_to(r, x.shape)


def _scan_block(a, u, carry, rev):
    nsub = a.shape[0] // SUBLANES
    loc = [_scan8(a[sb * SUBLANES:(sb + 1) * SUBLANES], u[sb * SUBLANES:(sb + 1) * SUBLANES], rev) for sb in range(nsub)]
    edges = [(_edge_row(a_c, rev), _edge_row(h_l, rev)) for a_c, h_l in loc]
    carries = [None] * nsub
    for sb in (range(nsub - 1, -1, -1) if rev else range(nsub)):
        carries[sb] = carry
        carry = edges[sb][0] * carry + edges[sb][1]
    h = jnp.concatenate([h_l + a_c * cin for (a_c, h_l), cin in zip(loc, carries)], axis=0)
    return h, carry


def _lru_gates(xc, pre_a, pre_x, ba, bx, sp8):
    r = _sigmoid(pre_a + ba)
    i = _sigmoid(pre_x + bx)
    log_a = -(r * sp8)
    a = jnp.exp(log_a)
    y = 2.0 * log_a
    series = -y * (1.0 + y * (0.5 + y * (1.0 / 6.0 + y * (1.0 / 24.0 + y * (1.0 / 120.0)))))
    om = jnp.where(y > -0.25, series, 1.0 - a * a)
    s = jnp.sqrt(om)
    return r, i, a, s


def _softplus(x):
    return jnp.maximum(x, 0.0) + jnp.log(1.0 + jnp.exp(-jnp.abs(x)))


def _conv_taps(xpad_ref, r0, nrows, shifts=(-2, -1, 0, 1)):
    ext = xpad_ref[pl.ds(r0, nrows + 2 * HALO), :]
    taps = []
    for s in shifts:
        sh = ext if s == 0 else pltpu.roll(ext, (-s) % (nrows + 2 * HALO), 0)
        taps.append(sh[HALO:HALO + nrows])
    return taps


def _row_mask(r0, nrows, seq):
    rows = r0 + lax.broadcasted_iota(jnp.int32, (nrows, LANES), 0)
    return jnp.logical_and(rows >= FRONT - N_META, rows < FRONT + seq).astype(F32)


def _lru_load_conv(z_hbm, xpad, xc, sem, cw_ref, cb_ref, t_rows, seq):
    j = pl.program_id(0)
    zeros = jnp.zeros((HALO, LANES), F32)
    xpad[0:HALO, :] = zeros
    xpad[t_rows + HALO:t_rows + 2 * HALO, :] = zeros
    cp = pltpu.make_async_copy(z_hbm.at[:, pl.ds(pl.multiple_of(j * LANES, LANES), LANES)],
                               xpad.at[pl.ds(HALO, t_rows)], sem)
    cp.start()
    cp.wait()
    cw = cw_ref[...]
    cb = cb_ref[...]

    def conv_blk(r, carry):
        r0 = pl.multiple_of(r * LRU_RB, LRU_RB)
        taps = _conv_taps(xpad, r0, LRU_RB)
        acc = cb + cw[0:1, :] * taps[0]
        for k in range(1, 4):
            acc = acc + cw[k:k + 1, :] * taps[k]
        xc[pl.ds(r0, LRU_RB), :] = acc * _row_mask(r0, LRU_RB, seq)
        return carry

    lax.fori_loop(0, t_rows // LRU_RB, conv_blk, 0)


def _lru_specs(t_rows):
    return [pl.BlockSpec(memory_space=pl.ANY),
            pl.BlockSpec((4, LANES), lambda j: (0, j)),
            pl.BlockSpec((1, LANES), lambda j: (0, j)),
            pl.BlockSpec((1, LANES, 4 * LANES), lambda j: (j, 0, 0)),
            pl.BlockSpec((SUBLANES, LANES), lambda j: (0, j))]


def _lru_fwd(z_lru, conv_w, conv_b, wg, pb, t_rows, seq, name, comm=None):
    nblk = t_rows // LRU_RB

    def kern(z_hbm, cw_ref, cb_ref, wg_ref, pb_ref, hf_ref, hb_ref, xpad, xc, sem):
        _lru_load_conv(z_hbm, xpad, xc, sem, cw_ref, cb_ref, t_rows, seq)
        w = wg_ref[0]
        pb_v = pb_ref[...]
        dirs = []
        for rev in (False, True):
            o = 2 if rev else 0
            dirs.append(dict(rev=rev, ba=pb_v[o:o + 1, :], bx=pb_v[o + 1:o + 2, :],
                             sp8=LRU_C * _softplus(-pb_v[5:6, :] if rev else -pb_v[4:5, :]),
                             w=w[:, o * LANES:(o + 2) * LANES], out=hb_ref if rev else hf_ref))

        def blk(it, carries):
            new = []
            for d, carry in zip(dirs, carries):
                r = (nblk - 1 - it) if d["rev"] else it
                r0 = pl.multiple_of(r * LRU_RB, LRU_RB)
                xcb = xc[pl.ds(r0, LRU_RB), :]
                pre = _dot(xcb.astype(BF16), d["w"])
                _, gi, a, s = _lru_gates(xcb, pre[:, 0:LANES], pre[:, LANES:2 * LANES], d["ba"], d["bx"], d["sp8"])
                h, carry = _scan_block(a, s * (gi * xcb), carry, d["rev"])
                d["out"][pl.ds(r0, LRU_RB), :] = h
                new.append(carry)
            return tuple(new)

        z8 = jnp.zeros((SUBLANES, LANES), F32)
        lax.fori_loop(0, nblk, blk, (z8, z8), unroll=2)

    res, comm_res = _pcall(
        kern, name, (LRU_W // LANES,), _lru_specs(t_rows), [pl.BlockSpec((t_rows, LANES), lambda j: (0, j))] * 2,
        [jax.ShapeDtypeStruct((t_rows, LRU_W), F32)] * 2,
        [pltpu.VMEM((t_rows + 2 * HALO, LANES), F32), pltpu.VMEM((t_rows, LANES), F32), pltpu.SemaphoreType.DMA],
        ("arbitrary",), (z_lru, conv_w, conv_b, wg, pb), comm)
    return res if comm is None else (res, comm_res)


def _lru_bwd(z_lru, conv_w, conv_b, wg, pb, h_f, h_b, dh, t_rows, seq, name, comm=None):
    nblk = t_rows // LRU_RB

    def kern(z_hbm, cw_ref, cb_ref, wg_ref, pb_ref, hf_ref, hb_ref, dh_ref,
             dx_ref, dcw_ref, dcb_ref, dwh_ref, dpb_ref, xpad, xc, dxc, dwg, sem):
        _lru_load_conv(z_hbm, xpad, xc, sem, cw_ref, cb_ref, t_rows, seq)
        w = wg_ref[0]
        pb_v = pb_ref[...]
        zeros = jnp.zeros((HALO, LANES), F32)
        dxc[0:HALO, :] = zeros
        dxc[t_rows + HALO:t_rows + 2 * HALO, :] = zeros
        dwg[...] = jnp.zeros_like(dwg)

        def zero_blk(r, carry):
            dxc[pl.ds(pl.multiple_of(r * LRU_RB, LRU_RB) + HALO, LRU_RB), :] = jnp.zeros((LRU_RB, LANES), F32)
            return carry

        lax.fori_loop(0, nblk, zero_blk, 0)
        dirs = []
        for rev in (False, True):
            o = 2 if rev else 0
            lam = pb_v[5:6, :] if rev else pb_v[4:5, :]
            dirs.append(dict(rev=rev, o=o, ba=pb_v[o:o + 1, :], bx=pb_v[o + 1:o + 2, :], sp8=LRU_C * _softplus(-lam),
                             dlam_scale=LRU_C * _sigmoid(-lam), h_ref=hb_ref if rev else hf_ref,
                             w=w[:, o * LANES:(o + 2) * LANES]))
        rows_b = lax.broadcasted_iota(jnp.int32, (LRU_RB, LANES), 0)

        def blk(it, carries):
            new = []
            for d, (nu_c, acc_ba, acc_bx, acc_lam) in zip(dirs, carries):
                rev, o, h_ref = d["rev"], d["o"], d["h_ref"]
                adj_rev = not rev
                r = (nblk - 1 - it) if adj_rev else it
                r0 = pl.multiple_of(r * LRU_RB, LRU_RB)
                xcb = xc[pl.ds(r0, LRU_RB), :]
                xc16 = xcb.astype(BF16)
                pre = _dot(xc16, d["w"])
                gr, gi, a, s = _lru_gates(xcb, pre[:, 0:LANES], pre[:, LANES:2 * LANES], d["ba"], d["bx"], d["sp8"])
                dhb = dh_ref[pl.ds(r0, LRU_RB), :]
                hb = h_ref[pl.ds(r0, LRU_RB), :]
                if rev:
                    nxt0 = pl.multiple_of(jnp.minimum(r0 + LRU_RB, t_rows - SUBLANES), SUBLANES)
                    edge = h_ref[pl.ds(nxt0, SUBLANES), :][0:1, :] * (r < nblk - 1).astype(F32)
                    h_prev = jnp.concatenate([hb[1:], edge], axis=0)
                else:
                    prv0 = pl.multiple_of(jnp.maximum(r0 - SUBLANES, 0), SUBLANES)
                    edge = h_ref[pl.ds(prv0, SUBLANES), :][SUBLANES - 1:SUBLANES, :] * (r > 0).astype(F32)
                    h_prev = jnp.concatenate([edge, hb[:-1]], axis=0)
                nu, nu_out = _scan_block(a, a * dhb, nu_c, adj_rev)
                cin = jnp.broadcast_to(nu_c[0:1, :], (LRU_RB, LANES))
                if adj_rev:
                    nu_next = jnp.where(rows_b == LRU_RB - 1, cin, pltpu.roll(nu, LRU_RB - 1, 0))
                else:
                    nu_next = jnp.where(rows_b == 0, cin, pltpu.roll(nu, 1, 0))
                du = dhb + nu_next
                da = du * h_prev
                ix = gi * xcb
                dlog = da * a - du * ix * (a * a) / s
                d_i = du * s * xcb
                dr = -dlog * d["sp8"]
                dpa = dr * gr * (1.0 - gr)
                dpx = d_i * gi * (1.0 - gi)
                dp16 = jnp.concatenate([dpa, dpx], axis=1).astype(BF16)
                dxc[pl.ds(r0 + HALO, LRU_RB), :] += du * s * gi + _dot(dp16, d["w"], NT)
                dwg[:, o * LANES:(o + 2) * LANES] += _dot(xc16, dp16, TN)
                new.append((nu_out, acc_ba + jnp.sum(dpa, axis=0, keepdims=True),
                            acc_bx + jnp.sum(dpx, axis=0, keepdims=True),
                            acc_lam + jnp.sum(dlog * gr, axis=0, keepdims=True)))
            return tuple(new)

        z1 = jnp.zeros((1, LANES), F32)
        init = (jnp.zeros((SUBLANES, LANES), F32), z1, z1, z1)
        (_, ba_f, bx_f, lam_f), (_, ba_b, bx_b, lam_b) = lax.fori_loop(0, nblk, blk, (init, init), unroll=2)
        dpb_ref[...] = jnp.concatenate([ba_f, bx_f, ba_b, bx_b, lam_f * dirs[0]["dlam_scale"],
                                        lam_b * dirs[1]["dlam_scale"], z1, z1], axis=0)
        for hh in range(2):
            for o in range(4):
                c0 = o * LANES + hh * LRU_HD
                dwh_ref[hh, :, o * LRU_HD:(o + 1) * LRU_HD] = dwg[hh * LRU_HD:(hh + 1) * LRU_HD, c0:c0 + LRU_HD]

        def mask_blk(r, carry):
            r0 = pl.multiple_of(r * LRU_RB, LRU_RB)
            dxc[pl.ds(r0 + HALO, LRU_RB), :] = dxc[pl.ds(r0 + HALO, LRU_RB), :] * _row_mask(r0, LRU_RB, seq)
            return carry

        lax.fori_loop(0, nblk, mask_blk, 0)

        cw = cw_ref[...]

        def conv_bwd(r, carry):
            r0 = pl.multiple_of(r * LRU_RB, LRU_RB)
            gt = _conv_taps(dxc, r0, LRU_RB, (2, 1, 0, -1))
            xt = _conv_taps(xpad, r0, LRU_RB)
            dx_ref[pl.ds(r0, LRU_RB), :] = (cw[0:1, :] * gt[0] + cw[1:2, :] * gt[1] + cw[2:3, :] * gt[2]
                                           + cw[3:4, :] * gt[3])
            g = gt[2]
            new = [carry[k] + jnp.sum(g * xt[k], axis=0, keepdims=True) for k in range(4)]
            new.append(carry[4] + jnp.sum(g, axis=0, keepdims=True))
            return tuple(new)

        z1 = jnp.zeros((1, LANES), F32)
        sums = lax.fori_loop(0, nblk, conv_bwd, (z1, z1, z1, z1, z1))
        dcw_ref[...] = jnp.concatenate(sums[:4], axis=0)
        dcb_ref[...] = sums[4]

    col = lambda j: (0, j)
    res, comm_res = _pcall(
        kern, name, (LRU_W // LANES,), _lru_specs(t_rows) + [pl.BlockSpec((t_rows, LANES), col)] * 3,
        [pl.BlockSpec((t_rows, LANES), col), pl.BlockSpec((4, LANES), col), pl.BlockSpec((1, LANES), col),
         pl.BlockSpec((2, LRU_HD, 4 * LRU_HD), lambda j: (j, 0, 0)), pl.BlockSpec((SUBLANES, LANES), col)],
        [jax.ShapeDtypeStruct((t_rows, LRU_W), F32), jax.ShapeDtypeStruct((4, LRU_W), F32),
         jax.ShapeDtypeStruct((1, LRU_W), F32), jax.ShapeDtypeStruct((LRU_HEADS, LRU_HD, 4 * LRU_HD), F32),
         jax.ShapeDtypeStruct((SUBLANES, LRU_W), F32)],
        [pltpu.VMEM((t_rows + 2 * HALO, LANES), F32), pltpu.VMEM((t_rows, LANES), F32),
         pltpu.VMEM((t_rows + 2 * HALO, LANES), F32), pltpu.VMEM((LANES, 4 * LANES), F32), pltpu.SemaphoreType.DMA],
        ("arbitrary",), (z_lru, conv_w, conv_b, wg, pb, h_f, h_b, dh), comm)
    return res if comm is None else (res, comm_res)


QK = GLA_H * GLA_DK
GLA_G = 6


def _cumsum_rows(x, rev):
    n = x.shape[0]
    rows = lax.broadcasted_iota(jnp.int32, x.shape, 0)
    d = 1
    while d < n:
        if rev:
            x = x + jnp.where(rows < n - d, pltpu.roll(x, n - d, 0), 0.0)
        else:
            x = x + jnp.where(rows >= d, pltpu.roll(x, d, 0), 0.0)
        d *= 2
    return x


def _gla_chunk_terms(q, k, zg, wg, bg, rev):
    pre = (_dot(zg.astype(BF16), wg) + bg)[:, (QK if rev else 0):(2 * QK if rev else QK)]
    g = (jnp.minimum(pre, 0.0) - jnp.log(1.0 + jnp.exp(-jnp.abs(pre)))) * (1.0 / GLA_GATE_NORM)
    b = _cumsum_rows(g, rev)
    b_last = b[0:1, :] if rev else b[CHUNK - 1:CHUNK, :]
    b_mid = b[CHUNK // 2:CHUNK // 2 + 1, :]
    qs = q * (GLA_DK ** -0.5)
    terms = dict(
        pre=pre, qs=qs, k=k,
        eb=jnp.exp(b), ek=jnp.exp(b_last - b), e_last=jnp.exp(b_last),
        eqm=jnp.exp(jnp.minimum(b - b_mid, EXP_CLAMP)), ekm=jnp.exp(jnp.minimum(b_mid - b, EXP_CLAMP)))
    return terms


def _gla_mask(rev):
    t = lax.broadcasted_iota(jnp.int32, (CHUNK, CHUNK), 0)
    s = lax.broadcasted_iota(jnp.int32, (CHUNK, CHUNK), 1)
    return (s > t) if rev else (s <= t)


def _gla_head_ops(tm, hd):
    sl = slice(hd * GLA_DK, (hd + 1) * GLA_DK)
    q_b = (tm["qs"][:, sl] * tm["eb"][:, sl]).astype(BF16)
    k_e = (tm["k"][:, sl] * tm["ek"][:, sl]).astype(BF16)
    q_m = (tm["qs"][:, sl] * tm["eqm"][:, sl]).astype(BF16)
    k_m = (tm["k"][:, sl] * tm["ekm"][:, sl]).astype(BF16)
    return sl, q_b, k_e, q_m, k_m


def _gla_fwd(z_qk, z_v, z_zg, wg, bg, t_rows, name, comm=None):
    nc = t_rows // CHUNK
    ng = nc // GLA_G
    rows = CHUNK * GLA_G

    def kern(qf, kf, vf, zf, qb, kb, vb, zb, wg_ref, bg_ref, of_ref, ob_ref, sf_ref, sb_ref, st_f, st_b):
        i = pl.program_id(0)

        @pl.when(i == 0)
        def _():
            st_f[...] = jnp.zeros_like(st_f)
            st_b[...] = jnp.zeros_like(st_b)

        wg_v, bg_v = wg_ref[...], bg_ref[...]
        for rev, (q_r, k_r, v_r, z_r, o_r, s_r, st) in ((False, (qf, kf, vf, zf, of_ref, sf_ref, st_f)),
                                                        (True, (qb, kb, vb, zb, ob_ref, sb_ref, st_b))):
            mask = _gla_mask(rev)
            for ck in (range(GLA_G - 1, -1, -1) if rev else range(GLA_G)):
                cr = slice(ck * CHUNK, (ck + 1) * CHUNK)
                tm = _gla_chunk_terms(q_r[cr, :], k_r[cr, :], z_r[cr, :], wg_v, bg_v, rev)
                v = v_r[cr, :]
                for hd in range(GLA_H):
                    sl, q_b, k_e, q_m, k_m = _gla_head_ops(tm, hd)
                    vs = slice(hd * GLA_DV, (hd + 1) * GLA_DV)
                    v16 = v[:, vs].astype(BF16)
                    a = jnp.where(mask, _dot(q_m, k_m, NT), 0.0).astype(BF16)
                    s_in = st[hd]
                    s_r[ck, hd] = s_in
                    o_r[cr, vs] = _dot(a, v16) + _dot(q_b, s_in.astype(BF16), NT)
                    st[hd] = s_in * tm["e_last"][:, sl] + _dot(v16, k_e, TN)

    fw = lambda c: (lambda i: (i, c))
    rv = lambda c: (lambda i: (ng - 1 - i, c))
    def side(ix):
        return [pl.BlockSpec((rows, QK), ix(0)), pl.BlockSpec((rows, QK), ix(1)),
                pl.BlockSpec((rows, GLA_W), ix(0)), pl.BlockSpec((rows, LANES), ix(0))]
    st_shape = (nc, GLA_H, GLA_DV, GLA_DK)
    res, comm_res = _pcall(
        kern, name, (ng,),
        side(fw) + side(rv) + [pl.BlockSpec(wg.shape, lambda i: (0, 0)), pl.BlockSpec(bg.shape, lambda i: (0, 0))],
        [pl.BlockSpec((rows, GLA_W), fw(0)), pl.BlockSpec((rows, GLA_W), rv(0)),
         pl.BlockSpec((GLA_G,) + st_shape[1:], lambda i: (i, 0, 0, 0)),
         pl.BlockSpec((GLA_G,) + st_shape[1:], lambda i: (ng - 1 - i, 0, 0, 0))],
        [jax.ShapeDtypeStruct((t_rows, GLA_W), F32)] * 2 + [jax.ShapeDtypeStruct(st_shape, F32)] * 2,
        [pltpu.VMEM(st_shape[1:], F32)] * 2, ("arbitrary",),
        (z_qk, z_qk, z_v, z_zg, z_qk, z_qk, z_v, z_zg, wg, bg), comm)
    return res if comm is None else (res, comm_res)


def _gla_bwd(z_qk, z_v, z_zg, wg, bg, do, s_f, s_b, t_rows, name, comm=None):
    nc = t_rows // CHUNK

    def kern(qf, kf, vf, zf, dof, ssf, qb, kb, vb, zb, dob, ssb, wg_ref, bg_ref,
             dqf, dkf, dvf, dpf, dqb, dkb, dvb, dpb, ds_f, ds_b):
        i = pl.program_id(0)

        @pl.when(i == 0)
        def _():
            ds_f[...] = jnp.zeros_like(ds_f)
            ds_b[...] = jnp.zeros_like(ds_b)

        wg_v, bg_v = wg_ref[...], bg_ref[...]
        sides = ((False, (qf, kf, vf, zf, dof, ssf, dqf, dkf, dvf, dpf, ds_f)),
                 (True, (qb, kb, vb, zb, dob, ssb, dqb, dkb, dvb, dpb, ds_b)))
        for rev, (q_r, k_r, v_r, z_r, do_r, ss_r, dq_r, dk_r, dv_r, dp_r, ds) in sides:
            mask = _gla_mask(rev)
            for ck in (range(GLA_G) if rev else range(GLA_G - 1, -1, -1)):
                cr = slice(ck * CHUNK, (ck + 1) * CHUNK)
                tm = _gla_chunk_terms(q_r[cr, :], k_r[cr, :], z_r[cr, :], wg_v, bg_v, rev)
                v = v_r[cr, :]
                d_o = do_r[cr, :]
                db_parts, cross_parts = [], []
                for hd in range(GLA_H):
                    sl, q_b, k_e, q_m, k_m = _gla_head_ops(tm, hd)
                    vs = slice(hd * GLA_DV, (hd + 1) * GLA_DV)
                    v16, do16 = v[:, vs].astype(BF16), d_o[:, vs].astype(BF16)
                    a = jnp.where(mask, _dot(q_m, k_m, NT), 0.0).astype(BF16)
                    da = jnp.where(mask, _dot(do16, v16, NT), 0.0).astype(BF16)
                    s_in = ss_r[ck, hd]
                    s_in16 = s_in.astype(BF16)
                    ds_out = ds[hd]
                    ds16 = ds_out.astype(BF16)
                    dv_r[cr, vs] = _dot(a, do16, TN) + _dot(k_e, ds16, NT)
                    dqs = _dot(da, k_m) * tm["eqm"][:, sl] + _dot(do16, s_in16) * tm["eb"][:, sl]
                    dk = _dot(da, q_m, TN) * tm["ekm"][:, sl] + _dot(v16, ds16) * tm["ek"][:, sl]
                    ds[hd] = ds_out * tm["e_last"][:, sl] + _dot(do16, q_b, TN)
                    dq_r[cr, sl] = dqs
                    dk_r[cr, sl] = dk
                    db_parts.append(tm["qs"][:, sl] * dqs - tm["k"][:, sl] * dk)
                    s_out = s_in * tm["e_last"][:, sl] + _dot(v16, k_e, TN)
                    cross_parts.append(jnp.sum(s_out * ds_out, axis=0, keepdims=True))
                d_b = jnp.concatenate(db_parts, axis=1)
                dg = _cumsum_rows(d_b, not rev) + jnp.concatenate(cross_parts, axis=1)
                dp_r[cr, :] = dg * (1.0 / GLA_GATE_NORM) * _sigmoid(-tm["pre"])

    ng = nc // GLA_G
    rows = CHUNK * GLA_G
    fw = lambda c: (lambda i: (ng - 1 - i, c))
    rv = lambda c: (lambda i: (i, c))
    st_blk = (GLA_G, GLA_H, GLA_DV, GLA_DK)
    def side(ix, st_ix):
        return [pl.BlockSpec((rows, QK), ix(0)), pl.BlockSpec((rows, QK), ix(1)),
                pl.BlockSpec((rows, GLA_W), ix(0)), pl.BlockSpec((rows, LANES), ix(0)),
                pl.BlockSpec((rows, GLA_W), ix(0)), pl.BlockSpec(st_blk, st_ix)]
    def oside(ix):
        return [pl.BlockSpec((rows, QK), ix(0)), pl.BlockSpec((rows, QK), ix(0)),
                pl.BlockSpec((rows, GLA_W), ix(0)), pl.BlockSpec((rows, QK), ix(0))]
    o_shapes = [jax.ShapeDtypeStruct((t_rows, QK), F32), jax.ShapeDtypeStruct((t_rows, QK), F32),
                jax.ShapeDtypeStruct((t_rows, GLA_W), F32), jax.ShapeDtypeStruct((t_rows, QK), F32)]
    res, comm_res = _pcall(
        kern, name, (ng,),
        (side(fw, lambda i: (ng - 1 - i, 0, 0, 0)) + side(rv, lambda i: (i, 0, 0, 0))
         + [pl.BlockSpec(wg.shape, lambda i: (0, 0)), pl.BlockSpec(bg.shape, lambda i: (0, 0))]),
        oside(fw) + oside(rv), o_shapes * 2, [pltpu.VMEM((GLA_H, GLA_DV, GLA_DK), F32)] * 2, ("arbitrary",),
        (z_qk, z_qk, z_v, z_zg, do, s_f, z_qk, z_qk, z_v, z_zg, do, s_b, wg, bg), comm)
    return res if comm is None else (res, comm_res)


FLAT_W = 1024


def _adam_math(wv, gv, mv, vv):
    bc1 = 1.0 - ADAM_B1 ** ADAM_STEP
    bc2 = 1.0 - ADAM_B2 ** ADAM_STEP
    m_new = ADAM_B1 * mv + (1.0 - ADAM_B1) * gv
    v_new = ADAM_B2 * vv + (1.0 - ADAM_B2) * (gv * gv)
    delta = -ADAM_LR * ((m_new / bc1) / (jnp.sqrt(v_new / bc2) + ADAM_EPS) + ADAM_WD * wv)
    return delta, m_new, v_new


def _row_tile(rows, cap=256):
    t = cap
    while rows % t:
        t //= 2
    assert t >= SUBLANES, rows
    return t


def _adamw_layers(w, m, v, g_layers, name, comm=None):
    depth, rows, cols = w.shape
    tr = _row_tile(rows)

    def kern(w_ref, m_ref, v_ref, *rest):
        g_refs, (go_ref, d_ref, mo_ref, vo_ref) = rest[:depth], rest[depth:]
        l = pl.program_id(0)
        gv = g_refs[0][...]
        for k in range(1, depth):
            gv = jnp.where(l == k, g_refs[k][...], gv)
        delta, m_new, v_new = _adam_math(w_ref[...], gv, m_ref[...], v_ref[...])
        go_ref[...] = gv
        d_ref[...] = delta
        mo_ref[...] = m_new
        vo_ref[...] = v_new

    stacked = pl.BlockSpec((None, tr, cols), lambda l, i: (l, i, 0))
    res, comm_res = _pcall(
        kern, name, (depth, rows // tr), [stacked] * 3 + [pl.BlockSpec((tr, cols), lambda l, i: (i, 0))] * depth,
        [stacked] * 4, [jax.ShapeDtypeStruct(w.shape, F32)] * 4, [], ("arbitrary", "arbitrary"),
        (w, m, v, *g_layers), comm)
    return res if comm is None else (res, comm_res)


def _adamw_small(ws, gs, ms, vs):
    n = len(ws)

    def kern(*refs):
        ins, outs = refs[:4 * n], refs[4 * n:]
        for k in range(n):
            delta, m_new, v_new = _adam_math(ins[k][...], ins[n + k][...], ins[2 * n + k][...], ins[3 * n + k][...])
            outs[k][...] = delta
            outs[n + k][...] = m_new
            outs[2 * n + k][...] = v_new

    shapes = [jax.ShapeDtypeStruct(w.shape, F32) for w in ws]
    res = pl.pallas_call(kern, name="adamw_small", out_shape=shapes * 3,
                         compiler_params=pltpu.CompilerParams(vmem_limit_bytes=VMEM_LIMIT))(*ws, *gs, *ms, *vs)
    return res[:n], res[n:2 * n], res[2 * n:]


def _add_sibling(g, got, c_idx, name):
    _, _, rows, cols = g.shape
    tr = _row_tile(rows)

    def kern(c_ref, g_ref, got_ref, o_ref):
        o_ref[...] = (g_ref[...] + got_ref[...]).astype(BF16)

    grid_spec = pltpu.PrefetchScalarGridSpec(
        num_scalar_prefetch=1, grid=(N_CHIPS, rows // tr),
        in_specs=[pl.BlockSpec((None, None, tr, cols), lambda q, i, c_ref: (q, c_ref[0], i, 0)),
                  pl.BlockSpec((None, tr, cols), lambda q, i, c_ref: (q, i, 0))],
        out_specs=pl.BlockSpec((None, tr, cols), lambda q, i, c_ref: (q, i, 0)))
    return pl.pallas_call(
        kern, name=name, grid_spec=grid_spec, out_shape=jax.ShapeDtypeStruct((N_CHIPS, rows, cols), BF16),
        compiler_params=_cparams(("arbitrary", "arbitrary")),
    )(c_idx, g, got)


def _sum_chips(x, name):
    _, rows, cols = x.shape
    tr = _row_tile(rows)

    def kern(x_ref, o_ref):
        xv = x_ref[...].astype(F32)
        o_ref[...] = ((xv[0] + xv[1]) + xv[2]) + xv[3]

    return pl.pallas_call(
        kern, name=name, grid=(rows // tr,),
        in_specs=[pl.BlockSpec((N_CHIPS, tr, cols), lambda i: (0, i, 0))],
        out_specs=pl.BlockSpec((tr, cols), lambda i: (i, 0)),
        out_shape=jax.ShapeDtypeStruct((rows, cols), F32),
        compiler_params=_cparams(("arbitrary",)),
    )(x)


def _place():
    x, y, c = lax.axis_index("x"), lax.axis_index("y"), lax.axis_index("c")
    return x, y, c


def _other_chips(x, y):
    return [(1 - x, y), (x, 1 - y), (1 - x, 1 - y)]


class _Comm:
    def __init__(self, ins, out_shapes, n_sems, stage, body):
        self.ins, self.out_shapes, self.body = list(ins), list(out_shapes), body
        self.scratch = [pltpu.SemaphoreType.DMA((n,)) for n in n_sems] + [pltpu.VMEM(s, d) for s, d in stage]


ANY_SPEC = pl.BlockSpec(memory_space=pl.ANY)


def _run_comm(comm, name):
    def kern(*refs):
        comm.body(refs, True, True)

    return pl.pallas_call(
        kern, name=name, in_specs=[ANY_SPEC] * len(comm.ins), out_specs=[ANY_SPEC] * len(comm.out_shapes),
        out_shape=comm.out_shapes, scratch_shapes=comm.scratch,
        compiler_params=pltpu.CompilerParams(has_side_effects=True, vmem_limit_bytes=VMEM_LIMIT),
    )(*comm.ins)


def _pcall(kern, name, grid, in_specs, out_specs, out_shape, scratch_shapes, sem, args, comm=None):
    if comm is None:
        return pl.pallas_call(kern, name=name, grid=grid, in_specs=in_specs, out_specs=out_specs, out_shape=out_shape,
                              scratch_shapes=scratch_shapes, compiler_params=_cparams(sem))(*args), None
    n_in, n_out, n_scr = len(in_specs), len(out_specs), len(scratch_shapes)
    c_in, c_out = len(comm.ins), len(comm.out_shapes)

    def hosted(*refs):
        a = n_in + c_in
        b = a + n_out + c_out
        main = refs[:n_in] + refs[a:a + n_out] + refs[b:b + n_scr]
        extra = refs[n_in:a] + refs[a + n_out:b] + refs[b + n_scr:]
        ids = [pl.program_id(d) for d in range(len(grid))]
        first, last = ids[0] == 0, ids[0] == grid[0] - 1
        for i, g in zip(ids[1:], grid[1:]):
            first, last = jnp.logical_and(first, i == 0), jnp.logical_and(last, i == g - 1)

        @pl.when(first)
        def _():
            comm.body(extra, True, False)

        kern(*main)

        @pl.when(last)
        def _():
            comm.body(extra, False, True)

    res = pl.pallas_call(
        hosted, name=name, grid=grid, in_specs=list(in_specs) + [ANY_SPEC] * c_in,
        out_specs=list(out_specs) + [ANY_SPEC] * c_out, out_shape=list(out_shape) + comm.out_shapes,
        scratch_shapes=list(scratch_shapes) + comm.scratch,
        compiler_params=pltpu.CompilerParams(dimension_semantics=("arbitrary",) * len(grid),
                                             vmem_limit_bytes=VMEM_LIMIT, has_side_effects=True),
    )(*args, *comm.ins)
    return res[:n_out], res[n_out:]


class _StagedCopies:
    def __init__(self, pairs, bufs, sem_in, sem_out):
        self.ins = [pltpu.make_async_copy(src, buf, sem_in.at[i]) for i, ((src, _), buf) in enumerate(zip(pairs, bufs))]
        self.outs = [pltpu.make_async_copy(buf, dst, sem_out.at[i]) for i, ((_, dst), buf) in enumerate(zip(pairs, bufs))]

    def load(self):
        for cp in self.ins:
            cp.start()

    def store(self):
        for cp in self.ins:
            cp.wait()
        for cp in self.outs:
            cp.start()

    def finish(self):
        for cp in self.outs:
            cp.wait()


DMA_CHUNK_BYTES = 1 << 20
DMA_MAX_CHUNKS = 4


def _row_chunks(rows, row_bytes, align=16):
    units = rows // align
    k = max(1, min(DMA_MAX_CHUNKS, -(-rows * row_bytes // DMA_CHUNK_BYTES), units))
    out, start = [], 0
    for i in range(k):
        size = (units // k + (1 if i < units % k else 0)) * align
        out.append((start, size))
        start += size
    if start < rows:
        out[-1] = (out[-1][0], out[-1][1] + rows - start)
    return out


def _row_bytes(shape, dtype):
    return math.prod(shape[1:]) * jnp.dtype(dtype).itemsize


def _remote(src, dst, send, recv, idx, dev):
    return pltpu.make_async_remote_copy(src, dst, send.at[idx], recv.at[idx], device_id=dev, device_id_type=MESH_ID)


def _allgather_chips(xs):
    na = len(xs)
    chunks = [_row_chunks(x.shape[1], _row_bytes(x.shape[1:], x.dtype)) for x in xs]
    n_cp = 3 * sum(len(ch) for ch in chunks)

    def body(refs, start, finish):
        x_refs, o_refs = refs[:na], refs[na:2 * na]
        send1, recv1, send2, recv2, loc_in, loc_out = refs[2 * na:2 * na + 6]
        px, py, c = _place()
        me = 2 * px + py
        sib = (px, py, 1 - c)
        own = _StagedCopies([(x, o.at[me]) for x, o in zip(x_refs, o_refs)], refs[2 * na + 6:], loc_in, loc_out)
        plan = [(a, qx, qy, pl.ds(s, n)) for a in range(na) for (qx, qy) in _other_chips(px, py) for (s, n) in chunks[a]]
        first = [_remote(x_refs[a].at[c, rows], o_refs[a].at[me, c, rows], send1, recv1, i, (qx, qy, c))
                 for i, (a, qx, qy, rows) in enumerate(plan)]
        if start:
            own.load()
            for cp in first:
                cp.start()
        if finish:
            own.store()
            passed = []
            for i, (a, qx, qy, rows) in enumerate(plan):
                slot = o_refs[a].at[2 * qx + qy, c, rows]
                _remote(slot, slot, send1, recv1, i, (qx, qy, c)).wait_recv()
                fwd = _remote(slot, slot, send2, recv2, i, sib)
                fwd.start()
                passed.append(fwd)
            for i, (a, qx, qy, rows) in enumerate(plan):
                slot = o_refs[a].at[2 * qx + qy, 1 - c, rows]
                _remote(slot, slot, send2, recv2, i, sib).wait_recv()
            for cp in first + passed:
                cp.wait_send()
            own.finish()

    outs = [jax.ShapeDtypeStruct((N_CHIPS,) + x.shape, x.dtype) for x in xs]
    return _Comm(xs, outs, [n_cp, n_cp, n_cp, n_cp, na, na], [(x.shape, x.dtype) for x in xs], body)


def _sibling_halves(gs):
    na = len(gs)
    chunks = [_row_chunks(g.shape[2], _row_bytes(g.shape[2:], g.dtype)) for g in gs]
    n_cp = N_CHIPS * sum(len(ch) for ch in chunks)

    def body(refs, start, finish):
        g_refs, o_refs = refs[:na], refs[na:2 * na]
        send, recv = refs[2 * na:]
        px, py, c = _place()
        plan = [(a, q, pl.ds(s, n)) for a in range(na) for q in range(N_CHIPS) for (s, n) in chunks[a]]
        cps = [_remote(g_refs[a].at[q, 1 - c, rows], o_refs[a].at[q, rows], send, recv, i, (px, py, 1 - c))
               for i, (a, q, rows) in enumerate(plan)]
        if start:
            for cp in cps:
                cp.start()
        if finish:
            for cp in cps:
                cp.wait()

    outs = [jax.ShapeDtypeStruct((N_CHIPS,) + g.shape[2:], g.dtype) for g in gs]
    return _Comm(gs, outs, [n_cp, n_cp], [], body)


def _chip_exchange(ps):
    na = len(ps)
    chunks = [_row_chunks(p.shape[1], _row_bytes(p.shape[1:], p.dtype)) for p in ps]
    n_cp = 3 * sum(len(ch) for ch in chunks)

    def body(refs, start, finish):
        p_refs, o_refs = refs[:na], refs[na:2 * na]
        send, recv, loc_in, loc_out = refs[2 * na:2 * na + 4]
        px, py, c = _place()
        me = 2 * px + py
        own = _StagedCopies([(p.at[me], o.at[me]) for p, o in zip(p_refs, o_refs)], refs[2 * na + 4:], loc_in, loc_out)
        plan = [(a, qx, qy, pl.ds(s, n)) for a in range(na) for (qx, qy) in _other_chips(px, py) for (s, n) in chunks[a]]
        cps = [_remote(p_refs[a].at[2 * qx + qy, rows], o_refs[a].at[me, rows], send, recv, i, (qx, qy, c))
               for i, (a, qx, qy, rows) in enumerate(plan)]
        if start:
            own.load()
            for cp in cps:
                cp.start()
        if finish:
            own.store()
            for cp in cps:
                cp.wait()
            own.finish()

    outs = [jax.ShapeDtypeStruct(p.shape, p.dtype) for p in ps]
    return _Comm(ps, outs, [n_cp, n_cp, na, na], [(p.shape[1:], p.dtype) for p in ps], body)


def _sibling_join(rs):
    na = len(rs)
    chunks = [_row_chunks(r.shape[0], _row_bytes(r.shape, r.dtype)) for r in rs]
    n_cp = sum(len(ch) for ch in chunks)

    def body(refs, start, finish):
        r_refs, o_refs = refs[:na], refs[na:2 * na]
        send, recv, loc_in, loc_out = refs[2 * na:2 * na + 4]
        px, py, c = _place()
        own = _StagedCopies([(r, o.at[c]) for r, o in zip(r_refs, o_refs)], refs[2 * na + 4:], loc_in, loc_out)
        plan = [(a, pl.ds(s, n)) for a in range(na) for (s, n) in chunks[a]]
        cps = [_remote(r_refs[a].at[rows], o_refs[a].at[c, rows], send, recv, i, (px, py, 1 - c))
               for i, (a, rows) in enumerate(plan)]
        if start:
            own.load()
            for cp in cps:
                cp.start()
        if finish:
            own.store()
            for cp in cps:
                cp.wait()
            own.finish()

    outs = [jax.ShapeDtypeStruct((2,) + r.shape, r.dtype) for r in rs]
    return _Comm(rs, outs, [n_cp, n_cp, na, na], [(r.shape, r.dtype) for r in rs], body)


def _reduce_stages(gs, c_idx, tag):
    got = yield _sibling_halves(gs)
    part = [_add_sibling(g, o, c_idx, f"reduce_add_sibling{tag}_{a}") for a, (g, o) in enumerate(zip(gs, got))]
    landed = yield _chip_exchange(part)
    mine = [_sum_chips(x, f"reduce_sum_chips{tag}_{a}") for a, x in enumerate(landed)]
    return (yield _sibling_join(mine))


class _Staged:
    def __init__(self, gen, tag):
        self.gen, self.tag, self.k, self.value = gen, tag, 0, None
        self.cur = next(gen)

    def stage(self):
        return self.cur

    def done(self, results):
        self.k += 1
        try:
            self.cur = self.gen.send(results)
        except StopIteration as stop:
            self.cur, self.value = None, stop.value

    def drain(self):
        while self.cur is not None:
            self.done(_run_comm(self.cur, f"comm{self.tag}_{self.k}"))
        return self.value


WEIGHTS = ["meta_tokens", "norm_mix_pre", "norm_mix_post", "norm_mlp_pre", "norm_mlp_post", "w_in", "conv_w",
           "conv_b", "lru_wa_f", "lru_ba_f", "lru_wx_f", "lru_bx_f", "lru_lambda_f", "lru_wa_b", "lru_ba_b",
           "lru_wx_b", "lru_bx_b", "lru_lambda_b", "gla_wg_f", "gla_bg_f", "gla_wg_b", "gla_bg_b", "gla_head_norm",
           "w_out", "w_mlp_up", "w_mlp_down"]
BIG = ("w_in", "w_out", "w_mlp_up", "w_mlp_down")
SMALL = [n for n in WEIGHTS if n not in BIG]
SMALL_SHARDED = {"meta_tokens": 1, "conv_w": 2, "gla_wg_f": 2, "gla_wg_b": 2}
N_CHIPS = 4
SMALL_RAW = [("g1", (1, D_MODEL)), ("g2", (1, D_MODEL)), ("g3", (1, D_MODEL)), ("g4", (1, D_MODEL)),
             ("conv_w", (4, LRU_W)), ("conv_b", (1, LRU_W)), ("lru_wg", (LRU_HEADS, LRU_HD, 4 * LRU_HD)),
             ("lru_pb", (SUBLANES, LRU_W)), ("gla_wg", (2 * GLA_RANK, 2 * QK)), ("gla_bg", (1, 2 * QK)),
             ("head_norm", (1, GLA_W))]
META_SHAPE = (N_META, D_MODEL)


def _pad_to(v, n):
    return jnp.pad(v, (0, n - v.shape[0]))


def _round_up(n, m):
    return -(-n // m) * m


def _flat_cat(arrs, total):
    return _pad_to(jnp.concatenate([a.reshape(-1) for a in arrs]), total)


def _split_flat(flat, shapes):
    out, off = [], 0
    for s in shapes:
        n = math.prod(s)
        out.append(flat[off:off + n].reshape(s))
        off += n
    return out


def _my_shard(full, axis, chip):
    n = full.shape[axis] // N_CHIPS
    return lax.dynamic_slice_in_dim(full, chip * n, n, axis=axis)


def _block_diag_gates(wa_f, wx_f, wa_b, wx_b):
    eye2 = jnp.eye(2, dtype=wa_f.dtype)

    def bd(w):
        w4 = w.reshape(4, 2, LRU_HD, LRU_HD)
        return (w4[:, :, :, None, :] * eye2[None, :, None, :, None]).reshape(4, LANES, LANES)
    return jnp.concatenate([bd(wa_f), bd(wx_f), bd(wa_b), bd(wx_b)], axis=2).astype(BF16)


def _gate_diag_blocks(dwh, o):
    return dwh[:, :, o * LRU_HD:(o + 1) * LRU_HD]


def kernel(x, meta_tokens, norm_mix_pre, norm_mix_post, norm_mlp_pre, norm_mlp_post, w_in, conv_w, conv_b, lru_wa_f, lru_ba_f, lru_wx_f, lru_bx_f, lru_lambda_f, lru_wa_b, lru_ba_b, lru_wx_b, lru_bx_b, lru_lambda_b, gla_wg_f, gla_bg_f, gla_wg_b, gla_bg_b, gla_head_norm, w_out, w_mlp_up, w_mlp_down, loss_target, m_meta_tokens, m_norm_mix_pre, m_norm_mix_post, m_norm_mlp_pre, m_norm_mlp_post, m_w_in, m_conv_w, m_conv_b, m_lru_wa_f, m_lru_ba_f, m_lru_wx_f, m_lru_bx_f, m_lru_lambda_f, m_lru_wa_b, m_lru_ba_b, m_lru_wx_b, m_lru_bx_b, m_lru_lambda_b, m_gla_wg_f, m_gla_bg_f, m_gla_wg_b, m_gla_bg_b, m_gla_head_norm, m_w_out, m_w_mlp_up, m_w_mlp_down, v_meta_tokens, v_norm_mix_pre, v_norm_mix_post, v_norm_mlp_pre, v_norm_mlp_post, v_w_in, v_conv_w, v_conv_b, v_lru_wa_f, v_lru_ba_f, v_lru_wx_f, v_lru_bx_f, v_lru_lambda_f, v_lru_wa_b, v_lru_ba_b, v_lru_wx_b, v_lru_bx_b, v_lru_lambda_b, v_gla_wg_f, v_gla_bg_f, v_gla_wg_b, v_gla_bg_b, v_gla_head_norm, v_w_out, v_w_mlp_up, v_w_mlp_down):
    args = dict(locals())
    w_loc = {n: args[n] for n in WEIGHTS}
    m_loc = {n: args["m_" + n] for n in WEIGHTS}
    v_loc = {n: args["v_" + n] for n in WEIGHTS}
    seq = x.shape[1]
    t_rows = FRONT + seq + BACK
    assert t_rows % 768 == 0 and seq % FRONT == 0, seq
    chip = 2 * lax.axis_index("x") + lax.axis_index("y")
    c_idx = lax.axis_index("c").astype(jnp.int32).reshape(1)

    def halves(a):
        return a.reshape((2, a.shape[0] // 2) + a.shape[1:])

    big16 = {n: w_loc[n].astype(BF16) for n in BIG}
    sm_names = list(SMALL_SHARDED)
    sm_len = _round_up(sum(w_loc[n].size for n in sm_names), 2 * SUBLANES * FLAT_W)
    sm_pack = _flat_cat([w_loc[n] for n in sm_names], sm_len).reshape(2, -1, FLAT_W)
    g_in0, g_small = _run_comm(_allgather_chips([halves(big16["w_in"][0]), sm_pack]), "gather_first")
    sm_parts = [_split_flat(g_small[q].reshape(-1), [w_loc[n].shape for n in sm_names]) for q in range(N_CHIPS)]
    full = {n: jnp.concatenate([sm_parts[q][i] for q in range(N_CHIPS)], axis=SMALL_SHARDED[n])
            for i, n in enumerate(sm_names)}
    for n in SMALL:
        if n not in SMALL_SHARDED:
            full[n] = w_loc[n]

    def full_w_in(g):
        w = jnp.transpose(g.reshape(N_CHIPS, D_MODEL, -1), (1, 0, 2)).reshape(D_MODEL, D_IN)
        return jnp.pad(w, ((0, 0), (0, D_IN_PAD - D_IN)))

    def full_rest(g_out, g_up, g_down):
        return dict(w_out=g_out.reshape(-1, D_MODEL),
                    w_up=jnp.transpose(g_up.reshape(N_CHIPS, D_MODEL, -1), (1, 0, 2)).reshape(D_MODEL, D_FF),
                    w_down=g_down.reshape(D_FF, D_MODEL))

    later_gathers = [_allgather_chips([halves(big16[n][0]) for n in BIG[1:]]),
                     _allgather_chips([halves(big16[n][l]) for l in range(1, DEPTH) for n in BIG])]

    meta_blk = jnp.concatenate([jnp.zeros((FRONT - N_META, D_MODEL), F32), full["meta_tokens"]], axis=0)
    h = jnp.concatenate([meta_blk, x[0], jnp.zeros((BACK, D_MODEL), F32)], axis=0)
    target = loss_target[0]

    layer_w = []
    for l in range(DEPTH):
        wg = jnp.zeros((LANES, 2 * QK), F32)
        wg = wg.at[0:GLA_RANK, 0:QK].set(full["gla_wg_f"][l]).at[GLA_RANK:2 * GLA_RANK, QK:].set(full["gla_wg_b"][l])
        pb = jnp.stack([full["lru_ba_f"][l], full["lru_bx_f"][l], full["lru_ba_b"][l], full["lru_bx_b"][l],
                        full["lru_lambda_f"][l], full["lru_lambda_b"][l], jnp.zeros((LRU_W,), F32),
                        jnp.zeros((LRU_W,), F32)])
        layer_w.append(dict(
            g1=full["norm_mix_pre"][l][None], g2=full["norm_mix_post"][l][None],
            g3=full["norm_mlp_pre"][l][None], g4=full["norm_mlp_post"][l][None],
            conv_w=full["conv_w"][l], conv_b=full["conv_b"][l][None],
            lru_wg=_block_diag_gates(full["lru_wa_f"][l], full["lru_wx_f"][l], full["lru_wa_b"][l], full["lru_wx_b"][l]),
            lru_pb=pb, gla_wg=wg.astype(BF16),
            gla_bg=jnp.concatenate([full["gla_bg_f"][l], full["gla_bg_b"][l]])[None],
            head_norm=full["gla_head_norm"][l][None]))
    layer_w[0]["w_in"] = full_w_in(g_in0)

    def split_z(acc):
        return (acc[:, 0:1024], acc[:, 1024:1536], acc[:, 1536:2048], acc[:, 2048:2560], acc[:, ZG_OFF:ZG_OFF + LANES])

    def relu2(acc):
        r = jnp.maximum(acc, 0.0)
        return acc, r * r

    tm_e = 768
    tm_ff = 1408 if t_rows % 1408 == 0 else 768
    tk_dw = 2816 if t_rows % 2816 == 0 else 768

    def post_mix(acc, hh, g2, g3):
        h1 = hh + _rms_fwd(acc, g2)
        return acc, h1, _rms_fwd(h1, g3)

    def post_mlp(acc, h1, g4, g1_next):
        h2 = h1 + _rms_fwd(acc, g4)
        return acc, h2, _rms_fwd(h2, g1_next)

    def post_mlp_loss(acc, h1, tgt, x_rows, g4):
        h2 = h1 + _rms_fwd(acc, g4)
        err = (h2 - tgt) * x_rows[:, 0:1]
        d_h = err * (1.0 / D_MODEL)
        dff, dg4 = _rms_bwd(acc, g4, d_h)
        loss_part = jnp.zeros((SUBLANES, LANES), F32) + jnp.sum(err * err) * (0.5 / D_MODEL)
        return d_h, dff, loss_part, dg4

    row3 = [(D_MODEL, F32), (D_MODEL, F32), (D_MODEL, BF16)]
    vec = ((1, D_MODEL), F32)
    target_p = jnp.concatenate([jnp.zeros((FRONT, D_MODEL), F32), target, jnp.zeros((BACK, D_MODEL), F32)], axis=0)
    x_rows = jnp.concatenate([jnp.zeros((FRONT, LANES), F32), jnp.ones((seq, LANES), F32), jnp.zeros((BACK, LANES), F32)])
    saved = []
    n1 = _norm_cast(h, layer_w[0]["g1"], t_rows, "norm_mix_pre_l0")
    for l in range(DEPTH):
        lw = layer_w[l]
        tag = f"_l{l}"
        z_lru, z_qk, z_v, z_go, z_zg = _mm(
            n1, lw["w_in"], "nn", 768, D_IN_PAD, D_MODEL,
            [(1024, F32), (512, F32), (512, F32), (512, F32), (LANES, F32)], "in_proj" + tag, epilogue=split_z)
        lru_args = (z_lru, lw["conv_w"], lw["conv_b"], lw["lru_wg"], lw["lru_pb"], t_rows, seq, "lru_fwd" + tag)
        gla_args = (z_qk, z_v, z_zg, lw["gla_wg"], lw["gla_bg"], t_rows, "gla_fwd" + tag)
        if l == 0:
            (h_f, h_b), g_rest = _lru_fwd(*lru_args, comm=later_gathers[0])
            lw.update(full_rest(*g_rest))
            (o_f, o_b, s_f, s_b), g_next = _gla_fwd(*gla_args, comm=later_gathers[1])
            for l2 in range(1, DEPTH):
                g_l2 = g_next[(l2 - 1) * len(BIG):l2 * len(BIG)]
                layer_w[l2].update(full_rest(*g_l2[1:]), w_in=full_w_in(g_l2[0]))
        else:
            h_f, h_b = _lru_fwd(*lru_args)
            o_f, o_b, s_f, s_b = _gla_fwd(*gla_args)
        ymix = _mix_fwd(h_f, h_b, z_lru, o_f, o_b, z_go, lw["head_norm"], t_rows, "mix_fwd" + tag)
        mix, h1, n3 = _mm(ymix, lw["w_out"], "nn", 768, D_MODEL, D_MODEL, row3, "out_proj" + tag, epilogue=post_mix,
                          rows_in=(h,), consts=(lw["g2"], lw["g3"]))
        u, act = _mm(n3, lw["w_up"], "nn", tm_ff, 1024, D_MODEL, [(D_FF, BF16), (D_FF, BF16)], "mlp_up" + tag,
                     epilogue=relu2)
        sv = dict(h=h, n1=n1, z_lru=z_lru, z_qk=z_qk, z_v=z_v, z_go=z_go, z_zg=z_zg, h_f=h_f, h_b=h_b,
                  o_f=o_f, o_b=o_b, s_f=s_f, s_b=s_b, ymix=ymix, mix=mix, h1=h1, n3=n3, u=u, act=act)
        if l + 1 < DEPTH:
            sv["ff"], h, n1 = _mm(act, lw["w_down"], "nn", tm_e, D_MODEL, D_FF, row3, "mlp_down" + tag,
                                  epilogue=post_mlp, rows_in=(h1,), consts=(lw["g4"], layer_w[l + 1]["g1"]))
        else:
            dh, dff, loss_part, dg4 = _mm(act, lw["w_down"], "nn", tm_e, D_MODEL, D_FF,
                                          [(D_MODEL, F32), (D_MODEL, BF16)], "mlp_down_loss" + tag,
                                          epilogue=post_mlp_loss, rows_in=(h1, target_p, x_rows), consts=(lw["g4"],),
                                          accs=[((SUBLANES, LANES), F32), vec])
        saved.append(sv)

    loss = lax.psum(loss_part[0, 0], ("x", "y", "c"))

    small_len = _round_up(sum(math.prod(s) for _, s in SMALL_RAW) + math.prod(META_SHAPE),
                          N_CHIPS * 2 * 2 * SUBLANES * FLAT_W)
    totals = [None] * DEPTH
    pend = None

    def with_stage(staged, fn):
        comm = staged.stage() if staged is not None else None
        if comm is None:
            return fn(None)
        res, comm_res = fn(comm)
        staged.done(comm_res)
        return res

    for l in reversed(range(DEPTH)):
        lw, sv = layer_w[l], saved[l]
        tag = f"_l{l}"
        raw = {"g4": dg4}
        gw_down = with_stage(pend, lambda comm: _mm(sv["act"], dff, "tn", 1024, D_MODEL, tk_dw, [(D_MODEL, F32)],
                                                    "dw_down" + tag, comm=comm))[0]
        du = _mm(dff, lw["w_down"], "nt", tm_ff, 1024, D_MODEL, [(D_FF, BF16)], "d_act" + tag,
                 epilogue=lambda acc, uu: (acc * 2.0 * jnp.maximum(uu.astype(F32), 0.0),), extras=(sv["u"],))[0]
        gw_up = _mm(sv["n3"], du, "tn", D_MODEL, D_FF // N_CHIPS, tk_dw, [(D_FF, F32)], "dw_up" + tag,
                    col_blocks_first=True)[0]

        def pre_mlp_bwd(acc, h1, d_h, mix, g3, g2):
            dx, dg3 = _rms_bwd(h1, g3, acc)
            d_h1 = d_h + dx
            d_mix, dg2 = _rms_bwd(mix, g2, d_h1)
            return d_h1, d_mix, dg3, dg2

        dh1, dmix, raw["g3"], raw["g2"] = _mm(
            du, lw["w_up"], "nt", tm_e, D_MODEL, D_FF, [(D_MODEL, F32), (D_MODEL, BF16)], "d_n3" + tag,
            epilogue=pre_mlp_bwd, rows_in=(sv["h1"], dh, sv["mix"]), consts=(lw["g3"], lw["g2"]), accs=[vec, vec])
        gw_out = _mm(sv["ymix"], dmix, "tn", D_MODEL, D_MODEL, tk_dw, [(D_MODEL, F32)], "dw_out" + tag)[0]
        mats = [gw_out.reshape(N_CHIPS, 2, -1, D_MODEL), gw_up.reshape(N_CHIPS, 2, D_MODEL // 2, D_FF // N_CHIPS),
                gw_down.reshape(N_CHIPS, 2, -1, D_MODEL)]
        early = _Staged(_reduce_stages(mats, c_idx, tag + "e"), tag + "e") if l == 0 else None
        half4 = [(LRU_W, F32), (LRU_W, F32), (GLA_W, F32), (GLA_W, F32)]
        dh_lru, dgate, do, dgo, raw["head_norm"] = with_stage(early, lambda comm: _mm(
            dmix, lw["w_out"], "nt", 768, D_MODEL, D_MODEL, half4, "d_ymix" + tag, epilogue=_mix_bwd_tile,
            rows_in=(sv["h_f"], sv["h_b"], sv["z_lru"], sv["o_f"], sv["o_b"], sv["z_go"]), consts=(lw["head_norm"],),
            accs=[((1, GLA_W), F32)], comm=comm))
        gla_grads = with_stage(pend, lambda comm: _gla_bwd(sv["z_qk"], sv["z_v"], sv["z_zg"], lw["gla_wg"],
                                                           lw["gla_bg"], do, sv["s_f"], sv["s_b"], t_rows,
                                                           "gla_bwd" + tag, comm=comm))
        dx_br, raw["conv_w"], raw["conv_b"], raw["lru_wg"], raw["lru_pb"] = with_stage(early, lambda comm: _lru_bwd(
            sv["z_lru"], lw["conv_w"], lw["conv_b"], lw["lru_wg"], lw["lru_pb"], sv["h_f"], sv["h_b"], dh_lru,
            t_rows, seq, "lru_bwd" + tag, comm=comm))
        dz, dwg_gla, raw["gla_bg"] = _dz_assemble(dx_br, dgate, gla_grads, dgo, sv["z_zg"], lw["gla_wg"], t_rows,
                                                  "dz_assemble" + tag)
        raw["gla_wg"] = dwg_gla[0:2 * GLA_RANK]
        gw_in = with_stage(pend, lambda comm: _mm(sv["n1"], dz, "tn", D_MODEL, 896, tk_dw, [(D_IN_PAD, F32)],
                                                  "dw_in" + tag, comm=comm))[0]
        if l > 0:
            def pre_mix_bwd(acc, hh, d_h1, ff_prev, g1, g4_prev):
                dx, dg1 = _rms_bwd(hh, g1, acc)
                d_h = d_h1 + dx
                d_ff, dg4_prev = _rms_bwd(ff_prev, g4_prev, d_h)
                return d_h, d_ff, dg1, dg4_prev

            dh, dff, raw["g1"], dg4 = _mm(
                dz, lw["w_in"], "nt", tm_e, D_MODEL, D_IN_PAD, [(D_MODEL, F32), (D_MODEL, BF16)], "d_n1" + tag,
                epilogue=pre_mix_bwd, rows_in=(sv["h"], dh1, saved[l - 1]["ff"]),
                consts=(lw["g1"], layer_w[l - 1]["g4"]), accs=[vec, vec])
        else:
            def pre_mix_bwd0(acc, hh, d_h1, g1):
                dx, dg1 = _rms_bwd(hh, g1, acc)
                return d_h1 + dx, dg1

            dh, raw["g1"] = with_stage(early, lambda comm: _mm(
                dz, lw["w_in"], "nt", tm_e, D_MODEL, D_IN_PAD, [(D_MODEL, F32)], "d_n1" + tag, epilogue=pre_mix_bwd0,
                rows_in=(sv["h"], dh1), consts=(lw["g1"],), accs=[vec], comm=comm))

        cols = D_IN // N_CHIPS
        in_sh = jnp.stack([gw_in[:, q * cols:(q + 1) * cols] for q in range(N_CHIPS)])
        meta_g = dh[FRONT - N_META:FRONT] if l == 0 else jnp.zeros(META_SHAPE, F32)
        small = _flat_cat([raw[n] for n, _ in SMALL_RAW] + [meta_g], small_len)
        rest = [in_sh.reshape(N_CHIPS, 2, D_MODEL // 2, cols), small.reshape(N_CHIPS, 2, -1, FLAT_W)]
        if pend is not None:
            totals[l + 1] = pend.drain()
        if early is None:
            pend = _Staged(_reduce_stages([rest[0]] + mats + [rest[1]], c_idx, tag), tag)
        else:
            totals[l] = [None] + list(early.drain()) + [None]
            late = _Staged(_reduce_stages(rest, c_idx, tag), tag)

    grad_x = dh[FRONT:FRONT + seq][None]

    delta, new_m, new_v, g_tot = {}, {}, {}, {}

    def adamw_matrix(a, n, staged):
        shp = w_loc[n].shape
        flat3 = lambda t: t.reshape(DEPTH, -1, shp[-1])
        g_l = [totals[l][a].reshape(-1, shp[-1]) for l in range(DEPTH)]
        outs = with_stage(staged, lambda comm: _adamw_layers(flat3(w_loc[n]), flat3(m_loc[n]), flat3(v_loc[n]), g_l,
                                                             "adamw_" + n, comm=comm))
        g_tot[n], delta[n], new_m[n], new_v[n] = [o.reshape(shp) for o in outs]

    for a, n in list(enumerate(BIG))[1:]:
        adamw_matrix(a, n, late)
    totals[0][0], totals[0][4] = late.drain()

    sm_all = _run_comm(_allgather_chips([totals[l][4] for l in range(DEPTH)]), "gather_small_grads")
    per_layer = []
    for l in range(DEPTH):
        pieces = _split_flat(sm_all[l].reshape(-1), [s for _, s in SMALL_RAW] + [META_SHAPE])
        per_layer.append(dict(zip([n for n, _ in SMALL_RAW] + ["meta"], pieces)))
    stack = lambda f: jnp.stack([f(per_layer[l]) for l in range(DEPTH)])
    g_tot["meta_tokens"] = _my_shard(per_layer[0]["meta"], 1, chip)
    for n, key in (("norm_mix_pre", "g1"), ("norm_mix_post", "g2"), ("norm_mlp_pre", "g3"), ("norm_mlp_post", "g4"),
                   ("conv_b", "conv_b"), ("gla_head_norm", "head_norm")):
        g_tot[n] = stack(lambda d, key=key: d[key][0])
    g_tot["conv_w"] = _my_shard(stack(lambda d: d["conv_w"]), 2, chip)
    for o, n in enumerate(("lru_wa_f", "lru_wx_f", "lru_wa_b", "lru_wx_b")):
        g_tot[n] = stack(lambda d, o=o: _gate_diag_blocks(d["lru_wg"], o))
    for row, n in enumerate(("lru_ba_f", "lru_bx_f", "lru_ba_b", "lru_bx_b", "lru_lambda_f", "lru_lambda_b")):
        g_tot[n] = stack(lambda d, row=row: d["lru_pb"][row])
    g_tot["gla_wg_f"] = _my_shard(stack(lambda d: d["gla_wg"][0:GLA_RANK, 0:QK]), 2, chip)
    g_tot["gla_wg_b"] = _my_shard(stack(lambda d: d["gla_wg"][GLA_RANK:, QK:]), 2, chip)
    g_tot["gla_bg_f"] = stack(lambda d: d["gla_bg"][0, 0:QK])
    g_tot["gla_bg_b"] = stack(lambda d: d["gla_bg"][0, QK:])

    adamw_matrix(0, BIG[0], None)
    outs = _adamw_small(*[[d[n] for n in SMALL] for d in (w_loc, g_tot, m_loc, v_loc)])
    for d, o in zip((delta, new_m, new_v), outs):
        d.update(zip(SMALL, o))
    return (loss, grad_x, *[g_tot[n] for n in WEIGHTS], *[delta[n] for n in WEIGHTS],
            *[new_m[n] for n in WEIGHTS], *[new_v[n] for n in WEIGHTS])
```

```python
import math

import jax
import jax.numpy as jnp
from jax import lax
from jax.experimental import pallas as pl
from jax.experimental.pallas import tpu as pltpu

F32 = jnp.float32
BF16 = jnp.bfloat16

D_MODEL = 1024
DEPTH = 2
N_META = 16
LRU_W = 512
LRU_HEADS = 8
LRU_HD = 64
LRU_C = 8.0
GLA_W = 512
GLA_H = 4
GLA_DK = 64
GLA_DV = 128
GLA_RANK = 16
GLA_GATE_NORM = 16.0
D_FF = 4096
D_IN = 2592
EPS = 1e-6
ADAM_LR, ADAM_B1, ADAM_B2, ADAM_EPS, ADAM_WD, ADAM_STEP = 0.001, 0.9, 0.999, 1e-08, 0.01, 10

LANES = 128
SUBLANES = 8
FRONT = 128
BACK = 128
D_IN_PAD = 2688
ZG_OFF = 2560
CHUNK = 64
EXP_CLAMP = 80.0
VMEM_LIMIT = 56 * 1024 * 1024
MESH_ID = pl.DeviceIdType.MESH

NN = (((1,), (0,)), ((), ()))
NT = (((1,), (1,)), ((), ()))
TN = (((0,), (0,)), ((), ()))


def _dot(a, b, dims=NN):
    return lax.dot_general(a, b, dims, preferred_element_type=F32)


def _cparams(sem):
    return pltpu.CompilerParams(dimension_semantics=sem, vmem_limit_bytes=VMEM_LIMIT)


def _sigmoid(x):
    return 1.0 / (1.0 + jnp.exp(-x))


def _gelu(x):
    c = math.sqrt(2.0 / math.pi)
    return 0.5 * x * (1.0 + jnp.tanh(c * (x + 0.044715 * x * x * x)))


def _gelu_grad(x):
    c = math.sqrt(2.0 / math.pi)
    t = jnp.tanh(c * (x + 0.044715 * x * x * x))
    return 0.5 * (1.0 + t) + 0.5 * x * (1.0 - t * t) * c * (1.0 + 3.0 * 0.044715 * x * x)


def _rms_fwd(x, g):
    rstd = lax.rsqrt(jnp.mean(x * x, axis=1, keepdims=True) + EPS)
    return (x * rstd) * g


def _rms_bwd(x, g, dy):
    rstd = lax.rsqrt(jnp.mean(x * x, axis=1, keepdims=True) + EPS)
    xh = x * rstd
    dxh = dy * g
    dx = rstd * (dxh - xh * jnp.mean(dxh * xh, axis=1, keepdims=True))
    dg = jnp.sum(dy * xh, axis=0, keepdims=True)
    return dx, dg


def _mm(a, b, mode, tm, tn, tk, outs, name, epilogue=None, extras=(), col_blocks_first=False,
        rows_in=(), consts=(), accs=(), comm=None):
    if mode == "nn":
        (m, k), n = a.shape, b.shape[1]
        a_spec = pl.BlockSpec((tm, tk), lambda i, j, kk: (i, kk))
        b_spec = pl.BlockSpec((tk, tn), lambda i, j, kk: (kk, j))
        dims = NN
    elif mode == "nt":
        (m, k), n = a.shape, b.shape[0]
        a_spec = pl.BlockSpec((tm, tk), lambda i, j, kk: (i, kk))
        b_spec = pl.BlockSpec((tn, tk), lambda i, j, kk: (j, kk))
        dims = NT
    else:
        (k, m), n = a.shape, b.shape[1]
        a_spec = pl.BlockSpec((tk, tm), lambda i, j, kk: (kk, i))
        b_spec = pl.BlockSpec((tk, tn), lambda i, j, kk: (kk, j))
        dims = TN
    assert m % tm == 0 and n % tn == 0 and k % tk == 0, (name, m, n, k)
    nk = k // tk
    if n == tn and nk == 1:
        b_spec = pl.BlockSpec(b_spec.block_shape, b_spec.index_map, pipeline_mode=pl.Buffered(1))
    ne = len(extras) + len(rows_in) + len(consts)
    e_specs = [pl.BlockSpec((tm, tn), lambda i, j, kk: (i, j)) for _ in extras]
    e_specs += [pl.BlockSpec((tm, r.shape[1]), lambda i, j, kk: (i, 0)) for r in rows_in]
    e_specs += [pl.BlockSpec(c.shape, lambda i, j, kk: (0, 0)) for c in consts]
    n_out = len(outs)
    o_specs, o_shapes = [], []
    for width, dtype in outs:
        if col_blocks_first:
            assert width == n, name
            o_specs.append(pl.BlockSpec((None, tm, tn), lambda i, j, kk: (j, i, 0)))
            o_shapes.append(jax.ShapeDtypeStruct((n // tn, m, tn), dtype))
            continue
        if width == n:
            o_specs.append(pl.BlockSpec((tm, tn), lambda i, j, kk: (i, j)))
        else:
            assert n == tn, name
            o_specs.append(pl.BlockSpec((tm, width), lambda i, j, kk: (i, 0)))
        o_shapes.append(jax.ShapeDtypeStruct((m, width), dtype))
    o_specs += [pl.BlockSpec(s, lambda i, j, kk: (0, 0)) for s, _ in accs]
    o_shapes += [jax.ShapeDtypeStruct(s, d) for s, d in accs]

    def finish(acc, extra_refs, out_refs):
        res = epilogue(acc, *[e[...] for e in extra_refs]) if epilogue else (acc,)
        for o, r in zip(out_refs[:n_out], res[:n_out]):
            o[...] = r.astype(o.dtype)
        first = jnp.logical_and(pl.program_id(0) == 0, pl.program_id(1) == 0)
        for o, r in zip(out_refs[n_out:], res[n_out:]):
            @pl.when(first)
            def _(o=o):
                o[...] = jnp.zeros_like(o)

            o[...] += r

    if nk == 1:
        def body(a_ref, b_ref, *rest):
            finish(_dot(a_ref[...], b_ref[...], dims), rest[:ne], rest[ne:])
        scratch = []
    else:
        def body(a_ref, b_ref, *rest):
            acc_ref = rest[-1]
            kk = pl.program_id(2)

            @pl.when(kk == 0)
            def _():
                acc_ref[...] = jnp.zeros_like(acc_ref)

            acc_ref[...] += _dot(a_ref[...], b_ref[...], dims)

            @pl.when(kk == nk - 1)
            def _():
                finish(acc_ref[...], rest[:ne], rest[ne:-1])
        scratch = [pltpu.VMEM((tm, tn), F32)]

    sem = ("arbitrary", "arbitrary", "arbitrary") if accs else ("parallel", "parallel", "arbitrary")
    res, comm_res = _pcall(body, name, (m // tm, n // tn, nk), [a_spec, b_spec] + e_specs, o_specs, o_shapes, scratch,
                           sem, (a, b, *extras, *rows_in, *consts), comm)
    return res if comm is None else (res, comm_res)


def _rows(body, n_rows, tm, ins, consts, outs, accs, name):
    assert n_rows % tm == 0, (name, n_rows, tm)
    in_specs = [pl.BlockSpec((tm, w), (lambda i, cb=cb: (i, cb))) for _, w, cb in ins]
    in_specs += [pl.BlockSpec(c.shape, lambda i: (0, 0)) for c in consts]
    out_specs = [pl.BlockSpec((tm, w), lambda i: (i, 0)) for w, _ in outs]
    out_specs += [pl.BlockSpec(s, lambda i: (0, 0)) for s, _ in accs]
    out_shape = [jax.ShapeDtypeStruct((n_rows, w), d) for w, d in outs]
    out_shape += [jax.ShapeDtypeStruct(s, d) for s, d in accs]
    ni, nc, no = len(ins), len(consts), len(outs)

    def kern(*refs):
        body(pl.program_id(0), refs[:ni], refs[ni:ni + nc], refs[ni + nc:ni + nc + no], refs[ni + nc + no:])

    res = pl.pallas_call(
        kern, name=name, grid=(n_rows // tm,), in_specs=in_specs, out_specs=out_specs, out_shape=out_shape,
        compiler_params=_cparams(("arbitrary",)),
    )(*[a for a, _, _ in ins], *consts)
    return res


def _acc_add(i, ref, val):
    @pl.when(i == 0)
    def _():
        ref[...] = jnp.zeros_like(ref)

    ref[...] += val


def _norm_cast(h, g, t_rows, name):
    def body(i, ins, consts, outs, accs):
        outs[0][...] = _rms_fwd(ins[0][...], consts[0][...]).astype(BF16)
    return _rows(body, t_rows, 384, [(h, D_MODEL, 0)], [g], [(D_MODEL, BF16)], [], name)[0]


def _head_norm_parts(o):
    ns, rs = [], []
    for hd in range(GLA_H):
        oh = o[:, hd * GLA_DV:(hd + 1) * GLA_DV]
        r = lax.rsqrt(jnp.mean(oh * oh, axis=1, keepdims=True) + EPS)
        ns.append(oh * r)
        rs.append(r)
    return ns, rs


def _mix_fwd(h_f, h_b, z_lru, o_f, o_b, z_go, head_norm, t_rows, name):
    def body(i, ins, consts, outs, accs):
        hf, hb, gate, of, ob, go = [r[...] for r in ins]
        hn = consts[0][...]
        outs[0][:, 0:LRU_W] = ((hf + hb) * _gelu(gate)).astype(BF16)
        ns, _ = _head_norm_parts(of + ob)
        silu = go * _sigmoid(go)
        for hd in range(GLA_H):
            sl = slice(hd * GLA_DV, (hd + 1) * GLA_DV)
            outs[0][:, LRU_W + hd * GLA_DV:LRU_W + (hd + 1) * GLA_DV] = (ns[hd] * hn[:, sl] * silu[:, sl]).astype(BF16)
    ins = [(h_f, LRU_W, 0), (h_b, LRU_W, 0), (z_lru, LRU_W, 1), (o_f, GLA_W, 0), (o_b, GLA_W, 0), (z_go, GLA_W, 0)]
    return _rows(body, t_rows, 384, ins, [head_norm], [(D_MODEL, BF16)], [], name)[0]


def _mix_bwd_tile(dymix, hf, hb, z_lru, of, ob, go, hn):
    dyl, dyg, gate = dymix[:, 0:LRU_W], dymix[:, LRU_W:], z_lru[:, LRU_W:]
    dh_lru = dyl * _gelu(gate)
    dgate = dyl * (hf + hb) * _gelu_grad(gate)
    ns, rs = _head_norm_parts(of + ob)
    sg = _sigmoid(go)
    silu = go * sg
    dsilu = sg * (1.0 + go * (1.0 - sg))
    d_o, d_go, dhn = [], [], []
    for hd in range(GLA_H):
        sl = slice(hd * GLA_DV, (hd + 1) * GLA_DV)
        dy = dyg[:, sl]
        e = dy * hn[:, sl] * silu[:, sl]
        d_o.append(rs[hd] * (e - ns[hd] * jnp.mean(e * ns[hd], axis=1, keepdims=True)))
        d_go.append(dy * ns[hd] * hn[:, sl] * dsilu[:, sl])
        dhn.append(jnp.sum(dy * ns[hd] * silu[:, sl], axis=0, keepdims=True))
    return dh_lru, dgate, jnp.concatenate(d_o, axis=1), jnp.concatenate(d_go, axis=1), jnp.concatenate(dhn, axis=1)


def _dz_assemble(dx_br, dgate, gla_grads, dgo, z_zg, wg, t_rows, name):
    dq_f, dk_f, dv_f, dpre_f, dq_b, dk_b, dv_b, dpre_b = gla_grads
    qk = GLA_H * GLA_DK

    def body(i, ins, consts, outs, accs):
        (dxb, dgt, dqf, dqb, dkf, dkb, dvf, dvb, dg_o, dpf, dpb, zg) = [r[...] for r in ins]
        w = consts[0][...]
        o = outs[0]
        o[:, 0:LRU_W] = dxb.astype(BF16)
        o[:, LRU_W:2 * LRU_W] = dgt.astype(BF16)
        o[:, 1024:1024 + qk] = ((dqf + dqb) * (GLA_DK ** -0.5)).astype(BF16)
        o[:, 1280:1280 + qk] = (dkf + dkb).astype(BF16)
        o[:, 1536:1536 + GLA_W] = (dvf + dvb).astype(BF16)
        o[:, 2048:2048 + GLA_W] = dg_o.astype(BF16)
        dpre = jnp.concatenate([dpf, dpb], axis=1)
        dpre16 = dpre.astype(BF16)
        o[:, ZG_OFF:ZG_OFF + LANES] = _dot(dpre16, w, NT).astype(BF16)
        _acc_add(i, accs[0], _dot(zg.astype(BF16), dpre16, TN))
        _acc_add(i, accs[1], jnp.sum(dpre, axis=0, keepdims=True))

    ins = [(dx_br, LRU_W, 0), (dgate, LRU_W, 0), (dq_f, qk, 0), (dq_b, qk, 0), (dk_f, qk, 0), (dk_b, qk, 0),
           (dv_f, GLA_W, 0), (dv_b, GLA_W, 0), (dgo, GLA_W, 0), (dpre_f, qk, 0), (dpre_b, qk, 0), (z_zg, LANES, 0)]
    return _rows(body, t_rows, 384, ins, [wg], [(D_IN_PAD, BF16)],
                 [((LANES, 2 * qk), F32), ((1, 2 * qk), F32)], name)


LRU_RB = 768
HALO = SUBLANES


def _scan8(a, u, rev):
    rows = lax.broadcasted_iota(jnp.int32, a.shape, 0)
    for d in (1, 2, 4):
        if rev:
            a_s, u_s, ok = pltpu.roll(a, SUBLANES - d, 0), pltpu.roll(u, SUBLANES - d, 0), rows < SUBLANES - d
        else:
            a_s, u_s, ok = pltpu.roll(a, d, 0), pltpu.roll(u, d, 0), rows >= d
        u = jnp.where(ok, a * u_s + u, u)
        a = jnp.where(ok, a * a_s, a)
    return a, u


def _edge_row(x, rev):
    r = x[0:1, :] if rev else x[SUBLANES - 1:SUBLANES, :]
    return jnp.broadcast_to(r, x.shape)


def _scan_block(a, u, carry, rev):
    nsub = a.shape[0] // SUBLANES
    loc = [_scan8(a[sb * SUBLANES:(sb + 1) * SUBLANES], u[sb * SUBLANES:(sb + 1) * SUBLANES], rev) for sb in range(nsub)]
    edges = [(_edge_row(a_c, rev), _edge_row(h_l, rev)) for a_c, h_l in loc]
    carries = [None] * nsub
    for sb in (range(nsub - 1, -1, -1) if rev else range(nsub)):
        carries[sb] = carry
        carry = edges[sb][0] * carry + edges[sb][1]
    h = jnp.concatenate([h_l + a_c * cin for (a_c, h_l), cin in zip(loc, carries)], axis=0)
    return h, carry


def _lru_gates(xc, pre_a, pre_x, ba, bx, sp8):
    r = _sigmoid(pre_a + ba)
    i = _sigmoid(pre_x + bx)
    log_a = -(r * sp8)
    a = jnp.exp(log_a)
    y = 2.0 * log_a
    series = -y * (1.0 + y * (0.5 + y * (1.0 / 6.0 + y * (1.0 / 24.0 + y * (1.0 / 120.0)))))
    om = jnp.where(y > -0.25, series, 1.0 - a * a)
    s = jnp.sqrt(om)
    return r, i, a, s


def _softplus(x):
    return jnp.maximum(x, 0.0) + jnp.log(1.0 + jnp.exp(-jnp.abs(x)))


def _conv_taps(xpad_ref, r0, nrows, shifts=(-2, -1, 0, 1)):
    ext = xpad_ref[pl.ds(r0, nrows + 2 * HALO), :]
    taps = []
    for s in shifts:
        sh = ext if s == 0 else pltpu.roll(ext, (-s) % (nrows + 2 * HALO), 0)
        taps.append(sh[HALO:HALO + nrows])
    return taps


def _row_mask(r0, nrows, seq):
    rows = r0 + lax.broadcasted_iota(jnp.int32, (nrows, LANES), 0)
    return jnp.logical_and(rows >= FRONT - N_META, rows < FRONT + seq).astype(F32)


def _lru_load_conv(z_hbm, xpad, xc, sem, cw_ref, cb_ref, t_rows, seq):
    j = pl.program_id(0)
    zeros = jnp.zeros((HALO, LANES), F32)
    xpad[0:HALO, :] = zeros
    xpad[t_rows + HALO:t_rows + 2 * HALO, :] = zeros
    cp = pltpu.make_async_copy(z_hbm.at[:, pl.ds(pl.multiple_of(j * LANES, LANES), LANES)],
                               xpad.at[pl.ds(HALO, t_rows)], sem)
    cp.start()
    cp.wait()
    cw = cw_ref[...]
    cb = cb_ref[...]

    def conv_blk(r, carry):
        r0 = pl.multiple_of(r * LRU_RB, LRU_RB)
        taps = _conv_taps(xpad, r0, LRU_RB)
        acc = cb + cw[0:1, :] * taps[0]
        for k in range(1, 4):
            acc = acc + cw[k:k + 1, :] * taps[k]
        xc[pl.ds(r0, LRU_RB), :] = acc * _row_mask(r0, LRU_RB, seq)
        return carry

    lax.fori_loop(0, t_rows // LRU_RB, conv_blk, 0)


def _lru_specs(t_rows):
    return [pl.BlockSpec(memory_space=pl.ANY),
            pl.BlockSpec((4, LANES), lambda j: (0, j)),
            pl.BlockSpec((1, LANES), lambda j: (0, j)),
            pl.BlockSpec((1, LANES, 4 * LANES), lambda j: (j, 0, 0)),
            pl.BlockSpec((SUBLANES, LANES), lambda j: (0, j))]


def _lru_fwd(z_lru, conv_w, conv_b, wg, pb, t_rows, seq, name, comm=None):
    nblk = t_rows // LRU_RB

    def kern(z_hbm, cw_ref, cb_ref, wg_ref, pb_ref, hf_ref, hb_ref, xpad, xc, sem):
        _lru_load_conv(z_hbm, xpad, xc, sem, cw_ref, cb_ref, t_rows, seq)
        w = wg_ref[0]
        pb_v = pb_ref[...]
        dirs = []
        for rev in (False, True):
            o = 2 if rev else 0
            dirs.append(dict(rev=rev, ba=pb_v[o:o + 1, :], bx=pb_v[o + 1:o + 2, :],
                             sp8=LRU_C * _softplus(-pb_v[5:6, :] if rev else -pb_v[4:5, :]),
                             w=w[:, o * LANES:(o + 2) * LANES], out=hb_ref if rev else hf_ref))

        def blk(it, carries):
            new = []
            for d, carry in zip(dirs, carries):
                r = (nblk - 1 - it) if d["rev"] else it
                r0 = pl.multiple_of(r * LRU_RB, LRU_RB)
                xcb = xc[pl.ds(r0, LRU_RB), :]
                pre = _dot(xcb.astype(BF16), d["w"])
                _, gi, a, s = _lru_gates(xcb, pre[:, 0:LANES], pre[:, LANES:2 * LANES], d["ba"], d["bx"], d["sp8"])
                h, carry = _scan_block(a, s * (gi * xcb), carry, d["rev"])
                d["out"][pl.ds(r0, LRU_RB), :] = h
                new.append(carry)
            return tuple(new)

        z8 = jnp.zeros((SUBLANES, LANES), F32)
        lax.fori_loop(0, nblk, blk, (z8, z8), unroll=2)

    res, comm_res = _pcall(
        kern, name, (LRU_W // LANES,), _lru_specs(t_rows), [pl.BlockSpec((t_rows, LANES), lambda j: (0, j))] * 2,
        [jax.ShapeDtypeStruct((t_rows, LRU_W), F32)] * 2,
        [pltpu.VMEM((t_rows + 2 * HALO, LANES), F32), pltpu.VMEM((t_rows, LANES), F32), pltpu.SemaphoreType.DMA],
        ("arbitrary",), (z_lru, conv_w, conv_b, wg, pb), comm)
    return res if comm is None else (res, comm_res)


def _lru_bwd(z_lru, conv_w, conv_b, wg, pb, h_f, h_b, dh, t_rows, seq, name, comm=None):
    nblk = t_rows // LRU_RB

    def kern(z_hbm, cw_ref, cb_ref, wg_ref, pb_ref, hf_ref, hb_ref, dh_ref,
             dx_ref, dcw_ref, dcb_ref, dwh_ref, dpb_ref, xpad, xc, dxc, dwg, sem):
        _lru_load_conv(z_hbm, xpad, xc, sem, cw_ref, cb_ref, t_rows, seq)
        w = wg_ref[0]
        pb_v = pb_ref[...]
        zeros = jnp.zeros((HALO, LANES), F32)
        dxc[0:HALO, :] = zeros
        dxc[t_rows + HALO:t_rows + 2 * HALO, :] = zeros
        dwg[...] = jnp.zeros_like(dwg)

        def zero_blk(r, carry):
            dxc[pl.ds(pl.multiple_of(r * LRU_RB, LRU_RB) + HALO, LRU_RB), :] = jnp.zeros((LRU_RB, LANES), F32)
            return carry

        lax.fori_loop(0, nblk, zero_blk, 0)
        dirs = []
        for rev in (False, True):
            o = 2 if rev else 0
            lam = pb_v[5:6, :] if rev else pb_v[4:5, :]
            dirs.append(dict(rev=rev, o=o, ba=pb_v[o:o + 1, :], bx=pb_v[o + 1:o + 2, :], sp8=LRU_C * _softplus(-lam),
                             dlam_scale=LRU_C * _sigmoid(-lam), h_ref=hb_ref if rev else hf_ref,
                             w=w[:, o * LANES:(o + 2) * LANES]))
        rows_b = lax.broadcasted_iota(jnp.int32, (LRU_RB, LANES), 0)

        def blk(it, carries):
            new = []
            for d, (nu_c, acc_ba, acc_bx, acc_lam) in zip(dirs, carries):
                rev, o, h_ref = d["rev"], d["o"], d["h_ref"]
                adj_rev = not rev
                r = (nblk - 1 - it) if adj_rev else it
                r0 = pl.multiple_of(r * LRU_RB, LRU_RB)
                xcb = xc[pl.ds(r0, LRU_RB), :]
                xc16 = xcb.astype(BF16)
                pre = _dot(xc16, d["w"])
                gr, gi, a, s = _lru_gates(xcb, pre[:, 0:LANES], pre[:, LANES:2 * LANES], d["ba"], d["bx"], d["sp8"])
                dhb = dh_ref[pl.ds(r0, LRU_RB), :]
                hb = h_ref[pl.ds(r0, LRU_RB), :]
                if rev:
                    nxt0 = pl.multiple_of(jnp.minimum(r0 + LRU_RB, t_rows - SUBLANES), SUBLANES)
                    edge = h_ref[pl.ds(nxt0, SUBLANES), :][0:1, :] * (r < nblk - 1).astype(F32)
                    h_prev = jnp.concatenate([hb[1:], edge], axis=0)
                else:
                    prv0 = pl.multiple_of(jnp.maximum(r0 - SUBLANES, 0), SUBLANES)
                    edge = h_ref[pl.ds(prv0, SUBLANES), :][SUBLANES - 1:SUBLANES, :] * (r > 0).astype(F32)
                    h_prev = jnp.concatenate([edge, hb[:-1]], axis=0)
                nu, nu_out = _scan_block(a, a * dhb, nu_c, adj_rev)
                cin = jnp.broadcast_to(nu_c[0:1, :], (LRU_RB, LANES))
                if adj_rev:
                    nu_next = jnp.where(rows_b == LRU_RB - 1, cin, pltpu.roll(nu, LRU_RB - 1, 0))
                else:
                    nu_next = jnp.where(rows_b == 0, cin, pltpu.roll(nu, 1, 0))
                du = dhb + nu_next
                da = du * h_prev
                ix = gi * xcb
                dlog = da * a - du * ix * (a * a) / s
                d_i = du * s * xcb
                dr = -dlog * d["sp8"]
                dpa = dr * gr * (1.0 - gr)
                dpx = d_i * gi * (1.0 - gi)
                dp16 = jnp.concatenate([dpa, dpx], axis=1).astype(BF16)
                dxc[pl.ds(r0 + HALO, LRU_RB), :] += du * s * gi + _dot(dp16, d["w"], NT)
                dwg[:, o * LANES:(o + 2) * LANES] += _dot(xc16, dp16, TN)
                new.append((nu_out, acc_ba + jnp.sum(dpa, axis=0, keepdims=True),
                            acc_bx + jnp.sum(dpx, axis=0, keepdims=True),
                            acc_lam + jnp.sum(dlog * gr, axis=0, keepdims=True)))
            return tuple(new)

        z1 = jnp.zeros((1, LANES), F32)
        init = (jnp.zeros((SUBLANES, LANES), F32), z1, z1, z1)
        (_, ba_f, bx_f, lam_f), (_, ba_b, bx_b, lam_b) = lax.fori_loop(0, nblk, blk, (init, init), unroll=2)
        dpb_ref[...] = jnp.concatenate([ba_f, bx_f, ba_b, bx_b, lam_f * dirs[0]["dlam_scale"],
                                        lam_b * dirs[1]["dlam_scale"], z1, z1], axis=0)
        for hh in range(2):
            for o in range(4):
                c0 = o * LANES + hh * LRU_HD
                dwh_ref[hh, :, o * LRU_HD:(o + 1) * LRU_HD] = dwg[hh * LRU_HD:(hh + 1) * LRU_HD, c0:c0 + LRU_HD]

        def mask_blk(r, carry):
            r0 = pl.multiple_of(r * LRU_RB, LRU_RB)
            dxc[pl.ds(r0 + HALO, LRU_RB), :] = dxc[pl.ds(r0 + HALO, LRU_RB), :] * _row_mask(r0, LRU_RB, seq)
            return carry

        lax.fori_loop(0, nblk, mask_blk, 0)

        cw = cw_ref[...]

        def conv_bwd(r, carry):
            r0 = pl.multiple_of(r * LRU_RB, LRU_RB)
            gt = _conv_taps(dxc, r0, LRU_RB, (2, 1, 0, -1))
            xt = _conv_taps(xpad, r0, LRU_RB)
            dx_ref[pl.ds(r0, LRU_RB), :] = (cw[0:1, :] * gt[0] + cw[1:2, :] * gt[1] + cw[2:3, :] * gt[2]
                                           + cw[3:4, :] * gt[3])
            g = gt[2]
            new = [carry[k] + jnp.sum(g * xt[k], axis=0, keepdims=True) for k in range(4)]
            new.append(carry[4] + jnp.sum(g, axis=0, keepdims=True))
            return tuple(new)

        z1 = jnp.zeros((1, LANES), F32)
        sums = lax.fori_loop(0, nblk, conv_bwd, (z1, z1, z1, z1, z1))
        dcw_ref[...] = jnp.concatenate(sums[:4], axis=0)
        dcb_ref[...] = sums[4]

    col = lambda j: (0, j)
    res, comm_res = _pcall(
        kern, name, (LRU_W // LANES,), _lru_specs(t_rows) + [pl.BlockSpec((t_rows, LANES), col)] * 3,
        [pl.BlockSpec((t_rows, LANES), col), pl.BlockSpec((4, LANES), col), pl.BlockSpec((1, LANES), col),
         pl.BlockSpec((2, LRU_HD, 4 * LRU_HD), lambda j: (j, 0, 0)), pl.BlockSpec((SUBLANES, LANES), col)],
        [jax.ShapeDtypeStruct((t_rows, LRU_W), F32), jax.ShapeDtypeStruct((4, LRU_W), F32),
         jax.ShapeDtypeStruct((1, LRU_W), F32), jax.ShapeDtypeStruct((LRU_HEADS, LRU_HD, 4 * LRU_HD), F32),
         jax.ShapeDtypeStruct((SUBLANES, LRU_W), F32)],
        [pltpu.VMEM((t_rows + 2 * HALO, LANES), F32), pltpu.VMEM((t_rows, LANES), F32),
         pltpu.VMEM((t_rows + 2 * HALO, LANES), F32), pltpu.VMEM((LANES, 4 * LANES), F32), pltpu.SemaphoreType.DMA],
        ("arbitrary",), (z_lru, conv_w, conv_b, wg, pb, h_f, h_b, dh), comm)
    return res if comm is None else (res, comm_res)


QK = GLA_H * GLA_DK
GLA_G = 6


def _cumsum_rows(x, rev):
    n = x.shape[0]
    rows = lax.broadcasted_iota(jnp.int32, x.shape, 0)
    d = 1
    while d < n:
        if rev:
            x = x + jnp.where(rows < n - d, pltpu.roll(x, n - d, 0), 0.0)
        else:
            x = x + jnp.where(rows >= d, pltpu.roll(x, d, 0), 0.0)
        d *= 2
    return x


def _gla_chunk_terms(q, k, zg, wg, bg, rev):
    pre = (_dot(zg.astype(BF16), wg) + bg)[:, (QK if rev else 0):(2 * QK if rev else QK)]
    g = (jnp.minimum(pre, 0.0) - jnp.log(1.0 + jnp.exp(-jnp.abs(pre)))) * (1.0 / GLA_GATE_NORM)
    b = _cumsum_rows(g, rev)
    b_last = b[0:1, :] if rev else b[CHUNK - 1:CHUNK, :]
    b_mid = b[CHUNK // 2:CHUNK // 2 + 1, :]
    qs = q * (GLA_DK ** -0.5)
    terms = dict(
        pre=pre, qs=qs, k=k,
        eb=jnp.exp(b), ek=jnp.exp(b_last - b), e_last=jnp.exp(b_last),
        eqm=jnp.exp(jnp.minimum(b - b_mid, EXP_CLAMP)), ekm=jnp.exp(jnp.minimum(b_mid - b, EXP_CLAMP)))
    return terms


def _gla_mask(rev):
    t = lax.broadcasted_iota(jnp.int32, (CHUNK, CHUNK), 0)
    s = lax.broadcasted_iota(jnp.int32, (CHUNK, CHUNK), 1)
    return (s > t) if rev else (s <= t)


def _gla_head_ops(tm, hd):
    sl = slice(hd * GLA_DK, (hd + 1) * GLA_DK)
    q_b = (tm["qs"][:, sl] * tm["eb"][:, sl]).astype(BF16)
    k_e = (tm["k"][:, sl] * tm["ek"][:, sl]).astype(BF16)
    q_m = (tm["qs"][:, sl] * tm["eqm"][:, sl]).astype(BF16)
    k_m = (tm["k"][:, sl] * tm["ekm"][:, sl]).astype(BF16)
    return sl, q_b, k_e, q_m, k_m


def _gla_fwd(z_qk, z_v, z_zg, wg, bg, t_rows, name, comm=None):
    nc = t_rows // CHUNK
    ng = nc // GLA_G
    rows = CHUNK * GLA_G

    def kern(qf, kf, vf, zf, qb, kb, vb, zb, wg_ref, bg_ref, of_ref, ob_ref, sf_ref, sb_ref, st_f, st_b):
        i = pl.program_id(0)

        @pl.when(i == 0)
        def _():
            st_f[...] = jnp.zeros_like(st_f)
            st_b[...] = jnp.zeros_like(st_b)

        wg_v, bg_v = wg_ref[...], bg_ref[...]
        for rev, (q_r, k_r, v_r, z_r, o_r, s_r, st) in ((False, (qf, kf, vf, zf, of_ref, sf_ref, st_f)),
                                                        (True, (qb, kb, vb, zb, ob_ref, sb_ref, st_b))):
            mask = _gla_mask(rev)
            for ck in (range(GLA_G - 1, -1, -1) if rev else range(GLA_G)):
                cr = slice(ck * CHUNK, (ck + 1) * CHUNK)
                tm = _gla_chunk_terms(q_r[cr, :], k_r[cr, :], z_r[cr, :], wg_v, bg_v, rev)
                v = v_r[cr, :]
                for hd in range(GLA_H):
                    sl, q_b, k_e, q_m, k_m = _gla_head_ops(tm, hd)
                    vs = slice(hd * GLA_DV, (hd + 1) * GLA_DV)
                    v16 = v[:, vs].astype(BF16)
                    a = jnp.where(mask, _dot(q_m, k_m, NT), 0.0).astype(BF16)
                    s_in = st[hd]
                    s_r[ck, hd] = s_in
                    o_r[cr, vs] = _dot(a, v16) + _dot(q_b, s_in.astype(BF16), NT)
                    st[hd] = s_in * tm["e_last"][:, sl] + _dot(v16, k_e, TN)

    fw = lambda c: (lambda i: (i, c))
    rv = lambda c: (lambda i: (ng - 1 - i, c))
    def side(ix):
        return [pl.BlockSpec((rows, QK), ix(0)), pl.BlockSpec((rows, QK), ix(1)),
                pl.BlockSpec((rows, GLA_W), ix(0)), pl.BlockSpec((rows, LANES), ix(0))]
    st_shape = (nc, GLA_H, GLA_DV, GLA_DK)
    res, comm_res = _pcall(
        kern, name, (ng,),
        side(fw) + side(rv) + [pl.BlockSpec(wg.shape, lambda i: (0, 0)), pl.BlockSpec(bg.shape, lambda i: (0, 0))],
        [pl.BlockSpec((rows, GLA_W), fw(0)), pl.BlockSpec((rows, GLA_W), rv(0)),
         pl.BlockSpec((GLA_G,) + st_shape[1:], lambda i: (i, 0, 0, 0)),
         pl.BlockSpec((GLA_G,) + st_shape[1:], lambda i: (ng - 1 - i, 0, 0, 0))],
        [jax.ShapeDtypeStruct((t_rows, GLA_W), F32)] * 2 + [jax.ShapeDtypeStruct(st_shape, F32)] * 2,
        [pltpu.VMEM(st_shape[1:], F32)] * 2, ("arbitrary",),
        (z_qk, z_qk, z_v, z_zg, z_qk, z_qk, z_v, z_zg, wg, bg), comm)
    return res if comm is None else (res, comm_res)


def _gla_bwd(z_qk, z_v, z_zg, wg, bg, do, s_f, s_b, t_rows, name, comm=None):
    nc = t_rows // CHUNK

    def kern(qf, kf, vf, zf, dof, ssf, qb, kb, vb, zb, dob, ssb, wg_ref, bg_ref,
             dqf, dkf, dvf, dpf, dqb, dkb, dvb, dpb, ds_f, ds_b):
        i = pl.program_id(0)

        @pl.when(i == 0)
        def _():
            ds_f[...] = jnp.zeros_like(ds_f)
            ds_b[...] = jnp.zeros_like(ds_b)

        wg_v, bg_v = wg_ref[...], bg_ref[...]
        sides = ((False, (qf, kf, vf, zf, dof, ssf, dqf, dkf, dvf, dpf, ds_f)),
                 (True, (qb, kb, vb, zb, dob, ssb, dqb, dkb, dvb, dpb, ds_b)))
        for rev, (q_r, k_r, v_r, z_r, do_r, ss_r, dq_r, dk_r, dv_r, dp_r, ds) in sides:
            mask = _gla_mask(rev)
            for ck in (range(GLA_G) if rev else range(GLA_G - 1, -1, -1)):
                cr = slice(ck * CHUNK, (ck + 1) * CHUNK)
                tm = _gla_chunk_terms(q_r[cr, :], k_r[cr, :], z_r[cr, :], wg_v, bg_v, rev)
                v = v_r[cr, :]
                d_o = do_r[cr, :]
                db_parts, cross_parts = [], []
                for hd in range(GLA_H):
                    sl, q_b, k_e, q_m, k_m = _gla_head_ops(tm, hd)
                    vs = slice(hd * GLA_DV, (hd + 1) * GLA_DV)
                    v16, do16 = v[:, vs].astype(BF16), d_o[:, vs].astype(BF16)
                    a = jnp.where(mask, _dot(q_m, k_m, NT), 0.0).astype(BF16)
                    da = jnp.where(mask, _dot(do16, v16, NT), 0.0).astype(BF16)
                    s_in = ss_r[ck, hd]
                    s_in16 = s_in.astype(BF16)
                    ds_out = ds[hd]
                    ds16 = ds_out.astype(BF16)
                    dv_r[cr, vs] = _dot(a, do16, TN) + _dot(k_e, ds16, NT)
                    dqs = _dot(da, k_m) * tm["eqm"][:, sl] + _dot(do16, s_in16) * tm["eb"][:, sl]
                    dk = _dot(da, q_m, TN) * tm["ekm"][:, sl] + _dot(v16, ds16) * tm["ek"][:, sl]
                    ds[hd] = ds_out * tm["e_last"][:, sl] + _dot(do16, q_b, TN)
                    dq_r[cr, sl] = dqs
                    dk_r[cr, sl] = dk
                    db_parts.append(tm["qs"][:, sl] * dqs - tm["k"][:, sl] * dk)
                    s_out = s_in * tm["e_last"][:, sl] + _dot(v16, k_e, TN)
                    cross_parts.append(jnp.sum(s_out * ds_out, axis=0, keepdims=True))
                d_b = jnp.concatenate(db_parts, axis=1)
                dg = _cumsum_rows(d_b, not rev) + jnp.concatenate(cross_parts, axis=1)
                dp_r[cr, :] = dg * (1.0 / GLA_GATE_NORM) * _sigmoid(-tm["pre"])

    ng = nc // GLA_G
    rows = CHUNK * GLA_G
    fw = lambda c: (lambda i: (ng - 1 - i, c))
    rv = lambda c: (lambda i: (i, c))
    st_blk = (GLA_G, GLA_H, GLA_DV, GLA_DK)
    def side(ix, st_ix):
        return [pl.BlockSpec((rows, QK), ix(0)), pl.BlockSpec((rows, QK), ix(1)),
                pl.BlockSpec((rows, GLA_W), ix(0)), pl.BlockSpec((rows, LANES), ix(0)),
                pl.BlockSpec((rows, GLA_W), ix(0)), pl.BlockSpec(st_blk, st_ix)]
    def oside(ix):
        return [pl.BlockSpec((rows, QK), ix(0)), pl.BlockSpec((rows, QK), ix(0)),
                pl.BlockSpec((rows, GLA_W), ix(0)), pl.BlockSpec((rows, QK), ix(0))]
    o_shapes = [jax.ShapeDtypeStruct((t_rows, QK), F32), jax.ShapeDtypeStruct((t_rows, QK), F32),
                jax.ShapeDtypeStruct((t_rows, GLA_W), F32), jax.ShapeDtypeStruct((t_rows, QK), F32)]
    res, comm_res = _pcall(
        kern, name, (ng,),
        (side(fw, lambda i: (ng - 1 - i, 0, 0, 0)) + side(rv, lambda i: (i, 0, 0, 0))
         + [pl.BlockSpec(wg.shape, lambda i: (0, 0)), pl.BlockSpec(bg.shape, lambda i: (0, 0))]),
        oside(fw) + oside(rv), o_shapes * 2, [pltpu.VMEM((GLA_H, GLA_DV, GLA_DK), F32)] * 2, ("arbitrary",),
        (z_qk, z_qk, z_v, z_zg, do, s_f, z_qk, z_qk, z_v, z_zg, do, s_b, wg, bg), comm)
    return res if comm is None else (res, comm_res)


FLAT_W = 1024


def _adam_math(wv, gv, mv, vv):
    bc1 = 1.0 - ADAM_B1 ** ADAM_STEP
    bc2 = 1.0 - ADAM_B2 ** ADAM_STEP
    m_new = ADAM_B1 * mv + (1.0 - ADAM_B1) * gv
    v_new = ADAM_B2 * vv + (1.0 - ADAM_B2) * (gv * gv)
    delta = -ADAM_LR * ((m_new / bc1) / (jnp.sqrt(v_new / bc2) + ADAM_EPS) + ADAM_WD * wv)
    return delta, m_new, v_new


def _row_tile(rows, cap=256):
    t = cap
    while rows % t:
        t //= 2
    assert t >= SUBLANES, rows
    return t


def _adamw_layers(w, m, v, g_layers, name, comm=None):
    depth, rows, cols = w.shape
    tr = _row_tile(rows)

    def kern(w_ref, m_ref, v_ref, *rest):
        g_refs, (go_ref, d_ref, mo_ref, vo_ref) = rest[:depth], rest[depth:]
        l = pl.program_id(0)
        gv = g_refs[0][...]
        for k in range(1, depth):
            gv = jnp.where(l == k, g_refs[k][...], gv)
        delta, m_new, v_new = _adam_math(w_ref[...], gv, m_ref[...], v_ref[...])
        go_ref[...] = gv
        d_ref[...] = delta
        mo_ref[...] = m_new
        vo_ref[...] = v_new

    stacked = pl.BlockSpec((None, tr, cols), lambda l, i: (l, i, 0))
    res, comm_res = _pcall(
        kern, name, (depth, rows // tr), [stacked] * 3 + [pl.BlockSpec((tr, cols), lambda l, i: (i, 0))] * depth,
        [stacked] * 4, [jax.ShapeDtypeStruct(w.shape, F32)] * 4, [], ("arbitrary", "arbitrary"),
        (w, m, v, *g_layers), comm)
    return res if comm is None else (res, comm_res)


def _adamw_small(ws, gs, ms, vs):
    n = len(ws)

    def kern(*refs):
        ins, outs = refs[:4 * n], refs[4 * n:]
        for k in range(n):
            delta, m_new, v_new = _adam_math(ins[k][...], ins[n + k][...], ins[2 * n + k][...], ins[3 * n + k][...])
            outs[k][...] = delta
            outs[n + k][...] = m_new
            outs[2 * n + k][...] = v_new

    shapes = [jax.ShapeDtypeStruct(w.shape, F32) for w in ws]
    res = pl.pallas_call(kern, name="adamw_small", out_shape=shapes * 3,
                         compiler_params=pltpu.CompilerParams(vmem_limit_bytes=VMEM_LIMIT))(*ws, *gs, *ms, *vs)
    return res[:n], res[n:2 * n], res[2 * n:]


def _add_sibling(g, got, c_idx, name):
    _, _, rows, cols = g.shape
    tr = _row_tile(rows)

    def kern(c_ref, g_ref, got_ref, o_ref):
        o_ref[...] = (g_ref[...] + got_ref[...]).astype(BF16)

    grid_spec = pltpu.PrefetchScalarGridSpec(
        num_scalar_prefetch=1, grid=(N_CHIPS, rows // tr),
        in_specs=[pl.BlockSpec((None, None, tr, cols), lambda q, i, c_ref: (q, c_ref[0], i, 0)),
                  pl.BlockSpec((None, tr, cols), lambda q, i, c_ref: (q, i, 0))],
        out_specs=pl.BlockSpec((None, tr, cols), lambda q, i, c_ref: (q, i, 0)))
    return pl.pallas_call(
        kern, name=name, grid_spec=grid_spec, out_shape=jax.ShapeDtypeStruct((N_CHIPS, rows, cols), BF16),
        compiler_params=_cparams(("arbitrary", "arbitrary")),
    )(c_idx, g, got)


def _sum_chips(x, name):
    _, rows, cols = x.shape
    tr = _row_tile(rows)

    def kern(x_ref, o_ref):
        xv = x_ref[...].astype(F32)
        o_ref[...] = ((xv[0] + xv[1]) + xv[2]) + xv[3]

    return pl.pallas_call(
        kern, name=name, grid=(rows // tr,),
        in_specs=[pl.BlockSpec((N_CHIPS, tr, cols), lambda i: (0, i, 0))],
        out_specs=pl.BlockSpec((tr, cols), lambda i: (i, 0)),
        out_shape=jax.ShapeDtypeStruct((rows, cols), F32),
        compiler_params=_cparams(("arbitrary",)),
    )(x)


def _place():
    x, y, c = lax.axis_index("x"), lax.axis_index("y"), lax.axis_index("c")
    return x, y, c


def _other_chips(x, y):
    return [(1 - x, y), (x, 1 - y), (1 - x, 1 - y)]


class _Comm:
    def __init__(self, ins, out_shapes, n_sems, stage, body):
        self.ins, self.out_shapes, self.body = list(ins), list(out_shapes), body
        self.scratch = [pltpu.SemaphoreType.DMA((n,)) for n in n_sems] + [pltpu.VMEM(s, d) for s, d in stage]


ANY_SPEC = pl.BlockSpec(memory_space=pl.ANY)


def _run_comm(comm, name):
    def kern(*refs):
        comm.body(refs, True, True)

    return pl.pallas_call(
        kern, name=name, in_specs=[ANY_SPEC] * len(comm.ins), out_specs=[ANY_SPEC] * len(comm.out_shapes),
        out_shape=comm.out_shapes, scratch_shapes=comm.scratch,
        compiler_params=pltpu.CompilerParams(has_side_effects=True, vmem_limit_bytes=VMEM_LIMIT),
    )(*comm.ins)


def _pcall(kern, name, grid, in_specs, out_specs, out_shape, scratch_shapes, sem, args, comm=None):
    if comm is None:
        return pl.pallas_call(kern, name=name, grid=grid, in_specs=in_specs, out_specs=out_specs, out_shape=out_shape,
                              scratch_shapes=scratch_shapes, compiler_params=_cparams(sem))(*args), None
    n_in, n_out, n_scr = len(in_specs), len(out_specs), len(scratch_shapes)
    c_in, c_out = len(comm.ins), len(comm.out_shapes)

    def hosted(*refs):
        a = n_in + c_in
        b = a + n_out + c_out
        main = refs[:n_in] + refs[a:a + n_out] + refs[b:b + n_scr]
        extra = refs[n_in:a] + refs[a + n_out:b] + refs[b + n_scr:]
        ids = [pl.program_id(d) for d in range(len(grid))]
        first, last = ids[0] == 0, ids[0] == grid[0] - 1
        for i, g in zip(ids[1:], grid[1:]):
            first, last = jnp.logical_and(first, i == 0), jnp.logical_and(last, i == g - 1)

        @pl.when(first)
        def _():
            comm.body(extra, True, False)

        kern(*main)

        @pl.when(last)
        def _():
            comm.body(extra, False, True)

    res = pl.pallas_call(
        hosted, name=name, grid=grid, in_specs=list(in_specs) + [ANY_SPEC] * c_in,
        out_specs=list(out_specs) + [ANY_SPEC] * c_out, out_shape=list(out_shape) + comm.out_shapes,
        scratch_shapes=list(scratch_shapes) + comm.scratch,
        compiler_params=pltpu.CompilerParams(dimension_semantics=("arbitrary",) * len(grid),
                                             vmem_limit_bytes=VMEM_LIMIT, has_side_effects=True),
    )(*args, *comm.ins)
    return res[:n_out], res[n_out:]


class _StagedCopies:
    def __init__(self, pairs, bufs, sem_in, sem_out):
        self.ins = [pltpu.make_async_copy(src, buf, sem_in.at[i]) for i, ((src, _), buf) in enumerate(zip(pairs, bufs))]
        self.outs = [pltpu.make_async_copy(buf, dst, sem_out.at[i]) for i, ((_, dst), buf) in enumerate(zip(pairs, bufs))]

    def load(self):
        for cp in self.ins:
            cp.start()

    def store(self):
        for cp in self.ins:
            cp.wait()
        for cp in self.outs:
            cp.start()

    def finish(self):
        for cp in self.outs:
            cp.wait()


DMA_CHUNK_BYTES = 1 << 20
DMA_MAX_CHUNKS = 4


def _row_chunks(rows, row_bytes, align=16):
    units = rows // align
    k = max(1, min(DMA_MAX_CHUNKS, -(-rows * row_bytes // DMA_CHUNK_BYTES), units))
    out, start = [], 0
    for i in range(k):
        size = (units // k + (1 if i < units % k else 0)) * align
        out.append((start, size))
        start += size
    if start < rows:
        out[-1] = (out[-1][0], out[-1][1] + rows - start)
    return out


def _row_bytes(shape, dtype):
    return math.prod(shape[1:]) * jnp.dtype(dtype).itemsize


def _remote(src, dst, send, recv, idx, dev):
    return pltpu.make_async_remote_copy(src, dst, send.at[idx], recv.at[idx], device_id=dev, device_id_type=MESH_ID)


def _allgather_chips(xs):
    na = len(xs)
    chunks = [_row_chunks(x.shape[1], _row_bytes(x.shape[1:], x.dtype)) for x in xs]
    n_cp = 3 * sum(len(ch) for ch in chunks)

    def body(refs, start, finish):
        x_refs, o_refs = refs[:na], refs[na:2 * na]
        send1, recv1, send2, recv2, loc_in, loc_out = refs[2 * na:2 * na + 6]
        px, py, c = _place()
        me = 2 * px + py
        sib = (px, py, 1 - c)
        own = _StagedCopies([(x, o.at[me]) for x, o in zip(x_refs, o_refs)], refs[2 * na + 6:], loc_in, loc_out)
        plan = [(a, qx, qy, pl.ds(s, n)) for a in range(na) for (qx, qy) in _other_chips(px, py) for (s, n) in chunks[a]]
        first = [_remote(x_refs[a].at[c, rows], o_refs[a].at[me, c, rows], send1, recv1, i, (qx, qy, c))
                 for i, (a, qx, qy, rows) in enumerate(plan)]
        if start:
            own.load()
            for cp in first:
                cp.start()
        if finish:
            own.store()
            passed = []
            for i, (a, qx, qy, rows) in enumerate(plan):
                slot = o_refs[a].at[2 * qx + qy, c, rows]
                _remote(slot, slot, send1, recv1, i, (qx, qy, c)).wait_recv()
                fwd = _remote(slot, slot, send2, recv2, i, sib)
                fwd.start()
                passed.append(fwd)
            for i, (a, qx, qy, rows) in enumerate(plan):
                slot = o_refs[a].at[2 * qx + qy, 1 - c, rows]
                _remote(slot, slot, send2, recv2, i, sib).wait_recv()
            for cp in first + passed:
                cp.wait_send()
            own.finish()

    outs = [jax.ShapeDtypeStruct((N_CHIPS,) + x.shape, x.dtype) for x in xs]
    return _Comm(xs, outs, [n_cp, n_cp, n_cp, n_cp, na, na], [(x.shape, x.dtype) for x in xs], body)


def _sibling_halves(gs):
    na = len(gs)
    chunks = [_row_chunks(g.shape[2], _row_bytes(g.shape[2:], g.dtype)) for g in gs]
    n_cp = N_CHIPS * sum(len(ch) for ch in chunks)

    def body(refs, start, finish):
        g_refs, o_refs = refs[:na], refs[na:2 * na]
        send, recv = refs[2 * na:]
        px, py, c = _place()
        plan = [(a, q, pl.ds(s, n)) for a in range(na) for q in range(N_CHIPS) for (s, n) in chunks[a]]
        cps = [_remote(g_refs[a].at[q, 1 - c, rows], o_refs[a].at[q, rows], send, recv, i, (px, py, 1 - c))
               for i, (a, q, rows) in enumerate(plan)]
        if start:
            for cp in cps:
                cp.start()
        if finish:
            for cp in cps:
                cp.wait()

    outs = [jax.ShapeDtypeStruct((N_CHIPS,) + g.shape[2:], g.dtype) for g in gs]
    return _Comm(gs, outs, [n_cp, n_cp], [], body)


def _chip_exchange(ps):
    na = len(ps)
    chunks = [_row_chunks(p.shape[1], _row_bytes(p.shape[1:], p.dtype)) for p in ps]
    n_cp = 3 * sum(len(ch) for ch in chunks)

    def body(refs, start, finish):
        p_refs, o_refs = refs[:na], refs[na:2 * na]
        send, recv, loc_in, loc_out = refs[2 * na:2 * na + 4]
        px, py, c = _place()
        me = 2 * px + py
        own = _StagedCopies([(p.at[me], o.at[me]) for p, o in zip(p_refs, o_refs)], refs[2 * na + 4:], loc_in, loc_out)
        plan = [(a, qx, qy, pl.ds(s, n)) for a in range(na) for (qx, qy) in _other_chips(px, py) for (s, n) in chunks[a]]
        cps = [_remote(p_refs[a].at[2 * qx + qy, rows], o_refs[a].at[me, rows], send, recv, i, (qx, qy, c))
               for i, (a, qx, qy, rows) in enumerate(plan)]
        if start:
            own.load()
            for cp in cps:
                cp.start()
        if finish:
            own.store()
            for cp in cps:
                cp.wait()
            own.finish()

    outs = [jax.ShapeDtypeStruct(p.shape, p.dtype) for p in ps]
    return _Comm(ps, outs, [n_cp, n_cp, na, na], [(p.shape[1:], p.dtype) for p in ps], body)


def _sibling_join(rs):
    na = len(rs)
    chunks = [_row_chunks(r.shape[0], _row_bytes(r.shape, r.dtype)) for r in rs]
    n_cp = sum(len(ch) for ch in chunks)

    def body(refs, start, finish):
        r_refs, o_refs = refs[:na], refs[na:2 * na]
        send, recv, loc_in, loc_out = refs[2 * na:2 * na + 4]
        px, py, c = _place()
        own = _StagedCopies([(r, o.at[c]) for r, o in zip(r_refs, o_refs)], refs[2 * na + 4:], loc_in, loc_out)
        plan = [(a, pl.ds(s, n)) for a in range(na) for (s, n) in chunks[a]]
        cps = [_remote(r_refs[a].at[rows], o_refs[a].at[c, rows], send, recv, i, (px, py, 1 - c))
               for i, (a, rows) in enumerate(plan)]
        if start:
            own.load()
            for cp in cps:
                cp.start()
        if finish:
            own.store()
            for cp in cps:
                cp.wait()
            own.finish()

    outs = [jax.ShapeDtypeStruct((2,) + r.shape, r.dtype) for r in rs]
    return _Comm(rs, outs, [n_cp, n_cp, na, na], [(r.shape, r.dtype) for r in rs], body)


def _reduce_stages(gs, c_idx, tag):
    got = yield _sibling_halves(gs)
    part = [_add_sibling(g, o, c_idx, f"reduce_add_sibling{tag}_{a}") for a, (g, o) in enumerate(zip(gs, got))]
    landed = yield _chip_exchange(part)
    mine = [_sum_chips(x, f"reduce_sum_chips{tag}_{a}") for a, x in enumerate(landed)]
    return (yield _sibling_join(mine))


class _Staged:
    def __init__(self, gen, tag):
        self.gen, self.tag, self.k, self.value = gen, tag, 0, None
        self.cur = next(gen)

    def stage(self):
        return self.cur

    def done(self, results):
        self.k += 1
        try:
            self.cur = self.gen.send(results)
        except StopIteration as stop:
            self.cur, self.value = None, stop.value

    def drain(self):
        while self.cur is not None:
            self.done(_run_comm(self.cur, f"comm{self.tag}_{self.k}"))
        return self.value


WEIGHTS = ["meta_tokens", "norm_mix_pre", "norm_mix_post", "norm_mlp_pre", "norm_mlp_post", "w_in", "conv_w",
           "conv_b", "lru_wa_f", "lru_ba_f", "lru_wx_f", "lru_bx_f", "lru_lambda_f", "lru_wa_b", "lru_ba_b",
           "lru_wx_b", "lru_bx_b", "lru_lambda_b", "gla_wg_f", "gla_bg_f", "gla_wg_b", "gla_bg_b", "gla_head_norm",
           "w_out", "w_mlp_up", "w_mlp_down"]
BIG = ("w_in", "w_out", "w_mlp_up", "w_mlp_down")
SMALL = [n for n in WEIGHTS if n not in BIG]
SMALL_SHARDED = {"meta_tokens": 1, "conv_w": 2, "gla_wg_f": 2, "gla_wg_b": 2}
N_CHIPS = 4
SMALL_RAW = [("g1", (1, D_MODEL)), ("g2", (1, D_MODEL)), ("g3", (1, D_MODEL)), ("g4", (1, D_MODEL)),
             ("conv_w", (4, LRU_W)), ("conv_b", (1, LRU_W)), ("lru_wg", (LRU_HEADS, LRU_HD, 4 * LRU_HD)),
             ("lru_pb", (SUBLANES, LRU_W)), ("gla_wg", (2 * GLA_RANK, 2 * QK)), ("gla_bg", (1, 2 * QK)),
             ("head_norm", (1, GLA_W))]
META_SHAPE = (N_META, D_MODEL)


def _pad_to(v, n):
    return jnp.pad(v, (0, n - v.shape[0]))


def _round_up(n, m):
    return -(-n // m) * m


def _flat_cat(arrs, total):
    return _pad_to(jnp.concatenate([a.reshape(-1) for a in arrs]), total)


def _split_flat(flat, shapes):
    out, off = [], 0
    for s in shapes:
        n = math.prod(s)
        out.append(flat[off:off + n].reshape(s))
        off += n
    return out


def _my_shard(full, axis, chip):
    n = full.shape[axis] // N_CHIPS
    return lax.dynamic_slice_in_dim(full, chip * n, n, axis=axis)


def _block_diag_gates(wa_f, wx_f, wa_b, wx_b):
    eye2 = jnp.eye(2, dtype=wa_f.dtype)

    def bd(w):
        w4 = w.reshape(4, 2, LRU_HD, LRU_HD)
        return (w4[:, :, :, None, :] * eye2[None, :, None, :, None]).reshape(4, LANES, LANES)
    return jnp.concatenate([bd(wa_f), bd(wx_f), bd(wa_b), bd(wx_b)], axis=2).astype(BF16)


def _gate_diag_blocks(dwh, o):
    return dwh[:, :, o * LRU_HD:(o + 1) * LRU_HD]


def kernel(x, meta_tokens, norm_mix_pre, norm_mix_post, norm_mlp_pre, norm_mlp_post, w_in, conv_w, conv_b, lru_wa_f, lru_ba_f, lru_wx_f, lru_bx_f, lru_lambda_f, lru_wa_b, lru_ba_b, lru_wx_b, lru_bx_b, lru_lambda_b, gla_wg_f, gla_bg_f, gla_wg_b, gla_bg_b, gla_head_norm, w_out, w_mlp_up, w_mlp_down, loss_target, m_meta_tokens, m_norm_mix_pre, m_norm_mix_post, m_norm_mlp_pre, m_norm_mlp_post, m_w_in, m_conv_w, m_conv_b, m_lru_wa_f, m_lru_ba_f, m_lru_wx_f, m_lru_bx_f, m_lru_lambda_f, m_lru_wa_b, m_lru_ba_b, m_lru_wx_b, m_lru_bx_b, m_lru_lambda_b, m_gla_wg_f, m_gla_bg_f, m_gla_wg_b, m_gla_bg_b, m_gla_head_norm, m_w_out, m_w_mlp_up, m_w_mlp_down, v_meta_tokens, v_norm_mix_pre, v_norm_mix_post, v_norm_mlp_pre, v_norm_mlp_post, v_w_in, v_conv_w, v_conv_b, v_lru_wa_f, v_lru_ba_f, v_lru_wx_f, v_lru_bx_f, v_lru_lambda_f, v_lru_wa_b, v_lru_ba_b, v_lru_wx_b, v_lru_bx_b, v_lru_lambda_b, v_gla_wg_f, v_gla_bg_f, v_gla_wg_b, v_gla_bg_b, v_gla_head_norm, v_w_out, v_w_mlp_up, v_w_mlp_down):
    args = dict(locals())
    w_loc = {n: args[n] for n in WEIGHTS}
    m_loc = {n: args["m_" + n] for n in WEIGHTS}
    v_loc = {n: args["v_" + n] for n in WEIGHTS}
    seq = x.shape[1]
    t_rows = FRONT + seq + BACK
    assert t_rows % 768 == 0 and seq % FRONT == 0, seq
    chip = 2 * lax.axis_index("x") + lax.axis_index("y")
    c_idx = lax.axis_index("c").astype(jnp.int32).reshape(1)

    def halves(a):
        return a.reshape((2, a.shape[0] // 2) + a.shape[1:])

    big16 = {n: w_loc[n].astype(BF16) for n in BIG}
    sm_names = list(SMALL_SHARDED)
    sm_len = _round_up(sum(w_loc[n].size for n in sm_names), 2 * SUBLANES * FLAT_W)
    sm_pack = _flat_cat([w_loc[n] for n in sm_names], sm_len).reshape(2, -1, FLAT_W)
    g_in0, g_small = _run_comm(_allgather_chips([halves(big16["w_in"][0]), sm_pack]), "gather_first")
    sm_parts = [_split_flat(g_small[q].reshape(-1), [w_loc[n].shape for n in sm_names]) for q in range(N_CHIPS)]
    full = {n: jnp.concatenate([sm_parts[q][i] for q in range(N_CHIPS)], axis=SMALL_SHARDED[n])
            for i, n in enumerate(sm_names)}
    for n in SMALL:
        if n not in SMALL_SHARDED:
            full[n] = w_loc[n]

    def full_w_in(g):
        w = jnp.transpose(g.reshape(N_CHIPS, D_MODEL, -1), (1, 0, 2)).reshape(D_MODEL, D_IN)
        return jnp.pad(w, ((0, 0), (0, D_IN_PAD - D_IN)))

    def full_rest(g_out, g_up, g_down):
        return dict(w_out=g_out.reshape(-1, D_MODEL),
                    w_up=jnp.transpose(g_up.reshape(N_CHIPS, D_MODEL, -1), (1, 0, 2)).reshape(D_MODEL, D_FF),
                    w_down=g_down.reshape(D_FF, D_MODEL))

    later_gathers = [_allgather_chips([halves(big16[n][0]) for n in BIG[1:]]),
                     _allgather_chips([halves(big16[n][l]) for l in range(1, DEPTH) for n in BIG])]

    meta_blk = jnp.concatenate([jnp.zeros((FRONT - N_META, D_MODEL), F32), full["meta_tokens"]], axis=0)
    h = jnp.concatenate([meta_blk, x[0], jnp.zeros((BACK, D_MODEL), F32)], axis=0)
    target = loss_target[0]

    layer_w = []
    for l in range(DEPTH):
        wg = jnp.zeros((LANES, 2 * QK), F32)
        wg = wg.at[0:GLA_RANK, 0:QK].set(full["gla_wg_f"][l]).at[GLA_RANK:2 * GLA_RANK, QK:].set(full["gla_wg_b"][l])
        pb = jnp.stack([full["lru_ba_f"][l], full["lru_bx_f"][l], full["lru_ba_b"][l], full["lru_bx_b"][l],
                        full["lru_lambda_f"][l], full["lru_lambda_b"][l], jnp.zeros((LRU_W,), F32),
                        jnp.zeros((LRU_W,), F32)])
        layer_w.append(dict(
            g1=full["norm_mix_pre"][l][None], g2=full["norm_mix_post"][l][None],
            g3=full["norm_mlp_pre"][l][None], g4=full["norm_mlp_post"][l][None],
            conv_w=full["conv_w"][l], conv_b=full["conv_b"][l][None],
            lru_wg=_block_diag_gates(full["lru_wa_f"][l], full["lru_wx_f"][l], full["lru_wa_b"][l], full["lru_wx_b"][l]),
            lru_pb=pb, gla_wg=wg.astype(BF16),
            gla_bg=jnp.concatenate([full["gla_bg_f"][l], full["gla_bg_b"][l]])[None],
            head_norm=full["gla_head_norm"][l][None]))
    layer_w[0]["w_in"] = full_w_in(g_in0)

    def split_z(acc):
        return (acc[:, 0:1024], acc[:, 1024:1536], acc[:, 1536:2048], acc[:, 2048:2560], acc[:, ZG_OFF:ZG_OFF + LANES])

    def relu2(acc):
        r = jnp.maximum(acc, 0.0)
        return acc, r * r

    tm_e = 768
    tm_ff = 1408 if t_rows % 1408 == 0 else 768
    tk_dw = 2816 if t_rows % 2816 == 0 else 768

    def post_mix(acc, hh, g2, g3):
        h1 = hh + _rms_fwd(acc, g2)
        return acc, h1, _rms_fwd(h1, g3)

    def post_mlp(acc, h1, g4, g1_next):
        h2 = h1 + _rms_fwd(acc, g4)
        return acc, h2, _rms_fwd(h2, g1_next)

    def post_mlp_loss(acc, h1, tgt, x_rows, g4):
        h2 = h1 + _rms_fwd(acc, g4)
        err = (h2 - tgt) * x_rows[:, 0:1]
        d_h = err * (1.0 / D_MODEL)
        dff, dg4 = _rms_bwd(acc, g4, d_h)
        loss_part = jnp.zeros((SUBLANES, LANES), F32) + jnp.sum(err * err) * (0.5 / D_MODEL)
        return d_h, dff, loss_part, dg4

    row3 = [(D_MODEL, F32), (D_MODEL, F32), (D_MODEL, BF16)]
    vec = ((1, D_MODEL), F32)
    target_p = jnp.concatenate([jnp.zeros((FRONT, D_MODEL), F32), target, jnp.zeros((BACK, D_MODEL), F32)], axis=0)
    x_rows = jnp.concatenate([jnp.zeros((FRONT, LANES), F32), jnp.ones((seq, LANES), F32), jnp.zeros((BACK, LANES), F32)])
    saved = []
    n1 = _norm_cast(h, layer_w[0]["g1"], t_rows, "norm_mix_pre_l0")
    for l in range(DEPTH):
        lw = layer_w[l]
        tag = f"_l{l}"
        z_lru, z_qk, z_v, z_go, z_zg = _mm(
            n1, lw["w_in"], "nn", 768, D_IN_PAD, D_MODEL,
            [(1024, F32), (512, F32), (512, F32), (512, F32), (LANES, F32)], "in_proj" + tag, epilogue=split_z)
        lru_args = (z_lru, lw["conv_w"], lw["conv_b"], lw["lru_wg"], lw["lru_pb"], t_rows, seq, "lru_fwd" + tag)
        gla_args = (z_qk, z_v, z_zg, lw["gla_wg"], lw["gla_bg"], t_rows, "gla_fwd" + tag)
        if l == 0:
            (h_f, h_b), g_rest = _lru_fwd(*lru_args, comm=later_gathers[0])
            lw.update(full_rest(*g_rest))
            (o_f, o_b, s_f, s_b), g_next = _gla_fwd(*gla_args, comm=later_gathers[1])
            for l2 in range(1, DEPTH):
                g_l2 = g_next[(l2 - 1) * len(BIG):l2 * len(BIG)]
                layer_w[l2].update(full_rest(*g_l2[1:]), w_in=full_w_in(g_l2[0]))
        else:
            h_f, h_b = _lru_fwd(*lru_args)
            o_f, o_b, s_f, s_b = _gla_fwd(*gla_args)
        ymix = _mix_fwd(h_f, h_b, z_lru, o_f, o_b, z_go, lw["head_norm"], t_rows, "mix_fwd" + tag)
        mix, h1, n3 = _mm(ymix, lw["w_out"], "nn", 768, D_MODEL, D_MODEL, row3, "out_proj" + tag, epilogue=post_mix,
                          rows_in=(h,), consts=(lw["g2"], lw["g3"]))
        u, act = _mm(n3, lw["w_up"], "nn", tm_ff, 1024, D_MODEL, [(D_FF, BF16), (D_FF, BF16)], "mlp_up" + tag,
                     epilogue=relu2)
        sv = dict(h=h, n1=n1, z_lru=z_lru, z_qk=z_qk, z_v=z_v, z_go=z_go, z_zg=z_zg, h_f=h_f, h_b=h_b,
                  o_f=o_f, o_b=o_b, s_f=s_f, s_b=s_b, ymix=ymix, mix=mix, h1=h1, n3=n3, u=u, act=act)
        if l + 1 < DEPTH:
            sv["ff"], h, n1 = _mm(act, lw["w_down"], "nn", tm_e, D_MODEL, D_FF, row3, "mlp_down" + tag,
                                  epilogue=post_mlp, rows_in=(h1,), consts=(lw["g4"], layer_w[l + 1]["g1"]))
        else:
            dh, dff, loss_part, dg4 = _mm(act, lw["w_down"], "nn", tm_e, D_MODEL, D_FF,
                                          [(D_MODEL, F32), (D_MODEL, BF16)], "mlp_down_loss" + tag,
                                          epilogue=post_mlp_loss, rows_in=(h1, target_p, x_rows), consts=(lw["g4"],),
                                          accs=[((SUBLANES, LANES), F32), vec])
        saved.append(sv)

    loss = lax.psum(loss_part[0, 0], ("x", "y", "c"))

    small_len = _round_up(sum(math.prod(s) for _, s in SMALL_RAW) + math.prod(META_SHAPE),
                          N_CHIPS * 2 * 2 * SUBLANES * FLAT_W)
    totals = [None] * DEPTH
    pend = None

    def with_stage(staged, fn):
        comm = staged.stage() if staged is not None else None
        if comm is None:
            return fn(None)
        res, comm_res = fn(comm)
        staged.done(comm_res)
        return res

    for l in reversed(range(DEPTH)):
        lw, sv = layer_w[l], saved[l]
        tag = f"_l{l}"
        raw = {"g4": dg4}
        gw_down = with_stage(pend, lambda comm: _mm(sv["act"], dff, "tn", 1024, D_MODEL, tk_dw, [(D_MODEL, F32)],
                                                    "dw_down" + tag, comm=comm))[0]
        du = _mm(dff, lw["w_down"], "nt", tm_ff, 1024, D_MODEL, [(D_FF, BF16)], "d_act" + tag,
                 epilogue=lambda acc, uu: (acc * 2.0 * jnp.maximum(uu.astype(F32), 0.0),), extras=(sv["u"],))[0]
        gw_up = _mm(sv["n3"], du, "tn", D_MODEL, D_FF // N_CHIPS, tk_dw, [(D_FF, F32)], "dw_up" + tag,
                    col_blocks_first=True)[0]

        def pre_mlp_bwd(acc, h1, d_h, mix, g3, g2):
            dx, dg3 = _rms_bwd(h1, g3, acc)
            d_h1 = d_h + dx
            d_mix, dg2 = _rms_bwd(mix, g2, d_h1)
            return d_h1, d_mix, dg3, dg2

        dh1, dmix, raw["g3"], raw["g2"] = _mm(
            du, lw["w_up"], "nt", tm_e, D_MODEL, D_FF, [(D_MODEL, F32), (D_MODEL, BF16)], "d_n3" + tag,
            epilogue=pre_mlp_bwd, rows_in=(sv["h1"], dh, sv["mix"]), consts=(lw["g3"], lw["g2"]), accs=[vec, vec])
        gw_out = _mm(sv["ymix"], dmix, "tn", D_MODEL, D_MODEL, tk_dw, [(D_MODEL, F32)], "dw_out" + tag)[0]
        mats = [gw_out.reshape(N_CHIPS, 2, -1, D_MODEL), gw_up.reshape(N_CHIPS, 2, D_MODEL // 2, D_FF // N_CHIPS),
                gw_down.reshape(N_CHIPS, 2, -1, D_MODEL)]
        early = _Staged(_reduce_stages(mats, c_idx, tag + "e"), tag + "e") if l == 0 else None
        half4 = [(LRU_W, F32), (LRU_W, F32), (GLA_W, F32), (GLA_W, F32)]
        dh_lru, dgate, do, dgo, raw["head_norm"] = with_stage(early, lambda comm: _mm(
            dmix, lw["w_out"], "nt", 768, D_MODEL, D_MODEL, half4, "d_ymix" + tag, epilogue=_mix_bwd_tile,
            rows_in=(sv["h_f"], sv["h_b"], sv["z_lru"], sv["o_f"], sv["o_b"], sv["z_go"]), consts=(lw["head_norm"],),
            accs=[((1, GLA_W), F32)], comm=comm))
        gla_grads = with_stage(pend, lambda comm: _gla_bwd(sv["z_qk"], sv["z_v"], sv["z_zg"], lw["gla_wg"],
                                                           lw["gla_bg"], do, sv["s_f"], sv["s_b"], t_rows,
                                                           "gla_bwd" + tag, comm=comm))
        dx_br, raw["conv_w"], raw["conv_b"], raw["lru_wg"], raw["lru_pb"] = with_stage(early, lambda comm: _lru_bwd(
            sv["z_lru"], lw["conv_w"], lw["conv_b"], lw["lru_wg"], lw["lru_pb"], sv["h_f"], sv["h_b"], dh_lru,
            t_rows, seq, "lru_bwd" + tag, comm=comm))
        dz, dwg_gla, raw["gla_bg"] = _dz_assemble(dx_br, dgate, gla_grads, dgo, sv["z_zg"], lw["gla_wg"], t_rows,
                                                  "dz_assemble" + tag)
        raw["gla_wg"] = dwg_gla[0:2 * GLA_RANK]
        gw_in = with_stage(pend, lambda comm: _mm(sv["n1"], dz, "tn", D_MODEL, 896, tk_dw, [(D_IN_PAD, F32)],
                                                  "dw_in" + tag, comm=comm))[0]
        if l > 0:
            def pre_mix_bwd(acc, hh, d_h1, ff_prev, g1, g4_prev):
                dx, dg1 = _rms_bwd(hh, g1, acc)
                d_h = d_h1 + dx
                d_ff, dg4_prev = _rms_bwd(ff_prev, g4_prev, d_h)
                return d_h, d_ff, dg1, dg4_prev

            dh, dff, raw["g1"], dg4 = _mm(
                dz, lw["w_in"], "nt", tm_e, D_MODEL, D_IN_PAD, [(D_MODEL, F32), (D_MODEL, BF16)], "d_n1" + tag,
                epilogue=pre_mix_bwd, rows_in=(sv["h"], dh1, saved[l - 1]["ff"]),
                consts=(lw["g1"], layer_w[l - 1]["g4"]), accs=[vec, vec])
        else:
            def pre_mix_bwd0(acc, hh, d_h1, g1):
                dx, dg1 = _rms_bwd(hh, g1, acc)
                return d_h1 + dx, dg1

            dh, raw["g1"] = with_stage(early, lambda comm: _mm(
                dz, lw["w_in"], "nt", tm_e, D_MODEL, D_IN_PAD, [(D_MODEL, F32)], "d_n1" + tag, epilogue=pre_mix_bwd0,
                rows_in=(sv["h"], dh1), consts=(lw["g1"],), accs=[vec], comm=comm))

        cols = D_IN // N_CHIPS
        in_sh = jnp.stack([gw_in[:, q * cols:(q + 1) * cols] for q in range(N_CHIPS)])
        meta_g = dh[FRONT - N_META:FRONT] if l == 0 else jnp.zeros(META_SHAPE, F32)
        small = _flat_cat([raw[n] for n, _ in SMALL_RAW] + [meta_g], small_len)
        rest = [in_sh.reshape(N_CHIPS, 2, D_MODEL // 2, cols), small.reshape(N_CHIPS, 2, -1, FLAT_W)]
        if pend is not None:
            totals[l + 1] = pend.drain()
        if early is None:
            pend = _Staged(_reduce_stages([rest[0]] + mats + [rest[1]], c_idx, tag), tag)
        else:
            totals[l] = [None] + list(early.drain()) + [None]
            late = _Staged(_reduce_stages(rest, c_idx, tag), tag)

    grad_x = dh[FRONT:FRONT + seq][None]

    delta, new_m, new_v, g_tot = {}, {}, {}, {}
    totals[0][0], totals[0][4] = late.drain()

    def adamw_matrix(a, n):
        shp = w_loc[n].shape
        flat3 = lambda t: t.reshape(DEPTH, -1, shp[-1])
        g_l = [totals[l][a].reshape(-1, shp[-1]) for l in range(DEPTH)]
        outs = _adamw_layers(flat3(w_loc[n]), flat3(m_loc[n]), flat3(v_loc[n]), g_l, "adamw_" + n)
        g_tot[n], delta[n], new_m[n], new_v[n] = [o.reshape(shp) for o in outs]

    sm_all = _run_comm(_allgather_chips([totals[l][4] for l in range(DEPTH)]), "gather_small_grads")
    per_layer = []
    for l in range(DEPTH):
        pieces = _split_flat(sm_all[l].reshape(-1), [s for _, s in SMALL_RAW] + [META_SHAPE])
        per_layer.append(dict(zip([n for n, _ in SMALL_RAW] + ["meta"], pieces)))
    stack = lambda f: jnp.stack([f(per_layer[l]) for l in range(DEPTH)])
    g_tot["meta_tokens"] = _my_shard(per_layer[0]["meta"], 1, chip)
    for n, key in (("norm_mix_pre", "g1"), ("norm_mix_post", "g2"), ("norm_mlp_pre", "g3"), ("norm_mlp_post", "g4"),
                   ("conv_b", "conv_b"), ("gla_head_norm", "head_norm")):
        g_tot[n] = stack(lambda d, key=key: d[key][0])
    g_tot["conv_w"] = _my_shard(stack(lambda d: d["conv_w"]), 2, chip)
    for o, n in enumerate(("lru_wa_f", "lru_wx_f", "lru_wa_b", "lru_wx_b")):
        g_tot[n] = stack(lambda d, o=o: _gate_diag_blocks(d["lru_wg"], o))
    for row, n in enumerate(("lru_ba_f", "lru_bx_f", "lru_ba_b", "lru_bx_b", "lru_lambda_f", "lru_lambda_b")):
        g_tot[n] = stack(lambda d, row=row: d["lru_pb"][row])
    g_tot["gla_wg_f"] = _my_shard(stack(lambda d: d["gla_wg"][0:GLA_RANK, 0:QK]), 2, chip)
    g_tot["gla_wg_b"] = _my_shard(stack(lambda d: d["gla_wg"][GLA_RANK:, QK:]), 2, chip)
    g_tot["gla_bg_f"] = stack(lambda d: d["gla_bg"][0, 0:QK])
    g_tot["gla_bg_b"] = stack(lambda d: d["gla_bg"][0, QK:])

    for a, n in enumerate(BIG):
        adamw_matrix(a, n)
    outs = _adamw_small(*[[d[n] for n in SMALL] for d in (w_loc, g_tot, m_loc, v_loc)])
    for d, o in zip((delta, new_m, new_v), outs):
        d.update(zip(SMALL, o))
    return (loss, grad_x, *[g_tot[n] for n in WEIGHTS], *[delta[n] for n in WEIGHTS],
            *[new_m[n] for n in WEIGHTS], *[new_v[n] for n in WEIGHTS])
```

```python
import math

import jax
import jax.numpy as jnp
from jax import lax
from jax.experimental import pallas as pl
from jax.experimental.pallas import tpu as pltpu

F32 = jnp.float32
BF16 = jnp.bfloat16

D_MODEL = 1024
DEPTH = 2
N_META = 16
LRU_W = 512
LRU_HEADS = 8
LRU_HD = 64
LRU_C = 8.0
GLA_W = 512
GLA_H = 4
GLA_DK = 64
GLA_DV = 128
GLA_RANK = 16
GLA_GATE_NORM = 16.0
D_FF = 4096
D_IN = 2592
EPS = 1e-6
ADAM_LR, ADAM_B1, ADAM_B2, ADAM_EPS, ADAM_WD, ADAM_STEP = 0.001, 0.9, 0.999, 1e-08, 0.01, 10

LANES = 128
SUBLANES = 8
FRONT = 128
BACK = 128
D_IN_PAD = 2688
ZG_OFF = 2560
CHUNK = 64
EXP_CLAMP = 80.0
VMEM_LIMIT = 56 * 1024 * 1024
MESH_ID = pl.DeviceIdType.MESH

NN = (((1,), (0,)), ((), ()))
NT = (((1,), (1,)), ((), ()))
TN = (((0,), (0,)), ((), ()))


def _dot(a, b, dims=NN):
    return lax.dot_general(a, b, dims, preferred_element_type=F32)


def _cparams(sem):
    return pltpu.CompilerParams(dimension_semantics=sem, vmem_limit_bytes=VMEM_LIMIT)


def _sigmoid(x):
    return 1.0 / (1.0 + jnp.exp(-x))


def _gelu(x):
    c = math.sqrt(2.0 / math.pi)
    return 0.5 * x * (1.0 + jnp.tanh(c * (x + 0.044715 * x * x * x)))


def _gelu_grad(x):
    c = math.sqrt(2.0 / math.pi)
    t = jnp.tanh(c * (x + 0.044715 * x * x * x))
    return 0.5 * (1.0 + t) + 0.5 * x * (1.0 - t * t) * c * (1.0 + 3.0 * 0.044715 * x * x)


def _rms_fwd(x, g):
    rstd = lax.rsqrt(jnp.mean(x * x, axis=1, keepdims=True) + EPS)
    return (x * rstd) * g


def _rms_bwd(x, g, dy):
    rstd = lax.rsqrt(jnp.mean(x * x, axis=1, keepdims=True) + EPS)
    xh = x * rstd
    dxh = dy * g
    dx = rstd * (dxh - xh * jnp.mean(dxh * xh, axis=1, keepdims=True))
    dg = jnp.sum(dy * xh, axis=0, keepdims=True)
    return dx, dg


def _mm(a, b, mode, tm, tn, tk, outs, name, epilogue=None, extras=(), col_blocks_first=False,
        rows_in=(), consts=(), accs=(), comm=None):
    if mode == "nn":
        (m, k), n = a.shape, b.shape[1]
        a_spec = pl.BlockSpec((tm, tk), lambda i, j, kk: (i, kk))
        b_spec = pl.BlockSpec((tk, tn), lambda i, j, kk: (kk, j))
        dims = NN
    elif mode == "nt":
        (m, k), n = a.shape, b.shape[0]
        a_spec = pl.BlockSpec((tm, tk), lambda i, j, kk: (i, kk))
        b_spec = pl.BlockSpec((tn, tk), lambda i, j, kk: (j, kk))
        dims = NT
    else:
        (k, m), n = a.shape, b.shape[1]
        a_spec = pl.BlockSpec((tk, tm), lambda i, j, kk: (kk, i))
        b_spec = pl.BlockSpec((tk, tn), lambda i, j, kk: (kk, j))
        dims = TN
    assert m % tm == 0 and n % tn == 0 and k % tk == 0, (name, m, n, k)
    nk = k // tk
    if n == tn and nk == 1:
        b_spec = pl.BlockSpec(b_spec.block_shape, b_spec.index_map, pipeline_mode=pl.Buffered(1))
    ne = len(extras) + len(rows_in) + len(consts)
    e_specs = [pl.BlockSpec((tm, tn), lambda i, j, kk: (i, j)) for _ in extras]
    e_specs += [pl.BlockSpec((tm, r.shape[1]), lambda i, j, kk: (i, 0)) for r in rows_in]
    e_specs += [pl.BlockSpec(c.shape, lambda i, j, kk: (0, 0)) for c in consts]
    n_out = len(outs)
    o_specs, o_shapes = [], []
    for width, dtype in outs:
        if col_blocks_first:
            assert width == n, name
            o_specs.append(pl.BlockSpec((None, tm, tn), lambda i, j, kk: (j, i, 0)))
            o_shapes.append(jax.ShapeDtypeStruct((n // tn, m, tn), dtype))
            continue
        if width == n:
            o_specs.append(pl.BlockSpec((tm, tn), lambda i, j, kk: (i, j)))
        else:
            assert n == tn, name
            o_specs.append(pl.BlockSpec((tm, width), lambda i, j, kk: (i, 0)))
        o_shapes.append(jax.ShapeDtypeStruct((m, width), dtype))
    o_specs += [pl.BlockSpec(s, lambda i, j, kk: (0, 0)) for s, _ in accs]
    o_shapes += [jax.ShapeDtypeStruct(s, d) for s, d in accs]

    def finish(acc, extra_refs, out_refs):
        res = epilogue(acc, *[e[...] for e in extra_refs]) if epilogue else (acc,)
        for o, r in zip(out_refs[:n_out], res[:n_out]):
            o[...] = r.astype(o.dtype)
        first = jnp.logical_and(pl.program_id(0) == 0, pl.program_id(1) == 0)
        for o, r in zip(out_refs[n_out:], res[n_out:]):
            @pl.when(first)
            def _(o=o):
                o[...] = jnp.zeros_like(o)

            o[...] += r

    if nk == 1:
        def body(a_ref, b_ref, *rest):
            finish(_dot(a_ref[...], b_ref[...], dims), rest[:ne], rest[ne:])
        scratch = []
    else:
        def body(a_ref, b_ref, *rest):
            acc_ref = rest[-1]
            kk = pl.program_id(2)

            @pl.when(kk == 0)
            def _():
                acc_ref[...] = jnp.zeros_like(acc_ref)

            acc_ref[...] += _dot(a_ref[...], b_ref[...], dims)

            @pl.when(kk == nk - 1)
            def _():
                finish(acc_ref[...], rest[:ne], rest[ne:-1])
        scratch = [pltpu.VMEM((tm, tn), F32)]

    sem = ("arbitrary", "arbitrary", "arbitrary") if accs else ("parallel", "parallel", "arbitrary")
    res, comm_res = _pcall(body, name, (m // tm, n // tn, nk), [a_spec, b_spec] + e_specs, o_specs, o_shapes, scratch,
                           sem, (a, b, *extras, *rows_in, *consts), comm)
    return res if comm is None else (res, comm_res)


def _rows(body, n_rows, tm, ins, consts, outs, accs, name):
    assert n_rows % tm == 0, (name, n_rows, tm)
    in_specs = [pl.BlockSpec((tm, w), (lambda i, cb=cb: (i, cb))) for _, w, cb in ins]
    in_specs += [pl.BlockSpec(c.shape, lambda i: (0, 0)) for c in consts]
    out_specs = [pl.BlockSpec((tm, w), lambda i: (i, 0)) for w, _ in outs]
    out_specs += [pl.BlockSpec(s, lambda i: (0, 0)) for s, _ in accs]
    out_shape = [jax.ShapeDtypeStruct((n_rows, w), d) for w, d in outs]
    out_shape += [jax.ShapeDtypeStruct(s, d) for s, d in accs]
    ni, nc, no = len(ins), len(consts), len(outs)

    def kern(*refs):
        body(pl.program_id(0), refs[:ni], refs[ni:ni + nc], refs[ni + nc:ni + nc + no], refs[ni + nc + no:])

    res = pl.pallas_call(
        kern, name=name, grid=(n_rows // tm,), in_specs=in_specs, out_specs=out_specs, out_shape=out_shape,
        compiler_params=_cparams(("arbitrary",)),
    )(*[a for a, _, _ in ins], *consts)
    return res


def _acc_add(i, ref, val):
    @pl.when(i == 0)
    def _():
        ref[...] = jnp.zeros_like(ref)

    ref[...] += val


def _norm_cast(h, g, t_rows, name):
    def body(i, ins, consts, outs, accs):
        outs[0][...] = _rms_fwd(ins[0][...], consts[0][...]).astype(BF16)
    return _rows(body, t_rows, 384, [(h, D_MODEL, 0)], [g], [(D_MODEL, BF16)], [], name)[0]


def _head_norm_parts(o):
    ns, rs = [], []
    for hd in range(GLA_H):
        oh = o[:, hd * GLA_DV:(hd + 1) * GLA_DV]
        r = lax.rsqrt(jnp.mean(oh * oh, axis=1, keepdims=True) + EPS)
        ns.append(oh * r)
        rs.append(r)
    return ns, rs


def _mix_fwd(h_f, h_b, z_lru, o_f, o_b, z_go, head_norm, t_rows, name):
    def body(i, ins, consts, outs, accs):
        hf, hb, gate, of, ob, go = [r[...] for r in ins]
        hn = consts[0][...]
        outs[0][:, 0:LRU_W] = ((hf + hb) * _gelu(gate)).astype(BF16)
        ns, _ = _head_norm_parts(of + ob)
        silu = go * _sigmoid(go)
        for hd in range(GLA_H):
            sl = slice(hd * GLA_DV, (hd + 1) * GLA_DV)
            outs[0][:, LRU_W + hd * GLA_DV:LRU_W + (hd + 1) * GLA_DV] = (ns[hd] * hn[:, sl] * silu[:, sl]).astype(BF16)
    ins = [(h_f, LRU_W, 0), (h_b, LRU_W, 0), (z_lru, LRU_W, 1), (o_f, GLA_W, 0), (o_b, GLA_W, 0), (z_go, GLA_W, 0)]
    return _rows(body, t_rows, 384, ins, [head_norm], [(D_MODEL, BF16)], [], name)[0]


def _mix_bwd_tile(dymix, hf, hb, z_lru, of, ob, go, hn):
    dyl, dyg, gate = dymix[:, 0:LRU_W], dymix[:, LRU_W:], z_lru[:, LRU_W:]
    dh_lru = dyl * _gelu(gate)
    dgate = dyl * (hf + hb) * _gelu_grad(gate)
    ns, rs = _head_norm_parts(of + ob)
    sg = _sigmoid(go)
    silu = go * sg
    dsilu = sg * (1.0 + go * (1.0 - sg))
    d_o, d_go, dhn = [], [], []
    for hd in range(GLA_H):
        sl = slice(hd * GLA_DV, (hd + 1) * GLA_DV)
        dy = dyg[:, sl]
        e = dy * hn[:, sl] * silu[:, sl]
        d_o.append(rs[hd] * (e - ns[hd] * jnp.mean(e * ns[hd], axis=1, keepdims=True)))
        d_go.append(dy * ns[hd] * hn[:, sl] * dsilu[:, sl])
        dhn.append(jnp.sum(dy * ns[hd] * silu[:, sl], axis=0, keepdims=True))
    return dh_lru, dgate, jnp.concatenate(d_o, axis=1), jnp.concatenate(d_go, axis=1), jnp.concatenate(dhn, axis=1)


def _dz_assemble(dx_br, dgate, gla_grads, dgo, z_zg, wg, t_rows, name):
    dq_f, dk_f, dv_f, dpre_f, dq_b, dk_b, dv_b, dpre_b = gla_grads
    qk = GLA_H * GLA_DK

    def body(i, ins, consts, outs, accs):
        (dxb, dgt, dqf, dqb, dkf, dkb, dvf, dvb, dg_o, dpf, dpb, zg) = [r[...] for r in ins]
        w = consts[0][...]
        o = outs[0]
        o[:, 0:LRU_W] = dxb.astype(BF16)
        o[:, LRU_W:2 * LRU_W] = dgt.astype(BF16)
        o[:, 1024:1024 + qk] = ((dqf + dqb) * (GLA_DK ** -0.5)).astype(BF16)
        o[:, 1280:1280 + qk] = (dkf + dkb).astype(BF16)
        o[:, 1536:1536 + GLA_W] = (dvf + dvb).astype(BF16)
        o[:, 2048:2048 + GLA_W] = dg_o.astype(BF16)
        dpre = jnp.concatenate([dpf, dpb], axis=1)
        dpre16 = dpre.astype(BF16)
        o[:, ZG_OFF:ZG_OFF + LANES] = _dot(dpre16, w, NT).astype(BF16)
        _acc_add(i, accs[0], _dot(zg.astype(BF16), dpre16, TN))
        _acc_add(i, accs[1], jnp.sum(dpre, axis=0, keepdims=True))

    ins = [(dx_br, LRU_W, 0), (dgate, LRU_W, 0), (dq_f, qk, 0), (dq_b, qk, 0), (dk_f, qk, 0), (dk_b, qk, 0),
           (dv_f, GLA_W, 0), (dv_b, GLA_W, 0), (dgo, GLA_W, 0), (dpre_f, qk, 0), (dpre_b, qk, 0), (z_zg, LANES, 0)]
    return _rows(body, t_rows, 384, ins, [wg], [(D_IN_PAD, BF16)],
                 [((LANES, 2 * qk), F32), ((1, 2 * qk), F32)], name)


LRU_RB = 768
HALO = SUBLANES


def _scan8(a, u, rev):
    rows = lax.broadcasted_iota(jnp.int32, a.shape, 0)
    for d in (1, 2, 4):
        if rev:
            a_s, u_s, ok = pltpu.roll(a, SUBLANES - d, 0), pltpu.roll(u, SUBLANES - d, 0), rows < SUBLANES - d
        else:
            a_s, u_s, ok = pltpu.roll(a, d, 0), pltpu.roll(u, d, 0), rows >= d
        u = jnp.where(ok, a * u_s + u, u)
        a = jnp.where(ok, a * a_s, a)
    return a, u


def _edge_row(x, rev):
    r = x[0:1, :] if rev else x[SUBLANES - 1:SUBLANES, :]
    return jnp.broadcast_to(r, x.shape)


def _scan_block(a, u, carry, rev):
    nsub = a.shape[0] // SUBLANES
    loc = [_scan8(a[sb * SUBLANES:(sb + 1) * SUBLANES], u[sb * SUBLANES:(sb + 1) * SUBLANES], rev) for sb in range(nsub)]
    edges = [(_edge_row(a_c, rev), _edge_row(h_l, rev)) for a_c, h_l in loc]
    carries = [None] * nsub
    for sb in (range(nsub - 1, -1, -1) if rev else range(nsub)):
        carries[sb] = carry
        carry = edges[sb][0] * carry + edges[sb][1]
    h = jnp.concatenate([h_l + a_c * cin for (a_c, h_l), cin in zip(loc, carries)], axis=0)
    return h, carry


def _lru_gates(xc, pre_a, pre_x, ba, bx, sp8):
    r = _sigmoid(pre_a + ba)
    i = _sigmoid(pre_x + bx)
    log_a = -(r * sp8)
    a = jnp.exp(log_a)
    y = 2.0 * log_a
    series = -y * (1.0 + y * (0.5 + y * (1.0 / 6.0 + y * (1.0 / 24.0 + y * (1.0 / 120.0)))))
    om = jnp.where(y > -0.25, series, 1.0 - a * a)
    s = jnp.sqrt(om)
    return r, i, a, s


def _softplus(x):
    return jnp.maximum(x, 0.0) + jnp.log(1.0 + jnp.exp(-jnp.abs(x)))


def _conv_taps(xpad_ref, r0, nrows, shifts=(-2, -1, 0, 1)):
    ext = xpad_ref[pl.ds(r0, nrows + 2 * HALO), :]
    taps = []
    for s in shifts:
        sh = ext if s == 0 else pltpu.roll(ext, (-s) % (nrows + 2 * HALO), 0)
        taps.append(sh[HALO:HALO + nrows])
    return taps


def _row_mask(r0, nrows, seq):
    rows = r0 + lax.broadcasted_iota(jnp.int32, (nrows, LANES), 0)
    return jnp.logical_and(rows >= FRONT - N_META, rows < FRONT + seq).astype(F32)


def _lru_load_conv(z_hbm, xpad, xc, sem, cw_ref, cb_ref, t_rows, seq):
    j = pl.program_id(0)
    zeros = jnp.zeros((HALO, LANES), F32)
    xpad[0:HALO, :] = zeros
    xpad[t_rows + HALO:t_rows + 2 * HALO, :] = zeros
    cp = pltpu.make_async_copy(z_hbm.at[:, pl.ds(pl.multiple_of(j * LANES, LANES), LANES)],
                               xpad.at[pl.ds(HALO, t_rows)], sem)
    cp.start()
    cp.wait()
    cw = cw_ref[...]
    cb = cb_ref[...]

    def conv_blk(r, carry):
        r0 = pl.multiple_of(r * LRU_RB, LRU_RB)
        taps = _conv_taps(xpad, r0, LRU_RB)
        acc = cb + cw[0:1, :] * taps[0]
        for k in range(1, 4):
            acc = acc + cw[k:k + 1, :] * taps[k]
        xc[pl.ds(r0, LRU_RB), :] = acc * _row_mask(r0, LRU_RB, seq)
        return carry

    lax.fori_loop(0, t_rows // LRU_RB, conv_blk, 0)


def _lru_specs(t_rows):
    return [pl.BlockSpec(memory_space=pl.ANY),
            pl.BlockSpec((4, LANES), lambda j: (0, j)),
            pl.BlockSpec((1, LANES), lambda j: (0, j)),
            pl.BlockSpec((1, LANES, 4 * LANES), lambda j: (j, 0, 0)),
            pl.BlockSpec((SUBLANES, LANES), lambda j: (0, j))]


def _lru_fwd(z_lru, conv_w, conv_b, wg, pb, t_rows, seq, name, comm=None):
    nblk = t_rows // LRU_RB

    def kern(z_hbm, cw_ref, cb_ref, wg_ref, pb_ref, hf_ref, hb_ref, xpad, xc, sem):
        _lru_load_conv(z_hbm, xpad, xc, sem, cw_ref, cb_ref, t_rows, seq)
        w = wg_ref[0]
        pb_v = pb_ref[...]
        dirs = []
        for rev in (False, True):
            o = 2 if rev else 0
            dirs.append(dict(rev=rev, ba=pb_v[o:o + 1, :], bx=pb_v[o + 1:o + 2, :],
                             sp8=LRU_C * _softplus(-pb_v[5:6, :] if rev else -pb_v[4:5, :]),
                             w=w[:, o * LANES:(o + 2) * LANES], out=hb_ref if rev else hf_ref))

        def blk(it, carries):
            new = []
            for d, carry in zip(dirs, carries):
                r = (nblk - 1 - it) if d["rev"] else it
                r0 = pl.multiple_of(r * LRU_RB, LRU_RB)
                xcb = xc[pl.ds(r0, LRU_RB), :]
                pre = _dot(xcb.astype(BF16), d["w"])
                _, gi, a, s = _lru_gates(xcb, pre[:, 0:LANES], pre[:, LANES:2 * LANES], d["ba"], d["bx"], d["sp8"])
                h, carry = _scan_block(a, s * (gi * xcb), carry, d["rev"])
                d["out"][pl.ds(r0, LRU_RB), :] = h
                new.append(carry)
            return tuple(new)

        z8 = jnp.zeros((SUBLANES, LANES), F32)
        lax.fori_loop(0, nblk, blk, (z8, z8), unroll=2)

    res, comm_res = _pcall(
        kern, name, (LRU_W // LANES,), _lru_specs(t_rows), [pl.BlockSpec((t_rows, LANES), lambda j: (0, j))] * 2,
        [jax.ShapeDtypeStruct((t_rows, LRU_W), F32)] * 2,
        [pltpu.VMEM((t_rows + 2 * HALO, LANES), F32), pltpu.VMEM((t_rows, LANES), F32), pltpu.SemaphoreType.DMA],
        ("arbitrary",), (z_lru, conv_w, conv_b, wg, pb), comm)
    return res if comm is None else (res, comm_res)


def _lru_bwd(z_lru, conv_w, conv_b, wg, pb, h_f, h_b, dh, t_rows, seq, name, comm=None):
    nblk = t_rows // LRU_RB

    def kern(z_hbm, cw_ref, cb_ref, wg_ref, pb_ref, hf_ref, hb_ref, dh_ref,
             dx_ref, dcw_ref, dcb_ref, dwh_ref, dpb_ref, xpad, xc, dxc, dwg, sem):
        _lru_load_conv(z_hbm, xpad, xc, sem, cw_ref, cb_ref, t_rows, seq)
        w = wg_ref[0]
        pb_v = pb_ref[...]
        zeros = jnp.zeros((HALO, LANES), F32)
        dxc[0:HALO, :] = zeros
        dxc[t_rows + HALO:t_rows + 2 * HALO, :] = zeros
        dwg[...] = jnp.zeros_like(dwg)

        def zero_blk(r, carry):
            dxc[pl.ds(pl.multiple_of(r * LRU_RB, LRU_RB) + HALO, LRU_RB), :] = jnp.zeros((LRU_RB, LANES), F32)
            return carry

        lax.fori_loop(0, nblk, zero_blk, 0)
        dirs = []
        for rev in (False, True):
            o = 2 if rev else 0
            lam = pb_v[5:6, :] if rev else pb_v[4:5, :]
            dirs.append(dict(rev=rev, o=o, ba=pb_v[o:o + 1, :], bx=pb_v[o + 1:o + 2, :], sp8=LRU_C * _softplus(-lam),
                             dlam_scale=LRU_C * _sigmoid(-lam), h_ref=hb_ref if rev else hf_ref,
                             w=w[:, o * LANES:(o + 2) * LANES]))
        rows_b = lax.broadcasted_iota(jnp.int32, (LRU_RB, LANES), 0)

        def blk(it, carries):
            new = []
            for d, (nu_c, acc_ba, acc_bx, acc_lam) in zip(dirs, carries):
                rev, o, h_ref = d["rev"], d["o"], d["h_ref"]
                adj_rev = not rev
                r = (nblk - 1 - it) if adj_rev else it
                r0 = pl.multiple_of(r * LRU_RB, LRU_RB)
                xcb = xc[pl.ds(r0, LRU_RB), :]
                xc16 = xcb.astype(BF16)
                pre = _dot(xc16, d["w"])
                gr, gi, a, s = _lru_gates(xcb, pre[:, 0:LANES], pre[:, LANES:2 * LANES], d["ba"], d["bx"], d["sp8"])
                dhb = dh_ref[pl.ds(r0, LRU_RB), :]
                hb = h_ref[pl.ds(r0, LRU_RB), :]
                if rev:
                    nxt0 = pl.multiple_of(jnp.minimum(r0 + LRU_RB, t_rows - SUBLANES), SUBLANES)
                    edge = h_ref[pl.ds(nxt0, SUBLANES), :][0:1, :] * (r < nblk - 1).astype(F32)
                    h_prev = jnp.concatenate([hb[1:], edge], axis=0)
                else:
                    prv0 = pl.multiple_of(jnp.maximum(r0 - SUBLANES, 0), SUBLANES)
                    edge = h_ref[pl.ds(prv0, SUBLANES), :][SUBLANES - 1:SUBLANES, :] * (r > 0).astype(F32)
                    h_prev = jnp.concatenate([edge, hb[:-1]], axis=0)
                nu, nu_out = _scan_block(a, a * dhb, nu_c, adj_rev)
                cin = jnp.broadcast_to(nu_c[0:1, :], (LRU_RB, LANES))
                if adj_rev:
                    nu_next = jnp.where(rows_b == LRU_RB - 1, cin, pltpu.roll(nu, LRU_RB - 1, 0))
                else:
                    nu_next = jnp.where(rows_b == 0, cin, pltpu.roll(nu, 1, 0))
                du = dhb + nu_next
                da = du * h_prev
                ix = gi * xcb
                dlog = da * a - du * ix * (a * a) / s
                d_i = du * s * xcb
                dr = -dlog * d["sp8"]
                dpa = dr * gr * (1.0 - gr)
                dpx = d_i * gi * (1.0 - gi)
                dp16 = jnp.concatenate([dpa, dpx], axis=1).astype(BF16)
                dxc[pl.ds(r0 + HALO, LRU_RB), :] += du * s * gi + _dot(dp16, d["w"], NT)
                dwg[:, o * LANES:(o + 2) * LANES] += _dot(xc16, dp16, TN)
                new.append((nu_out, acc_ba + jnp.sum(dpa, axis=0, keepdims=True),
                            acc_bx + jnp.sum(dpx, axis=0, keepdims=True),
                            acc_lam + jnp.sum(dlog * gr, axis=0, keepdims=True)))
            return tuple(new)

        z1 = jnp.zeros((1, LANES), F32)
        init = (jnp.zeros((SUBLANES, LANES), F32), z1, z1, z1)
        (_, ba_f, bx_f, lam_f), (_, ba_b, bx_b, lam_b) = lax.fori_loop(0, nblk, blk, (init, init), unroll=2)
        dpb_ref[...] = jnp.concatenate([ba_f, bx_f, ba_b, bx_b, lam_f * dirs[0]["dlam_scale"],
                                        lam_b * dirs[1]["dlam_scale"], z1, z1], axis=0)
        for hh in range(2):
            for o in range(4):
                c0 = o * LANES + hh * LRU_HD
                dwh_ref[hh, :, o * LRU_HD:(o + 1) * LRU_HD] = dwg[hh * LRU_HD:(hh + 1) * LRU_HD, c0:c0 + LRU_HD]

        def mask_blk(r, carry):
            r0 = pl.multiple_of(r * LRU_RB, LRU_RB)
            dxc[pl.ds(r0 + HALO, LRU_RB), :] = dxc[pl.ds(r0 + HALO, LRU_RB), :] * _row_mask(r0, LRU_RB, seq)
            return carry

        lax.fori_loop(0, nblk, mask_blk, 0)

        cw = cw_ref[...]

        def conv_bwd(r, carry):
            r0 = pl.multiple_of(r * LRU_RB, LRU_RB)
            gt = _conv_taps(dxc, r0, LRU_RB, (2, 1, 0, -1))
            xt = _conv_taps(xpad, r0, LRU_RB)
            dx_ref[pl.ds(r0, LRU_RB), :] = (cw[0:1, :] * gt[0] + cw[1:2, :] * gt[1] + cw[2:3, :] * gt[2]
                                           + cw[3:4, :] * gt[3])
            g = gt[2]
            new = [carry[k] + jnp.sum(g * xt[k], axis=0, keepdims=True) for k in range(4)]
            new.append(carry[4] + jnp.sum(g, axis=0, keepdims=True))
            return tuple(new)

        z1 = jnp.zeros((1, LANES), F32)
        sums = lax.fori_loop(0, nblk, conv_bwd, (z1, z1, z1, z1, z1))
        dcw_ref[...] = jnp.concatenate(sums[:4], axis=0)
        dcb_ref[...] = sums[4]

    col = lambda j: (0, j)
    res, comm_res = _pcall(
        kern, name, (LRU_W // LANES,), _lru_specs(t_rows) + [pl.BlockSpec((t_rows, LANES), col)] * 3,
        [pl.BlockSpec((t_rows, LANES), col), pl.BlockSpec((4, LANES), col), pl.BlockSpec((1, LANES), col),
         pl.BlockSpec((2, LRU_HD, 4 * LRU_HD), lambda j: (j, 0, 0)), pl.BlockSpec((SUBLANES, LANES), col)],
        [jax.ShapeDtypeStruct((t_rows, LRU_W), F32), jax.ShapeDtypeStruct((4, LRU_W), F32),
         jax.ShapeDtypeStruct((1, LRU_W), F32), jax.ShapeDtypeStruct((LRU_HEADS, LRU_HD, 4 * LRU_HD), F32),
         jax.ShapeDtypeStruct((SUBLANES, LRU_W), F32)],
        [pltpu.VMEM((t_rows + 2 * HALO, LANES), F32), pltpu.VMEM((t_rows, LANES), F32),
         pltpu.VMEM((t_rows + 2 * HALO, LANES), F32), pltpu.VMEM((LANES, 4 * LANES), F32), pltpu.SemaphoreType.DMA],
        ("arbitrary",), (z_lru, conv_w, conv_b, wg, pb, h_f, h_b, dh), comm)
    return res if comm is None else (res, comm_res)


QK = GLA_H * GLA_DK
GLA_G = 6


def _cumsum_rows(x, rev):
    n = x.shape[0]
    rows = lax.broadcasted_iota(jnp.int32, x.shape, 0)
    d = 1
    while d < n:
        if rev:
            x = x + jnp.where(rows < n - d, pltpu.roll(x, n - d, 0), 0.0)
        else:
            x = x + jnp.where(rows >= d, pltpu.roll(x, d, 0), 0.0)
        d *= 2
    return x


def _gla_chunk_terms(q, k, zg, wg, bg, rev):
    pre = (_dot(zg.astype(BF16), wg) + bg)[:, (QK if rev else 0):(2 * QK if rev else QK)]
    g = (jnp.minimum(pre, 0.0) - jnp.log(1.0 + jnp.exp(-jnp.abs(pre)))) * (1.0 / GLA_GATE_NORM)
    b = _cumsum_rows(g, rev)
    b_last = b[0:1, :] if rev else b[CHUNK - 1:CHUNK, :]
    b_mid = b[CHUNK // 2:CHUNK // 2 + 1, :]
    qs = q * (GLA_DK ** -0.5)
    terms = dict(
        pre=pre, qs=qs, k=k,
        eb=jnp.exp(b), ek=jnp.exp(b_last - b), e_last=jnp.exp(b_last),
        eqm=jnp.exp(jnp.minimum(b - b_mid, EXP_CLAMP)), ekm=jnp.exp(jnp.minimum(b_mid - b, EXP_CLAMP)))
    return terms


def _gla_mask(rev):
    t = lax.broadcasted_iota(jnp.int32, (CHUNK, CHUNK), 0)
    s = lax.broadcasted_iota(jnp.int32, (CHUNK, CHUNK), 1)
    return (s > t) if rev else (s <= t)


def _gla_head_ops(tm, hd):
    sl = slice(hd * GLA_DK, (hd + 1) * GLA_DK)
    q_b = (tm["qs"][:, sl] * tm["eb"][:, sl]).astype(BF16)
    k_e = (tm["k"][:, sl] * tm["ek"][:, sl]).astype(BF16)
    q_m = (tm["qs"][:, sl] * tm["eqm"][:, sl]).astype(BF16)
    k_m = (tm["k"][:, sl] * tm["ekm"][:, sl]).astype(BF16)
    return sl, q_b, k_e, q_m, k_m


def _gla_fwd(z_qk, z_v, z_zg, wg, bg, t_rows, name, comm=None):
    nc = t_rows // CHUNK
    ng = nc // GLA_G
    rows = CHUNK * GLA_G

    def kern(qf, kf, vf, zf, qb, kb, vb, zb, wg_ref, bg_ref, of_ref, ob_ref, sf_ref, sb_ref, st_f, st_b):
        i = pl.program_id(0)

        @pl.when(i == 0)
        def _():
            st_f[...] = jnp.zeros_like(st_f)
            st_b[...] = jnp.zeros_like(st_b)

        wg_v, bg_v = wg_ref[...], bg_ref[...]
        for rev, (q_r, k_r, v_r, z_r, o_r, s_r, st) in ((False, (qf, kf, vf, zf, of_ref, sf_ref, st_f)),
                                                        (True, (qb, kb, vb, zb, ob_ref, sb_ref, st_b))):
            mask = _gla_mask(rev)
            for ck in (range(GLA_G - 1, -1, -1) if rev else range(GLA_G)):
                cr = slice(ck * CHUNK, (ck + 1) * CHUNK)
                tm = _gla_chunk_terms(q_r[cr, :], k_r[cr, :], z_r[cr, :], wg_v, bg_v, rev)
                v = v_r[cr, :]
                for hd in range(GLA_H):
                    sl, q_b, k_e, q_m, k_m = _gla_head_ops(tm, hd)
                    vs = slice(hd * GLA_DV, (hd + 1) * GLA_DV)
                    v16 = v[:, vs].astype(BF16)
                    a = jnp.where(mask, _dot(q_m, k_m, NT), 0.0).astype(BF16)
                    s_in = st[hd]
                    s_r[ck, hd] = s_in
                    o_r[cr, vs] = _dot(a, v16) + _dot(q_b, s_in.astype(BF16), NT)
                    st[hd] = s_in * tm["e_last"][:, sl] + _dot(v16, k_e, TN)

    fw = lambda c: (lambda i: (i, c))
    rv = lambda c: (lambda i: (ng - 1 - i, c))
    def side(ix):
        return [pl.BlockSpec((rows, QK), ix(0)), pl.BlockSpec((rows, QK), ix(1)),
                pl.BlockSpec((rows, GLA_W), ix(0)), pl.BlockSpec((rows, LANES), ix(0))]
    st_shape = (nc, GLA_H, GLA_DV, GLA_DK)
    res, comm_res = _pcall(
        kern, name, (ng,),
        side(fw) + side(rv) + [pl.BlockSpec(wg.shape, lambda i: (0, 0)), pl.BlockSpec(bg.shape, lambda i: (0, 0))],
        [pl.BlockSpec((rows, GLA_W), fw(0)), pl.BlockSpec((rows, GLA_W), rv(0)),
         pl.BlockSpec((GLA_G,) + st_shape[1:], lambda i: (i, 0, 0, 0)),
         pl.BlockSpec((GLA_G,) + st_shape[1:], lambda i: (ng - 1 - i, 0, 0, 0))],
        [jax.ShapeDtypeStruct((t_rows, GLA_W), F32)] * 2 + [jax.ShapeDtypeStruct(st_shape, F32)] * 2,
        [pltpu.VMEM(st_shape[1:], F32)] * 2, ("arbitrary",),
        (z_qk, z_qk, z_v, z_zg, z_qk, z_qk, z_v, z_zg, wg, bg), comm)
    return res if comm is None else (res, comm_res)


def _gla_bwd(z_qk, z_v, z_zg, wg, bg, do, s_f, s_b, t_rows, name, comm=None):
    nc = t_rows // CHUNK

    def kern(qf, kf, vf, zf, dof, ssf, qb, kb, vb, zb, dob, ssb, wg_ref, bg_ref,
             dqf, dkf, dvf, dpf, dqb, dkb, dvb, dpb, ds_f, ds_b):
        i = pl.program_id(0)

        @pl.when(i == 0)
        def _():
            ds_f[...] = jnp.zeros_like(ds_f)
            ds_b[...] = jnp.zeros_like(ds_b)

        wg_v, bg_v = wg_ref[...], bg_ref[...]
        sides = ((False, (qf, kf, vf, zf, dof, ssf, dqf, dkf, dvf, dpf, ds_f)),
                 (True, (qb, kb, vb, zb, dob, ssb, dqb, dkb, dvb, dpb, ds_b)))
        for rev, (q_r, k_r, v_r, z_r, do_r, ss_r, dq_r, dk_r, dv_r, dp_r, ds) in sides:
            mask = _gla_mask(rev)
            for ck in (range(GLA_G) if rev else range(GLA_G - 1, -1, -1)):
                cr = slice(ck * CHUNK, (ck + 1) * CHUNK)
                tm = _gla_chunk_terms(q_r[cr, :], k_r[cr, :], z_r[cr, :], wg_v, bg_v, rev)
                v = v_r[cr, :]
                d_o = do_r[cr, :]
                db_parts, cross_parts = [], []
                for hd in range(GLA_H):
                    sl, q_b, k_e, q_m, k_m = _gla_head_ops(tm, hd)
                    vs = slice(hd * GLA_DV, (hd + 1) * GLA_DV)
                    v16, do16 = v[:, vs].astype(BF16), d_o[:, vs].astype(BF16)
                    a = jnp.where(mask, _dot(q_m, k_m, NT), 0.0).astype(BF16)
                    da = jnp.where(mask, _dot(do16, v16, NT), 0.0).astype(BF16)
                    s_in = ss_r[ck, hd]
                    s_in16 = s_in.astype(BF16)
                    ds_out = ds[hd]
                    ds16 = ds_out.astype(BF16)
                    dv_r[cr, vs] = _dot(a, do16, TN) + _dot(k_e, ds16, NT)
                    dqs = _dot(da, k_m) * tm["eqm"][:, sl] + _dot(do16, s_in16) * tm["eb"][:, sl]
                    dk = _dot(da, q_m, TN) * tm["ekm"][:, sl] + _dot(v16, ds16) * tm["ek"][:, sl]
                    ds[hd] = ds_out * tm["e_last"][:, sl] + _dot(do16, q_b, TN)
                    dq_r[cr, sl] = dqs
                    dk_r[cr, sl] = dk
                    db_parts.append(tm["qs"][:, sl] * dqs - tm["k"][:, sl] * dk)
                    s_out = s_in * tm["e_last"][:, sl] + _dot(v16, k_e, TN)
                    cross_parts.append(jnp.sum(s_out * ds_out, axis=0, keepdims=True))
                d_b = jnp.concatenate(db_parts, axis=1)
                dg = _cumsum_rows(d_b, not rev) + jnp.concatenate(cross_parts, axis=1)
                dp_r[cr, :] = dg * (1.0 / GLA_GATE_NORM) * _sigmoid(-tm["pre"])

    ng = nc // GLA_G
    rows = CHUNK * GLA_G
    fw = lambda c: (lambda i: (ng - 1 - i, c))
    rv = lambda c: (lambda i: (i, c))
    st_blk = (GLA_G, GLA_H, GLA_DV, GLA_DK)
    def side(ix, st_ix):
        return [pl.BlockSpec((rows, QK), ix(0)), pl.BlockSpec((rows, QK), ix(1)),
                pl.BlockSpec((rows, GLA_W), ix(0)), pl.BlockSpec((rows, LANES), ix(0)),
                pl.BlockSpec((rows, GLA_W), ix(0)), pl.BlockSpec(st_blk, st_ix)]
    def oside(ix):
        return [pl.BlockSpec((rows, QK), ix(0)), pl.BlockSpec((rows, QK), ix(0)),
                pl.BlockSpec((rows, GLA_W), ix(0)), pl.BlockSpec((rows, QK), ix(0))]
    o_shapes = [jax.ShapeDtypeStruct((t_rows, QK), F32), jax.ShapeDtypeStruct((t_rows, QK), F32),
                jax.ShapeDtypeStruct((t_rows, GLA_W), F32), jax.ShapeDtypeStruct((t_rows, QK), F32)]
    res, comm_res = _pcall(
        kern, name, (ng,),
        (side(fw, lambda i: (ng - 1 - i, 0, 0, 0)) + side(rv, lambda i: (i, 0, 0, 0))
         + [pl.BlockSpec(wg.shape, lambda i: (0, 0)), pl.BlockSpec(bg.shape, lambda i: (0, 0))]),
        oside(fw) + oside(rv), o_shapes * 2, [pltpu.VMEM((GLA_H, GLA_DV, GLA_DK), F32)] * 2, ("arbitrary",),
        (z_qk, z_qk, z_v, z_zg, do, s_f, z_qk, z_qk, z_v, z_zg, do, s_b, wg, bg), comm)
    return res if comm is None else (res, comm_res)


FLAT_W = 1024


def _adam_math(wv, gv, mv, vv):
    bc1 = 1.0 - ADAM_B1 ** ADAM_STEP
    bc2 = 1.0 - ADAM_B2 ** ADAM_STEP
    m_new = ADAM_B1 * mv + (1.0 - ADAM_B1) * gv
    v_new = ADAM_B2 * vv + (1.0 - ADAM_B2) * (gv * gv)
    delta = -ADAM_LR * ((m_new / bc1) / (jnp.sqrt(v_new / bc2) + ADAM_EPS) + ADAM_WD * wv)
    return delta, m_new, v_new


def _row_tile(rows, cap=256):
    t = cap
    while rows % t:
        t //= 2
    assert t >= SUBLANES, rows
    return t


def _adamw_layers(w, m, v, g_layers, name, comm=None):
    depth, rows, cols = w.shape
    tr = _row_tile(rows)

    def kern(w_ref, m_ref, v_ref, *rest):
        g_refs, (go_ref, d_ref, mo_ref, vo_ref) = rest[:depth], rest[depth:]
        l = pl.program_id(0)
        gv = g_refs[0][...]
        for k in range(1, depth):
            gv = jnp.where(l == k, g_refs[k][...], gv)
        delta, m_new, v_new = _adam_math(w_ref[...], gv, m_ref[...], v_ref[...])
        go_ref[...] = gv
        d_ref[...] = delta
        mo_ref[...] = m_new
        vo_ref[...] = v_new

    stacked = pl.BlockSpec((None, tr, cols), lambda l, i: (l, i, 0))
    res, comm_res = _pcall(
        kern, name, (depth, rows // tr), [stacked] * 3 + [pl.BlockSpec((tr, cols), lambda l, i: (i, 0))] * depth,
        [stacked] * 4, [jax.ShapeDtypeStruct(w.shape, F32)] * 4, [], ("arbitrary", "arbitrary"),
        (w, m, v, *g_layers), comm)
    return res if comm is None else (res, comm_res)


def _adamw_small(ws, gs, ms, vs):
    n = len(ws)

    def kern(*refs):
        ins, outs = refs[:4 * n], refs[4 * n:]
        for k in range(n):
            delta, m_new, v_new = _adam_math(ins[k][...], ins[n + k][...], ins[2 * n + k][...], ins[3 * n + k][...])
            outs[k][...] = delta
            outs[n + k][...] = m_new
            outs[2 * n + k][...] = v_new

    shapes = [jax.ShapeDtypeStruct(w.shape, F32) for w in ws]
    res = pl.pallas_call(kern, name="adamw_small", out_shape=shapes * 3,
                         compiler_params=pltpu.CompilerParams(vmem_limit_bytes=VMEM_LIMIT))(*ws, *gs, *ms, *vs)
    return res[:n], res[n:2 * n], res[2 * n:]


def _add_sibling(g, got, c_idx, name):
    _, _, rows, cols = g.shape
    tr = _row_tile(rows)

    def kern(c_ref, g_ref, got_ref, o_ref):
        o_ref[...] = (g_ref[...] + got_ref[...]).astype(BF16)

    grid_spec = pltpu.PrefetchScalarGridSpec(
        num_scalar_prefetch=1, grid=(N_CHIPS, rows // tr),
        in_specs=[pl.BlockSpec((None, None, tr, cols), lambda q, i, c_ref: (q, c_ref[0], i, 0)),
                  pl.BlockSpec((None, tr, cols), lambda q, i, c_ref: (q, i, 0))],
        out_specs=pl.BlockSpec((None, tr, cols), lambda q, i, c_ref: (q, i, 0)))
    return pl.pallas_call(
        kern, name=name, grid_spec=grid_spec, out_shape=jax.ShapeDtypeStruct((N_CHIPS, rows, cols), BF16),
        compiler_params=_cparams(("arbitrary", "arbitrary")),
    )(c_idx, g, got)


def _sum_chips(x, name):
    _, rows, cols = x.shape
    tr = _row_tile(rows)

    def kern(x_ref, o_ref):
        xv = x_ref[...].astype(F32)
        o_ref[...] = ((xv[0] + xv[1]) + xv[2]) + xv[3]

    return pl.pallas_call(
        kern, name=name, grid=(rows // tr,),
        in_specs=[pl.BlockSpec((N_CHIPS, tr, cols), lambda i: (0, i, 0))],
        out_specs=pl.BlockSpec((tr, cols), lambda i: (i, 0)),
        out_shape=jax.ShapeDtypeStruct((rows, cols), F32),
        compiler_params=_cparams(("arbitrary",)),
    )(x)


def _place():
    x, y, c = lax.axis_index("x"), lax.axis_index("y"), lax.axis_index("c")
    return x, y, c


def _other_chips(x, y):
    return [(1 - x, y), (x, 1 - y), (1 - x, 1 - y)]


class _Comm:
    def __init__(self, ins, out_shapes, n_sems, stage, body):
        self.ins, self.out_shapes, self.body = list(ins), list(out_shapes), body
        self.scratch = [pltpu.SemaphoreType.DMA((n,)) for n in n_sems] + [pltpu.VMEM(s, d) for s, d in stage]


ANY_SPEC = pl.BlockSpec(memory_space=pl.ANY)


def _run_comm(comm, name):
    def kern(*refs):
        comm.body(refs, True, True)

    return pl.pallas_call(
        kern, name=name, in_specs=[ANY_SPEC] * len(comm.ins), out_specs=[ANY_SPEC] * len(comm.out_shapes),
        out_shape=comm.out_shapes, scratch_shapes=comm.scratch,
        compiler_params=pltpu.CompilerParams(has_side_effects=True, vmem_limit_bytes=VMEM_LIMIT),
    )(*comm.ins)


def _pcall(kern, name, grid, in_specs, out_specs, out_shape, scratch_shapes, sem, args, comm=None):
    if comm is None:
        return pl.pallas_call(kern, name=name, grid=grid, in_specs=in_specs, out_specs=out_specs, out_shape=out_shape,
                              scratch_shapes=scratch_shapes, compiler_params=_cparams(sem))(*args), None
    n_in, n_out, n_scr = len(in_specs), len(out_specs), len(scratch_shapes)
    c_in, c_out = len(comm.ins), len(comm.out_shapes)

    def hosted(*refs):
        a = n_in + c_in
        b = a + n_out + c_out
        main = refs[:n_in] + refs[a:a + n_out] + refs[b:b + n_scr]
        extra = refs[n_in:a] + refs[a + n_out:b] + refs[b + n_scr:]
        ids = [pl.program_id(d) for d in range(len(grid))]
        first, last = ids[0] == 0, ids[0] == grid[0] - 1
        for i, g in zip(ids[1:], grid[1:]):
            first, last = jnp.logical_and(first, i == 0), jnp.logical_and(last, i == g - 1)

        @pl.when(first)
        def _():
            comm.body(extra, True, False)

        kern(*main)

        @pl.when(last)
        def _():
            comm.body(extra, False, True)

    res = pl.pallas_call(
        hosted, name=name, grid=grid, in_specs=list(in_specs) + [ANY_SPEC] * c_in,
        out_specs=list(out_specs) + [ANY_SPEC] * c_out, out_shape=list(out_shape) + comm.out_shapes,
        scratch_shapes=list(scratch_shapes) + comm.scratch,
        compiler_params=pltpu.CompilerParams(dimension_semantics=("arbitrary",) * len(grid),
                                             vmem_limit_bytes=VMEM_LIMIT, has_side_effects=True),
    )(*args, *comm.ins)
    return res[:n_out], res[n_out:]


class _StagedCopies:
    def __init__(self, pairs, bufs, sem_in, sem_out):
        self.ins = [pltpu.make_async_copy(src, buf, sem_in.at[i]) for i, ((src, _), buf) in enumerate(zip(pairs, bufs))]
        self.outs = [pltpu.make_async_copy(buf, dst, sem_out.at[i]) for i, ((_, dst), buf) in enumerate(zip(pairs, bufs))]

    def load(self):
        for cp in self.ins:
            cp.start()

    def store(self):
        for cp in self.ins:
            cp.wait()
        for cp in self.outs:
            cp.start()

    def finish(self):
        for cp in self.outs:
            cp.wait()


DMA_CHUNK_BYTES = 1 << 20
DMA_MAX_CHUNKS = 4


def _row_chunks(rows, row_bytes, align=16):
    units = rows // align
    k = max(1, min(DMA_MAX_CHUNKS, -(-rows * row_bytes // DMA_CHUNK_BYTES), units))
    out, start = [], 0
    for i in range(k):
        size = (units // k + (1 if i < units % k else 0)) * align
        out.append((start, size))
        start += size
    if start < rows:
        out[-1] = (out[-1][0], out[-1][1] + rows - start)
    return out


def _row_bytes(shape, dtype):
    return math.prod(shape[1:]) * jnp.dtype(dtype).itemsize


def _remote(src, dst, send, recv, idx, dev):
    return pltpu.make_async_remote_copy(src, dst, send.at[idx], recv.at[idx], device_id=dev, device_id_type=MESH_ID)


def _allgather_chips(xs):
    na = len(xs)
    chunks = [_row_chunks(x.shape[1], _row_bytes(x.shape[1:], x.dtype)) for x in xs]
    n_cp = 3 * sum(len(ch) for ch in chunks)

    def body(refs, start, finish):
        x_refs, o_refs = refs[:na], refs[na:2 * na]
        send1, recv1, send2, recv2, loc_in, loc_out = refs[2 * na:2 * na + 6]
        px, py, c = _place()
        me = 2 * px + py
        sib = (px, py, 1 - c)
        own = _StagedCopies([(x, o.at[me]) for x, o in zip(x_refs, o_refs)], refs[2 * na + 6:], loc_in, loc_out)
        plan = [(a, qx, qy, pl.ds(s, n)) for a in range(na) for (qx, qy) in _other_chips(px, py) for (s, n) in chunks[a]]
        first = [_remote(x_refs[a].at[c, rows], o_refs[a].at[me, c, rows], send1, recv1, i, (qx, qy, c))
                 for i, (a, qx, qy, rows) in enumerate(plan)]
        if start:
            own.load()
            for cp in first:
                cp.start()
        if finish:
            own.store()
            passed = []
            for i, (a, qx, qy, rows) in enumerate(plan):
                slot = o_refs[a].at[2 * qx + qy, c, rows]
                _remote(slot, slot, send1, recv1, i, (qx, qy, c)).wait_recv()
                fwd = _remote(slot, slot, send2, recv2, i, sib)
                fwd.start()
                passed.append(fwd)
            for i, (a, qx, qy, rows) in enumerate(plan):
                slot = o_refs[a].at[2 * qx + qy, 1 - c, rows]
                _remote(slot, slot, send2, recv2, i, sib).wait_recv()
            for cp in first + passed:
                cp.wait_send()
            own.finish()

    outs = [jax.ShapeDtypeStruct((N_CHIPS,) + x.shape, x.dtype) for x in xs]
    return _Comm(xs, outs, [n_cp, n_cp, n_cp, n_cp, na, na], [(x.shape, x.dtype) for x in xs], body)


def _sibling_halves(gs):
    na = len(gs)
    chunks = [_row_chunks(g.shape[2], _row_bytes(g.shape[2:], g.dtype)) for g in gs]
    n_cp = N_CHIPS * sum(len(ch) for ch in chunks)

    def body(refs, start, finish):
        g_refs, o_refs = refs[:na], refs[na:2 * na]
        send, recv = refs[2 * na:]
        px, py, c = _place()
        plan = [(a, q, pl.ds(s, n)) for a in range(na) for q in range(N_CHIPS) for (s, n) in chunks[a]]
        cps = [_remote(g_refs[a].at[q, 1 - c, rows], o_refs[a].at[q, rows], send, recv, i, (px, py, 1 - c))
               for i, (a, q, rows) in enumerate(plan)]
        if start:
            for cp in cps:
                cp.start()
        if finish:
            for cp in cps:
                cp.wait()

    outs = [jax.ShapeDtypeStruct((N_CHIPS,) + g.shape[2:], g.dtype) for g in gs]
    return _Comm(gs, outs, [n_cp, n_cp], [], body)


def _chip_exchange(ps):
    na = len(ps)
    chunks = [_row_chunks(p.shape[1], _row_bytes(p.shape[1:], p.dtype)) for p in ps]
    n_cp = 3 * sum(len(ch) for ch in chunks)

    def body(refs, start, finish):
        p_refs, o_refs = refs[:na], refs[na:2 * na]
        send, recv, loc_in, loc_out = refs[2 * na:2 * na + 4]
        px, py, c = _place()
        me = 2 * px + py
        own = _StagedCopies([(p.at[me], o.at[me]) for p, o in zip(p_refs, o_refs)], refs[2 * na + 4:], loc_in, loc_out)
        plan = [(a, qx, qy, pl.ds(s, n)) for a in range(na) for (qx, qy) in _other_chips(px, py) for (s, n) in chunks[a]]
        cps = [_remote(p_refs[a].at[2 * qx + qy, rows], o_refs[a].at[me, rows], send, recv, i, (qx, qy, c))
               for i, (a, qx, qy, rows) in enumerate(plan)]
        if start:
            own.load()
            for cp in cps:
                cp.start()
        if finish:
            own.store()
            for cp in cps:
                cp.wait()
            own.finish()

    outs = [jax.ShapeDtypeStruct(p.shape, p.dtype) for p in ps]
    return _Comm(ps, outs, [n_cp, n_cp, na, na], [(p.shape[1:], p.dtype) for p in ps], body)


def _sibling_join(rs):
    na = len(rs)
    chunks = [_row_chunks(r.shape[0], _row_bytes(r.shape, r.dtype)) for r in rs]
    n_cp = sum(len(ch) for ch in chunks)

    def body(refs, start, finish):
        r_refs, o_refs = refs[:na], refs[na:2 * na]
        send, recv, loc_in, loc_out = refs[2 * na:2 * na + 4]
        px, py, c = _place()
        own = _StagedCopies([(r, o.at[c]) for r, o in zip(r_refs, o_refs)], refs[2 * na + 4:], loc_in, loc_out)
        plan = [(a, pl.ds(s, n)) for a in range(na) for (s, n) in chunks[a]]
        cps = [_remote(r_refs[a].at[rows], o_refs[a].at[c, rows], send, recv, i, (px, py, 1 - c))
               for i, (a, rows) in enumerate(plan)]
        if start:
            own.load()
            for cp in cps:
                cp.start()
        if finish:
            own.store()
            for cp in cps:
                cp.wait()
            own.finish()

    outs = [jax.ShapeDtypeStruct((2,) + r.shape, r.dtype) for r in rs]
    return _Comm(rs, outs, [n_cp, n_cp, na, na], [(r.shape, r.dtype) for r in rs], body)


def _reduce_stages(gs, c_idx, tag):
    got = yield _sibling_halves(gs)
    part = [_add_sibling(g, o, c_idx, f"reduce_add_sibling{tag}_{a}") for a, (g, o) in enumerate(zip(gs, got))]
    landed = yield _chip_exchange(part)
    mine = [_sum_chips(x, f"reduce_sum_chips{tag}_{a}") for a, x in enumerate(landed)]
    return (yield _sibling_join(mine))


class _Staged:
    def __init__(self, gen, tag):
        self.gen, self.tag, self.k, self.value = gen, tag, 0, None
        self.cur = next(gen)

    def stage(self):
        return self.cur

    def done(self, results):
        self.k += 1
        try:
            self.cur = self.gen.send(results)
        except StopIteration as stop:
            self.cur, self.value = None, stop.value

    def drain(self):
        while self.cur is not None:
            self.done(_run_comm(self.cur, f"comm{self.tag}_{self.k}"))
        return self.value


WEIGHTS = ["meta_tokens", "norm_mix_pre", "norm_mix_post", "norm_mlp_pre", "norm_mlp_post", "w_in", "conv_w",
           "conv_b", "lru_wa_f", "lru_ba_f", "lru_wx_f", "lru_bx_f", "lru_lambda_f", "lru_wa_b", "lru_ba_b",
           "lru_wx_b", "lru_bx_b", "lru_lambda_b", "gla_wg_f", "gla_bg_f", "gla_wg_b", "gla_bg_b", "gla_head_norm",
           "w_out", "w_mlp_up", "w_mlp_down"]
BIG = ("w_in", "w_out", "w_mlp_up", "w_mlp_down")
SMALL = [n for n in WEIGHTS if n not in BIG]
SMALL_SHARDED = {"meta_tokens": 1, "conv_w": 2, "gla_wg_f": 2, "gla_wg_b": 2}
N_CHIPS = 4
SMALL_RAW = [("g1", (1, D_MODEL)), ("g2", (1, D_MODEL)), ("g3", (1, D_MODEL)), ("g4", (1, D_MODEL)),
             ("conv_w", (4, LRU_W)), ("conv_b", (1, LRU_W)), ("lru_wg", (LRU_HEADS, LRU_HD, 4 * LRU_HD)),
             ("lru_pb", (SUBLANES, LRU_W)), ("gla_wg", (2 * GLA_RANK, 2 * QK)), ("gla_bg", (1, 2 * QK)),
             ("head_norm", (1, GLA_W))]
META_SHAPE = (N_META, D_MODEL)


def _pad_to(v, n):
    return jnp.pad(v, (0, n - v.shape[0]))


def _round_up(n, m):
    return -(-n // m) * m


def _flat_cat(arrs, total):
    return _pad_to(jnp.concatenate([a.reshape(-1) for a in arrs]), total)


def _split_flat(flat, shapes):
    out, off = [], 0
    for s in shapes:
        n = math.prod(s)
        out.append(flat[off:off + n].reshape(s))
        off += n
    return out


def _my_shard(full, axis, chip):
    n = full.shape[axis] // N_CHIPS
    return lax.dynamic_slice_in_dim(full, chip * n, n, axis=axis)


def _block_diag_gates(wa_f, wx_f, wa_b, wx_b):
    eye2 = jnp.eye(2, dtype=wa_f.dtype)

    def bd(w):
        w4 = w.reshape(4, 2, LRU_HD, LRU_HD)
        return (w4[:, :, :, None, :] * eye2[None, :, None, :, None]).reshape(4, LANES, LANES)
    return jnp.concatenate([bd(wa_f), bd(wx_f), bd(wa_b), bd(wx_b)], axis=2).astype(BF16)


def _gate_diag_blocks(dwh, o):
    return dwh[:, :, o * LRU_HD:(o + 1) * LRU_HD]


def kernel(x, meta_tokens, norm_mix_pre, norm_mix_post, norm_mlp_pre, norm_mlp_post, w_in, conv_w, conv_b, lru_wa_f, lru_ba_f, lru_wx_f, lru_bx_f, lru_lambda_f, lru_wa_b, lru_ba_b, lru_wx_b, lru_bx_b, lru_lambda_b, gla_wg_f, gla_bg_f, gla_wg_b, gla_bg_b, gla_head_norm, w_out, w_mlp_up, w_mlp_down, loss_target, m_meta_tokens, m_norm_mix_pre, m_norm_mix_post, m_norm_mlp_pre, m_norm_mlp_post, m_w_in, m_conv_w, m_conv_b, m_lru_wa_f, m_lru_ba_f, m_lru_wx_f, m_lru_bx_f, m_lru_lambda_f, m_lru_wa_b, m_lru_ba_b, m_lru_wx_b, m_lru_bx_b, m_lru_lambda_b, m_gla_wg_f, m_gla_bg_f, m_gla_wg_b, m_gla_bg_b, m_gla_head_norm, m_w_out, m_w_mlp_up, m_w_mlp_down, v_meta_tokens, v_norm_mix_pre, v_norm_mix_post, v_norm_mlp_pre, v_norm_mlp_post, v_w_in, v_conv_w, v_conv_b, v_lru_wa_f, v_lru_ba_f, v_lru_wx_f, v_lru_bx_f, v_lru_lambda_f, v_lru_wa_b, v_lru_ba_b, v_lru_wx_b, v_lru_bx_b, v_lru_lambda_b, v_gla_wg_f, v_gla_bg_f, v_gla_wg_b, v_gla_bg_b, v_gla_head_norm, v_w_out, v_w_mlp_up, v_w_mlp_down):
    args = dict(locals())
    w_loc = {n: args[n] for n in WEIGHTS}
    m_loc = {n: args["m_" + n] for n in WEIGHTS}
    v_loc = {n: args["v_" + n] for n in WEIGHTS}
    seq = x.shape[1]
    t_rows = FRONT + seq + BACK
    assert t_rows % 768 == 0 and seq % FRONT == 0, seq
    chip = 2 * lax.axis_index("x") + lax.axis_index("y")
    c_idx = lax.axis_index("c").astype(jnp.int32).reshape(1)

    def halves(a):
        return a.reshape((2, a.shape[0] // 2) + a.shape[1:])

    big16 = {n: w_loc[n].astype(BF16) for n in BIG}
    sm_names = list(SMALL_SHARDED)
    sm_len = _round_up(sum(w_loc[n].size for n in sm_names), 2 * SUBLANES * FLAT_W)
    sm_pack = _flat_cat([w_loc[n] for n in sm_names], sm_len).reshape(2, -1, FLAT_W)
    g_in0, g_small = _run_comm(_allgather_chips([halves(big16["w_in"][0]), sm_pack]), "gather_first")
    sm_parts = [_split_flat(g_small[q].reshape(-1), [w_loc[n].shape for n in sm_names]) for q in range(N_CHIPS)]
    full = {n: jnp.concatenate([sm_parts[q][i] for q in range(N_CHIPS)], axis=SMALL_SHARDED[n])
            for i, n in enumerate(sm_names)}
    for n in SMALL:
        if n not in SMALL_SHARDED:
            full[n] = w_loc[n]

    def full_w_in(g):
        w = jnp.transpose(g.reshape(N_CHIPS, D_MODEL, -1), (1, 0, 2)).reshape(D_MODEL, D_IN)
        return jnp.pad(w, ((0, 0), (0, D_IN_PAD - D_IN)))

    def full_rest(g_out, g_up, g_down):
        return dict(w_out=g_out.reshape(-1, D_MODEL),
                    w_up=jnp.transpose(g_up.reshape(N_CHIPS, D_MODEL, -1), (1, 0, 2)).reshape(D_MODEL, D_FF),
                    w_down=g_down.reshape(D_FF, D_MODEL))

    later_gathers = [_allgather_chips([halves(big16[n][0]) for n in BIG[1:]]),
                     _allgather_chips([halves(big16[n][l]) for l in range(1, DEPTH) for n in BIG])]

    meta_blk = jnp.concatenate([jnp.zeros((FRONT - N_META, D_MODEL), F32), full["meta_tokens"]], axis=0)
    h = jnp.concatenate([meta_blk, x[0], jnp.zeros((BACK, D_MODEL), F32)], axis=0)
    target = loss_target[0]

    layer_w = []
    for l in range(DEPTH):
        wg = jnp.zeros((LANES, 2 * QK), F32)
        wg = wg.at[0:GLA_RANK, 0:QK].set(full["gla_wg_f"][l]).at[GLA_RANK:2 * GLA_RANK, QK:].set(full["gla_wg_b"][l])
        pb = jnp.stack([full["lru_ba_f"][l], full["lru_bx_f"][l], full["lru_ba_b"][l], full["lru_bx_b"][l],
                        full["lru_lambda_f"][l], full["lru_lambda_b"][l], jnp.zeros((LRU_W,), F32),
                        jnp.zeros((LRU_W,), F32)])
        layer_w.append(dict(
            g1=full["norm_mix_pre"][l][None], g2=full["norm_mix_post"][l][None],
            g3=full["norm_mlp_pre"][l][None], g4=full["norm_mlp_post"][l][None],
            conv_w=full["conv_w"][l], conv_b=full["conv_b"][l][None],
            lru_wg=_block_diag_gates(full["lru_wa_f"][l], full["lru_wx_f"][l], full["lru_wa_b"][l], full["lru_wx_b"][l]),
            lru_pb=pb, gla_wg=wg.astype(BF16),
            gla_bg=jnp.concatenate([full["gla_bg_f"][l], full["gla_bg_b"][l]])[None],
            head_norm=full["gla_head_norm"][l][None]))
    layer_w[0]["w_in"] = full_w_in(g_in0)

    def split_z(acc):
        return (acc[:, 0:1024], acc[:, 1024:1536], acc[:, 1536:2048], acc[:, 2048:2560], acc[:, ZG_OFF:ZG_OFF + LANES])

    def relu2(acc):
        r = jnp.maximum(acc, 0.0)
        return acc, r * r

    tm_e = 768
    tm_ff = 2816 if t_rows % 2816 == 0 else 768
    tk_dw = 2816 if t_rows % 2816 == 0 else 768

    def post_mix(acc, hh, g2, g3):
        h1 = hh + _rms_fwd(acc, g2)
        return acc, h1, _rms_fwd(h1, g3)

    def post_mlp(acc, h1, g4, g1_next):
        h2 = h1 + _rms_fwd(acc, g4)
        return acc, h2, _rms_fwd(h2, g1_next)

    def post_mlp_loss(acc, h1, tgt, x_rows, g4):
        h2 = h1 + _rms_fwd(acc, g4)
        err = (h2 - tgt) * x_rows[:, 0:1]
        d_h = err * (1.0 / D_MODEL)
        dff, dg4 = _rms_bwd(acc, g4, d_h)
        loss_part = jnp.zeros((SUBLANES, LANES), F32) + jnp.sum(err * err) * (0.5 / D_MODEL)
        return d_h, dff, loss_part, dg4

    row3 = [(D_MODEL, F32), (D_MODEL, F32), (D_MODEL, BF16)]
    vec = ((1, D_MODEL), F32)
    target_p = jnp.concatenate([jnp.zeros((FRONT, D_MODEL), F32), target, jnp.zeros((BACK, D_MODEL), F32)], axis=0)
    x_rows = jnp.concatenate([jnp.zeros((FRONT, LANES), F32), jnp.ones((seq, LANES), F32), jnp.zeros((BACK, LANES), F32)])
    saved = []
    n1 = _norm_cast(h, layer_w[0]["g1"], t_rows, "norm_mix_pre_l0")
    for l in range(DEPTH):
        lw = layer_w[l]
        tag = f"_l{l}"
        z_lru, z_qk, z_v, z_go, z_zg = _mm(
            n1, lw["w_in"], "nn", 768, D_IN_PAD, D_MODEL,
            [(1024, F32), (512, F32), (512, F32), (512, F32), (LANES, F32)], "in_proj" + tag, epilogue=split_z)
        lru_args = (z_lru, lw["conv_w"], lw["conv_b"], lw["lru_wg"], lw["lru_pb"], t_rows, seq, "lru_fwd" + tag)
        gla_args = (z_qk, z_v, z_zg, lw["gla_wg"], lw["gla_bg"], t_rows, "gla_fwd" + tag)
        if l == 0:
            (h_f, h_b), g_rest = _lru_fwd(*lru_args, comm=later_gathers[0])
            lw.update(full_rest(*g_rest))
            (o_f, o_b, s_f, s_b), g_next = _gla_fwd(*gla_args, comm=later_gathers[1])
            for l2 in range(1, DEPTH):
                g_l2 = g_next[(l2 - 1) * len(BIG):l2 * len(BIG)]
                layer_w[l2].update(full_rest(*g_l2[1:]), w_in=full_w_in(g_l2[0]))
        else:
            h_f, h_b = _lru_fwd(*lru_args)
            o_f, o_b, s_f, s_b = _gla_fwd(*gla_args)
        ymix = _mix_fwd(h_f, h_b, z_lru, o_f, o_b, z_go, lw["head_norm"], t_rows, "mix_fwd" + tag)
        mix, h1, n3 = _mm(ymix, lw["w_out"], "nn", 768, D_MODEL, D_MODEL, row3, "out_proj" + tag, epilogue=post_mix,
                          rows_in=(h,), consts=(lw["g2"], lw["g3"]))
        u, act = _mm(n3, lw["w_up"], "nn", tm_ff, 1024, D_MODEL, [(D_FF, BF16), (D_FF, BF16)], "mlp_up" + tag,
                     epilogue=relu2)
        sv = dict(h=h, n1=n1, z_lru=z_lru, z_qk=z_qk, z_v=z_v, z_go=z_go, z_zg=z_zg, h_f=h_f, h_b=h_b,
                  o_f=o_f, o_b=o_b, s_f=s_f, s_b=s_b, ymix=ymix, mix=mix, h1=h1, n3=n3, u=u, act=act)
        if l + 1 < DEPTH:
            sv["ff"], h, n1 = _mm(act, lw["w_down"], "nn", tm_e, D_MODEL, D_FF, row3, "mlp_down" + tag,
                                  epilogue=post_mlp, rows_in=(h1,), consts=(lw["g4"], layer_w[l + 1]["g1"]))
        else:
            dh, dff, loss_part, dg4 = _mm(act, lw["w_down"], "nn", tm_e, D_MODEL, D_FF,
                                          [(D_MODEL, F32), (D_MODEL, BF16)], "mlp_down_loss" + tag,
                                          epilogue=post_mlp_loss, rows_in=(h1, target_p, x_rows), consts=(lw["g4"],),
                                          accs=[((SUBLANES, LANES), F32), vec])
        saved.append(sv)

    loss = lax.psum(loss_part[0, 0], ("x", "y", "c"))

    small_len = _round_up(sum(math.prod(s) for _, s in SMALL_RAW) + math.prod(META_SHAPE),
                          N_CHIPS * 2 * 2 * SUBLANES * FLAT_W)
    totals = [None] * DEPTH
    pend = None

    def with_stage(staged, fn):
        comm = staged.stage() if staged is not None else None
        if comm is None:
            return fn(None)
        res, comm_res = fn(comm)
        staged.done(comm_res)
        return res

    for l in reversed(range(DEPTH)):
        lw, sv = layer_w[l], saved[l]
        tag = f"_l{l}"
        raw = {"g4": dg4}
        gw_down = with_stage(pend, lambda comm: _mm(sv["act"], dff, "tn", 1024, D_MODEL, tk_dw, [(D_MODEL, F32)],
                                                    "dw_down" + tag, comm=comm))[0]
        du = _mm(dff, lw["w_down"], "nt", tm_ff, 1024, D_MODEL, [(D_FF, BF16)], "d_act" + tag,
                 epilogue=lambda acc, uu: (acc * 2.0 * jnp.maximum(uu.astype(F32), 0.0),), extras=(sv["u"],))[0]
        gw_up = _mm(sv["n3"], du, "tn", D_MODEL, D_FF // N_CHIPS, tk_dw, [(D_FF, F32)], "dw_up" + tag,
                    col_blocks_first=True)[0]

        def pre_mlp_bwd(acc, h1, d_h, mix, g3, g2):
            dx, dg3 = _rms_bwd(h1, g3, acc)
            d_h1 = d_h + dx
            d_mix, dg2 = _rms_bwd(mix, g2, d_h1)
            return d_h1, d_mix, dg3, dg2

        dh1, dmix, raw["g3"], raw["g2"] = _mm(
            du, lw["w_up"], "nt", tm_e, D_MODEL, D_FF, [(D_MODEL, F32), (D_MODEL, BF16)], "d_n3" + tag,
            epilogue=pre_mlp_bwd, rows_in=(sv["h1"], dh, sv["mix"]), consts=(lw["g3"], lw["g2"]), accs=[vec, vec])
        gw_out = _mm(sv["ymix"], dmix, "tn", D_MODEL, D_MODEL, tk_dw, [(D_MODEL, F32)], "dw_out" + tag)[0]
        mats = [gw_out.reshape(N_CHIPS, 2, -1, D_MODEL), gw_up.reshape(N_CHIPS, 2, D_MODEL // 2, D_FF // N_CHIPS),
                gw_down.reshape(N_CHIPS, 2, -1, D_MODEL)]
        early = _Staged(_reduce_stages(mats, c_idx, tag + "e"), tag + "e") if l == 0 else None
        half4 = [(LRU_W, F32), (LRU_W, F32), (GLA_W, F32), (GLA_W, F32)]
        dh_lru, dgate, do, dgo, raw["head_norm"] = with_stage(early, lambda comm: _mm(
            dmix, lw["w_out"], "nt", 768, D_MODEL, D_MODEL, half4, "d_ymix" + tag, epilogue=_mix_bwd_tile,
            rows_in=(sv["h_f"], sv["h_b"], sv["z_lru"], sv["o_f"], sv["o_b"], sv["z_go"]), consts=(lw["head_norm"],),
            accs=[((1, GLA_W), F32)], comm=comm))
        gla_grads = with_stage(pend, lambda comm: _gla_bwd(sv["z_qk"], sv["z_v"], sv["z_zg"], lw["gla_wg"],
                                                           lw["gla_bg"], do, sv["s_f"], sv["s_b"], t_rows,
                                                           "gla_bwd" + tag, comm=comm))
        dx_br, raw["conv_w"], raw["conv_b"], raw["lru_wg"], raw["lru_pb"] = with_stage(early, lambda comm: _lru_bwd(
            sv["z_lru"], lw["conv_w"], lw["conv_b"], lw["lru_wg"], lw["lru_pb"], sv["h_f"], sv["h_b"], dh_lru,
            t_rows, seq, "lru_bwd" + tag, comm=comm))
        dz, dwg_gla, raw["gla_bg"] = _dz_assemble(dx_br, dgate, gla_grads, dgo, sv["z_zg"], lw["gla_wg"], t_rows,
                                                  "dz_assemble" + tag)
        raw["gla_wg"] = dwg_gla[0:2 * GLA_RANK]
        gw_in = with_stage(pend, lambda comm: _mm(sv["n1"], dz, "tn", D_MODEL, 896, tk_dw, [(D_IN_PAD, F32)],
                                                  "dw_in" + tag, comm=comm))[0]
        if l > 0:
            def pre_mix_bwd(acc, hh, d_h1, ff_prev, g1, g4_prev):
                dx, dg1 = _rms_bwd(hh, g1, acc)
                d_h = d_h1 + dx
                d_ff, dg4_prev = _rms_bwd(ff_prev, g4_prev, d_h)
                return d_h, d_ff, dg1, dg4_prev

            dh, dff, raw["g1"], dg4 = _mm(
                dz, lw["w_in"], "nt", tm_e, D_MODEL, D_IN_PAD, [(D_MODEL, F32), (D_MODEL, BF16)], "d_n1" + tag,
                epilogue=pre_mix_bwd, rows_in=(sv["h"], dh1, saved[l - 1]["ff"]),
                consts=(lw["g1"], layer_w[l - 1]["g4"]), accs=[vec, vec])
        else:
            def pre_mix_bwd0(acc, hh, d_h1, g1):
                dx, dg1 = _rms_bwd(hh, g1, acc)
                return d_h1 + dx, dg1

            dh, raw["g1"] = with_stage(early, lambda comm: _mm(
                dz, lw["w_in"], "nt", tm_e, D_MODEL, D_IN_PAD, [(D_MODEL, F32)], "d_n1" + tag, epilogue=pre_mix_bwd0,
                rows_in=(sv["h"], dh1), consts=(lw["g1"],), accs=[vec], comm=comm))

        cols = D_IN // N_CHIPS
        in_sh = jnp.stack([gw_in[:, q * cols:(q + 1) * cols] for q in range(N_CHIPS)])
        meta_g = dh[FRONT - N_META:FRONT] if l == 0 else jnp.zeros(META_SHAPE, F32)
        small = _flat_cat([raw[n] for n, _ in SMALL_RAW] + [meta_g], small_len)
        rest = [in_sh.reshape(N_CHIPS, 2, D_MODEL // 2, cols), small.reshape(N_CHIPS, 2, -1, FLAT_W)]
        if pend is not None:
            totals[l + 1] = pend.drain()
        if early is None:
            pend = _Staged(_reduce_stages([rest[0]] + mats + [rest[1]], c_idx, tag), tag)
        else:
            totals[l] = [None] + list(early.drain()) + [None]
            late = _Staged(_reduce_stages(rest, c_idx, tag), tag)

    grad_x = dh[FRONT:FRONT + seq][None]

    delta, new_m, new_v, g_tot = {}, {}, {}, {}
    totals[0][0], totals[0][4] = late.drain()

    def adamw_matrix(a, n):
        shp = w_loc[n].shape
        flat3 = lambda t: t.reshape(DEPTH, -1, shp[-1])
        g_l = [totals[l][a].reshape(-1, shp[-1]) for l in range(DEPTH)]
        outs = _adamw_layers(flat3(w_loc[n]), flat3(m_loc[n]), flat3(v_loc[n]), g_l, "adamw_" + n)
        g_tot[n], delta[n], new_m[n], new_v[n] = [o.reshape(shp) for o in outs]

    sm_all = _run_comm(_allgather_chips([totals[l][4] for l in range(DEPTH)]), "gather_small_grads")
    per_layer = []
    for l in range(DEPTH):
        pieces = _split_flat(sm_all[l].reshape(-1), [s for _, s in SMALL_RAW] + [META_SHAPE])
        per_layer.append(dict(zip([n for n, _ in SMALL_RAW] + ["meta"], pieces)))
    stack = lambda f: jnp.stack([f(per_layer[l]) for l in range(DEPTH)])
    g_tot["meta_tokens"] = _my_shard(per_layer[0]["meta"], 1, chip)
    for n, key in (("norm_mix_pre", "g1"), ("norm_mix_post", "g2"), ("norm_mlp_pre", "g3"), ("norm_mlp_post", "g4"),
                   ("conv_b", "conv_b"), ("gla_head_norm", "head_norm")):
        g_tot[n] = stack(lambda d, key=key: d[key][0])
    g_tot["conv_w"] = _my_shard(stack(lambda d: d["conv_w"]), 2, chip)
    for o, n in enumerate(("lru_wa_f", "lru_wx_f", "lru_wa_b", "lru_wx_b")):
        g_tot[n] = stack(lambda d, o=o: _gate_diag_blocks(d["lru_wg"], o))
    for row, n in enumerate(("lru_ba_f", "lru_bx_f", "lru_ba_b", "lru_bx_b", "lru_lambda_f", "lru_lambda_b")):
        g_tot[n] = stack(lambda d, row=row: d["lru_pb"][row])
    g_tot["gla_wg_f"] = _my_shard(stack(lambda d: d["gla_wg"][0:GLA_RANK, 0:QK]), 2, chip)
    g_tot["gla_wg_b"] = _my_shard(stack(lambda d: d["gla_wg"][GLA_RANK:, QK:]), 2, chip)
    g_tot["gla_bg_f"] = stack(lambda d: d["gla_bg"][0, 0:QK])
    g_tot["gla_bg_b"] = stack(lambda d: d["gla_bg"][0, QK:])

    for a, n in enumerate(BIG):
        adamw_matrix(a, n)
    outs = _adamw_small(*[[d[n] for n in SMALL] for d in (w_loc, g_tot, m_loc, v_loc)])
    for d, o in zip((delta, new_m, new_v), outs):
        d.update(zip(SMALL, o))
    return (loss, grad_x, *[g_tot[n] for n in WEIGHTS], *[delta[n] for n in WEIGHTS],
            *[new_m[n] for n in WEIGHTS], *[new_v[n] for n in WEIGHTS])
```
